```python
import math
import jax, jax.numpy as jnp
from jax import lax
import numpy as np

D_MODEL = 1024
BATCH = 2
SEQ = 8192
DEPTH = 4
DEC_BATCH = 4
DEC_SEQ = 4096
PAST_LEN = 128

N_MIXERS = 2
N_ATTN_LAYERS = (DEPTH + N_MIXERS - 1) // N_MIXERS
N_SSM_LAYERS = DEPTH // N_MIXERS
ATTN_HEADS = 16
HEAD_DIM = 64
ATTN_WIDTH = ATTN_HEADS * HEAD_DIM
DILATED_PAIRS = ((128, 1), (512, 4), (2048, 16))
N_DIL = len(DILATED_PAIRS)
ATTN_IN = (3 * N_DIL + 1) * ATTN_WIDTH
ROPE_THETA = 10000.0
SSM_WIDTH = D_MODEL
GROUP_CH = 16
SSM_GROUPS = SSM_WIDTH // GROUP_CH
STATE = 64
DT_MIN = 0.001
DT_MAX = 0.1
LAMBDA_RE_MAX = -1e-4
NORM_EPS = 1e-6
NEG_INF = -1e30

kernel_name = "dilated_attn_s5_interleaved_adaln_encoder"


def rms_norm(x, g):
    x32 = x.astype(jnp.float32)
    y = x32 * lax.rsqrt(jnp.mean(x32 * x32, axis=-1, keepdims=True) + NORM_EPS)
    return (y * g.astype(jnp.float32)).astype(x.dtype)


def rope_tables(s):
    inv_freq = ROPE_THETA ** (-jnp.arange(0, HEAD_DIM, 2, dtype=jnp.float32) / HEAD_DIM)
    ang = jnp.arange(s, dtype=jnp.float32)[:, None] * inv_freq[None, :]
    return jnp.cos(ang)[:, None, :], jnp.sin(ang)[:, None, :]


def apply_rope(t, cos, sin):
    t32 = t.astype(jnp.float32)
    t1, t2 = jnp.split(t32, 2, axis=-1)
    out = jnp.concatenate([t1 * cos - t2 * sin, t2 * cos + t1 * sin], axis=-1)
    return out.astype(t.dtype)


def dilated_window_attention(q, k, v, dil, radius):
    bsz, s, h, e = q.shape
    blk = radius
    m = s // dil
    nb = -(-m // blk)
    mp = nb * blk

    def to_sub(t):
        return t.reshape(bsz, m, dil, h, e).transpose(0, 2, 3, 1, 4)

    qs = jnp.pad(to_sub(q), ((0, 0), (0, 0), (0, 0), (0, mp - m), (0, 0)))
    qs = qs.reshape(bsz, dil, h, nb, blk, e)

    def neighbours(t):
        tp = jnp.pad(to_sub(t), ((0, 0), (0, 0), (0, 0), (blk, mp - m + blk), (0, 0)))
        tp = tp.reshape(bsz, dil, h, nb + 2, blk, e)
        return jnp.concatenate([tp[:, :, :, :-2], tp[:, :, :, 1:-1], tp[:, :, :, 2:]], axis=4)

    kw = neighbours(k)
    vw = neighbours(v)
    qi = jnp.arange(nb)[:, None, None] * blk + jnp.arange(blk)[None, :, None]
    kj = jnp.arange(nb)[:, None, None] * blk - blk + jnp.arange(3 * blk)[None, None, :]
    valid = (jnp.abs(kj - qi) <= radius) & (kj >= 0) & (kj < m)

    scores = jnp.einsum('bdhnqe,bdhnke->bdhnqk', qs, kw).astype(jnp.float32)
    scores = jnp.where(valid, scores, NEG_INF)
    lse = jax.nn.logsumexp(scores, axis=-1)
    probs = jnp.exp(scores - lse[..., None]).astype(v.dtype)
    o = jnp.einsum('bdhnqk,bdhnke->bdhnqe', probs, vw)
    o = o.reshape(bsz, dil, h, mp, e)[:, :, :, :m].transpose(0, 3, 1, 2, 4).reshape(bsz, s, h, e)
    lse = lse.reshape(bsz, dil, h, mp)[:, :, :, :m].transpose(0, 3, 1, 2).reshape(bsz, s, h)
    return o, lse


def dilated_mixer(h, w_in, w_out):
    bsz, s, _ = h.shape
    proj = h @ w_in
    z = proj[..., 3 * N_DIL * ATTN_WIDTH:]
    cos, sin = rope_tables(s)
    scale = HEAD_DIM ** -0.5
    outs, lses = [], []
    for g, (window, dil) in enumerate(DILATED_PAIRS):
        qkv = proj[..., g * 3 * ATTN_WIDTH:(g + 1) * 3 * ATTN_WIDTH].reshape(bsz, s, 3, ATTN_HEADS, HEAD_DIM)
        q = apply_rope(qkv[:, :, 0], cos, sin) * scale
        k = apply_rope(qkv[:, :, 1], cos, sin)
        v = qkv[:, :, 2]
        o, lse = dilated_window_attention(q, k, v, dil, window // (2 * dil))
        outs.append(o)
        lses.append(lse)
    weights = jax.nn.softmax(jnp.stack(lses, axis=0), axis=0)
    o = jnp.einsum('gbsh,gbshe->bshe', weights.astype(h.dtype), jnp.stack(outs, axis=0))
    y = o.reshape(bsz, s, ATTN_WIDTH) * jax.nn.silu(z)
    return y @ w_out


def _linear_recurrence(e1, e2):
    a1, b1 = e1
    a2, b2 = e2
    return a1 * a2, a2 * b1 + b2


def s5_mixer(h, w_in, lam_re, lam_im, log_dt, b_re, b_im, c_re, c_im, d_skip, w_glu, w_out):
    bsz, s, _ = h.shape
    u, z = jnp.split(h @ w_in, 2, axis=-1)
    u32 = u.astype(jnp.float32)
    ug = u32.reshape(bsz, s, SSM_GROUPS, GROUP_CH)
    y = (d_skip.astype(jnp.float32) * u32).reshape(bsz, s, SSM_GROUPS, GROUP_CH)
    ugc = ug.astype(jnp.complex64)
    for direction in range(2):
        lam = lax.complex(jnp.minimum(lam_re[direction].astype(jnp.float32), LAMBDA_RE_MAX),
                          lam_im[direction].astype(jnp.float32))
        dt = jnp.exp(log_dt[direction].astype(jnp.float32))[:, None]
        lam_bar = jnp.exp(lam * dt)
        b_mat = lax.complex(b_re[direction].astype(jnp.float32), b_im[direction].astype(jnp.float32))
        b_bar = ((lam_bar - 1.0) / lam)[..., None] * b_mat
        bu = jnp.einsum('bsgc,gpc->bsgp', ugc, b_bar)
        a = jnp.broadcast_to(lam_bar, bu.shape)
        _, states = lax.associative_scan(_linear_recurrence, (a, bu), axis=1,
                                         reverse=(direction == 1))
        c_mat = lax.complex(c_re[direction].astype(jnp.float32), c_im[direction].astype(jnp.float32))
        y = y + jnp.real(jnp.einsum('bsgp,gcp->bsgc', states, c_mat))
    y = y.reshape(bsz, s, SSM_WIDTH).astype(h.dtype)
    g = jax.nn.gelu(y)
    y = g * jax.nn.sigmoid(g @ w_glu)
    return (y * jax.nn.silu(z)) @ w_out


def trunk(x, c, norm_g, ada_w, ada_b, attn_w_in, attn_w_out, ssm_w_in, ssm_lam_re, ssm_lam_im,
          ssm_log_dt, ssm_b_re, ssm_b_im, ssm_c_re, ssm_c_im, ssm_d, ssm_w_glu, ssm_w_out, final_norm_g):
    for i in range(DEPTH):
        ada = jax.nn.silu(c) @ ada_w[i] + ada_b[i]
        shift, scale, gate = jnp.split(ada[:, None, :], 3, axis=-1)
        hmod = rms_norm(x, norm_g[i]) * (1.0 + scale) + shift
        j = i // N_MIXERS
        if i % N_MIXERS == 0:
            out = dilated_mixer(hmod, attn_w_in[j], attn_w_out[j])
        else:
            out = s5_mixer(hmod, ssm_w_in[j], ssm_lam_re[j], ssm_lam_im[j], ssm_log_dt[j],
                           ssm_b_re[j], ssm_b_im[j], ssm_c_re[j], ssm_c_im[j], ssm_d[j],
                           ssm_w_glu[j], ssm_w_out[j])
        x = x + gate * out
    return rms_norm(x, final_norm_g)


def setup_inputs(seed: int = 0) -> dict:
    key = jax.random.key(seed)
    ks = jax.random.split(key, 24)
    f32 = jnp.float32

    def nrm(k, shape, s):
        return jax.random.normal(k, shape, f32) * s

    nA, nB = N_ATTN_LAYERS, N_SSM_LAYERS
    G, P, GC = SSM_GROUPS, STATE, GROUP_CH
    lam_im_base = jnp.broadcast_to(jnp.pi * jnp.arange(P, dtype=f32), (nB, 2, G, P))
    return {
        "x_prompt": nrm(ks[0], (BATCH, SEQ, D_MODEL), 1.0),
        "x_sample": nrm(ks[1], (DEC_BATCH, DEC_SEQ, D_MODEL), 1.0),
        "c_prompt": nrm(ks[2], (BATCH, D_MODEL), 1.0),
        "c_sample": nrm(ks[3], (DEC_BATCH, D_MODEL), 1.0),
        "norm_g": 1.0 + nrm(ks[4], (DEPTH, D_MODEL), 0.02),
        "ada_w": nrm(ks[5], (DEPTH, D_MODEL, 3 * D_MODEL), 0.5 * D_MODEL ** -0.5),
        "ada_b": nrm(ks[6], (DEPTH, 3 * D_MODEL), 0.02),
        "attn_w_in": nrm(ks[7], (nA, D_MODEL, ATTN_IN), D_MODEL ** -0.5),
        "attn_w_out": nrm(ks[8], (nA, ATTN_WIDTH, D_MODEL), ATTN_WIDTH ** -0.5),
        "ssm_w_in": nrm(ks[9], (nB, D_MODEL, 2 * SSM_WIDTH), D_MODEL ** -0.5),
        "ssm_lam_re": -0.5 + nrm(ks[10], (nB, 2, G, P), 0.01),
        "ssm_lam_im": lam_im_base + nrm(ks[11], (nB, 2, G, P), 0.01),
        "ssm_log_dt": jax.random.uniform(ks[12], (nB, 2, G), f32, math.log(DT_MIN), math.log(DT_MAX)),
        "ssm_b_re": nrm(ks[13], (nB, 2, G, P, GC), (2 * GC) ** -0.5),
        "ssm_b_im": nrm(ks[14], (nB, 2, G, P, GC), (2 * GC) ** -0.5),
        "ssm_c_re": nrm(ks[15], (nB, 2, G, GC, P), P ** -0.5),
        "ssm_c_im": nrm(ks[16], (nB, 2, G, GC, P), P ** -0.5),
        "ssm_d": nrm(ks[17], (nB, SSM_WIDTH), 0.5),
        "ssm_w_glu": nrm(ks[18], (nB, SSM_WIDTH, SSM_WIDTH), SSM_WIDTH ** -0.5),
        "ssm_w_out": nrm(ks[19], (nB, SSM_WIDTH, D_MODEL), SSM_WIDTH ** -0.5),
        "final_norm_g": 1.0 + nrm(ks[20], (D_MODEL,), 0.02),
    }


def reference(x_prompt, x_sample, c_prompt, c_sample, norm_g, ada_w, ada_b, attn_w_in, attn_w_out,
              ssm_w_in, ssm_lam_re, ssm_lam_im, ssm_log_dt, ssm_b_re, ssm_b_im, ssm_c_re, ssm_c_im,
              ssm_d, ssm_w_glu, ssm_w_out, final_norm_g):
    y_prompt = trunk(x_prompt, c_prompt, norm_g, ada_w, ada_b, attn_w_in, attn_w_out, ssm_w_in,
                     ssm_lam_re, ssm_lam_im, ssm_log_dt, ssm_b_re, ssm_b_im, ssm_c_re, ssm_c_im,
                     ssm_d, ssm_w_glu, ssm_w_out, final_norm_g)
    y_sample = trunk(x_sample, c_sample, norm_g, ada_w, ada_b, attn_w_in, attn_w_out, ssm_w_in,
                     ssm_lam_re, ssm_lam_im, ssm_log_dt, ssm_b_re, ssm_b_im, ssm_c_re, ssm_c_im,
                     ssm_d, ssm_w_glu, ssm_w_out, final_norm_g)
    return (y_prompt, y_sample)
```

```python
import functools
import math

import numpy as np
import jax
import jax.numpy as jnp
from jax import lax
from jax.experimental import pallas as pl
from jax.experimental.pallas import tpu as pltpu

D_MODEL = 1024
DEPTH = 4
N_MIXERS = 2
ATTN_HEADS = 16
HEAD_DIM = 64
ATTN_WIDTH = ATTN_HEADS * HEAD_DIM
DILATED_PAIRS = ((128, 1), (512, 4), (2048, 16))
N_DIL = len(DILATED_PAIRS)
ROPE_THETA = 10000.0
SSM_WIDTH = D_MODEL
GROUP_CH = 16
SSM_GROUPS = SSM_WIDTH // GROUP_CH
STATE = 64
LAMBDA_RE_MAX = -1e-4
NORM_EPS = 1e-6
NEG_INF = -1e30

LANES = 128
CHUNK = 64
CHUNK_ROWS = CHUNK * GROUP_CH
T_PER_VREG = LANES // GROUP_CH
V_LANES = 2 * CHUNK_ROWS + LANES
ATTN_TQ = 128
VMEM_LIMIT = 48 * 1024 * 1024

F32 = jnp.float32
BF16 = jnp.bfloat16
HI = lax.Precision.HIGHEST


def _silu(v):
    return v * (1.0 / (1.0 + jnp.exp(-v)))


def _rms_modulate(x, g, scale, shift):
    rs = lax.rsqrt(jnp.mean(x * x, axis=-1, keepdims=True) + NORM_EPS)
    return (x * rs * g) * (1.0 + scale) + shift


def _ada_kernel(c_ref, w_ref, b_ref, o_ref):
    o_ref[...] = jnp.dot(_silu(c_ref[...]), w_ref[...], precision=HI,
                         preferred_element_type=F32) + b_ref[...]


def _ada(c, ada_w, ada_b):
    nb = c.shape[0]
    return pl.pallas_call(
        _ada_kernel,
        out_shape=jax.ShapeDtypeStruct((DEPTH, nb, 3 * D_MODEL), F32),
        grid=(DEPTH, 3),
        in_specs=[
            pl.BlockSpec((nb, D_MODEL), lambda i, j: (0, 0)),
            pl.BlockSpec((None, D_MODEL, D_MODEL), lambda i, j: (i, 0, j)),
            pl.BlockSpec((None, 1, D_MODEL), lambda i, j: (i, 0, j)),
        ],
        out_specs=pl.BlockSpec((None, nb, D_MODEL), lambda i, j: (i, 0, j)),
        name="ada",
    )(c, ada_w, ada_b.reshape(DEPTH, 1, 3 * D_MODEL))


def _attn_in_kernel(x_ref, shift_ref, scale_ref, g_ref, w_ref, cos_ref, sin_ref,
                    qkv_ref, z_ref, h_scr):
    j = pl.program_id(2)

    @pl.when(j == 0)
    def _():
        h_scr[...] = _rms_modulate(x_ref[...], g_ref[...], scale_ref[...],
                                   shift_ref[...]).astype(BF16)

    acc = jnp.dot(h_scr[...], w_ref[...], preferred_element_type=F32)
    is_qkv = j < 3 * N_DIL
    part = j % 3

    @pl.when(jnp.logical_and(is_qkv, part == 2))
    def _():
        qkv_ref[...] = acc.astype(BF16)

    @pl.when(jnp.logical_and(is_qkv, part != 2))
    def _():
        qscale = jnp.where(part == 0, HEAD_DIM ** -0.5, 1.0).astype(F32)
        cos = cos_ref[...] * qscale
        sin = sin_ref[...] * qscale
        for b in range(ATTN_WIDTH // LANES):
            sl = slice(b * LANES, (b + 1) * LANES)
            t = acc[:, sl]
            qkv_ref[:, sl] = (t * cos + pltpu.roll(t, LANES // 2, 1) * sin).astype(BF16)

    @pl.when(j == 3 * N_DIL)
    def _():
        z_ref[...] = acc


def _attn_in(x, mod, g, w, cos, sin, tm=512):
    bsz, s, _ = x.shape
    n_col = 3 * N_DIL + 1
    return pl.pallas_call(
        _attn_in_kernel,
        out_shape=(jax.ShapeDtypeStruct((bsz, s, 3 * N_DIL * ATTN_WIDTH), BF16),
                   jax.ShapeDtypeStruct((bsz, s, ATTN_WIDTH), F32)),
        grid=(bsz, s // tm, n_col),
        in_specs=[
            pl.BlockSpec((None, tm, D_MODEL), lambda b, i, j: (b, i, 0)),
            pl.BlockSpec((None, 1, D_MODEL), lambda b, i, j: (b, 0, 0)),
            pl.BlockSpec((None, 1, D_MODEL), lambda b, i, j: (b, 0, 1)),
            pl.BlockSpec((1, D_MODEL), lambda b, i, j: (0, 0)),
            pl.BlockSpec((D_MODEL, ATTN_WIDTH), lambda b, i, j: (0, j)),
            pl.BlockSpec((tm, LANES), lambda b, i, j: (i, 0)),
            pl.BlockSpec((tm, LANES), lambda b, i, j: (i, 0)),
        ],
        out_specs=(
            pl.BlockSpec((None, tm, ATTN_WIDTH),
                         lambda b, i, j: (b, i, jnp.minimum(j, 3 * N_DIL - 1))),
            pl.BlockSpec((None, tm, ATTN_WIDTH), lambda b, i, j: (b, i, 0)),
        ),
        scratch_shapes=[pltpu.VMEM((tm, D_MODEL), BF16)],
        compiler_params=pltpu.CompilerParams(
            dimension_semantics=("parallel", "parallel", "arbitrary"),
            vmem_limit_bytes=VMEM_LIMIT),
        name="attn_in",
    )(x, mod, mod, g, w, cos, sin)


def _attn_kernel(q_ref, kp_ref, kc_ref, kn_ref, vp_ref, vc_ref, vn_ref,
                 o_ref, lse_ref, *, radius, m):
    i = pl.program_id(2)
    tq = q_ref.shape[0]
    nk = tq + 2 * radius
    row = lax.broadcasted_iota(jnp.int32, (tq, nk), 0)
    col = lax.broadcasted_iota(jnp.int32, (tq, nk), 1)
    rel = col - radius - row
    kabs = i * tq - radius + col
    valid = (jnp.abs(rel) <= radius) & (kabs >= 0) & (kabs < m)
    lane = lax.broadcasted_iota(jnp.int32, (tq, LANES), 1)
    q_is_odd = (lane // (HEAD_DIM // 2)) % 2 == 1
    v_is_odd = lane >= HEAD_DIM
    for b in range(ATTN_WIDTH // LANES):
        sl = slice(b * LANES, (b + 1) * LANES)
        qb = q_ref[:, sl].astype(F32)
        kb = jnp.concatenate([kp_ref[:, sl], kc_ref[:, sl], kn_ref[:, sl]], axis=0)
        vb = jnp.concatenate([vp_ref[:, sl], vc_ref[:, sl], vn_ref[:, sl]], axis=0)
        outs = []
        for odd in (False, True):
            qm = (jnp.where(q_is_odd, qb, 0.0) if odd else jnp.where(q_is_odd, 0.0, qb)).astype(BF16)
            sc = lax.dot_general(qm, kb, (((1,), (1,)), ((), ())),
                                 preferred_element_type=F32)
            sc = jnp.where(valid, sc, NEG_INF)
            mx = jnp.max(sc, axis=1, keepdims=True)
            p = jnp.exp(sc - mx)
            den = jnp.sum(p, axis=1, keepdims=True)
            o = jnp.dot(p.astype(BF16), vb, preferred_element_type=F32)
            outs.append((o * (1.0 / den), mx + jnp.log(den)))
        o_ref[:, sl] = jnp.where(v_is_odd, outs[1][0], outs[0][0])
        lse_ref[:, sl] = jnp.where(v_is_odd, outs[1][1], outs[0][1])


def _attn_group(qkv, group):
    bsz, s, width = qkv.shape
    window, dil = DILATED_PAIRS[group]
    radius = window // (2 * dil)
    m = s // dil
    tq = ATTN_TQ
    per_tq = tq // radius
    n_rblk = m // radius
    qkv_v = qkv.reshape(bsz, m, dil * width)
    blocks_per_token = width // ATTN_WIDTH
    base = 3 * group

    def col(r, part):
        return r * blocks_per_token + base + part

    def cur(part):
        return pl.BlockSpec((None, tq, ATTN_WIDTH), lambda b, r, i: (b, i, col(r, part)))

    def prev(part):
        return pl.BlockSpec((None, radius, ATTN_WIDTH),
                            lambda b, r, i: (b, jnp.maximum(i * per_tq - 1, 0), col(r, part)))

    def nxt(part):
        return pl.BlockSpec((None, radius, ATTN_WIDTH),
                            lambda b, r, i: (b, jnp.minimum((i + 1) * per_tq, n_rblk - 1),
                                             col(r, part)))

    out_spec = pl.BlockSpec((None, tq, ATTN_WIDTH), lambda b, r, i: (b, i, r))
    o, lse = pl.pallas_call(
        functools.partial(_attn_kernel, radius=radius, m=m),
        out_shape=(jax.ShapeDtypeStruct((bsz, m, dil * ATTN_WIDTH), F32),
                   jax.ShapeDtypeStruct((bsz, m, dil * ATTN_WIDTH), F32)),
        grid=(bsz, dil, m // tq),
        in_specs=[cur(0), prev(1), cur(1), nxt(1), prev(2), cur(2), nxt(2)],
        out_specs=(out_spec, out_spec),
        compiler_params=pltpu.CompilerParams(
            dimension_semantics=("parallel", "parallel", "parallel"),
            vmem_limit_bytes=VMEM_LIMIT),
        name=f"attn_group{group}",
    )(qkv_v, qkv_v, qkv_v, qkv_v, qkv_v, qkv_v, qkv_v)
    return o.reshape(bsz, s, ATTN_WIDTH), lse.reshape(bsz, s, ATTN_WIDTH)


def _attn_out_kernel(o0, o1, o2, l0, l1, l2, z_ref, x_ref, gate_ref, w_ref, out_ref):
    la, lb, lc = l0[...], l1[...], l2[...]
    mx = jnp.maximum(jnp.maximum(la, lb), lc)
    ea, eb, ec = jnp.exp(la - mx), jnp.exp(lb - mx), jnp.exp(lc - mx)
    inv = 1.0 / (ea + eb + ec)
    o = (ea * inv) * o0[...] + (eb * inv) * o1[...] + (ec * inv) * o2[...]
    y = (o * _silu(z_ref[...])).astype(BF16)
    out = jnp.dot(y, w_ref[...], preferred_element_type=F32)
    out_ref[...] = x_ref[...] + gate_ref[...] * out


def _attn_out(os, lses, z, x, mod, w, tm=256):
    bsz, s, _ = x.shape
    row = pl.BlockSpec((None, tm, D_MODEL), lambda b, i: (b, i, 0))
    return pl.pallas_call(
        _attn_out_kernel,
        out_shape=jax.ShapeDtypeStruct(x.shape, F32),
        grid=(bsz, s // tm),
        in_specs=[row] * 8 + [
            pl.BlockSpec((None, 1, D_MODEL), lambda b, i: (b, 0, 2)),
            pl.BlockSpec((ATTN_WIDTH, D_MODEL), lambda b, i: (0, 0)),
        ],
        out_specs=row,
        compiler_params=pltpu.CompilerParams(
            dimension_semantics=("parallel", "parallel"),
            vmem_limit_bytes=VMEM_LIMIT),
        name="attn_out",
    )(*os, *lses, z, x, mod, w)


def _cmul(ar, ai, br, bi):
    return ar * br - ai * bi, ar * bi + ai * br


def _s5_prep_kernel(lam_re_a, lam_im_a, lam_re_b, lam_im_b, logdt, bt_re, bt_im,
                    c_re, c_im, toep_ref, mb_ref, mc_ref, al_ref, v_scr):
    lane = lax.broadcasted_iota(jnp.int32, (STATE, LANES), 1)
    a_idx = (lane // GROUP_CH).astype(F32)
    n_tiles = CHUNK // T_PER_VREG

    mb_rows = []
    v_rhs = []
    v_lhs = []
    mc_cols = []
    zbar_f = None
    for dr in range(2):
        dt = jnp.exp(logdt[dr])
        lr = jnp.minimum(lam_re_a[dr], LAMBDA_RE_MAX)
        li = lam_im_a[dr]
        ldr, ldi = lr * dt, li * dt

        def powers(e):
            mag = jnp.exp(e * ldr)
            return mag * jnp.cos(e * ldi), mag * jnp.sin(e * ldi)

        l1r, l1i = powers(1.0)
        den = lr * lr + li * li
        nr, ni = l1r - 1.0, l1i
        cr, ci = (nr * lr + ni * li) / den, (ni * lr - nr * li) / den
        bbr, bbi = _cmul(cr, ci, bt_re[dr], bt_im[dr])
        l8r, l8i = powers(float(T_PER_VREG))
        tiles = [None] * n_tiles
        if dr == 0:
            pr, pi = powers(float(T_PER_VREG - 1) - a_idx)
            cur = _cmul(pr, pi, bbr, bbi)
            for jt in range(n_tiles - 1, -1, -1):
                tiles[jt] = cur
                if jt:
                    cur = _cmul(cur[0], cur[1], l8r, l8i)
            v_rhs.append(_cmul(tiles[n_tiles - 1][0], tiles[n_tiles - 1][1], l1r, l1i))
            zbar_f = (jnp.where(lane < GROUP_CH, bbr, 0.0), jnp.where(lane < GROUP_CH, bbi, 0.0))
        else:
            pr, pi = powers(a_idx)
            cur = _cmul(pr, pi, bbr, bbi)
            for jt in range(n_tiles):
                tiles[jt] = cur
                if jt < n_tiles - 1:
                    cur = _cmul(cur[0], cur[1], l8r, l8i)
            v_rhs.append(tiles[0])
        mb_rows.append(jnp.concatenate([t[0] for t in tiles], axis=1))
        mb_rows.append(jnp.concatenate([t[1] for t in tiles], axis=1))
        a_r, a_i = l8r, l8i
        for _ in range(int(math.log2(CHUNK // T_PER_VREG))):
            a_r, a_i = _cmul(a_r, a_i, a_r, a_i)
        al_ref[2 * dr] = jnp.broadcast_to(a_r, (STATE, LANES))
        al_ref[2 * dr + 1] = jnp.broadcast_to(a_i, (STATE, LANES))

        lrb = jnp.minimum(lam_re_b[dr], LAMBDA_RE_MAX)
        lib = lam_im_b[dr]
        mag = jnp.exp(lrb * dt)
        pw_r, pw_i = mag * jnp.cos(lib * dt), mag * jnp.sin(lib * dt)
        pows = [(pw_r, pw_i)]
        for _ in range(int(math.log2(CHUNK)) - 1):
            pows.append(_cmul(*pows[-1], *pows[-1]))
        cre, cim = c_re[dr], c_im[dr]
        xr, xi = _cmul(cre, cim, *pows[0])
        for k in range(int(math.log2(CHUNK))):
            yr, yi = _cmul(xr, xi, *pows[k])
            if dr == 0:
                xr, xi = jnp.concatenate([xr, yr], 0), jnp.concatenate([xi, yi], 0)
            else:
                xr, xi = jnp.concatenate([yr, xr], 0), jnp.concatenate([yi, xi], 0)
        mc_cols += [xr, -xi]
        xr, xi = cre, cim
        for k in range(int(math.log2(n_tiles))):
            yr, yi = _cmul(xr, xi, *pows[k + int(math.log2(T_PER_VREG))])
            if dr == 0:
                xr, xi = jnp.concatenate([yr, xr], 0), jnp.concatenate([yi, xi], 0)
            else:
                xr, xi = jnp.concatenate([xr, yr], 0), jnp.concatenate([xi, yi], 0)
        v_lhs.append((xr, xi))

    mb_ref[...] = jnp.concatenate(mb_rows, axis=0).astype(BF16)
    mc_ref[...] = jnp.concatenate(mc_cols, axis=1).astype(BF16)

    def cdot(lhs, rhs):
        return (jnp.dot(lhs[0], rhs[0], precision=HI, preferred_element_type=F32)
                - jnp.dot(lhs[1], rhs[1], precision=HI, preferred_element_type=F32))

    out_f = cdot(v_lhs[0], v_rhs[0])
    zero_rows = jnp.zeros((CHUNK_ROWS // T_PER_VREG - GROUP_CH, STATE), F32)
    lag0_lhs = (jnp.concatenate([c_re[0], zero_rows], 0), jnp.concatenate([c_im[0], zero_rows], 0))
    out_b = cdot(v_lhs[1], v_rhs[1]) + cdot(lag0_lhs, zbar_f)
    for jt in range(n_tiles):
        rows = slice(jt * GROUP_CH, (jt + 1) * GROUP_CH)
        v_scr[:, jt * LANES:(jt + 1) * LANES] = out_f[rows]
        v_scr[:, (n_tiles + jt) * LANES:(n_tiles + jt + 1) * LANES] = out_b[rows]
    v_scr[:, 2 * CHUNK_ROWS:] = jnp.zeros((GROUP_CH, LANES), F32)

    for k in range(T_PER_VREG):
        vk = v_scr[:, k * GROUP_CH:k * GROUP_CH + 2 * CHUNK_ROWS]
        for t_out in range(CHUNK):
            off = (CHUNK - t_out) * GROUP_CH
            if off % LANES == k * GROUP_CH:
                al_off = off - k * GROUP_CH
                toep_ref[t_out * GROUP_CH:(t_out + 1) * GROUP_CH, :] = (
                    vk[:, al_off:al_off + CHUNK_ROWS].astype(BF16))


def _s5_prep(lam_re, lam_im, log_dt, b_re, b_im, c_re, c_im):
    g = SSM_GROUPS
    lam_a = lambda a: a.reshape(2, g, STATE, 1)
    lam_b = lambda a: a.reshape(2, g, 1, STATE)
    tile_b = lambda a: jnp.tile(a, (1, 1, 1, T_PER_VREG))
    dir_spec = lambda r, c: pl.BlockSpec((2, None, r, c), lambda i: (0, i, 0, 0))
    return pl.pallas_call(
        _s5_prep_kernel,
        out_shape=(jax.ShapeDtypeStruct((g, CHUNK_ROWS, CHUNK_ROWS), BF16),
                   jax.ShapeDtypeStruct((g, 4 * STATE, CHUNK_ROWS), BF16),
                   jax.ShapeDtypeStruct((g, CHUNK_ROWS, 4 * STATE), BF16),
                   jax.ShapeDtypeStruct((g, 4, STATE, LANES), F32)),
        grid=(g,),
        in_specs=[dir_spec(STATE, 1), dir_spec(STATE, 1), dir_spec(1, STATE), dir_spec(1, STATE),
                  dir_spec(1, 1), dir_spec(STATE, LANES), dir_spec(STATE, LANES),
                  dir_spec(GROUP_CH, STATE), dir_spec(GROUP_CH, STATE)],
        out_specs=(pl.BlockSpec((None, CHUNK_ROWS, CHUNK_ROWS), lambda i: (i, 0, 0)),
                   pl.BlockSpec((None, 4 * STATE, CHUNK_ROWS), lambda i: (i, 0, 0)),
                   pl.BlockSpec((None, CHUNK_ROWS, 4 * STATE), lambda i: (i, 0, 0)),
                   pl.BlockSpec((None, 4, STATE, LANES), lambda i: (i, 0, 0, 0))),
        scratch_shapes=[pltpu.VMEM((GROUP_CH, V_LANES), F32)],
        compiler_params=pltpu.CompilerParams(dimension_semantics=("parallel",)),
        name="s5_prep",
    )(lam_a(lam_re), lam_a(lam_im), lam_b(lam_re), lam_b(lam_im), log_dt.reshape(2, g, 1, 1),
      tile_b(b_re), tile_b(b_im), c_re, c_im)


def _s5_in_kernel(x_ref, shift_ref, scale_ref, g_ref, w_ref, u_ref, z_ref):
    h = _rms_modulate(x_ref[...], g_ref[...], scale_ref[...], shift_ref[...]).astype(BF16)
    uz = lax.dot_general(w_ref[...], h, (((1,), (1,)), ((), ())), preferred_element_type=F32)
    u_ref[...] = uz[:SSM_WIDTH]
    z_ref[...] = uz[SSM_WIDTH:]


def _s5_in(xc, shift_c, scale_c, g, w_t):
    n_chunks = xc.shape[0]
    row = pl.BlockSpec((n_chunks, D_MODEL), lambda t: (0, t))
    const = pl.BlockSpec((n_chunks, D_MODEL), lambda t: (0, 0))
    out = pl.BlockSpec((None, SSM_WIDTH, n_chunks), lambda t: (t, 0, 0))
    return pl.pallas_call(
        _s5_in_kernel,
        out_shape=(jax.ShapeDtypeStruct((CHUNK, SSM_WIDTH, n_chunks), F32),) * 2,
        grid=(CHUNK,),
        in_specs=[row, const, const,
                  pl.BlockSpec((1, D_MODEL), lambda t: (0, 0)),
                  pl.BlockSpec((2 * SSM_WIDTH, D_MODEL), lambda t: (0, 0))],
        out_specs=(out, out),
        compiler_params=pltpu.CompilerParams(
            dimension_semantics=("parallel",), vmem_limit_bytes=VMEM_LIMIT),
        name="s5_in",
    )(xc, shift_c, scale_c, g, w_t)


def _gelu_tanh(v):
    return 0.5 * v * (1.0 + jnp.tanh(math.sqrt(2.0 / math.pi) * (v + 0.044715 * (v * v * v))))


def _s5_core_kernel(u_ref, toep_ref, mb_ref, mc_ref, al_ref, d_ref, o_ref, *, chunks_per_seq):
    n_chunks = u_ref.shape[-1]
    u = u_ref[...].reshape(CHUNK_ROWS, n_chunks)
    ub = u.astype(BF16)
    inc = jnp.dot(mb_ref[...], ub, preferred_element_type=F32)
    pos = lax.broadcasted_iota(jnp.int32, (STATE, n_chunks), 1) % chunks_per_seq
    reps = n_chunks // LANES

    def lane_tile(a):
        return jnp.concatenate([a] * reps, axis=1) if reps > 1 else a

    states = []
    for dr in range(2):
        xr = inc[(2 * dr) * STATE:(2 * dr + 1) * STATE]
        xi = inc[(2 * dr + 1) * STATE:(2 * dr + 2) * STATE]
        ar, ai = lane_tile(al_ref[2 * dr]), lane_tile(al_ref[2 * dr + 1])

        def shifted(v, step):
            if dr == 0:
                return jnp.where(pos >= step, pltpu.roll(v, step, 1), 0.0)
            return jnp.where(pos < chunks_per_seq - step, pltpu.roll(v, n_chunks - step, 1), 0.0)

        step = 1
        while step < chunks_per_seq:
            sr, si = shifted(xr, step), shifted(xi, step)
            xr, xi = xr + ar * sr - ai * si, xi + ar * si + ai * sr
            ar, ai = _cmul(ar, ai, ar, ai)
            step *= 2
        states += [shifted(xr, 1), shifted(xi, 1)]
    h_in = jnp.concatenate(states, axis=0).astype(BF16)
    y = (jnp.dot(toep_ref[...], ub, preferred_element_type=F32)
         + jnp.dot(mc_ref[...], h_in, preferred_element_type=F32)
         + d_ref[...] * u)
    o_ref[...] = _gelu_tanh(y).reshape(CHUNK, GROUP_CH, n_chunks)


def _s5_core(u_cl, toep, mb, mc, al, d_col, chunks_per_seq):
    n_chunks = u_cl.shape[-1]
    u4 = u_cl.reshape(CHUNK, SSM_GROUPS, GROUP_CH, n_chunks)
    grp = pl.BlockSpec((CHUNK, None, GROUP_CH, n_chunks), lambda g: (0, g, 0, 0))
    out = pl.pallas_call(
        functools.partial(_s5_core_kernel, chunks_per_seq=chunks_per_seq),
        out_shape=jax.ShapeDtypeStruct(u4.shape, F32),
        grid=(SSM_GROUPS,),
        in_specs=[grp,
                  pl.BlockSpec((None, CHUNK_ROWS, CHUNK_ROWS), lambda g: (g, 0, 0)),
                  pl.BlockSpec((None, 4 * STATE, CHUNK_ROWS), lambda g: (g, 0, 0)),
                  pl.BlockSpec((None, CHUNK_ROWS, 4 * STATE), lambda g: (g, 0, 0)),
                  pl.BlockSpec((None, 4, STATE, LANES), lambda g: (g, 0, 0, 0)),
                  pl.BlockSpec((None, CHUNK_ROWS, 1), lambda g: (g, 0, 0))],
        out_specs=grp,
        compiler_params=pltpu.CompilerParams(
            dimension_semantics=("parallel",), vmem_limit_bytes=VMEM_LIMIT),
        name="s5_core",
    )(u4, toep, mb, mc, al, d_col)
    return out.reshape(CHUNK, SSM_WIDTH, n_chunks)


def _s5_out_kernel(g_ref, z_ref, wglu_ref, wout_ref, x_ref, gate_ref, fg_ref, o_ref, *, final):
    g = g_ref[...]
    glu = jnp.dot(wglu_ref[...], g.astype(BF16), preferred_element_type=F32)
    y = g * (1.0 / (1.0 + jnp.exp(-glu))) * _silu(z_ref[...])
    out_t = jnp.dot(wout_ref[...], y.astype(BF16), preferred_element_type=F32)
    xn = x_ref[...] + gate_ref[...] * out_t.T
    if final:
        xn = xn * lax.rsqrt(jnp.mean(xn * xn, axis=-1, keepdims=True) + NORM_EPS) * fg_ref[...]
    o_ref[...] = xn


def _s5_out(g_cl, z_cl, wglu_t, wout_t, xc, gate_c, final_g, final):
    n_chunks = xc.shape[0]
    act = pl.BlockSpec((None, SSM_WIDTH, n_chunks), lambda t: (t, 0, 0))
    wspec = pl.BlockSpec((SSM_WIDTH, SSM_WIDTH), lambda t: (0, 0))
    row = pl.BlockSpec((n_chunks, D_MODEL), lambda t: (0, t))
    return pl.pallas_call(
        functools.partial(_s5_out_kernel, final=final),
        out_shape=jax.ShapeDtypeStruct(xc.shape, F32),
        grid=(CHUNK,),
        in_specs=[act, act, wspec, wspec, row,
                  pl.BlockSpec((n_chunks, D_MODEL), lambda t: (0, 0)),
                  pl.BlockSpec((1, D_MODEL), lambda t: (0, 0))],
        out_specs=row,
        compiler_params=pltpu.CompilerParams(
            dimension_semantics=("parallel",), vmem_limit_bytes=VMEM_LIMIT),
        name="s5_out",
    )(g_cl, z_cl, wglu_t, wout_t, xc, gate_c, final_g)


def _rope_tables(s):
    inv_freq = ROPE_THETA ** (-jnp.arange(0, HEAD_DIM, 2, dtype=F32) / HEAD_DIM)
    ang = jnp.arange(s, dtype=F32)[:, None] * inv_freq[None, :]
    cos, sin = jnp.cos(ang), jnp.sin(ang)
    reps = LANES // (HEAD_DIM // 2)
    sign = jnp.where(jnp.arange(LANES) < LANES // 2, -1.0, 1.0).astype(F32)
    return jnp.tile(cos, (1, reps)), jnp.tile(sin, (1, reps)) * sign


def _qk_column_order():
    n = np.arange(ATTN_WIDTH)
    blk, lane = n // LANES, n % LANES
    half_w = HEAD_DIM // 2
    head = 2 * blk + (lane // half_w) % 2
    half = lane // HEAD_DIM
    return head * HEAD_DIM + half * half_w + lane % half_w


def _prep_attn_w_in(w):
    order = _qk_column_order()
    cols = []
    for g in range(N_DIL):
        base = g * 3 * ATTN_WIDTH
        cols += [base + order, base + ATTN_WIDTH + order, base + 2 * ATTN_WIDTH + np.arange(ATTN_WIDTH)]
    cols.append(3 * N_DIL * ATTN_WIDTH + np.arange(ATTN_WIDTH))
    return w[:, np.concatenate(cols)].astype(BF16)


def _trunk(x, ada, norm_g, attn_w, s5_w, s5_ops, final_norm_g):
    bsz, s, _ = x.shape
    chunks_per_seq = s // CHUNK
    n_chunks = bsz * chunks_per_seq
    cos, sin = _rope_tables(s)
    fg = final_norm_g.reshape(1, D_MODEL)
    for i in range(DEPTH):
        mod = ada[i].reshape(bsz, 1, 3 * D_MODEL)
        g = norm_g[i].reshape(1, D_MODEL)
        j = i // N_MIXERS
        if i % N_MIXERS == 0:
            w_in, w_out = attn_w[j]
            qkv, z = _attn_in(x, mod, g, w_in, cos, sin)
            outs = [_attn_group(qkv, grp) for grp in range(N_DIL)]
            x = _attn_out([o for o, _ in outs], [l for _, l in outs], z, x, mod, w_out)
        else:
            w_in_t, wglu_t, wout_t, d_col = s5_w[j]
            toep, mb, mc, al = s5_ops[j]
            per_chunk = lambda v: jnp.repeat(v, chunks_per_seq, axis=0)
            shift_c = per_chunk(ada[i][:, :D_MODEL])
            scale_c = per_chunk(ada[i][:, D_MODEL:2 * D_MODEL])
            gate_c = per_chunk(ada[i][:, 2 * D_MODEL:])
            xc = x.reshape(n_chunks, CHUNK * D_MODEL)
            u_cl, z_cl = _s5_in(xc, shift_c, scale_c, g, w_in_t)
            g_cl = _s5_core(u_cl, toep, mb, mc, al, d_col, chunks_per_seq)
            xc = _s5_out(g_cl, z_cl, wglu_t, wout_t, xc, gate_c, fg, final=(i == DEPTH - 1))
            x = xc.reshape(bsz, s, D_MODEL)
    return x


def kernel(x_prompt, x_sample, c_prompt, c_sample, norm_g, ada_w, ada_b, attn_w_in, attn_w_out,
           ssm_w_in, ssm_lam_re, ssm_lam_im, ssm_log_dt, ssm_b_re, ssm_b_im, ssm_c_re, ssm_c_im,
           ssm_d, ssm_w_glu, ssm_w_out, final_norm_g):
    assert (DEPTH - 1) % N_MIXERS == 1, "the final norm is fused into the last S5 layer"
    n_prompt = c_prompt.shape[0]
    ada = _ada(jnp.concatenate([c_prompt, c_sample], axis=0), ada_w, ada_b)
    attn_w = [(_prep_attn_w_in(attn_w_in[j]), attn_w_out[j].astype(BF16))
              for j in range(attn_w_in.shape[0])]
    s5_w, s5_ops = [], []
    for j in range(ssm_w_in.shape[0]):
        d_col = jnp.tile(ssm_d[j].reshape(SSM_GROUPS, 1, GROUP_CH), (1, CHUNK, 1))
        s5_w.append((ssm_w_in[j].T.astype(BF16), ssm_w_glu[j].T.astype(BF16),
                     ssm_w_out[j].T.astype(BF16), d_col.reshape(SSM_GROUPS, CHUNK_ROWS, 1)))
        s5_ops.append(_s5_prep(ssm_lam_re[j], ssm_lam_im[j], ssm_log_dt[j], ssm_b_re[j],
                               ssm_b_im[j], ssm_c_re[j], ssm_c_im[j]))
    y_prompt = _trunk(x_prompt, ada[:, :n_prompt], norm_g, attn_w, s5_w, s5_ops, final_norm_g)
    y_sample = _trunk(x_sample, ada[:, n_prompt:], norm_g, attn_w, s5_w, s5_ops, final_norm_g)
    return (y_prompt, y_sample)
```

```python
import functools
import math

import numpy as np
import jax
import jax.numpy as jnp
from jax import lax
from jax.experimental import pallas as pl
from jax.experimental.pallas import tpu as pltpu

D_MODEL = 1024
DEPTH = 4
N_MIXERS = 2
ATTN_HEADS = 16
HEAD_DIM = 64
ATTN_WIDTH = ATTN_HEADS * HEAD_DIM
DILATED_PAIRS = ((128, 1), (512, 4), (2048, 16))
N_DIL = len(DILATED_PAIRS)
ROPE_THETA = 10000.0
SSM_WIDTH = D_MODEL
GROUP_CH = 16
SSM_GROUPS = SSM_WIDTH // GROUP_CH
STATE = 64
LAMBDA_RE_MAX = -1e-4
NORM_EPS = 1e-6
NEG_INF = -1e30

LANES = 128
SUBLANES = 8
CHUNK = 64
CHUNK_ROWS = CHUNK * GROUP_CH
T_PER_VREG = LANES // GROUP_CH
V_LANES = 2 * CHUNK_ROWS + LANES
STREAMS = 16
RADIUS = 64
ATTN_TQ = 128
VMEM_LIMIT = 48 * 1024 * 1024

F32 = jnp.float32
BF16 = jnp.bfloat16
HI = lax.Precision.HIGHEST


def _silu(v):
    return v * (1.0 / (1.0 + jnp.exp(-v)))


def _rms_modulate(x, g, scale, shift):
    rs = lax.rsqrt(jnp.mean(x * x, axis=-1, keepdims=True) + NORM_EPS)
    return (x * rs * g) * (1.0 + scale) + shift


def _ada_kernel(c_ref, w_ref, b_ref, o_ref):
    o_ref[...] = jnp.dot(_silu(c_ref[...]), w_ref[...], precision=HI,
                         preferred_element_type=F32) + b_ref[...]


def _ada(c, ada_w, ada_b):
    nb = c.shape[0]
    return pl.pallas_call(
        _ada_kernel,
        out_shape=jax.ShapeDtypeStruct((DEPTH, nb, 3 * D_MODEL), F32),
        grid=(DEPTH, 3),
        in_specs=[
            pl.BlockSpec((nb, D_MODEL), lambda i, j: (0, 0)),
            pl.BlockSpec((None, D_MODEL, D_MODEL), lambda i, j: (i, 0, j)),
            pl.BlockSpec((None, 1, D_MODEL), lambda i, j: (i, 0, j)),
        ],
        out_specs=pl.BlockSpec((None, nb, D_MODEL), lambda i, j: (i, 0, j)),
        name="ada",
    )(c, ada_w, ada_b.reshape(DEPTH, 1, 3 * D_MODEL))


def _attn_in_kernel(x_ref, shift_ref, scale_ref, g_ref, w_ref, cos_ref, sin_ref, cosp_ref, sinp_ref,
                    qkv0_ref, qkv12_ref, z_ref, h_nat, h_perm, h_f32):
    j = pl.program_id(2)
    tm = x_ref.shape[0]
    rows = tm // STREAMS

    @pl.when(j == 0)
    def _():
        h = _rms_modulate(x_ref[...], g_ref[...], scale_ref[...], shift_ref[...])
        h_nat[...] = h.astype(BF16)
        h_f32[...] = h.reshape(rows, STREAMS, D_MODEL)
        for r in range(STREAMS):
            h_perm[r * rows:(r + 1) * rows, :] = h_f32[:, r, :].astype(BF16)

    part = j % 3

    def rope_blocks(acc, cos, sin):
        qscale = jnp.where(part == 0, HEAD_DIM ** -0.5, 1.0).astype(F32)
        cos = jnp.where(part == 2, 1.0, cos * qscale)
        sin = jnp.where(part == 2, 0.0, sin * qscale)
        for b in range(ATTN_WIDTH // LANES):
            sl = slice(b * LANES, (b + 1) * LANES)
            t = acc[:, sl]
            yield sl, (t * cos + pltpu.roll(t, LANES // 2, 1) * sin).astype(BF16)

    @pl.when(j < 3)
    def _():
        acc = jnp.dot(h_nat[...], w_ref[...], preferred_element_type=F32)
        for sl, blk in rope_blocks(acc, cos_ref[...], sin_ref[...]):
            qkv0_ref[:, sl] = blk

    @pl.when(jnp.logical_and(j >= 3, j < 3 * N_DIL))
    def _():
        acc = jnp.dot(h_perm[...], w_ref[...], preferred_element_type=F32)
        cosp = cosp_ref[...].reshape(tm, LANES)
        sinp = sinp_ref[...].reshape(tm, LANES)
        for sl, blk in rope_blocks(acc, cosp, sinp):
            qkv12_ref[:, :, sl] = blk.reshape(STREAMS, rows, LANES)

    @pl.when(j == 3 * N_DIL)
    def _():
        z_ref[...] = jnp.dot(h_nat[...], w_ref[...], preferred_element_type=F32).astype(BF16)


def _attn_in(x, mod, g, w, tables, tm=512):
    bsz, s, _ = x.shape
    cos, sin, cosp, sinp = tables
    rows = tm // STREAMS
    n_col = 3 * N_DIL + 1
    return pl.pallas_call(
        _attn_in_kernel,
        out_shape=(jax.ShapeDtypeStruct((bsz, s, 3 * ATTN_WIDTH), BF16),
                   jax.ShapeDtypeStruct((bsz, STREAMS, s // STREAMS, 6 * ATTN_WIDTH), BF16),
                   jax.ShapeDtypeStruct((bsz, s, ATTN_WIDTH), BF16)),
        grid=(bsz, s // tm, n_col),
        in_specs=[
            pl.BlockSpec((None, tm, D_MODEL), lambda b, i, j: (b, i, 0)),
            pl.BlockSpec((None, 1, D_MODEL), lambda b, i, j: (b, 0, 0)),
            pl.BlockSpec((None, 1, D_MODEL), lambda b, i, j: (b, 0, 1)),
            pl.BlockSpec((1, D_MODEL), lambda b, i, j: (0, 0)),
            pl.BlockSpec((D_MODEL, ATTN_WIDTH), lambda b, i, j: (0, j)),
            pl.BlockSpec((tm, LANES), lambda b, i, j: (i, 0)),
            pl.BlockSpec((tm, LANES), lambda b, i, j: (i, 0)),
            pl.BlockSpec((STREAMS, rows, LANES), lambda b, i, j: (0, i, 0)),
            pl.BlockSpec((STREAMS, rows, LANES), lambda b, i, j: (0, i, 0)),
        ],
        out_specs=(
            pl.BlockSpec((None, tm, ATTN_WIDTH), lambda b, i, j: (b, i, jnp.clip(j, 0, 2))),
            pl.BlockSpec((None, STREAMS, rows, ATTN_WIDTH),
                         lambda b, i, j: (b, 0, i, jnp.clip(j - 3, 0, 5))),
            pl.BlockSpec((None, tm, ATTN_WIDTH), lambda b, i, j: (b, i, 0)),
        ),
        scratch_shapes=[pltpu.VMEM((tm, D_MODEL), BF16), pltpu.VMEM((tm, D_MODEL), BF16),
                        pltpu.VMEM((rows, STREAMS, D_MODEL), F32)],
        compiler_params=pltpu.CompilerParams(
            dimension_semantics=("parallel", "parallel", "arbitrary"),
            vmem_limit_bytes=VMEM_LIMIT),
        name="attn_in",
    )(x, mod, mod, g, w, cos, sin, cosp, sinp)


def _attn_pairs(get_q, get_k, get_v, valid, get_prev, prev_lse, put_o):
    tq = valid.shape[0]
    lane = lax.broadcasted_iota(jnp.int32, (tq, LANES), 1)
    q_is_odd = (lane // (HEAD_DIM // 2)) % 2 == 1
    v_is_odd = lane >= HEAD_DIM
    lse_tile = jnp.zeros((tq, LANES), F32)
    for b in range(ATTN_WIDTH // LANES):
        sl = slice(b * LANES, (b + 1) * LANES)
        qb = get_q(sl).astype(F32)
        kb = get_k(sl)
        vb = get_v(sl)
        po = None if get_prev is None else get_prev(sl)
        outs = []
        for odd in (0, 1):
            qm = (jnp.where(q_is_odd, qb, 0.0) if odd else jnp.where(q_is_odd, 0.0, qb)).astype(BF16)
            sc = lax.dot_general(qm, kb, (((1,), (1,)), ((), ())), preferred_element_type=F32)
            sc = jnp.where(valid, sc, NEG_INF)
            mx = jnp.max(sc, axis=1, keepdims=True)
            p = jnp.exp(sc - mx)
            den = jnp.sum(p, axis=1, keepdims=True)
            acc = jnp.dot(p.astype(BF16), vb, preferred_element_type=F32)
            lse = mx + jnp.log(den)
            if po is None:
                o = acc * (1.0 / den)
            else:
                head = 2 * b + odd
                lp = prev_lse[:, head:head + 1]
                top = jnp.maximum(lp, lse)
                new = top + jnp.log(jnp.exp(lp - top) + jnp.exp(lse - top))
                o = jnp.exp(lp - new) * po + (jnp.exp(lse - new) / den) * acc
                lse = new
            outs.append(o)
            lse_tile = jnp.where(lane == 2 * b + odd, lse, lse_tile)
        put_o(sl, jnp.where(v_is_odd, outs[1], outs[0]))
    return lse_tile


def _window_mask(i, tq, m):
    nk = tq + 2 * RADIUS
    row = lax.broadcasted_iota(jnp.int32, (tq, nk), 0)
    col = lax.broadcasted_iota(jnp.int32, (tq, nk), 1)
    kabs = i * tq - RADIUS + col
    return (jnp.abs(col - RADIUS - row) <= RADIUS) & (kabs >= 0) & (kabs < m)


def _cat3(p_ref, c_ref, n_ref):
    return lambda sl: jnp.concatenate([p_ref[:, sl], c_ref[:, sl], n_ref[:, sl]], axis=0)


def _attn16_kernel(q_ref, kp_ref, kc_ref, kn_ref, vp_ref, vc_ref, vn_ref, o_ref, lse_ref, *, m):
    valid = _window_mask(pl.program_id(2), q_ref.shape[0], m)

    def put_o(sl, val):
        o_ref[:, sl] = val

    lse_ref[...] = _attn_pairs(lambda sl: q_ref[:, sl], _cat3(kp_ref, kc_ref, kn_ref),
                               _cat3(vp_ref, vc_ref, vn_ref), valid, None, None, put_o)


def _attn4_kernel(q_ref, kp_ref, kc_ref, kn_ref, vp_ref, vc_ref, vn_ref, po_ref, pl_ref,
                  o_ref, lse_ref, *, m):
    i = pl.program_id(2)
    ns, rows = q_ref.shape[0], q_ref.shape[1]
    halo = kp_ref.shape[1]
    tq, nk = ns * rows, ns * (rows + 2 * halo)
    n = lax.broadcasted_iota(jnp.int32, (tq, nk), 0)
    c = lax.broadcasted_iota(jnp.int32, (tq, nk), 1)
    qpos = ns * (rows * i + n % rows) + n // rows
    is_prev, is_next = c < ns * halo, c >= ns * (halo + rows)
    edge = is_prev | is_next
    cc = jnp.where(is_prev, c, jnp.where(is_next, c - ns * (halo + rows), c - ns * halo))
    krow = rows * i + jnp.where(is_prev, -halo, jnp.where(is_next, rows, 0)) + jnp.where(edge, cc % halo, cc % rows)
    kpos = ns * krow + jnp.where(edge, cc // halo, cc // rows)
    valid = (jnp.abs(kpos - qpos) <= RADIUS) & (krow >= 0) & (krow < m)

    def cat(p_ref, c_ref, n_ref):
        return lambda sl: jnp.concatenate(
            [p_ref[:, :, sl].reshape(ns * halo, LANES), c_ref[:, :, sl].reshape(tq, LANES),
             n_ref[:, :, sl].reshape(ns * halo, LANES)], axis=0)

    def put_o(sl, val):
        o_ref[:, :, sl] = val.reshape(ns, rows, LANES)

    lse = _attn_pairs(lambda sl: q_ref[:, :, sl].reshape(tq, LANES), cat(kp_ref, kc_ref, kn_ref),
                      cat(vp_ref, vc_ref, vn_ref), valid,
                      lambda sl: po_ref[:, :, sl].reshape(tq, LANES),
                      pl_ref[...].reshape(tq, LANES), put_o)
    lse_ref[...] = lse.reshape(ns, rows, LANES)


def _attn1_kernel(q_ref, kp_ref, kc_ref, kn_ref, vp_ref, vc_ref, vn_ref, po_ref, pl_ref, o_ref, *, m):
    valid = _window_mask(pl.program_id(1), q_ref.shape[0], m)
    rows = po_ref.shape[1]

    def natural(ref, sl):
        return jnp.concatenate([ref[:, ii, sl] for ii in range(rows)], axis=0)

    def put_o(sl, val):
        o_ref[:, sl] = val.astype(BF16)

    _attn_pairs(lambda sl: q_ref[:, sl], _cat3(kp_ref, kc_ref, kn_ref), _cat3(vp_ref, vc_ref, vn_ref),
                valid, lambda sl: natural(po_ref, sl), natural(pl_ref, slice(0, LANES)), put_o)


def _attention(qkv0, qkv12):
    bsz, s, _ = qkv0.shape
    m16 = s // STREAMS
    tq, half = ATTN_TQ, RADIUS
    params = lambda n: pltpu.CompilerParams(dimension_semantics=("parallel",) * n,
                                            vmem_limit_bytes=VMEM_LIMIT)

    def spec16(nrows, row_idx, col):
        return pl.BlockSpec((None, None, nrows, ATTN_WIDTH), lambda b, r, i: (b, r, row_idx(i), col))
    per = tq // half
    cur = lambda i: i
    prv = lambda i: jnp.maximum(i * per - 1, 0)
    nxt = lambda i: jnp.minimum((i + 1) * per, m16 // half - 1)
    o2, l2 = pl.pallas_call(
        functools.partial(_attn16_kernel, m=m16),
        out_shape=(jax.ShapeDtypeStruct((bsz, STREAMS, m16, ATTN_WIDTH), F32),
                   jax.ShapeDtypeStruct((bsz, STREAMS, m16, LANES), F32)),
        grid=(bsz, STREAMS, m16 // tq),
        in_specs=[spec16(tq, cur, 3), spec16(half, prv, 4), spec16(tq, cur, 4), spec16(half, nxt, 4),
                  spec16(half, prv, 5), spec16(tq, cur, 5), spec16(half, nxt, 5)],
        out_specs=(pl.BlockSpec((None, None, tq, ATTN_WIDTH), lambda b, r, i: (b, r, i, 0)),
                   pl.BlockSpec((None, None, tq, LANES), lambda b, r, i: (b, r, i, 0))),
        compiler_params=params(3), name="attn_dil16",
    )(*([qkv12] * 7))

    ns = STREAMS // 4
    rows, halo = tq // ns, half // ns

    def view4(a):
        return a.reshape(bsz, ns, 4, m16, a.shape[-1])

    def spec4(nrows, row_idx, col, width=ATTN_WIDTH):
        return pl.BlockSpec((None, ns, None, nrows, width), lambda b, r, i: (b, 0, r, row_idx(i), col))
    per4 = rows // halo
    prv4 = lambda i: jnp.maximum(i * per4 - 1, 0)
    nxt4 = lambda i: jnp.minimum((i + 1) * per4, m16 // halo - 1)
    q4 = view4(qkv12)
    o12, l12 = pl.pallas_call(
        functools.partial(_attn4_kernel, m=m16),
        out_shape=(jax.ShapeDtypeStruct((bsz, ns, 4, m16, ATTN_WIDTH), F32),
                   jax.ShapeDtypeStruct((bsz, ns, 4, m16, LANES), F32)),
        grid=(bsz, 4, m16 // rows),
        in_specs=[spec4(rows, cur, 0), spec4(halo, prv4, 1), spec4(rows, cur, 1), spec4(halo, nxt4, 1),
                  spec4(halo, prv4, 2), spec4(rows, cur, 2), spec4(halo, nxt4, 2),
                  spec4(rows, cur, 0), spec4(rows, cur, 0, LANES)],
        out_specs=(spec4(rows, cur, 0), spec4(rows, cur, 0, LANES)),
        compiler_params=params(3), name="attn_dil4",
    )(q4, q4, q4, q4, q4, q4, q4, view4(o2), view4(l2))
    o12 = o12.reshape(bsz, STREAMS, m16, ATTN_WIDTH)
    l12 = l12.reshape(bsz, STREAMS, m16, LANES)

    def spec1(nrows, row_idx, col):
        return pl.BlockSpec((None, nrows, ATTN_WIDTH), lambda b, i: (b, row_idx(i), col))
    nxt1 = lambda i: jnp.minimum((i + 1) * per, s // half - 1)
    srow = tq // STREAMS
    return pl.pallas_call(
        functools.partial(_attn1_kernel, m=s),
        out_shape=jax.ShapeDtypeStruct((bsz, s, ATTN_WIDTH), BF16),
        grid=(bsz, s // tq),
        in_specs=[spec1(tq, cur, 0), spec1(half, prv, 1), spec1(tq, cur, 1), spec1(half, nxt1, 1),
                  spec1(half, prv, 2), spec1(tq, cur, 2), spec1(half, nxt1, 2),
                  pl.BlockSpec((None, STREAMS, srow, ATTN_WIDTH), lambda b, i: (b, 0, i, 0)),
                  pl.BlockSpec((None, STREAMS, srow, LANES), lambda b, i: (b, 0, i, 0))],
        out_specs=pl.BlockSpec((None, tq, ATTN_WIDTH), lambda b, i: (b, i, 0)),
        compiler_params=params(2), name="attn_dil1",
    )(*([qkv0] * 7), o12, l12)


def _attn_out_kernel(o_ref, z_ref, x_ref, gate_ref, w_ref, out_ref):
    y = (o_ref[...].astype(F32) * _silu(z_ref[...].astype(F32))).astype(BF16)
    out = jnp.dot(y, w_ref[...], preferred_element_type=F32)
    out_ref[...] = x_ref[...] + gate_ref[...] * out


def _attn_out(o, z, x, mod, w, tm=512):
    bsz, s, _ = x.shape
    row = pl.BlockSpec((None, tm, D_MODEL), lambda b, i: (b, i, 0))
    return pl.pallas_call(
        _attn_out_kernel,
        out_shape=jax.ShapeDtypeStruct(x.shape, F32),
        grid=(bsz, s // tm),
        in_specs=[row, row, row,
                  pl.BlockSpec((None, 1, D_MODEL), lambda b, i: (b, 0, 2)),
                  pl.BlockSpec((ATTN_WIDTH, D_MODEL), lambda b, i: (0, 0))],
        out_specs=row,
        compiler_params=pltpu.CompilerParams(
            dimension_semantics=("parallel", "parallel"),
            vmem_limit_bytes=VMEM_LIMIT),
        name="attn_out",
    )(o, z, x, mod, w)


def _cmul(ar, ai, br, bi):
    return ar * br - ai * bi, ar * bi + ai * br


def _s5_prep_kernel(lam_re_a, lam_im_a, lam_re_b, lam_im_b, logdt, bt_re, bt_im,
                    c_re, c_im, toep_ref, mb_ref, mc_ref, al_ref, v_scr):
    lane = lax.broadcasted_iota(jnp.int32, (STATE, LANES), 1)
    a_idx = (lane // GROUP_CH).astype(F32)
    n_tiles = CHUNK // T_PER_VREG

    mb_rows = []
    v_rhs = []
    v_lhs = []
    mc_cols = []
    zbar_f = None
    for dr in range(2):
        dt = jnp.exp(logdt[dr])
        lr = jnp.minimum(lam_re_a[dr], LAMBDA_RE_MAX)
        li = lam_im_a[dr]
        ldr, ldi = lr * dt, li * dt

        def powers(e):
            mag = jnp.exp(e * ldr)
            return mag * jnp.cos(e * ldi), mag * jnp.sin(e * ldi)

        l1r, l1i = powers(1.0)
        den = lr * lr + li * li
        nr, ni = l1r - 1.0, l1i
        cr, ci = (nr * lr + ni * li) / den, (ni * lr - nr * li) / den
        bbr, bbi = _cmul(cr, ci, bt_re[dr], bt_im[dr])
        l8r, l8i = powers(float(T_PER_VREG))
        tiles = [None] * n_tiles
        if dr == 0:
            pr, pi = powers(float(T_PER_VREG - 1) - a_idx)
            cur = _cmul(pr, pi, bbr, bbi)
            for jt in range(n_tiles - 1, -1, -1):
                tiles[jt] = cur
                if jt:
                    cur = _cmul(cur[0], cur[1], l8r, l8i)
            v_rhs.append(_cmul(tiles[n_tiles - 1][0], tiles[n_tiles - 1][1], l1r, l1i))
            zbar_f = (jnp.where(lane < GROUP_CH, bbr, 0.0), jnp.where(lane < GROUP_CH, bbi, 0.0))
        else:
            pr, pi = powers(a_idx)
            cur = _cmul(pr, pi, bbr, bbi)
            for jt in range(n_tiles):
                tiles[jt] = cur
                if jt < n_tiles - 1:
                    cur = _cmul(cur[0], cur[1], l8r, l8i)
            v_rhs.append(tiles[0])
        mb_rows.append(jnp.concatenate([t[0] for t in tiles], axis=1))
        mb_rows.append(jnp.concatenate([t[1] for t in tiles], axis=1))
        a_r, a_i = l8r, l8i
        for _ in range(int(math.log2(CHUNK // T_PER_VREG))):
            a_r, a_i = _cmul(a_r, a_i, a_r, a_i)
        al_ref[2 * dr] = jnp.broadcast_to(a_r, (STATE, LANES))
        al_ref[2 * dr + 1] = jnp.broadcast_to(a_i, (STATE, LANES))

        lrb = jnp.minimum(lam_re_b[dr], LAMBDA_RE_MAX)
        lib = lam_im_b[dr]
        mag = jnp.exp(lrb * dt)
        pw_r, pw_i = mag * jnp.cos(lib * dt), mag * jnp.sin(lib * dt)
        pows = [(pw_r, pw_i)]
        for _ in range(int(math.log2(CHUNK)) - 1):
            pows.append(_cmul(*pows[-1], *pows[-1]))
        cre, cim = c_re[dr], c_im[dr]
        xr, xi = _cmul(cre, cim, *pows[0])
        for k in range(int(math.log2(CHUNK))):
            yr, yi = _cmul(xr, xi, *pows[k])
            if dr == 0:
                xr, xi = jnp.concatenate([xr, yr], 0), jnp.concatenate([xi, yi], 0)
            else:
                xr, xi = jnp.concatenate([yr, xr], 0), jnp.concatenate([yi, xi], 0)
        mc_cols += [xr, -xi]
        xr, xi = cre, cim
        for k in range(int(math.log2(n_tiles))):
            yr, yi = _cmul(xr, xi, *pows[k + int(math.log2(T_PER_VREG))])
            if dr == 0:
                xr, xi = jnp.concatenate([yr, xr], 0), jnp.concatenate([yi, xi], 0)
            else:
                xr, xi = jnp.concatenate([xr, yr], 0), jnp.concatenate([xi, yi], 0)
        v_lhs.append((xr, xi))

    mb_ref[...] = jnp.concatenate(mb_rows, axis=0).astype(BF16)
    mc_ref[...] = jnp.concatenate(mc_cols, axis=1).astype(BF16)

    def cdot(lhs, rhs):
        return (jnp.dot(lhs[0], rhs[0], precision=HI, preferred_element_type=F32)
                - jnp.dot(lhs[1], rhs[1], precision=HI, preferred_element_type=F32))

    out_f = cdot(v_lhs[0], v_rhs[0])
    zero_rows = jnp.zeros((CHUNK_ROWS // T_PER_VREG - GROUP_CH, STATE), F32)
    lag0_lhs = (jnp.concatenate([c_re[0], zero_rows], 0), jnp.concatenate([c_im[0], zero_rows], 0))
    out_b = cdot(v_lhs[1], v_rhs[1]) + cdot(lag0_lhs, zbar_f)
    for jt in range(n_tiles):
        rows = slice(jt * GROUP_CH, (jt + 1) * GROUP_CH)
        v_scr[:, jt * LANES:(jt + 1) * LANES] = out_f[rows]
        v_scr[:, (n_tiles + jt) * LANES:(n_tiles + jt + 1) * LANES] = out_b[rows]
    v_scr[:, 2 * CHUNK_ROWS:] = jnp.zeros((GROUP_CH, LANES), F32)

    for k in range(T_PER_VREG):
        vk = v_scr[:, k * GROUP_CH:k * GROUP_CH + 2 * CHUNK_ROWS]
        for t_out in range(CHUNK):
            off = (CHUNK - t_out) * GROUP_CH
            if off % LANES == k * GROUP_CH:
                al_off = off - k * GROUP_CH
                toep_ref[t_out * GROUP_CH:(t_out + 1) * GROUP_CH, :] = (
                    vk[:, al_off:al_off + CHUNK_ROWS].astype(BF16))


def _s5_prep(lam_re, lam_im, log_dt, b_re, b_im, c_re, c_im):
    g = SSM_GROUPS
    lam_a = lambda a: a.reshape(2, g, STATE, 1)
    lam_b = lambda a: a.reshape(2, g, 1, STATE)
    tile_b = lambda a: jnp.tile(a, (1, 1, 1, T_PER_VREG))
    dir_spec = lambda r, c: pl.BlockSpec((2, None, r, c), lambda i: (0, i, 0, 0))
    return pl.pallas_call(
        _s5_prep_kernel,
        out_shape=(jax.ShapeDtypeStruct((g, CHUNK_ROWS, CHUNK_ROWS), BF16),
                   jax.ShapeDtypeStruct((g, 4 * STATE, CHUNK_ROWS), BF16),
                   jax.ShapeDtypeStruct((g, CHUNK_ROWS, 4 * STATE), BF16),
                   jax.ShapeDtypeStruct((g, 4, STATE, LANES), F32)),
        grid=(g,),
        in_specs=[dir_spec(STATE, 1), dir_spec(STATE, 1), dir_spec(1, STATE), dir_spec(1, STATE),
                  dir_spec(1, 1), dir_spec(STATE, LANES), dir_spec(STATE, LANES),
                  dir_spec(GROUP_CH, STATE), dir_spec(GROUP_CH, STATE)],
        out_specs=(pl.BlockSpec((None, CHUNK_ROWS, CHUNK_ROWS), lambda i: (i, 0, 0)),
                   pl.BlockSpec((None, 4 * STATE, CHUNK_ROWS), lambda i: (i, 0, 0)),
                   pl.BlockSpec((None, CHUNK_ROWS, 4 * STATE), lambda i: (i, 0, 0)),
                   pl.BlockSpec((None, 4, STATE, LANES), lambda i: (i, 0, 0, 0))),
        scratch_shapes=[pltpu.VMEM((GROUP_CH, V_LANES), F32)],
        compiler_params=pltpu.CompilerParams(dimension_semantics=("parallel",)),
        name="s5_prep",
    )(lam_a(lam_re), lam_a(lam_im), lam_b(lam_re), lam_b(lam_im), log_dt.reshape(2, g, 1, 1),
      tile_b(b_re), tile_b(b_im), c_re, c_im)


def _s5_in_kernel(x_ref, shift_ref, scale_ref, g_ref, w_ref, u_ref, z_ref):
    k = pl.program_id(0) % SUBLANES
    h = _rms_modulate(x_ref[:, k, :], g_ref[...], scale_ref[...], shift_ref[...]).astype(BF16)
    uz = lax.dot_general(w_ref[...], h, (((1,), (1,)), ((), ())), preferred_element_type=F32)
    u_ref[...] = uz[:SSM_WIDTH].astype(BF16)
    z_ref[...] = uz[SSM_WIDTH:].astype(BF16)


def _s5_in(x3, shift_c, scale_c, g, w_t):
    n_chunks = x3.shape[0]
    const = pl.BlockSpec((n_chunks, D_MODEL), lambda t: (0, 0))
    out = pl.BlockSpec((None, SSM_WIDTH, n_chunks), lambda t: (t, 0, 0))
    return pl.pallas_call(
        _s5_in_kernel,
        out_shape=(jax.ShapeDtypeStruct((CHUNK, SSM_WIDTH, n_chunks), BF16),) * 2,
        grid=(CHUNK,),
        in_specs=[pl.BlockSpec((n_chunks, SUBLANES, D_MODEL), lambda t: (0, t // SUBLANES, 0)),
                  const, const,
                  pl.BlockSpec((1, D_MODEL), lambda t: (0, 0)),
                  pl.BlockSpec((2 * SSM_WIDTH, D_MODEL), lambda t: (0, 0))],
        out_specs=(out, out),
        compiler_params=pltpu.CompilerParams(
            dimension_semantics=("arbitrary",), vmem_limit_bytes=VMEM_LIMIT),
        name="s5_in",
    )(x3, shift_c, scale_c, g, w_t)


def _gelu_tanh(v):
    return 0.5 * v * (1.0 + jnp.tanh(math.sqrt(2.0 / math.pi) * (v + 0.044715 * (v * v * v))))


def _s5_core_kernel(u_ref, toep_ref, mb_ref, mc_ref, al_ref, d_ref, o_ref, *, chunks_per_seq):
    n_chunks = u_ref.shape[-1]
    ub = u_ref[...].reshape(CHUNK_ROWS, n_chunks)
    inc = jnp.dot(mb_ref[...], ub, preferred_element_type=F32)
    pos = lax.broadcasted_iota(jnp.int32, (STATE, n_chunks), 1) % chunks_per_seq
    reps = n_chunks // LANES

    def lane_tile(a):
        return jnp.concatenate([a] * reps, axis=1) if reps > 1 else a

    states = []
    for dr in range(2):
        xr = inc[(2 * dr) * STATE:(2 * dr + 1) * STATE]
        xi = inc[(2 * dr + 1) * STATE:(2 * dr + 2) * STATE]
        ar, ai = lane_tile(al_ref[2 * dr]), lane_tile(al_ref[2 * dr + 1])

        def shifted(v, step):
            if dr == 0:
                return jnp.where(pos >= step, pltpu.roll(v, step, 1), 0.0)
            return jnp.where(pos < chunks_per_seq - step, pltpu.roll(v, n_chunks - step, 1), 0.0)

        step = 1
        while step < chunks_per_seq:
            sr, si = shifted(xr, step), shifted(xi, step)
            xr, xi = xr + ar * sr - ai * si, xi + ar * si + ai * sr
            ar, ai = _cmul(ar, ai, ar, ai)
            step *= 2
        states += [shifted(xr, 1), shifted(xi, 1)]
    h_in = jnp.concatenate(states, axis=0).astype(BF16)
    y = (jnp.dot(toep_ref[...], ub, preferred_element_type=F32)
         + jnp.dot(mc_ref[...], h_in, preferred_element_type=F32)
         + d_ref[...] * ub.astype(F32))
    o_ref[...] = _gelu_tanh(y).astype(BF16).reshape(CHUNK, GROUP_CH, n_chunks)


def _s5_core(u_cl, toep, mb, mc, al, d_col, chunks_per_seq):
    n_chunks = u_cl.shape[-1]
    u4 = u_cl.reshape(CHUNK, SSM_GROUPS, GROUP_CH, n_chunks)
    grp = pl.BlockSpec((CHUNK, None, GROUP_CH, n_chunks), lambda g: (0, g, 0, 0))
    out = pl.pallas_call(
        functools.partial(_s5_core_kernel, chunks_per_seq=chunks_per_seq),
        out_shape=jax.ShapeDtypeStruct(u4.shape, BF16),
        grid=(SSM_GROUPS,),
        in_specs=[grp,
                  pl.BlockSpec((None, CHUNK_ROWS, CHUNK_ROWS), lambda g: (g, 0, 0)),
                  pl.BlockSpec((None, 4 * STATE, CHUNK_ROWS), lambda g: (g, 0, 0)),
                  pl.BlockSpec((None, CHUNK_ROWS, 4 * STATE), lambda g: (g, 0, 0)),
                  pl.BlockSpec((None, 4, STATE, LANES), lambda g: (g, 0, 0, 0)),
                  pl.BlockSpec((None, CHUNK_ROWS, 1), lambda g: (g, 0, 0))],
        out_specs=grp,
        compiler_params=pltpu.CompilerParams(
            dimension_semantics=("parallel",), vmem_limit_bytes=VMEM_LIMIT),
        name="s5_core",
    )(u4, toep, mb, mc, al, d_col)
    return out.reshape(CHUNK, SSM_WIDTH, n_chunks)


def _s5_out_kernel(g_ref, z_ref, wglu_ref, wout_ref, x_ref, gate_ref, fg_ref, o_ref, *, final):
    k = pl.program_id(0) % SUBLANES
    gb = g_ref[...]
    glu = jnp.dot(wglu_ref[...], gb, preferred_element_type=F32)
    y = gb.astype(F32) * (1.0 / (1.0 + jnp.exp(-glu))) * _silu(z_ref[...].astype(F32))
    out_t = jnp.dot(wout_ref[...], y.astype(BF16), preferred_element_type=F32)
    xn = x_ref[:, k, :] + gate_ref[...] * out_t.T
    if final:
        xn = xn * lax.rsqrt(jnp.mean(xn * xn, axis=-1, keepdims=True) + NORM_EPS) * fg_ref[...]
    o_ref[:, k, :] = xn


def _s5_out(g_cl, z_cl, wglu_t, wout_t, x3, gate_c, final_g, final):
    n_chunks = x3.shape[0]
    act = pl.BlockSpec((None, SSM_WIDTH, n_chunks), lambda t: (t, 0, 0))
    wspec = pl.BlockSpec((SSM_WIDTH, SSM_WIDTH), lambda t: (0, 0))
    row = pl.BlockSpec((n_chunks, SUBLANES, D_MODEL), lambda t: (0, t // SUBLANES, 0))
    return pl.pallas_call(
        functools.partial(_s5_out_kernel, final=final),
        out_shape=jax.ShapeDtypeStruct(x3.shape, F32),
        grid=(CHUNK,),
        in_specs=[act, act, wspec, wspec, row,
                  pl.BlockSpec((n_chunks, D_MODEL), lambda t: (0, 0)),
                  pl.BlockSpec((1, D_MODEL), lambda t: (0, 0))],
        out_specs=row,
        compiler_params=pltpu.CompilerParams(
            dimension_semantics=("arbitrary",), vmem_limit_bytes=VMEM_LIMIT),
        name="s5_out",
    )(g_cl, z_cl, wglu_t, wout_t, x3, gate_c, final_g)


def _rope_tables(s):
    inv_freq = ROPE_THETA ** (-jnp.arange(0, HEAD_DIM, 2, dtype=F32) / HEAD_DIM)
    ang = jnp.arange(s, dtype=F32)[:, None] * inv_freq[None, :]
    reps = LANES // (HEAD_DIM // 2)
    sign = jnp.where(jnp.arange(LANES) < LANES // 2, -1.0, 1.0).astype(F32)
    cos, sin = jnp.tile(jnp.cos(ang), (1, reps)), jnp.tile(jnp.sin(ang), (1, reps)) * sign
    stream_major = lambda a: a.reshape(s // STREAMS, STREAMS, LANES).transpose(1, 0, 2)
    return cos, sin, stream_major(cos), stream_major(sin)


def _qk_column_order():
    n = np.arange(ATTN_WIDTH)
    blk, lane = n // LANES, n % LANES
    half_w = HEAD_DIM // 2
    head = 2 * blk + (lane // half_w) % 2
    half = lane // HEAD_DIM
    return head * HEAD_DIM + half * half_w + lane % half_w


def _prep_attn_w_in(w):
    order = _qk_column_order()
    cols = []
    for g in range(N_DIL):
        base = g * 3 * ATTN_WIDTH
        cols += [base + order, base + ATTN_WIDTH + order, base + 2 * ATTN_WIDTH + np.arange(ATTN_WIDTH)]
    cols.append(3 * N_DIL * ATTN_WIDTH + np.arange(ATTN_WIDTH))
    return w[:, np.concatenate(cols)].astype(BF16)


def _trunk(x, ada, norm_g, attn_w, s5_w, s5_ops, final_norm_g):
    bsz, s, _ = x.shape
    chunks_per_seq = s // CHUNK
    n_chunks = bsz * chunks_per_seq
    tables = _rope_tables(s)
    fg = final_norm_g.reshape(1, D_MODEL)
    for i in range(DEPTH):
        mod = ada[i].reshape(bsz, 1, 3 * D_MODEL)
        g = norm_g[i].reshape(1, D_MODEL)
        j = i // N_MIXERS
        if i % N_MIXERS == 0:
            w_in, w_out = attn_w[j]
            qkv0, qkv12, z = _attn_in(x, mod, g, w_in, tables)
            x = _attn_out(_attention(qkv0, qkv12), z, x, mod, w_out)
        else:
            w_in_t, wglu_t, wout_t, d_col = s5_w[j]
            toep, mb, mc, al = s5_ops[j]
            per_chunk = lambda v: jnp.repeat(v, chunks_per_seq, axis=0)
            shift_c = per_chunk(ada[i][:, :D_MODEL])
            scale_c = per_chunk(ada[i][:, D_MODEL:2 * D_MODEL])
            gate_c = per_chunk(ada[i][:, 2 * D_MODEL:])
            x3 = x.reshape(n_chunks, CHUNK, D_MODEL)
            u_cl, z_cl = _s5_in(x3, shift_c, scale_c, g, w_in_t)
            g_cl = _s5_core(u_cl, toep, mb, mc, al, d_col, chunks_per_seq)
            x3 = _s5_out(g_cl, z_cl, wglu_t, wout_t, x3, gate_c, fg, final=(i == DEPTH - 1))
            x = x3.reshape(bsz, s, D_MODEL)
    return x


def kernel(x_prompt, x_sample, c_prompt, c_sample, norm_g, ada_w, ada_b, attn_w_in, attn_w_out,
           ssm_w_in, ssm_lam_re, ssm_lam_im, ssm_log_dt, ssm_b_re, ssm_b_im, ssm_c_re, ssm_c_im,
           ssm_d, ssm_w_glu, ssm_w_out, final_norm_g):
    assert (DEPTH - 1) % N_MIXERS == 1, "the final norm is fused into the last S5 layer"
    assert all(w // (2 * d) == RADIUS for w, d in DILATED_PAIRS)
    assert [d for _, d in DILATED_PAIRS] == [1, 4, STREAMS]
    n_prompt = c_prompt.shape[0]
    ada = _ada(jnp.concatenate([c_prompt, c_sample], axis=0), ada_w, ada_b)
    attn_w = [(_prep_attn_w_in(attn_w_in[j]), attn_w_out[j].astype(BF16))
              for j in range(attn_w_in.shape[0])]
    s5_w, s5_ops = [], []
    for j in range(ssm_w_in.shape[0]):
        d_col = jnp.tile(ssm_d[j].reshape(SSM_GROUPS, 1, GROUP_CH), (1, CHUNK, 1))
        s5_w.append((ssm_w_in[j].T.astype(BF16), ssm_w_glu[j].T.astype(BF16),
                     ssm_w_out[j].T.astype(BF16), d_col.reshape(SSM_GROUPS, CHUNK_ROWS, 1)))
        s5_ops.append(_s5_prep(ssm_lam_re[j], ssm_lam_im[j], ssm_log_dt[j], ssm_b_re[j],
                               ssm_b_im[j], ssm_c_re[j], ssm_c_im[j]))
    y_prompt = _trunk(x_prompt, ada[:, :n_prompt], norm_g, attn_w, s5_w, s5_ops, final_norm_g)
    y_sample = _trunk(x_sample, ada[:, n_prompt:], norm_g, attn_w, s5_w, s5_ops, final_norm_g)
    return (y_prompt, y_sample)
```

```python
import functools
import math

import numpy as np
import jax
import jax.numpy as jnp
from jax import lax
from jax.experimental import pallas as pl
from jax.experimental.pallas import tpu as pltpu

D_MODEL = 1024
DEPTH = 4
N_MIXERS = 2
ATTN_HEADS = 16
HEAD_DIM = 64
ATTN_WIDTH = ATTN_HEADS * HEAD_DIM
DILATED_PAIRS = ((128, 1), (512, 4), (2048, 16))
N_DIL = len(DILATED_PAIRS)
ROPE_THETA = 10000.0
SSM_WIDTH = D_MODEL
GROUP_CH = 16
SSM_GROUPS = SSM_WIDTH // GROUP_CH
STATE = 64
LAMBDA_RE_MAX = -1e-4
NORM_EPS = 1e-6
NEG_INF = -1e30

LANES = 128
SUBLANES = 8
CHUNK = 64
CHUNK_ROWS = CHUNK * GROUP_CH
T_PER_VREG = LANES // GROUP_CH
V_LANES = 2 * CHUNK_ROWS + LANES
STREAMS = 16
RADIUS = 64
ATTN_TQ = 128
SLAB = 256
SLAB_CHUNKS = SLAB // SUBLANES
VMEM_LIMIT = 48 * 1024 * 1024
VMEM_LIMIT_BIG = 56 * 1024 * 1024

F32 = jnp.float32
BF16 = jnp.bfloat16
HI = lax.Precision.HIGHEST


def _silu(v):
    return v * (1.0 / (1.0 + jnp.exp(-v)))


def _rms_modulate(x, g, scale, shift):
    rs = lax.rsqrt(jnp.mean(x * x, axis=-1, keepdims=True) + NORM_EPS)
    return (x * rs * g) * (1.0 + scale) + shift


def _ada_kernel(c_ref, w_ref, b_ref, o_ref):
    o_ref[...] = jnp.dot(_silu(c_ref[...]), w_ref[...], precision=HI,
                         preferred_element_type=F32) + b_ref[...]


def _ada(c, ada_w, ada_b):
    nb = c.shape[0]
    return pl.pallas_call(
        _ada_kernel,
        out_shape=jax.ShapeDtypeStruct((DEPTH, nb, 3 * D_MODEL), F32),
        grid=(DEPTH, 3),
        in_specs=[
            pl.BlockSpec((nb, D_MODEL), lambda i, j: (0, 0)),
            pl.BlockSpec((None, D_MODEL, D_MODEL), lambda i, j: (i, 0, j)),
            pl.BlockSpec((None, 1, D_MODEL), lambda i, j: (i, 0, j)),
        ],
        out_specs=pl.BlockSpec((None, nb, D_MODEL), lambda i, j: (i, 0, j)),
        name="ada",
    )(c, ada_w, ada_b.reshape(DEPTH, 1, 3 * D_MODEL))


def _attn_in_kernel(x_ref, shift_ref, scale_ref, g_ref, w_ref, cos_ref, sin_ref, cosp_ref, sinp_ref,
                    perm_ref, qkv0_ref, qkv12_ref, z_ref, h_nat, h_perm):
    j = pl.program_id(2)
    tm = x_ref.shape[0]
    rows = tm // STREAMS

    @pl.when(j == 0)
    def _():
        h = _rms_modulate(x_ref[...], g_ref[...], scale_ref[...], shift_ref[...])
        h_nat[...] = h.astype(BF16)
        h_perm[...] = jnp.dot(perm_ref[...], h_nat[...], preferred_element_type=F32).astype(BF16)

    part = j % 3

    def rope_blocks(acc, cos, sin):
        qscale = jnp.where(part == 0, HEAD_DIM ** -0.5, 1.0).astype(F32)
        cos = jnp.where(part == 2, 1.0, cos * qscale)
        sin = jnp.where(part == 2, 0.0, sin * qscale)
        for b in range(ATTN_WIDTH // LANES):
            sl = slice(b * LANES, (b + 1) * LANES)
            t = acc[:, sl]
            yield sl, (t * cos + pltpu.roll(t, LANES // 2, 1) * sin).astype(BF16)

    @pl.when(j < 3)
    def _():
        acc = jnp.dot(h_nat[...], w_ref[...], preferred_element_type=F32)
        for sl, blk in rope_blocks(acc, cos_ref[...], sin_ref[...]):
            qkv0_ref[:, sl] = blk

    @pl.when(jnp.logical_and(j >= 3, j < 3 * N_DIL))
    def _():
        acc = jnp.dot(h_perm[...], w_ref[...], preferred_element_type=F32)
        cosp = cosp_ref[...].reshape(tm, LANES)
        sinp = sinp_ref[...].reshape(tm, LANES)
        for sl, blk in rope_blocks(acc, cosp, sinp):
            qkv12_ref[:, :, sl] = blk.reshape(STREAMS, rows, LANES)

    @pl.when(j == 3 * N_DIL)
    def _():
        z_ref[...] = jnp.dot(h_nat[...], w_ref[...], preferred_element_type=F32).astype(BF16)


def _attn_in(x, mod, g, w, tables, tm=512):
    bsz, s, _ = x.shape
    cos, sin, cosp, sinp = tables
    rows = tm // STREAMS
    n_col = 3 * N_DIL + 1
    dst = np.arange(tm)
    perm = np.zeros((tm, tm), np.float32)
    perm[dst, STREAMS * (dst % rows) + dst // rows] = 1.0
    perm = jnp.asarray(perm, BF16)
    return pl.pallas_call(
        _attn_in_kernel,
        out_shape=(jax.ShapeDtypeStruct((bsz, s, 3 * ATTN_WIDTH), BF16),
                   jax.ShapeDtypeStruct((bsz, STREAMS, s // STREAMS, 6 * ATTN_WIDTH), BF16),
                   jax.ShapeDtypeStruct((bsz, s, ATTN_WIDTH), BF16)),
        grid=(bsz, s // tm, n_col),
        in_specs=[
            pl.BlockSpec((None, tm, D_MODEL), lambda b, i, j: (b, i, 0)),
            pl.BlockSpec((None, 1, D_MODEL), lambda b, i, j: (b, 0, 0)),
            pl.BlockSpec((None, 1, D_MODEL), lambda b, i, j: (b, 0, 1)),
            pl.BlockSpec((1, D_MODEL), lambda b, i, j: (0, 0)),
            pl.BlockSpec((D_MODEL, ATTN_WIDTH), lambda b, i, j: (0, j)),
            pl.BlockSpec((tm, LANES), lambda b, i, j: (i, 0)),
            pl.BlockSpec((tm, LANES), lambda b, i, j: (i, 0)),
            pl.BlockSpec((STREAMS, rows, LANES), lambda b, i, j: (0, i, 0)),
            pl.BlockSpec((STREAMS, rows, LANES), lambda b, i, j: (0, i, 0)),
            pl.BlockSpec((tm, tm), lambda b, i, j: (0, 0)),
        ],
        out_specs=(
            pl.BlockSpec((None, tm, ATTN_WIDTH), lambda b, i, j: (b, i, jnp.clip(j, 0, 2))),
            pl.BlockSpec((None, STREAMS, rows, ATTN_WIDTH),
                         lambda b, i, j: (b, 0, i, jnp.clip(j - 3, 0, 5))),
            pl.BlockSpec((None, tm, ATTN_WIDTH), lambda b, i, j: (b, i, 0)),
        ),
        scratch_shapes=[pltpu.VMEM((tm, D_MODEL), BF16), pltpu.VMEM((tm, D_MODEL), BF16)],
        compiler_params=pltpu.CompilerParams(
            dimension_semantics=("parallel", "parallel", "arbitrary"),
            vmem_limit_bytes=VMEM_LIMIT),
        name="attn_in",
    )(x, mod, mod, g, w, cos, sin, cosp, sinp, perm)


def _attn_pairs(get_q, get_k, get_v, valid, put_acc):
    tq = valid.shape[0]
    lane = lax.broadcasted_iota(jnp.int32, (tq, LANES), 1)
    q_is_odd = (lane // (HEAD_DIM // 2)) % 2 == 1
    v_is_odd = lane >= HEAD_DIM
    mx_tile = jnp.zeros((tq, LANES), F32)
    den_tile = jnp.ones((tq, LANES), F32)
    for b in range(ATTN_WIDTH // LANES):
        sl = slice(b * LANES, (b + 1) * LANES)
        qb = get_q(sl).astype(F32)
        kb = get_k(sl)
        vb = get_v(sl)
        accs = []
        for odd in (0, 1):
            qm = (jnp.where(q_is_odd, qb, 0.0) if odd else jnp.where(q_is_odd, 0.0, qb)).astype(BF16)
            sc = lax.dot_general(qm, kb, (((1,), (1,)), ((), ())), preferred_element_type=F32)
            sc = jnp.where(valid, sc, NEG_INF)
            mx = jnp.max(sc, axis=1, keepdims=True)
            p = jnp.exp(sc - mx)
            den = jnp.sum(p, axis=1, keepdims=True)
            accs.append(jnp.dot(p.astype(BF16), vb, preferred_element_type=F32))
            mx_tile = jnp.where(lane == 2 * b + odd, mx, mx_tile)
            den_tile = jnp.where(lane == 2 * b + odd, den, den_tile)
        put_acc(sl, jnp.where(v_is_odd, accs[1], accs[0]))
    return mx_tile, den_tile


def _pair_factor(tile, b):
    lane = lax.broadcasted_iota(jnp.int32, tile.shape, 1)
    return jnp.where(lane >= HEAD_DIM, tile[:, 2 * b + 1:2 * b + 2], tile[:, 2 * b:2 * b + 1])


def _merge_groups(mx, den, pmx, pden, get_acc, get_prev, put_o, normalise):
    top = jnp.maximum(mx, pmx)
    wc, wp = jnp.exp(mx - top), jnp.exp(pmx - top)
    new_den = wc * den + wp * pden
    if normalise:
        inv = 1.0 / new_den
        wc, wp = wc * inv, wp * inv
    for b in range(ATTN_WIDTH // LANES):
        sl = slice(b * LANES, (b + 1) * LANES)
        put_o(sl, _pair_factor(wc, b) * get_acc(sl) + _pair_factor(wp, b) * get_prev(sl))
    return top, new_den


def _window_mask(i, tq, m):
    nk = tq + 2 * RADIUS
    row = lax.broadcasted_iota(jnp.int32, (tq, nk), 0)
    col = lax.broadcasted_iota(jnp.int32, (tq, nk), 1)
    kabs = i * tq - RADIUS + col
    return (jnp.abs(col - RADIUS - row) <= RADIUS) & (kabs >= 0) & (kabs < m)


def _cat3(p_ref, c_ref, n_ref):
    return lambda sl: jnp.concatenate([p_ref[:, sl], c_ref[:, sl], n_ref[:, sl]], axis=0)


def _attn16_kernel(q_ref, kp_ref, kc_ref, kn_ref, vp_ref, vc_ref, vn_ref, o_ref, mx_ref, den_ref, *, m):
    valid = _window_mask(pl.program_id(2), q_ref.shape[0], m)

    def put_acc(sl, val):
        o_ref[:, sl] = val

    mx_ref[...], den_ref[...] = _attn_pairs(lambda sl: q_ref[:, sl], _cat3(kp_ref, kc_ref, kn_ref),
                                            _cat3(vp_ref, vc_ref, vn_ref), valid, put_acc)


def _attn4_kernel(q_ref, kp_ref, kc_ref, kn_ref, vp_ref, vc_ref, vn_ref, po_ref, pmx_ref, pden_ref,
                  o_ref, mx_ref, den_ref, *, m):
    i = pl.program_id(2)
    ns, rows = q_ref.shape[0], q_ref.shape[1]
    halo = kp_ref.shape[1]
    tq, nk = ns * rows, ns * (rows + 2 * halo)
    n = lax.broadcasted_iota(jnp.int32, (tq, nk), 0)
    c = lax.broadcasted_iota(jnp.int32, (tq, nk), 1)
    qpos = ns * (rows * i + n % rows) + n // rows
    is_prev, is_next = c < ns * halo, c >= ns * (halo + rows)
    edge = is_prev | is_next
    cc = jnp.where(is_prev, c, jnp.where(is_next, c - ns * (halo + rows), c - ns * halo))
    krow = rows * i + jnp.where(is_prev, -halo, jnp.where(is_next, rows, 0)) + jnp.where(edge, cc % halo, cc % rows)
    kpos = ns * krow + jnp.where(edge, cc // halo, cc // rows)
    valid = (jnp.abs(kpos - qpos) <= RADIUS) & (krow >= 0) & (krow < m)

    def cat(p_ref, c_ref, n_ref):
        return lambda sl: jnp.concatenate(
            [p_ref[:, :, sl].reshape(ns * halo, LANES), c_ref[:, :, sl].reshape(tq, LANES),
             n_ref[:, :, sl].reshape(ns * halo, LANES)], axis=0)

    def put_o(sl, val):
        o_ref[:, :, sl] = val.reshape(ns, rows, LANES)

    flat = lambda ref: (lambda sl: ref[:, :, sl].reshape(tq, LANES))
    mx, den = _attn_pairs(flat(q_ref), cat(kp_ref, kc_ref, kn_ref), cat(vp_ref, vc_ref, vn_ref),
                          valid, put_o)
    mx, den = _merge_groups(mx, den, pmx_ref[...].reshape(tq, LANES), pden_ref[...].reshape(tq, LANES),
                            flat(o_ref), flat(po_ref), put_o, normalise=False)
    mx_ref[...] = mx.reshape(ns, rows, LANES)
    den_ref[...] = den.reshape(ns, rows, LANES)


def _attn1_kernel(q_ref, kp_ref, kc_ref, kn_ref, vp_ref, vc_ref, vn_ref, po_ref, pmx_ref, pden_ref,
                  o_ref, acc_scr, *, m):
    valid = _window_mask(pl.program_id(1), q_ref.shape[0], m)
    rows = po_ref.shape[1]

    def natural(ref, sl):
        return jnp.concatenate([ref[:, ii, sl] for ii in range(rows)], axis=0)

    def put_acc(sl, val):
        acc_scr[:, sl] = val

    def put_o(sl, val):
        o_ref[:, sl] = val.astype(BF16)

    mx, den = _attn_pairs(lambda sl: q_ref[:, sl], _cat3(kp_ref, kc_ref, kn_ref),
                          _cat3(vp_ref, vc_ref, vn_ref), valid, put_acc)
    stats = slice(0, LANES)
    _merge_groups(mx, den, natural(pmx_ref, stats), natural(pden_ref, stats),
                  lambda sl: acc_scr[:, sl], lambda sl: natural(po_ref, sl), put_o, normalise=True)


def _attention(qkv0, qkv12):
    bsz, s, _ = qkv0.shape
    m16 = s // STREAMS
    tq, half = ATTN_TQ, RADIUS
    params = lambda n: pltpu.CompilerParams(dimension_semantics=("parallel",) * n,
                                            vmem_limit_bytes=VMEM_LIMIT)

    def spec16(nrows, row_idx, col, width=ATTN_WIDTH):
        return pl.BlockSpec((None, None, nrows, width), lambda b, r, i: (b, r, row_idx(i), col))
    per = tq // half
    cur = lambda i: i
    prv = lambda i: jnp.maximum(i * per - 1, 0)
    nxt = lambda i: jnp.minimum((i + 1) * per, m16 // half - 1)
    stat16 = jax.ShapeDtypeStruct((bsz, STREAMS, m16, LANES), F32)
    o2, mx2, den2 = pl.pallas_call(
        functools.partial(_attn16_kernel, m=m16),
        out_shape=(jax.ShapeDtypeStruct((bsz, STREAMS, m16, ATTN_WIDTH), F32), stat16, stat16),
        grid=(bsz, STREAMS, m16 // tq),
        in_specs=[spec16(tq, cur, 3), spec16(half, prv, 4), spec16(tq, cur, 4), spec16(half, nxt, 4),
                  spec16(half, prv, 5), spec16(tq, cur, 5), spec16(half, nxt, 5)],
        out_specs=(spec16(tq, cur, 0), spec16(tq, cur, 0, LANES), spec16(tq, cur, 0, LANES)),
        compiler_params=params(3), name="attn_dil16",
    )(*([qkv12] * 7))

    ns = STREAMS // 4
    rows, halo = tq // ns, half // ns

    def view4(a):
        return a.reshape(bsz, ns, 4, m16, a.shape[-1])

    def spec4(nrows, row_idx, col, width=ATTN_WIDTH):
        return pl.BlockSpec((None, ns, None, nrows, width), lambda b, r, i: (b, 0, r, row_idx(i), col))
    per4 = rows // halo
    prv4 = lambda i: jnp.maximum(i * per4 - 1, 0)
    nxt4 = lambda i: jnp.minimum((i + 1) * per4, m16 // halo - 1)
    q4 = view4(qkv12)
    stat4 = jax.ShapeDtypeStruct((bsz, ns, 4, m16, LANES), F32)
    o12, mx12, den12 = pl.pallas_call(
        functools.partial(_attn4_kernel, m=m16),
        out_shape=(jax.ShapeDtypeStruct((bsz, ns, 4, m16, ATTN_WIDTH), F32), stat4, stat4),
        grid=(bsz, 4, m16 // rows),
        in_specs=[spec4(rows, cur, 0), spec4(halo, prv4, 1), spec4(rows, cur, 1), spec4(halo, nxt4, 1),
                  spec4(halo, prv4, 2), spec4(rows, cur, 2), spec4(halo, nxt4, 2),
                  spec4(rows, cur, 0), spec4(rows, cur, 0, LANES), spec4(rows, cur, 0, LANES)],
        out_specs=(spec4(rows, cur, 0), spec4(rows, cur, 0, LANES), spec4(rows, cur, 0, LANES)),
        compiler_params=params(3), name="attn_dil4",
    )(q4, q4, q4, q4, q4, q4, q4, view4(o2), view4(mx2), view4(den2))
    streams = lambda a: a.reshape(bsz, STREAMS, m16, a.shape[-1])

    def spec1(nrows, row_idx, col):
        return pl.BlockSpec((None, nrows, ATTN_WIDTH), lambda b, i: (b, row_idx(i), col))
    nxt1 = lambda i: jnp.minimum((i + 1) * per, s // half - 1)
    srow = tq // STREAMS
    sm = lambda width: pl.BlockSpec((None, STREAMS, srow, width), lambda b, i: (b, 0, i, 0))
    return pl.pallas_call(
        functools.partial(_attn1_kernel, m=s),
        out_shape=jax.ShapeDtypeStruct((bsz, s, ATTN_WIDTH), BF16),
        grid=(bsz, s // tq),
        in_specs=[spec1(tq, cur, 0), spec1(half, prv, 1), spec1(tq, cur, 1), spec1(half, nxt1, 1),
                  spec1(half, prv, 2), spec1(tq, cur, 2), spec1(half, nxt1, 2),
                  sm(ATTN_WIDTH), sm(LANES), sm(LANES)],
        out_specs=pl.BlockSpec((None, tq, ATTN_WIDTH), lambda b, i: (b, i, 0)),
        scratch_shapes=[pltpu.VMEM((tq, ATTN_WIDTH), F32)],
        compiler_params=params(2), name="attn_dil1",
    )(*([qkv0] * 7), streams(o12), streams(mx12), streams(den12))


def _attn_out_kernel(o_ref, z_ref, x_ref, gate_ref, w_ref, out_ref):
    y = (o_ref[...].astype(F32) * _silu(z_ref[...].astype(F32))).astype(BF16)
    out = jnp.dot(y, w_ref[...], preferred_element_type=F32)
    out_ref[...] = x_ref[...] + gate_ref[...] * out


def _attn_out(o, z, x, mod, w, tm=512):
    bsz, s, _ = x.shape
    row = pl.BlockSpec((None, tm, D_MODEL), lambda b, i: (b, i, 0))
    return pl.pallas_call(
        _attn_out_kernel,
        out_shape=jax.ShapeDtypeStruct(x.shape, F32),
        grid=(bsz, s // tm),
        in_specs=[row, row, row,
                  pl.BlockSpec((None, 1, D_MODEL), lambda b, i: (b, 0, 2)),
                  pl.BlockSpec((ATTN_WIDTH, D_MODEL), lambda b, i: (0, 0))],
        out_specs=row,
        compiler_params=pltpu.CompilerParams(
            dimension_semantics=("parallel", "parallel"),
            vmem_limit_bytes=VMEM_LIMIT),
        name="attn_out",
    )(o, z, x, mod, w)


def _cmul(ar, ai, br, bi):
    return ar * br - ai * bi, ar * bi + ai * br


def _s5_prep_kernel(lam_re_a, lam_im_a, lam_re_b, lam_im_b, logdt, bt_re, bt_im,
                    c_re, c_im, toep_ref, mb_ref, mc_ref, al_ref, v_scr):
    lane = lax.broadcasted_iota(jnp.int32, (STATE, LANES), 1)
    a_idx = (lane // GROUP_CH).astype(F32)
    n_tiles = CHUNK // T_PER_VREG

    mb_rows = []
    v_rhs = []
    v_lhs = []
    mc_cols = []
    zbar_f = None
    for dr in range(2):
        dt = jnp.exp(logdt[dr])
        lr = jnp.minimum(lam_re_a[dr], LAMBDA_RE_MAX)
        li = lam_im_a[dr]
        ldr, ldi = lr * dt, li * dt

        def powers(e):
            mag = jnp.exp(e * ldr)
            return mag * jnp.cos(e * ldi), mag * jnp.sin(e * ldi)

        l1r, l1i = powers(1.0)
        den = lr * lr + li * li
        nr, ni = l1r - 1.0, l1i
        cr, ci = (nr * lr + ni * li) / den, (ni * lr - nr * li) / den
        bbr, bbi = _cmul(cr, ci, bt_re[dr], bt_im[dr])
        l8r, l8i = powers(float(T_PER_VREG))
        tiles = [None] * n_tiles
        if dr == 0:
            pr, pi = powers(float(T_PER_VREG - 1) - a_idx)
            cur = _cmul(pr, pi, bbr, bbi)
            for jt in range(n_tiles - 1, -1, -1):
                tiles[jt] = cur
                if jt:
                    cur = _cmul(cur[0], cur[1], l8r, l8i)
            v_rhs.append(_cmul(tiles[n_tiles - 1][0], tiles[n_tiles - 1][1], l1r, l1i))
            zbar_f = (jnp.where(lane < GROUP_CH, bbr, 0.0), jnp.where(lane < GROUP_CH, bbi, 0.0))
        else:
            pr, pi = powers(a_idx)
            cur = _cmul(pr, pi, bbr, bbi)
            for jt in range(n_tiles):
                tiles[jt] = cur
                if jt < n_tiles - 1:
                    cur = _cmul(cur[0], cur[1], l8r, l8i)
            v_rhs.append(tiles[0])
        mb_rows.append(jnp.concatenate([t[0] for t in tiles], axis=1))
        mb_rows.append(jnp.concatenate([t[1] for t in tiles], axis=1))
        a_r, a_i = l8r, l8i
        for _ in range(int(math.log2(CHUNK // T_PER_VREG))):
            a_r, a_i = _cmul(a_r, a_i, a_r, a_i)
        al_ref[2 * dr] = jnp.broadcast_to(a_r, (STATE, LANES))
        al_ref[2 * dr + 1] = jnp.broadcast_to(a_i, (STATE, LANES))

        lrb = jnp.minimum(lam_re_b[dr], LAMBDA_RE_MAX)
        lib = lam_im_b[dr]
        mag = jnp.exp(lrb * dt)
        pw_r, pw_i = mag * jnp.cos(lib * dt), mag * jnp.sin(lib * dt)
        pows = [(pw_r, pw_i)]
        for _ in range(int(math.log2(CHUNK)) - 1):
            pows.append(_cmul(*pows[-1], *pows[-1]))
        cre, cim = c_re[dr], c_im[dr]
        xr, xi = _cmul(cre, cim, *pows[0])
        for k in range(int(math.log2(CHUNK))):
            yr, yi = _cmul(xr, xi, *pows[k])
            if dr == 0:
                xr, xi = jnp.concatenate([xr, yr], 0), jnp.concatenate([xi, yi], 0)
            else:
                xr, xi = jnp.concatenate([yr, xr], 0), jnp.concatenate([yi, xi], 0)
        mc_cols += [xr, -xi]
        xr, xi = cre, cim
        for k in range(int(math.log2(n_tiles))):
            yr, yi = _cmul(xr, xi, *pows[k + int(math.log2(T_PER_VREG))])
            if dr == 0:
                xr, xi = jnp.concatenate([yr, xr], 0), jnp.concatenate([yi, xi], 0)
            else:
                xr, xi = jnp.concatenate([xr, yr], 0), jnp.concatenate([xi, yi], 0)
        v_lhs.append((xr, xi))

    mb_ref[...] = jnp.concatenate(mb_rows, axis=0).astype(BF16)
    mc_ref[...] = jnp.concatenate(mc_cols, axis=1).astype(BF16)

    def cdot(lhs, rhs):
        return (jnp.dot(lhs[0], rhs[0], precision=HI, preferred_element_type=F32)
                - jnp.dot(lhs[1], rhs[1], precision=HI, preferred_element_type=F32))

    out_f = cdot(v_lhs[0], v_rhs[0])
    zero_rows = jnp.zeros((CHUNK_ROWS // T_PER_VREG - GROUP_CH, STATE), F32)
    lag0_lhs = (jnp.concatenate([c_re[0], zero_rows], 0), jnp.concatenate([c_im[0], zero_rows], 0))
    out_b = cdot(v_lhs[1], v_rhs[1]) + cdot(lag0_lhs, zbar_f)
    for jt in range(n_tiles):
        rows = slice(jt * GROUP_CH, (jt + 1) * GROUP_CH)
        v_scr[:, jt * LANES:(jt + 1) * LANES] = out_f[rows]
        v_scr[:, (n_tiles + jt) * LANES:(n_tiles + jt + 1) * LANES] = out_b[rows]
    v_scr[:, 2 * CHUNK_ROWS:] = jnp.zeros((GROUP_CH, LANES), F32)

    for k in range(T_PER_VREG):
        vk = v_scr[:, k * GROUP_CH:k * GROUP_CH + 2 * CHUNK_ROWS]
        for t_out in range(CHUNK):
            off = (CHUNK - t_out) * GROUP_CH
            if off % LANES == k * GROUP_CH:
                al_off = off - k * GROUP_CH
                toep_ref[t_out * GROUP_CH:(t_out + 1) * GROUP_CH, :] = (
                    vk[:, al_off:al_off + CHUNK_ROWS].astype(BF16))


def _s5_prep(lam_re, lam_im, log_dt, b_re, b_im, c_re, c_im):
    g = SSM_GROUPS
    lam_a = lambda a: a.reshape(2, g, STATE, 1)
    lam_b = lambda a: a.reshape(2, g, 1, STATE)
    tile_b = lambda a: jnp.tile(a, (1, 1, 1, T_PER_VREG))
    dir_spec = lambda r, c: pl.BlockSpec((2, None, r, c), lambda i: (0, i, 0, 0))
    return pl.pallas_call(
        _s5_prep_kernel,
        out_shape=(jax.ShapeDtypeStruct((g, CHUNK_ROWS, CHUNK_ROWS), BF16),
                   jax.ShapeDtypeStruct((g, 4 * STATE, CHUNK_ROWS), BF16),
                   jax.ShapeDtypeStruct((g, CHUNK_ROWS, 4 * STATE), BF16),
                   jax.ShapeDtypeStruct((g, 4, STATE, LANES), F32)),
        grid=(g,),
        in_specs=[dir_spec(STATE, 1), dir_spec(STATE, 1), dir_spec(1, STATE), dir_spec(1, STATE),
                  dir_spec(1, 1), dir_spec(STATE, LANES), dir_spec(STATE, LANES),
                  dir_spec(GROUP_CH, STATE), dir_spec(GROUP_CH, STATE)],
        out_specs=(pl.BlockSpec((None, CHUNK_ROWS, CHUNK_ROWS), lambda i: (i, 0, 0)),
                   pl.BlockSpec((None, 4 * STATE, CHUNK_ROWS), lambda i: (i, 0, 0)),
                   pl.BlockSpec((None, CHUNK_ROWS, 4 * STATE), lambda i: (i, 0, 0)),
                   pl.BlockSpec((None, 4, STATE, LANES), lambda i: (i, 0, 0, 0))),
        scratch_shapes=[pltpu.VMEM((GROUP_CH, V_LANES), F32)],
        compiler_params=pltpu.CompilerParams(dimension_semantics=("parallel",)),
        name="s5_prep",
    )(lam_a(lam_re), lam_a(lam_im), lam_b(lam_re), lam_b(lam_im), log_dt.reshape(2, g, 1, 1),
      tile_b(b_re), tile_b(b_im), c_re, c_im)


def _slab_perm():
    dst = np.arange(SLAB)
    perm = np.zeros((SLAB, SLAB), np.float32)
    perm[dst, (dst % SLAB_CHUNKS) * SUBLANES + dst // SLAB_CHUNKS] = 1.0
    return perm


def _s5_in_kernel(x_ref, mod_ref, g_ref, w_ref, p_ref, u_ref, z_ref, h_scr, *, chunks_per_seq):
    n_chunks = x_ref.shape[0]
    for b in range(n_chunks // chunks_per_seq):
        rows = slice(b * chunks_per_seq, (b + 1) * chunks_per_seq)
        h = _rms_modulate(x_ref[rows], g_ref[...], mod_ref[b:b + 1, D_MODEL:2 * D_MODEL],
                          mod_ref[b:b + 1, :D_MODEL])
        h2 = h.reshape(chunks_per_seq * SUBLANES, D_MODEL).astype(BF16)
        for s in range(chunks_per_seq // SLAB_CHUNKS):
            hp = jnp.dot(p_ref[...], h2[s * SLAB:(s + 1) * SLAB], preferred_element_type=F32)
            c0 = b * chunks_per_seq + s * SLAB_CHUNKS
            for k in range(SUBLANES):
                h_scr[k, c0:c0 + SLAB_CHUNKS, :] = hp[k * SLAB_CHUNKS:(k + 1) * SLAB_CHUNKS].astype(BF16)
    for k in range(SUBLANES):
        ht = h_scr[k].astype(F32).T.astype(BF16)
        uz = jnp.dot(w_ref[...], ht, preferred_element_type=F32)
        u_ref[k] = uz[:SSM_WIDTH].astype(BF16)
        z_ref[k] = uz[SSM_WIDTH:].astype(BF16)


def _s5_in(x3, mod, g, w_t, chunks_per_seq):
    n_chunks = x3.shape[0]
    once = lambda shape: pl.BlockSpec(shape, lambda t: (0,) * len(shape), pipeline_mode=pl.Buffered(1))
    out = pl.BlockSpec((SUBLANES, SSM_WIDTH, n_chunks), lambda t: (t, 0, 0))
    return pl.pallas_call(
        functools.partial(_s5_in_kernel, chunks_per_seq=chunks_per_seq),
        out_shape=(jax.ShapeDtypeStruct((CHUNK, SSM_WIDTH, n_chunks), BF16),) * 2,
        grid=(CHUNK // SUBLANES,),
        in_specs=[pl.BlockSpec((n_chunks, SUBLANES, D_MODEL), lambda t: (0, t, 0)),
                  once(mod.shape), once((1, D_MODEL)), once((2 * SSM_WIDTH, D_MODEL)),
                  once((SLAB, SLAB))],
        out_specs=(out, out),
        scratch_shapes=[pltpu.VMEM((SUBLANES, n_chunks, D_MODEL), BF16)],
        compiler_params=pltpu.CompilerParams(
            dimension_semantics=("parallel",), vmem_limit_bytes=VMEM_LIMIT_BIG),
        name="s5_in",
    )(x3, mod, g, w_t, jnp.asarray(_slab_perm(), BF16))


def _gelu_tanh(v):
    return 0.5 * v * (1.0 + jnp.tanh(math.sqrt(2.0 / math.pi) * (v + 0.044715 * (v * v * v))))


def _s5_core_kernel(u_ref, toep_ref, mb_ref, mc_ref, al_ref, d_ref, o_ref, *, chunks_per_seq):
    n_chunks = u_ref.shape[-1]
    ub = u_ref[...].reshape(CHUNK_ROWS, n_chunks)
    inc = jnp.dot(mb_ref[...], ub, preferred_element_type=F32)
    pos = lax.broadcasted_iota(jnp.int32, (STATE, n_chunks), 1) % chunks_per_seq
    reps = n_chunks // LANES

    def lane_tile(a):
        return jnp.concatenate([a] * reps, axis=1) if reps > 1 else a

    states = []
    for dr in range(2):
        xr = inc[(2 * dr) * STATE:(2 * dr + 1) * STATE]
        xi = inc[(2 * dr + 1) * STATE:(2 * dr + 2) * STATE]
        ar, ai = lane_tile(al_ref[2 * dr]), lane_tile(al_ref[2 * dr + 1])

        def shifted(v, step):
            if dr == 0:
                return jnp.where(pos >= step, pltpu.roll(v, step, 1), 0.0)
            return jnp.where(pos < chunks_per_seq - step, pltpu.roll(v, n_chunks - step, 1), 0.0)

        step = 1
        while step < chunks_per_seq:
            sr, si = shifted(xr, step), shifted(xi, step)
            xr, xi = xr + ar * sr - ai * si, xi + ar * si + ai * sr
            ar, ai = _cmul(ar, ai, ar, ai)
            step *= 2
        states += [shifted(xr, 1), shifted(xi, 1)]
    h_in = jnp.concatenate(states, axis=0).astype(BF16)
    y = (jnp.dot(toep_ref[...], ub, preferred_element_type=F32)
         + jnp.dot(mc_ref[...], h_in, preferred_element_type=F32)
         + d_ref[...] * ub.astype(F32))
    o_ref[...] = _gelu_tanh(y).astype(BF16).reshape(CHUNK, GROUP_CH, n_chunks)


def _s5_core(u_cl, toep, mb, mc, al, d_col, chunks_per_seq):
    n_chunks = u_cl.shape[-1]
    u4 = u_cl.reshape(CHUNK, SSM_GROUPS, GROUP_CH, n_chunks)
    grp = pl.BlockSpec((CHUNK, None, GROUP_CH, n_chunks), lambda g: (0, g, 0, 0))
    out = pl.pallas_call(
        functools.partial(_s5_core_kernel, chunks_per_seq=chunks_per_seq),
        out_shape=jax.ShapeDtypeStruct(u4.shape, BF16),
        grid=(SSM_GROUPS,),
        in_specs=[grp,
                  pl.BlockSpec((None, CHUNK_ROWS, CHUNK_ROWS), lambda g: (g, 0, 0)),
                  pl.BlockSpec((None, 4 * STATE, CHUNK_ROWS), lambda g: (g, 0, 0)),
                  pl.BlockSpec((None, CHUNK_ROWS, 4 * STATE), lambda g: (g, 0, 0)),
                  pl.BlockSpec((None, 4, STATE, LANES), lambda g: (g, 0, 0, 0)),
                  pl.BlockSpec((None, CHUNK_ROWS, 1), lambda g: (g, 0, 0))],
        out_specs=grp,
        compiler_params=pltpu.CompilerParams(
            dimension_semantics=("parallel",), vmem_limit_bytes=VMEM_LIMIT),
        name="s5_core",
    )(u4, toep, mb, mc, al, d_col)
    return out.reshape(CHUNK, SSM_WIDTH, n_chunks)


def _s5_out_kernel(g_ref, z_ref, wglu_ref, wout_ref, pt_ref, x_ref, mod_ref, fg_ref, o_ref, y_scr,
                   *, final, chunks_per_seq):
    half = pl.program_id(1)
    n_half = x_ref.shape[0]

    @pl.when(half == 0)
    def _():
        for k in range(SUBLANES):
            gb = g_ref[k]
            glu = jnp.dot(wglu_ref[...], gb, preferred_element_type=F32)
            y = gb.astype(F32) * (1.0 / (1.0 + jnp.exp(-glu))) * _silu(z_ref[k].astype(F32))
            y_scr[k] = y.T.astype(BF16)

    for hv in range(y_scr.shape[1] // n_half):
        @pl.when(half == hv)
        def _():
            for s in range(n_half // SLAB_CHUNKS):
                c0 = hv * n_half + s * SLAB_CHUNKS
                src = jnp.concatenate([y_scr[k, c0:c0 + SLAB_CHUNKS, :] for k in range(SUBLANES)], axis=0)
                yp = jnp.dot(pt_ref[...], src, preferred_element_type=F32).astype(BF16)
                out = jnp.dot(yp, wout_ref[...], preferred_element_type=F32)
                b = c0 // chunks_per_seq
                upd = mod_ref[b:b + 1, 2 * D_MODEL:] * out
                rows = slice(s * SLAB_CHUNKS, (s + 1) * SLAB_CHUNKS)
                xn = x_ref[rows] + upd.reshape(SLAB_CHUNKS, SUBLANES, D_MODEL)
                if final:
                    xn = xn * lax.rsqrt(jnp.mean(xn * xn, axis=-1, keepdims=True) + NORM_EPS) * fg_ref[...]
                o_ref[rows] = xn


def _s5_out(g_cl, z_cl, wglu_t, w_out, x3, mod, final_g, final, chunks_per_seq, n_split=2):
    n_chunks = x3.shape[0]
    n_half = n_chunks // n_split
    once = lambda shape: pl.BlockSpec(shape, lambda t, h: (0,) * len(shape), pipeline_mode=pl.Buffered(1))
    act = pl.BlockSpec((SUBLANES, SSM_WIDTH, n_chunks), lambda t, h: (t, 0, 0))
    row = pl.BlockSpec((n_half, SUBLANES, D_MODEL), lambda t, h: (h, t, 0))
    return pl.pallas_call(
        functools.partial(_s5_out_kernel, final=final, chunks_per_seq=chunks_per_seq),
        out_shape=jax.ShapeDtypeStruct(x3.shape, F32),
        grid=(CHUNK // SUBLANES, n_split),
        in_specs=[act, act, once((SSM_WIDTH, SSM_WIDTH)), once((SSM_WIDTH, D_MODEL)), once((SLAB, SLAB)),
                  row, once(mod.shape), once((1, D_MODEL))],
        out_specs=row,
        scratch_shapes=[pltpu.VMEM((SUBLANES, n_chunks, SSM_WIDTH), BF16)],
        compiler_params=pltpu.CompilerParams(
            dimension_semantics=("parallel", "arbitrary"), vmem_limit_bytes=VMEM_LIMIT_BIG),
        name="s5_out",
    )(g_cl, z_cl, wglu_t, w_out, jnp.asarray(_slab_perm().T, BF16), x3, mod, final_g)


def _rope_tables(s):
    inv_freq = ROPE_THETA ** (-jnp.arange(0, HEAD_DIM, 2, dtype=F32) / HEAD_DIM)
    ang = jnp.arange(s, dtype=F32)[:, None] * inv_freq[None, :]
    reps = LANES // (HEAD_DIM // 2)
    sign = jnp.where(jnp.arange(LANES) < LANES // 2, -1.0, 1.0).astype(F32)
    cos, sin = jnp.tile(jnp.cos(ang), (1, reps)), jnp.tile(jnp.sin(ang), (1, reps)) * sign
    stream_major = lambda a: a.reshape(s // STREAMS, STREAMS, LANES).transpose(1, 0, 2)
    return cos, sin, stream_major(cos), stream_major(sin)


def _qk_column_order():
    n = np.arange(ATTN_WIDTH)
    blk, lane = n // LANES, n % LANES
    half_w = HEAD_DIM // 2
    head = 2 * blk + (lane // half_w) % 2
    half = lane // HEAD_DIM
    return head * HEAD_DIM + half * half_w + lane % half_w


def _prep_attn_w_in(w):
    order = _qk_column_order()
    cols = []
    for g in range(N_DIL):
        base = g * 3 * ATTN_WIDTH
        cols += [base + order, base + ATTN_WIDTH + order, base + 2 * ATTN_WIDTH + np.arange(ATTN_WIDTH)]
    cols.append(3 * N_DIL * ATTN_WIDTH + np.arange(ATTN_WIDTH))
    return w[:, np.concatenate(cols)].astype(BF16)


def _trunk(x, ada, norm_g, attn_w, s5_w, s5_ops, final_norm_g):
    bsz, s, _ = x.shape
    chunks_per_seq = s // CHUNK
    n_chunks = bsz * chunks_per_seq
    tables = _rope_tables(s)
    fg = final_norm_g.reshape(1, D_MODEL)
    for i in range(DEPTH):
        mod = ada[i].reshape(bsz, 1, 3 * D_MODEL)
        g = norm_g[i].reshape(1, D_MODEL)
        j = i // N_MIXERS
        if i % N_MIXERS == 0:
            w_in, w_out = attn_w[j]
            qkv0, qkv12, z = _attn_in(x, mod, g, w_in, tables)
            x = _attn_out(_attention(qkv0, qkv12), z, x, mod, w_out)
        else:
            w_in_t, wglu_t, w_out, d_col = s5_w[j]
            toep, mb, mc, al = s5_ops[j]
            x3 = x.reshape(n_chunks, CHUNK, D_MODEL)
            u_cl, z_cl = _s5_in(x3, ada[i], g, w_in_t, chunks_per_seq)
            g_cl = _s5_core(u_cl, toep, mb, mc, al, d_col, chunks_per_seq)
            x3 = _s5_out(g_cl, z_cl, wglu_t, w_out, x3, ada[i], fg, i == DEPTH - 1, chunks_per_seq)
            x = x3.reshape(bsz, s, D_MODEL)
    return x


def kernel(x_prompt, x_sample, c_prompt, c_sample, norm_g, ada_w, ada_b, attn_w_in, attn_w_out,
           ssm_w_in, ssm_lam_re, ssm_lam_im, ssm_log_dt, ssm_b_re, ssm_b_im, ssm_c_re, ssm_c_im,
           ssm_d, ssm_w_glu, ssm_w_out, final_norm_g):
    assert (DEPTH - 1) % N_MIXERS == 1, "the final norm is fused into the last S5 layer"
    assert all(w // (2 * d) == RADIUS for w, d in DILATED_PAIRS)
    assert [d for _, d in DILATED_PAIRS] == [1, 4, STREAMS]
    n_prompt = c_prompt.shape[0]
    ada = _ada(jnp.concatenate([c_prompt, c_sample], axis=0), ada_w, ada_b)
    attn_w = [(_prep_attn_w_in(attn_w_in[j]), attn_w_out[j].astype(BF16))
              for j in range(attn_w_in.shape[0])]
    s5_w, s5_ops = [], []
    for j in range(ssm_w_in.shape[0]):
        d_col = jnp.tile(ssm_d[j].reshape(SSM_GROUPS, 1, GROUP_CH), (1, CHUNK, 1))
        s5_w.append((ssm_w_in[j].T.astype(BF16), ssm_w_glu[j].T.astype(BF16),
                     ssm_w_out[j].astype(BF16), d_col.reshape(SSM_GROUPS, CHUNK_ROWS, 1)))
        s5_ops.append(_s5_prep(ssm_lam_re[j], ssm_lam_im[j], ssm_log_dt[j], ssm_b_re[j],
                               ssm_b_im[j], ssm_c_re[j], ssm_c_im[j]))
    y_prompt = _trunk(x_prompt, ada[:, :n_prompt], norm_g, attn_w, s5_w, s5_ops, final_norm_g)
    y_sample = _trunk(x_sample, ada[:, n_prompt:], norm_g, attn_w, s5_w, s5_ops, final_norm_g)
    return (y_prompt, y_sample)
```

```python
import functools
import math

import numpy as np
import jax
import jax.numpy as jnp
from jax import lax
from jax.experimental import pallas as pl
from jax.experimental.pallas import tpu as pltpu

D_MODEL = 1024
DEPTH = 4
N_MIXERS = 2
ATTN_HEADS = 16
HEAD_DIM = 64
ATTN_WIDTH = ATTN_HEADS * HEAD_DIM
DILATED_PAIRS = ((128, 1), (512, 4), (2048, 16))
N_DIL = len(DILATED_PAIRS)
ROPE_THETA = 10000.0
SSM_WIDTH = D_MODEL
GROUP_CH = 16
SSM_GROUPS = SSM_WIDTH // GROUP_CH
STATE = 64
LAMBDA_RE_MAX = -1e-4
NORM_EPS = 1e-6
NEG_INF = -1e30

LANES = 128
SUBLANES = 8
CHUNK = 64
CHUNK_ROWS = CHUNK * GROUP_CH
T_PER_VREG = LANES // GROUP_CH
V_LANES = 2 * CHUNK_ROWS + LANES
STREAMS = 16
RADIUS = 64
ATTN_TQ = 128
PERM_ROWS = 512
LOG2_E = math.log2(math.e)
SLAB = 256
SLAB_CHUNKS = SLAB // SUBLANES
VMEM_LIMIT = 48 * 1024 * 1024
VMEM_LIMIT_BIG = 56 * 1024 * 1024

F32 = jnp.float32
BF16 = jnp.bfloat16
HI = lax.Precision.HIGHEST


def _silu(v):
    return v * (1.0 / (1.0 + jnp.exp(-v)))


def _rms_modulate(x, g, scale, shift):
    rs = lax.rsqrt(jnp.mean(x * x, axis=-1, keepdims=True) + NORM_EPS)
    return (x * rs * g) * (1.0 + scale) + shift


def _ada_kernel(c_ref, w_ref, b_ref, o_ref):
    o_ref[...] = jnp.dot(_silu(c_ref[...]), w_ref[...], precision=HI,
                         preferred_element_type=F32) + b_ref[...]


def _ada(c, ada_w, ada_b):
    nb = c.shape[0]
    return pl.pallas_call(
        _ada_kernel,
        out_shape=jax.ShapeDtypeStruct((DEPTH, nb, 3 * D_MODEL), F32),
        grid=(DEPTH, 3),
        in_specs=[
            pl.BlockSpec((nb, D_MODEL), lambda i, j: (0, 0)),
            pl.BlockSpec((None, D_MODEL, D_MODEL), lambda i, j: (i, 0, j)),
            pl.BlockSpec((None, 1, D_MODEL), lambda i, j: (i, 0, j)),
        ],
        out_specs=pl.BlockSpec((None, nb, D_MODEL), lambda i, j: (i, 0, j)),
        name="ada",
    )(c, ada_w, ada_b.reshape(DEPTH, 1, 3 * D_MODEL))


def _attn_in_kernel(x_ref, shift_ref, scale_ref, g_ref, w_ref, cos_ref, sin_ref, cosp_ref, sinp_ref,
                    perm_ref, qkv0_ref, qkv12_ref, z_ref, h_nat, h_perm):
    j = pl.program_id(2)
    tm = x_ref.shape[0]
    n_sub = tm // PERM_ROWS
    sub_rows = PERM_ROWS // STREAMS

    @pl.when(j == 0)
    def _():
        h = _rms_modulate(x_ref[...], g_ref[...], scale_ref[...], shift_ref[...])
        h_nat[...] = h.astype(BF16)
        for u in range(n_sub):
            rs = slice(u * PERM_ROWS, (u + 1) * PERM_ROWS)
            h_perm[rs, :] = jnp.dot(perm_ref[...], h_nat[rs, :], preferred_element_type=F32).astype(BF16)

    part = j % 3

    def rope_blocks(acc, cos, sin):
        qscale = jnp.where(part == 0, HEAD_DIM ** -0.5 * LOG2_E, 1.0).astype(F32)
        cos = jnp.where(part == 2, 1.0, cos * qscale)
        sin = jnp.where(part == 2, 0.0, sin * qscale)
        for b in range(ATTN_WIDTH // LANES):
            sl = slice(b * LANES, (b + 1) * LANES)
            t = acc[:, sl]
            yield sl, (t * cos + pltpu.roll(t, LANES // 2, 1) * sin).astype(BF16)

    @pl.when(j < 3)
    def _():
        acc = jnp.dot(h_nat[...], w_ref[...], preferred_element_type=F32)
        for sl, blk in rope_blocks(acc, cos_ref[...], sin_ref[...]):
            qkv0_ref[:, sl] = blk

    @pl.when(jnp.logical_and(j >= 3, j < 3 * N_DIL))
    def _():
        acc = jnp.dot(h_perm[...], w_ref[...], preferred_element_type=F32)

        def table(ref):
            return jnp.concatenate([ref[:, u * sub_rows:(u + 1) * sub_rows, :].reshape(PERM_ROWS, LANES)
                                    for u in range(n_sub)], axis=0)

        for sl, blk in rope_blocks(acc, table(cosp_ref), table(sinp_ref)):
            for u in range(n_sub):
                qkv12_ref[:, u * sub_rows:(u + 1) * sub_rows, sl] = (
                    blk[u * PERM_ROWS:(u + 1) * PERM_ROWS].reshape(STREAMS, sub_rows, LANES))

    @pl.when(j == 3 * N_DIL)
    def _():
        z_ref[...] = jnp.dot(h_nat[...], w_ref[...], preferred_element_type=F32).astype(BF16)


def _attn_in(x, mod, g, w, tables, tm=1024):
    bsz, s, _ = x.shape
    cos, sin, cosp, sinp = tables
    rows = tm // STREAMS
    n_col = 3 * N_DIL + 1
    dst = np.arange(PERM_ROWS)
    sub_rows = PERM_ROWS // STREAMS
    perm = np.zeros((PERM_ROWS, PERM_ROWS), np.float32)
    perm[dst, STREAMS * (dst % sub_rows) + dst // sub_rows] = 1.0
    perm = jnp.asarray(perm, BF16)
    return pl.pallas_call(
        _attn_in_kernel,
        out_shape=(jax.ShapeDtypeStruct((bsz, s, 3 * ATTN_WIDTH), BF16),
                   jax.ShapeDtypeStruct((bsz, STREAMS, s // STREAMS, 6 * ATTN_WIDTH), BF16),
                   jax.ShapeDtypeStruct((bsz, s, ATTN_WIDTH), BF16)),
        grid=(bsz, s // tm, n_col),
        in_specs=[
            pl.BlockSpec((None, tm, D_MODEL), lambda b, i, j: (b, i, 0)),
            pl.BlockSpec((None, 1, D_MODEL), lambda b, i, j: (b, 0, 0)),
            pl.BlockSpec((None, 1, D_MODEL), lambda b, i, j: (b, 0, 1)),
            pl.BlockSpec((1, D_MODEL), lambda b, i, j: (0, 0)),
            pl.BlockSpec((D_MODEL, ATTN_WIDTH), lambda b, i, j: (0, j)),
            pl.BlockSpec((tm, LANES), lambda b, i, j: (i, 0)),
            pl.BlockSpec((tm, LANES), lambda b, i, j: (i, 0)),
            pl.BlockSpec((STREAMS, rows, LANES), lambda b, i, j: (0, i, 0)),
            pl.BlockSpec((STREAMS, rows, LANES), lambda b, i, j: (0, i, 0)),
            pl.BlockSpec((PERM_ROWS, PERM_ROWS), lambda b, i, j: (0, 0)),
        ],
        out_specs=(
            pl.BlockSpec((None, tm, ATTN_WIDTH), lambda b, i, j: (b, i, jnp.clip(j, 0, 2))),
            pl.BlockSpec((None, STREAMS, rows, ATTN_WIDTH),
                         lambda b, i, j: (b, 0, i, jnp.clip(j - 3, 0, 5))),
            pl.BlockSpec((None, tm, ATTN_WIDTH), lambda b, i, j: (b, i, 0)),
        ),
        scratch_shapes=[pltpu.VMEM((tm, D_MODEL), BF16), pltpu.VMEM((tm, D_MODEL), BF16)],
        compiler_params=pltpu.CompilerParams(
            dimension_semantics=("parallel", "parallel", "arbitrary"),
            vmem_limit_bytes=VMEM_LIMIT),
        name="attn_in",
    )(x, mod, mod, g, w, cos, sin, cosp, sinp, perm)


def _mask_bias(valid):
    return jnp.where(valid, 0.0, NEG_INF).astype(F32)


def _attn_pairs(get_q, get_k, get_v, bias, put_acc):
    tq = bias.shape[0]
    lane = lax.broadcasted_iota(jnp.int32, (tq, LANES), 1)
    q_is_odd = (lane // (HEAD_DIM // 2)) % 2 == 1
    v_is_odd = lane >= HEAD_DIM
    mx_tile = jnp.zeros((tq, LANES), F32)
    den_tile = jnp.ones((tq, LANES), F32)
    ones = jnp.ones((bias.shape[1], LANES), BF16)
    for b in range(ATTN_WIDTH // LANES):
        sl = slice(b * LANES, (b + 1) * LANES)
        qb = get_q(sl).astype(F32)
        kb = get_k(sl)
        vb = jnp.concatenate([get_v(sl), ones], axis=1)
        accs = []
        for odd in (0, 1):
            qm = (jnp.where(q_is_odd, qb, 0.0) if odd else jnp.where(q_is_odd, 0.0, qb)).astype(BF16)
            sc = lax.dot_general(qm, kb, (((1,), (1,)), ((), ())), preferred_element_type=F32)
            sc = sc + bias
            mx = jnp.max(sc, axis=1, keepdims=True)
            p = jnp.exp2((sc - mx).astype(BF16))
            acc = jnp.dot(p, vb, preferred_element_type=F32)
            accs.append(acc[:, :LANES])
            mx_tile = jnp.where(lane == 2 * b + odd, mx, mx_tile)
            den_tile = jnp.where(lane == 2 * b + odd, acc[:, LANES:], den_tile)
        put_acc(sl, jnp.where(v_is_odd, accs[1], accs[0]))
    return mx_tile, den_tile


def _pair_factor(tile, b):
    lane = lax.broadcasted_iota(jnp.int32, tile.shape, 1)
    return jnp.where(lane >= HEAD_DIM, tile[:, 2 * b + 1:2 * b + 2], tile[:, 2 * b:2 * b + 1])


def _merge_groups(mx, den, pmx, pden, get_acc, get_prev, put_o, normalise):
    top = jnp.maximum(mx, pmx)
    wc, wp = jnp.exp2(mx - top), jnp.exp2(pmx - top)
    new_den = wc * den + wp * pden
    if normalise:
        inv = 1.0 / new_den
        wc, wp = wc * inv, wp * inv
    for b in range(ATTN_WIDTH // LANES):
        sl = slice(b * LANES, (b + 1) * LANES)
        put_o(sl, _pair_factor(wc, b) * get_acc(sl) + _pair_factor(wp, b) * get_prev(sl))
    return top, new_den


def _window_mask(i, tq, m):
    nk = tq + 2 * RADIUS
    row = lax.broadcasted_iota(jnp.int32, (tq, nk), 0)
    col = lax.broadcasted_iota(jnp.int32, (tq, nk), 1)
    kabs = i * tq - RADIUS + col
    return _mask_bias((jnp.abs(col - RADIUS - row) <= RADIUS) & (kabs >= 0) & (kabs < m))


def _cat3(p_ref, c_ref, n_ref):
    return lambda sl: jnp.concatenate([p_ref[:, sl], c_ref[:, sl], n_ref[:, sl]], axis=0)


def _attn16_kernel(q_ref, kp_ref, kc_ref, kn_ref, vp_ref, vc_ref, vn_ref, o_ref, mx_ref, den_ref, *, m):
    valid = _window_mask(pl.program_id(2), q_ref.shape[0], m)

    def put_acc(sl, val):
        o_ref[:, sl] = val

    mx_ref[...], den_ref[...] = _attn_pairs(lambda sl: q_ref[:, sl], _cat3(kp_ref, kc_ref, kn_ref),
                                            _cat3(vp_ref, vc_ref, vn_ref), valid, put_acc)


def _attn4_kernel(q_ref, kp_ref, kc_ref, kn_ref, vp_ref, vc_ref, vn_ref, po_ref, pmx_ref, pden_ref,
                  o_ref, mx_ref, den_ref, *, m):
    i = pl.program_id(2)
    ns, rows = q_ref.shape[0], q_ref.shape[1]
    halo = kp_ref.shape[1]
    tq, nk = ns * rows, ns * (rows + 2 * halo)
    n = lax.broadcasted_iota(jnp.int32, (tq, nk), 0)
    c = lax.broadcasted_iota(jnp.int32, (tq, nk), 1)
    qpos = ns * (rows * i + n % rows) + n // rows
    is_prev, is_next = c < ns * halo, c >= ns * (halo + rows)
    edge = is_prev | is_next
    cc = jnp.where(is_prev, c, jnp.where(is_next, c - ns * (halo + rows), c - ns * halo))
    krow = rows * i + jnp.where(is_prev, -halo, jnp.where(is_next, rows, 0)) + jnp.where(edge, cc % halo, cc % rows)
    kpos = ns * krow + jnp.where(edge, cc // halo, cc // rows)
    valid = _mask_bias((jnp.abs(kpos - qpos) <= RADIUS) & (krow >= 0) & (krow < m))

    def cat(p_ref, c_ref, n_ref):
        return lambda sl: jnp.concatenate(
            [p_ref[:, :, sl].reshape(ns * halo, LANES), c_ref[:, :, sl].reshape(tq, LANES),
             n_ref[:, :, sl].reshape(ns * halo, LANES)], axis=0)

    def put_o(sl, val):
        o_ref[:, :, sl] = val.reshape(ns, rows, LANES)

    flat = lambda ref: (lambda sl: ref[:, :, sl].reshape(tq, LANES))
    mx, den = _attn_pairs(flat(q_ref), cat(kp_ref, kc_ref, kn_ref), cat(vp_ref, vc_ref, vn_ref),
                          valid, put_o)
    mx, den = _merge_groups(mx, den, pmx_ref[...].reshape(tq, LANES), pden_ref[...].reshape(tq, LANES),
                            flat(o_ref), flat(po_ref), put_o, normalise=False)
    mx_ref[...] = mx.reshape(ns, rows, LANES)
    den_ref[...] = den.reshape(ns, rows, LANES)


def _attn1_kernel(q_ref, kp_ref, kc_ref, kn_ref, vp_ref, vc_ref, vn_ref, po_ref, pmx_ref, pden_ref,
                  o_ref, acc_scr, *, m):
    valid = _window_mask(pl.program_id(1), q_ref.shape[0], m)
    rows = po_ref.shape[1]

    def natural(ref, sl):
        return jnp.concatenate([ref[:, ii, sl] for ii in range(rows)], axis=0)

    def put_acc(sl, val):
        acc_scr[:, sl] = val

    def put_o(sl, val):
        o_ref[:, sl] = val.astype(BF16)

    mx, den = _attn_pairs(lambda sl: q_ref[:, sl], _cat3(kp_ref, kc_ref, kn_ref),
                          _cat3(vp_ref, vc_ref, vn_ref), valid, put_acc)
    stats = slice(0, LANES)
    _merge_groups(mx, den, natural(pmx_ref, stats), natural(pden_ref, stats),
                  lambda sl: acc_scr[:, sl], lambda sl: natural(po_ref, sl), put_o, normalise=True)


def _attention(qkv0, qkv12):
    bsz, s, _ = qkv0.shape
    m16 = s // STREAMS
    tq, half = ATTN_TQ, RADIUS
    params = lambda n: pltpu.CompilerParams(dimension_semantics=("parallel",) * n,
                                            vmem_limit_bytes=VMEM_LIMIT)

    def spec16(nrows, row_idx, col, width=ATTN_WIDTH):
        return pl.BlockSpec((None, None, nrows, width), lambda b, r, i: (b, r, row_idx(i), col))
    per = tq // half
    cur = lambda i: i
    prv = lambda i: jnp.maximum(i * per - 1, 0)
    nxt = lambda i: jnp.minimum((i + 1) * per, m16 // half - 1)
    stat16 = jax.ShapeDtypeStruct((bsz, STREAMS, m16, LANES), F32)
    o2, mx2, den2 = pl.pallas_call(
        functools.partial(_attn16_kernel, m=m16),
        out_shape=(jax.ShapeDtypeStruct((bsz, STREAMS, m16, ATTN_WIDTH), F32), stat16, stat16),
        grid=(bsz, STREAMS, m16 // tq),
        in_specs=[spec16(tq, cur, 3), spec16(half, prv, 4), spec16(tq, cur, 4), spec16(half, nxt, 4),
                  spec16(half, prv, 5), spec16(tq, cur, 5), spec16(half, nxt, 5)],
        out_specs=(spec16(tq, cur, 0), spec16(tq, cur, 0, LANES), spec16(tq, cur, 0, LANES)),
        compiler_params=params(3), name="attn_dil16",
    )(*([qkv12] * 7))

    ns = STREAMS // 4
    rows, halo = tq // ns, half // ns

    def view4(a):
        return a.reshape(bsz, ns, 4, m16, a.shape[-1])

    def spec4(nrows, row_idx, col, width=ATTN_WIDTH):
        return pl.BlockSpec((None, ns, None, nrows, width), lambda b, r, i: (b, 0, r, row_idx(i), col))
    per4 = rows // halo
    prv4 = lambda i: jnp.maximum(i * per4 - 1, 0)
    nxt4 = lambda i: jnp.minimum((i + 1) * per4, m16 // halo - 1)
    q4 = view4(qkv12)
    stat4 = jax.ShapeDtypeStruct((bsz, ns, 4, m16, LANES), F32)
    o12, mx12, den12 = pl.pallas_call(
        functools.partial(_attn4_kernel, m=m16),
        out_shape=(jax.ShapeDtypeStruct((bsz, ns, 4, m16, ATTN_WIDTH), F32), stat4, stat4),
        grid=(bsz, 4, m16 // rows),
        in_specs=[spec4(rows, cur, 0), spec4(halo, prv4, 1), spec4(rows, cur, 1), spec4(halo, nxt4, 1),
                  spec4(halo, prv4, 2), spec4(rows, cur, 2), spec4(halo, nxt4, 2),
                  spec4(rows, cur, 0), spec4(rows, cur, 0, LANES), spec4(rows, cur, 0, LANES)],
        out_specs=(spec4(rows, cur, 0), spec4(rows, cur, 0, LANES), spec4(rows, cur, 0, LANES)),
        compiler_params=params(3), name="attn_dil4",
    )(q4, q4, q4, q4, q4, q4, q4, view4(o2), view4(mx2), view4(den2))
    streams = lambda a: a.reshape(bsz, STREAMS, m16, a.shape[-1])

    def spec1(nrows, row_idx, col):
        return pl.BlockSpec((None, nrows, ATTN_WIDTH), lambda b, i: (b, row_idx(i), col))
    nxt1 = lambda i: jnp.minimum((i + 1) * per, s // half - 1)
    srow = tq // STREAMS
    sm = lambda width: pl.BlockSpec((None, STREAMS, srow, width), lambda b, i: (b, 0, i, 0))
    return pl.pallas_call(
        functools.partial(_attn1_kernel, m=s),
        out_shape=jax.ShapeDtypeStruct((bsz, s, ATTN_WIDTH), BF16),
        grid=(bsz, s // tq),
        in_specs=[spec1(tq, cur, 0), spec1(half, prv, 1), spec1(tq, cur, 1), spec1(half, nxt1, 1),
                  spec1(half, prv, 2), spec1(tq, cur, 2), spec1(half, nxt1, 2),
                  sm(ATTN_WIDTH), sm(LANES), sm(LANES)],
        out_specs=pl.BlockSpec((None, tq, ATTN_WIDTH), lambda b, i: (b, i, 0)),
        scratch_shapes=[pltpu.VMEM((tq, ATTN_WIDTH), F32)],
        compiler_params=params(2), name="attn_dil1",
    )(*([qkv0] * 7), streams(o12), streams(mx12), streams(den12))


def _attn_out_kernel(o_ref, z_ref, x_ref, gate_ref, w_ref, out_ref):
    y = (o_ref[...].astype(F32) * _silu(z_ref[...].astype(F32))).astype(BF16)
    out = jnp.dot(y, w_ref[...], preferred_element_type=F32)
    out_ref[...] = x_ref[...] + gate_ref[...] * out


def _attn_out(o, z, x, mod, w, tm=512):
    bsz, s, _ = x.shape
    row = pl.BlockSpec((None, tm, D_MODEL), lambda b, i: (b, i, 0))
    return pl.pallas_call(
        _attn_out_kernel,
        out_shape=jax.ShapeDtypeStruct(x.shape, F32),
        grid=(bsz, s // tm),
        in_specs=[row, row, row,
                  pl.BlockSpec((None, 1, D_MODEL), lambda b, i: (b, 0, 2)),
                  pl.BlockSpec((ATTN_WIDTH, D_MODEL), lambda b, i: (0, 0))],
        out_specs=row,
        compiler_params=pltpu.CompilerParams(
            dimension_semantics=("parallel", "parallel"),
            vmem_limit_bytes=VMEM_LIMIT),
        name="attn_out",
    )(o, z, x, mod, w)


def _cmul(ar, ai, br, bi):
    return ar * br - ai * bi, ar * bi + ai * br


def _s5_prep_kernel(lam_re_a, lam_im_a, lam_re_b, lam_im_b, logdt, bt_re, bt_im,
                    c_re, c_im, toep_ref, mb_ref, mc_ref, al_ref, v_scr):
    lane = lax.broadcasted_iota(jnp.int32, (STATE, LANES), 1)
    a_idx = (lane // GROUP_CH).astype(F32)
    n_tiles = CHUNK // T_PER_VREG

    mb_rows = []
    v_rhs = []
    v_lhs = []
    mc_cols = []
    zbar_f = None
    for dr in range(2):
        dt = jnp.exp(logdt[dr])
        lr = jnp.minimum(lam_re_a[dr], LAMBDA_RE_MAX)
        li = lam_im_a[dr]
        ldr, ldi = lr * dt, li * dt

        def powers(e):
            mag = jnp.exp(e * ldr)
            return mag * jnp.cos(e * ldi), mag * jnp.sin(e * ldi)

        l1r, l1i = powers(1.0)
        den = lr * lr + li * li
        nr, ni = l1r - 1.0, l1i
        cr, ci = (nr * lr + ni * li) / den, (ni * lr - nr * li) / den
        bbr, bbi = _cmul(cr, ci, bt_re[dr], bt_im[dr])
        l8r, l8i = powers(float(T_PER_VREG))
        tiles = [None] * n_tiles
        if dr == 0:
            pr, pi = powers(float(T_PER_VREG - 1) - a_idx)
            cur = _cmul(pr, pi, bbr, bbi)
            for jt in range(n_tiles - 1, -1, -1):
                tiles[jt] = cur
                if jt:
                    cur = _cmul(cur[0], cur[1], l8r, l8i)
            v_rhs.append(_cmul(tiles[n_tiles - 1][0], tiles[n_tiles - 1][1], l1r, l1i))
            zbar_f = (jnp.where(lane < GROUP_CH, bbr, 0.0), jnp.where(lane < GROUP_CH, bbi, 0.0))
        else:
            pr, pi = powers(a_idx)
            cur = _cmul(pr, pi, bbr, bbi)
            for jt in range(n_tiles):
                tiles[jt] = cur
                if jt < n_tiles - 1:
                    cur = _cmul(cur[0], cur[1], l8r, l8i)
            v_rhs.append(tiles[0])
        mb_rows.append(jnp.concatenate([t[0] for t in tiles], axis=1))
        mb_rows.append(jnp.concatenate([t[1] for t in tiles], axis=1))
        a_r, a_i = l8r, l8i
        for _ in range(int(math.log2(CHUNK // T_PER_VREG))):
            a_r, a_i = _cmul(a_r, a_i, a_r, a_i)
        al_ref[2 * dr] = jnp.broadcast_to(a_r, (STATE, LANES))
        al_ref[2 * dr + 1] = jnp.broadcast_to(a_i, (STATE, LANES))

        lrb = jnp.minimum(lam_re_b[dr], LAMBDA_RE_MAX)
        lib = lam_im_b[dr]
        mag = jnp.exp(lrb * dt)
        pw_r, pw_i = mag * jnp.cos(lib * dt), mag * jnp.sin(lib * dt)
        pows = [(pw_r, pw_i)]
        for _ in range(int(math.log2(CHUNK)) - 1):
            pows.append(_cmul(*pows[-1], *pows[-1]))
        cre, cim = c_re[dr], c_im[dr]
        xr, xi = _cmul(cre, cim, *pows[0])
        for k in range(int(math.log2(CHUNK))):
            yr, yi = _cmul(xr, xi, *pows[k])
            if dr == 0:
                xr, xi = jnp.concatenate([xr, yr], 0), jnp.concatenate([xi, yi], 0)
            else:
                xr, xi = jnp.concatenate([yr, xr], 0), jnp.concatenate([yi, xi], 0)
        mc_cols += [xr, -xi]
        xr, xi = cre, cim
        for k in range(int(math.log2(n_tiles))):
            yr, yi = _cmul(xr, xi, *pows[k + int(math.log2(T_PER_VREG))])
            if dr == 0:
                xr, xi = jnp.concatenate([yr, xr], 0), jnp.concatenate([yi, xi], 0)
            else:
                xr, xi = jnp.concatenate([xr, yr], 0), jnp.concatenate([xi, yi], 0)
        v_lhs.append((xr, xi))

    mb_ref[...] = jnp.concatenate(mb_rows, axis=0).astype(BF16)
    mc_ref[...] = jnp.concatenate(mc_cols, axis=1).astype(BF16)

    def cdot(lhs, rhs):
        return (jnp.dot(lhs[0], rhs[0], precision=HI, preferred_element_type=F32)
                - jnp.dot(lhs[1], rhs[1], precision=HI, preferred_element_type=F32))

    out_f = cdot(v_lhs[0], v_rhs[0])
    zero_rows = jnp.zeros((CHUNK_ROWS // T_PER_VREG - GROUP_CH, STATE), F32)
    lag0_lhs = (jnp.concatenate([c_re[0], zero_rows], 0), jnp.concatenate([c_im[0], zero_rows], 0))
    out_b = cdot(v_lhs[1], v_rhs[1]) + cdot(lag0_lhs, zbar_f)
    for jt in range(n_tiles):
        rows = slice(jt * GROUP_CH, (jt + 1) * GROUP_CH)
        v_scr[:, jt * LANES:(jt + 1) * LANES] = out_f[rows]
        v_scr[:, (n_tiles + jt) * LANES:(n_tiles + jt + 1) * LANES] = out_b[rows]
    v_scr[:, 2 * CHUNK_ROWS:] = jnp.zeros((GROUP_CH, LANES), F32)

    for k in range(T_PER_VREG):
        vk = v_scr[:, k * GROUP_CH:k * GROUP_CH + 2 * CHUNK_ROWS]
        for t_out in range(CHUNK):
            off = (CHUNK - t_out) * GROUP_CH
            if off % LANES == k * GROUP_CH:
                al_off = off - k * GROUP_CH
                toep_ref[t_out * GROUP_CH:(t_out + 1) * GROUP_CH, :] = (
                    vk[:, al_off:al_off + CHUNK_ROWS].astype(BF16))


def _s5_prep(lam_re, lam_im, log_dt, b_re, b_im, c_re, c_im):
    g = SSM_GROUPS
    lam_a = lambda a: a.reshape(2, g, STATE, 1)
    lam_b = lambda a: a.reshape(2, g, 1, STATE)
    tile_b = lambda a: jnp.tile(a, (1, 1, 1, T_PER_VREG))
    dir_spec = lambda r, c: pl.BlockSpec((2, None, r, c), lambda i: (0, i, 0, 0))
    return pl.pallas_call(
        _s5_prep_kernel,
        out_shape=(jax.ShapeDtypeStruct((g, CHUNK_ROWS, CHUNK_ROWS), BF16),
                   jax.ShapeDtypeStruct((g, 4 * STATE, CHUNK_ROWS), BF16),
                   jax.ShapeDtypeStruct((g, CHUNK_ROWS, 4 * STATE), BF16),
                   jax.ShapeDtypeStruct((g, 4, STATE, LANES), F32)),
        grid=(g,),
        in_specs=[dir_spec(STATE, 1), dir_spec(STATE, 1), dir_spec(1, STATE), dir_spec(1, STATE),
                  dir_spec(1, 1), dir_spec(STATE, LANES), dir_spec(STATE, LANES),
                  dir_spec(GROUP_CH, STATE), dir_spec(GROUP_CH, STATE)],
        out_specs=(pl.BlockSpec((None, CHUNK_ROWS, CHUNK_ROWS), lambda i: (i, 0, 0)),
                   pl.BlockSpec((None, 4 * STATE, CHUNK_ROWS), lambda i: (i, 0, 0)),
                   pl.BlockSpec((None, CHUNK_ROWS, 4 * STATE), lambda i: (i, 0, 0)),
                   pl.BlockSpec((None, 4, STATE, LANES), lambda i: (i, 0, 0, 0))),
        scratch_shapes=[pltpu.VMEM((GROUP_CH, V_LANES), F32)],
        compiler_params=pltpu.CompilerParams(dimension_semantics=("parallel",)),
        name="s5_prep",
    )(lam_a(lam_re), lam_a(lam_im), lam_b(lam_re), lam_b(lam_im), log_dt.reshape(2, g, 1, 1),
      tile_b(b_re), tile_b(b_im), c_re, c_im)


def _slab_perm():
    dst = np.arange(SLAB)
    perm = np.zeros((SLAB, SLAB), np.float32)
    perm[dst, (dst % SLAB_CHUNKS) * SUBLANES + dst // SLAB_CHUNKS] = 1.0
    return perm


def _s5_in_kernel(x_ref, mod_ref, g_ref, w_ref, p_ref, u_ref, z_ref, h_scr, *, chunks_per_seq):
    n_chunks = x_ref.shape[0]
    for b in range(n_chunks // chunks_per_seq):
        rows = slice(b * chunks_per_seq, (b + 1) * chunks_per_seq)
        h = _rms_modulate(x_ref[rows], g_ref[...], mod_ref[b:b + 1, D_MODEL:2 * D_MODEL],
                          mod_ref[b:b + 1, :D_MODEL])
        h2 = h.reshape(chunks_per_seq * SUBLANES, D_MODEL).astype(BF16)
        for s in range(chunks_per_seq // SLAB_CHUNKS):
            hp = jnp.dot(p_ref[...], h2[s * SLAB:(s + 1) * SLAB], preferred_element_type=F32)
            c0 = b * chunks_per_seq + s * SLAB_CHUNKS
            for k in range(SUBLANES):
                h_scr[k, c0:c0 + SLAB_CHUNKS, :] = hp[k * SLAB_CHUNKS:(k + 1) * SLAB_CHUNKS].astype(BF16)
    for k in range(SUBLANES):
        ht = h_scr[k].astype(F32).T.astype(BF16)
        uz = jnp.dot(w_ref[...], ht, preferred_element_type=F32)
        u_ref[k] = uz[:SSM_WIDTH].astype(BF16)
        z_ref[k] = uz[SSM_WIDTH:].astype(BF16)


def _s5_in(x3, mod, g, w_t, chunks_per_seq):
    n_chunks = x3.shape[0]
    once = lambda shape: pl.BlockSpec(shape, lambda t: (0,) * len(shape), pipeline_mode=pl.Buffered(1))
    out = pl.BlockSpec((SUBLANES, SSM_WIDTH, n_chunks), lambda t: (t, 0, 0))
    return pl.pallas_call(
        functools.partial(_s5_in_kernel, chunks_per_seq=chunks_per_seq),
        out_shape=(jax.ShapeDtypeStruct((CHUNK, SSM_WIDTH, n_chunks), BF16),) * 2,
        grid=(CHUNK // SUBLANES,),
        in_specs=[pl.BlockSpec((n_chunks, SUBLANES, D_MODEL), lambda t: (0, t, 0)),
                  once(mod.shape), once((1, D_MODEL)), once((2 * SSM_WIDTH, D_MODEL)),
                  once((SLAB, SLAB))],
        out_specs=(out, out),
        scratch_shapes=[pltpu.VMEM((SUBLANES, n_chunks, D_MODEL), BF16)],
        compiler_params=pltpu.CompilerParams(
            dimension_semantics=("parallel",), vmem_limit_bytes=VMEM_LIMIT_BIG),
        name="s5_in",
    )(x3, mod, g, w_t, jnp.asarray(_slab_perm(), BF16))


def _gelu_tanh(v):
    return 0.5 * v * (1.0 + jnp.tanh(math.sqrt(2.0 / math.pi) * (v + 0.044715 * (v * v * v))))


def _s5_core_kernel(u_ref, toep_ref, mb_ref, mc_ref, al_ref, d_ref, o_ref, *, chunks_per_seq):
    n_chunks = u_ref.shape[-1]
    ub = u_ref[...].reshape(CHUNK_ROWS, n_chunks)
    inc = jnp.dot(mb_ref[...], ub, preferred_element_type=F32)
    pos = lax.broadcasted_iota(jnp.int32, (STATE, n_chunks), 1) % chunks_per_seq
    reps = n_chunks // LANES

    def lane_tile(a):
        return jnp.concatenate([a] * reps, axis=1) if reps > 1 else a

    states = []
    for dr in range(2):
        xr = inc[(2 * dr) * STATE:(2 * dr + 1) * STATE]
        xi = inc[(2 * dr + 1) * STATE:(2 * dr + 2) * STATE]
        ar, ai = lane_tile(al_ref[2 * dr]), lane_tile(al_ref[2 * dr + 1])

        def shifted(v, step):
            if dr == 0:
                return jnp.where(pos >= step, pltpu.roll(v, step, 1), 0.0)
            return jnp.where(pos < chunks_per_seq - step, pltpu.roll(v, n_chunks - step, 1), 0.0)

        step = 1
        while step < chunks_per_seq:
            sr, si = shifted(xr, step), shifted(xi, step)
            xr, xi = xr + ar * sr - ai * si, xi + ar * si + ai * sr
            ar, ai = _cmul(ar, ai, ar, ai)
            step *= 2
        states += [shifted(xr, 1), shifted(xi, 1)]
    h_in = jnp.concatenate(states, axis=0).astype(BF16)
    y = (jnp.dot(toep_ref[...], ub, preferred_element_type=F32)
         + jnp.dot(mc_ref[...], h_in, preferred_element_type=F32)
         + d_ref[...] * ub.astype(F32))
    o_ref[...] = _gelu_tanh(y).astype(BF16).reshape(CHUNK, GROUP_CH, n_chunks)


def _s5_core(u_cl, toep, mb, mc, al, d_col, chunks_per_seq):
    n_chunks = u_cl.shape[-1]
    u4 = u_cl.reshape(CHUNK, SSM_GROUPS, GROUP_CH, n_chunks)
    grp = pl.BlockSpec((CHUNK, None, GROUP_CH, n_chunks), lambda g: (0, g, 0, 0))
    out = pl.pallas_call(
        functools.partial(_s5_core_kernel, chunks_per_seq=chunks_per_seq),
        out_shape=jax.ShapeDtypeStruct(u4.shape, BF16),
        grid=(SSM_GROUPS,),
        in_specs=[grp,
                  pl.BlockSpec((None, CHUNK_ROWS, CHUNK_ROWS), lambda g: (g, 0, 0)),
                  pl.BlockSpec((None, 4 * STATE, CHUNK_ROWS), lambda g: (g, 0, 0)),
                  pl.BlockSpec((None, CHUNK_ROWS, 4 * STATE), lambda g: (g, 0, 0)),
                  pl.BlockSpec((None, 4, STATE, LANES), lambda g: (g, 0, 0, 0)),
                  pl.BlockSpec((None, CHUNK_ROWS, 1), lambda g: (g, 0, 0))],
        out_specs=grp,
        compiler_params=pltpu.CompilerParams(
            dimension_semantics=("parallel",), vmem_limit_bytes=VMEM_LIMIT),
        name="s5_core",
    )(u4, toep, mb, mc, al, d_col)
    return out.reshape(CHUNK, SSM_WIDTH, n_chunks)


def _s5_out_kernel(g_ref, z_ref, wglu_ref, wout_ref, pt_ref, x_ref, mod_ref, fg_ref, o_ref, y_scr,
                   *, final, chunks_per_seq):
    half = pl.program_id(1)
    n_half = x_ref.shape[0]

    @pl.when(half == 0)
    def _():
        for k in range(SUBLANES):
            gb = g_ref[k]
            glu = jnp.dot(wglu_ref[...], gb, preferred_element_type=F32)
            y = gb.astype(F32) * (1.0 / (1.0 + jnp.exp(-glu))) * _silu(z_ref[k].astype(F32))
            y_scr[k] = y.T.astype(BF16)

    for hv in range(y_scr.shape[1] // n_half):
        @pl.when(half == hv)
        def _():
            for s in range(n_half // SLAB_CHUNKS):
                c0 = hv * n_half + s * SLAB_CHUNKS
                src = jnp.concatenate([y_scr[k, c0:c0 + SLAB_CHUNKS, :] for k in range(SUBLANES)], axis=0)
                yp = jnp.dot(pt_ref[...], src, preferred_element_type=F32).astype(BF16)
                out = jnp.dot(yp, wout_ref[...], preferred_element_type=F32)
                b = c0 // chunks_per_seq
                upd = mod_ref[b:b + 1, 2 * D_MODEL:] * out
                rows = slice(s * SLAB_CHUNKS, (s + 1) * SLAB_CHUNKS)
                xn = x_ref[rows] + upd.reshape(SLAB_CHUNKS, SUBLANES, D_MODEL)
                if final:
                    xn = xn * lax.rsqrt(jnp.mean(xn * xn, axis=-1, keepdims=True) + NORM_EPS) * fg_ref[...]
                o_ref[rows] = xn


def _s5_out(g_cl, z_cl, wglu_t, w_out, x3, mod, final_g, final, chunks_per_seq, n_split=2):
    n_chunks = x3.shape[0]
    n_half = n_chunks // n_split
    once = lambda shape: pl.BlockSpec(shape, lambda t, h: (0,) * len(shape), pipeline_mode=pl.Buffered(1))
    act = pl.BlockSpec((SUBLANES, SSM_WIDTH, n_chunks), lambda t, h: (t, 0, 0))
    row = pl.BlockSpec((n_half, SUBLANES, D_MODEL), lambda t, h: (h, t, 0))
    return pl.pallas_call(
        functools.partial(_s5_out_kernel, final=final, chunks_per_seq=chunks_per_seq),
        out_shape=jax.ShapeDtypeStruct(x3.shape, F32),
        grid=(CHUNK // SUBLANES, n_split),
        in_specs=[act, act, once((SSM_WIDTH, SSM_WIDTH)), once((SSM_WIDTH, D_MODEL)), once((SLAB, SLAB)),
                  row, once(mod.shape), once((1, D_MODEL))],
        out_specs=row,
        scratch_shapes=[pltpu.VMEM((SUBLANES, n_chunks, SSM_WIDTH), BF16)],
        compiler_params=pltpu.CompilerParams(
            dimension_semantics=("parallel", "arbitrary"), vmem_limit_bytes=VMEM_LIMIT_BIG),
        name="s5_out",
    )(g_cl, z_cl, wglu_t, w_out, jnp.asarray(_slab_perm().T, BF16), x3, mod, final_g)


def _rope_tables(s):
    inv_freq = ROPE_THETA ** (-jnp.arange(0, HEAD_DIM, 2, dtype=F32) / HEAD_DIM)
    ang = jnp.arange(s, dtype=F32)[:, None] * inv_freq[None, :]
    reps = LANES // (HEAD_DIM // 2)
    sign = jnp.where(jnp.arange(LANES) < LANES // 2, -1.0, 1.0).astype(F32)
    cos, sin = jnp.tile(jnp.cos(ang), (1, reps)), jnp.tile(jnp.sin(ang), (1, reps)) * sign
    stream_major = lambda a: a.reshape(s // STREAMS, STREAMS, LANES).transpose(1, 0, 2)
    return cos, sin, stream_major(cos), stream_major(sin)


def _prep_attn_w_in(w):
    d_in = w.shape[0]
    w = w.astype(BF16)
    qkv = w[:, :3 * N_DIL * ATTN_WIDTH].reshape(d_in, N_DIL, 3, ATTN_HEADS // 2, 2, 2, HEAD_DIM // 2)
    qk = qkv[:, :, :2].transpose(0, 1, 2, 3, 5, 4, 6)
    qkv = jnp.concatenate([qk, qkv[:, :, 2:]], axis=2).reshape(d_in, 3 * N_DIL * ATTN_WIDTH)
    return jnp.concatenate([qkv, w[:, 3 * N_DIL * ATTN_WIDTH:]], axis=1)


def _trunk(x, ada, norm_g, attn_w, s5_w, s5_ops, final_norm_g):
    bsz, s, _ = x.shape
    chunks_per_seq = s // CHUNK
    n_chunks = bsz * chunks_per_seq
    tables = _rope_tables(s)
    fg = final_norm_g.reshape(1, D_MODEL)
    for i in range(DEPTH):
        mod = ada[i].reshape(bsz, 1, 3 * D_MODEL)
        g = norm_g[i].reshape(1, D_MODEL)
        j = i // N_MIXERS
        if i % N_MIXERS == 0:
            w_in, w_out = attn_w[j]
            qkv0, qkv12, z = _attn_in(x, mod, g, w_in, tables)
            x = _attn_out(_attention(qkv0, qkv12), z, x, mod, w_out)
        else:
            w_in_t, wglu_t, w_out, d_col = s5_w[j]
            toep, mb, mc, al = s5_ops[j]
            x3 = x.reshape(n_chunks, CHUNK, D_MODEL)
            u_cl, z_cl = _s5_in(x3, ada[i], g, w_in_t, chunks_per_seq)
            g_cl = _s5_core(u_cl, toep, mb, mc, al, d_col, chunks_per_seq)
            x3 = _s5_out(g_cl, z_cl, wglu_t, w_out, x3, ada[i], fg, i == DEPTH - 1, chunks_per_seq)
            x = x3.reshape(bsz, s, D_MODEL)
    return x


def kernel(x_prompt, x_sample, c_prompt, c_sample, norm_g, ada_w, ada_b, attn_w_in, attn_w_out,
           ssm_w_in, ssm_lam_re, ssm_lam_im, ssm_log_dt, ssm_b_re, ssm_b_im, ssm_c_re, ssm_c_im,
           ssm_d, ssm_w_glu, ssm_w_out, final_norm_g):
    assert (DEPTH - 1) % N_MIXERS == 1, "the final norm is fused into the last S5 layer"
    assert all(w // (2 * d) == RADIUS for w, d in DILATED_PAIRS)
    assert [d for _, d in DILATED_PAIRS] == [1, 4, STREAMS]
    n_prompt = c_prompt.shape[0]
    ada = _ada(jnp.concatenate([c_prompt, c_sample], axis=0), ada_w, ada_b)
    attn_w = [(_prep_attn_w_in(attn_w_in[j]), attn_w_out[j].astype(BF16))
              for j in range(attn_w_in.shape[0])]
    s5_w, s5_ops = [], []
    for j in range(ssm_w_in.shape[0]):
        d_col = jnp.tile(ssm_d[j].reshape(SSM_GROUPS, 1, GROUP_CH), (1, CHUNK, 1))
        s5_w.append((ssm_w_in[j].T.astype(BF16), ssm_w_glu[j].T.astype(BF16),
                     ssm_w_out[j].astype(BF16), d_col.reshape(SSM_GROUPS, CHUNK_ROWS, 1)))
        s5_ops.append(_s5_prep(ssm_lam_re[j], ssm_lam_im[j], ssm_log_dt[j], ssm_b_re[j],
                               ssm_b_im[j], ssm_c_re[j], ssm_c_im[j]))
    y_prompt = _trunk(x_prompt, ada[:, :n_prompt], norm_g, attn_w, s5_w, s5_ops, final_norm_g)
    y_sample = _trunk(x_sample, ada[:, n_prompt:], norm_g, attn_w, s5_w, s5_ops, final_norm_g)
    return (y_prompt, y_sample)
```

```python
import functools
import math

import numpy as np
import jax
import jax.numpy as jnp
from jax import lax
from jax.experimental import pallas as pl
from jax.experimental.pallas import tpu as pltpu

D_MODEL = 1024
DEPTH = 4
N_MIXERS = 2
ATTN_HEADS = 16
HEAD_DIM = 64
ATTN_WIDTH = ATTN_HEADS * HEAD_DIM
DILATED_PAIRS = ((128, 1), (512, 4), (2048, 16))
N_DIL = len(DILATED_PAIRS)
ROPE_THETA = 10000.0
SSM_WIDTH = D_MODEL
GROUP_CH = 16
SSM_GROUPS = SSM_WIDTH // GROUP_CH
STATE = 64
LAMBDA_RE_MAX = -1e-4
NORM_EPS = 1e-6
NEG_INF = -1e30

LANES = 128
SUBLANES = 8
CHUNK = 64
CHUNK_ROWS = CHUNK * GROUP_CH
T_PER_VREG = LANES // GROUP_CH
V_LANES = 2 * CHUNK_ROWS + LANES
STREAMS = 16
RADIUS = 64
ATTN_TQ = 128
ATTN_SUB = 4
ATTN_IN_BLOCK_ORDER = (1, 2, 0, 4, 5, 7, 8, 3, 6, 9)
PERM_ROWS = 512
LOG2_E = math.log2(math.e)
SLAB = 256
SLAB_CHUNKS = SLAB // SUBLANES
VMEM_LIMIT = 48 * 1024 * 1024
VMEM_LIMIT_BIG = 56 * 1024 * 1024

F32 = jnp.float32
BF16 = jnp.bfloat16
HI = lax.Precision.HIGHEST


def _silu(v):
    return v * (1.0 / (1.0 + jnp.exp(-v)))


def _rms_modulate(x, g, scale, shift):
    rs = lax.rsqrt(jnp.mean(x * x, axis=-1, keepdims=True) + NORM_EPS)
    return (x * rs * g) * (1.0 + scale) + shift


def _ada_kernel(c_ref, w_ref, b_ref, o_ref):
    o_ref[...] = jnp.dot(_silu(c_ref[...]), w_ref[...], precision=HI,
                         preferred_element_type=F32) + b_ref[...]


def _ada(c, ada_w, ada_b):
    nb = c.shape[0]
    return pl.pallas_call(
        _ada_kernel,
        out_shape=jax.ShapeDtypeStruct((DEPTH, nb, 3 * D_MODEL), F32),
        grid=(DEPTH, 3),
        in_specs=[
            pl.BlockSpec((nb, D_MODEL), lambda i, j: (0, 0)),
            pl.BlockSpec((None, D_MODEL, D_MODEL), lambda i, j: (i, 0, j)),
            pl.BlockSpec((None, 1, D_MODEL), lambda i, j: (i, 0, j)),
        ],
        out_specs=pl.BlockSpec((None, nb, D_MODEL), lambda i, j: (i, 0, j)),
        name="ada",
    )(c, ada_w, ada_b.reshape(DEPTH, 1, 3 * D_MODEL))


def _attn_in_kernel(x_ref, shift_ref, scale_ref, g_ref, w_ref, cos_ref, sin_ref, cosp_ref, sinp_ref,
                    perm_ref, qkv0_ref, qkv12_ref, z_ref, h_nat, h_perm):
    j = pl.program_id(2)
    tm = x_ref.shape[0]
    n_sub = tm // PERM_ROWS
    sub_rows = PERM_ROWS // STREAMS

    @pl.when(j == 0)
    def _():
        h = _rms_modulate(x_ref[...], g_ref[...], scale_ref[...], shift_ref[...])
        h_nat[...] = h.astype(BF16)
        for u in range(n_sub):
            rs = slice(u * PERM_ROWS, (u + 1) * PERM_ROWS)
            h_perm[rs, :] = jnp.dot(perm_ref[...], h_nat[rs, :], preferred_element_type=F32).astype(BF16)

    is_q = jnp.logical_or(j == 2, j >= 7)
    is_v = jnp.logical_or(j == 1, jnp.logical_or(j == 4, j == 6))

    def rope_blocks(acc, cos, sin):
        qscale = jnp.where(is_q, HEAD_DIM ** -0.5 * LOG2_E, 1.0).astype(F32)
        cos = jnp.where(is_v, 1.0, cos * qscale)
        sin = jnp.where(is_v, 0.0, sin * qscale)
        for b in range(ATTN_WIDTH // LANES):
            sl = slice(b * LANES, (b + 1) * LANES)
            t = acc[:, sl]
            yield sl, (t * cos + pltpu.roll(t, LANES // 2, 1) * sin).astype(BF16)

    @pl.when(j < 3)
    def _():
        acc = jnp.dot(h_nat[...], w_ref[...], preferred_element_type=F32)
        for sl, blk in rope_blocks(acc, cos_ref[...], sin_ref[...]):
            qkv0_ref[:, sl] = blk

    @pl.when(jnp.logical_and(j >= 3, j < 3 * N_DIL))
    def _():
        acc = jnp.dot(h_perm[...], w_ref[...], preferred_element_type=F32)

        def table(ref):
            return jnp.concatenate([ref[:, u * sub_rows:(u + 1) * sub_rows, :].reshape(PERM_ROWS, LANES)
                                    for u in range(n_sub)], axis=0)

        for sl, blk in rope_blocks(acc, table(cosp_ref), table(sinp_ref)):
            for u in range(n_sub):
                qkv12_ref[:, u * sub_rows:(u + 1) * sub_rows, sl] = (
                    blk[u * PERM_ROWS:(u + 1) * PERM_ROWS].reshape(STREAMS, sub_rows, LANES))

    @pl.when(j == 3 * N_DIL)
    def _():
        z_ref[...] = jnp.dot(h_nat[...], w_ref[...], preferred_element_type=F32).astype(BF16)


def _attn_in(x, mod, g, w, tables, tm=1024):
    bsz, s, _ = x.shape
    cos, sin, cosp, sinp = tables
    rows = tm // STREAMS
    n_col = 3 * N_DIL + 1
    dst = np.arange(PERM_ROWS)
    sub_rows = PERM_ROWS // STREAMS
    perm = np.zeros((PERM_ROWS, PERM_ROWS), np.float32)
    perm[dst, STREAMS * (dst % sub_rows) + dst // sub_rows] = 1.0
    perm = jnp.asarray(perm, BF16)
    return pl.pallas_call(
        _attn_in_kernel,
        out_shape=(jax.ShapeDtypeStruct((bsz, s, 3 * ATTN_WIDTH), BF16),
                   jax.ShapeDtypeStruct((bsz, STREAMS, s // STREAMS, 6 * ATTN_WIDTH), BF16),
                   jax.ShapeDtypeStruct((bsz, s, ATTN_WIDTH), BF16)),
        grid=(bsz, s // tm, n_col),
        in_specs=[
            pl.BlockSpec((None, tm, D_MODEL), lambda b, i, j: (b, i, 0)),
            pl.BlockSpec((None, 1, D_MODEL), lambda b, i, j: (b, 0, 0)),
            pl.BlockSpec((None, 1, D_MODEL), lambda b, i, j: (b, 0, 1)),
            pl.BlockSpec((1, D_MODEL), lambda b, i, j: (0, 0)),
            pl.BlockSpec((D_MODEL, ATTN_WIDTH), lambda b, i, j: (0, j)),
            pl.BlockSpec((tm, LANES), lambda b, i, j: (i, 0)),
            pl.BlockSpec((tm, LANES), lambda b, i, j: (i, 0)),
            pl.BlockSpec((STREAMS, rows, LANES), lambda b, i, j: (0, i, 0)),
            pl.BlockSpec((STREAMS, rows, LANES), lambda b, i, j: (0, i, 0)),
            pl.BlockSpec((PERM_ROWS, PERM_ROWS), lambda b, i, j: (0, 0)),
        ],
        out_specs=(
            pl.BlockSpec((None, tm, ATTN_WIDTH), lambda b, i, j: (b, i, jnp.clip(j, 0, 2))),
            pl.BlockSpec((None, STREAMS, rows, ATTN_WIDTH),
                         lambda b, i, j: (b, 0, i, jnp.clip(j - 3, 0, 5))),
            pl.BlockSpec((None, tm, ATTN_WIDTH), lambda b, i, j: (b, i, 0)),
        ),
        scratch_shapes=[pltpu.VMEM((tm, D_MODEL), BF16), pltpu.VMEM((tm, D_MODEL), BF16)],
        compiler_params=pltpu.CompilerParams(
            dimension_semantics=("parallel", "parallel", "arbitrary"),
            vmem_limit_bytes=VMEM_LIMIT),
        name="attn_in",
    )(x, mod, mod, g, w, cos, sin, cosp, sinp, perm)


def _lane_of_odd_head(lane):
    return (lane // (HEAD_DIM // 2)) % 2 == 1


def _mask_bias(valid):
    return jnp.where(valid, 0.0, NEG_INF).astype(F32)


def _attn_pairs(get_q, get_k, get_v, bias, put_acc):
    tq = bias.shape[0]
    lane = lax.broadcasted_iota(jnp.int32, (tq, LANES), 1)
    q_is_odd = _lane_of_odd_head(lane)
    mx_tile = jnp.zeros((tq, LANES), F32)
    den_tile = jnp.ones((tq, LANES), F32)
    ones = jnp.ones((bias.shape[1], LANES), BF16)
    bias2 = jnp.concatenate([bias, bias], axis=0)
    for b in range(ATTN_WIDTH // LANES):
        sl = slice(b * LANES, (b + 1) * LANES)
        qb = get_q(sl).astype(F32)
        kb = get_k(sl)
        vb = jnp.concatenate([get_v(sl), ones], axis=1)
        qm = jnp.concatenate([jnp.where(q_is_odd, 0.0, qb), jnp.where(q_is_odd, qb, 0.0)], axis=0).astype(BF16)
        sc = lax.dot_general(qm, kb, (((1,), (1,)), ((), ())), preferred_element_type=F32) + bias2
        mx = jnp.max(sc, axis=1, keepdims=True)
        p = jnp.exp2((sc - mx).astype(BF16))
        acc = jnp.dot(p, vb, preferred_element_type=F32)
        for odd in (0, 1):
            rows = slice(odd * tq, (odd + 1) * tq)
            mx_tile = jnp.where(lane == 2 * b + odd, mx[rows], mx_tile)
            den_tile = jnp.where(lane == 2 * b + odd, acc[rows, LANES:], den_tile)
        put_acc(sl, jnp.where(q_is_odd, acc[tq:, :LANES], acc[:tq, :LANES]))
    return mx_tile, den_tile


def _pair_factor(tile, b):
    lane = lax.broadcasted_iota(jnp.int32, tile.shape, 1)
    return jnp.where(_lane_of_odd_head(lane), tile[:, 2 * b + 1:2 * b + 2], tile[:, 2 * b:2 * b + 1])


def _merge_groups(mx, den, pmx, pden, get_acc, get_prev, put_o, normalise):
    top = jnp.maximum(mx, pmx)
    wc, wp = jnp.exp2(mx - top), jnp.exp2(pmx - top)
    new_den = wc * den + wp * pden
    if normalise:
        inv = 1.0 / new_den
        wc, wp = wc * inv, wp * inv
    for b in range(ATTN_WIDTH // LANES):
        sl = slice(b * LANES, (b + 1) * LANES)
        put_o(sl, _pair_factor(wc, b) * get_acc(sl) + _pair_factor(wp, b) * get_prev(sl))
    return top, new_den


def _window_mask(tile, tq, m):
    nk = tq + 2 * RADIUS
    row = lax.broadcasted_iota(jnp.int32, (tq, nk), 0)
    col = lax.broadcasted_iota(jnp.int32, (tq, nk), 1)
    kabs = tile * tq - RADIUS + col
    return _mask_bias((jnp.abs(col - RADIUS - row) <= RADIUS) & (kabs >= 0) & (kabs < m))


def _stack_rows(read, prev_ref, cur_ref, next_ref, lo, cnt):
    halo, cur_n = prev_ref.shape[-2], cur_ref.shape[-2]
    a, b = lo, lo + cnt
    parts = []
    if a < 0:
        parts.append(read(prev_ref, halo + a, halo))
        a = 0
    parts.append(read(cur_ref, a, min(b, cur_n)))
    if b > cur_n:
        parts.append(read(next_ref, 0, b - cur_n))
    return parts


def _kv_getters(stack):
    get_k = lambda sl: stack(sl)
    get_v = lambda sl: stack(slice(ATTN_WIDTH + sl.start, ATTN_WIDTH + sl.stop))
    return get_k, get_v


def _attn16_kernel(q_ref, kvp_ref, kvc_ref, kvn_ref, o_ref, mx_ref, den_ref, *, m):
    tq = ATTN_TQ
    n_sub = q_ref.shape[0] // tq
    for t in range(n_sub):
        rows = slice(t * tq, (t + 1) * tq)
        bias = _window_mask(pl.program_id(2) * n_sub + t, tq, m)

        def stack(cols):
            return jnp.concatenate(_stack_rows(lambda ref, a, b: ref[a:b, cols], kvp_ref, kvc_ref, kvn_ref,
                                               t * tq - RADIUS, tq + 2 * RADIUS), axis=0)

        def put_acc(sl, val):
            o_ref[rows, sl] = val

        mx_ref[rows], den_ref[rows] = _attn_pairs(lambda sl: q_ref[rows, sl], *_kv_getters(stack), bias, put_acc)


def _attn4_kernel(q_ref, kvp_ref, kvc_ref, kvn_ref, po_ref, pmx_ref, pden_ref, o_ref, mx_ref, den_ref, *, m):
    ns, halo = q_ref.shape[0], kvp_ref.shape[1]
    rows = ATTN_TQ // ns
    span = rows + 2 * halo
    n_sub = q_ref.shape[1] // rows
    n = lax.broadcasted_iota(jnp.int32, (ns * rows, ns * span), 0)
    c = lax.broadcasted_iota(jnp.int32, (ns * rows, ns * span), 1)
    for t in range(n_sub):
        base = (pl.program_id(2) * n_sub + t) * rows
        rs = slice(t * rows, (t + 1) * rows)
        krow = base - halo + c % span
        rel = ns * (krow - (base + n % rows)) + (c // span - n // rows)
        bias = _mask_bias((jnp.abs(rel) <= RADIUS) & (krow >= 0) & (krow < m))

        def stack(cols):
            parts = []
            for s in range(ns):
                parts += _stack_rows(lambda ref, a, b: ref[s, a:b, cols], kvp_ref, kvc_ref, kvn_ref,
                                     t * rows - halo, span)
            return jnp.concatenate(parts, axis=0)

        flat = lambda ref: (lambda sl: ref[:, rs, sl].reshape(ns * rows, LANES))

        def put_o(sl, val):
            o_ref[:, rs, sl] = val.reshape(ns, rows, LANES)

        mx, den = _attn_pairs(flat(q_ref), *_kv_getters(stack), bias, put_o)
        stat = slice(0, LANES)
        mx, den = _merge_groups(mx, den, flat(pmx_ref)(stat), flat(pden_ref)(stat),
                                flat(o_ref), flat(po_ref), put_o, normalise=False)
        mx_ref[:, rs, :] = mx.reshape(ns, rows, LANES)
        den_ref[:, rs, :] = den.reshape(ns, rows, LANES)


def _attn1_kernel(q_ref, kvp_ref, kvc_ref, kvn_ref, po_ref, pmx_ref, pden_ref, o_ref, acc_scr, *, m):
    tq = ATTN_TQ
    n_sub = q_ref.shape[0] // tq
    srow = tq // STREAMS
    for t in range(n_sub):
        rows = slice(t * tq, (t + 1) * tq)
        bias = _window_mask(pl.program_id(1) * n_sub + t, tq, m)

        def stack(cols):
            return jnp.concatenate(_stack_rows(lambda ref, a, b: ref[a:b, cols], kvp_ref, kvc_ref, kvn_ref,
                                               t * tq - RADIUS, tq + 2 * RADIUS), axis=0)

        def natural(ref, sl):
            return jnp.concatenate([ref[:, t * srow + ii, sl] for ii in range(srow)], axis=0)

        def put_acc(sl, val):
            acc_scr[:, sl] = val

        def put_o(sl, val):
            o_ref[rows, sl] = val.astype(BF16)

        mx, den = _attn_pairs(lambda sl: q_ref[rows, sl], *_kv_getters(stack), bias, put_acc)
        stat = slice(0, LANES)
        _merge_groups(mx, den, natural(pmx_ref, stat), natural(pden_ref, stat),
                      lambda sl: acc_scr[:, sl], lambda sl: natural(po_ref, sl), put_o, normalise=True)


def _attention(qkv0, qkv12):
    bsz, s, _ = qkv0.shape
    m16 = s // STREAMS
    tq, half = ATTN_TQ, RADIUS
    cur_rows = min(ATTN_SUB * tq, m16)
    params = lambda n: pltpu.CompilerParams(dimension_semantics=("parallel",) * n,
                                            vmem_limit_bytes=VMEM_LIMIT)
    same = lambda i: i

    def halo_idx(cur_n, halo_n, total):
        per = cur_n // halo_n
        return (lambda i: jnp.maximum(i * per - 1, 0)), (lambda i: jnp.minimum((i + 1) * per, total // halo_n - 1))

    def spec16(nrows, row_idx, col, width=ATTN_WIDTH):
        return pl.BlockSpec((None, None, nrows, width), lambda b, r, i: (b, r, row_idx(i), col))
    prv, nxt = halo_idx(cur_rows, half, m16)
    stat16 = jax.ShapeDtypeStruct((bsz, STREAMS, m16, LANES), F32)
    o2, mx2, den2 = pl.pallas_call(
        functools.partial(_attn16_kernel, m=m16),
        out_shape=(jax.ShapeDtypeStruct((bsz, STREAMS, m16, ATTN_WIDTH), F32), stat16, stat16),
        grid=(bsz, STREAMS, m16 // cur_rows),
        in_specs=[spec16(cur_rows, same, 5), spec16(half, prv, 1, 2 * ATTN_WIDTH),
                  spec16(cur_rows, same, 1, 2 * ATTN_WIDTH), spec16(half, nxt, 1, 2 * ATTN_WIDTH)],
        out_specs=(spec16(cur_rows, same, 0), spec16(cur_rows, same, 0, LANES), spec16(cur_rows, same, 0, LANES)),
        compiler_params=params(3), name="attn_dil16",
    )(*([qkv12] * 4))

    ns = STREAMS // 4
    rows4, halo4 = cur_rows // ns, half // ns

    def view4(a):
        return a.reshape(bsz, ns, 4, m16, a.shape[-1])

    def spec4(nrows, row_idx, col, width=ATTN_WIDTH):
        return pl.BlockSpec((None, ns, None, nrows, width), lambda b, r, i: (b, 0, r, row_idx(i), col))
    prv4, nxt4 = halo_idx(rows4, halo4, m16)
    q4 = view4(qkv12)
    stat4 = jax.ShapeDtypeStruct((bsz, ns, 4, m16, LANES), F32)
    run4 = [spec4(rows4, same, 0), spec4(rows4, same, 0, LANES), spec4(rows4, same, 0, LANES)]
    o12, mx12, den12 = pl.pallas_call(
        functools.partial(_attn4_kernel, m=m16),
        out_shape=(jax.ShapeDtypeStruct((bsz, ns, 4, m16, ATTN_WIDTH), F32), stat4, stat4),
        grid=(bsz, 4, m16 // rows4),
        in_specs=[spec4(rows4, same, 4), spec4(halo4, prv4, 0, 2 * ATTN_WIDTH),
                  spec4(rows4, same, 0, 2 * ATTN_WIDTH), spec4(halo4, nxt4, 0, 2 * ATTN_WIDTH)] + run4,
        out_specs=tuple(run4),
        compiler_params=params(3), name="attn_dil4",
    )(q4, q4, q4, q4, view4(o2), view4(mx2), view4(den2))
    streams = lambda a: a.reshape(bsz, STREAMS, m16, a.shape[-1])

    cur1 = ATTN_SUB * tq

    def spec1(nrows, row_idx, col, width=ATTN_WIDTH):
        return pl.BlockSpec((None, nrows, width), lambda b, i: (b, row_idx(i), col))
    prv1, nxt1 = halo_idx(cur1, half, s)
    sm = lambda width: pl.BlockSpec((None, STREAMS, cur1 // STREAMS, width), lambda b, i: (b, 0, i, 0))
    return pl.pallas_call(
        functools.partial(_attn1_kernel, m=s),
        out_shape=jax.ShapeDtypeStruct((bsz, s, ATTN_WIDTH), BF16),
        grid=(bsz, s // cur1),
        in_specs=[spec1(cur1, same, 2), spec1(half, prv1, 0, 2 * ATTN_WIDTH),
                  spec1(cur1, same, 0, 2 * ATTN_WIDTH), spec1(half, nxt1, 0, 2 * ATTN_WIDTH),
                  sm(ATTN_WIDTH), sm(LANES), sm(LANES)],
        out_specs=spec1(cur1, same, 0),
        scratch_shapes=[pltpu.VMEM((tq, ATTN_WIDTH), F32)],
        compiler_params=params(2), name="attn_dil1",
    )(*([qkv0] * 4), streams(o12), streams(mx12), streams(den12))


def _attn_out_kernel(o_ref, z_ref, x_ref, gate_ref, w_ref, out_ref):
    y = (o_ref[...].astype(F32) * _silu(z_ref[...].astype(F32))).astype(BF16)
    out = jnp.dot(y, w_ref[...], preferred_element_type=F32)
    out_ref[...] = x_ref[...] + gate_ref[...] * out


def _attn_out(o, z, x, mod, w, tm=512):
    bsz, s, _ = x.shape
    row = pl.BlockSpec((None, tm, D_MODEL), lambda b, i: (b, i, 0))
    return pl.pallas_call(
        _attn_out_kernel,
        out_shape=jax.ShapeDtypeStruct(x.shape, F32),
        grid=(bsz, s // tm),
        in_specs=[row, row, row,
                  pl.BlockSpec((None, 1, D_MODEL), lambda b, i: (b, 0, 2)),
                  pl.BlockSpec((ATTN_WIDTH, D_MODEL), lambda b, i: (0, 0))],
        out_specs=row,
        compiler_params=pltpu.CompilerParams(
            dimension_semantics=("parallel", "parallel"),
            vmem_limit_bytes=VMEM_LIMIT),
        name="attn_out",
    )(o, z, x, mod, w)


def _cmul(ar, ai, br, bi):
    return ar * br - ai * bi, ar * bi + ai * br


def _s5_prep_kernel(lam_re_a, lam_im_a, lam_re_b, lam_im_b, logdt, bt_re, bt_im,
                    c_re, c_im, toep_ref, mb_ref, mc_ref, al_ref, v_scr):
    lane = lax.broadcasted_iota(jnp.int32, (STATE, LANES), 1)
    a_idx = (lane // GROUP_CH).astype(F32)
    n_tiles = CHUNK // T_PER_VREG

    mb_rows = []
    v_rhs = []
    v_lhs = []
    mc_cols = []
    zbar_f = None
    for dr in range(2):
        dt = jnp.exp(logdt[dr])
        lr = jnp.minimum(lam_re_a[dr], LAMBDA_RE_MAX)
        li = lam_im_a[dr]
        ldr, ldi = lr * dt, li * dt

        def powers(e):
            mag = jnp.exp(e * ldr)
            return mag * jnp.cos(e * ldi), mag * jnp.sin(e * ldi)

        l1r, l1i = powers(1.0)
        den = lr * lr + li * li
        nr, ni = l1r - 1.0, l1i
        cr, ci = (nr * lr + ni * li) / den, (ni * lr - nr * li) / den
        bbr, bbi = _cmul(cr, ci, bt_re[dr], bt_im[dr])
        l8r, l8i = powers(float(T_PER_VREG))
        tiles = [None] * n_tiles
        if dr == 0:
            pr, pi = powers(float(T_PER_VREG - 1) - a_idx)
            cur = _cmul(pr, pi, bbr, bbi)
            for jt in range(n_tiles - 1, -1, -1):
                tiles[jt] = cur
                if jt:
                    cur = _cmul(cur[0], cur[1], l8r, l8i)
            v_rhs.append(_cmul(tiles[n_tiles - 1][0], tiles[n_tiles - 1][1], l1r, l1i))
            zbar_f = (jnp.where(lane < GROUP_CH, bbr, 0.0), jnp.where(lane < GROUP_CH, bbi, 0.0))
        else:
            pr, pi = powers(a_idx)
            cur = _cmul(pr, pi, bbr, bbi)
            for jt in range(n_tiles):
                tiles[jt] = cur
                if jt < n_tiles - 1:
                    cur = _cmul(cur[0], cur[1], l8r, l8i)
            v_rhs.append(tiles[0])
        mb_rows.append(jnp.concatenate([t[0] for t in tiles], axis=1))
        mb_rows.append(jnp.concatenate([t[1] for t in tiles], axis=1))
        a_r, a_i = l8r, l8i
        for _ in range(int(math.log2(CHUNK // T_PER_VREG))):
            a_r, a_i = _cmul(a_r, a_i, a_r, a_i)
        al_ref[2 * dr] = jnp.broadcast_to(a_r, (STATE, LANES))
        al_ref[2 * dr + 1] = jnp.broadcast_to(a_i, (STATE, LANES))

        lrb = jnp.minimum(lam_re_b[dr], LAMBDA_RE_MAX)
        lib = lam_im_b[dr]
        mag = jnp.exp(lrb * dt)
        pw_r, pw_i = mag * jnp.cos(lib * dt), mag * jnp.sin(lib * dt)
        pows = [(pw_r, pw_i)]
        for _ in range(int(math.log2(CHUNK)) - 1):
            pows.append(_cmul(*pows[-1], *pows[-1]))
        cre, cim = c_re[dr], c_im[dr]
        xr, xi = _cmul(cre, cim, *pows[0])
        for k in range(int(math.log2(CHUNK))):
            yr, yi = _cmul(xr, xi, *pows[k])
            if dr == 0:
                xr, xi = jnp.concatenate([xr, yr], 0), jnp.concatenate([xi, yi], 0)
            else:
                xr, xi = jnp.concatenate([yr, xr], 0), jnp.concatenate([yi, xi], 0)
        mc_cols += [xr, -xi]
        xr, xi = cre, cim
        for k in range(int(math.log2(n_tiles))):
            yr, yi = _cmul(xr, xi, *pows[k + int(math.log2(T_PER_VREG))])
            if dr == 0:
                xr, xi = jnp.concatenate([yr, xr], 0), jnp.concatenate([yi, xi], 0)
            else:
                xr, xi = jnp.concatenate([xr, yr], 0), jnp.concatenate([xi, yi], 0)
        v_lhs.append((xr, xi))

    mb_ref[...] = jnp.concatenate(mb_rows, axis=0).astype(BF16)
    mc_ref[...] = jnp.concatenate(mc_cols, axis=1).astype(BF16)

    def cdot(lhs, rhs):
        return (jnp.dot(lhs[0], rhs[0], precision=HI, preferred_element_type=F32)
                - jnp.dot(lhs[1], rhs[1], precision=HI, preferred_element_type=F32))

    out_f = cdot(v_lhs[0], v_rhs[0])
    zero_rows = jnp.zeros((CHUNK_ROWS // T_PER_VREG - GROUP_CH, STATE), F32)
    lag0_lhs = (jnp.concatenate([c_re[0], zero_rows], 0), jnp.concatenate([c_im[0], zero_rows], 0))
    out_b = cdot(v_lhs[1], v_rhs[1]) + cdot(lag0_lhs, zbar_f)
    for jt in range(n_tiles):
        rows = slice(jt * GROUP_CH, (jt + 1) * GROUP_CH)
        v_scr[:, jt * LANES:(jt + 1) * LANES] = out_f[rows]
        v_scr[:, (n_tiles + jt) * LANES:(n_tiles + jt + 1) * LANES] = out_b[rows]
    v_scr[:, 2 * CHUNK_ROWS:] = jnp.zeros((GROUP_CH, LANES), F32)

    for k in range(T_PER_VREG):
        vk = v_scr[:, k * GROUP_CH:k * GROUP_CH + 2 * CHUNK_ROWS]
        for t_out in range(CHUNK):
            off = (CHUNK - t_out) * GROUP_CH
            if off % LANES == k * GROUP_CH:
                al_off = off - k * GROUP_CH
                toep_ref[t_out * GROUP_CH:(t_out + 1) * GROUP_CH, :] = (
                    vk[:, al_off:al_off + CHUNK_ROWS].astype(BF16))


def _s5_prep(lam_re, lam_im, log_dt, b_re, b_im, c_re, c_im):
    g = SSM_GROUPS
    lam_a = lambda a: a.reshape(2, g, STATE, 1)
    lam_b = lambda a: a.reshape(2, g, 1, STATE)
    tile_b = lambda a: jnp.tile(a, (1, 1, 1, T_PER_VREG))
    dir_spec = lambda r, c: pl.BlockSpec((2, None, r, c), lambda i: (0, i, 0, 0))
    return pl.pallas_call(
        _s5_prep_kernel,
        out_shape=(jax.ShapeDtypeStruct((g, CHUNK_ROWS, CHUNK_ROWS), BF16),
                   jax.ShapeDtypeStruct((g, 4 * STATE, CHUNK_ROWS), BF16),
                   jax.ShapeDtypeStruct((g, CHUNK_ROWS, 4 * STATE), BF16),
                   jax.ShapeDtypeStruct((g, 4, STATE, LANES), F32)),
        grid=(g,),
        in_specs=[dir_spec(STATE, 1), dir_spec(STATE, 1), dir_spec(1, STATE), dir_spec(1, STATE),
                  dir_spec(1, 1), dir_spec(STATE, LANES), dir_spec(STATE, LANES),
                  dir_spec(GROUP_CH, STATE), dir_spec(GROUP_CH, STATE)],
        out_specs=(pl.BlockSpec((None, CHUNK_ROWS, CHUNK_ROWS), lambda i: (i, 0, 0)),
                   pl.BlockSpec((None, 4 * STATE, CHUNK_ROWS), lambda i: (i, 0, 0)),
                   pl.BlockSpec((None, CHUNK_ROWS, 4 * STATE), lambda i: (i, 0, 0)),
                   pl.BlockSpec((None, 4, STATE, LANES), lambda i: (i, 0, 0, 0))),
        scratch_shapes=[pltpu.VMEM((GROUP_CH, V_LANES), F32)],
        compiler_params=pltpu.CompilerParams(dimension_semantics=("parallel",)),
        name="s5_prep",
    )(lam_a(lam_re), lam_a(lam_im), lam_b(lam_re), lam_b(lam_im), log_dt.reshape(2, g, 1, 1),
      tile_b(b_re), tile_b(b_im), c_re, c_im)


def _slab_perm():
    dst = np.arange(SLAB)
    perm = np.zeros((SLAB, SLAB), np.float32)
    perm[dst, (dst % SLAB_CHUNKS) * SUBLANES + dst // SLAB_CHUNKS] = 1.0
    return perm


def _s5_in_kernel(x_ref, mod_ref, g_ref, w_ref, p_ref, u_ref, z_ref, h_scr, *, chunks_per_seq):
    n_chunks = x_ref.shape[0]
    for b in range(n_chunks // chunks_per_seq):
        rows = slice(b * chunks_per_seq, (b + 1) * chunks_per_seq)
        h = _rms_modulate(x_ref[rows], g_ref[...], mod_ref[b:b + 1, D_MODEL:2 * D_MODEL],
                          mod_ref[b:b + 1, :D_MODEL])
        h2 = h.reshape(chunks_per_seq * SUBLANES, D_MODEL).astype(BF16)
        for s in range(chunks_per_seq // SLAB_CHUNKS):
            hp = jnp.dot(p_ref[...], h2[s * SLAB:(s + 1) * SLAB], preferred_element_type=F32)
            c0 = b * chunks_per_seq + s * SLAB_CHUNKS
            for k in range(SUBLANES):
                h_scr[k, c0:c0 + SLAB_CHUNKS, :] = hp[k * SLAB_CHUNKS:(k + 1) * SLAB_CHUNKS].astype(BF16)
    for k in range(SUBLANES):
        ht = h_scr[k].astype(F32).T.astype(BF16)
        uz = jnp.dot(w_ref[...], ht, preferred_element_type=F32)
        u_ref[k] = uz[:SSM_WIDTH].astype(BF16)
        z_ref[k] = uz[SSM_WIDTH:].astype(BF16)


def _s5_in(x3, mod, g, w_t, chunks_per_seq):
    n_chunks = x3.shape[0]
    once = lambda shape: pl.BlockSpec(shape, lambda t: (0,) * len(shape), pipeline_mode=pl.Buffered(1))
    out = pl.BlockSpec((SUBLANES, SSM_WIDTH, n_chunks), lambda t: (t, 0, 0))
    return pl.pallas_call(
        functools.partial(_s5_in_kernel, chunks_per_seq=chunks_per_seq),
        out_shape=(jax.ShapeDtypeStruct((CHUNK, SSM_WIDTH, n_chunks), BF16),) * 2,
        grid=(CHUNK // SUBLANES,),
        in_specs=[pl.BlockSpec((n_chunks, SUBLANES, D_MODEL), lambda t: (0, t, 0)),
                  once(mod.shape), once((1, D_MODEL)), once((2 * SSM_WIDTH, D_MODEL)),
                  once((SLAB, SLAB))],
        out_specs=(out, out),
        scratch_shapes=[pltpu.VMEM((SUBLANES, n_chunks, D_MODEL), BF16)],
        compiler_params=pltpu.CompilerParams(
            dimension_semantics=("parallel",), vmem_limit_bytes=VMEM_LIMIT_BIG),
        name="s5_in",
    )(x3, mod, g, w_t, jnp.asarray(_slab_perm(), BF16))


def _gelu_tanh(v):
    return 0.5 * v * (1.0 + jnp.tanh(math.sqrt(2.0 / math.pi) * (v + 0.044715 * (v * v * v))))


def _s5_core_kernel(u_ref, toep_ref, mb_ref, mc_ref, al_ref, d_ref, o_ref, *, chunks_per_seq):
    n_chunks = u_ref.shape[-1]
    ub = u_ref[...].reshape(CHUNK_ROWS, n_chunks)
    inc = jnp.dot(mb_ref[...], ub, preferred_element_type=F32)
    pos = lax.broadcasted_iota(jnp.int32, (STATE, n_chunks), 1) % chunks_per_seq
    reps = n_chunks // LANES

    def lane_tile(a):
        return jnp.concatenate([a] * reps, axis=1) if reps > 1 else a

    states = []
    for dr in range(2):
        xr = inc[(2 * dr) * STATE:(2 * dr + 1) * STATE]
        xi = inc[(2 * dr + 1) * STATE:(2 * dr + 2) * STATE]
        ar, ai = lane_tile(al_ref[2 * dr]), lane_tile(al_ref[2 * dr + 1])

        def shifted(v, step):
            if dr == 0:
                return jnp.where(pos >= step, pltpu.roll(v, step, 1), 0.0)
            return jnp.where(pos < chunks_per_seq - step, pltpu.roll(v, n_chunks - step, 1), 0.0)

        step = 1
        while step < chunks_per_seq:
            sr, si = shifted(xr, step), shifted(xi, step)
            xr, xi = xr + ar * sr - ai * si, xi + ar * si + ai * sr
            ar, ai = _cmul(ar, ai, ar, ai)
            step *= 2
        states += [shifted(xr, 1), shifted(xi, 1)]
    h_in = jnp.concatenate(states, axis=0).astype(BF16)
    y = (jnp.dot(toep_ref[...], ub, preferred_element_type=F32)
         + jnp.dot(mc_ref[...], h_in, preferred_element_type=F32)
         + d_ref[...] * ub.astype(F32))
    o_ref[...] = _gelu_tanh(y).astype(BF16).reshape(CHUNK, GROUP_CH, n_chunks)


def _s5_core(u_cl, toep, mb, mc, al, d_col, chunks_per_seq):
    n_chunks = u_cl.shape[-1]
    u4 = u_cl.reshape(CHUNK, SSM_GROUPS, GROUP_CH, n_chunks)
    grp = pl.BlockSpec((CHUNK, None, GROUP_CH, n_chunks), lambda g: (0, g, 0, 0))
    out = pl.pallas_call(
        functools.partial(_s5_core_kernel, chunks_per_seq=chunks_per_seq),
        out_shape=jax.ShapeDtypeStruct(u4.shape, BF16),
        grid=(SSM_GROUPS,),
        in_specs=[grp,
                  pl.BlockSpec((None, CHUNK_ROWS, CHUNK_ROWS), lambda g: (g, 0, 0)),
                  pl.BlockSpec((None, 4 * STATE, CHUNK_ROWS), lambda g: (g, 0, 0)),
                  pl.BlockSpec((None, CHUNK_ROWS, 4 * STATE), lambda g: (g, 0, 0)),
                  pl.BlockSpec((None, 4, STATE, LANES), lambda g: (g, 0, 0, 0)),
                  pl.BlockSpec((None, CHUNK_ROWS, 1), lambda g: (g, 0, 0))],
        out_specs=grp,
        compiler_params=pltpu.CompilerParams(
            dimension_semantics=("parallel",), vmem_limit_bytes=VMEM_LIMIT),
        name="s5_core",
    )(u4, toep, mb, mc, al, d_col)
    return out.reshape(CHUNK, SSM_WIDTH, n_chunks)


def _s5_out_kernel(g_ref, z_ref, wglu_ref, wout_ref, pt_ref, x_ref, mod_ref, fg_ref, o_ref, y_scr,
                   *, final, chunks_per_seq):
    half = pl.program_id(1)
    n_half = x_ref.shape[0]

    @pl.when(half == 0)
    def _():
        for k in range(SUBLANES):
            gb = g_ref[k]
            glu = jnp.dot(wglu_ref[...], gb, preferred_element_type=F32)
            y = gb.astype(F32) * (1.0 / (1.0 + jnp.exp(-glu))) * _silu(z_ref[k].astype(F32))
            y_scr[k] = y.T.astype(BF16)

    for hv in range(y_scr.shape[1] // n_half):
        @pl.when(half == hv)
        def _():
            for s in range(n_half // SLAB_CHUNKS):
                c0 = hv * n_half + s * SLAB_CHUNKS
                src = jnp.concatenate([y_scr[k, c0:c0 + SLAB_CHUNKS, :] for k in range(SUBLANES)], axis=0)
                yp = jnp.dot(pt_ref[...], src, preferred_element_type=F32).astype(BF16)
                out = jnp.dot(yp, wout_ref[...], preferred_element_type=F32)
                b = c0 // chunks_per_seq
                upd = mod_ref[b:b + 1, 2 * D_MODEL:] * out
                rows = slice(s * SLAB_CHUNKS, (s + 1) * SLAB_CHUNKS)
                xn = x_ref[rows] + upd.reshape(SLAB_CHUNKS, SUBLANES, D_MODEL)
                if final:
                    xn = xn * lax.rsqrt(jnp.mean(xn * xn, axis=-1, keepdims=True) + NORM_EPS) * fg_ref[...]
                o_ref[rows] = xn


def _s5_out(g_cl, z_cl, wglu_t, w_out, x3, mod, final_g, final, chunks_per_seq, n_split=2):
    n_chunks = x3.shape[0]
    n_half = n_chunks // n_split
    once = lambda shape: pl.BlockSpec(shape, lambda t, h: (0,) * len(shape), pipeline_mode=pl.Buffered(1))
    act = pl.BlockSpec((SUBLANES, SSM_WIDTH, n_chunks), lambda t, h: (t, 0, 0))
    row = pl.BlockSpec((n_half, SUBLANES, D_MODEL), lambda t, h: (h, t, 0))
    return pl.pallas_call(
        functools.partial(_s5_out_kernel, final=final, chunks_per_seq=chunks_per_seq),
        out_shape=jax.ShapeDtypeStruct(x3.shape, F32),
        grid=(CHUNK // SUBLANES, n_split),
        in_specs=[act, act, once((SSM_WIDTH, SSM_WIDTH)), once((SSM_WIDTH, D_MODEL)), once((SLAB, SLAB)),
                  row, once(mod.shape), once((1, D_MODEL))],
        out_specs=row,
        scratch_shapes=[pltpu.VMEM((SUBLANES, n_chunks, SSM_WIDTH), BF16)],
        compiler_params=pltpu.CompilerParams(
            dimension_semantics=("parallel", "arbitrary"), vmem_limit_bytes=VMEM_LIMIT_BIG),
        name="s5_out",
    )(g_cl, z_cl, wglu_t, w_out, jnp.asarray(_slab_perm().T, BF16), x3, mod, final_g)


def _rope_tables(s):
    inv_freq = ROPE_THETA ** (-jnp.arange(0, HEAD_DIM, 2, dtype=F32) / HEAD_DIM)
    ang = jnp.arange(s, dtype=F32)[:, None] * inv_freq[None, :]
    reps = LANES // (HEAD_DIM // 2)
    sign = jnp.where(jnp.arange(LANES) < LANES // 2, -1.0, 1.0).astype(F32)
    cos, sin = jnp.tile(jnp.cos(ang), (1, reps)), jnp.tile(jnp.sin(ang), (1, reps)) * sign
    stream_major = lambda a: a.reshape(s // STREAMS, STREAMS, LANES).transpose(1, 0, 2)
    return cos, sin, stream_major(cos), stream_major(sin)


def _prep_attn_w_in(w):
    d_in = w.shape[0]
    w = w.astype(BF16).reshape(d_in, -1, ATTN_HEADS // 2, 2, 2, HEAD_DIM // 2)
    w = w.transpose(0, 1, 2, 4, 3, 5)
    return w[:, np.array(ATTN_IN_BLOCK_ORDER)].reshape(d_in, -1)


def _prep_attn_w_out(w):
    w = w.astype(BF16).reshape(ATTN_HEADS // 2, 2, 2, HEAD_DIM // 2, w.shape[1])
    return w.transpose(0, 2, 1, 3, 4).reshape(ATTN_WIDTH, -1)


def _trunk(x, ada, norm_g, attn_w, s5_w, s5_ops, final_norm_g):
    bsz, s, _ = x.shape
    chunks_per_seq = s // CHUNK
    n_chunks = bsz * chunks_per_seq
    tables = _rope_tables(s)
    fg = final_norm_g.reshape(1, D_MODEL)
    for i in range(DEPTH):
        mod = ada[i].reshape(bsz, 1, 3 * D_MODEL)
        g = norm_g[i].reshape(1, D_MODEL)
        j = i // N_MIXERS
        if i % N_MIXERS == 0:
            w_in, w_out = attn_w[j]
            qkv0, qkv12, z = _attn_in(x, mod, g, w_in, tables)
            x = _attn_out(_attention(qkv0, qkv12), z, x, mod, w_out)
        else:
            w_in_t, wglu_t, w_out, d_col = s5_w[j]
            toep, mb, mc, al = s5_ops[j]
            x3 = x.reshape(n_chunks, CHUNK, D_MODEL)
            u_cl, z_cl = _s5_in(x3, ada[i], g, w_in_t, chunks_per_seq)
            g_cl = _s5_core(u_cl, toep, mb, mc, al, d_col, chunks_per_seq)
            x3 = _s5_out(g_cl, z_cl, wglu_t, w_out, x3, ada[i], fg, i == DEPTH - 1, chunks_per_seq)
            x = x3.reshape(bsz, s, D_MODEL)
    return x


def kernel(x_prompt, x_sample, c_prompt, c_sample, norm_g, ada_w, ada_b, attn_w_in, attn_w_out,
           ssm_w_in, ssm_lam_re, ssm_lam_im, ssm_log_dt, ssm_b_re, ssm_b_im, ssm_c_re, ssm_c_im,
           ssm_d, ssm_w_glu, ssm_w_out, final_norm_g):
    assert (DEPTH - 1) % N_MIXERS == 1, "the final norm is fused into the last S5 layer"
    assert all(w // (2 * d) == RADIUS for w, d in DILATED_PAIRS)
    assert [d for _, d in DILATED_PAIRS] == [1, 4, STREAMS]
    n_prompt = c_prompt.shape[0]
    ada = _ada(jnp.concatenate([c_prompt, c_sample], axis=0), ada_w, ada_b)
    attn_w = [(_prep_attn_w_in(attn_w_in[j]), _prep_attn_w_out(attn_w_out[j]))
              for j in range(attn_w_in.shape[0])]
    s5_w, s5_ops = [], []
    for j in range(ssm_w_in.shape[0]):
        d_col = jnp.tile(ssm_d[j].reshape(SSM_GROUPS, 1, GROUP_CH), (1, CHUNK, 1))
        s5_w.append((ssm_w_in[j].T.astype(BF16), ssm_w_glu[j].T.astype(BF16),
                     ssm_w_out[j].astype(BF16), d_col.reshape(SSM_GROUPS, CHUNK_ROWS, 1)))
        s5_ops.append(_s5_prep(ssm_lam_re[j], ssm_lam_im[j], ssm_log_dt[j], ssm_b_re[j],
                               ssm_b_im[j], ssm_c_re[j], ssm_c_im[j]))
    y_prompt = _trunk(x_prompt, ada[:, :n_prompt], norm_g, attn_w, s5_w, s5_ops, final_norm_g)
    y_sample = _trunk(x_sample, ada[:, n_prompt:], norm_g, attn_w, s5_w, s5_ops, final_norm_g)
    return (y_prompt, y_sample)
```

```python
import functools
import math

import numpy as np
import jax
import jax.numpy as jnp
from jax import lax
from jax.experimental import pallas as pl
from jax.experimental.pallas import tpu as pltpu

D_MODEL = 1024
DEPTH = 4
N_MIXERS = 2
ATTN_HEADS = 16
HEAD_DIM = 64
ATTN_WIDTH = ATTN_HEADS * HEAD_DIM
DILATED_PAIRS = ((128, 1), (512, 4), (2048, 16))
N_DIL = len(DILATED_PAIRS)
ROPE_THETA = 10000.0
SSM_WIDTH = D_MODEL
GROUP_CH = 16
SSM_GROUPS = SSM_WIDTH // GROUP_CH
STATE = 64
LAMBDA_RE_MAX = -1e-4
NORM_EPS = 1e-6
NEG_INF = -1e30

LANES = 128
SUBLANES = 8
CHUNK = 64
CHUNK_ROWS = CHUNK * GROUP_CH
T_PER_VREG = LANES // GROUP_CH
V_LANES = 2 * CHUNK_ROWS + LANES
STREAMS = 16
RADIUS = 64
ATTN_TQ = 128
ATTN_SUB = 4
ATTN_IN_BLOCK_ORDER = (1, 2, 0, 4, 5, 7, 8, 3, 6, 9)
PERM_ROWS = 512
LOG2_E = math.log2(math.e)
SLAB = 256
SLAB_CHUNKS = SLAB // SUBLANES
VMEM_LIMIT = 48 * 1024 * 1024
VMEM_LIMIT_BIG = 56 * 1024 * 1024

F32 = jnp.float32
BF16 = jnp.bfloat16
HI = lax.Precision.HIGHEST


def _silu(v):
    return v * (1.0 / (1.0 + jnp.exp(-v)))


def _rms_modulate(x, g, scale, shift):
    rs = lax.rsqrt(jnp.mean(x * x, axis=-1, keepdims=True) + NORM_EPS)
    return (x * rs * g) * (1.0 + scale) + shift


def _ada_kernel(c_ref, w_ref, b_ref, o_ref):
    o_ref[...] = jnp.dot(_silu(c_ref[...]), w_ref[...], precision=HI,
                         preferred_element_type=F32) + b_ref[...]


def _ada(c, ada_w, ada_b):
    nb = c.shape[0]
    return pl.pallas_call(
        _ada_kernel,
        out_shape=jax.ShapeDtypeStruct((DEPTH, nb, 3 * D_MODEL), F32),
        grid=(DEPTH, 3),
        in_specs=[
            pl.BlockSpec((nb, D_MODEL), lambda i, j: (0, 0)),
            pl.BlockSpec((None, D_MODEL, D_MODEL), lambda i, j: (i, 0, j)),
            pl.BlockSpec((None, 1, D_MODEL), lambda i, j: (i, 0, j)),
        ],
        out_specs=pl.BlockSpec((None, nb, D_MODEL), lambda i, j: (i, 0, j)),
        name="ada",
    )(c, ada_w, ada_b.reshape(DEPTH, 1, 3 * D_MODEL))


def _attn_in_kernel(x_ref, shift_ref, scale_ref, g_ref, w_ref, cos_ref, sin_ref, cosp_ref, sinp_ref,
                    perm_ref, qkv0_ref, qkv12_ref, z_ref, h_nat, h_perm):
    j = pl.program_id(2)
    tm = x_ref.shape[0]
    n_sub = tm // PERM_ROWS
    sub_rows = PERM_ROWS // STREAMS

    @pl.when(j == 0)
    def _():
        h = _rms_modulate(x_ref[...], g_ref[...], scale_ref[...], shift_ref[...])
        h_nat[...] = h.astype(BF16)
        for u in range(n_sub):
            rs = slice(u * PERM_ROWS, (u + 1) * PERM_ROWS)
            h_perm[rs, :] = jnp.dot(perm_ref[...], h_nat[rs, :], preferred_element_type=F32).astype(BF16)

    is_q = jnp.logical_or(j == 2, j >= 7)
    is_v = jnp.logical_or(j == 1, jnp.logical_or(j == 4, j == 6))

    def rope_blocks(acc, cos, sin):
        qscale = jnp.where(is_q, HEAD_DIM ** -0.5 * LOG2_E, 1.0).astype(F32)
        cos = jnp.where(is_v, 1.0, cos * qscale)
        sin = jnp.where(is_v, 0.0, sin * qscale)
        for b in range(ATTN_WIDTH // LANES):
            sl = slice(b * LANES, (b + 1) * LANES)
            t = acc[:, sl]
            yield sl, (t * cos + pltpu.roll(t, LANES // 2, 1) * sin).astype(BF16)

    @pl.when(j < 3)
    def _():
        acc = jnp.dot(h_nat[...], w_ref[...], preferred_element_type=F32)
        for sl, blk in rope_blocks(acc, cos_ref[...], sin_ref[...]):
            qkv0_ref[:, sl] = blk

    @pl.when(jnp.logical_and(j >= 3, j < 3 * N_DIL))
    def _():
        acc = jnp.dot(h_perm[...], w_ref[...], preferred_element_type=F32)

        def table(ref):
            return jnp.concatenate([ref[:, u * sub_rows:(u + 1) * sub_rows, :].reshape(PERM_ROWS, LANES)
                                    for u in range(n_sub)], axis=0)

        for sl, blk in rope_blocks(acc, table(cosp_ref), table(sinp_ref)):
            for u in range(n_sub):
                qkv12_ref[:, u * sub_rows:(u + 1) * sub_rows, sl] = (
                    blk[u * PERM_ROWS:(u + 1) * PERM_ROWS].reshape(STREAMS, sub_rows, LANES))

    @pl.when(j == 3 * N_DIL)
    def _():
        z_ref[...] = jnp.dot(h_nat[...], w_ref[...], preferred_element_type=F32).astype(BF16)


def _stream_perm(n):
    dst = np.arange(n)
    rows = n // STREAMS
    perm = np.zeros((n, n), np.float32)
    perm[dst, STREAMS * (dst % rows) + dst // rows] = 1.0
    return perm


def _attn_in(x, mod, g, w, tables, tm=1024):
    bsz, s, _ = x.shape
    cos, sin, cosp, sinp = tables
    rows = tm // STREAMS
    n_col = 3 * N_DIL + 1
    perm = jnp.asarray(_stream_perm(PERM_ROWS), BF16)
    return pl.pallas_call(
        _attn_in_kernel,
        out_shape=(jax.ShapeDtypeStruct((bsz, s, 3 * ATTN_WIDTH), BF16),
                   jax.ShapeDtypeStruct((bsz, STREAMS, s // STREAMS, 6 * ATTN_WIDTH), BF16),
                   jax.ShapeDtypeStruct((bsz, s, ATTN_WIDTH), BF16)),
        grid=(bsz, s // tm, n_col),
        in_specs=[
            pl.BlockSpec((None, tm, D_MODEL), lambda b, i, j: (b, i, 0)),
            pl.BlockSpec((None, 1, D_MODEL), lambda b, i, j: (b, 0, 0)),
            pl.BlockSpec((None, 1, D_MODEL), lambda b, i, j: (b, 0, 1)),
            pl.BlockSpec((1, D_MODEL), lambda b, i, j: (0, 0)),
            pl.BlockSpec((D_MODEL, ATTN_WIDTH), lambda b, i, j: (0, j)),
            pl.BlockSpec((tm, LANES), lambda b, i, j: (i, 0)),
            pl.BlockSpec((tm, LANES), lambda b, i, j: (i, 0)),
            pl.BlockSpec((STREAMS, rows, LANES), lambda b, i, j: (0, i, 0)),
            pl.BlockSpec((STREAMS, rows, LANES), lambda b, i, j: (0, i, 0)),
            pl.BlockSpec((PERM_ROWS, PERM_ROWS), lambda b, i, j: (0, 0)),
        ],
        out_specs=(
            pl.BlockSpec((None, tm, ATTN_WIDTH), lambda b, i, j: (b, i, jnp.clip(j, 0, 2))),
            pl.BlockSpec((None, STREAMS, rows, ATTN_WIDTH),
                         lambda b, i, j: (b, 0, i, jnp.clip(j - 3, 0, 5))),
            pl.BlockSpec((None, tm, ATTN_WIDTH), lambda b, i, j: (b, i, 0)),
        ),
        scratch_shapes=[pltpu.VMEM((tm, D_MODEL), BF16), pltpu.VMEM((tm, D_MODEL), BF16)],
        compiler_params=pltpu.CompilerParams(
            dimension_semantics=("parallel", "parallel", "arbitrary"),
            vmem_limit_bytes=VMEM_LIMIT),
        name="attn_in",
    )(x, mod, mod, g, w, cos, sin, cosp, sinp, perm)


def _lane_of_odd_head(lane):
    return (lane // (HEAD_DIM // 2)) % 2 == 1


def _mask_bias(valid):
    return jnp.where(valid, 0.0, NEG_INF).astype(F32)


def _attn_pairs(get_q, get_k, get_v, bias, put_acc):
    tq = bias.shape[0]
    lane = lax.broadcasted_iota(jnp.int32, (tq, LANES), 1)
    q_is_odd = _lane_of_odd_head(lane)
    mx_tile = jnp.zeros((tq, LANES), F32)
    den_tile = jnp.ones((tq, LANES), F32)
    ones = jnp.ones((bias.shape[1], LANES), BF16)
    bias2 = jnp.concatenate([bias, bias], axis=0)
    for b in range(ATTN_WIDTH // LANES):
        sl = slice(b * LANES, (b + 1) * LANES)
        qb = get_q(sl).astype(F32)
        kb = get_k(sl)
        vb = jnp.concatenate([get_v(sl), ones], axis=1)
        qm = jnp.concatenate([jnp.where(q_is_odd, 0.0, qb), jnp.where(q_is_odd, qb, 0.0)], axis=0).astype(BF16)
        sc = lax.dot_general(qm, kb, (((1,), (1,)), ((), ())), preferred_element_type=F32) + bias2
        mx = jnp.max(sc, axis=1, keepdims=True)
        p = jnp.exp2((sc - mx).astype(BF16))
        acc = jnp.dot(p, vb, preferred_element_type=F32)
        for odd in (0, 1):
            rows = slice(odd * tq, (odd + 1) * tq)
            mx_tile = jnp.where(lane == 2 * b + odd, mx[rows], mx_tile)
            den_tile = jnp.where(lane == 2 * b + odd, acc[rows, LANES:], den_tile)
        put_acc(sl, jnp.where(q_is_odd, acc[tq:, :LANES], acc[:tq, :LANES]))
    return mx_tile, den_tile


def _pair_factor(tile, b):
    lane = lax.broadcasted_iota(jnp.int32, tile.shape, 1)
    return jnp.where(_lane_of_odd_head(lane), tile[:, 2 * b + 1:2 * b + 2], tile[:, 2 * b:2 * b + 1])


def _merge_groups(stats, get_accs, put_o):
    top = functools.reduce(jnp.maximum, [mx for mx, _ in stats])
    ws = [jnp.exp2(mx - top) for mx, _ in stats]
    inv = 1.0 / sum(w * den for w, (_, den) in zip(ws, stats))
    fs = [w * inv for w in ws]
    for b in range(ATTN_WIDTH // LANES):
        sl = slice(b * LANES, (b + 1) * LANES)
        put_o(sl, sum(_pair_factor(f, b) * get_acc(sl) for f, get_acc in zip(fs, get_accs)))


def _window_mask(tile, tq, m):
    nk = tq + 2 * RADIUS
    row = lax.broadcasted_iota(jnp.int32, (tq, nk), 0)
    col = lax.broadcasted_iota(jnp.int32, (tq, nk), 1)
    kabs = tile * tq - RADIUS + col
    return _mask_bias((jnp.abs(col - RADIUS - row) <= RADIUS) & (kabs >= 0) & (kabs < m))


def _stack_rows(read, prev_ref, cur_ref, next_ref, lo, cnt):
    halo, cur_n = prev_ref.shape[-2], cur_ref.shape[-2]
    a, b = lo, lo + cnt
    parts = []
    if a < 0:
        parts.append(read(prev_ref, halo + a, halo))
        a = 0
    parts.append(read(cur_ref, a, min(b, cur_n)))
    if b > cur_n:
        parts.append(read(next_ref, 0, b - cur_n))
    return parts


def _kv_getters(stack):
    get_k = lambda sl: stack(sl)
    get_v = lambda sl: stack(slice(ATTN_WIDTH + sl.start, ATTN_WIDTH + sl.stop))
    return get_k, get_v


def _attn16_kernel(q_ref, kvp_ref, kvc_ref, kvn_ref, o_ref, mx_ref, den_ref, *, m):
    tq = ATTN_TQ
    n_sub = q_ref.shape[0] // tq
    for t in range(n_sub):
        rows = slice(t * tq, (t + 1) * tq)
        bias = _window_mask(pl.program_id(2) * n_sub + t, tq, m)

        def stack(cols):
            return jnp.concatenate(_stack_rows(lambda ref, a, b: ref[a:b, cols], kvp_ref, kvc_ref, kvn_ref,
                                               t * tq - RADIUS, tq + 2 * RADIUS), axis=0)

        def put_acc(sl, val):
            o_ref[rows, sl] = val.astype(BF16)

        mx_ref[rows], den_ref[rows] = _attn_pairs(lambda sl: q_ref[rows, sl], *_kv_getters(stack), bias, put_acc)


def _attn4_kernel(q_ref, kvp_ref, kvc_ref, kvn_ref, o_ref, mx_ref, den_ref, *, m):
    ns, halo = q_ref.shape[0], kvp_ref.shape[1]
    rows = ATTN_TQ // ns
    span = rows + 2 * halo
    n_sub = q_ref.shape[1] // rows
    n = lax.broadcasted_iota(jnp.int32, (ns * rows, ns * span), 0)
    c = lax.broadcasted_iota(jnp.int32, (ns * rows, ns * span), 1)
    for t in range(n_sub):
        base = (pl.program_id(2) * n_sub + t) * rows
        rs = slice(t * rows, (t + 1) * rows)
        krow = base - halo + c % span
        rel = ns * (krow - (base + n % rows)) + (c // span - n // rows)
        bias = _mask_bias((jnp.abs(rel) <= RADIUS) & (krow >= 0) & (krow < m))

        def stack(cols):
            parts = []
            for s in range(ns):
                parts += _stack_rows(lambda ref, a, b: ref[s, a:b, cols], kvp_ref, kvc_ref, kvn_ref,
                                     t * rows - halo, span)
            return jnp.concatenate(parts, axis=0)

        flat = lambda ref: (lambda sl: ref[:, rs, sl].reshape(ns * rows, LANES))

        def put_acc(sl, val):
            o_ref[:, rs, sl] = val.astype(BF16).reshape(ns, rows, LANES)

        mx, den = _attn_pairs(flat(q_ref), *_kv_getters(stack), bias, put_acc)
        mx_ref[:, rs, :] = mx.reshape(ns, rows, LANES)
        den_ref[:, rs, :] = den.reshape(ns, rows, LANES)


def _attn1_kernel(q_ref, kvp_ref, kvc_ref, kvn_ref, o2_ref, mx2_ref, den2_ref, o1_ref, mx1_ref, den1_ref,
                  unperm_ref, o_ref, acc_scr, a2_scr, a1_scr, *, m):
    tq = ATTN_TQ
    n_sub = q_ref.shape[0] // tq
    srow = tq // STREAMS
    for src, dst in ((o2_ref, a2_scr), (o1_ref, a1_scr)):
        dst[...] = jnp.dot(unperm_ref[...], src[...].reshape(dst.shape), preferred_element_type=F32)
    for t in range(n_sub):
        rows = slice(t * tq, (t + 1) * tq)
        bias = _window_mask(pl.program_id(1) * n_sub + t, tq, m)

        def stack(cols):
            return jnp.concatenate(_stack_rows(lambda ref, a, b: ref[a:b, cols], kvp_ref, kvc_ref, kvn_ref,
                                               t * tq - RADIUS, tq + 2 * RADIUS), axis=0)

        def natural(ref):
            return jnp.concatenate([ref[:, t * srow + ii, :] for ii in range(srow)], axis=0)

        def put_acc(sl, val):
            acc_scr[:, sl] = val

        def put_o(sl, val):
            o_ref[rows, sl] = val.astype(BF16)

        stats = [_attn_pairs(lambda sl: q_ref[rows, sl], *_kv_getters(stack), bias, put_acc),
                 (natural(mx2_ref), natural(den2_ref)), (natural(mx1_ref), natural(den1_ref))]
        _merge_groups(stats, [lambda sl: acc_scr[:, sl], lambda sl: a2_scr[rows, sl],
                              lambda sl: a1_scr[rows, sl]], put_o)


def _attention(qkv0, qkv12):
    bsz, s, _ = qkv0.shape
    m16 = s // STREAMS
    tq, half = ATTN_TQ, RADIUS
    cur_rows = min(ATTN_SUB * tq, m16)
    params = lambda n: pltpu.CompilerParams(dimension_semantics=("parallel",) * n,
                                            vmem_limit_bytes=VMEM_LIMIT)
    same = lambda i: i

    def halo_idx(cur_n, halo_n, total):
        per = cur_n // halo_n
        return (lambda i: jnp.maximum(i * per - 1, 0)), (lambda i: jnp.minimum((i + 1) * per, total // halo_n - 1))

    def spec16(nrows, row_idx, col, width=ATTN_WIDTH):
        return pl.BlockSpec((None, None, nrows, width), lambda b, r, i: (b, r, row_idx(i), col))
    prv, nxt = halo_idx(cur_rows, half, m16)
    stat16 = jax.ShapeDtypeStruct((bsz, STREAMS, m16, LANES), F32)
    o2, mx2, den2 = pl.pallas_call(
        functools.partial(_attn16_kernel, m=m16),
        out_shape=(jax.ShapeDtypeStruct((bsz, STREAMS, m16, ATTN_WIDTH), BF16), stat16, stat16),
        grid=(bsz, STREAMS, m16 // cur_rows),
        in_specs=[spec16(cur_rows, same, 5), spec16(half, prv, 1, 2 * ATTN_WIDTH),
                  spec16(cur_rows, same, 1, 2 * ATTN_WIDTH), spec16(half, nxt, 1, 2 * ATTN_WIDTH)],
        out_specs=(spec16(cur_rows, same, 0), spec16(cur_rows, same, 0, LANES), spec16(cur_rows, same, 0, LANES)),
        compiler_params=params(3), name="attn_dil16",
    )(*([qkv12] * 4))

    ns = STREAMS // 4
    rows4, halo4 = cur_rows // ns, half // ns

    def view4(a):
        return a.reshape(bsz, ns, 4, m16, a.shape[-1])

    def spec4(nrows, row_idx, col, width=ATTN_WIDTH):
        return pl.BlockSpec((None, ns, None, nrows, width), lambda b, r, i: (b, 0, r, row_idx(i), col))
    prv4, nxt4 = halo_idx(rows4, halo4, m16)
    q4 = view4(qkv12)
    stat4 = jax.ShapeDtypeStruct((bsz, ns, 4, m16, LANES), F32)
    o1, mx1, den1 = pl.pallas_call(
        functools.partial(_attn4_kernel, m=m16),
        out_shape=(jax.ShapeDtypeStruct((bsz, ns, 4, m16, ATTN_WIDTH), BF16), stat4, stat4),
        grid=(bsz, 4, m16 // rows4),
        in_specs=[spec4(rows4, same, 4), spec4(halo4, prv4, 0, 2 * ATTN_WIDTH),
                  spec4(rows4, same, 0, 2 * ATTN_WIDTH), spec4(halo4, nxt4, 0, 2 * ATTN_WIDTH)],
        out_specs=(spec4(rows4, same, 0), spec4(rows4, same, 0, LANES), spec4(rows4, same, 0, LANES)),
        compiler_params=params(3), name="attn_dil4",
    )(q4, q4, q4, q4)
    streams = lambda a: a.reshape(bsz, STREAMS, m16, a.shape[-1])

    cur1 = ATTN_SUB * tq

    def spec1(nrows, row_idx, col, width=ATTN_WIDTH):
        return pl.BlockSpec((None, nrows, width), lambda b, i: (b, row_idx(i), col))
    prv1, nxt1 = halo_idx(cur1, half, s)
    sm = lambda width: pl.BlockSpec((None, STREAMS, cur1 // STREAMS, width), lambda b, i: (b, 0, i, 0))
    run = [sm(ATTN_WIDTH), sm(LANES), sm(LANES)]
    unperm = jnp.asarray(_stream_perm(cur1).T, BF16)
    return pl.pallas_call(
        functools.partial(_attn1_kernel, m=s),
        out_shape=jax.ShapeDtypeStruct((bsz, s, ATTN_WIDTH), BF16),
        grid=(bsz, s // cur1),
        in_specs=[spec1(cur1, same, 2), spec1(half, prv1, 0, 2 * ATTN_WIDTH),
                  spec1(cur1, same, 0, 2 * ATTN_WIDTH), spec1(half, nxt1, 0, 2 * ATTN_WIDTH)] + run + run
                 + [pl.BlockSpec((cur1, cur1), lambda b, i: (0, 0))],
        out_specs=spec1(cur1, same, 0),
        scratch_shapes=[pltpu.VMEM((tq, ATTN_WIDTH), F32), pltpu.VMEM((cur1, ATTN_WIDTH), F32),
                        pltpu.VMEM((cur1, ATTN_WIDTH), F32)],
        compiler_params=params(2), name="attn_dil1",
    )(*([qkv0] * 4), o2, mx2, den2, streams(o1), streams(mx1), streams(den1), unperm)


def _attn_out_kernel(o_ref, z_ref, x_ref, gate_ref, w_ref, out_ref):
    y = (o_ref[...].astype(F32) * _silu(z_ref[...].astype(F32))).astype(BF16)
    out = jnp.dot(y, w_ref[...], preferred_element_type=F32)
    out_ref[...] = x_ref[...] + gate_ref[...] * out


def _attn_out(o, z, x, mod, w, tm=512):
    bsz, s, _ = x.shape
    row = pl.BlockSpec((None, tm, D_MODEL), lambda b, i: (b, i, 0))
    return pl.pallas_call(
        _attn_out_kernel,
        out_shape=jax.ShapeDtypeStruct(x.shape, F32),
        grid=(bsz, s // tm),
        in_specs=[row, row, row,
                  pl.BlockSpec((None, 1, D_MODEL), lambda b, i: (b, 0, 2)),
                  pl.BlockSpec((ATTN_WIDTH, D_MODEL), lambda b, i: (0, 0))],
        out_specs=row,
        compiler_params=pltpu.CompilerParams(
            dimension_semantics=("parallel", "parallel"),
            vmem_limit_bytes=VMEM_LIMIT),
        name="attn_out",
    )(o, z, x, mod, w)


def _cmul(ar, ai, br, bi):
    return ar * br - ai * bi, ar * bi + ai * br


def _s5_prep_kernel(lam_re_a, lam_im_a, lam_re_b, lam_im_b, logdt, bt_re, bt_im,
                    c_re, c_im, toep_ref, mb_ref, mc_ref, al_ref, v_scr):
    lane = lax.broadcasted_iota(jnp.int32, (STATE, LANES), 1)
    a_idx = (lane // GROUP_CH).astype(F32)
    n_tiles = CHUNK // T_PER_VREG

    mb_rows = []
    v_rhs = []
    v_lhs = []
    mc_cols = []
    zbar_f = None
    for dr in range(2):
        dt = jnp.exp(logdt[dr])
        lr = jnp.minimum(lam_re_a[dr], LAMBDA_RE_MAX)
        li = lam_im_a[dr]
        ldr, ldi = lr * dt, li * dt

        def powers(e):
            mag = jnp.exp(e * ldr)
            return mag * jnp.cos(e * ldi), mag * jnp.sin(e * ldi)

        l1r, l1i = powers(1.0)
        den = lr * lr + li * li
        nr, ni = l1r - 1.0, l1i
        cr, ci = (nr * lr + ni * li) / den, (ni * lr - nr * li) / den
        bbr, bbi = _cmul(cr, ci, bt_re[dr], bt_im[dr])
        l8r, l8i = powers(float(T_PER_VREG))
        tiles = [None] * n_tiles
        if dr == 0:
            pr, pi = powers(float(T_PER_VREG - 1) - a_idx)
            cur = _cmul(pr, pi, bbr, bbi)
            for jt in range(n_tiles - 1, -1, -1):
                tiles[jt] = cur
                if jt:
                    cur = _cmul(cur[0], cur[1], l8r, l8i)
            v_rhs.append(_cmul(tiles[n_tiles - 1][0], tiles[n_tiles - 1][1], l1r, l1i))
            zbar_f = (jnp.where(lane < GROUP_CH, bbr, 0.0), jnp.where(lane < GROUP_CH, bbi, 0.0))
        else:
            pr, pi = powers(a_idx)
            cur = _cmul(pr, pi, bbr, bbi)
            for jt in range(n_tiles):
                tiles[jt] = cur
                if jt < n_tiles - 1:
                    cur = _cmul(cur[0], cur[1], l8r, l8i)
            v_rhs.append(tiles[0])
        mb_rows.append(jnp.concatenate([t[0] for t in tiles], axis=1))
        mb_rows.append(jnp.concatenate([t[1] for t in tiles], axis=1))
        a_r, a_i = l8r, l8i
        for _ in range(int(math.log2(CHUNK // T_PER_VREG))):
            a_r, a_i = _cmul(a_r, a_i, a_r, a_i)
        al_ref[2 * dr] = jnp.broadcast_to(a_r, (STATE, LANES))
        al_ref[2 * dr + 1] = jnp.broadcast_to(a_i, (STATE, LANES))

        lrb = jnp.minimum(lam_re_b[dr], LAMBDA_RE_MAX)
        lib = lam_im_b[dr]
        mag = jnp.exp(lrb * dt)
        pw_r, pw_i = mag * jnp.cos(lib * dt), mag * jnp.sin(lib * dt)
        pows = [(pw_r, pw_i)]
        for _ in range(int(math.log2(CHUNK)) - 1):
            pows.append(_cmul(*pows[-1], *pows[-1]))
        cre, cim = c_re[dr], c_im[dr]
        xr, xi = _cmul(cre, cim, *pows[0])
        for k in range(int(math.log2(CHUNK))):
            yr, yi = _cmul(xr, xi, *pows[k])
            if dr == 0:
                xr, xi = jnp.concatenate([xr, yr], 0), jnp.concatenate([xi, yi], 0)
            else:
                xr, xi = jnp.concatenate([yr, xr], 0), jnp.concatenate([yi, xi], 0)
        mc_cols += [xr, -xi]
        xr, xi = cre, cim
        for k in range(int(math.log2(n_tiles))):
            yr, yi = _cmul(xr, xi, *pows[k + int(math.log2(T_PER_VREG))])
            if dr == 0:
                xr, xi = jnp.concatenate([yr, xr], 0), jnp.concatenate([yi, xi], 0)
            else:
                xr, xi = jnp.concatenate([xr, yr], 0), jnp.concatenate([xi, yi], 0)
        v_lhs.append((xr, xi))

    mb_ref[...] = jnp.concatenate(mb_rows, axis=0).astype(BF16)
    mc_ref[...] = jnp.concatenate(mc_cols, axis=1).astype(BF16)

    def cdot(lhs, rhs):
        return (jnp.dot(lhs[0], rhs[0], precision=HI, preferred_element_type=F32)
                - jnp.dot(lhs[1], rhs[1], precision=HI, preferred_element_type=F32))

    out_f = cdot(v_lhs[0], v_rhs[0])
    zero_rows = jnp.zeros((CHUNK_ROWS // T_PER_VREG - GROUP_CH, STATE), F32)
    lag0_lhs = (jnp.concatenate([c_re[0], zero_rows], 0), jnp.concatenate([c_im[0], zero_rows], 0))
    out_b = cdot(v_lhs[1], v_rhs[1]) + cdot(lag0_lhs, zbar_f)
    for jt in range(n_tiles):
        rows = slice(jt * GROUP_CH, (jt + 1) * GROUP_CH)
        v_scr[:, jt * LANES:(jt + 1) * LANES] = out_f[rows]
        v_scr[:, (n_tiles + jt) * LANES:(n_tiles + jt + 1) * LANES] = out_b[rows]
    v_scr[:, 2 * CHUNK_ROWS:] = jnp.zeros((GROUP_CH, LANES), F32)

    for k in range(T_PER_VREG):
        vk = v_scr[:, k * GROUP_CH:k * GROUP_CH + 2 * CHUNK_ROWS]
        for t_out in range(CHUNK):
            off = (CHUNK - t_out) * GROUP_CH
            if off % LANES == k * GROUP_CH:
                al_off = off - k * GROUP_CH
                toep_ref[t_out * GROUP_CH:(t_out + 1) * GROUP_CH, :] = (
                    vk[:, al_off:al_off + CHUNK_ROWS].astype(BF16))


def _s5_prep(lam_re, lam_im, log_dt, b_re, b_im, c_re, c_im):
    g = SSM_GROUPS
    lam_a = lambda a: a.reshape(2, g, STATE, 1)
    lam_b = lambda a: a.reshape(2, g, 1, STATE)
    tile_b = lambda a: jnp.tile(a, (1, 1, 1, T_PER_VREG))
    dir_spec = lambda r, c: pl.BlockSpec((2, None, r, c), lambda i: (0, i, 0, 0))
    return pl.pallas_call(
        _s5_prep_kernel,
        out_shape=(jax.ShapeDtypeStruct((g, CHUNK_ROWS, CHUNK_ROWS), BF16),
                   jax.ShapeDtypeStruct((g, 4 * STATE, CHUNK_ROWS), BF16),
                   jax.ShapeDtypeStruct((g, CHUNK_ROWS, 4 * STATE), BF16),
                   jax.ShapeDtypeStruct((g, 4, STATE, LANES), F32)),
        grid=(g,),
        in_specs=[dir_spec(STATE, 1), dir_spec(STATE, 1), dir_spec(1, STATE), dir_spec(1, STATE),
                  dir_spec(1, 1), dir_spec(STATE, LANES), dir_spec(STATE, LANES),
                  dir_spec(GROUP_CH, STATE), dir_spec(GROUP_CH, STATE)],
        out_specs=(pl.BlockSpec((None, CHUNK_ROWS, CHUNK_ROWS), lambda i: (i, 0, 0)),
                   pl.BlockSpec((None, 4 * STATE, CHUNK_ROWS), lambda i: (i, 0, 0)),
                   pl.BlockSpec((None, CHUNK_ROWS, 4 * STATE), lambda i: (i, 0, 0)),
                   pl.BlockSpec((None, 4, STATE, LANES), lambda i: (i, 0, 0, 0))),
        scratch_shapes=[pltpu.VMEM((GROUP_CH, V_LANES), F32)],
        compiler_params=pltpu.CompilerParams(dimension_semantics=("parallel",)),
        name="s5_prep",
    )(lam_a(lam_re), lam_a(lam_im), lam_b(lam_re), lam_b(lam_im), log_dt.reshape(2, g, 1, 1),
      tile_b(b_re), tile_b(b_im), c_re, c_im)


def _slab_perm():
    dst = np.arange(SLAB)
    perm = np.zeros((SLAB, SLAB), np.float32)
    perm[dst, (dst % SLAB_CHUNKS) * SUBLANES + dst // SLAB_CHUNKS] = 1.0
    return perm


def _s5_in_kernel(x_ref, mod_ref, g_ref, w_ref, p_ref, u_ref, z_ref, h_scr, *, chunks_per_seq):
    n_chunks = x_ref.shape[0]
    for b in range(n_chunks // chunks_per_seq):
        rows = slice(b * chunks_per_seq, (b + 1) * chunks_per_seq)
        h = _rms_modulate(x_ref[rows], g_ref[...], mod_ref[b:b + 1, D_MODEL:2 * D_MODEL],
                          mod_ref[b:b + 1, :D_MODEL])
        h2 = h.reshape(chunks_per_seq * SUBLANES, D_MODEL).astype(BF16)
        for s in range(chunks_per_seq // SLAB_CHUNKS):
            hp = jnp.dot(p_ref[...], h2[s * SLAB:(s + 1) * SLAB], preferred_element_type=F32)
            c0 = b * chunks_per_seq + s * SLAB_CHUNKS
            for k in range(SUBLANES):
                h_scr[k, c0:c0 + SLAB_CHUNKS, :] = hp[k * SLAB_CHUNKS:(k + 1) * SLAB_CHUNKS].astype(BF16)
    for k in range(SUBLANES):
        ht = h_scr[k].astype(F32).T.astype(BF16)
        uz = jnp.dot(w_ref[...], ht, preferred_element_type=F32)
        u_ref[k] = uz[:SSM_WIDTH].astype(BF16)
        z_ref[k] = uz[SSM_WIDTH:].astype(BF16)


def _s5_in(x3, mod, g, w_t, chunks_per_seq):
    n_chunks = x3.shape[0]
    once = lambda shape: pl.BlockSpec(shape, lambda t: (0,) * len(shape), pipeline_mode=pl.Buffered(1))
    out = pl.BlockSpec((SUBLANES, SSM_WIDTH, n_chunks), lambda t: (t, 0, 0))
    return pl.pallas_call(
        functools.partial(_s5_in_kernel, chunks_per_seq=chunks_per_seq),
        out_shape=(jax.ShapeDtypeStruct((CHUNK, SSM_WIDTH, n_chunks), BF16),) * 2,
        grid=(CHUNK // SUBLANES,),
        in_specs=[pl.BlockSpec((n_chunks, SUBLANES, D_MODEL), lambda t: (0, t, 0)),
                  once(mod.shape), once((1, D_MODEL)), once((2 * SSM_WIDTH, D_MODEL)),
                  once((SLAB, SLAB))],
        out_specs=(out, out),
        scratch_shapes=[pltpu.VMEM((SUBLANES, n_chunks, D_MODEL), BF16)],
        compiler_params=pltpu.CompilerParams(
            dimension_semantics=("parallel",), vmem_limit_bytes=VMEM_LIMIT_BIG),
        name="s5_in",
    )(x3, mod, g, w_t, jnp.asarray(_slab_perm(), BF16))


def _gelu_tanh(v):
    return 0.5 * v * (1.0 + jnp.tanh(math.sqrt(2.0 / math.pi) * (v + 0.044715 * (v * v * v))))


def _s5_core_kernel(u_ref, toep_ref, mb_ref, mc_ref, al_ref, d_ref, o_ref, *, chunks_per_seq):
    n_chunks = u_ref.shape[-1]
    ub = u_ref[...].reshape(CHUNK_ROWS, n_chunks)
    inc = jnp.dot(mb_ref[...], ub, preferred_element_type=F32)
    pos = lax.broadcasted_iota(jnp.int32, (STATE, n_chunks), 1) % chunks_per_seq
    reps = n_chunks // LANES

    def lane_tile(a):
        return jnp.concatenate([a] * reps, axis=1) if reps > 1 else a

    states = []
    for dr in range(2):
        xr = inc[(2 * dr) * STATE:(2 * dr + 1) * STATE]
        xi = inc[(2 * dr + 1) * STATE:(2 * dr + 2) * STATE]
        ar, ai = lane_tile(al_ref[2 * dr]), lane_tile(al_ref[2 * dr + 1])

        def shifted(v, step):
            if dr == 0:
                return jnp.where(pos >= step, pltpu.roll(v, step, 1), 0.0)
            return jnp.where(pos < chunks_per_seq - step, pltpu.roll(v, n_chunks - step, 1), 0.0)

        step = 1
        while step < chunks_per_seq:
            sr, si = shifted(xr, step), shifted(xi, step)
            xr, xi = xr + ar * sr - ai * si, xi + ar * si + ai * sr
            ar, ai = _cmul(ar, ai, ar, ai)
            step *= 2
        states += [shifted(xr, 1), shifted(xi, 1)]
    h_in = jnp.concatenate(states, axis=0).astype(BF16)
    y = (jnp.dot(toep_ref[...], ub, preferred_element_type=F32)
         + jnp.dot(mc_ref[...], h_in, preferred_element_type=F32)
         + d_ref[...] * ub.astype(F32))
    o_ref[...] = _gelu_tanh(y).astype(BF16).reshape(CHUNK, GROUP_CH, n_chunks)


def _s5_core(u_cl, toep, mb, mc, al, d_col, chunks_per_seq):
    n_chunks = u_cl.shape[-1]
    u4 = u_cl.reshape(CHUNK, SSM_GROUPS, GROUP_CH, n_chunks)
    grp = pl.BlockSpec((CHUNK, None, GROUP_CH, n_chunks), lambda g: (0, g, 0, 0))
    out = pl.pallas_call(
        functools.partial(_s5_core_kernel, chunks_per_seq=chunks_per_seq),
        out_shape=jax.ShapeDtypeStruct(u4.shape, BF16),
        grid=(SSM_GROUPS,),
        in_specs=[grp,
                  pl.BlockSpec((None, CHUNK_ROWS, CHUNK_ROWS), lambda g: (g, 0, 0)),
                  pl.BlockSpec((None, 4 * STATE, CHUNK_ROWS), lambda g: (g, 0, 0)),
                  pl.BlockSpec((None, CHUNK_ROWS, 4 * STATE), lambda g: (g, 0, 0)),
                  pl.BlockSpec((None, 4, STATE, LANES), lambda g: (g, 0, 0, 0)),
                  pl.BlockSpec((None, CHUNK_ROWS, 1), lambda g: (g, 0, 0))],
        out_specs=grp,
        compiler_params=pltpu.CompilerParams(
            dimension_semantics=("parallel",), vmem_limit_bytes=VMEM_LIMIT),
        name="s5_core",
    )(u4, toep, mb, mc, al, d_col)
    return out.reshape(CHUNK, SSM_WIDTH, n_chunks)


def _s5_out_kernel(g_ref, z_ref, wglu_ref, wout_ref, pt_ref, x_ref, mod_ref, fg_ref, o_ref, y_scr,
                   *, final, chunks_per_seq):
    half = pl.program_id(1)
    n_half = x_ref.shape[0]

    @pl.when(half == 0)
    def _():
        for k in range(SUBLANES):
            gb = g_ref[k]
            glu = jnp.dot(wglu_ref[...], gb, preferred_element_type=F32)
            y = gb.astype(F32) * (1.0 / (1.0 + jnp.exp(-glu))) * _silu(z_ref[k].astype(F32))
            y_scr[k] = y.T.astype(BF16)

    for hv in range(y_scr.shape[1] // n_half):
        @pl.when(half == hv)
        def _():
            for s in range(n_half // SLAB_CHUNKS):
                c0 = hv * n_half + s * SLAB_CHUNKS
                src = jnp.concatenate([y_scr[k, c0:c0 + SLAB_CHUNKS, :] for k in range(SUBLANES)], axis=0)
                yp = jnp.dot(pt_ref[...], src, preferred_element_type=F32).astype(BF16)
                out = jnp.dot(yp, wout_ref[...], preferred_element_type=F32)
                b = c0 // chunks_per_seq
                upd = mod_ref[b:b + 1, 2 * D_MODEL:] * out
                rows = slice(s * SLAB_CHUNKS, (s + 1) * SLAB_CHUNKS)
                xn = x_ref[rows] + upd.reshape(SLAB_CHUNKS, SUBLANES, D_MODEL)
                if final:
                    xn = xn * lax.rsqrt(jnp.mean(xn * xn, axis=-1, keepdims=True) + NORM_EPS) * fg_ref[...]
                o_ref[rows] = xn


def _s5_out(g_cl, z_cl, wglu_t, w_out, x3, mod, final_g, final, chunks_per_seq, n_split=2):
    n_chunks = x3.shape[0]
    n_half = n_chunks // n_split
    once = lambda shape: pl.BlockSpec(shape, lambda t, h: (0,) * len(shape), pipeline_mode=pl.Buffered(1))
    act = pl.BlockSpec((SUBLANES, SSM_WIDTH, n_chunks), lambda t, h: (t, 0, 0))
    row = pl.BlockSpec((n_half, SUBLANES, D_MODEL), lambda t, h: (h, t, 0))
    return pl.pallas_call(
        functools.partial(_s5_out_kernel, final=final, chunks_per_seq=chunks_per_seq),
        out_shape=jax.ShapeDtypeStruct(x3.shape, F32),
        grid=(CHUNK // SUBLANES, n_split),
        in_specs=[act, act, once((SSM_WIDTH, SSM_WIDTH)), once((SSM_WIDTH, D_MODEL)), once((SLAB, SLAB)),
                  row, once(mod.shape), once((1, D_MODEL))],
        out_specs=row,
        scratch_shapes=[pltpu.VMEM((SUBLANES, n_chunks, SSM_WIDTH), BF16)],
        compiler_params=pltpu.CompilerParams(
            dimension_semantics=("parallel", "arbitrary"), vmem_limit_bytes=VMEM_LIMIT_BIG),
        name="s5_out",
    )(g_cl, z_cl, wglu_t, w_out, jnp.asarray(_slab_perm().T, BF16), x3, mod, final_g)


def _rope_tables(s):
    inv_freq = ROPE_THETA ** (-jnp.arange(0, HEAD_DIM, 2, dtype=F32) / HEAD_DIM)
    ang = jnp.arange(s, dtype=F32)[:, None] * inv_freq[None, :]
    reps = LANES // (HEAD_DIM // 2)
    sign = jnp.where(jnp.arange(LANES) < LANES // 2, -1.0, 1.0).astype(F32)
    cos, sin = jnp.tile(jnp.cos(ang), (1, reps)), jnp.tile(jnp.sin(ang), (1, reps)) * sign
    stream_major = lambda a: a.reshape(s // STREAMS, STREAMS, LANES).transpose(1, 0, 2)
    return cos, sin, stream_major(cos), stream_major(sin)


def _prep_attn_w_in(w):
    d_in = w.shape[0]
    w = w.astype(BF16).reshape(d_in, -1, ATTN_HEADS // 2, 2, 2, HEAD_DIM // 2)
    w = w.transpose(0, 1, 2, 4, 3, 5)
    return w[:, np.array(ATTN_IN_BLOCK_ORDER)].reshape(d_in, -1)


def _prep_attn_w_out(w):
    w = w.astype(BF16).reshape(ATTN_HEADS // 2, 2, 2, HEAD_DIM // 2, w.shape[1])
    return w.transpose(0, 2, 1, 3, 4).reshape(ATTN_WIDTH, -1)


def _trunk(x, ada, norm_g, attn_w, s5_w, s5_ops, final_norm_g):
    bsz, s, _ = x.shape
    chunks_per_seq = s // CHUNK
    n_chunks = bsz * chunks_per_seq
    tables = _rope_tables(s)
    fg = final_norm_g.reshape(1, D_MODEL)
    for i in range(DEPTH):
        mod = ada[i].reshape(bsz, 1, 3 * D_MODEL)
        g = norm_g[i].reshape(1, D_MODEL)
        j = i // N_MIXERS
        if i % N_MIXERS == 0:
            w_in, w_out = attn_w[j]
            qkv0, qkv12, z = _attn_in(x, mod, g, w_in, tables)
            x = _attn_out(_attention(qkv0, qkv12), z, x, mod, w_out)
        else:
            w_in_t, wglu_t, w_out, d_col = s5_w[j]
            toep, mb, mc, al = s5_ops[j]
            x3 = x.reshape(n_chunks, CHUNK, D_MODEL)
            u_cl, z_cl = _s5_in(x3, ada[i], g, w_in_t, chunks_per_seq)
            g_cl = _s5_core(u_cl, toep, mb, mc, al, d_col, chunks_per_seq)
            x3 = _s5_out(g_cl, z_cl, wglu_t, w_out, x3, ada[i], fg, i == DEPTH - 1, chunks_per_seq)
            x = x3.reshape(bsz, s, D_MODEL)
    return x


def kernel(x_prompt, x_sample, c_prompt, c_sample, norm_g, ada_w, ada_b, attn_w_in, attn_w_out,
           ssm_w_in, ssm_lam_re, ssm_lam_im, ssm_log_dt, ssm_b_re, ssm_b_im, ssm_c_re, ssm_c_im,
           ssm_d, ssm_w_glu, ssm_w_out, final_norm_g):
    assert (DEPTH - 1) % N_MIXERS == 1, "the final norm is fused into the last S5 layer"
    assert all(w // (2 * d) == RADIUS for w, d in DILATED_PAIRS)
    assert [d for _, d in DILATED_PAIRS] == [1, 4, STREAMS]
    n_prompt = c_prompt.shape[0]
    ada = _ada(jnp.concatenate([c_prompt, c_sample], axis=0), ada_w, ada_b)
    attn_w = [(_prep_attn_w_in(attn_w_in[j]), _prep_attn_w_out(attn_w_out[j]))
              for j in range(attn_w_in.shape[0])]
    s5_w, s5_ops = [], []
    for j in range(ssm_w_in.shape[0]):
        d_col = jnp.tile(ssm_d[j].reshape(SSM_GROUPS, 1, GROUP_CH), (1, CHUNK, 1))
        s5_w.append((ssm_w_in[j].T.astype(BF16), ssm_w_glu[j].T.astype(BF16),
                     ssm_w_out[j].astype(BF16), d_col.reshape(SSM_GROUPS, CHUNK_ROWS, 1)))
        s5_ops.append(_s5_prep(ssm_lam_re[j], ssm_lam_im[j], ssm_log_dt[j], ssm_b_re[j],
                               ssm_b_im[j], ssm_c_re[j], ssm_c_im[j]))
    y_prompt = _trunk(x_prompt, ada[:, :n_prompt], norm_g, attn_w, s5_w, s5_ops, final_norm_g)
    y_sample = _trunk(x_sample, ada[:, n_prompt:], norm_g, attn_w, s5_w, s5_ops, final_norm_g)
    return (y_prompt, y_sample)
```

```python
import functools
import math

import numpy as np
import jax
import jax.numpy as jnp
from jax import lax
from jax.experimental import pallas as pl
from jax.experimental.pallas import tpu as pltpu

D_MODEL = 1024
DEPTH = 4
N_MIXERS = 2
ATTN_HEADS = 16
HEAD_DIM = 64
ATTN_WIDTH = ATTN_HEADS * HEAD_DIM
DILATED_PAIRS = ((128, 1), (512, 4), (2048, 16))
N_DIL = len(DILATED_PAIRS)
ROPE_THETA = 10000.0
SSM_WIDTH = D_MODEL
GROUP_CH = 16
SSM_GROUPS = SSM_WIDTH // GROUP_CH
STATE = 64
LAMBDA_RE_MAX = -1e-4
NORM_EPS = 1e-6
NEG_INF = -1e30

LANES = 128
SUBLANES = 8
CHUNK = 64
CHUNK_ROWS = CHUNK * GROUP_CH
T_PER_VREG = LANES // GROUP_CH
V_LANES = 2 * CHUNK_ROWS + LANES
STREAMS = 16
RADIUS = 64
ATTN_TQ = 128
ATTN_SUB = 4
STAT_PITCH = 40
ATTN_IN_BLOCK_ORDER = (1, 2, 0, 4, 5, 7, 8, 3, 6, 9)
PERM_ROWS = 512
LOG2_E = math.log2(math.e)
SLAB = 256
SLAB_CHUNKS = SLAB // SUBLANES
VMEM_LIMIT = 48 * 1024 * 1024
VMEM_LIMIT_BIG = 56 * 1024 * 1024

F32 = jnp.float32
BF16 = jnp.bfloat16
HI = lax.Precision.HIGHEST


def _silu(v):
    return v * (1.0 / (1.0 + jnp.exp(-v)))


def _rms_modulate(x, g, scale, shift):
    rs = lax.rsqrt(jnp.mean(x * x, axis=-1, keepdims=True) + NORM_EPS)
    return (x * rs * g) * (1.0 + scale) + shift


def _ada_kernel(c_ref, w_ref, b_ref, o_ref):
    o_ref[...] = jnp.dot(_silu(c_ref[...]), w_ref[...], precision=HI,
                         preferred_element_type=F32) + b_ref[...]


def _ada(c, ada_w, ada_b):
    nb = c.shape[0]
    return pl.pallas_call(
        _ada_kernel,
        out_shape=jax.ShapeDtypeStruct((DEPTH, nb, 3 * D_MODEL), F32),
        grid=(DEPTH, 3),
        in_specs=[
            pl.BlockSpec((nb, D_MODEL), lambda i, j: (0, 0)),
            pl.BlockSpec((None, D_MODEL, D_MODEL), lambda i, j: (i, 0, j)),
            pl.BlockSpec((None, 1, D_MODEL), lambda i, j: (i, 0, j)),
        ],
        out_specs=pl.BlockSpec((None, nb, D_MODEL), lambda i, j: (i, 0, j)),
        name="ada",
    )(c, ada_w, ada_b.reshape(DEPTH, 1, 3 * D_MODEL))


def _attn_in_kernel(x_ref, shift_ref, scale_ref, g_ref, w_ref, cos_ref, sin_ref, cosp_ref, sinp_ref,
                    perm_ref, qkv0_ref, qkv12_ref, z_ref, h_nat, h_perm):
    j = pl.program_id(2)
    tm = x_ref.shape[0]
    n_sub = tm // PERM_ROWS
    sub_rows = PERM_ROWS // STREAMS

    @pl.when(j == 0)
    def _():
        h = _rms_modulate(x_ref[...], g_ref[...], scale_ref[...], shift_ref[...])
        h_nat[...] = h.astype(BF16)
        for u in range(n_sub):
            rs = slice(u * PERM_ROWS, (u + 1) * PERM_ROWS)
            h_perm[rs, :] = jnp.dot(perm_ref[...], h_nat[rs, :], preferred_element_type=F32).astype(BF16)

    is_q = jnp.logical_or(j == 2, j >= 7)
    is_v = jnp.logical_or(j == 1, jnp.logical_or(j == 4, j == 6))

    def rope_blocks(acc, cos, sin):
        qscale = jnp.where(is_q, HEAD_DIM ** -0.5 * LOG2_E, 1.0).astype(F32)
        cos = jnp.where(is_v, 1.0, cos * qscale)
        sin = jnp.where(is_v, 0.0, sin * qscale)
        for b in range(ATTN_WIDTH // LANES):
            sl = slice(b * LANES, (b + 1) * LANES)
            t = acc[:, sl]
            yield sl, (t * cos + pltpu.roll(t, LANES // 2, 1) * sin).astype(BF16)

    @pl.when(j < 3)
    def _():
        acc = jnp.dot(h_nat[...], w_ref[...], preferred_element_type=F32)
        for sl, blk in rope_blocks(acc, cos_ref[...], sin_ref[...]):
            qkv0_ref[:, sl] = blk

    @pl.when(jnp.logical_and(j >= 3, j < 3 * N_DIL))
    def _():
        acc = jnp.dot(h_perm[...], w_ref[...], preferred_element_type=F32)

        def table(ref):
            return jnp.concatenate([ref[:, u * sub_rows:(u + 1) * sub_rows, :].reshape(PERM_ROWS, LANES)
                                    for u in range(n_sub)], axis=0)

        for sl, blk in rope_blocks(acc, table(cosp_ref), table(sinp_ref)):
            for u in range(n_sub):
                qkv12_ref[:, u * sub_rows:(u + 1) * sub_rows, sl] = (
                    blk[u * PERM_ROWS:(u + 1) * PERM_ROWS].reshape(STREAMS, sub_rows, LANES))

    @pl.when(j == 3 * N_DIL)
    def _():
        z_ref[...] = jnp.dot(h_nat[...], w_ref[...], preferred_element_type=F32).astype(BF16)


def _stream_perm(n):
    dst = np.arange(n)
    rows = n // STREAMS
    perm = np.zeros((n, n), np.float32)
    perm[dst, STREAMS * (dst % rows) + dst // rows] = 1.0
    return perm


def _attn_in(x, mod, g, w, tables, tm=1024):
    bsz, s, _ = x.shape
    cos, sin, cosp, sinp = tables
    rows = tm // STREAMS
    n_col = 3 * N_DIL + 1
    perm = jnp.asarray(_stream_perm(PERM_ROWS), BF16)
    return pl.pallas_call(
        _attn_in_kernel,
        out_shape=(jax.ShapeDtypeStruct((bsz, s, 3 * ATTN_WIDTH), BF16),
                   jax.ShapeDtypeStruct((bsz, STREAMS, s // STREAMS, 6 * ATTN_WIDTH), BF16),
                   jax.ShapeDtypeStruct((bsz, s, ATTN_WIDTH), BF16)),
        grid=(bsz, s // tm, n_col),
        in_specs=[
            pl.BlockSpec((None, tm, D_MODEL), lambda b, i, j: (b, i, 0)),
            pl.BlockSpec((None, 1, D_MODEL), lambda b, i, j: (b, 0, 0)),
            pl.BlockSpec((None, 1, D_MODEL), lambda b, i, j: (b, 0, 1)),
            pl.BlockSpec((1, D_MODEL), lambda b, i, j: (0, 0)),
            pl.BlockSpec((D_MODEL, ATTN_WIDTH), lambda b, i, j: (0, j)),
            pl.BlockSpec((tm, LANES), lambda b, i, j: (i, 0)),
            pl.BlockSpec((tm, LANES), lambda b, i, j: (i, 0)),
            pl.BlockSpec((STREAMS, rows, LANES), lambda b, i, j: (0, i, 0)),
            pl.BlockSpec((STREAMS, rows, LANES), lambda b, i, j: (0, i, 0)),
            pl.BlockSpec((PERM_ROWS, PERM_ROWS), lambda b, i, j: (0, 0)),
        ],
        out_specs=(
            pl.BlockSpec((None, tm, ATTN_WIDTH), lambda b, i, j: (b, i, jnp.clip(j, 0, 2))),
            pl.BlockSpec((None, STREAMS, rows, ATTN_WIDTH),
                         lambda b, i, j: (b, 0, i, jnp.clip(j - 3, 0, 5))),
            pl.BlockSpec((None, tm, ATTN_WIDTH), lambda b, i, j: (b, i, 0)),
        ),
        scratch_shapes=[pltpu.VMEM((tm, D_MODEL), BF16), pltpu.VMEM((tm, D_MODEL), BF16)],
        compiler_params=pltpu.CompilerParams(
            dimension_semantics=("parallel", "parallel", "arbitrary"),
            vmem_limit_bytes=VMEM_LIMIT),
        name="attn_in",
    )(x, mod, mod, g, w, cos, sin, cosp, sinp, perm)


def _lane_of_odd_head(lane):
    return (lane // (HEAD_DIM // 2)) % 2 == 1


def _mask_bias(valid):
    return jnp.where(valid, 0.0, NEG_INF).astype(F32)


def _attn_pairs(get_q, get_k, get_v, bias, put_acc):
    tq = bias.shape[0]
    lane = lax.broadcasted_iota(jnp.int32, (tq, LANES), 1)
    q_is_odd = _lane_of_odd_head(lane)
    mx_tile = jnp.zeros((tq, LANES), F32)
    den_tile = jnp.ones((tq, LANES), F32)
    ones = jnp.ones((bias.shape[1], LANES), BF16)
    bias2 = jnp.concatenate([bias, bias], axis=0)
    for b in range(ATTN_WIDTH // LANES):
        sl = slice(b * LANES, (b + 1) * LANES)
        qb = get_q(sl).astype(F32)
        kb = get_k(sl)
        vb = jnp.concatenate([get_v(sl), ones], axis=1)
        qm = jnp.concatenate([jnp.where(q_is_odd, 0.0, qb), jnp.where(q_is_odd, qb, 0.0)], axis=0).astype(BF16)
        sc = lax.dot_general(qm, kb, (((1,), (1,)), ((), ())), preferred_element_type=F32) + bias2
        mx = jnp.max(sc, axis=1, keepdims=True)
        p = jnp.exp2((sc - mx).astype(BF16))
        acc = jnp.dot(p, vb, preferred_element_type=F32)
        for odd in (0, 1):
            rows = slice(odd * tq, (odd + 1) * tq)
            mx_tile = jnp.where(lane == 2 * b + odd, mx[rows], mx_tile)
            den_tile = jnp.where(lane == 2 * b + odd, acc[rows, LANES:], den_tile)
        put_acc(sl, jnp.where(q_is_odd, acc[tq:, :LANES], acc[:tq, :LANES]))
    return mx_tile, den_tile


def _pair_factor(tile, b):
    lane = lax.broadcasted_iota(jnp.int32, tile.shape, 1)
    return jnp.where(_lane_of_odd_head(lane), tile[:, 2 * b + 1:2 * b + 2], tile[:, 2 * b:2 * b + 1])


def _merge_groups(stats, get_accs, put_o):
    top = functools.reduce(jnp.maximum, [mx for mx, _ in stats])
    ws = [jnp.exp2(mx - top) for mx, _ in stats]
    inv = 1.0 / sum(w * den for w, (_, den) in zip(ws, stats))
    fs = [w * inv for w in ws]
    for b in range(ATTN_WIDTH // LANES):
        sl = slice(b * LANES, (b + 1) * LANES)
        put_o(sl, sum(_pair_factor(f, b) * get_acc(sl) for f, get_acc in zip(fs, get_accs)))


def _window_mask(tile, tq, m):
    nk = tq + 2 * RADIUS
    row = lax.broadcasted_iota(jnp.int32, (tq, nk), 0)
    col = lax.broadcasted_iota(jnp.int32, (tq, nk), 1)
    kabs = tile * tq - RADIUS + col
    return _mask_bias((jnp.abs(col - RADIUS - row) <= RADIUS) & (kabs >= 0) & (kabs < m))


def _stack_rows(read, prev_ref, cur_ref, next_ref, lo, cnt):
    halo, cur_n = prev_ref.shape[-2], cur_ref.shape[-2]
    a, b = lo, lo + cnt
    parts = []
    if a < 0:
        parts.append(read(prev_ref, halo + a, halo))
        a = 0
    parts.append(read(cur_ref, a, min(b, cur_n)))
    if b > cur_n:
        parts.append(read(next_ref, 0, b - cur_n))
    return parts


def _kv_getters(stack):
    get_k = lambda sl: stack(sl)
    get_v = lambda sl: stack(slice(ATTN_WIDTH + sl.start, ATTN_WIDTH + sl.stop))
    return get_k, get_v


def _attn16_kernel(q_ref, kvp_ref, kvc_ref, kvn_ref, o_ref, mx_ref, den_ref, *, m):
    tq = ATTN_TQ
    n_sub = q_ref.shape[0] // tq
    for t in range(n_sub):
        rows = slice(t * tq, (t + 1) * tq)
        bias = _window_mask(pl.program_id(2) * n_sub + t, tq, m)

        def stack(cols):
            return jnp.concatenate(_stack_rows(lambda ref, a, b: ref[a:b, cols], kvp_ref, kvc_ref, kvn_ref,
                                               t * tq - RADIUS, tq + 2 * RADIUS), axis=0)

        def put_acc(sl, val):
            o_ref[rows, sl] = val.astype(BF16)

        mx_ref[rows], den_ref[rows] = _attn_pairs(lambda sl: q_ref[rows, sl], *_kv_getters(stack), bias, put_acc)


def _attn4_kernel(q_ref, kvp_ref, kvc_ref, kvn_ref, o_ref, mx_ref, den_ref, *, m):
    ns, halo = q_ref.shape[0], kvp_ref.shape[1]
    rows = ATTN_TQ // ns
    span = rows + 2 * halo
    n_sub = q_ref.shape[1] // rows
    n = lax.broadcasted_iota(jnp.int32, (ns * rows, ns * span), 0)
    c = lax.broadcasted_iota(jnp.int32, (ns * rows, ns * span), 1)
    for t in range(n_sub):
        base = (pl.program_id(2) * n_sub + t) * rows
        rs = slice(t * rows, (t + 1) * rows)
        krow = base - halo + c % span
        rel = ns * (krow - (base + n % rows)) + (c // span - n // rows)
        bias = _mask_bias((jnp.abs(rel) <= RADIUS) & (krow >= 0) & (krow < m))

        def stack(cols):
            parts = []
            for s in range(ns):
                parts += _stack_rows(lambda ref, a, b: ref[s, a:b, cols], kvp_ref, kvc_ref, kvn_ref,
                                     t * rows - halo, span)
            return jnp.concatenate(parts, axis=0)

        flat = lambda ref: (lambda sl: ref[:, rs, sl].reshape(ns * rows, LANES))

        def put_acc(sl, val):
            o_ref[:, rs, sl] = val.astype(BF16).reshape(ns, rows, LANES)

        mx, den = _attn_pairs(flat(q_ref), *_kv_getters(stack), bias, put_acc)
        mx_ref[:, rs, :] = mx.reshape(ns, rows, LANES)
        den_ref[:, rs, :] = den.reshape(ns, rows, LANES)


def _attn1_kernel(q_ref, kvp_ref, kvc_ref, kvn_ref, o2_ref, mx2_ref, den2_ref, o1_ref, mx1_ref, den1_ref,
                  unperm_ref, o_ref, acc_scr, a2_scr, a1_scr, stat_scr, *, m):
    tq = ATTN_TQ
    n_sub = q_ref.shape[0] // tq
    srow, n_srow = tq // STREAMS, mx2_ref.shape[1]
    for src, dst in ((o2_ref, a2_scr), (o1_ref, a1_scr)):
        dst[...] = jnp.dot(unperm_ref[...], src[...].reshape(dst.shape), preferred_element_type=F32)
    for k, src in enumerate((mx2_ref, den2_ref, mx1_ref, den1_ref)):
        for r in range(STREAMS):
            stat_scr[k, r * STAT_PITCH:r * STAT_PITCH + n_srow, :] = src[r]
    for t in range(n_sub):
        rows = slice(t * tq, (t + 1) * tq)
        bias = _window_mask(pl.program_id(1) * n_sub + t, tq, m)

        def stack(cols):
            return jnp.concatenate(_stack_rows(lambda ref, a, b: ref[a:b, cols], kvp_ref, kvc_ref, kvn_ref,
                                               t * tq - RADIUS, tq + 2 * RADIUS), axis=0)

        def natural(k):
            return jnp.concatenate([stat_scr[k, pl.ds(t * srow + ii, STREAMS, stride=STAT_PITCH), :]
                                    for ii in range(srow)], axis=0)

        def put_acc(sl, val):
            acc_scr[:, sl] = val

        def put_o(sl, val):
            o_ref[rows, sl] = val.astype(BF16)

        stats = [_attn_pairs(lambda sl: q_ref[rows, sl], *_kv_getters(stack), bias, put_acc),
                 (natural(0), natural(1)), (natural(2), natural(3))]
        _merge_groups(stats, [lambda sl: acc_scr[:, sl], lambda sl: a2_scr[rows, sl],
                              lambda sl: a1_scr[rows, sl]], put_o)


def _attention(qkv0, qkv12):
    bsz, s, _ = qkv0.shape
    m16 = s // STREAMS
    tq, half = ATTN_TQ, RADIUS
    cur_rows = min(ATTN_SUB * tq, m16)
    params = lambda n: pltpu.CompilerParams(dimension_semantics=("parallel",) * n,
                                            vmem_limit_bytes=VMEM_LIMIT)
    same = lambda i: i

    def halo_idx(cur_n, halo_n, total):
        per = cur_n // halo_n
        return (lambda i: jnp.maximum(i * per - 1, 0)), (lambda i: jnp.minimum((i + 1) * per, total // halo_n - 1))

    def spec16(nrows, row_idx, col, width=ATTN_WIDTH):
        return pl.BlockSpec((None, None, nrows, width), lambda b, r, i: (b, r, row_idx(i), col))
    prv, nxt = halo_idx(cur_rows, half, m16)
    stat16 = jax.ShapeDtypeStruct((bsz, STREAMS, m16, LANES), F32)
    o2, mx2, den2 = pl.pallas_call(
        functools.partial(_attn16_kernel, m=m16),
        out_shape=(jax.ShapeDtypeStruct((bsz, STREAMS, m16, ATTN_WIDTH), BF16), stat16, stat16),
        grid=(bsz, STREAMS, m16 // cur_rows),
        in_specs=[spec16(cur_rows, same, 5), spec16(half, prv, 1, 2 * ATTN_WIDTH),
                  spec16(cur_rows, same, 1, 2 * ATTN_WIDTH), spec16(half, nxt, 1, 2 * ATTN_WIDTH)],
        out_specs=(spec16(cur_rows, same, 0), spec16(cur_rows, same, 0, LANES), spec16(cur_rows, same, 0, LANES)),
        compiler_params=params(3), name="attn_dil16",
    )(*([qkv12] * 4))

    ns = STREAMS // 4
    rows4, halo4 = cur_rows // ns, half // ns

    def view4(a):
        return a.reshape(bsz, ns, 4, m16, a.shape[-1])

    def spec4(nrows, row_idx, col, width=ATTN_WIDTH):
        return pl.BlockSpec((None, ns, None, nrows, width), lambda b, r, i: (b, 0, r, row_idx(i), col))
    prv4, nxt4 = halo_idx(rows4, halo4, m16)
    q4 = view4(qkv12)
    stat4 = jax.ShapeDtypeStruct((bsz, ns, 4, m16, LANES), F32)
    o1, mx1, den1 = pl.pallas_call(
        functools.partial(_attn4_kernel, m=m16),
        out_shape=(jax.ShapeDtypeStruct((bsz, ns, 4, m16, ATTN_WIDTH), BF16), stat4, stat4),
        grid=(bsz, 4, m16 // rows4),
        in_specs=[spec4(rows4, same, 4), spec4(halo4, prv4, 0, 2 * ATTN_WIDTH),
                  spec4(rows4, same, 0, 2 * ATTN_WIDTH), spec4(halo4, nxt4, 0, 2 * ATTN_WIDTH)],
        out_specs=(spec4(rows4, same, 0), spec4(rows4, same, 0, LANES), spec4(rows4, same, 0, LANES)),
        compiler_params=params(3), name="attn_dil4",
    )(q4, q4, q4, q4)
    streams = lambda a: a.reshape(bsz, STREAMS, m16, a.shape[-1])

    cur1 = ATTN_SUB * tq

    def spec1(nrows, row_idx, col, width=ATTN_WIDTH):
        return pl.BlockSpec((None, nrows, width), lambda b, i: (b, row_idx(i), col))
    prv1, nxt1 = halo_idx(cur1, half, s)
    sm = lambda width: pl.BlockSpec((None, STREAMS, cur1 // STREAMS, width), lambda b, i: (b, 0, i, 0))
    run = [sm(ATTN_WIDTH), sm(LANES), sm(LANES)]
    unperm = jnp.asarray(_stream_perm(cur1).T, BF16)
    return pl.pallas_call(
        functools.partial(_attn1_kernel, m=s),
        out_shape=jax.ShapeDtypeStruct((bsz, s, ATTN_WIDTH), BF16),
        grid=(bsz, s // cur1),
        in_specs=[spec1(cur1, same, 2), spec1(half, prv1, 0, 2 * ATTN_WIDTH),
                  spec1(cur1, same, 0, 2 * ATTN_WIDTH), spec1(half, nxt1, 0, 2 * ATTN_WIDTH)] + run + run
                 + [pl.BlockSpec((cur1, cur1), lambda b, i: (0, 0))],
        out_specs=spec1(cur1, same, 0),
        scratch_shapes=[pltpu.VMEM((tq, ATTN_WIDTH), F32), pltpu.VMEM((cur1, ATTN_WIDTH), F32),
                        pltpu.VMEM((cur1, ATTN_WIDTH), F32), pltpu.VMEM((4, STREAMS * STAT_PITCH, LANES), F32)],
        compiler_params=params(2), name="attn_dil1",
    )(*([qkv0] * 4), o2, mx2, den2, streams(o1), streams(mx1), streams(den1), unperm)


def _attn_out_kernel(o_ref, z_ref, x_ref, gate_ref, w_ref, out_ref):
    y = (o_ref[...].astype(F32) * _silu(z_ref[...].astype(F32))).astype(BF16)
    out = jnp.dot(y, w_ref[...], preferred_element_type=F32)
    out_ref[...] = x_ref[...] + gate_ref[...] * out


def _attn_out(o, z, x, mod, w, tm=512):
    bsz, s, _ = x.shape
    row = pl.BlockSpec((None, tm, D_MODEL), lambda b, i: (b, i, 0))
    return pl.pallas_call(
        _attn_out_kernel,
        out_shape=jax.ShapeDtypeStruct(x.shape, F32),
        grid=(bsz, s // tm),
        in_specs=[row, row, row,
                  pl.BlockSpec((None, 1, D_MODEL), lambda b, i: (b, 0, 2)),
                  pl.BlockSpec((ATTN_WIDTH, D_MODEL), lambda b, i: (0, 0))],
        out_specs=row,
        compiler_params=pltpu.CompilerParams(
            dimension_semantics=("parallel", "parallel"),
            vmem_limit_bytes=VMEM_LIMIT),
        name="attn_out",
    )(o, z, x, mod, w)


def _cmul(ar, ai, br, bi):
    return ar * br - ai * bi, ar * bi + ai * br


def _s5_prep_kernel(lam_re_a, lam_im_a, lam_re_b, lam_im_b, logdt, bt_re, bt_im,
                    c_re, c_im, d_diag, toep_ref, mb_ref, mc_ref, al_ref, v_scr):
    lane = lax.broadcasted_iota(jnp.int32, (STATE, LANES), 1)
    a_idx = lane // GROUP_CH
    n_tiles = CHUNK // T_PER_VREG

    mb_rows = []
    v_rhs = []
    v_lhs = []
    mc_cols = []
    zbar_f = None
    for dr in range(2):
        dt = jnp.exp(logdt[dr])
        lr = jnp.minimum(lam_re_a[dr], LAMBDA_RE_MAX)
        li = lam_im_a[dr]
        ldr, ldi = lr * dt, li * dt

        mag = jnp.exp(ldr)
        l1r, l1i = mag * jnp.cos(ldi), mag * jnp.sin(ldi)
        small = [(jnp.ones_like(l1r), jnp.zeros_like(l1r)), (l1r, l1i)]
        for _ in range(T_PER_VREG - 1):
            small.append(_cmul(*small[-1], l1r, l1i))

        def lane_powers(exponent_of_step):
            pr, pi = small[exponent_of_step[0]]
            pr, pi = jnp.broadcast_to(pr, (STATE, LANES)), jnp.broadcast_to(pi, (STATE, LANES))
            for a in range(1, T_PER_VREG):
                er, ei = small[exponent_of_step[a]]
                pr, pi = jnp.where(a_idx == a, er, pr), jnp.where(a_idx == a, ei, pi)
            return pr, pi

        den = lr * lr + li * li
        nr, ni = l1r - 1.0, l1i
        cr, ci = (nr * lr + ni * li) / den, (ni * lr - nr * li) / den
        bbr, bbi = _cmul(cr, ci, bt_re[dr], bt_im[dr])
        l8r, l8i = small[T_PER_VREG]
        tiles = [None] * n_tiles
        if dr == 0:
            pr, pi = lane_powers([T_PER_VREG - 1 - a for a in range(T_PER_VREG)])
            cur = _cmul(pr, pi, bbr, bbi)
            for jt in range(n_tiles - 1, -1, -1):
                tiles[jt] = cur
                if jt:
                    cur = _cmul(cur[0], cur[1], l8r, l8i)
            v_rhs.append(_cmul(tiles[n_tiles - 1][0], tiles[n_tiles - 1][1], l1r, l1i))
            zbar_f = (jnp.where(lane < GROUP_CH, bbr, 0.0), jnp.where(lane < GROUP_CH, bbi, 0.0))
        else:
            pr, pi = lane_powers(list(range(T_PER_VREG)))
            cur = _cmul(pr, pi, bbr, bbi)
            for jt in range(n_tiles):
                tiles[jt] = cur
                if jt < n_tiles - 1:
                    cur = _cmul(cur[0], cur[1], l8r, l8i)
            v_rhs.append(tiles[0])
        mb_rows.append(jnp.concatenate([t[0] for t in tiles], axis=1))
        mb_rows.append(jnp.concatenate([t[1] for t in tiles], axis=1))
        a_r, a_i = l8r, l8i
        for _ in range(int(math.log2(CHUNK // T_PER_VREG))):
            a_r, a_i = _cmul(a_r, a_i, a_r, a_i)
        al_ref[2 * dr] = jnp.broadcast_to(a_r, (STATE, LANES))
        al_ref[2 * dr + 1] = jnp.broadcast_to(a_i, (STATE, LANES))

        lrb = jnp.minimum(lam_re_b[dr], LAMBDA_RE_MAX)
        lib = lam_im_b[dr]
        mag = jnp.exp(lrb * dt)
        pw_r, pw_i = mag * jnp.cos(lib * dt), mag * jnp.sin(lib * dt)
        pows = [(pw_r, pw_i)]
        for _ in range(int(math.log2(CHUNK)) - 1):
            pows.append(_cmul(*pows[-1], *pows[-1]))
        cre, cim = c_re[dr], c_im[dr]
        xr, xi = _cmul(cre, cim, *pows[0])
        for k in range(int(math.log2(CHUNK))):
            yr, yi = _cmul(xr, xi, *pows[k])
            if dr == 0:
                xr, xi = jnp.concatenate([xr, yr], 0), jnp.concatenate([xi, yi], 0)
            else:
                xr, xi = jnp.concatenate([yr, xr], 0), jnp.concatenate([yi, xi], 0)
        mc_cols += [xr, -xi]
        xr, xi = cre, cim
        for k in range(int(math.log2(n_tiles))):
            yr, yi = _cmul(xr, xi, *pows[k + int(math.log2(T_PER_VREG))])
            if dr == 0:
                xr, xi = jnp.concatenate([yr, xr], 0), jnp.concatenate([yi, xi], 0)
            else:
                xr, xi = jnp.concatenate([xr, yr], 0), jnp.concatenate([xi, yi], 0)
        v_lhs.append((xr, xi))

    mb_ref[...] = jnp.concatenate(mb_rows, axis=0).astype(BF16)
    mc_ref[...] = jnp.concatenate(mc_cols, axis=1).astype(BF16)

    def cdot(lhs, rhs):
        return (jnp.dot(lhs[0], rhs[0], precision=HI, preferred_element_type=F32)
                - jnp.dot(lhs[1], rhs[1], precision=HI, preferred_element_type=F32))

    out_f = cdot(v_lhs[0], v_rhs[0])
    zero_rows = jnp.zeros((CHUNK_ROWS // T_PER_VREG - GROUP_CH, STATE), F32)
    lag0_lhs = (jnp.concatenate([c_re[0], zero_rows], 0), jnp.concatenate([c_im[0], zero_rows], 0))
    out_b = cdot(v_lhs[1], v_rhs[1]) + cdot(lag0_lhs, zbar_f)
    out_b = out_b + jnp.concatenate([d_diag[...], jnp.zeros((CHUNK_ROWS // T_PER_VREG - GROUP_CH, LANES), F32)], 0)
    for jt in range(n_tiles):
        rows = slice(jt * GROUP_CH, (jt + 1) * GROUP_CH)
        v_scr[:, jt * LANES:(jt + 1) * LANES] = out_f[rows]
        v_scr[:, (n_tiles + jt) * LANES:(n_tiles + jt + 1) * LANES] = out_b[rows]
    v_scr[:, 2 * CHUNK_ROWS:] = jnp.zeros((GROUP_CH, LANES), F32)

    for k in range(T_PER_VREG):
        vk = v_scr[:, k * GROUP_CH:k * GROUP_CH + 2 * CHUNK_ROWS]
        for t_out in range(CHUNK):
            off = (CHUNK - t_out) * GROUP_CH
            if off % LANES == k * GROUP_CH:
                al_off = off - k * GROUP_CH
                toep_ref[t_out * GROUP_CH:(t_out + 1) * GROUP_CH, :] = (
                    vk[:, al_off:al_off + CHUNK_ROWS].astype(BF16))


def _s5_prep(lam_re, lam_im, log_dt, b_re, b_im, c_re, c_im, d_skip):
    g = SSM_GROUPS
    d_diag = d_skip.reshape(g, GROUP_CH, 1) * jnp.eye(GROUP_CH, LANES, dtype=F32)
    lam_a = lambda a: a.reshape(2, g, STATE, 1)
    lam_b = lambda a: a.reshape(2, g, 1, STATE)
    tile_b = lambda a: jnp.tile(a, (1, 1, 1, T_PER_VREG))
    dir_spec = lambda r, c: pl.BlockSpec((2, None, r, c), lambda i: (0, i, 0, 0))
    return pl.pallas_call(
        _s5_prep_kernel,
        out_shape=(jax.ShapeDtypeStruct((g, CHUNK_ROWS, CHUNK_ROWS), BF16),
                   jax.ShapeDtypeStruct((g, 4 * STATE, CHUNK_ROWS), BF16),
                   jax.ShapeDtypeStruct((g, CHUNK_ROWS, 4 * STATE), BF16),
                   jax.ShapeDtypeStruct((g, 4, STATE, LANES), F32)),
        grid=(g,),
        in_specs=[dir_spec(STATE, 1), dir_spec(STATE, 1), dir_spec(1, STATE), dir_spec(1, STATE),
                  dir_spec(1, 1), dir_spec(STATE, LANES), dir_spec(STATE, LANES),
                  dir_spec(GROUP_CH, STATE), dir_spec(GROUP_CH, STATE),
                  pl.BlockSpec((None, GROUP_CH, LANES), lambda i: (i, 0, 0))],
        out_specs=(pl.BlockSpec((None, CHUNK_ROWS, CHUNK_ROWS), lambda i: (i, 0, 0)),
                   pl.BlockSpec((None, 4 * STATE, CHUNK_ROWS), lambda i: (i, 0, 0)),
                   pl.BlockSpec((None, CHUNK_ROWS, 4 * STATE), lambda i: (i, 0, 0)),
                   pl.BlockSpec((None, 4, STATE, LANES), lambda i: (i, 0, 0, 0))),
        scratch_shapes=[pltpu.VMEM((GROUP_CH, V_LANES), F32)],
        compiler_params=pltpu.CompilerParams(dimension_semantics=("parallel",)),
        name="s5_prep",
    )(lam_a(lam_re), lam_a(lam_im), lam_b(lam_re), lam_b(lam_im), log_dt.reshape(2, g, 1, 1),
      tile_b(b_re), tile_b(b_im), c_re, c_im, d_diag)


def _slab_perm():
    dst = np.arange(SLAB)
    perm = np.zeros((SLAB, SLAB), np.float32)
    perm[dst, (dst % SLAB_CHUNKS) * SUBLANES + dst // SLAB_CHUNKS] = 1.0
    return perm


def _s5_in_kernel(x_ref, mod_ref, g_ref, w_ref, p_ref, u_ref, z_ref, h_scr, *, chunks_per_seq):
    n_chunks = x_ref.shape[0]
    for b in range(n_chunks // chunks_per_seq):
        rows = slice(b * chunks_per_seq, (b + 1) * chunks_per_seq)
        h = _rms_modulate(x_ref[rows], g_ref[...], mod_ref[b:b + 1, D_MODEL:2 * D_MODEL],
                          mod_ref[b:b + 1, :D_MODEL])
        h2 = h.reshape(chunks_per_seq * SUBLANES, D_MODEL).astype(BF16)
        for s in range(chunks_per_seq // SLAB_CHUNKS):
            hp = jnp.dot(p_ref[...], h2[s * SLAB:(s + 1) * SLAB], preferred_element_type=F32)
            c0 = b * chunks_per_seq + s * SLAB_CHUNKS
            for k in range(SUBLANES):
                h_scr[k, c0:c0 + SLAB_CHUNKS, :] = hp[k * SLAB_CHUNKS:(k + 1) * SLAB_CHUNKS].astype(BF16)
    for k in range(SUBLANES):
        ht = h_scr[k].astype(F32).T.astype(BF16)
        uz = jnp.dot(w_ref[...], ht, preferred_element_type=F32)
        u_ref[k] = uz[:SSM_WIDTH].astype(BF16)
        z_ref[k] = uz[SSM_WIDTH:].astype(BF16)


def _s5_in(x3, mod, g, w_t, chunks_per_seq):
    n_chunks = x3.shape[0]
    once = lambda shape: pl.BlockSpec(shape, lambda t: (0,) * len(shape), pipeline_mode=pl.Buffered(1))
    out = pl.BlockSpec((SUBLANES, SSM_WIDTH, n_chunks), lambda t: (t, 0, 0))
    return pl.pallas_call(
        functools.partial(_s5_in_kernel, chunks_per_seq=chunks_per_seq),
        out_shape=(jax.ShapeDtypeStruct((CHUNK, SSM_WIDTH, n_chunks), BF16),) * 2,
        grid=(CHUNK // SUBLANES,),
        in_specs=[pl.BlockSpec((n_chunks, SUBLANES, D_MODEL), lambda t: (0, t, 0)),
                  once(mod.shape), once((1, D_MODEL)), once((2 * SSM_WIDTH, D_MODEL)),
                  once((SLAB, SLAB))],
        out_specs=(out, out),
        scratch_shapes=[pltpu.VMEM((SUBLANES, n_chunks, D_MODEL), BF16)],
        compiler_params=pltpu.CompilerParams(
            dimension_semantics=("parallel",), vmem_limit_bytes=VMEM_LIMIT_BIG),
        name="s5_in",
    )(x3, mod, g, w_t, jnp.asarray(_slab_perm(), BF16))


def _gelu_tanh(v):
    return 0.5 * v * (1.0 + jnp.tanh(math.sqrt(2.0 / math.pi) * (v + 0.044715 * (v * v * v))))


def _s5_core_kernel(u_ref, toep_ref, mb_ref, mc_ref, al_ref, o_ref, *, chunks_per_seq):
    n_chunks = u_ref.shape[-1]
    ub = u_ref[...].reshape(CHUNK_ROWS, n_chunks)
    inc = jnp.dot(mb_ref[...], ub, preferred_element_type=F32)
    pos = lax.broadcasted_iota(jnp.int32, (STATE, n_chunks), 1) % chunks_per_seq
    reps = n_chunks // LANES

    def lane_tile(a):
        return jnp.concatenate([a] * reps, axis=1) if reps > 1 else a

    states = []
    for dr in range(2):
        xr = inc[(2 * dr) * STATE:(2 * dr + 1) * STATE]
        xi = inc[(2 * dr + 1) * STATE:(2 * dr + 2) * STATE]
        ar, ai = al_ref[2 * dr], al_ref[2 * dr + 1]

        def shifted(v, step):
            if dr == 0:
                return jnp.where(pos >= step, pltpu.roll(v, step, 1), 0.0)
            return jnp.where(pos < chunks_per_seq - step, pltpu.roll(v, n_chunks - step, 1), 0.0)

        step = 1
        while step < chunks_per_seq:
            sr, si = shifted(xr, step), shifted(xi, step)
            tr, ti = lane_tile(ar), lane_tile(ai)
            xr, xi = xr + tr * sr - ti * si, xi + tr * si + ti * sr
            ar, ai = _cmul(ar, ai, ar, ai)
            step *= 2
        states += [shifted(xr, 1), shifted(xi, 1)]
    h_in = jnp.concatenate(states, axis=0).astype(BF16)
    y = (jnp.dot(toep_ref[...], ub, preferred_element_type=F32)
         + jnp.dot(mc_ref[...], h_in, preferred_element_type=F32))
    o_ref[...] = _gelu_tanh(y).astype(BF16).reshape(CHUNK, GROUP_CH, n_chunks)


def _s5_core(u_cl, toep, mb, mc, al, chunks_per_seq):
    n_chunks = u_cl.shape[-1]
    u4 = u_cl.reshape(CHUNK, SSM_GROUPS, GROUP_CH, n_chunks)
    grp = pl.BlockSpec((CHUNK, None, GROUP_CH, n_chunks), lambda g: (0, g, 0, 0))
    out = pl.pallas_call(
        functools.partial(_s5_core_kernel, chunks_per_seq=chunks_per_seq),
        out_shape=jax.ShapeDtypeStruct(u4.shape, BF16),
        grid=(SSM_GROUPS,),
        in_specs=[grp,
                  pl.BlockSpec((None, CHUNK_ROWS, CHUNK_ROWS), lambda g: (g, 0, 0)),
                  pl.BlockSpec((None, 4 * STATE, CHUNK_ROWS), lambda g: (g, 0, 0)),
                  pl.BlockSpec((None, CHUNK_ROWS, 4 * STATE), lambda g: (g, 0, 0)),
                  pl.BlockSpec((None, 4, STATE, LANES), lambda g: (g, 0, 0, 0))],
        out_specs=grp,
        compiler_params=pltpu.CompilerParams(
            dimension_semantics=("parallel",), vmem_limit_bytes=VMEM_LIMIT),
        name="s5_core",
    )(u4, toep, mb, mc, al)
    return out.reshape(CHUNK, SSM_WIDTH, n_chunks)


def _s5_out_kernel(g_ref, z_ref, wglu_ref, wout_ref, pt_ref, x_ref, mod_ref, fg_ref, o_ref, y_scr,
                   *, final, chunks_per_seq):
    half = pl.program_id(1)
    n_half = x_ref.shape[0]

    @pl.when(half == 0)
    def _():
        for k in range(SUBLANES):
            gb = g_ref[k]
            glu = jnp.dot(wglu_ref[...], gb, preferred_element_type=F32)
            y = gb.astype(F32) * (1.0 / (1.0 + jnp.exp(-glu))) * _silu(z_ref[k].astype(F32))
            y_scr[k] = y.T.astype(BF16)

    for hv in range(y_scr.shape[1] // n_half):
        @pl.when(half == hv)
        def _():
            for s in range(n_half // SLAB_CHUNKS):
                c0 = hv * n_half + s * SLAB_CHUNKS
                src = jnp.concatenate([y_scr[k, c0:c0 + SLAB_CHUNKS, :] for k in range(SUBLANES)], axis=0)
                yp = jnp.dot(pt_ref[...], src, preferred_element_type=F32).astype(BF16)
                out = jnp.dot(yp, wout_ref[...], preferred_element_type=F32)
                b = c0 // chunks_per_seq
                upd = mod_ref[b:b + 1, 2 * D_MODEL:] * out
                rows = slice(s * SLAB_CHUNKS, (s + 1) * SLAB_CHUNKS)
                xn = x_ref[rows] + upd.reshape(SLAB_CHUNKS, SUBLANES, D_MODEL)
                if final:
                    xn = xn * lax.rsqrt(jnp.mean(xn * xn, axis=-1, keepdims=True) + NORM_EPS) * fg_ref[...]
                o_ref[rows] = xn


def _s5_out(g_cl, z_cl, wglu_t, w_out, x3, mod, final_g, final, chunks_per_seq, n_split=2):
    n_chunks = x3.shape[0]
    n_half = n_chunks // n_split
    once = lambda shape: pl.BlockSpec(shape, lambda t, h: (0,) * len(shape), pipeline_mode=pl.Buffered(1))
    act = pl.BlockSpec((SUBLANES, SSM_WIDTH, n_chunks), lambda t, h: (t, 0, 0))
    row = pl.BlockSpec((n_half, SUBLANES, D_MODEL), lambda t, h: (h, t, 0))
    return pl.pallas_call(
        functools.partial(_s5_out_kernel, final=final, chunks_per_seq=chunks_per_seq),
        out_shape=jax.ShapeDtypeStruct(x3.shape, F32),
        grid=(CHUNK // SUBLANES, n_split),
        in_specs=[act, act, once((SSM_WIDTH, SSM_WIDTH)), once((SSM_WIDTH, D_MODEL)), once((SLAB, SLAB)),
                  row, once(mod.shape), once((1, D_MODEL))],
        out_specs=row,
        scratch_shapes=[pltpu.VMEM((SUBLANES, n_chunks, SSM_WIDTH), BF16)],
        compiler_params=pltpu.CompilerParams(
            dimension_semantics=("parallel", "arbitrary"), vmem_limit_bytes=VMEM_LIMIT_BIG),
        name="s5_out",
    )(g_cl, z_cl, wglu_t, w_out, jnp.asarray(_slab_perm().T, BF16), x3, mod, final_g)


def _rope_tables(s):
    inv_freq = ROPE_THETA ** (-jnp.arange(0, HEAD_DIM, 2, dtype=F32) / HEAD_DIM)
    ang = jnp.arange(s, dtype=F32)[:, None] * inv_freq[None, :]
    reps = LANES // (HEAD_DIM // 2)
    sign = jnp.where(jnp.arange(LANES) < LANES // 2, -1.0, 1.0).astype(F32)
    cos, sin = jnp.tile(jnp.cos(ang), (1, reps)), jnp.tile(jnp.sin(ang), (1, reps)) * sign
    stream_major = lambda a: a.reshape(s // STREAMS, STREAMS, LANES).transpose(1, 0, 2)
    return cos, sin, stream_major(cos), stream_major(sin)


def _prep_attn_w_in(w):
    d_in = w.shape[0]
    w = w.astype(BF16).reshape(d_in, -1, ATTN_HEADS // 2, 2, 2, HEAD_DIM // 2)
    w = w.transpose(0, 1, 2, 4, 3, 5)
    return w[:, np.array(ATTN_IN_BLOCK_ORDER)].reshape(d_in, -1)


def _prep_attn_w_out(w):
    w = w.astype(BF16).reshape(ATTN_HEADS // 2, 2, 2, HEAD_DIM // 2, w.shape[1])
    return w.transpose(0, 2, 1, 3, 4).reshape(ATTN_WIDTH, -1)


def _trunk(x, ada, norm_g, attn_w, s5_w, s5_ops, final_norm_g):
    bsz, s, _ = x.shape
    chunks_per_seq = s // CHUNK
    n_chunks = bsz * chunks_per_seq
    tables = _rope_tables(s)
    fg = final_norm_g.reshape(1, D_MODEL)
    for i in range(DEPTH):
        mod = ada[i].reshape(bsz, 1, 3 * D_MODEL)
        g = norm_g[i].reshape(1, D_MODEL)
        j = i // N_MIXERS
        if i % N_MIXERS == 0:
            w_in, w_out = attn_w[j]
            qkv0, qkv12, z = _attn_in(x, mod, g, w_in, tables)
            x = _attn_out(_attention(qkv0, qkv12), z, x, mod, w_out)
        else:
            w_in_t, wglu_t, w_out = s5_w[j]
            toep, mb, mc, al = s5_ops[j]
            x3 = x.reshape(n_chunks, CHUNK, D_MODEL)
            u_cl, z_cl = _s5_in(x3, ada[i], g, w_in_t, chunks_per_seq)
            g_cl = _s5_core(u_cl, toep, mb, mc, al, chunks_per_seq)
            x3 = _s5_out(g_cl, z_cl, wglu_t, w_out, x3, ada[i], fg, i == DEPTH - 1, chunks_per_seq)
            x = x3.reshape(bsz, s, D_MODEL)
    return x


def kernel(x_prompt, x_sample, c_prompt, c_sample, norm_g, ada_w, ada_b, attn_w_in, attn_w_out,
           ssm_w_in, ssm_lam_re, ssm_lam_im, ssm_log_dt, ssm_b_re, ssm_b_im, ssm_c_re, ssm_c_im,
           ssm_d, ssm_w_glu, ssm_w_out, final_norm_g):
    assert (DEPTH - 1) % N_MIXERS == 1, "the final norm is fused into the last S5 layer"
    assert all(w // (2 * d) == RADIUS for w, d in DILATED_PAIRS)
    assert [d for _, d in DILATED_PAIRS] == [1, 4, STREAMS]
    n_prompt = c_prompt.shape[0]
    ada = _ada(jnp.concatenate([c_prompt, c_sample], axis=0), ada_w, ada_b)
    attn_w = [(_prep_attn_w_in(attn_w_in[j]), _prep_attn_w_out(attn_w_out[j]))
              for j in range(attn_w_in.shape[0])]
    s5_w, s5_ops = [], []
    for j in range(ssm_w_in.shape[0]):
        s5_w.append((ssm_w_in[j].T.astype(BF16), ssm_w_glu[j].T.astype(BF16), ssm_w_out[j].astype(BF16)))
        s5_ops.append(_s5_prep(ssm_lam_re[j], ssm_lam_im[j], ssm_log_dt[j], ssm_b_re[j],
                               ssm_b_im[j], ssm_c_re[j], ssm_c_im[j], ssm_d[j]))
    y_prompt = _trunk(x_prompt, ada[:, :n_prompt], norm_g, attn_w, s5_w, s5_ops, final_norm_g)
    y_sample = _trunk(x_sample, ada[:, n_prompt:], norm_g, attn_w, s5_w, s5_ops, final_norm_g)
    return (y_prompt, y_sample)
```

```python
import functools
import math

import numpy as np
import jax
import jax.numpy as jnp
from jax import lax
from jax.experimental import pallas as pl
from jax.experimental.pallas import tpu as pltpu

D_MODEL = 1024
DEPTH = 4
N_MIXERS = 2
ATTN_HEADS = 16
HEAD_DIM = 64
ATTN_WIDTH = ATTN_HEADS * HEAD_DIM
DILATED_PAIRS = ((128, 1), (512, 4), (2048, 16))
N_DIL = len(DILATED_PAIRS)
ROPE_THETA = 10000.0
SSM_WIDTH = D_MODEL
GROUP_CH = 16
SSM_GROUPS = SSM_WIDTH // GROUP_CH
STATE = 64
LAMBDA_RE_MAX = -1e-4
NORM_EPS = 1e-6
NEG_INF = -1e30

LANES = 128
SUBLANES = 8
CHUNK = 64
CHUNK_ROWS = CHUNK * GROUP_CH
T_PER_VREG = LANES // GROUP_CH
V_LANES = 2 * CHUNK_ROWS + LANES
STREAMS = 16
RADIUS = 64
ATTN_TQ = 128
ATTN_SUB = 4
STAT_PITCH = 40
ATTN_IN_BLOCK_ORDER = (1, 2, 0, 4, 5, 7, 8, 3, 6, 9)
PERM_ROWS = 512
LOG2_E = math.log2(math.e)
SLAB = 256
SLAB_CHUNKS = SLAB // SUBLANES
VMEM_LIMIT = 48 * 1024 * 1024
VMEM_LIMIT_BIG = 56 * 1024 * 1024

F32 = jnp.float32
BF16 = jnp.bfloat16
HI = lax.Precision.HIGHEST


def _sigmoid(v):
    return 0.5 * jnp.tanh(0.5 * v) + 0.5


def _silu(v):
    return v * _sigmoid(v)


def _rms_modulate(x, g, scale, shift):
    rs = lax.rsqrt(jnp.mean(x * x, axis=-1, keepdims=True) + NORM_EPS)
    return (x * rs * g) * (1.0 + scale) + shift


def _ada_kernel(c_ref, w_ref, b_ref, o_ref):
    o_ref[...] = jnp.dot(_silu(c_ref[...]), w_ref[...], precision=HI,
                         preferred_element_type=F32) + b_ref[...]


def _ada(c, ada_w, ada_b):
    nb = c.shape[0]
    return pl.pallas_call(
        _ada_kernel,
        out_shape=jax.ShapeDtypeStruct((DEPTH, nb, 3 * D_MODEL), F32),
        grid=(DEPTH, 3),
        in_specs=[
            pl.BlockSpec((nb, D_MODEL), lambda i, j: (0, 0)),
            pl.BlockSpec((None, D_MODEL, D_MODEL), lambda i, j: (i, 0, j)),
            pl.BlockSpec((None, 1, D_MODEL), lambda i, j: (i, 0, j)),
        ],
        out_specs=pl.BlockSpec((None, nb, D_MODEL), lambda i, j: (i, 0, j)),
        name="ada",
    )(c, ada_w, ada_b.reshape(DEPTH, 1, 3 * D_MODEL))


def _attn_in_kernel(x_ref, shift_ref, scale_ref, g_ref, w_ref, cos_ref, sin_ref, cosp_ref, sinp_ref,
                    perm_ref, qkv0_ref, qkv12_ref, z_ref, h_nat, h_perm):
    j = pl.program_id(2)
    tm = x_ref.shape[0]
    n_sub = tm // PERM_ROWS
    sub_rows = PERM_ROWS // STREAMS

    @pl.when(j == 0)
    def _():
        h = _rms_modulate(x_ref[...], g_ref[...], scale_ref[...], shift_ref[...])
        h_nat[...] = h.astype(BF16)
        for u in range(n_sub):
            rs = slice(u * PERM_ROWS, (u + 1) * PERM_ROWS)
            h_perm[rs, :] = jnp.dot(perm_ref[...], h_nat[rs, :], preferred_element_type=F32).astype(BF16)

    is_q = jnp.logical_or(j == 2, j >= 7)
    is_v = jnp.logical_or(j == 1, jnp.logical_or(j == 4, j == 6))

    def rope_blocks(acc, cos, sin):
        qscale = jnp.where(is_q, HEAD_DIM ** -0.5 * LOG2_E, 1.0).astype(F32)
        cos = jnp.where(is_v, 1.0, cos * qscale)
        sin = jnp.where(is_v, 0.0, sin * qscale)
        for b in range(ATTN_WIDTH // LANES):
            sl = slice(b * LANES, (b + 1) * LANES)
            t = acc[:, sl]
            yield sl, (t * cos + pltpu.roll(t, LANES // 2, 1) * sin).astype(BF16)

    @pl.when(j < 3)
    def _():
        acc = jnp.dot(h_nat[...], w_ref[...], preferred_element_type=F32)
        for sl, blk in rope_blocks(acc, cos_ref[...], sin_ref[...]):
            qkv0_ref[:, sl] = blk

    @pl.when(jnp.logical_and(j >= 3, j < 3 * N_DIL))
    def _():
        acc = jnp.dot(h_perm[...], w_ref[...], preferred_element_type=F32)

        def table(ref):
            return jnp.concatenate([ref[:, u * sub_rows:(u + 1) * sub_rows, :].reshape(PERM_ROWS, LANES)
                                    for u in range(n_sub)], axis=0)

        for sl, blk in rope_blocks(acc, table(cosp_ref), table(sinp_ref)):
            for u in range(n_sub):
                qkv12_ref[:, u * sub_rows:(u + 1) * sub_rows, sl] = (
                    blk[u * PERM_ROWS:(u + 1) * PERM_ROWS].reshape(STREAMS, sub_rows, LANES))

    @pl.when(j == 3 * N_DIL)
    def _():
        z_ref[...] = jnp.dot(h_nat[...], w_ref[...], preferred_element_type=F32).astype(BF16)


def _stream_perm(n):
    dst = np.arange(n)
    rows = n // STREAMS
    perm = np.zeros((n, n), np.float32)
    perm[dst, STREAMS * (dst % rows) + dst // rows] = 1.0
    return perm


def _attn_in(x, mod, g, w, tables, tm=1024):
    bsz, s, _ = x.shape
    cos, sin, cosp, sinp = tables
    rows = tm // STREAMS
    n_col = 3 * N_DIL + 1
    perm = jnp.asarray(_stream_perm(PERM_ROWS), BF16)
    return pl.pallas_call(
        _attn_in_kernel,
        out_shape=(jax.ShapeDtypeStruct((bsz, s, 3 * ATTN_WIDTH), BF16),
                   jax.ShapeDtypeStruct((bsz, STREAMS, s // STREAMS, 6 * ATTN_WIDTH), BF16),
                   jax.ShapeDtypeStruct((bsz, s, ATTN_WIDTH), BF16)),
        grid=(bsz, s // tm, n_col),
        in_specs=[
            pl.BlockSpec((None, tm, D_MODEL), lambda b, i, j: (b, i, 0)),
            pl.BlockSpec((None, 1, D_MODEL), lambda b, i, j: (b, 0, 0)),
            pl.BlockSpec((None, 1, D_MODEL), lambda b, i, j: (b, 0, 1)),
            pl.BlockSpec((1, D_MODEL), lambda b, i, j: (0, 0)),
            pl.BlockSpec((D_MODEL, ATTN_WIDTH), lambda b, i, j: (0, j)),
            pl.BlockSpec((tm, LANES), lambda b, i, j: (i, 0)),
            pl.BlockSpec((tm, LANES), lambda b, i, j: (i, 0)),
            pl.BlockSpec((STREAMS, rows, LANES), lambda b, i, j: (0, i, 0)),
            pl.BlockSpec((STREAMS, rows, LANES), lambda b, i, j: (0, i, 0)),
            pl.BlockSpec((PERM_ROWS, PERM_ROWS), lambda b, i, j: (0, 0)),
        ],
        out_specs=(
            pl.BlockSpec((None, tm, ATTN_WIDTH), lambda b, i, j: (b, i, jnp.clip(j, 0, 2))),
            pl.BlockSpec((None, STREAMS, rows, ATTN_WIDTH),
                         lambda b, i, j: (b, 0, i, jnp.clip(j - 3, 0, 5))),
            pl.BlockSpec((None, tm, ATTN_WIDTH), lambda b, i, j: (b, i, 0)),
        ),
        scratch_shapes=[pltpu.VMEM((tm, D_MODEL), BF16), pltpu.VMEM((tm, D_MODEL), BF16)],
        compiler_params=pltpu.CompilerParams(
            dimension_semantics=("parallel", "parallel", "arbitrary"),
            vmem_limit_bytes=VMEM_LIMIT),
        name="attn_in",
    )(x, mod, mod, g, w, cos, sin, cosp, sinp, perm)


def _lane_of_odd_head(lane):
    return (lane // (HEAD_DIM // 2)) % 2 == 1


def _mask_bias(valid):
    return jnp.where(valid, 0.0, NEG_INF).astype(F32)


def _attn_pairs(get_q, get_k, get_v, bias, put_acc):
    tq = bias.shape[0]
    lane = lax.broadcasted_iota(jnp.int32, (tq, LANES), 1)
    q_is_odd = _lane_of_odd_head(lane)
    mx_tile = jnp.zeros((tq, LANES), F32)
    den_tile = jnp.ones((tq, LANES), F32)
    ones = jnp.ones((bias.shape[1], LANES), BF16)
    bias2 = jnp.concatenate([bias, bias], axis=0)
    for b in range(ATTN_WIDTH // LANES):
        sl = slice(b * LANES, (b + 1) * LANES)
        qb = get_q(sl).astype(F32)
        kb = get_k(sl)
        vb = jnp.concatenate([get_v(sl), ones], axis=1)
        qm = jnp.concatenate([jnp.where(q_is_odd, 0.0, qb), jnp.where(q_is_odd, qb, 0.0)], axis=0).astype(BF16)
        sc = lax.dot_general(qm, kb, (((1,), (1,)), ((), ())), preferred_element_type=F32) + bias2
        mx = jnp.max(sc, axis=1, keepdims=True)
        p = jnp.exp2((sc - mx).astype(BF16))
        acc = jnp.dot(p, vb, preferred_element_type=F32)
        for odd in (0, 1):
            rows = slice(odd * tq, (odd + 1) * tq)
            mx_tile = jnp.where(lane == 2 * b + odd, mx[rows], mx_tile)
            den_tile = jnp.where(lane == 2 * b + odd, acc[rows, LANES:], den_tile)
        put_acc(sl, jnp.where(q_is_odd, acc[tq:, :LANES], acc[:tq, :LANES]))
    return mx_tile, den_tile


def _pair_factor(tile, b):
    lane = lax.broadcasted_iota(jnp.int32, tile.shape, 1)
    head = 2 * b + _lane_of_odd_head(lane).astype(jnp.int32)
    return jnp.take_along_axis(tile, head, axis=1, mode="promise_in_bounds")


def _merge_groups(stats, get_accs, put_o):
    top = functools.reduce(jnp.maximum, [mx for mx, _ in stats])
    ws = [jnp.exp2(mx - top) for mx, _ in stats]
    inv = 1.0 / sum(w * den for w, (_, den) in zip(ws, stats))
    fs = [w * inv for w in ws]
    for b in range(ATTN_WIDTH // LANES):
        sl = slice(b * LANES, (b + 1) * LANES)
        put_o(sl, sum(_pair_factor(f, b) * get_acc(sl) for f, get_acc in zip(fs, get_accs)))


def _window_mask(tile, tq, m):
    nk = tq + 2 * RADIUS
    row = lax.broadcasted_iota(jnp.int32, (tq, nk), 0)
    col = lax.broadcasted_iota(jnp.int32, (tq, nk), 1)
    kabs = tile * tq - RADIUS + col
    return _mask_bias((jnp.abs(col - RADIUS - row) <= RADIUS) & (kabs >= 0) & (kabs < m))


def _stack_rows(read, prev_ref, cur_ref, next_ref, lo, cnt):
    halo, cur_n = prev_ref.shape[-2], cur_ref.shape[-2]
    a, b = lo, lo + cnt
    parts = []
    if a < 0:
        parts.append(read(prev_ref, halo + a, halo))
        a = 0
    parts.append(read(cur_ref, a, min(b, cur_n)))
    if b > cur_n:
        parts.append(read(next_ref, 0, b - cur_n))
    return parts


def _kv_getters(stack):
    get_k = lambda sl: stack(sl)
    get_v = lambda sl: stack(slice(ATTN_WIDTH + sl.start, ATTN_WIDTH + sl.stop))
    return get_k, get_v


def _attn16_kernel(q_ref, kvp_ref, kvc_ref, kvn_ref, o_ref, mx_ref, den_ref, *, m):
    tq = ATTN_TQ
    n_sub = q_ref.shape[0] // tq
    for t in range(n_sub):
        rows = slice(t * tq, (t + 1) * tq)
        bias = _window_mask(pl.program_id(2) * n_sub + t, tq, m)

        def stack(cols):
            return jnp.concatenate(_stack_rows(lambda ref, a, b: ref[a:b, cols], kvp_ref, kvc_ref, kvn_ref,
                                               t * tq - RADIUS, tq + 2 * RADIUS), axis=0)

        def put_acc(sl, val):
            o_ref[rows, sl] = val.astype(BF16)

        mx_ref[rows], den_ref[rows] = _attn_pairs(lambda sl: q_ref[rows, sl], *_kv_getters(stack), bias, put_acc)


def _attn4_kernel(q_ref, kvp_ref, kvc_ref, kvn_ref, o_ref, mx_ref, den_ref, *, m):
    ns, halo = q_ref.shape[0], kvp_ref.shape[1]
    rows = ATTN_TQ // ns
    span = rows + 2 * halo
    n_sub = q_ref.shape[1] // rows
    n = lax.broadcasted_iota(jnp.int32, (ns * rows, ns * span), 0)
    c = lax.broadcasted_iota(jnp.int32, (ns * rows, ns * span), 1)
    for t in range(n_sub):
        base = (pl.program_id(2) * n_sub + t) * rows
        rs = slice(t * rows, (t + 1) * rows)
        krow = base - halo + c % span
        rel = ns * (krow - (base + n % rows)) + (c // span - n // rows)
        bias = _mask_bias((jnp.abs(rel) <= RADIUS) & (krow >= 0) & (krow < m))

        def stack(cols):
            parts = []
            for s in range(ns):
                parts += _stack_rows(lambda ref, a, b: ref[s, a:b, cols], kvp_ref, kvc_ref, kvn_ref,
                                     t * rows - halo, span)
            return jnp.concatenate(parts, axis=0)

        flat = lambda ref: (lambda sl: ref[:, rs, sl].reshape(ns * rows, LANES))

        def put_acc(sl, val):
            o_ref[:, rs, sl] = val.astype(BF16).reshape(ns, rows, LANES)

        mx, den = _attn_pairs(flat(q_ref), *_kv_getters(stack), bias, put_acc)
        mx_ref[:, rs, :] = mx.reshape(ns, rows, LANES)
        den_ref[:, rs, :] = den.reshape(ns, rows, LANES)


def _attn1_kernel(q_ref, kvp_ref, kvc_ref, kvn_ref, o2_ref, mx2_ref, den2_ref, o1_ref, mx1_ref, den1_ref,
                  unperm_ref, o_ref, acc_scr, a2_scr, a1_scr, stat_scr, *, m):
    tq = ATTN_TQ
    n_sub = q_ref.shape[0] // tq
    srow, n_srow = tq // STREAMS, mx2_ref.shape[1]
    for src, dst in ((o2_ref, a2_scr), (o1_ref, a1_scr)):
        dst[...] = jnp.dot(unperm_ref[...], src[...].reshape(dst.shape), preferred_element_type=F32)
    for k, src in enumerate((mx2_ref, den2_ref, mx1_ref, den1_ref)):
        for r in range(STREAMS):
            stat_scr[k, r * STAT_PITCH:r * STAT_PITCH + n_srow, :] = src[r]
    for t in range(n_sub):
        rows = slice(t * tq, (t + 1) * tq)
        bias = _window_mask(pl.program_id(1) * n_sub + t, tq, m)

        def stack(cols):
            return jnp.concatenate(_stack_rows(lambda ref, a, b: ref[a:b, cols], kvp_ref, kvc_ref, kvn_ref,
                                               t * tq - RADIUS, tq + 2 * RADIUS), axis=0)

        def natural(k):
            return jnp.concatenate([stat_scr[k, pl.ds(t * srow + ii, STREAMS, stride=STAT_PITCH), :]
                                    for ii in range(srow)], axis=0)

        def put_acc(sl, val):
            acc_scr[:, sl] = val

        def put_o(sl, val):
            o_ref[rows, sl] = val.astype(BF16)

        stats = [_attn_pairs(lambda sl: q_ref[rows, sl], *_kv_getters(stack), bias, put_acc),
                 (natural(0), natural(1)), (natural(2), natural(3))]
        _merge_groups(stats, [lambda sl: acc_scr[:, sl], lambda sl: a2_scr[rows, sl],
                              lambda sl: a1_scr[rows, sl]], put_o)


def _attention(qkv0, qkv12):
    bsz, s, _ = qkv0.shape
    m16 = s // STREAMS
    tq, half = ATTN_TQ, RADIUS
    cur_rows = min(ATTN_SUB * tq, m16)
    params = lambda n: pltpu.CompilerParams(dimension_semantics=("parallel",) * n,
                                            vmem_limit_bytes=VMEM_LIMIT)
    same = lambda i: i

    def halo_idx(cur_n, halo_n, total):
        per = cur_n // halo_n
        return (lambda i: jnp.maximum(i * per - 1, 0)), (lambda i: jnp.minimum((i + 1) * per, total // halo_n - 1))

    def spec16(nrows, row_idx, col, width=ATTN_WIDTH):
        return pl.BlockSpec((None, None, nrows, width), lambda b, r, i: (b, r, row_idx(i), col))
    prv, nxt = halo_idx(cur_rows, half, m16)
    stat16 = jax.ShapeDtypeStruct((bsz, STREAMS, m16, LANES), F32)
    o2, mx2, den2 = pl.pallas_call(
        functools.partial(_attn16_kernel, m=m16),
        out_shape=(jax.ShapeDtypeStruct((bsz, STREAMS, m16, ATTN_WIDTH), BF16), stat16, stat16),
        grid=(bsz, STREAMS, m16 // cur_rows),
        in_specs=[spec16(cur_rows, same, 5), spec16(half, prv, 1, 2 * ATTN_WIDTH),
                  spec16(cur_rows, same, 1, 2 * ATTN_WIDTH), spec16(half, nxt, 1, 2 * ATTN_WIDTH)],
        out_specs=(spec16(cur_rows, same, 0), spec16(cur_rows, same, 0, LANES), spec16(cur_rows, same, 0, LANES)),
        compiler_params=params(3), name="attn_dil16",
    )(*([qkv12] * 4))

    ns = STREAMS // 4
    rows4, halo4 = cur_rows // ns, half // ns

    def view4(a):
        return a.reshape(bsz, ns, 4, m16, a.shape[-1])

    def spec4(nrows, row_idx, col, width=ATTN_WIDTH):
        return pl.BlockSpec((None, ns, None, nrows, width), lambda b, r, i: (b, 0, r, row_idx(i), col))
    prv4, nxt4 = halo_idx(rows4, halo4, m16)
    q4 = view4(qkv12)
    stat4 = jax.ShapeDtypeStruct((bsz, ns, 4, m16, LANES), F32)
    o1, mx1, den1 = pl.pallas_call(
        functools.partial(_attn4_kernel, m=m16),
        out_shape=(jax.ShapeDtypeStruct((bsz, ns, 4, m16, ATTN_WIDTH), BF16), stat4, stat4),
        grid=(bsz, 4, m16 // rows4),
        in_specs=[spec4(rows4, same, 4), spec4(halo4, prv4, 0, 2 * ATTN_WIDTH),
                  spec4(rows4, same, 0, 2 * ATTN_WIDTH), spec4(halo4, nxt4, 0, 2 * ATTN_WIDTH)],
        out_specs=(spec4(rows4, same, 0), spec4(rows4, same, 0, LANES), spec4(rows4, same, 0, LANES)),
        compiler_params=params(3), name="attn_dil4",
    )(q4, q4, q4, q4)
    streams = lambda a: a.reshape(bsz, STREAMS, m16, a.shape[-1])

    cur1 = ATTN_SUB * tq

    def spec1(nrows, row_idx, col, width=ATTN_WIDTH):
        return pl.BlockSpec((None, nrows, width), lambda b, i: (b, row_idx(i), col))
    prv1, nxt1 = halo_idx(cur1, half, s)
    sm = lambda width: pl.BlockSpec((None, STREAMS, cur1 // STREAMS, width), lambda b, i: (b, 0, i, 0))
    run = [sm(ATTN_WIDTH), sm(LANES), sm(LANES)]
    unperm = jnp.asarray(_stream_perm(cur1).T, BF16)
    return pl.pallas_call(
        functools.partial(_attn1_kernel, m=s),
        out_shape=jax.ShapeDtypeStruct((bsz, s, ATTN_WIDTH), BF16),
        grid=(bsz, s // cur1),
        in_specs=[spec1(cur1, same, 2), spec1(half, prv1, 0, 2 * ATTN_WIDTH),
                  spec1(cur1, same, 0, 2 * ATTN_WIDTH), spec1(half, nxt1, 0, 2 * ATTN_WIDTH)] + run + run
                 + [pl.BlockSpec((cur1, cur1), lambda b, i: (0, 0))],
        out_specs=spec1(cur1, same, 0),
        scratch_shapes=[pltpu.VMEM((tq, ATTN_WIDTH), F32), pltpu.VMEM((cur1, ATTN_WIDTH), F32),
                        pltpu.VMEM((cur1, ATTN_WIDTH), F32), pltpu.VMEM((4, STREAMS * STAT_PITCH, LANES), F32)],
        compiler_params=params(2), name="attn_dil1",
    )(*([qkv0] * 4), o2, mx2, den2, streams(o1), streams(mx1), streams(den1), unperm)


def _attn_out_kernel(o_ref, z_ref, x_ref, gate_ref, w_ref, out_ref):
    y = (o_ref[...].astype(F32) * _silu(z_ref[...].astype(F32))).astype(BF16)
    out = jnp.dot(y, w_ref[...], preferred_element_type=F32)
    out_ref[...] = x_ref[...] + gate_ref[...] * out


def _attn_out(o, z, x, mod, w, tm=512):
    bsz, s, _ = x.shape
    row = pl.BlockSpec((None, tm, D_MODEL), lambda b, i: (b, i, 0))
    return pl.pallas_call(
        _attn_out_kernel,
        out_shape=jax.ShapeDtypeStruct(x.shape, F32),
        grid=(bsz, s // tm),
        in_specs=[row, row, row,
                  pl.BlockSpec((None, 1, D_MODEL), lambda b, i: (b, 0, 2)),
                  pl.BlockSpec((ATTN_WIDTH, D_MODEL), lambda b, i: (0, 0))],
        out_specs=row,
        compiler_params=pltpu.CompilerParams(
            dimension_semantics=("parallel", "parallel"),
            vmem_limit_bytes=VMEM_LIMIT),
        name="attn_out",
    )(o, z, x, mod, w)


def _cmul(ar, ai, br, bi):
    return ar * br - ai * bi, ar * bi + ai * br


def _s5_prep_kernel(lam_re_a, lam_im_a, lam_re_b, lam_im_b, logdt, bt_re, bt_im,
                    c_re, c_im, d_diag, toep_ref, mb_ref, mc_ref, al_ref, v_scr):
    lane = lax.broadcasted_iota(jnp.int32, (STATE, LANES), 1)
    a_idx = lane // GROUP_CH
    n_tiles = CHUNK // T_PER_VREG

    mb_rows = []
    v_rhs = []
    v_lhs = []
    mc_cols = []
    zbar_f = None
    for dr in range(2):
        dt = jnp.exp(logdt[dr])
        lr = jnp.minimum(lam_re_a[dr], LAMBDA_RE_MAX)
        li = lam_im_a[dr]
        ldr, ldi = lr * dt, li * dt

        mag = jnp.exp(ldr)
        l1r, l1i = mag * jnp.cos(ldi), mag * jnp.sin(ldi)
        small = [(jnp.ones_like(l1r), jnp.zeros_like(l1r)), (l1r, l1i)]
        for _ in range(T_PER_VREG - 1):
            small.append(_cmul(*small[-1], l1r, l1i))

        def lane_powers(exponent_of_step):
            pr, pi = small[exponent_of_step[0]]
            pr, pi = jnp.broadcast_to(pr, (STATE, LANES)), jnp.broadcast_to(pi, (STATE, LANES))
            for a in range(1, T_PER_VREG):
                er, ei = small[exponent_of_step[a]]
                pr, pi = jnp.where(a_idx == a, er, pr), jnp.where(a_idx == a, ei, pi)
            return pr, pi

        den = lr * lr + li * li
        nr, ni = l1r - 1.0, l1i
        cr, ci = (nr * lr + ni * li) / den, (ni * lr - nr * li) / den
        bbr, bbi = _cmul(cr, ci, bt_re[dr], bt_im[dr])
        l8r, l8i = small[T_PER_VREG]
        tiles = [None] * n_tiles
        if dr == 0:
            pr, pi = lane_powers([T_PER_VREG - 1 - a for a in range(T_PER_VREG)])
            cur = _cmul(pr, pi, bbr, bbi)
            for jt in range(n_tiles - 1, -1, -1):
                tiles[jt] = cur
                if jt:
                    cur = _cmul(cur[0], cur[1], l8r, l8i)
            v_rhs.append(_cmul(tiles[n_tiles - 1][0], tiles[n_tiles - 1][1], l1r, l1i))
            zbar_f = (jnp.where(lane < GROUP_CH, bbr, 0.0), jnp.where(lane < GROUP_CH, bbi, 0.0))
        else:
            pr, pi = lane_powers(list(range(T_PER_VREG)))
            cur = _cmul(pr, pi, bbr, bbi)
            for jt in range(n_tiles):
                tiles[jt] = cur
                if jt < n_tiles - 1:
                    cur = _cmul(cur[0], cur[1], l8r, l8i)
            v_rhs.append(tiles[0])
        mb_rows.append(jnp.concatenate([t[0] for t in tiles], axis=1))
        mb_rows.append(jnp.concatenate([t[1] for t in tiles], axis=1))
        a_r, a_i = l8r, l8i
        for _ in range(int(math.log2(CHUNK // T_PER_VREG))):
            a_r, a_i = _cmul(a_r, a_i, a_r, a_i)
        al_ref[2 * dr] = jnp.broadcast_to(a_r, (STATE, LANES))
        al_ref[2 * dr + 1] = jnp.broadcast_to(a_i, (STATE, LANES))

        lrb = jnp.minimum(lam_re_b[dr], LAMBDA_RE_MAX)
        lib = lam_im_b[dr]
        mag = jnp.exp(lrb * dt)
        pw_r, pw_i = mag * jnp.cos(lib * dt), mag * jnp.sin(lib * dt)
        pows = [(pw_r, pw_i)]
        for _ in range(int(math.log2(CHUNK)) - 1):
            pows.append(_cmul(*pows[-1], *pows[-1]))
        cre, cim = c_re[dr], c_im[dr]
        xr, xi = _cmul(cre, cim, *pows[0])
        for k in range(int(math.log2(CHUNK))):
            yr, yi = _cmul(xr, xi, *pows[k])
            if dr == 0:
                xr, xi = jnp.concatenate([xr, yr], 0), jnp.concatenate([xi, yi], 0)
            else:
                xr, xi = jnp.concatenate([yr, xr], 0), jnp.concatenate([yi, xi], 0)
        mc_cols += [xr, -xi]
        xr, xi = cre, cim
        for k in range(int(math.log2(n_tiles))):
            yr, yi = _cmul(xr, xi, *pows[k + int(math.log2(T_PER_VREG))])
            if dr == 0:
                xr, xi = jnp.concatenate([yr, xr], 0), jnp.concatenate([yi, xi], 0)
            else:
                xr, xi = jnp.concatenate([xr, yr], 0), jnp.concatenate([xi, yi], 0)
        v_lhs.append((xr, xi))

    mb_ref[...] = jnp.concatenate(mb_rows, axis=0).astype(BF16)
    mc_ref[...] = jnp.concatenate(mc_cols, axis=1).astype(BF16)

    def cdot(lhs, rhs):
        return (jnp.dot(lhs[0], rhs[0], precision=HI, preferred_element_type=F32)
                - jnp.dot(lhs[1], rhs[1], precision=HI, preferred_element_type=F32))

    out_f = cdot(v_lhs[0], v_rhs[0])
    zero_rows = jnp.zeros((CHUNK_ROWS // T_PER_VREG - GROUP_CH, STATE), F32)
    lag0_lhs = (jnp.concatenate([c_re[0], zero_rows], 0), jnp.concatenate([c_im[0], zero_rows], 0))
    out_b = cdot(v_lhs[1], v_rhs[1]) + cdot(lag0_lhs, zbar_f)
    out_b = out_b + jnp.concatenate([d_diag[...], jnp.zeros((CHUNK_ROWS // T_PER_VREG - GROUP_CH, LANES), F32)], 0)
    for jt in range(n_tiles):
        rows = slice(jt * GROUP_CH, (jt + 1) * GROUP_CH)
        v_scr[:, jt * LANES:(jt + 1) * LANES] = out_f[rows]
        v_scr[:, (n_tiles + jt) * LANES:(n_tiles + jt + 1) * LANES] = out_b[rows]
    v_scr[:, 2 * CHUNK_ROWS:] = jnp.zeros((GROUP_CH, LANES), F32)

    for k in range(T_PER_VREG):
        vk = v_scr[:, k * GROUP_CH:k * GROUP_CH + 2 * CHUNK_ROWS]
        for t_out in range(CHUNK):
            off = (CHUNK - t_out) * GROUP_CH
            if off % LANES == k * GROUP_CH:
                al_off = off - k * GROUP_CH
                toep_ref[t_out * GROUP_CH:(t_out + 1) * GROUP_CH, :] = (
                    vk[:, al_off:al_off + CHUNK_ROWS].astype(BF16))


def _s5_prep(lam_re, lam_im, log_dt, b_re, b_im, c_re, c_im, d_skip):
    g = SSM_GROUPS
    d_diag = d_skip.reshape(g, GROUP_CH, 1) * jnp.eye(GROUP_CH, LANES, dtype=F32)
    lam_a = lambda a: a.reshape(2, g, STATE, 1)
    lam_b = lambda a: a.reshape(2, g, 1, STATE)
    tile_b = lambda a: jnp.tile(a, (1, 1, 1, T_PER_VREG))
    dir_spec = lambda r, c: pl.BlockSpec((2, None, r, c), lambda i: (0, i, 0, 0))
    return pl.pallas_call(
        _s5_prep_kernel,
        out_shape=(jax.ShapeDtypeStruct((g, CHUNK_ROWS, CHUNK_ROWS), BF16),
                   jax.ShapeDtypeStruct((g, 4 * STATE, CHUNK_ROWS), BF16),
                   jax.ShapeDtypeStruct((g, CHUNK_ROWS, 4 * STATE), BF16),
                   jax.ShapeDtypeStruct((g, 4, STATE, LANES), F32)),
        grid=(g,),
        in_specs=[dir_spec(STATE, 1), dir_spec(STATE, 1), dir_spec(1, STATE), dir_spec(1, STATE),
                  dir_spec(1, 1), dir_spec(STATE, LANES), dir_spec(STATE, LANES),
                  dir_spec(GROUP_CH, STATE), dir_spec(GROUP_CH, STATE),
                  pl.BlockSpec((None, GROUP_CH, LANES), lambda i: (i, 0, 0))],
        out_specs=(pl.BlockSpec((None, CHUNK_ROWS, CHUNK_ROWS), lambda i: (i, 0, 0)),
                   pl.BlockSpec((None, 4 * STATE, CHUNK_ROWS), lambda i: (i, 0, 0)),
                   pl.BlockSpec((None, CHUNK_ROWS, 4 * STATE), lambda i: (i, 0, 0)),
                   pl.BlockSpec((None, 4, STATE, LANES), lambda i: (i, 0, 0, 0))),
        scratch_shapes=[pltpu.VMEM((GROUP_CH, V_LANES), F32)],
        compiler_params=pltpu.CompilerParams(dimension_semantics=("parallel",)),
        name="s5_prep",
    )(lam_a(lam_re), lam_a(lam_im), lam_b(lam_re), lam_b(lam_im), log_dt.reshape(2, g, 1, 1),
      tile_b(b_re), tile_b(b_im), c_re, c_im, d_diag)


def _slab_perm():
    dst = np.arange(SLAB)
    perm = np.zeros((SLAB, SLAB), np.float32)
    perm[dst, (dst % SLAB_CHUNKS) * SUBLANES + dst // SLAB_CHUNKS] = 1.0
    return perm


def _s5_in_kernel(x_ref, mod_ref, g_ref, w_ref, p_ref, u_ref, z_ref, h_scr, *, chunks_per_seq):
    n_chunks = x_ref.shape[0]
    for b in range(n_chunks // chunks_per_seq):
        rows = slice(b * chunks_per_seq, (b + 1) * chunks_per_seq)
        h = _rms_modulate(x_ref[rows], g_ref[...], mod_ref[b:b + 1, D_MODEL:2 * D_MODEL],
                          mod_ref[b:b + 1, :D_MODEL])
        h2 = h.reshape(chunks_per_seq * SUBLANES, D_MODEL).astype(BF16)
        for s in range(chunks_per_seq // SLAB_CHUNKS):
            hp = jnp.dot(p_ref[...], h2[s * SLAB:(s + 1) * SLAB], preferred_element_type=F32)
            c0 = b * chunks_per_seq + s * SLAB_CHUNKS
            for k in range(SUBLANES):
                h_scr[k, c0:c0 + SLAB_CHUNKS, :] = hp[k * SLAB_CHUNKS:(k + 1) * SLAB_CHUNKS].astype(BF16)
    for k in range(SUBLANES):
        ht = h_scr[k].astype(F32).T.astype(BF16)
        uz = jnp.dot(w_ref[...], ht, preferred_element_type=F32)
        u_ref[k] = uz[:SSM_WIDTH].astype(BF16)
        z_ref[k] = uz[SSM_WIDTH:].astype(BF16)


def _s5_in(x3, mod, g, w_t, chunks_per_seq):
    n_chunks = x3.shape[0]
    once = lambda shape: pl.BlockSpec(shape, lambda t: (0,) * len(shape), pipeline_mode=pl.Buffered(1))
    out = pl.BlockSpec((SUBLANES, SSM_WIDTH, n_chunks), lambda t: (t, 0, 0))
    return pl.pallas_call(
        functools.partial(_s5_in_kernel, chunks_per_seq=chunks_per_seq),
        out_shape=(jax.ShapeDtypeStruct((CHUNK, SSM_WIDTH, n_chunks), BF16),) * 2,
        grid=(CHUNK // SUBLANES,),
        in_specs=[pl.BlockSpec((n_chunks, SUBLANES, D_MODEL), lambda t: (0, t, 0)),
                  once(mod.shape), once((1, D_MODEL)), once((2 * SSM_WIDTH, D_MODEL)),
                  once((SLAB, SLAB))],
        out_specs=(out, out),
        scratch_shapes=[pltpu.VMEM((SUBLANES, n_chunks, D_MODEL), BF16)],
        compiler_params=pltpu.CompilerParams(
            dimension_semantics=("parallel",), vmem_limit_bytes=VMEM_LIMIT_BIG),
        name="s5_in",
    )(x3, mod, g, w_t, jnp.asarray(_slab_perm(), BF16))


def _gelu_tanh(v):
    return 0.5 * v * (1.0 + jnp.tanh(math.sqrt(2.0 / math.pi) * (v + 0.044715 * (v * v * v))))


def _s5_core_kernel(u_ref, toep_ref, mb_ref, mc_ref, al_ref, o_ref, *, chunks_per_seq):
    n_chunks = u_ref.shape[-1]
    ub = u_ref[...].reshape(CHUNK_ROWS, n_chunks)
    inc = jnp.dot(mb_ref[...], ub, preferred_element_type=F32)
    pos = lax.broadcasted_iota(jnp.int32, (STATE, n_chunks), 1) % chunks_per_seq
    reps = n_chunks // LANES

    def lane_tile(a):
        return jnp.concatenate([a] * reps, axis=1) if reps > 1 else a

    states = []
    for dr in range(2):
        xr = inc[(2 * dr) * STATE:(2 * dr + 1) * STATE]
        xi = inc[(2 * dr + 1) * STATE:(2 * dr + 2) * STATE]
        ar, ai = al_ref[2 * dr], al_ref[2 * dr + 1]

        def shifted(v, step):
            if dr == 0:
                return jnp.where(pos >= step, pltpu.roll(v, step, 1), 0.0)
            return jnp.where(pos < chunks_per_seq - step, pltpu.roll(v, n_chunks - step, 1), 0.0)

        step = 1
        while step < chunks_per_seq:
            sr, si = shifted(xr, step), shifted(xi, step)
            tr, ti = lane_tile(ar), lane_tile(ai)
            xr, xi = xr + tr * sr - ti * si, xi + tr * si + ti * sr
            ar, ai = _cmul(ar, ai, ar, ai)
            step *= 2
        states += [shifted(xr, 1), shifted(xi, 1)]
    h_in = jnp.concatenate(states, axis=0).astype(BF16)
    y = (jnp.dot(toep_ref[...], ub, preferred_element_type=F32)
         + jnp.dot(mc_ref[...], h_in, preferred_element_type=F32))
    o_ref[...] = _gelu_tanh(y).astype(BF16).reshape(CHUNK, GROUP_CH, n_chunks)


def _s5_core(u_cl, toep, mb, mc, al, chunks_per_seq):
    n_chunks = u_cl.shape[-1]
    u4 = u_cl.reshape(CHUNK, SSM_GROUPS, GROUP_CH, n_chunks)
    grp = pl.BlockSpec((CHUNK, None, GROUP_CH, n_chunks), lambda g: (0, g, 0, 0))
    out = pl.pallas_call(
        functools.partial(_s5_core_kernel, chunks_per_seq=chunks_per_seq),
        out_shape=jax.ShapeDtypeStruct(u4.shape, BF16),
        grid=(SSM_GROUPS,),
        in_specs=[grp,
                  pl.BlockSpec((None, CHUNK_ROWS, CHUNK_ROWS), lambda g: (g, 0, 0)),
                  pl.BlockSpec((None, 4 * STATE, CHUNK_ROWS), lambda g: (g, 0, 0)),
                  pl.BlockSpec((None, CHUNK_ROWS, 4 * STATE), lambda g: (g, 0, 0)),
                  pl.BlockSpec((None, 4, STATE, LANES), lambda g: (g, 0, 0, 0))],
        out_specs=grp,
        compiler_params=pltpu.CompilerParams(
            dimension_semantics=("parallel",), vmem_limit_bytes=VMEM_LIMIT),
        name="s5_core",
    )(u4, toep, mb, mc, al)
    return out.reshape(CHUNK, SSM_WIDTH, n_chunks)


def _s5_out_kernel(g_ref, z_ref, wglu_ref, wout_ref, pt_ref, x_ref, mod_ref, fg_ref, o_ref, y_scr,
                   *, final, chunks_per_seq):
    half = pl.program_id(1)
    n_half = x_ref.shape[0]

    @pl.when(half == 0)
    def _():
        for k in range(SUBLANES):
            gb = g_ref[k]
            glu = jnp.dot(wglu_ref[...], gb, preferred_element_type=F32)
            y = gb.astype(F32) * _sigmoid(glu) * _silu(z_ref[k].astype(F32))
            y_scr[k] = y.T.astype(BF16)

    for hv in range(y_scr.shape[1] // n_half):
        @pl.when(half == hv)
        def _():
            for s in range(n_half // SLAB_CHUNKS):
                c0 = hv * n_half + s * SLAB_CHUNKS
                src = jnp.concatenate([y_scr[k, c0:c0 + SLAB_CHUNKS, :] for k in range(SUBLANES)], axis=0)
                yp = jnp.dot(pt_ref[...], src, preferred_element_type=F32).astype(BF16)
                out = jnp.dot(yp, wout_ref[...], preferred_element_type=F32)
                b = c0 // chunks_per_seq
                upd = mod_ref[b:b + 1, 2 * D_MODEL:] * out
                rows = slice(s * SLAB_CHUNKS, (s + 1) * SLAB_CHUNKS)
                xn = x_ref[rows] + upd.reshape(SLAB_CHUNKS, SUBLANES, D_MODEL)
                if final:
                    xn = xn * lax.rsqrt(jnp.mean(xn * xn, axis=-1, keepdims=True) + NORM_EPS) * fg_ref[...]
                o_ref[rows] = xn


def _s5_out(g_cl, z_cl, wglu_t, w_out, x3, mod, final_g, final, chunks_per_seq, n_split=2):
    n_chunks = x3.shape[0]
    n_half = n_chunks // n_split
    once = lambda shape: pl.BlockSpec(shape, lambda t, h: (0,) * len(shape), pipeline_mode=pl.Buffered(1))
    act = pl.BlockSpec((SUBLANES, SSM_WIDTH, n_chunks), lambda t, h: (t, 0, 0))
    row = pl.BlockSpec((n_half, SUBLANES, D_MODEL), lambda t, h: (h, t, 0))
    return pl.pallas_call(
        functools.partial(_s5_out_kernel, final=final, chunks_per_seq=chunks_per_seq),
        out_shape=jax.ShapeDtypeStruct(x3.shape, F32),
        grid=(CHUNK // SUBLANES, n_split),
        in_specs=[act, act, once((SSM_WIDTH, SSM_WIDTH)), once((SSM_WIDTH, D_MODEL)), once((SLAB, SLAB)),
                  row, once(mod.shape), once((1, D_MODEL))],
        out_specs=row,
        scratch_shapes=[pltpu.VMEM((SUBLANES, n_chunks, SSM_WIDTH), BF16)],
        compiler_params=pltpu.CompilerParams(
            dimension_semantics=("parallel", "arbitrary"), vmem_limit_bytes=VMEM_LIMIT_BIG),
        name="s5_out",
    )(g_cl, z_cl, wglu_t, w_out, jnp.asarray(_slab_perm().T, BF16), x3, mod, final_g)


def _rope_tables(s):
    inv_freq = ROPE_THETA ** (-jnp.arange(0, HEAD_DIM, 2, dtype=F32) / HEAD_DIM)
    ang = jnp.arange(s, dtype=F32)[:, None] * inv_freq[None, :]
    reps = LANES // (HEAD_DIM // 2)
    sign = jnp.where(jnp.arange(LANES) < LANES // 2, -1.0, 1.0).astype(F32)
    cos, sin = jnp.tile(jnp.cos(ang), (1, reps)), jnp.tile(jnp.sin(ang), (1, reps)) * sign
    stream_major = lambda a: a.reshape(s // STREAMS, STREAMS, LANES).transpose(1, 0, 2)
    return cos, sin, stream_major(cos), stream_major(sin)


def _prep_attn_w_in(w):
    d_in = w.shape[0]
    w = w.astype(BF16).reshape(d_in, -1, ATTN_HEADS // 2, 2, 2, HEAD_DIM // 2)
    w = w.transpose(0, 1, 2, 4, 3, 5)
    return w[:, np.array(ATTN_IN_BLOCK_ORDER)].reshape(d_in, -1)


def _prep_attn_w_out(w):
    w = w.astype(BF16).reshape(ATTN_HEADS // 2, 2, 2, HEAD_DIM // 2, w.shape[1])
    return w.transpose(0, 2, 1, 3, 4).reshape(ATTN_WIDTH, -1)


def _trunk(x, ada, norm_g, attn_w, s5_w, s5_ops, final_norm_g):
    bsz, s, _ = x.shape
    chunks_per_seq = s // CHUNK
    n_chunks = bsz * chunks_per_seq
    tables = _rope_tables(s)
    fg = final_norm_g.reshape(1, D_MODEL)
    for i in range(DEPTH):
        mod = ada[i].reshape(bsz, 1, 3 * D_MODEL)
        g = norm_g[i].reshape(1, D_MODEL)
        j = i // N_MIXERS
        if i % N_MIXERS == 0:
            w_in, w_out = attn_w[j]
            qkv0, qkv12, z = _attn_in(x, mod, g, w_in, tables)
            x = _attn_out(_attention(qkv0, qkv12), z, x, mod, w_out)
        else:
            w_in_t, wglu_t, w_out = s5_w[j]
            toep, mb, mc, al = s5_ops[j]
            x3 = x.reshape(n_chunks, CHUNK, D_MODEL)
            u_cl, z_cl = _s5_in(x3, ada[i], g, w_in_t, chunks_per_seq)
            g_cl = _s5_core(u_cl, toep, mb, mc, al, chunks_per_seq)
            x3 = _s5_out(g_cl, z_cl, wglu_t, w_out, x3, ada[i], fg, i == DEPTH - 1, chunks_per_seq)
            x = x3.reshape(bsz, s, D_MODEL)
    return x


def kernel(x_prompt, x_sample, c_prompt, c_sample, norm_g, ada_w, ada_b, attn_w_in, attn_w_out,
           ssm_w_in, ssm_lam_re, ssm_lam_im, ssm_log_dt, ssm_b_re, ssm_b_im, ssm_c_re, ssm_c_im,
           ssm_d, ssm_w_glu, ssm_w_out, final_norm_g):
    assert (DEPTH - 1) % N_MIXERS == 1, "the final norm is fused into the last S5 layer"
    assert all(w // (2 * d) == RADIUS for w, d in DILATED_PAIRS)
    assert [d for _, d in DILATED_PAIRS] == [1, 4, STREAMS]
    n_prompt = c_prompt.shape[0]
    ada = _ada(jnp.concatenate([c_prompt, c_sample], axis=0), ada_w, ada_b)
    attn_w = [(_prep_attn_w_in(attn_w_in[j]), _prep_attn_w_out(attn_w_out[j]))
              for j in range(attn_w_in.shape[0])]
    s5_w, s5_ops = [], []
    for j in range(ssm_w_in.shape[0]):
        s5_w.append((ssm_w_in[j].T.astype(BF16), ssm_w_glu[j].T.astype(BF16), ssm_w_out[j].astype(BF16)))
        s5_ops.append(_s5_prep(ssm_lam_re[j], ssm_lam_im[j], ssm_log_dt[j], ssm_b_re[j],
                               ssm_b_im[j], ssm_c_re[j], ssm_c_im[j], ssm_d[j]))
    y_prompt = _trunk(x_prompt, ada[:, :n_prompt], norm_g, attn_w, s5_w, s5_ops, final_norm_g)
    y_sample = _trunk(x_sample, ada[:, n_prompt:], norm_g, attn_w, s5_w, s5_ops, final_norm_g)
    return (y_prompt, y_sample)
```

```python
import functools
import math

import numpy as np
import jax
import jax.numpy as jnp
from jax import lax
from jax.experimental import pallas as pl
from jax.experimental.pallas import tpu as pltpu

D_MODEL = 1024
DEPTH = 4
N_MIXERS = 2
ATTN_HEADS = 16
HEAD_DIM = 64
ATTN_WIDTH = ATTN_HEADS * HEAD_DIM
DILATED_PAIRS = ((128, 1), (512, 4), (2048, 16))
N_DIL = len(DILATED_PAIRS)
ROPE_THETA = 10000.0
SSM_WIDTH = D_MODEL
GROUP_CH = 16
SSM_GROUPS = SSM_WIDTH // GROUP_CH
STATE = 64
LAMBDA_RE_MAX = -1e-4
NORM_EPS = 1e-6
NEG_INF = -1e30

LANES = 128
SUBLANES = 8
CHUNK = 64
CHUNK_ROWS = CHUNK * GROUP_CH
T_PER_VREG = LANES // GROUP_CH
V_LANES = 2 * CHUNK_ROWS + LANES
STREAMS = 16
RADIUS = 64
ATTN_TQ = 128
ATTN_SUB = 4
STAT_PITCH = 40
ATTN_IN_BLOCK_ORDER = (1, 2, 0, 9, 4, 5, 7, 8, 3, 6)
PERM_ROWS = 512
LOG2_E = math.log2(math.e)
S5_GROUPS_PER_STEP = 2
SLAB = 256
SLAB_CHUNKS = SLAB // SUBLANES
VMEM_LIMIT = 48 * 1024 * 1024
VMEM_LIMIT_BIG = 56 * 1024 * 1024

F32 = jnp.float32
BF16 = jnp.bfloat16
HI = lax.Precision.HIGHEST


def _sigmoid(v):
    return 0.5 * jnp.tanh(0.5 * v) + 0.5


def _silu(v):
    return v * _sigmoid(v)


def _rms_modulate(x, g, scale, shift):
    rs = lax.rsqrt(jnp.mean(x * x, axis=-1, keepdims=True) + NORM_EPS)
    return (x * rs * g) * (1.0 + scale) + shift


def _ada_kernel(c_ref, w_ref, b_ref, o_ref):
    o_ref[...] = jnp.dot(_silu(c_ref[...]), w_ref[...], precision=HI,
                         preferred_element_type=F32) + b_ref[...]


def _ada(c, ada_w, ada_b):
    nb = c.shape[0]
    return pl.pallas_call(
        _ada_kernel,
        out_shape=jax.ShapeDtypeStruct((DEPTH, nb, 3 * D_MODEL), F32),
        grid=(DEPTH, 3),
        in_specs=[
            pl.BlockSpec((nb, D_MODEL), lambda i, j: (0, 0)),
            pl.BlockSpec((None, D_MODEL, D_MODEL), lambda i, j: (i, 0, j)),
            pl.BlockSpec((None, 1, D_MODEL), lambda i, j: (i, 0, j)),
        ],
        out_specs=pl.BlockSpec((None, nb, D_MODEL), lambda i, j: (i, 0, j)),
        name="ada",
    )(c, ada_w, ada_b.reshape(DEPTH, 1, 3 * D_MODEL))


def _attn_in_kernel(x_ref, shift_ref, scale_ref, g_ref, w_ref, cos_ref, sin_ref, cosp_ref, sinp_ref,
                    perm_ref, nat_ref, sm_ref, h_nat, h_perm):
    j = pl.program_id(2)
    tm = x_ref.shape[0]
    n_sub = tm // PERM_ROWS
    sub_rows = PERM_ROWS // STREAMS

    @pl.when(j == 0)
    def _():
        h = _rms_modulate(x_ref[...], g_ref[...], scale_ref[...], shift_ref[...])
        h_nat[...] = h.astype(BF16)
        for u in range(n_sub):
            rs = slice(u * PERM_ROWS, (u + 1) * PERM_ROWS)
            h_perm[rs, :] = jnp.dot(perm_ref[...], h_nat[rs, :], preferred_element_type=F32).astype(BF16)

    def rope_blocks(acc, cos, sin):
        for half in range(2):
            is_q = jnp.logical_or(j == 4, jnp.logical_and(j == 1, half == 0))
            plain = (j != 4) if half == 1 else (j < 0)
            qscale = jnp.where(is_q, HEAD_DIM ** -0.5 * LOG2_E, 1.0).astype(F32)
            c = jnp.where(plain, 1.0, cos * qscale)
            sn = jnp.where(plain, 0.0, sin * qscale)
            for b in range(ATTN_WIDTH // LANES):
                sl = slice(half * ATTN_WIDTH + b * LANES, half * ATTN_WIDTH + (b + 1) * LANES)
                t = acc[:, sl]
                yield sl, (t * c + pltpu.roll(t, LANES // 2, 1) * sn).astype(BF16)

    @pl.when(j < 2)
    def _():
        acc = jnp.dot(h_nat[...], w_ref[...], preferred_element_type=F32)
        for sl, blk in rope_blocks(acc, cos_ref[...], sin_ref[...]):
            nat_ref[:, sl] = blk

    @pl.when(j >= 2)
    def _():
        acc = jnp.dot(h_perm[...], w_ref[...], preferred_element_type=F32)

        def table(ref):
            return jnp.concatenate([ref[:, u * sub_rows:(u + 1) * sub_rows, :].reshape(PERM_ROWS, LANES)
                                    for u in range(n_sub)], axis=0)

        for sl, blk in rope_blocks(acc, table(cosp_ref), table(sinp_ref)):
            for u in range(n_sub):
                sm_ref[:, u * sub_rows:(u + 1) * sub_rows, sl] = (
                    blk[u * PERM_ROWS:(u + 1) * PERM_ROWS].reshape(STREAMS, sub_rows, LANES))


def _stream_perm(n):
    dst = np.arange(n)
    rows = n // STREAMS
    perm = np.zeros((n, n), np.float32)
    perm[dst, STREAMS * (dst % rows) + dst // rows] = 1.0
    return perm


def _attn_in(x, mod, g, w, tables, tm=1024):
    bsz, s, _ = x.shape
    cos, sin, cosp, sinp = tables
    rows = tm // STREAMS
    wide = 2 * ATTN_WIDTH
    perm = jnp.asarray(_stream_perm(PERM_ROWS), BF16)
    return pl.pallas_call(
        _attn_in_kernel,
        out_shape=(jax.ShapeDtypeStruct((bsz, s, 2 * wide), BF16),
                   jax.ShapeDtypeStruct((bsz, STREAMS, s // STREAMS, 3 * wide), BF16)),
        grid=(bsz, s // tm, 5),
        in_specs=[
            pl.BlockSpec((None, tm, D_MODEL), lambda b, i, j: (b, i, 0)),
            pl.BlockSpec((None, 1, D_MODEL), lambda b, i, j: (b, 0, 0)),
            pl.BlockSpec((None, 1, D_MODEL), lambda b, i, j: (b, 0, 1)),
            pl.BlockSpec((1, D_MODEL), lambda b, i, j: (0, 0)),
            pl.BlockSpec((D_MODEL, wide), lambda b, i, j: (0, j)),
            pl.BlockSpec((tm, LANES), lambda b, i, j: (i, 0)),
            pl.BlockSpec((tm, LANES), lambda b, i, j: (i, 0)),
            pl.BlockSpec((STREAMS, rows, LANES), lambda b, i, j: (0, i, 0)),
            pl.BlockSpec((STREAMS, rows, LANES), lambda b, i, j: (0, i, 0)),
            pl.BlockSpec((PERM_ROWS, PERM_ROWS), lambda b, i, j: (0, 0)),
        ],
        out_specs=(
            pl.BlockSpec((None, tm, wide), lambda b, i, j: (b, i, jnp.clip(j, 0, 1))),
            pl.BlockSpec((None, STREAMS, rows, wide), lambda b, i, j: (b, 0, i, jnp.clip(j - 2, 0, 2))),
        ),
        scratch_shapes=[pltpu.VMEM((tm, D_MODEL), BF16), pltpu.VMEM((tm, D_MODEL), BF16)],
        compiler_params=pltpu.CompilerParams(
            dimension_semantics=("parallel", "parallel", "arbitrary"),
            vmem_limit_bytes=VMEM_LIMIT_BIG),
        name="attn_in",
    )(x, mod, mod, g, w, cos, sin, cosp, sinp, perm)


def _lane_of_odd_head(lane):
    return (lane // (HEAD_DIM // 2)) % 2 == 1


def _mask_bias(valid):
    return jnp.where(valid, 0.0, NEG_INF).astype(F32)


def _attn_pairs(get_q, get_k, get_v, bias, put_acc):
    tq = bias.shape[0]
    lane = lax.broadcasted_iota(jnp.int32, (tq, LANES), 1)
    q_is_odd = _lane_of_odd_head(lane)
    mx_tile = jnp.zeros((tq, LANES), F32)
    den_tile = jnp.ones((tq, LANES), F32)
    ones = jnp.ones((bias.shape[1], LANES), BF16)
    bias2 = jnp.concatenate([bias, bias], axis=0)
    for b in range(ATTN_WIDTH // LANES):
        sl = slice(b * LANES, (b + 1) * LANES)
        qb = get_q(sl).astype(F32)
        kb = get_k(sl)
        vb = jnp.concatenate([get_v(sl), ones], axis=1)
        qm = jnp.concatenate([jnp.where(q_is_odd, 0.0, qb), jnp.where(q_is_odd, qb, 0.0)], axis=0).astype(BF16)
        sc = lax.dot_general(qm, kb, (((1,), (1,)), ((), ())), preferred_element_type=F32) + bias2
        mx = jnp.max(sc, axis=1, keepdims=True)
        p = jnp.exp2((sc - mx).astype(BF16))
        acc = jnp.dot(p, vb, preferred_element_type=F32)
        for odd in (0, 1):
            rows = slice(odd * tq, (odd + 1) * tq)
            mx_tile = jnp.where(lane == 2 * b + odd, mx[rows], mx_tile)
            den_tile = jnp.where(lane == 2 * b + odd, acc[rows, LANES:], den_tile)
        put_acc(sl, jnp.where(q_is_odd, acc[tq:, :LANES], acc[:tq, :LANES]))
    return mx_tile, den_tile


def _pair_factor(tile, b):
    lane = lax.broadcasted_iota(jnp.int32, tile.shape, 1)
    head = 2 * b + _lane_of_odd_head(lane).astype(jnp.int32)
    return jnp.take_along_axis(tile, head, axis=1, mode="promise_in_bounds")


def _merge_groups(stats, get_accs, put_o):
    top = functools.reduce(jnp.maximum, [mx for mx, _ in stats])
    ws = [jnp.exp2(mx - top) for mx, _ in stats]
    inv = 1.0 / sum(w * den for w, (_, den) in zip(ws, stats))
    fs = [w * inv for w in ws]
    for b in range(ATTN_WIDTH // LANES):
        sl = slice(b * LANES, (b + 1) * LANES)
        put_o(sl, sum(_pair_factor(f, b) * get_acc(sl) for f, get_acc in zip(fs, get_accs)))


def _window_mask(tile, tq, m):
    nk = tq + 2 * RADIUS
    row = lax.broadcasted_iota(jnp.int32, (tq, nk), 0)
    col = lax.broadcasted_iota(jnp.int32, (tq, nk), 1)
    kabs = tile * tq - RADIUS + col
    return _mask_bias((jnp.abs(col - RADIUS - row) <= RADIUS) & (kabs >= 0) & (kabs < m))


def _stack_rows(read, prev_ref, cur_ref, next_ref, lo, cnt):
    halo, cur_n = prev_ref.shape[-2], cur_ref.shape[-2]
    a, b = lo, lo + cnt
    parts = []
    if a < 0:
        parts.append(read(prev_ref, halo + a, halo))
        a = 0
    parts.append(read(cur_ref, a, min(b, cur_n)))
    if b > cur_n:
        parts.append(read(next_ref, 0, b - cur_n))
    return parts


def _kv_getters(stack):
    get_k = lambda sl: stack(sl)
    get_v = lambda sl: stack(slice(ATTN_WIDTH + sl.start, ATTN_WIDTH + sl.stop))
    return get_k, get_v


def _attn16_kernel(q_ref, kvp_ref, kvc_ref, kvn_ref, o_ref, mx_ref, den_ref, *, m):
    tq = ATTN_TQ
    n_sub = q_ref.shape[0] // tq
    for t in range(n_sub):
        rows = slice(t * tq, (t + 1) * tq)
        bias = _window_mask(pl.program_id(2) * n_sub + t, tq, m)

        def stack(cols):
            return jnp.concatenate(_stack_rows(lambda ref, a, b: ref[a:b, cols], kvp_ref, kvc_ref, kvn_ref,
                                               t * tq - RADIUS, tq + 2 * RADIUS), axis=0)

        def put_acc(sl, val):
            o_ref[rows, sl] = val.astype(BF16)

        mx_ref[rows], den_ref[rows] = _attn_pairs(lambda sl: q_ref[rows, sl], *_kv_getters(stack), bias, put_acc)


def _attn4_kernel(q_ref, kvp_ref, kvc_ref, kvn_ref, o_ref, mx_ref, den_ref, *, m):
    ns, halo = q_ref.shape[0], kvp_ref.shape[1]
    rows = ATTN_TQ // ns
    span = rows + 2 * halo
    n_sub = q_ref.shape[1] // rows
    n = lax.broadcasted_iota(jnp.int32, (ns * rows, ns * span), 0)
    c = lax.broadcasted_iota(jnp.int32, (ns * rows, ns * span), 1)
    for t in range(n_sub):
        base = (pl.program_id(2) * n_sub + t) * rows
        rs = slice(t * rows, (t + 1) * rows)
        krow = base - halo + c % span
        rel = ns * (krow - (base + n % rows)) + (c // span - n // rows)
        bias = _mask_bias((jnp.abs(rel) <= RADIUS) & (krow >= 0) & (krow < m))

        def stack(cols):
            parts = []
            for s in range(ns):
                parts += _stack_rows(lambda ref, a, b: ref[s, a:b, cols], kvp_ref, kvc_ref, kvn_ref,
                                     t * rows - halo, span)
            return jnp.concatenate(parts, axis=0)

        flat = lambda ref: (lambda sl: ref[:, rs, sl].reshape(ns * rows, LANES))

        def put_acc(sl, val):
            o_ref[:, rs, sl] = val.astype(BF16).reshape(ns, rows, LANES)

        mx, den = _attn_pairs(flat(q_ref), *_kv_getters(stack), bias, put_acc)
        mx_ref[:, rs, :] = mx.reshape(ns, rows, LANES)
        den_ref[:, rs, :] = den.reshape(ns, rows, LANES)


def _attn1_kernel(q_ref, kvp_ref, kvc_ref, kvn_ref, o2_ref, mx2_ref, den2_ref, o1_ref, mx1_ref, den1_ref,
                  unperm_ref, o_ref, acc_scr, a2_scr, a1_scr, stat_scr, *, m):
    tq = ATTN_TQ
    n_sub = q_ref.shape[0] // tq
    srow, n_srow = tq // STREAMS, mx2_ref.shape[1]
    for src, dst in ((o2_ref, a2_scr), (o1_ref, a1_scr)):
        dst[...] = jnp.dot(unperm_ref[...], src[...].reshape(dst.shape), preferred_element_type=F32)
    for k, src in enumerate((mx2_ref, den2_ref, mx1_ref, den1_ref)):
        for r in range(STREAMS):
            stat_scr[k, r * STAT_PITCH:r * STAT_PITCH + n_srow, :] = src[r]
    for t in range(n_sub):
        rows = slice(t * tq, (t + 1) * tq)
        bias = _window_mask(pl.program_id(1) * n_sub + t, tq, m)

        def stack(cols):
            return jnp.concatenate(_stack_rows(lambda ref, a, b: ref[a:b, cols], kvp_ref, kvc_ref, kvn_ref,
                                               t * tq - RADIUS, tq + 2 * RADIUS), axis=0)

        def natural(k):
            return jnp.concatenate([stat_scr[k, pl.ds(t * srow + ii, STREAMS, stride=STAT_PITCH), :]
                                    for ii in range(srow)], axis=0)

        def put_acc(sl, val):
            acc_scr[:, sl] = val

        def put_o(sl, val):
            o_ref[rows, sl] = val.astype(BF16)

        stats = [_attn_pairs(lambda sl: q_ref[rows, sl], *_kv_getters(stack), bias, put_acc),
                 (natural(0), natural(1)), (natural(2), natural(3))]
        _merge_groups(stats, [lambda sl: acc_scr[:, sl], lambda sl: a2_scr[rows, sl],
                              lambda sl: a1_scr[rows, sl]], put_o)


def _attention(qkv0, qkv12):
    bsz, s, _ = qkv0.shape
    m16 = s // STREAMS
    tq, half = ATTN_TQ, RADIUS
    cur_rows = min(ATTN_SUB * tq, m16)
    params = lambda n: pltpu.CompilerParams(dimension_semantics=("parallel",) * n,
                                            vmem_limit_bytes=VMEM_LIMIT)
    same = lambda i: i

    def halo_idx(cur_n, halo_n, total):
        per = cur_n // halo_n
        return (lambda i: jnp.maximum(i * per - 1, 0)), (lambda i: jnp.minimum((i + 1) * per, total // halo_n - 1))

    def spec16(nrows, row_idx, col, width=ATTN_WIDTH):
        return pl.BlockSpec((None, None, nrows, width), lambda b, r, i: (b, r, row_idx(i), col))
    prv, nxt = halo_idx(cur_rows, half, m16)
    stat16 = jax.ShapeDtypeStruct((bsz, STREAMS, m16, LANES), F32)
    o2, mx2, den2 = pl.pallas_call(
        functools.partial(_attn16_kernel, m=m16),
        out_shape=(jax.ShapeDtypeStruct((bsz, STREAMS, m16, ATTN_WIDTH), BF16), stat16, stat16),
        grid=(bsz, STREAMS, m16 // cur_rows),
        in_specs=[spec16(cur_rows, same, 5), spec16(half, prv, 1, 2 * ATTN_WIDTH),
                  spec16(cur_rows, same, 1, 2 * ATTN_WIDTH), spec16(half, nxt, 1, 2 * ATTN_WIDTH)],
        out_specs=(spec16(cur_rows, same, 0), spec16(cur_rows, same, 0, LANES), spec16(cur_rows, same, 0, LANES)),
        compiler_params=params(3), name="attn_dil16",
    )(*([qkv12] * 4))

    ns = STREAMS // 4
    rows4, halo4 = cur_rows // ns, half // ns

    def view4(a):
        return a.reshape(bsz, ns, 4, m16, a.shape[-1])

    def spec4(nrows, row_idx, col, width=ATTN_WIDTH):
        return pl.BlockSpec((None, ns, None, nrows, width), lambda b, r, i: (b, 0, r, row_idx(i), col))
    prv4, nxt4 = halo_idx(rows4, halo4, m16)
    q4 = view4(qkv12)
    stat4 = jax.ShapeDtypeStruct((bsz, ns, 4, m16, LANES), F32)
    o1, mx1, den1 = pl.pallas_call(
        functools.partial(_attn4_kernel, m=m16),
        out_shape=(jax.ShapeDtypeStruct((bsz, ns, 4, m16, ATTN_WIDTH), BF16), stat4, stat4),
        grid=(bsz, 4, m16 // rows4),
        in_specs=[spec4(rows4, same, 4), spec4(halo4, prv4, 0, 2 * ATTN_WIDTH),
                  spec4(rows4, same, 0, 2 * ATTN_WIDTH), spec4(halo4, nxt4, 0, 2 * ATTN_WIDTH)],
        out_specs=(spec4(rows4, same, 0), spec4(rows4, same, 0, LANES), spec4(rows4, same, 0, LANES)),
        compiler_params=params(3), name="attn_dil4",
    )(q4, q4, q4, q4)
    streams = lambda a: a.reshape(bsz, STREAMS, m16, a.shape[-1])

    cur1 = ATTN_SUB * tq

    def spec1(nrows, row_idx, col, width=ATTN_WIDTH):
        return pl.BlockSpec((None, nrows, width), lambda b, i: (b, row_idx(i), col))
    prv1, nxt1 = halo_idx(cur1, half, s)
    sm = lambda width: pl.BlockSpec((None, STREAMS, cur1 // STREAMS, width), lambda b, i: (b, 0, i, 0))
    run = [sm(ATTN_WIDTH), sm(LANES), sm(LANES)]
    unperm = jnp.asarray(_stream_perm(cur1).T, BF16)
    return pl.pallas_call(
        functools.partial(_attn1_kernel, m=s),
        out_shape=jax.ShapeDtypeStruct((bsz, s, ATTN_WIDTH), BF16),
        grid=(bsz, s // cur1),
        in_specs=[spec1(cur1, same, 2), spec1(half, prv1, 0, 2 * ATTN_WIDTH),
                  spec1(cur1, same, 0, 2 * ATTN_WIDTH), spec1(half, nxt1, 0, 2 * ATTN_WIDTH)] + run + run
                 + [pl.BlockSpec((cur1, cur1), lambda b, i: (0, 0))],
        out_specs=spec1(cur1, same, 0),
        scratch_shapes=[pltpu.VMEM((tq, ATTN_WIDTH), F32), pltpu.VMEM((cur1, ATTN_WIDTH), F32),
                        pltpu.VMEM((cur1, ATTN_WIDTH), F32), pltpu.VMEM((4, STREAMS * STAT_PITCH, LANES), F32)],
        compiler_params=params(2), name="attn_dil1",
    )(*([qkv0] * 4), o2, mx2, den2, streams(o1), streams(mx1), streams(den1), unperm)


def _attn_out_kernel(o_ref, z_ref, x_ref, gate_ref, w_ref, out_ref):
    y = (o_ref[...].astype(F32) * _silu(z_ref[...].astype(F32))).astype(BF16)
    out = jnp.dot(y, w_ref[...], preferred_element_type=F32)
    out_ref[...] = x_ref[...] + gate_ref[...] * out


def _attn_out(o, nat, x, mod, w, tm=512):
    bsz, s, _ = x.shape
    row = pl.BlockSpec((None, tm, D_MODEL), lambda b, i: (b, i, 0))
    z_col = nat.shape[-1] // ATTN_WIDTH - 1
    return pl.pallas_call(
        _attn_out_kernel,
        out_shape=jax.ShapeDtypeStruct(x.shape, F32),
        grid=(bsz, s // tm),
        in_specs=[row, pl.BlockSpec((None, tm, ATTN_WIDTH), lambda b, i: (b, i, z_col)), row,
                  pl.BlockSpec((None, 1, D_MODEL), lambda b, i: (b, 0, 2)),
                  pl.BlockSpec((ATTN_WIDTH, D_MODEL), lambda b, i: (0, 0))],
        out_specs=row,
        compiler_params=pltpu.CompilerParams(
            dimension_semantics=("parallel", "parallel"),
            vmem_limit_bytes=VMEM_LIMIT),
        name="attn_out",
    )(o, nat, x, mod, w)


def _cmul(ar, ai, br, bi):
    return ar * br - ai * bi, ar * bi + ai * br


def _s5_prep_kernel(lam_re_a, lam_im_a, lam_re_b, lam_im_b, logdt, bt_re, bt_im,
                    c_re, c_im, d_diag, toep_ref, mb_ref, mc_ref, al_ref, v_scr):
    lane = lax.broadcasted_iota(jnp.int32, (STATE, LANES), 1)
    a_idx = lane // GROUP_CH
    n_tiles = CHUNK // T_PER_VREG

    mb_rows = []
    v_rhs = []
    v_lhs = []
    mc_cols = []
    zbar_f = None
    for dr in range(2):
        dt = jnp.exp(logdt[dr])
        lr = jnp.minimum(lam_re_a[dr], LAMBDA_RE_MAX)
        li = lam_im_a[dr]
        ldr, ldi = lr * dt, li * dt

        mag = jnp.exp(ldr)
        l1r, l1i = mag * jnp.cos(ldi), mag * jnp.sin(ldi)
        small = [(jnp.ones_like(l1r), jnp.zeros_like(l1r)), (l1r, l1i)]
        for _ in range(T_PER_VREG - 1):
            small.append(_cmul(*small[-1], l1r, l1i))

        def lane_powers(exponent_of_step):
            pr, pi = small[exponent_of_step[0]]
            pr, pi = jnp.broadcast_to(pr, (STATE, LANES)), jnp.broadcast_to(pi, (STATE, LANES))
            for a in range(1, T_PER_VREG):
                er, ei = small[exponent_of_step[a]]
                pr, pi = jnp.where(a_idx == a, er, pr), jnp.where(a_idx == a, ei, pi)
            return pr, pi

        den = lr * lr + li * li
        nr, ni = l1r - 1.0, l1i
        cr, ci = (nr * lr + ni * li) / den, (ni * lr - nr * li) / den
        bbr, bbi = _cmul(cr, ci, bt_re[dr], bt_im[dr])
        l8r, l8i = small[T_PER_VREG]
        tiles = [None] * n_tiles
        if dr == 0:
            pr, pi = lane_powers([T_PER_VREG - 1 - a for a in range(T_PER_VREG)])
            cur = _cmul(pr, pi, bbr, bbi)
            for jt in range(n_tiles - 1, -1, -1):
                tiles[jt] = cur
                if jt:
                    cur = _cmul(cur[0], cur[1], l8r, l8i)
            v_rhs.append(_cmul(tiles[n_tiles - 1][0], tiles[n_tiles - 1][1], l1r, l1i))
            zbar_f = (jnp.where(lane < GROUP_CH, bbr, 0.0), jnp.where(lane < GROUP_CH, bbi, 0.0))
        else:
            pr, pi = lane_powers(list(range(T_PER_VREG)))
            cur = _cmul(pr, pi, bbr, bbi)
            for jt in range(n_tiles):
                tiles[jt] = cur
                if jt < n_tiles - 1:
                    cur = _cmul(cur[0], cur[1], l8r, l8i)
            v_rhs.append(tiles[0])
        mb_rows.append(jnp.concatenate([t[0] for t in tiles], axis=1))
        mb_rows.append(jnp.concatenate([t[1] for t in tiles], axis=1))
        a_r, a_i = l8r, l8i
        for _ in range(int(math.log2(CHUNK // T_PER_VREG))):
            a_r, a_i = _cmul(a_r, a_i, a_r, a_i)
        al_ref[2 * dr] = jnp.broadcast_to(a_r, (STATE, LANES))
        al_ref[2 * dr + 1] = jnp.broadcast_to(a_i, (STATE, LANES))

        lrb = jnp.minimum(lam_re_b[dr], LAMBDA_RE_MAX)
        lib = lam_im_b[dr]
        mag = jnp.exp(lrb * dt)
        pw_r, pw_i = mag * jnp.cos(lib * dt), mag * jnp.sin(lib * dt)
        pows = [(pw_r, pw_i)]
        for _ in range(int(math.log2(CHUNK)) - 1):
            pows.append(_cmul(*pows[-1], *pows[-1]))
        cre, cim = c_re[dr], c_im[dr]
        xr, xi = _cmul(cre, cim, *pows[0])
        for k in range(int(math.log2(CHUNK))):
            yr, yi = _cmul(xr, xi, *pows[k])
            if dr == 0:
                xr, xi = jnp.concatenate([xr, yr], 0), jnp.concatenate([xi, yi], 0)
            else:
                xr, xi = jnp.concatenate([yr, xr], 0), jnp.concatenate([yi, xi], 0)
        mc_cols += [xr, -xi]
        xr, xi = cre, cim
        for k in range(int(math.log2(n_tiles))):
            yr, yi = _cmul(xr, xi, *pows[k + int(math.log2(T_PER_VREG))])
            if dr == 0:
                xr, xi = jnp.concatenate([yr, xr], 0), jnp.concatenate([yi, xi], 0)
            else:
                xr, xi = jnp.concatenate([xr, yr], 0), jnp.concatenate([xi, yi], 0)
        v_lhs.append((xr, xi))

    mb_ref[...] = jnp.concatenate(mb_rows, axis=0).astype(BF16)
    mc_ref[...] = jnp.concatenate(mc_cols, axis=1).astype(BF16)

    def cdot(lhs, rhs):
        return (jnp.dot(lhs[0], rhs[0], precision=HI, preferred_element_type=F32)
                - jnp.dot(lhs[1], rhs[1], precision=HI, preferred_element_type=F32))

    out_f = cdot(v_lhs[0], v_rhs[0])
    zero_rows = jnp.zeros((CHUNK_ROWS // T_PER_VREG - GROUP_CH, STATE), F32)
    lag0_lhs = (jnp.concatenate([c_re[0], zero_rows], 0), jnp.concatenate([c_im[0], zero_rows], 0))
    out_b = cdot(v_lhs[1], v_rhs[1]) + cdot(lag0_lhs, zbar_f)
    out_b = out_b + jnp.concatenate([d_diag[...], jnp.zeros((CHUNK_ROWS // T_PER_VREG - GROUP_CH, LANES), F32)], 0)
    for jt in range(n_tiles):
        rows = slice(jt * GROUP_CH, (jt + 1) * GROUP_CH)
        v_scr[:, jt * LANES:(jt + 1) * LANES] = out_f[rows]
        v_scr[:, (n_tiles + jt) * LANES:(n_tiles + jt + 1) * LANES] = out_b[rows]
    v_scr[:, 2 * CHUNK_ROWS:] = jnp.zeros((GROUP_CH, LANES), F32)

    for k in range(T_PER_VREG):
        vk = v_scr[:, k * GROUP_CH:k * GROUP_CH + 2 * CHUNK_ROWS]
        for t_out in range(CHUNK):
            off = (CHUNK - t_out) * GROUP_CH
            if off % LANES == k * GROUP_CH:
                al_off = off - k * GROUP_CH
                toep_ref[t_out * GROUP_CH:(t_out + 1) * GROUP_CH, :] = (
                    vk[:, al_off:al_off + CHUNK_ROWS].astype(BF16))


def _s5_prep(lam_re, lam_im, log_dt, b_re, b_im, c_re, c_im, d_skip):
    g = SSM_GROUPS
    d_diag = d_skip.reshape(g, GROUP_CH, 1) * jnp.eye(GROUP_CH, LANES, dtype=F32)
    lam_a = lambda a: a.reshape(2, g, STATE, 1)
    lam_b = lambda a: a.reshape(2, g, 1, STATE)
    tile_b = lambda a: jnp.tile(a, (1, 1, 1, T_PER_VREG))
    dir_spec = lambda r, c: pl.BlockSpec((2, None, r, c), lambda i: (0, i, 0, 0))
    return pl.pallas_call(
        _s5_prep_kernel,
        out_shape=(jax.ShapeDtypeStruct((g, CHUNK_ROWS, CHUNK_ROWS), BF16),
                   jax.ShapeDtypeStruct((g, 4 * STATE, CHUNK_ROWS), BF16),
                   jax.ShapeDtypeStruct((g, CHUNK_ROWS, 4 * STATE), BF16),
                   jax.ShapeDtypeStruct((g, 4, STATE, LANES), F32)),
        grid=(g,),
        in_specs=[dir_spec(STATE, 1), dir_spec(STATE, 1), dir_spec(1, STATE), dir_spec(1, STATE),
                  dir_spec(1, 1), dir_spec(STATE, LANES), dir_spec(STATE, LANES),
                  dir_spec(GROUP_CH, STATE), dir_spec(GROUP_CH, STATE),
                  pl.BlockSpec((None, GROUP_CH, LANES), lambda i: (i, 0, 0))],
        out_specs=(pl.BlockSpec((None, CHUNK_ROWS, CHUNK_ROWS), lambda i: (i, 0, 0)),
                   pl.BlockSpec((None, 4 * STATE, CHUNK_ROWS), lambda i: (i, 0, 0)),
                   pl.BlockSpec((None, CHUNK_ROWS, 4 * STATE), lambda i: (i, 0, 0)),
                   pl.BlockSpec((None, 4, STATE, LANES), lambda i: (i, 0, 0, 0))),
        scratch_shapes=[pltpu.VMEM((GROUP_CH, V_LANES), F32)],
        compiler_params=pltpu.CompilerParams(dimension_semantics=("parallel",)),
        name="s5_prep",
    )(lam_a(lam_re), lam_a(lam_im), lam_b(lam_re), lam_b(lam_im), log_dt.reshape(2, g, 1, 1),
      tile_b(b_re), tile_b(b_im), c_re, c_im, d_diag)


def _slab_perm():
    dst = np.arange(SLAB)
    perm = np.zeros((SLAB, SLAB), np.float32)
    perm[dst, (dst % SLAB_CHUNKS) * SUBLANES + dst // SLAB_CHUNKS] = 1.0
    return perm


def _s5_in_kernel(x_ref, mod_ref, g_ref, w_ref, p_ref, u_ref, z_ref, h_scr, *, chunks_per_seq):
    n_chunks = x_ref.shape[0]
    for b in range(n_chunks // chunks_per_seq):
        rows = slice(b * chunks_per_seq, (b + 1) * chunks_per_seq)
        h = _rms_modulate(x_ref[rows], g_ref[...], mod_ref[b:b + 1, D_MODEL:2 * D_MODEL],
                          mod_ref[b:b + 1, :D_MODEL])
        h2 = h.reshape(chunks_per_seq * SUBLANES, D_MODEL).astype(BF16)
        for s in range(chunks_per_seq // SLAB_CHUNKS):
            hp = jnp.dot(p_ref[...], h2[s * SLAB:(s + 1) * SLAB], preferred_element_type=F32)
            c0 = b * chunks_per_seq + s * SLAB_CHUNKS
            for k in range(SUBLANES):
                h_scr[k, c0:c0 + SLAB_CHUNKS, :] = hp[k * SLAB_CHUNKS:(k + 1) * SLAB_CHUNKS].astype(BF16)
    for k in range(SUBLANES):
        ht = h_scr[k].astype(F32).T.astype(BF16)
        uz = jnp.dot(w_ref[...], ht, preferred_element_type=F32)
        u_ref[k] = uz[:SSM_WIDTH].astype(BF16)
        z_ref[k] = uz[SSM_WIDTH:].astype(BF16)


def _s5_in(x3, mod, g, w_t, chunks_per_seq):
    n_chunks = x3.shape[0]
    once = lambda shape: pl.BlockSpec(shape, lambda t: (0,) * len(shape), pipeline_mode=pl.Buffered(1))
    out = pl.BlockSpec((SUBLANES, SSM_WIDTH, n_chunks), lambda t: (t, 0, 0))
    return pl.pallas_call(
        functools.partial(_s5_in_kernel, chunks_per_seq=chunks_per_seq),
        out_shape=(jax.ShapeDtypeStruct((CHUNK, SSM_WIDTH, n_chunks), BF16),) * 2,
        grid=(CHUNK // SUBLANES,),
        in_specs=[pl.BlockSpec((n_chunks, SUBLANES, D_MODEL), lambda t: (0, t, 0)),
                  once(mod.shape), once((1, D_MODEL)), once((2 * SSM_WIDTH, D_MODEL)),
                  once((SLAB, SLAB))],
        out_specs=(out, out),
        scratch_shapes=[pltpu.VMEM((SUBLANES, n_chunks, D_MODEL), BF16)],
        compiler_params=pltpu.CompilerParams(
            dimension_semantics=("parallel",), vmem_limit_bytes=VMEM_LIMIT_BIG),
        name="s5_in",
    )(x3, mod, g, w_t, jnp.asarray(_slab_perm(), BF16))


def _gelu_tanh(v):
    return 0.5 * v * (1.0 + jnp.tanh(math.sqrt(2.0 / math.pi) * (v + 0.044715 * (v * v * v))))


def _s5_core_kernel(u_ref, toep_ref, mb_ref, mc_ref, al_ref, o_ref, *, chunks_per_seq):
    for gi in range(u_ref.shape[1]):
        _s5_core_group(u_ref.at[:, gi], toep_ref.at[gi], mb_ref.at[gi], mc_ref.at[gi], al_ref.at[gi],
                       o_ref.at[:, gi], chunks_per_seq)


def _s5_core_group(u_ref, toep_ref, mb_ref, mc_ref, al_ref, o_ref, chunks_per_seq):
    n_chunks = u_ref.shape[-1]
    ub = u_ref[...].reshape(CHUNK_ROWS, n_chunks)
    inc = jnp.dot(mb_ref[...], ub, preferred_element_type=F32)
    pos = lax.broadcasted_iota(jnp.int32, (STATE, n_chunks), 1) % chunks_per_seq
    reps = n_chunks // LANES

    def lane_tile(a):
        return jnp.concatenate([a] * reps, axis=1) if reps > 1 else a

    states = []
    for dr in range(2):
        xr = inc[(2 * dr) * STATE:(2 * dr + 1) * STATE]
        xi = inc[(2 * dr + 1) * STATE:(2 * dr + 2) * STATE]
        ar, ai = al_ref[2 * dr], al_ref[2 * dr + 1]

        def shifted(v, step):
            if dr == 0:
                return jnp.where(pos >= step, pltpu.roll(v, step, 1), 0.0)
            return jnp.where(pos < chunks_per_seq - step, pltpu.roll(v, n_chunks - step, 1), 0.0)

        step = 1
        while step < chunks_per_seq:
            sr, si = shifted(xr, step), shifted(xi, step)
            tr, ti = lane_tile(ar), lane_tile(ai)
            xr, xi = xr + tr * sr - ti * si, xi + tr * si + ti * sr
            ar, ai = _cmul(ar, ai, ar, ai)
            step *= 2
        states += [shifted(xr, 1), shifted(xi, 1)]
    h_in = jnp.concatenate(states, axis=0).astype(BF16)
    y = (jnp.dot(toep_ref[...], ub, preferred_element_type=F32)
         + jnp.dot(mc_ref[...], h_in, preferred_element_type=F32))
    o_ref[...] = _gelu_tanh(y).astype(BF16).reshape(CHUNK, GROUP_CH, n_chunks)


def _s5_core(u_cl, toep, mb, mc, al, chunks_per_seq):
    n_chunks = u_cl.shape[-1]
    u4 = u_cl.reshape(CHUNK, SSM_GROUPS, GROUP_CH, n_chunks)
    gs = S5_GROUPS_PER_STEP
    grp = pl.BlockSpec((CHUNK, gs, GROUP_CH, n_chunks), lambda g: (0, g, 0, 0))
    out = pl.pallas_call(
        functools.partial(_s5_core_kernel, chunks_per_seq=chunks_per_seq),
        out_shape=jax.ShapeDtypeStruct(u4.shape, BF16),
        grid=(SSM_GROUPS // gs,),
        in_specs=[grp,
                  pl.BlockSpec((gs, CHUNK_ROWS, CHUNK_ROWS), lambda g: (g, 0, 0)),
                  pl.BlockSpec((gs, 4 * STATE, CHUNK_ROWS), lambda g: (g, 0, 0)),
                  pl.BlockSpec((gs, CHUNK_ROWS, 4 * STATE), lambda g: (g, 0, 0)),
                  pl.BlockSpec((gs, 4, STATE, LANES), lambda g: (g, 0, 0, 0))],
        out_specs=grp,
        compiler_params=pltpu.CompilerParams(
            dimension_semantics=("parallel",), vmem_limit_bytes=VMEM_LIMIT),
        name="s5_core",
    )(u4, toep, mb, mc, al)
    return out.reshape(CHUNK, SSM_WIDTH, n_chunks)


def _s5_out_kernel(g_ref, z_ref, wglu_ref, wout_ref, pt_ref, x_ref, mod_ref, fg_ref, o_ref, y_scr,
                   *, final, chunks_per_seq):
    half = pl.program_id(1)
    n_half = x_ref.shape[0]

    @pl.when(half == 0)
    def _():
        for k in range(SUBLANES):
            gb = g_ref[k]
            glu = jnp.dot(wglu_ref[...], gb, preferred_element_type=F32)
            y = gb.astype(F32) * _sigmoid(glu) * _silu(z_ref[k].astype(F32))
            y_scr[k] = y.T.astype(BF16)

    for hv in range(y_scr.shape[1] // n_half):
        @pl.when(half == hv)
        def _():
            for s in range(n_half // SLAB_CHUNKS):
                c0 = hv * n_half + s * SLAB_CHUNKS
                src = jnp.concatenate([y_scr[k, c0:c0 + SLAB_CHUNKS, :] for k in range(SUBLANES)], axis=0)
                yp = jnp.dot(pt_ref[...], src, preferred_element_type=F32).astype(BF16)
                out = jnp.dot(yp, wout_ref[...], preferred_element_type=F32)
                b = c0 // chunks_per_seq
                upd = mod_ref[b:b + 1, 2 * D_MODEL:] * out
                rows = slice(s * SLAB_CHUNKS, (s + 1) * SLAB_CHUNKS)
                xn = x_ref[rows] + upd.reshape(SLAB_CHUNKS, SUBLANES, D_MODEL)
                if final:
                    xn = xn * lax.rsqrt(jnp.mean(xn * xn, axis=-1, keepdims=True) + NORM_EPS) * fg_ref[...]
                o_ref[rows] = xn


def _s5_out(g_cl, z_cl, wglu_t, w_out, x3, mod, final_g, final, chunks_per_seq, n_split=2):
    n_chunks = x3.shape[0]
    n_half = n_chunks // n_split
    once = lambda shape: pl.BlockSpec(shape, lambda t, h: (0,) * len(shape), pipeline_mode=pl.Buffered(1))
    act = pl.BlockSpec((SUBLANES, SSM_WIDTH, n_chunks), lambda t, h: (t, 0, 0))
    row = pl.BlockSpec((n_half, SUBLANES, D_MODEL), lambda t, h: (h, t, 0))
    return pl.pallas_call(
        functools.partial(_s5_out_kernel, final=final, chunks_per_seq=chunks_per_seq),
        out_shape=jax.ShapeDtypeStruct(x3.shape, F32),
        grid=(CHUNK // SUBLANES, n_split),
        in_specs=[act, act, once((SSM_WIDTH, SSM_WIDTH)), once((SSM_WIDTH, D_MODEL)), once((SLAB, SLAB)),
                  row, once(mod.shape), once((1, D_MODEL))],
        out_specs=row,
        scratch_shapes=[pltpu.VMEM((SUBLANES, n_chunks, SSM_WIDTH), BF16)],
        compiler_params=pltpu.CompilerParams(
            dimension_semantics=("parallel", "arbitrary"), vmem_limit_bytes=VMEM_LIMIT_BIG),
        name="s5_out",
    )(g_cl, z_cl, wglu_t, w_out, jnp.asarray(_slab_perm().T, BF16), x3, mod, final_g)


def _rope_tables(s):
    inv_freq = ROPE_THETA ** (-jnp.arange(0, HEAD_DIM, 2, dtype=F32) / HEAD_DIM)
    ang = jnp.arange(s, dtype=F32)[:, None] * inv_freq[None, :]
    reps = LANES // (HEAD_DIM // 2)
    sign = jnp.where(jnp.arange(LANES) < LANES // 2, -1.0, 1.0).astype(F32)
    cos, sin = jnp.tile(jnp.cos(ang), (1, reps)), jnp.tile(jnp.sin(ang), (1, reps)) * sign
    stream_major = lambda a: a.reshape(s // STREAMS, STREAMS, LANES).transpose(1, 0, 2)
    return cos, sin, stream_major(cos), stream_major(sin)


def _prep_attn_w_in(w):
    d_in = w.shape[0]
    w = w.astype(BF16).reshape(d_in, -1, ATTN_HEADS // 2, 2, 2, HEAD_DIM // 2)
    w = w.transpose(0, 1, 2, 4, 3, 5)
    return w[:, np.array(ATTN_IN_BLOCK_ORDER)].reshape(d_in, -1)


def _prep_attn_w_out(w):
    w = w.astype(BF16).reshape(ATTN_HEADS // 2, 2, 2, HEAD_DIM // 2, w.shape[1])
    return w.transpose(0, 2, 1, 3, 4).reshape(ATTN_WIDTH, -1)


def _trunk(x, ada, norm_g, attn_w, s5_w, s5_ops, final_norm_g):
    bsz, s, _ = x.shape
    chunks_per_seq = s // CHUNK
    n_chunks = bsz * chunks_per_seq
    tables = _rope_tables(s)
    fg = final_norm_g.reshape(1, D_MODEL)
    for i in range(DEPTH):
        mod = ada[i].reshape(bsz, 1, 3 * D_MODEL)
        g = norm_g[i].reshape(1, D_MODEL)
        j = i // N_MIXERS
        if i % N_MIXERS == 0:
            w_in, w_out = attn_w[j]
            nat, qkv12 = _attn_in(x, mod, g, w_in, tables)
            x = _attn_out(_attention(nat, qkv12), nat, x, mod, w_out)
        else:
            w_in_t, wglu_t, w_out = s5_w[j]
            toep, mb, mc, al = s5_ops[j]
            x3 = x.reshape(n_chunks, CHUNK, D_MODEL)
            u_cl, z_cl = _s5_in(x3, ada[i], g, w_in_t, chunks_per_seq)
            g_cl = _s5_core(u_cl, toep, mb, mc, al, chunks_per_seq)
            x3 = _s5_out(g_cl, z_cl, wglu_t, w_out, x3, ada[i], fg, i == DEPTH - 1, chunks_per_seq)
            x = x3.reshape(bsz, s, D_MODEL)
    return x


def kernel(x_prompt, x_sample, c_prompt, c_sample, norm_g, ada_w, ada_b, attn_w_in, attn_w_out,
           ssm_w_in, ssm_lam_re, ssm_lam_im, ssm_log_dt, ssm_b_re, ssm_b_im, ssm_c_re, ssm_c_im,
           ssm_d, ssm_w_glu, ssm_w_out, final_norm_g):
    assert (DEPTH - 1) % N_MIXERS == 1, "the final norm is fused into the last S5 layer"
    assert all(w // (2 * d) == RADIUS for w, d in DILATED_PAIRS)
    assert [d for _, d in DILATED_PAIRS] == [1, 4, STREAMS]
    n_prompt = c_prompt.shape[0]
    ada = _ada(jnp.concatenate([c_prompt, c_sample], axis=0), ada_w, ada_b)
    attn_w = [(_prep_attn_w_in(attn_w_in[j]), _prep_attn_w_out(attn_w_out[j]))
              for j in range(attn_w_in.shape[0])]
    s5_w, s5_ops = [], []
    for j in range(ssm_w_in.shape[0]):
        s5_w.append((ssm_w_in[j].T.astype(BF16), ssm_w_glu[j].T.astype(BF16), ssm_w_out[j].astype(BF16)))
        s5_ops.append(_s5_prep(ssm_lam_re[j], ssm_lam_im[j], ssm_log_dt[j], ssm_b_re[j],
                               ssm_b_im[j], ssm_c_re[j], ssm_c_im[j], ssm_d[j]))
    y_prompt = _trunk(x_prompt, ada[:, :n_prompt], norm_g, attn_w, s5_w, s5_ops, final_norm_g)
    y_sample = _trunk(x_sample, ada[:, n_prompt:], norm_g, attn_w, s5_w, s5_ops, final_norm_g)
    return (y_prompt, y_sample)
```

```python
import functools
import math

import numpy as np
import jax
import jax.numpy as jnp
from jax import lax
from jax.experimental import pallas as pl
from jax.experimental.pallas import tpu as pltpu

D_MODEL = 1024
DEPTH = 4
N_MIXERS = 2
ATTN_HEADS = 16
HEAD_DIM = 64
ATTN_WIDTH = ATTN_HEADS * HEAD_DIM
DILATED_PAIRS = ((128, 1), (512, 4), (2048, 16))
N_DIL = len(DILATED_PAIRS)
ROPE_THETA = 10000.0
SSM_WIDTH = D_MODEL
GROUP_CH = 16
SSM_GROUPS = SSM_WIDTH // GROUP_CH
STATE = 64
LAMBDA_RE_MAX = -1e-4
NORM_EPS = 1e-6
NEG_INF = -1e30

LANES = 128
SUBLANES = 8
CHUNK = 64
CHUNK_ROWS = CHUNK * GROUP_CH
T_PER_VREG = LANES // GROUP_CH
V_LANES = 2 * CHUNK_ROWS + LANES
STREAMS = 16
RADIUS = 64
ATTN_TQ = 128
ATTN_SUB = 4
STAT_PITCH = 40
ATTN_IN_BLOCK_ORDER = (1, 2, 0, 9, 4, 5, 7, 8, 3, 6)
PERM_ROWS = 512
LOG2_E = math.log2(math.e)
S5_GROUPS_PER_STEP = 2
SLAB = 256
SLAB_CHUNKS = SLAB // SUBLANES
VMEM_LIMIT = 48 * 1024 * 1024
VMEM_LIMIT_BIG = 56 * 1024 * 1024

F32 = jnp.float32
BF16 = jnp.bfloat16
HI = lax.Precision.HIGHEST


def _sigmoid(v):
    return 0.5 * jnp.tanh(0.5 * v) + 0.5


def _silu(v):
    return v * _sigmoid(v)


def _rms_modulate(x, g, scale, shift):
    rs = lax.rsqrt(jnp.mean(x * x, axis=-1, keepdims=True) + NORM_EPS)
    return (x * rs * g) * (1.0 + scale) + shift


def _ada_kernel(c_ref, w_ref, b_ref, o_ref):
    o_ref[...] = jnp.dot(_silu(c_ref[...]), w_ref[...], precision=HI,
                         preferred_element_type=F32) + b_ref[...]


def _ada(c, ada_w, ada_b):
    nb = c.shape[0]
    return pl.pallas_call(
        _ada_kernel,
        out_shape=jax.ShapeDtypeStruct((DEPTH, nb, 3 * D_MODEL), F32),
        grid=(DEPTH, 3),
        in_specs=[
            pl.BlockSpec((nb, D_MODEL), lambda i, j: (0, 0)),
            pl.BlockSpec((None, D_MODEL, D_MODEL), lambda i, j: (i, 0, j)),
            pl.BlockSpec((None, 1, D_MODEL), lambda i, j: (i, 0, j)),
        ],
        out_specs=pl.BlockSpec((None, nb, D_MODEL), lambda i, j: (i, 0, j)),
        name="ada",
    )(c, ada_w, ada_b.reshape(DEPTH, 1, 3 * D_MODEL))


def _attn_in_kernel(x_ref, shift_ref, scale_ref, g_ref, w_ref, cos_ref, sin_ref, cosp_ref, sinp_ref,
                    perm_ref, nat_ref, sm_ref, h_nat, h_perm):
    j = pl.program_id(2)
    tm = x_ref.shape[0]
    n_sub = tm // PERM_ROWS
    sub_rows = PERM_ROWS // STREAMS

    @pl.when(j == 0)
    def _():
        h = _rms_modulate(x_ref[...], g_ref[...], scale_ref[...], shift_ref[...])
        h_nat[...] = h.astype(BF16)
        for u in range(n_sub):
            rs = slice(u * PERM_ROWS, (u + 1) * PERM_ROWS)
            h_perm[rs, :] = jnp.dot(perm_ref[...], h_nat[rs, :], preferred_element_type=F32).astype(BF16)

    def rope_blocks(acc, cos, sin):
        for half in range(2):
            is_q = jnp.logical_or(j == 4, jnp.logical_and(j == 1, half == 0))
            plain = (j != 4) if half == 1 else (j < 0)
            qscale = jnp.where(is_q, HEAD_DIM ** -0.5 * LOG2_E, 1.0).astype(F32)
            c = jnp.where(plain, 1.0, cos * qscale)
            sn = jnp.where(plain, 0.0, sin * qscale)
            for b in range(ATTN_WIDTH // LANES):
                sl = slice(half * ATTN_WIDTH + b * LANES, half * ATTN_WIDTH + (b + 1) * LANES)
                t = acc[:, sl]
                yield sl, (t * c + pltpu.roll(t, LANES // 2, 1) * sn).astype(BF16)

    @pl.when(j < 2)
    def _():
        acc = jnp.dot(h_nat[...], w_ref[...], preferred_element_type=F32)
        for sl, blk in rope_blocks(acc, cos_ref[...], sin_ref[...]):
            nat_ref[:, sl] = blk

    @pl.when(j >= 2)
    def _():
        acc = jnp.dot(h_perm[...], w_ref[...], preferred_element_type=F32)

        def table(ref):
            return jnp.concatenate([ref[:, u * sub_rows:(u + 1) * sub_rows, :].reshape(PERM_ROWS, LANES)
                                    for u in range(n_sub)], axis=0)

        for sl, blk in rope_blocks(acc, table(cosp_ref), table(sinp_ref)):
            for u in range(n_sub):
                sm_ref[:, u * sub_rows:(u + 1) * sub_rows, sl] = (
                    blk[u * PERM_ROWS:(u + 1) * PERM_ROWS].reshape(STREAMS, sub_rows, LANES))


def _stream_perm(n):
    dst = np.arange(n)
    rows = n // STREAMS
    perm = np.zeros((n, n), np.float32)
    perm[dst, STREAMS * (dst % rows) + dst // rows] = 1.0
    return perm


def _attn_in(x, mod, g, w, tables, tm=1024):
    bsz, s, _ = x.shape
    cos, sin, cosp, sinp = tables
    rows = tm // STREAMS
    wide = 2 * ATTN_WIDTH
    perm = jnp.asarray(_stream_perm(PERM_ROWS), BF16)
    return pl.pallas_call(
        _attn_in_kernel,
        out_shape=(jax.ShapeDtypeStruct((bsz, s, 2 * wide), BF16),
                   jax.ShapeDtypeStruct((bsz, STREAMS, s // STREAMS, 3 * wide), BF16)),
        grid=(bsz, s // tm, 5),
        in_specs=[
            pl.BlockSpec((None, tm, D_MODEL), lambda b, i, j: (b, i, 0)),
            pl.BlockSpec((None, 1, D_MODEL), lambda b, i, j: (b, 0, 0)),
            pl.BlockSpec((None, 1, D_MODEL), lambda b, i, j: (b, 0, 1)),
            pl.BlockSpec((1, D_MODEL), lambda b, i, j: (0, 0)),
            pl.BlockSpec((D_MODEL, wide), lambda b, i, j: (0, j)),
            pl.BlockSpec((tm, LANES), lambda b, i, j: (i, 0)),
            pl.BlockSpec((tm, LANES), lambda b, i, j: (i, 0)),
            pl.BlockSpec((STREAMS, rows, LANES), lambda b, i, j: (0, i, 0)),
            pl.BlockSpec((STREAMS, rows, LANES), lambda b, i, j: (0, i, 0)),
            pl.BlockSpec((PERM_ROWS, PERM_ROWS), lambda b, i, j: (0, 0)),
        ],
        out_specs=(
            pl.BlockSpec((None, tm, wide), lambda b, i, j: (b, i, jnp.clip(j, 0, 1))),
            pl.BlockSpec((None, STREAMS, rows, wide), lambda b, i, j: (b, 0, i, jnp.clip(j - 2, 0, 2))),
        ),
        scratch_shapes=[pltpu.VMEM((tm, D_MODEL), BF16), pltpu.VMEM((tm, D_MODEL), BF16)],
        compiler_params=pltpu.CompilerParams(
            dimension_semantics=("parallel", "parallel", "arbitrary"),
            vmem_limit_bytes=VMEM_LIMIT_BIG),
        name="attn_in",
    )(x, mod, mod, g, w, cos, sin, cosp, sinp, perm)


def _lane_of_odd_head(lane):
    return (lane // (HEAD_DIM // 2)) % 2 == 1


def _mask_bias(valid):
    return jnp.where(valid, 0.0, NEG_INF).astype(F32)


def _attn_pairs(get_q, get_k, get_v, bias, put_acc):
    tq = bias.shape[0]
    lane = lax.broadcasted_iota(jnp.int32, (tq, LANES), 1)
    q_is_odd = _lane_of_odd_head(lane)
    mx_tile = jnp.zeros((tq, LANES), F32)
    den_tile = jnp.ones((tq, LANES), F32)
    ones = jnp.ones((bias.shape[1], LANES), BF16)
    bias2 = jnp.concatenate([bias, bias], axis=0)
    for b in range(ATTN_WIDTH // LANES):
        sl = slice(b * LANES, (b + 1) * LANES)
        qb = get_q(sl).astype(F32)
        kb = get_k(sl)
        vb = jnp.concatenate([get_v(sl), ones], axis=1)
        qm = jnp.concatenate([jnp.where(q_is_odd, 0.0, qb), jnp.where(q_is_odd, qb, 0.0)], axis=0).astype(BF16)
        sc = lax.dot_general(qm, kb, (((1,), (1,)), ((), ())), preferred_element_type=F32) + bias2
        mx = jnp.max(sc, axis=1, keepdims=True)
        p = jnp.exp2((sc - mx).astype(BF16))
        acc = jnp.dot(p, vb, preferred_element_type=F32)
        for odd in (0, 1):
            rows = slice(odd * tq, (odd + 1) * tq)
            mx_tile = jnp.where(lane == 2 * b + odd, mx[rows], mx_tile)
            den_tile = jnp.where(lane == 2 * b + odd, acc[rows, LANES:], den_tile)
        put_acc(sl, jnp.where(q_is_odd, acc[tq:, :LANES], acc[:tq, :LANES]))
    return mx_tile, den_tile


def _pair_factor(tile, b):
    lane = lax.broadcasted_iota(jnp.int32, tile.shape, 1)
    head = 2 * b + _lane_of_odd_head(lane).astype(jnp.int32)
    return jnp.take_along_axis(tile, head, axis=1, mode="promise_in_bounds")


def _merge_groups(stats, get_accs, put_o):
    top = functools.reduce(jnp.maximum, [mx for mx, _ in stats])
    ws = [jnp.exp2(mx - top) for mx, _ in stats]
    inv = 1.0 / sum(w * den for w, (_, den) in zip(ws, stats))
    fs = [w * inv for w in ws]
    for b in range(ATTN_WIDTH // LANES):
        sl = slice(b * LANES, (b + 1) * LANES)
        put_o(sl, sum(_pair_factor(f, b) * get_acc(sl) for f, get_acc in zip(fs, get_accs)))


def _window_mask(tile, tq, m):
    nk = tq + 2 * RADIUS
    row = lax.broadcasted_iota(jnp.int32, (tq, nk), 0)
    col = lax.broadcasted_iota(jnp.int32, (tq, nk), 1)
    kabs = tile * tq - RADIUS + col
    return _mask_bias((jnp.abs(col - RADIUS - row) <= RADIUS) & (kabs >= 0) & (kabs < m))


def _stack_rows(read, prev_ref, cur_ref, next_ref, lo, cnt):
    halo, cur_n = prev_ref.shape[-2], cur_ref.shape[-2]
    a, b = lo, lo + cnt
    parts = []
    if a < 0:
        parts.append(read(prev_ref, halo + a, halo))
        a = 0
    parts.append(read(cur_ref, a, min(b, cur_n)))
    if b > cur_n:
        parts.append(read(next_ref, 0, b - cur_n))
    return parts


def _kv_getters(stack):
    get_k = lambda sl: stack(sl)
    get_v = lambda sl: stack(slice(ATTN_WIDTH + sl.start, ATTN_WIDTH + sl.stop))
    return get_k, get_v


def _attn16_kernel(q_ref, kvp_ref, kvc_ref, kvn_ref, o_ref, mx_ref, den_ref, *, m):
    tq = ATTN_TQ
    n_sub = q_ref.shape[0] // tq
    for t in range(n_sub):
        rows = slice(t * tq, (t + 1) * tq)
        bias = _window_mask(pl.program_id(2) * n_sub + t, tq, m)

        def stack(cols):
            return jnp.concatenate(_stack_rows(lambda ref, a, b: ref[a:b, cols], kvp_ref, kvc_ref, kvn_ref,
                                               t * tq - RADIUS, tq + 2 * RADIUS), axis=0)

        def put_acc(sl, val):
            o_ref[rows, sl] = val.astype(BF16)

        mx_ref[rows], den_ref[rows] = _attn_pairs(lambda sl: q_ref[rows, sl], *_kv_getters(stack), bias, put_acc)


def _attn4_kernel(q_ref, kvp_ref, kvc_ref, kvn_ref, o_ref, mx_ref, den_ref, *, m):
    ns, halo = q_ref.shape[0], kvp_ref.shape[1]
    rows = ATTN_TQ // ns
    span = rows + 2 * halo
    n_sub = q_ref.shape[1] // rows
    n = lax.broadcasted_iota(jnp.int32, (ns * rows, ns * span), 0)
    c = lax.broadcasted_iota(jnp.int32, (ns * rows, ns * span), 1)
    for t in range(n_sub):
        base = (pl.program_id(2) * n_sub + t) * rows
        rs = slice(t * rows, (t + 1) * rows)
        krow = base - halo + c % span
        rel = ns * (krow - (base + n % rows)) + (c // span - n // rows)
        bias = _mask_bias((jnp.abs(rel) <= RADIUS) & (krow >= 0) & (krow < m))

        def stack(cols):
            parts = []
            for s in range(ns):
                parts += _stack_rows(lambda ref, a, b: ref[s, a:b, cols], kvp_ref, kvc_ref, kvn_ref,
                                     t * rows - halo, span)
            return jnp.concatenate(parts, axis=0)

        flat = lambda ref: (lambda sl: ref[:, rs, sl].reshape(ns * rows, LANES))

        def put_acc(sl, val):
            o_ref[:, rs, sl] = val.astype(BF16).reshape(ns, rows, LANES)

        mx, den = _attn_pairs(flat(q_ref), *_kv_getters(stack), bias, put_acc)
        mx_ref[:, rs, :] = mx.reshape(ns, rows, LANES)
        den_ref[:, rs, :] = den.reshape(ns, rows, LANES)


def _attn1_kernel(q_ref, kvp_ref, kvc_ref, kvn_ref, o_ref, mx_ref, den_ref, *, m):
    tq = ATTN_TQ
    n_sub = q_ref.shape[0] // tq
    for t in range(n_sub):
        rows = slice(t * tq, (t + 1) * tq)
        bias = _window_mask(pl.program_id(1) * n_sub + t, tq, m)

        def stack(cols):
            return jnp.concatenate(_stack_rows(lambda ref, a, b: ref[a:b, cols], kvp_ref, kvc_ref, kvn_ref,
                                               t * tq - RADIUS, tq + 2 * RADIUS), axis=0)

        def put_acc(sl, val):
            o_ref[rows, sl] = val.astype(BF16)

        mx_ref[rows], den_ref[rows] = _attn_pairs(lambda sl: q_ref[rows, sl], *_kv_getters(stack), bias, put_acc)


def _attention(qkv0, qkv12):
    bsz, s, _ = qkv0.shape
    m16 = s // STREAMS
    tq, half = ATTN_TQ, RADIUS
    cur_rows = min(ATTN_SUB * tq, m16)
    params = lambda n: pltpu.CompilerParams(dimension_semantics=("parallel",) * n,
                                            vmem_limit_bytes=VMEM_LIMIT)
    same = lambda i: i

    def halo_idx(cur_n, halo_n, total):
        per = cur_n // halo_n
        return (lambda i: jnp.maximum(i * per - 1, 0)), (lambda i: jnp.minimum((i + 1) * per, total // halo_n - 1))

    def spec16(nrows, row_idx, col, width=ATTN_WIDTH):
        return pl.BlockSpec((None, None, nrows, width), lambda b, r, i: (b, r, row_idx(i), col))
    prv, nxt = halo_idx(cur_rows, half, m16)
    stat16 = jax.ShapeDtypeStruct((bsz, STREAMS, m16, LANES), F32)
    o2, mx2, den2 = pl.pallas_call(
        functools.partial(_attn16_kernel, m=m16),
        out_shape=(jax.ShapeDtypeStruct((bsz, STREAMS, m16, ATTN_WIDTH), BF16), stat16, stat16),
        grid=(bsz, STREAMS, m16 // cur_rows),
        in_specs=[spec16(cur_rows, same, 5), spec16(half, prv, 1, 2 * ATTN_WIDTH),
                  spec16(cur_rows, same, 1, 2 * ATTN_WIDTH), spec16(half, nxt, 1, 2 * ATTN_WIDTH)],
        out_specs=(spec16(cur_rows, same, 0), spec16(cur_rows, same, 0, LANES), spec16(cur_rows, same, 0, LANES)),
        compiler_params=params(3), name="attn_dil16",
    )(*([qkv12] * 4))

    ns = STREAMS // 4
    rows4, halo4 = cur_rows // ns, half // ns

    def view4(a):
        return a.reshape(bsz, ns, 4, m16, a.shape[-1])

    def spec4(nrows, row_idx, col, width=ATTN_WIDTH):
        return pl.BlockSpec((None, ns, None, nrows, width), lambda b, r, i: (b, 0, r, row_idx(i), col))
    prv4, nxt4 = halo_idx(rows4, halo4, m16)
    q4 = view4(qkv12)
    stat4 = jax.ShapeDtypeStruct((bsz, ns, 4, m16, LANES), F32)
    o1, mx1, den1 = pl.pallas_call(
        functools.partial(_attn4_kernel, m=m16),
        out_shape=(jax.ShapeDtypeStruct((bsz, ns, 4, m16, ATTN_WIDTH), BF16), stat4, stat4),
        grid=(bsz, 4, m16 // rows4),
        in_specs=[spec4(rows4, same, 4), spec4(halo4, prv4, 0, 2 * ATTN_WIDTH),
                  spec4(rows4, same, 0, 2 * ATTN_WIDTH), spec4(halo4, nxt4, 0, 2 * ATTN_WIDTH)],
        out_specs=(spec4(rows4, same, 0), spec4(rows4, same, 0, LANES), spec4(rows4, same, 0, LANES)),
        compiler_params=params(3), name="attn_dil4",
    )(q4, q4, q4, q4)
    streams = lambda a: a.reshape(bsz, STREAMS, m16, a.shape[-1])

    cur1 = ATTN_SUB * tq

    def spec1(nrows, row_idx, col, width=ATTN_WIDTH):
        return pl.BlockSpec((None, nrows, width), lambda b, i: (b, row_idx(i), col))
    prv1, nxt1 = halo_idx(cur1, half, s)
    stat1 = jax.ShapeDtypeStruct((bsz, s, LANES), F32)
    group0 = pl.pallas_call(
        functools.partial(_attn1_kernel, m=s),
        out_shape=(jax.ShapeDtypeStruct((bsz, s, ATTN_WIDTH), BF16), stat1, stat1),
        grid=(bsz, s // cur1),
        in_specs=[spec1(cur1, same, 2), spec1(half, prv1, 0, 2 * ATTN_WIDTH),
                  spec1(cur1, same, 0, 2 * ATTN_WIDTH), spec1(half, nxt1, 0, 2 * ATTN_WIDTH)],
        out_specs=(spec1(cur1, same, 0), spec1(cur1, same, 0, LANES), spec1(cur1, same, 0, LANES)),
        compiler_params=params(2), name="attn_dil1",
    )(*([qkv0] * 4))
    return group0, (o2, mx2, den2), (streams(o1), streams(mx1), streams(den1))


def _attn_out_kernel(o0_ref, mx0_ref, den0_ref, o2_ref, mx2_ref, den2_ref, o1_ref, mx1_ref, den1_ref,
                     unperm_ref, z_ref, x_ref, gate_ref, w_ref, out_ref, y_scr, a2_scr, a1_scr, stat_scr):
    tm = x_ref.shape[0]
    tq = ATTN_TQ
    srow, n_srow = tq // STREAMS, mx2_ref.shape[1]
    for src, dst in ((o2_ref, a2_scr), (o1_ref, a1_scr)):
        dst[...] = jnp.dot(unperm_ref[...], src[...].reshape(dst.shape), preferred_element_type=F32)
    for k, src in enumerate((mx2_ref, den2_ref, mx1_ref, den1_ref)):
        for r in range(STREAMS):
            stat_scr[k, r * STAT_PITCH:r * STAT_PITCH + n_srow, :] = src[r]
    for t in range(tm // tq):
        rows = slice(t * tq, (t + 1) * tq)

        def natural(k):
            return jnp.concatenate([stat_scr[k, pl.ds(t * srow + ii, STREAMS, stride=STAT_PITCH), :]
                                    for ii in range(srow)], axis=0)

        def put_y(sl, val):
            y_scr[rows, sl] = (val * _silu(z_ref[rows, sl].astype(F32))).astype(BF16)

        stats = [(mx0_ref[rows], den0_ref[rows]), (natural(0), natural(1)), (natural(2), natural(3))]
        _merge_groups(stats, [lambda sl: o0_ref[rows, sl].astype(F32), lambda sl: a2_scr[rows, sl],
                              lambda sl: a1_scr[rows, sl]], put_y)
    out = jnp.dot(y_scr[...], w_ref[...], preferred_element_type=F32)
    out_ref[...] = x_ref[...] + gate_ref[...] * out


def _attn_out(groups, nat, x, mod, w, tm=PERM_ROWS):
    bsz, s, _ = x.shape
    row = lambda width: pl.BlockSpec((None, tm, width), lambda b, i: (b, i, 0))
    sm = lambda width: pl.BlockSpec((None, STREAMS, tm // STREAMS, width), lambda b, i: (b, 0, i, 0))
    z_col = nat.shape[-1] // ATTN_WIDTH - 1
    unperm = jnp.asarray(_stream_perm(tm).T, BF16)
    return pl.pallas_call(
        _attn_out_kernel,
        out_shape=jax.ShapeDtypeStruct(x.shape, F32),
        grid=(bsz, s // tm),
        in_specs=[row(ATTN_WIDTH), row(LANES), row(LANES)] + [sm(ATTN_WIDTH), sm(LANES), sm(LANES)] * 2
                 + [pl.BlockSpec((tm, tm), lambda b, i: (0, 0)),
                    pl.BlockSpec((None, tm, ATTN_WIDTH), lambda b, i: (b, i, z_col)), row(D_MODEL),
                    pl.BlockSpec((None, 1, D_MODEL), lambda b, i: (b, 0, 2)),
                    pl.BlockSpec((ATTN_WIDTH, D_MODEL), lambda b, i: (0, 0))],
        out_specs=row(D_MODEL),
        scratch_shapes=[pltpu.VMEM((tm, ATTN_WIDTH), BF16), pltpu.VMEM((tm, ATTN_WIDTH), F32),
                        pltpu.VMEM((tm, ATTN_WIDTH), F32), pltpu.VMEM((4, STREAMS * STAT_PITCH, LANES), F32)],
        compiler_params=pltpu.CompilerParams(
            dimension_semantics=("parallel", "parallel"),
            vmem_limit_bytes=VMEM_LIMIT),
        name="attn_out",
    )(*groups[0], *groups[1], *groups[2], unperm, nat, x, mod, w)


def _cmul(ar, ai, br, bi):
    return ar * br - ai * bi, ar * bi + ai * br


def _s5_prep_kernel(lam_re_a, lam_im_a, lam_re_b, lam_im_b, logdt, bt_re, bt_im,
                    c_re, c_im, d_diag, toep_ref, mb_ref, mc_ref, al_ref, v_scr):
    lane = lax.broadcasted_iota(jnp.int32, (STATE, LANES), 1)
    a_idx = lane // GROUP_CH
    n_tiles = CHUNK // T_PER_VREG

    mb_rows = []
    v_rhs = []
    v_lhs = []
    mc_cols = []
    zbar_f = None
    for dr in range(2):
        dt = jnp.exp(logdt[dr])
        lr = jnp.minimum(lam_re_a[dr], LAMBDA_RE_MAX)
        li = lam_im_a[dr]
        ldr, ldi = lr * dt, li * dt

        mag = jnp.exp(ldr)
        l1r, l1i = mag * jnp.cos(ldi), mag * jnp.sin(ldi)
        small = [(jnp.ones_like(l1r), jnp.zeros_like(l1r)), (l1r, l1i)]
        for _ in range(T_PER_VREG - 1):
            small.append(_cmul(*small[-1], l1r, l1i))

        def lane_powers(exponent_of_step):
            pr, pi = small[exponent_of_step[0]]
            pr, pi = jnp.broadcast_to(pr, (STATE, LANES)), jnp.broadcast_to(pi, (STATE, LANES))
            for a in range(1, T_PER_VREG):
                er, ei = small[exponent_of_step[a]]
                pr, pi = jnp.where(a_idx == a, er, pr), jnp.where(a_idx == a, ei, pi)
            return pr, pi

        den = lr * lr + li * li
        nr, ni = l1r - 1.0, l1i
        cr, ci = (nr * lr + ni * li) / den, (ni * lr - nr * li) / den
        bbr, bbi = _cmul(cr, ci, bt_re[dr], bt_im[dr])
        l8r, l8i = small[T_PER_VREG]
        tiles = [None] * n_tiles
        if dr == 0:
            pr, pi = lane_powers([T_PER_VREG - 1 - a for a in range(T_PER_VREG)])
            cur = _cmul(pr, pi, bbr, bbi)
            for jt in range(n_tiles - 1, -1, -1):
                tiles[jt] = cur
                if jt:
                    cur = _cmul(cur[0], cur[1], l8r, l8i)
            v_rhs.append(_cmul(tiles[n_tiles - 1][0], tiles[n_tiles - 1][1], l1r, l1i))
            zbar_f = (jnp.where(lane < GROUP_CH, bbr, 0.0), jnp.where(lane < GROUP_CH, bbi, 0.0))
        else:
            pr, pi = lane_powers(list(range(T_PER_VREG)))
            cur = _cmul(pr, pi, bbr, bbi)
            for jt in range(n_tiles):
                tiles[jt] = cur
                if jt < n_tiles - 1:
                    cur = _cmul(cur[0], cur[1], l8r, l8i)
            v_rhs.append(tiles[0])
        mb_rows.append(jnp.concatenate([t[0] for t in tiles], axis=1))
        mb_rows.append(jnp.concatenate([t[1] for t in tiles], axis=1))
        a_r, a_i = l8r, l8i
        for _ in range(int(math.log2(CHUNK // T_PER_VREG))):
            a_r, a_i = _cmul(a_r, a_i, a_r, a_i)
        al_ref[2 * dr] = jnp.broadcast_to(a_r, (STATE, LANES))
        al_ref[2 * dr + 1] = jnp.broadcast_to(a_i, (STATE, LANES))

        lrb = jnp.minimum(lam_re_b[dr], LAMBDA_RE_MAX)
        lib = lam_im_b[dr]
        mag = jnp.exp(lrb * dt)
        pw_r, pw_i = mag * jnp.cos(lib * dt), mag * jnp.sin(lib * dt)
        pows = [(pw_r, pw_i)]
        for _ in range(int(math.log2(CHUNK)) - 1):
            pows.append(_cmul(*pows[-1], *pows[-1]))
        cre, cim = c_re[dr], c_im[dr]
        xr, xi = _cmul(cre, cim, *pows[0])
        for k in range(int(math.log2(CHUNK))):
            yr, yi = _cmul(xr, xi, *pows[k])
            if dr == 0:
                xr, xi = jnp.concatenate([xr, yr], 0), jnp.concatenate([xi, yi], 0)
            else:
                xr, xi = jnp.concatenate([yr, xr], 0), jnp.concatenate([yi, xi], 0)
        mc_cols += [xr, -xi]
        xr, xi = cre, cim
        for k in range(int(math.log2(n_tiles))):
            yr, yi = _cmul(xr, xi, *pows[k + int(math.log2(T_PER_VREG))])
            if dr == 0:
                xr, xi = jnp.concatenate([yr, xr], 0), jnp.concatenate([yi, xi], 0)
            else:
                xr, xi = jnp.concatenate([xr, yr], 0), jnp.concatenate([xi, yi], 0)
        v_lhs.append((xr, xi))

    mb_ref[...] = jnp.concatenate(mb_rows, axis=0).astype(BF16)
    mc_ref[...] = jnp.concatenate(mc_cols, axis=1).astype(BF16)

    def cdot(lhs, rhs):
        return (jnp.dot(lhs[0], rhs[0], precision=HI, preferred_element_type=F32)
                - jnp.dot(lhs[1], rhs[1], precision=HI, preferred_element_type=F32))

    out_f = cdot(v_lhs[0], v_rhs[0])
    zero_rows = jnp.zeros((CHUNK_ROWS // T_PER_VREG - GROUP_CH, STATE), F32)
    lag0_lhs = (jnp.concatenate([c_re[0], zero_rows], 0), jnp.concatenate([c_im[0], zero_rows], 0))
    out_b = cdot(v_lhs[1], v_rhs[1]) + cdot(lag0_lhs, zbar_f)
    out_b = out_b + jnp.concatenate([d_diag[...], jnp.zeros((CHUNK_ROWS // T_PER_VREG - GROUP_CH, LANES), F32)], 0)
    for jt in range(n_tiles):
        rows = slice(jt * GROUP_CH, (jt + 1) * GROUP_CH)
        v_scr[:, jt * LANES:(jt + 1) * LANES] = out_f[rows]
        v_scr[:, (n_tiles + jt) * LANES:(n_tiles + jt + 1) * LANES] = out_b[rows]
    v_scr[:, 2 * CHUNK_ROWS:] = jnp.zeros((GROUP_CH, LANES), F32)

    for k in range(T_PER_VREG):
        vk = v_scr[:, k * GROUP_CH:k * GROUP_CH + 2 * CHUNK_ROWS]
        for t_out in range(CHUNK):
            off = (CHUNK - t_out) * GROUP_CH
            if off % LANES == k * GROUP_CH:
                al_off = off - k * GROUP_CH
                toep_ref[t_out * GROUP_CH:(t_out + 1) * GROUP_CH, :] = (
                    vk[:, al_off:al_off + CHUNK_ROWS].astype(BF16))


def _s5_prep(lam_re, lam_im, log_dt, b_re, b_im, c_re, c_im, d_skip):
    g = SSM_GROUPS
    d_diag = d_skip.reshape(g, GROUP_CH, 1) * jnp.eye(GROUP_CH, LANES, dtype=F32)
    lam_a = lambda a: a.reshape(2, g, STATE, 1)
    lam_b = lambda a: a.reshape(2, g, 1, STATE)
    tile_b = lambda a: jnp.tile(a, (1, 1, 1, T_PER_VREG))
    dir_spec = lambda r, c: pl.BlockSpec((2, None, r, c), lambda i: (0, i, 0, 0))
    return pl.pallas_call(
        _s5_prep_kernel,
        out_shape=(jax.ShapeDtypeStruct((g, CHUNK_ROWS, CHUNK_ROWS), BF16),
                   jax.ShapeDtypeStruct((g, 4 * STATE, CHUNK_ROWS), BF16),
                   jax.ShapeDtypeStruct((g, CHUNK_ROWS, 4 * STATE), BF16),
                   jax.ShapeDtypeStruct((g, 4, STATE, LANES), F32)),
        grid=(g,),
        in_specs=[dir_spec(STATE, 1), dir_spec(STATE, 1), dir_spec(1, STATE), dir_spec(1, STATE),
                  dir_spec(1, 1), dir_spec(STATE, LANES), dir_spec(STATE, LANES),
                  dir_spec(GROUP_CH, STATE), dir_spec(GROUP_CH, STATE),
                  pl.BlockSpec((None, GROUP_CH, LANES), lambda i: (i, 0, 0))],
        out_specs=(pl.BlockSpec((None, CHUNK_ROWS, CHUNK_ROWS), lambda i: (i, 0, 0)),
                   pl.BlockSpec((None, 4 * STATE, CHUNK_ROWS), lambda i: (i, 0, 0)),
                   pl.BlockSpec((None, CHUNK_ROWS, 4 * STATE), lambda i: (i, 0, 0)),
                   pl.BlockSpec((None, 4, STATE, LANES), lambda i: (i, 0, 0, 0))),
        scratch_shapes=[pltpu.VMEM((GROUP_CH, V_LANES), F32)],
        compiler_params=pltpu.CompilerParams(dimension_semantics=("parallel",)),
        name="s5_prep",
    )(lam_a(lam_re), lam_a(lam_im), lam_b(lam_re), lam_b(lam_im), log_dt.reshape(2, g, 1, 1),
      tile_b(b_re), tile_b(b_im), c_re, c_im, d_diag)


def _slab_perm():
    dst = np.arange(SLAB)
    perm = np.zeros((SLAB, SLAB), np.float32)
    perm[dst, (dst % SLAB_CHUNKS) * SUBLANES + dst // SLAB_CHUNKS] = 1.0
    return perm


def _s5_in_kernel(x_ref, mod_ref, g_ref, w_ref, p_ref, u_ref, z_ref, h_scr, *, chunks_per_seq):
    n_chunks = x_ref.shape[0]
    for b in range(n_chunks // chunks_per_seq):
        rows = slice(b * chunks_per_seq, (b + 1) * chunks_per_seq)
        h = _rms_modulate(x_ref[rows], g_ref[...], mod_ref[b:b + 1, D_MODEL:2 * D_MODEL],
                          mod_ref[b:b + 1, :D_MODEL])
        h2 = h.reshape(chunks_per_seq * SUBLANES, D_MODEL).astype(BF16)
        for s in range(chunks_per_seq // SLAB_CHUNKS):
            hp = jnp.dot(p_ref[...], h2[s * SLAB:(s + 1) * SLAB], preferred_element_type=F32)
            c0 = b * chunks_per_seq + s * SLAB_CHUNKS
            for k in range(SUBLANES):
                h_scr[k, c0:c0 + SLAB_CHUNKS, :] = hp[k * SLAB_CHUNKS:(k + 1) * SLAB_CHUNKS].astype(BF16)
    for k in range(SUBLANES):
        ht = h_scr[k].astype(F32).T.astype(BF16)
        uz = jnp.dot(w_ref[...], ht, preferred_element_type=F32)
        u_ref[k] = uz[:SSM_WIDTH].astype(BF16)
        z_ref[k] = uz[SSM_WIDTH:].astype(BF16)


def _s5_in(x3, mod, g, w_t, chunks_per_seq):
    n_chunks = x3.shape[0]
    once = lambda shape: pl.BlockSpec(shape, lambda t: (0,) * len(shape), pipeline_mode=pl.Buffered(1))
    out = pl.BlockSpec((SUBLANES, SSM_WIDTH, n_chunks), lambda t: (t, 0, 0))
    return pl.pallas_call(
        functools.partial(_s5_in_kernel, chunks_per_seq=chunks_per_seq),
        out_shape=(jax.ShapeDtypeStruct((CHUNK, SSM_WIDTH, n_chunks), BF16),) * 2,
        grid=(CHUNK // SUBLANES,),
        in_specs=[pl.BlockSpec((n_chunks, SUBLANES, D_MODEL), lambda t: (0, t, 0)),
                  once(mod.shape), once((1, D_MODEL)), once((2 * SSM_WIDTH, D_MODEL)),
                  once((SLAB, SLAB))],
        out_specs=(out, out),
        scratch_shapes=[pltpu.VMEM((SUBLANES, n_chunks, D_MODEL), BF16)],
        compiler_params=pltpu.CompilerParams(
            dimension_semantics=("parallel",), vmem_limit_bytes=VMEM_LIMIT_BIG),
        name="s5_in",
    )(x3, mod, g, w_t, jnp.asarray(_slab_perm(), BF16))


def _gelu_tanh(v):
    return 0.5 * v * (1.0 + jnp.tanh(math.sqrt(2.0 / math.pi) * (v + 0.044715 * (v * v * v))))


def _s5_core_kernel(u_ref, toep_ref, mb_ref, mc_ref, al_ref, o_ref, *, chunks_per_seq):
    for gi in range(u_ref.shape[1]):
        _s5_core_group(u_ref.at[:, gi], toep_ref.at[gi], mb_ref.at[gi], mc_ref.at[gi], al_ref.at[gi],
                       o_ref.at[:, gi], chunks_per_seq)


def _s5_core_group(u_ref, toep_ref, mb_ref, mc_ref, al_ref, o_ref, chunks_per_seq):
    n_chunks = u_ref.shape[-1]
    ub = u_ref[...].reshape(CHUNK_ROWS, n_chunks)
    inc = jnp.dot(mb_ref[...], ub, preferred_element_type=F32)
    pos = lax.broadcasted_iota(jnp.int32, (STATE, n_chunks), 1) % chunks_per_seq
    reps = n_chunks // LANES

    def lane_tile(a):
        return jnp.concatenate([a] * reps, axis=1) if reps > 1 else a

    states = []
    for dr in range(2):
        xr = inc[(2 * dr) * STATE:(2 * dr + 1) * STATE]
        xi = inc[(2 * dr + 1) * STATE:(2 * dr + 2) * STATE]
        ar, ai = al_ref[2 * dr], al_ref[2 * dr + 1]

        def shifted(v, step):
            if dr == 0:
                return jnp.where(pos >= step, pltpu.roll(v, step, 1), 0.0)
            return jnp.where(pos < chunks_per_seq - step, pltpu.roll(v, n_chunks - step, 1), 0.0)

        step = 1
        while step < chunks_per_seq:
            sr, si = shifted(xr, step), shifted(xi, step)
            tr, ti = lane_tile(ar), lane_tile(ai)
            xr, xi = xr + tr * sr - ti * si, xi + tr * si + ti * sr
            ar, ai = _cmul(ar, ai, ar, ai)
            step *= 2
        states += [shifted(xr, 1), shifted(xi, 1)]
    h_in = jnp.concatenate(states, axis=0).astype(BF16)
    y = (jnp.dot(toep_ref[...], ub, preferred_element_type=F32)
         + jnp.dot(mc_ref[...], h_in, preferred_element_type=F32))
    o_ref[...] = _gelu_tanh(y).astype(BF16).reshape(CHUNK, GROUP_CH, n_chunks)


def _s5_core(u_cl, toep, mb, mc, al, chunks_per_seq):
    n_chunks = u_cl.shape[-1]
    u4 = u_cl.reshape(CHUNK, SSM_GROUPS, GROUP_CH, n_chunks)
    gs = S5_GROUPS_PER_STEP
    grp = pl.BlockSpec((CHUNK, gs, GROUP_CH, n_chunks), lambda g: (0, g, 0, 0))
    out = pl.pallas_call(
        functools.partial(_s5_core_kernel, chunks_per_seq=chunks_per_seq),
        out_shape=jax.ShapeDtypeStruct(u4.shape, BF16),
        grid=(SSM_GROUPS // gs,),
        in_specs=[grp,
                  pl.BlockSpec((gs, CHUNK_ROWS, CHUNK_ROWS), lambda g: (g, 0, 0)),
                  pl.BlockSpec((gs, 4 * STATE, CHUNK_ROWS), lambda g: (g, 0, 0)),
                  pl.BlockSpec((gs, CHUNK_ROWS, 4 * STATE), lambda g: (g, 0, 0)),
                  pl.BlockSpec((gs, 4, STATE, LANES), lambda g: (g, 0, 0, 0))],
        out_specs=grp,
        compiler_params=pltpu.CompilerParams(
            dimension_semantics=("parallel",), vmem_limit_bytes=VMEM_LIMIT),
        name="s5_core",
    )(u4, toep, mb, mc, al)
    return out.reshape(CHUNK, SSM_WIDTH, n_chunks)


def _s5_out_kernel(g_ref, z_ref, wglu_ref, wout_ref, pt_ref, x_ref, mod_ref, fg_ref, o_ref, y_scr,
                   *, final, chunks_per_seq):
    half = pl.program_id(1)
    n_half = x_ref.shape[0]

    @pl.when(half == 0)
    def _():
        for k in range(SUBLANES):
            gb = g_ref[k]
            glu = jnp.dot(wglu_ref[...], gb, preferred_element_type=F32)
            y = gb.astype(F32) * _sigmoid(glu) * _silu(z_ref[k].astype(F32))
            y_scr[k] = y.T.astype(BF16)

    for hv in range(y_scr.shape[1] // n_half):
        @pl.when(half == hv)
        def _():
            for s in range(n_half // SLAB_CHUNKS):
                c0 = hv * n_half + s * SLAB_CHUNKS
                src = jnp.concatenate([y_scr[k, c0:c0 + SLAB_CHUNKS, :] for k in range(SUBLANES)], axis=0)
                yp = jnp.dot(pt_ref[...], src, preferred_element_type=F32).astype(BF16)
                out = jnp.dot(yp, wout_ref[...], preferred_element_type=F32)
                b = c0 // chunks_per_seq
                upd = mod_ref[b:b + 1, 2 * D_MODEL:] * out
                rows = slice(s * SLAB_CHUNKS, (s + 1) * SLAB_CHUNKS)
                xn = x_ref[rows] + upd.reshape(SLAB_CHUNKS, SUBLANES, D_MODEL)
                if final:
                    xn = xn * lax.rsqrt(jnp.mean(xn * xn, axis=-1, keepdims=True) + NORM_EPS) * fg_ref[...]
                o_ref[rows] = xn


def _s5_out(g_cl, z_cl, wglu_t, w_out, x3, mod, final_g, final, chunks_per_seq, n_split=2):
    n_chunks = x3.shape[0]
    n_half = n_chunks // n_split
    once = lambda shape: pl.BlockSpec(shape, lambda t, h: (0,) * len(shape), pipeline_mode=pl.Buffered(1))
    act = pl.BlockSpec((SUBLANES, SSM_WIDTH, n_chunks), lambda t, h: (t, 0, 0))
    row = pl.BlockSpec((n_half, SUBLANES, D_MODEL), lambda t, h: (h, t, 0))
    return pl.pallas_call(
        functools.partial(_s5_out_kernel, final=final, chunks_per_seq=chunks_per_seq),
        out_shape=jax.ShapeDtypeStruct(x3.shape, F32),
        grid=(CHUNK // SUBLANES, n_split),
        in_specs=[act, act, once((SSM_WIDTH, SSM_WIDTH)), once((SSM_WIDTH, D_MODEL)), once((SLAB, SLAB)),
                  row, once(mod.shape), once((1, D_MODEL))],
        out_specs=row,
        scratch_shapes=[pltpu.VMEM((SUBLANES, n_chunks, SSM_WIDTH), BF16)],
        compiler_params=pltpu.CompilerParams(
            dimension_semantics=("parallel", "arbitrary"), vmem_limit_bytes=VMEM_LIMIT_BIG),
        name="s5_out",
    )(g_cl, z_cl, wglu_t, w_out, jnp.asarray(_slab_perm().T, BF16), x3, mod, final_g)


def _rope_tables(s):
    inv_freq = ROPE_THETA ** (-jnp.arange(0, HEAD_DIM, 2, dtype=F32) / HEAD_DIM)
    ang = jnp.arange(s, dtype=F32)[:, None] * inv_freq[None, :]
    reps = LANES // (HEAD_DIM // 2)
    sign = jnp.where(jnp.arange(LANES) < LANES // 2, -1.0, 1.0).astype(F32)
    cos, sin = jnp.tile(jnp.cos(ang), (1, reps)), jnp.tile(jnp.sin(ang), (1, reps)) * sign
    stream_major = lambda a: a.reshape(s // STREAMS, STREAMS, LANES).transpose(1, 0, 2)
    return cos, sin, stream_major(cos), stream_major(sin)


def _prep_attn_w_in(w):
    d_in = w.shape[0]
    w = w.astype(BF16).reshape(d_in, -1, ATTN_HEADS // 2, 2, 2, HEAD_DIM // 2)
    w = w.transpose(0, 1, 2, 4, 3, 5)
    return w[:, np.array(ATTN_IN_BLOCK_ORDER)].reshape(d_in, -1)


def _prep_attn_w_out(w):
    w = w.astype(BF16).reshape(ATTN_HEADS // 2, 2, 2, HEAD_DIM // 2, w.shape[1])
    return w.transpose(0, 2, 1, 3, 4).reshape(ATTN_WIDTH, -1)


def _trunk(x, ada, norm_g, attn_w, s5_w, s5_ops, final_norm_g):
    bsz, s, _ = x.shape
    chunks_per_seq = s // CHUNK
    n_chunks = bsz * chunks_per_seq
    tables = _rope_tables(s)
    fg = final_norm_g.reshape(1, D_MODEL)
    for i in range(DEPTH):
        mod = ada[i].reshape(bsz, 1, 3 * D_MODEL)
        g = norm_g[i].reshape(1, D_MODEL)
        j = i // N_MIXERS
        if i % N_MIXERS == 0:
            w_in, w_out = attn_w[j]
            nat, qkv12 = _attn_in(x, mod, g, w_in, tables)
            x = _attn_out(_attention(nat, qkv12), nat, x, mod, w_out)
        else:
            w_in_t, wglu_t, w_out = s5_w[j]
            toep, mb, mc, al = s5_ops[j]
            x3 = x.reshape(n_chunks, CHUNK, D_MODEL)
            u_cl, z_cl = _s5_in(x3, ada[i], g, w_in_t, chunks_per_seq)
            g_cl = _s5_core(u_cl, toep, mb, mc, al, chunks_per_seq)
            x3 = _s5_out(g_cl, z_cl, wglu_t, w_out, x3, ada[i], fg, i == DEPTH - 1, chunks_per_seq)
            x = x3.reshape(bsz, s, D_MODEL)
    return x


def kernel(x_prompt, x_sample, c_prompt, c_sample, norm_g, ada_w, ada_b, attn_w_in, attn_w_out,
           ssm_w_in, ssm_lam_re, ssm_lam_im, ssm_log_dt, ssm_b_re, ssm_b_im, ssm_c_re, ssm_c_im,
           ssm_d, ssm_w_glu, ssm_w_out, final_norm_g):
    assert (DEPTH - 1) % N_MIXERS == 1, "the final norm is fused into the last S5 layer"
    assert all(w // (2 * d) == RADIUS for w, d in DILATED_PAIRS)
    assert [d for _, d in DILATED_PAIRS] == [1, 4, STREAMS]
    n_prompt = c_prompt.shape[0]
    ada = _ada(jnp.concatenate([c_prompt, c_sample], axis=0), ada_w, ada_b)
    attn_w = [(_prep_attn_w_in(attn_w_in[j]), _prep_attn_w_out(attn_w_out[j]))
              for j in range(attn_w_in.shape[0])]
    s5_w, s5_ops = [], []
    for j in range(ssm_w_in.shape[0]):
        s5_w.append((ssm_w_in[j].T.astype(BF16), ssm_w_glu[j].T.astype(BF16), ssm_w_out[j].astype(BF16)))
        s5_ops.append(_s5_prep(ssm_lam_re[j], ssm_lam_im[j], ssm_log_dt[j], ssm_b_re[j],
                               ssm_b_im[j], ssm_c_re[j], ssm_c_im[j], ssm_d[j]))
    y_prompt = _trunk(x_prompt, ada[:, :n_prompt], norm_g, attn_w, s5_w, s5_ops, final_norm_g)
    y_sample = _trunk(x_sample, ada[:, n_prompt:], norm_g, attn_w, s5_w, s5_ops, final_norm_g)
    return (y_prompt, y_sample)
```

```python
import functools
import math

import numpy as np
import jax
import jax.numpy as jnp
from jax import lax
from jax.experimental import pallas as pl
from jax.experimental.pallas import tpu as pltpu

D_MODEL = 1024
DEPTH = 4
N_MIXERS = 2
ATTN_HEADS = 16
HEAD_DIM = 64
ATTN_WIDTH = ATTN_HEADS * HEAD_DIM
DILATED_PAIRS = ((128, 1), (512, 4), (2048, 16))
N_DIL = len(DILATED_PAIRS)
ROPE_THETA = 10000.0
SSM_WIDTH = D_MODEL
GROUP_CH = 16
SSM_GROUPS = SSM_WIDTH // GROUP_CH
STATE = 64
LAMBDA_RE_MAX = -1e-4
NORM_EPS = 1e-6
NEG_INF = -1e30

LANES = 128
SUBLANES = 8
CHUNK = 64
CHUNK_ROWS = CHUNK * GROUP_CH
T_PER_VREG = LANES // GROUP_CH
V_LANES = 2 * CHUNK_ROWS + LANES
STREAMS = 16
RADIUS = 64
ATTN_TQ = 128
ATTN_SUB = 4
STAT_PITCH = 40
ATTN_IN_BLOCK_ORDER = (1, 2, 0, 9, 4, 5, 7, 8, 3, 6)
PERM_ROWS = 512
UNPERM_ROWS = 256
LOG2_E = math.log2(math.e)
S5_GROUPS_PER_STEP = 2
S5_OUT_K = 4
SLAB = 256
SLAB_CHUNKS = SLAB // SUBLANES
VMEM_LIMIT = 48 * 1024 * 1024
VMEM_LIMIT_BIG = 56 * 1024 * 1024

F32 = jnp.float32
BF16 = jnp.bfloat16
HI = lax.Precision.HIGHEST


def _sigmoid(v):
    return 0.5 * jnp.tanh(0.5 * v) + 0.5


def _silu(v):
    return v * _sigmoid(v)


def _rms_modulate(x, g, scale, shift):
    rs = lax.rsqrt(jnp.mean(x * x, axis=-1, keepdims=True) + NORM_EPS)
    return (x * rs * g) * (1.0 + scale) + shift


def _ada_kernel(c_ref, w_ref, b_ref, o_ref):
    o_ref[...] = jnp.dot(_silu(c_ref[...]), w_ref[...], precision=HI,
                         preferred_element_type=F32) + b_ref[...]


def _ada(c, ada_w, ada_b):
    nb = c.shape[0]
    return pl.pallas_call(
        _ada_kernel,
        out_shape=jax.ShapeDtypeStruct((DEPTH, nb, 3 * D_MODEL), F32),
        grid=(DEPTH, 3),
        in_specs=[
            pl.BlockSpec((nb, D_MODEL), lambda i, j: (0, 0)),
            pl.BlockSpec((None, D_MODEL, D_MODEL), lambda i, j: (i, 0, j)),
            pl.BlockSpec((None, 1, D_MODEL), lambda i, j: (i, 0, j)),
        ],
        out_specs=pl.BlockSpec((None, nb, D_MODEL), lambda i, j: (i, 0, j)),
        name="ada",
    )(c, ada_w, ada_b.reshape(DEPTH, 1, 3 * D_MODEL))


def _attn_in_kernel(x_ref, shift_ref, scale_ref, g_ref, w_ref, cos_ref, sin_ref, cosp_ref, sinp_ref,
                    perm_ref, nat_ref, sm_ref, h_nat, h_perm):
    j = pl.program_id(2)
    tm = x_ref.shape[0]
    n_sub = tm // PERM_ROWS
    sub_rows = PERM_ROWS // STREAMS

    @pl.when(j == 0)
    def _():
        h = _rms_modulate(x_ref[...], g_ref[...], scale_ref[...], shift_ref[...])
        h_nat[...] = h.astype(BF16)
        for u in range(n_sub):
            rs = slice(u * PERM_ROWS, (u + 1) * PERM_ROWS)
            h_perm[rs, :] = jnp.dot(perm_ref[...], h_nat[rs, :], preferred_element_type=F32).astype(BF16)

    def rope_blocks(acc, cos, sin):
        for half in range(2):
            is_q = jnp.logical_or(j == 4, jnp.logical_and(j == 1, half == 0))
            plain = (j != 4) if half == 1 else (j < 0)
            qscale = jnp.where(is_q, HEAD_DIM ** -0.5 * LOG2_E, 1.0).astype(F32)
            c = jnp.where(plain, 1.0, cos * qscale)
            sn = jnp.where(plain, 0.0, sin * qscale)
            for b in range(ATTN_WIDTH // LANES):
                sl = slice(half * ATTN_WIDTH + b * LANES, half * ATTN_WIDTH + (b + 1) * LANES)
                t = acc[:, sl]
                yield sl, (t * c + pltpu.roll(t, LANES // 2, 1) * sn).astype(BF16)

    @pl.when(j < 2)
    def _():
        acc = jnp.dot(h_nat[...], w_ref[...], preferred_element_type=F32)
        for sl, blk in rope_blocks(acc, cos_ref[...], sin_ref[...]):
            nat_ref[:, sl] = blk

    @pl.when(j >= 2)
    def _():
        acc = jnp.dot(h_perm[...], w_ref[...], preferred_element_type=F32)

        def table(ref):
            return jnp.concatenate([ref[:, u * sub_rows:(u + 1) * sub_rows, :].reshape(PERM_ROWS, LANES)
                                    for u in range(n_sub)], axis=0)

        for sl, blk in rope_blocks(acc, table(cosp_ref), table(sinp_ref)):
            for u in range(n_sub):
                sm_ref[:, u * sub_rows:(u + 1) * sub_rows, sl] = (
                    blk[u * PERM_ROWS:(u + 1) * PERM_ROWS].reshape(STREAMS, sub_rows, LANES))


def _stream_perm(n):
    dst = np.arange(n)
    rows = n // STREAMS
    perm = np.zeros((n, n), np.float32)
    perm[dst, STREAMS * (dst % rows) + dst // rows] = 1.0
    return perm


def _attn_in(x, mod, g, w, tables, tm=1024):
    bsz, s, _ = x.shape
    cos, sin, cosp, sinp = tables
    rows = tm // STREAMS
    wide = 2 * ATTN_WIDTH
    perm = jnp.asarray(_stream_perm(PERM_ROWS), BF16)
    return pl.pallas_call(
        _attn_in_kernel,
        out_shape=(jax.ShapeDtypeStruct((bsz, s, 2 * wide), BF16),
                   jax.ShapeDtypeStruct((bsz, STREAMS, s // STREAMS, 3 * wide), BF16)),
        grid=(bsz, s // tm, 5),
        in_specs=[
            pl.BlockSpec((None, tm, D_MODEL), lambda b, i, j: (b, i, 0)),
            pl.BlockSpec((None, 1, D_MODEL), lambda b, i, j: (b, 0, 0)),
            pl.BlockSpec((None, 1, D_MODEL), lambda b, i, j: (b, 0, 1)),
            pl.BlockSpec((1, D_MODEL), lambda b, i, j: (0, 0)),
            pl.BlockSpec((D_MODEL, wide), lambda b, i, j: (0, j)),
            pl.BlockSpec((tm, LANES), lambda b, i, j: (i, 0)),
            pl.BlockSpec((tm, LANES), lambda b, i, j: (i, 0)),
            pl.BlockSpec((STREAMS, rows, LANES), lambda b, i, j: (0, i, 0)),
            pl.BlockSpec((STREAMS, rows, LANES), lambda b, i, j: (0, i, 0)),
            pl.BlockSpec((PERM_ROWS, PERM_ROWS), lambda b, i, j: (0, 0)),
        ],
        out_specs=(
            pl.BlockSpec((None, tm, wide), lambda b, i, j: (b, i, jnp.clip(j, 0, 1))),
            pl.BlockSpec((None, STREAMS, rows, wide), lambda b, i, j: (b, 0, i, jnp.clip(j - 2, 0, 2))),
        ),
        scratch_shapes=[pltpu.VMEM((tm, D_MODEL), BF16), pltpu.VMEM((tm, D_MODEL), BF16)],
        compiler_params=pltpu.CompilerParams(
            dimension_semantics=("parallel", "parallel", "arbitrary"),
            vmem_limit_bytes=VMEM_LIMIT_BIG),
        name="attn_in",
    )(x, mod, mod, g, w, cos, sin, cosp, sinp, perm)


def _lane_of_odd_head(lane):
    return (lane // (HEAD_DIM // 2)) % 2 == 1


def _mask_bias(valid):
    return jnp.where(valid, 0.0, NEG_INF).astype(F32)


def _attn_pairs(get_q, get_k, get_v, bias, put_acc):
    tq = bias.shape[0]
    lane = lax.broadcasted_iota(jnp.int32, (tq, LANES), 1)
    q_is_odd = _lane_of_odd_head(lane)
    mx_tile = jnp.zeros((tq, LANES), F32)
    den_tile = jnp.ones((tq, LANES), F32)
    ones = jnp.ones((bias.shape[1], LANES), BF16)
    bias2 = jnp.concatenate([bias, bias], axis=0)
    for b in range(ATTN_WIDTH // LANES):
        sl = slice(b * LANES, (b + 1) * LANES)
        qb = get_q(sl).astype(F32)
        kb = get_k(sl)
        vb = jnp.concatenate([get_v(sl), ones], axis=1)
        qm = jnp.concatenate([jnp.where(q_is_odd, 0.0, qb), jnp.where(q_is_odd, qb, 0.0)], axis=0).astype(BF16)
        sc = lax.dot_general(qm, kb, (((1,), (1,)), ((), ())), preferred_element_type=F32) + bias2
        mx = jnp.max(sc, axis=1, keepdims=True)
        p = jnp.exp2((sc - mx).astype(BF16))
        acc = jnp.dot(p, vb, preferred_element_type=F32)
        for odd in (0, 1):
            rows = slice(odd * tq, (odd + 1) * tq)
            mx_tile = jnp.where(lane == 2 * b + odd, mx[rows], mx_tile)
            den_tile = jnp.where(lane == 2 * b + odd, acc[rows, LANES:], den_tile)
        put_acc(sl, jnp.where(q_is_odd, acc[tq:, :LANES], acc[:tq, :LANES]))
    return mx_tile, den_tile


def _pair_factor(tile, b):
    lane = lax.broadcasted_iota(jnp.int32, tile.shape, 1)
    head = 2 * b + _lane_of_odd_head(lane).astype(jnp.int32)
    return jnp.take_along_axis(tile, head, axis=1, mode="promise_in_bounds")


def _merge_groups(stats, get_accs, put_o):
    top = functools.reduce(jnp.maximum, [mx for mx, _ in stats])
    ws = [jnp.exp2(mx - top) for mx, _ in stats]
    inv = 1.0 / sum(w * den for w, (_, den) in zip(ws, stats))
    fs = [w * inv for w in ws]
    for b in range(ATTN_WIDTH // LANES):
        sl = slice(b * LANES, (b + 1) * LANES)
        put_o(sl, sum(_pair_factor(f, b) * get_acc(sl) for f, get_acc in zip(fs, get_accs)))


def _window_mask(tile, tq, m):
    nk = tq + 2 * RADIUS
    row = lax.broadcasted_iota(jnp.int32, (tq, nk), 0)
    col = lax.broadcasted_iota(jnp.int32, (tq, nk), 1)
    kabs = tile * tq - RADIUS + col
    return _mask_bias((jnp.abs(col - RADIUS - row) <= RADIUS) & (kabs >= 0) & (kabs < m))


def _stack_rows(read, prev_ref, cur_ref, next_ref, lo, cnt):
    halo, cur_n = prev_ref.shape[-2], cur_ref.shape[-2]
    a, b = lo, lo + cnt
    parts = []
    if a < 0:
        parts.append(read(prev_ref, halo + a, halo))
        a = 0
    parts.append(read(cur_ref, a, min(b, cur_n)))
    if b > cur_n:
        parts.append(read(next_ref, 0, b - cur_n))
    return parts


def _kv_getters(stack):
    get_k = lambda sl: stack(sl)
    get_v = lambda sl: stack(slice(ATTN_WIDTH + sl.start, ATTN_WIDTH + sl.stop))
    return get_k, get_v


def _attn16_kernel(q_ref, kvp_ref, kvc_ref, kvn_ref, o_ref, mx_ref, den_ref, *, m):
    tq = ATTN_TQ
    n_sub = q_ref.shape[0] // tq
    for t in range(n_sub):
        rows = slice(t * tq, (t + 1) * tq)
        bias = _window_mask(pl.program_id(2) * n_sub + t, tq, m)

        def stack(cols):
            return jnp.concatenate(_stack_rows(lambda ref, a, b: ref[a:b, cols], kvp_ref, kvc_ref, kvn_ref,
                                               t * tq - RADIUS, tq + 2 * RADIUS), axis=0)

        def put_acc(sl, val):
            o_ref[rows, sl] = val.astype(BF16)

        mx_ref[rows], den_ref[rows] = _attn_pairs(lambda sl: q_ref[rows, sl], *_kv_getters(stack), bias, put_acc)


def _attn4_kernel(q_ref, kvp_ref, kvc_ref, kvn_ref, o_ref, mx_ref, den_ref, *, m):
    ns, halo = q_ref.shape[0], kvp_ref.shape[1]
    rows = ATTN_TQ // ns
    span = rows + 2 * halo
    n_sub = q_ref.shape[1] // rows
    n = lax.broadcasted_iota(jnp.int32, (ns * rows, ns * span), 0)
    c = lax.broadcasted_iota(jnp.int32, (ns * rows, ns * span), 1)
    for t in range(n_sub):
        base = (pl.program_id(2) * n_sub + t) * rows
        rs = slice(t * rows, (t + 1) * rows)
        krow = base - halo + c % span
        rel = ns * (krow - (base + n % rows)) + (c // span - n // rows)
        bias = _mask_bias((jnp.abs(rel) <= RADIUS) & (krow >= 0) & (krow < m))

        def stack(cols):
            parts = []
            for s in range(ns):
                parts += _stack_rows(lambda ref, a, b: ref[s, a:b, cols], kvp_ref, kvc_ref, kvn_ref,
                                     t * rows - halo, span)
            return jnp.concatenate(parts, axis=0)

        flat = lambda ref: (lambda sl: ref[:, rs, sl].reshape(ns * rows, LANES))

        def put_acc(sl, val):
            o_ref[:, rs, sl] = val.astype(BF16).reshape(ns, rows, LANES)

        mx, den = _attn_pairs(flat(q_ref), *_kv_getters(stack), bias, put_acc)
        mx_ref[:, rs, :] = mx.reshape(ns, rows, LANES)
        den_ref[:, rs, :] = den.reshape(ns, rows, LANES)


def _attn1_kernel(q_ref, kvp_ref, kvc_ref, kvn_ref, o_ref, mx_ref, den_ref, *, m):
    tq = ATTN_TQ
    n_sub = q_ref.shape[0] // tq
    for t in range(n_sub):
        rows = slice(t * tq, (t + 1) * tq)
        bias = _window_mask(pl.program_id(1) * n_sub + t, tq, m)

        def stack(cols):
            return jnp.concatenate(_stack_rows(lambda ref, a, b: ref[a:b, cols], kvp_ref, kvc_ref, kvn_ref,
                                               t * tq - RADIUS, tq + 2 * RADIUS), axis=0)

        def put_acc(sl, val):
            o_ref[rows, sl] = val.astype(BF16)

        mx_ref[rows], den_ref[rows] = _attn_pairs(lambda sl: q_ref[rows, sl], *_kv_getters(stack), bias, put_acc)


def _attention(qkv0, qkv12):
    bsz, s, _ = qkv0.shape
    m16 = s // STREAMS
    tq, half = ATTN_TQ, RADIUS
    cur_rows = min(ATTN_SUB * tq, m16)
    params = lambda n: pltpu.CompilerParams(dimension_semantics=("parallel",) * n,
                                            vmem_limit_bytes=VMEM_LIMIT)
    same = lambda i: i

    def halo_idx(cur_n, halo_n, total):
        per = cur_n // halo_n
        return (lambda i: jnp.maximum(i * per - 1, 0)), (lambda i: jnp.minimum((i + 1) * per, total // halo_n - 1))

    def spec16(nrows, row_idx, col, width=ATTN_WIDTH):
        return pl.BlockSpec((None, None, nrows, width), lambda b, r, i: (b, r, row_idx(i), col))
    prv, nxt = halo_idx(cur_rows, half, m16)
    stat16 = jax.ShapeDtypeStruct((bsz, STREAMS, m16, LANES), F32)
    o2, mx2, den2 = pl.pallas_call(
        functools.partial(_attn16_kernel, m=m16),
        out_shape=(jax.ShapeDtypeStruct((bsz, STREAMS, m16, ATTN_WIDTH), BF16), stat16, stat16),
        grid=(bsz, STREAMS, m16 // cur_rows),
        in_specs=[spec16(cur_rows, same, 5), spec16(half, prv, 1, 2 * ATTN_WIDTH),
                  spec16(cur_rows, same, 1, 2 * ATTN_WIDTH), spec16(half, nxt, 1, 2 * ATTN_WIDTH)],
        out_specs=(spec16(cur_rows, same, 0), spec16(cur_rows, same, 0, LANES), spec16(cur_rows, same, 0, LANES)),
        compiler_params=params(3), name="attn_dil16",
    )(*([qkv12] * 4))

    ns = STREAMS // 4
    rows4, halo4 = cur_rows // ns, half // ns

    def view4(a):
        return a.reshape(bsz, ns, 4, m16, a.shape[-1])

    def spec4(nrows, row_idx, col, width=ATTN_WIDTH):
        return pl.BlockSpec((None, ns, None, nrows, width), lambda b, r, i: (b, 0, r, row_idx(i), col))
    prv4, nxt4 = halo_idx(rows4, halo4, m16)
    q4 = view4(qkv12)
    stat4 = jax.ShapeDtypeStruct((bsz, ns, 4, m16, LANES), F32)
    o1, mx1, den1 = pl.pallas_call(
        functools.partial(_attn4_kernel, m=m16),
        out_shape=(jax.ShapeDtypeStruct((bsz, ns, 4, m16, ATTN_WIDTH), BF16), stat4, stat4),
        grid=(bsz, 4, m16 // rows4),
        in_specs=[spec4(rows4, same, 4), spec4(halo4, prv4, 0, 2 * ATTN_WIDTH),
                  spec4(rows4, same, 0, 2 * ATTN_WIDTH), spec4(halo4, nxt4, 0, 2 * ATTN_WIDTH)],
        out_specs=(spec4(rows4, same, 0), spec4(rows4, same, 0, LANES), spec4(rows4, same, 0, LANES)),
        compiler_params=params(3), name="attn_dil4",
    )(q4, q4, q4, q4)
    streams = lambda a: a.reshape(bsz, STREAMS, m16, a.shape[-1])

    cur1 = ATTN_SUB * tq

    def spec1(nrows, row_idx, col, width=ATTN_WIDTH):
        return pl.BlockSpec((None, nrows, width), lambda b, i: (b, row_idx(i), col))
    prv1, nxt1 = halo_idx(cur1, half, s)
    stat1 = jax.ShapeDtypeStruct((bsz, s, LANES), F32)
    group0 = pl.pallas_call(
        functools.partial(_attn1_kernel, m=s),
        out_shape=(jax.ShapeDtypeStruct((bsz, s, ATTN_WIDTH), BF16), stat1, stat1),
        grid=(bsz, s // cur1),
        in_specs=[spec1(cur1, same, 2), spec1(half, prv1, 0, 2 * ATTN_WIDTH),
                  spec1(cur1, same, 0, 2 * ATTN_WIDTH), spec1(half, nxt1, 0, 2 * ATTN_WIDTH)],
        out_specs=(spec1(cur1, same, 0), spec1(cur1, same, 0, LANES), spec1(cur1, same, 0, LANES)),
        compiler_params=params(2), name="attn_dil1",
    )(*([qkv0] * 4))
    return group0, (o2, mx2, den2), (streams(o1), streams(mx1), streams(den1))


def _attn_out_kernel(o0_ref, mx0_ref, den0_ref, o2_ref, mx2_ref, den2_ref, o1_ref, mx1_ref, den1_ref,
                     unperm_ref, z_ref, x_ref, gate_ref, w_ref, out_ref, y_scr, a2_scr, a1_scr, stat_scr):
    tm = x_ref.shape[0]
    tq = ATTN_TQ
    srow, n_srow = tq // STREAMS, mx2_ref.shape[1]
    n_un = unperm_ref.shape[0]
    for src, dst in ((o2_ref, a2_scr), (o1_ref, a1_scr)):
        for u in range(tm // n_un):
            rs = slice(u * n_un // STREAMS, (u + 1) * n_un // STREAMS)
            part = jnp.concatenate([src[r, rs, :] for r in range(STREAMS)], axis=0)
            dst[u * n_un:(u + 1) * n_un, :] = jnp.dot(unperm_ref[...], part, preferred_element_type=F32)
    for k, src in enumerate((mx2_ref, den2_ref, mx1_ref, den1_ref)):
        for r in range(STREAMS):
            stat_scr[k, r * STAT_PITCH:r * STAT_PITCH + n_srow, :] = src[r]
    for t in range(tm // tq):
        rows = slice(t * tq, (t + 1) * tq)

        def natural(k):
            return jnp.concatenate([stat_scr[k, pl.ds(t * srow + ii, STREAMS, stride=STAT_PITCH), :]
                                    for ii in range(srow)], axis=0)

        def put_y(sl, val):
            y_scr[rows, sl] = (val * _silu(z_ref[rows, sl].astype(F32))).astype(BF16)

        stats = [(mx0_ref[rows], den0_ref[rows]), (natural(0), natural(1)), (natural(2), natural(3))]
        _merge_groups(stats, [lambda sl: o0_ref[rows, sl].astype(F32), lambda sl: a2_scr[rows, sl],
                              lambda sl: a1_scr[rows, sl]], put_y)
    out = jnp.dot(y_scr[...], w_ref[...], preferred_element_type=F32)
    out_ref[...] = x_ref[...] + gate_ref[...] * out


def _attn_out(groups, nat, x, mod, w, tm=PERM_ROWS):
    bsz, s, _ = x.shape
    row = lambda width: pl.BlockSpec((None, tm, width), lambda b, i: (b, i, 0))
    sm = lambda width: pl.BlockSpec((None, STREAMS, tm // STREAMS, width), lambda b, i: (b, 0, i, 0))
    z_col = nat.shape[-1] // ATTN_WIDTH - 1
    unperm = jnp.asarray(_stream_perm(UNPERM_ROWS).T, BF16)
    return pl.pallas_call(
        _attn_out_kernel,
        out_shape=jax.ShapeDtypeStruct(x.shape, F32),
        grid=(bsz, s // tm),
        in_specs=[row(ATTN_WIDTH), row(LANES), row(LANES)] + [sm(ATTN_WIDTH), sm(LANES), sm(LANES)] * 2
                 + [pl.BlockSpec((UNPERM_ROWS, UNPERM_ROWS), lambda b, i: (0, 0)),
                    pl.BlockSpec((None, tm, ATTN_WIDTH), lambda b, i: (b, i, z_col)), row(D_MODEL),
                    pl.BlockSpec((None, 1, D_MODEL), lambda b, i: (b, 0, 2)),
                    pl.BlockSpec((ATTN_WIDTH, D_MODEL), lambda b, i: (0, 0))],
        out_specs=row(D_MODEL),
        scratch_shapes=[pltpu.VMEM((tm, ATTN_WIDTH), BF16), pltpu.VMEM((tm, ATTN_WIDTH), F32),
                        pltpu.VMEM((tm, ATTN_WIDTH), F32), pltpu.VMEM((4, STREAMS * STAT_PITCH, LANES), F32)],
        compiler_params=pltpu.CompilerParams(
            dimension_semantics=("parallel", "parallel"),
            vmem_limit_bytes=VMEM_LIMIT),
        name="attn_out",
    )(*groups[0], *groups[1], *groups[2], unperm, nat, x, mod, w)


def _cmul(ar, ai, br, bi):
    return ar * br - ai * bi, ar * bi + ai * br


def _s5_prep_kernel(lam_re_a, lam_im_a, lam_re_b, lam_im_b, logdt, bt_re, bt_im,
                    c_re, c_im, d_diag, toep_ref, mb_ref, mc_ref, al_ref, v_scr):
    lane = lax.broadcasted_iota(jnp.int32, (STATE, LANES), 1)
    a_idx = lane // GROUP_CH
    n_tiles = CHUNK // T_PER_VREG

    mb_rows = []
    v_rhs = []
    v_lhs = []
    mc_cols = []
    zbar_f = None
    for dr in range(2):
        dt = jnp.exp(logdt[dr])
        lr = jnp.minimum(lam_re_a[dr], LAMBDA_RE_MAX)
        li = lam_im_a[dr]
        ldr, ldi = lr * dt, li * dt

        mag = jnp.exp(ldr)
        l1r, l1i = mag * jnp.cos(ldi), mag * jnp.sin(ldi)
        small = [(jnp.ones_like(l1r), jnp.zeros_like(l1r)), (l1r, l1i)]
        for _ in range(T_PER_VREG - 1):
            small.append(_cmul(*small[-1], l1r, l1i))

        def lane_powers(exponent_of_step):
            pr, pi = small[exponent_of_step[0]]
            pr, pi = jnp.broadcast_to(pr, (STATE, LANES)), jnp.broadcast_to(pi, (STATE, LANES))
            for a in range(1, T_PER_VREG):
                er, ei = small[exponent_of_step[a]]
                pr, pi = jnp.where(a_idx == a, er, pr), jnp.where(a_idx == a, ei, pi)
            return pr, pi

        den = lr * lr + li * li
        nr, ni = l1r - 1.0, l1i
        cr, ci = (nr * lr + ni * li) / den, (ni * lr - nr * li) / den
        bbr, bbi = _cmul(cr, ci, bt_re[dr], bt_im[dr])
        l8r, l8i = small[T_PER_VREG]
        tiles = [None] * n_tiles
        if dr == 0:
            pr, pi = lane_powers([T_PER_VREG - 1 - a for a in range(T_PER_VREG)])
            cur = _cmul(pr, pi, bbr, bbi)
            for jt in range(n_tiles - 1, -1, -1):
                tiles[jt] = cur
                if jt:
                    cur = _cmul(cur[0], cur[1], l8r, l8i)
            v_rhs.append(_cmul(tiles[n_tiles - 1][0], tiles[n_tiles - 1][1], l1r, l1i))
            zbar_f = (jnp.where(lane < GROUP_CH, bbr, 0.0), jnp.where(lane < GROUP_CH, bbi, 0.0))
        else:
            pr, pi = lane_powers(list(range(T_PER_VREG)))
            cur = _cmul(pr, pi, bbr, bbi)
            for jt in range(n_tiles):
                tiles[jt] = cur
                if jt < n_tiles - 1:
                    cur = _cmul(cur[0], cur[1], l8r, l8i)
            v_rhs.append(tiles[0])
        mb_rows.append(jnp.concatenate([t[0] for t in tiles], axis=1))
        mb_rows.append(jnp.concatenate([t[1] for t in tiles], axis=1))
        a_r, a_i = l8r, l8i
        for _ in range(int(math.log2(CHUNK // T_PER_VREG))):
            a_r, a_i = _cmul(a_r, a_i, a_r, a_i)
        al_ref[2 * dr] = jnp.broadcast_to(a_r, (STATE, LANES))
        al_ref[2 * dr + 1] = jnp.broadcast_to(a_i, (STATE, LANES))

        lrb = jnp.minimum(lam_re_b[dr], LAMBDA_RE_MAX)
        lib = lam_im_b[dr]
        mag = jnp.exp(lrb * dt)
        pw_r, pw_i = mag * jnp.cos(lib * dt), mag * jnp.sin(lib * dt)
        pows = [(pw_r, pw_i)]
        for _ in range(int(math.log2(CHUNK)) - 1):
            pows.append(_cmul(*pows[-1], *pows[-1]))
        cre, cim = c_re[dr], c_im[dr]
        xr, xi = _cmul(cre, cim, *pows[0])
        for k in range(int(math.log2(CHUNK))):
            yr, yi = _cmul(xr, xi, *pows[k])
            if dr == 0:
                xr, xi = jnp.concatenate([xr, yr], 0), jnp.concatenate([xi, yi], 0)
            else:
                xr, xi = jnp.concatenate([yr, xr], 0), jnp.concatenate([yi, xi], 0)
        mc_cols += [xr, -xi]
        xr, xi = cre, cim
        for k in range(int(math.log2(n_tiles))):
            yr, yi = _cmul(xr, xi, *pows[k + int(math.log2(T_PER_VREG))])
            if dr == 0:
                xr, xi = jnp.concatenate([yr, xr], 0), jnp.concatenate([yi, xi], 0)
            else:
                xr, xi = jnp.concatenate([xr, yr], 0), jnp.concatenate([xi, yi], 0)
        v_lhs.append((xr, xi))

    mb_ref[...] = jnp.concatenate(mb_rows, axis=0).astype(BF16)
    mc_ref[...] = jnp.concatenate(mc_cols, axis=1).astype(BF16)

    def cdot(lhs, rhs):
        return (jnp.dot(lhs[0], rhs[0], precision=HI, preferred_element_type=F32)
                - jnp.dot(lhs[1], rhs[1], precision=HI, preferred_element_type=F32))

    out_f = cdot(v_lhs[0], v_rhs[0])
    zero_rows = jnp.zeros((CHUNK_ROWS // T_PER_VREG - GROUP_CH, STATE), F32)
    lag0_lhs = (jnp.concatenate([c_re[0], zero_rows], 0), jnp.concatenate([c_im[0], zero_rows], 0))
    out_b = cdot(v_lhs[1], v_rhs[1]) + cdot(lag0_lhs, zbar_f)
    out_b = out_b + jnp.concatenate([d_diag[...], jnp.zeros((CHUNK_ROWS // T_PER_VREG - GROUP_CH, LANES), F32)], 0)
    for jt in range(n_tiles):
        rows = slice(jt * GROUP_CH, (jt + 1) * GROUP_CH)
        v_scr[:, jt * LANES:(jt + 1) * LANES] = out_f[rows]
        v_scr[:, (n_tiles + jt) * LANES:(n_tiles + jt + 1) * LANES] = out_b[rows]
    v_scr[:, 2 * CHUNK_ROWS:] = jnp.zeros((GROUP_CH, LANES), F32)

    for k in range(T_PER_VREG):
        vk = v_scr[:, k * GROUP_CH:k * GROUP_CH + 2 * CHUNK_ROWS]
        for t_out in range(CHUNK):
            off = (CHUNK - t_out) * GROUP_CH
            if off % LANES == k * GROUP_CH:
                al_off = off - k * GROUP_CH
                toep_ref[t_out * GROUP_CH:(t_out + 1) * GROUP_CH, :] = (
                    vk[:, al_off:al_off + CHUNK_ROWS].astype(BF16))


def _s5_prep(lam_re, lam_im, log_dt, b_re, b_im, c_re, c_im, d_skip):
    g = SSM_GROUPS
    d_diag = d_skip.reshape(g, GROUP_CH, 1) * jnp.eye(GROUP_CH, LANES, dtype=F32)
    lam_a = lambda a: a.reshape(2, g, STATE, 1)
    lam_b = lambda a: a.reshape(2, g, 1, STATE)
    tile_b = lambda a: jnp.tile(a, (1, 1, 1, T_PER_VREG))
    dir_spec = lambda r, c: pl.BlockSpec((2, None, r, c), lambda i: (0, i, 0, 0))
    return pl.pallas_call(
        _s5_prep_kernel,
        out_shape=(jax.ShapeDtypeStruct((g, CHUNK_ROWS, CHUNK_ROWS), BF16),
                   jax.ShapeDtypeStruct((g, 4 * STATE, CHUNK_ROWS), BF16),
                   jax.ShapeDtypeStruct((g, CHUNK_ROWS, 4 * STATE), BF16),
                   jax.ShapeDtypeStruct((g, 4, STATE, LANES), F32)),
        grid=(g,),
        in_specs=[dir_spec(STATE, 1), dir_spec(STATE, 1), dir_spec(1, STATE), dir_spec(1, STATE),
                  dir_spec(1, 1), dir_spec(STATE, LANES), dir_spec(STATE, LANES),
                  dir_spec(GROUP_CH, STATE), dir_spec(GROUP_CH, STATE),
                  pl.BlockSpec((None, GROUP_CH, LANES), lambda i: (i, 0, 0))],
        out_specs=(pl.BlockSpec((None, CHUNK_ROWS, CHUNK_ROWS), lambda i: (i, 0, 0)),
                   pl.BlockSpec((None, 4 * STATE, CHUNK_ROWS), lambda i: (i, 0, 0)),
                   pl.BlockSpec((None, CHUNK_ROWS, 4 * STATE), lambda i: (i, 0, 0)),
                   pl.BlockSpec((None, 4, STATE, LANES), lambda i: (i, 0, 0, 0))),
        scratch_shapes=[pltpu.VMEM((GROUP_CH, V_LANES), F32)],
        compiler_params=pltpu.CompilerParams(dimension_semantics=("parallel",)),
        name="s5_prep",
    )(lam_a(lam_re), lam_a(lam_im), lam_b(lam_re), lam_b(lam_im), log_dt.reshape(2, g, 1, 1),
      tile_b(b_re), tile_b(b_im), c_re, c_im, d_diag)


def _slab_perm():
    dst = np.arange(SLAB)
    perm = np.zeros((SLAB, SLAB), np.float32)
    perm[dst, (dst % SLAB_CHUNKS) * SUBLANES + dst // SLAB_CHUNKS] = 1.0
    return perm


def _s5_in_kernel(x_ref, mod_ref, g_ref, w_ref, p_ref, u_ref, z_ref, h_scr, *, chunks_per_seq):
    n_chunks = x_ref.shape[0]
    for b in range(n_chunks // chunks_per_seq):
        rows = slice(b * chunks_per_seq, (b + 1) * chunks_per_seq)
        h = _rms_modulate(x_ref[rows], g_ref[...], mod_ref[b:b + 1, D_MODEL:2 * D_MODEL],
                          mod_ref[b:b + 1, :D_MODEL])
        h2 = h.reshape(chunks_per_seq * SUBLANES, D_MODEL).astype(BF16)
        for s in range(chunks_per_seq // SLAB_CHUNKS):
            hp = jnp.dot(p_ref[...], h2[s * SLAB:(s + 1) * SLAB], preferred_element_type=F32)
            c0 = b * chunks_per_seq + s * SLAB_CHUNKS
            for k in range(SUBLANES):
                h_scr[k, c0:c0 + SLAB_CHUNKS, :] = hp[k * SLAB_CHUNKS:(k + 1) * SLAB_CHUNKS].astype(BF16)
    for k in range(SUBLANES):
        ht = h_scr[k].astype(F32).T.astype(BF16)
        uz = jnp.dot(w_ref[...], ht, preferred_element_type=F32)
        u_ref[k] = uz[:SSM_WIDTH].astype(BF16)
        z_ref[k] = uz[SSM_WIDTH:].astype(BF16)


def _s5_in(x3, mod, g, w_t, chunks_per_seq):
    n_chunks = x3.shape[0]
    once = lambda shape: pl.BlockSpec(shape, lambda t: (0,) * len(shape), pipeline_mode=pl.Buffered(1))
    out = pl.BlockSpec((SUBLANES, SSM_WIDTH, n_chunks), lambda t: (t, 0, 0))
    return pl.pallas_call(
        functools.partial(_s5_in_kernel, chunks_per_seq=chunks_per_seq),
        out_shape=(jax.ShapeDtypeStruct((CHUNK, SSM_WIDTH, n_chunks), BF16),) * 2,
        grid=(CHUNK // SUBLANES,),
        in_specs=[pl.BlockSpec((n_chunks, SUBLANES, D_MODEL), lambda t: (0, t, 0)),
                  once(mod.shape), once((1, D_MODEL)), once((2 * SSM_WIDTH, D_MODEL)),
                  once((SLAB, SLAB))],
        out_specs=(out, out),
        scratch_shapes=[pltpu.VMEM((SUBLANES, n_chunks, D_MODEL), BF16)],
        compiler_params=pltpu.CompilerParams(
            dimension_semantics=("parallel",), vmem_limit_bytes=VMEM_LIMIT_BIG),
        name="s5_in",
    )(x3, mod, g, w_t, jnp.asarray(_slab_perm(), BF16))


def _gelu_tanh(v):
    return 0.5 * v * (1.0 + jnp.tanh(math.sqrt(2.0 / math.pi) * (v + 0.044715 * (v * v * v))))


def _s5_core_kernel(u_ref, toep_ref, mb_ref, mc_ref, al_ref, o_ref, *, chunks_per_seq):
    for gi in range(u_ref.shape[1]):
        _s5_core_group(u_ref.at[:, gi], toep_ref.at[gi], mb_ref.at[gi], mc_ref.at[gi], al_ref.at[gi],
                       o_ref.at[:, gi], chunks_per_seq)


def _s5_core_group(u_ref, toep_ref, mb_ref, mc_ref, al_ref, o_ref, chunks_per_seq):
    n_chunks = u_ref.shape[-1]
    ub = u_ref[...].reshape(CHUNK_ROWS, n_chunks)
    inc = jnp.dot(mb_ref[...], ub, preferred_element_type=F32)
    pos = lax.broadcasted_iota(jnp.int32, (STATE, n_chunks), 1) % chunks_per_seq
    reps = n_chunks // LANES

    def lane_tile(a):
        return jnp.concatenate([a] * reps, axis=1) if reps > 1 else a

    states = []
    for dr in range(2):
        xr = inc[(2 * dr) * STATE:(2 * dr + 1) * STATE]
        xi = inc[(2 * dr + 1) * STATE:(2 * dr + 2) * STATE]
        ar, ai = al_ref[2 * dr], al_ref[2 * dr + 1]

        def shifted(v, step):
            if dr == 0:
                return jnp.where(pos >= step, pltpu.roll(v, step, 1), 0.0)
            return jnp.where(pos < chunks_per_seq - step, pltpu.roll(v, n_chunks - step, 1), 0.0)

        step = 1
        while step < chunks_per_seq:
            sr, si = shifted(xr, step), shifted(xi, step)
            tr, ti = lane_tile(ar), lane_tile(ai)
            xr, xi = xr + tr * sr - ti * si, xi + tr * si + ti * sr
            ar, ai = _cmul(ar, ai, ar, ai)
            step *= 2
        states += [shifted(xr, 1), shifted(xi, 1)]
    h_in = jnp.concatenate(states, axis=0).astype(BF16)
    y = (jnp.dot(toep_ref[...], ub, preferred_element_type=F32)
         + jnp.dot(mc_ref[...], h_in, preferred_element_type=F32))
    o_ref[...] = _gelu_tanh(y).astype(BF16).reshape(CHUNK, GROUP_CH, n_chunks)


def _s5_core(u_cl, toep, mb, mc, al, chunks_per_seq):
    n_chunks = u_cl.shape[-1]
    u4 = u_cl.reshape(CHUNK, SSM_GROUPS, GROUP_CH, n_chunks)
    gs = S5_GROUPS_PER_STEP
    grp = pl.BlockSpec((CHUNK, gs, GROUP_CH, n_chunks), lambda g: (0, g, 0, 0))
    out = pl.pallas_call(
        functools.partial(_s5_core_kernel, chunks_per_seq=chunks_per_seq),
        out_shape=jax.ShapeDtypeStruct(u4.shape, BF16),
        grid=(SSM_GROUPS // gs,),
        in_specs=[grp,
                  pl.BlockSpec((gs, CHUNK_ROWS, CHUNK_ROWS), lambda g: (g, 0, 0)),
                  pl.BlockSpec((gs, 4 * STATE, CHUNK_ROWS), lambda g: (g, 0, 0)),
                  pl.BlockSpec((gs, CHUNK_ROWS, 4 * STATE), lambda g: (g, 0, 0)),
                  pl.BlockSpec((gs, 4, STATE, LANES), lambda g: (g, 0, 0, 0))],
        out_specs=grp,
        compiler_params=pltpu.CompilerParams(
            dimension_semantics=("parallel",), vmem_limit_bytes=VMEM_LIMIT),
        name="s5_core",
    )(u4, toep, mb, mc, al)
    return out.reshape(CHUNK, SSM_WIDTH, n_chunks)


def _s5_out_kernel(g_ref, z_ref, wglu_ref, wout_ref, pt_ref, x_ref, mod_ref, fg_ref, o_ref, y_scr,
                   *, final, chunks_per_seq):
    q = pl.program_id(1)
    n_half = x_ref.shape[0]
    k_per = g_ref.shape[0]
    n_glu = SUBLANES // k_per

    for qv in range(n_glu):
        @pl.when(q == qv)
        def _():
            for kk in range(k_per):
                gb = g_ref[kk]
                glu = jnp.dot(wglu_ref[...], gb, preferred_element_type=F32)
                y = gb.astype(F32) * _sigmoid(glu) * _silu(z_ref[kk].astype(F32))
                y_scr[qv * k_per + kk] = y.T.astype(BF16)

    for hv in range(y_scr.shape[1] // n_half):
        @pl.when(q == n_glu + hv)
        def _():
            for s in range(n_half // SLAB_CHUNKS):
                c0 = hv * n_half + s * SLAB_CHUNKS
                src = jnp.concatenate([y_scr[k, c0:c0 + SLAB_CHUNKS, :] for k in range(SUBLANES)], axis=0)
                yp = jnp.dot(pt_ref[...], src, preferred_element_type=F32).astype(BF16)
                out = jnp.dot(yp, wout_ref[...], preferred_element_type=F32)
                b = c0 // chunks_per_seq
                upd = mod_ref[b:b + 1, 2 * D_MODEL:] * out
                rows = slice(s * SLAB_CHUNKS, (s + 1) * SLAB_CHUNKS)
                xn = x_ref[rows] + upd.reshape(SLAB_CHUNKS, SUBLANES, D_MODEL)
                if final:
                    xn = xn * lax.rsqrt(jnp.mean(xn * xn, axis=-1, keepdims=True) + NORM_EPS) * fg_ref[...]
                o_ref[rows] = xn


def _s5_out(g_cl, z_cl, wglu_t, w_out, x3, mod, final_g, final, chunks_per_seq, n_split=2):
    n_chunks = x3.shape[0]
    n_half = n_chunks // n_split
    n_glu = SUBLANES // S5_OUT_K
    once = lambda shape: pl.BlockSpec(shape, lambda t, q: (0,) * len(shape), pipeline_mode=pl.Buffered(1))
    act = pl.BlockSpec((S5_OUT_K, SSM_WIDTH, n_chunks), lambda t, q: (t * n_glu + jnp.minimum(q, n_glu - 1), 0, 0))
    row = pl.BlockSpec((n_half, SUBLANES, D_MODEL), lambda t, q: (jnp.maximum(q - n_glu, 0), t, 0))
    return pl.pallas_call(
        functools.partial(_s5_out_kernel, final=final, chunks_per_seq=chunks_per_seq),
        out_shape=jax.ShapeDtypeStruct(x3.shape, F32),
        grid=(CHUNK // SUBLANES, n_glu + n_split),
        in_specs=[act, act, once((SSM_WIDTH, SSM_WIDTH)), once((SSM_WIDTH, D_MODEL)), once((SLAB, SLAB)),
                  row, once(mod.shape), once((1, D_MODEL))],
        out_specs=row,
        scratch_shapes=[pltpu.VMEM((SUBLANES, n_chunks, SSM_WIDTH), BF16)],
        compiler_params=pltpu.CompilerParams(
            dimension_semantics=("parallel", "arbitrary"), vmem_limit_bytes=VMEM_LIMIT_BIG),
        name="s5_out",
    )(g_cl, z_cl, wglu_t, w_out, jnp.asarray(_slab_perm().T, BF16), x3, mod, final_g)


def _rope_tables(s):
    inv_freq = ROPE_THETA ** (-jnp.arange(0, HEAD_DIM, 2, dtype=F32) / HEAD_DIM)
    ang = jnp.arange(s, dtype=F32)[:, None] * inv_freq[None, :]
    reps = LANES // (HEAD_DIM // 2)
    sign = jnp.where(jnp.arange(LANES) < LANES // 2, -1.0, 1.0).astype(F32)
    cos, sin = jnp.tile(jnp.cos(ang), (1, reps)), jnp.tile(jnp.sin(ang), (1, reps)) * sign
    stream_major = lambda a: a.reshape(s // STREAMS, STREAMS, LANES).transpose(1, 0, 2)
    return cos, sin, stream_major(cos), stream_major(sin)


def _prep_attn_w_in(w):
    d_in = w.shape[0]
    w = w.astype(BF16).reshape(d_in, -1, ATTN_HEADS // 2, 2, 2, HEAD_DIM // 2)
    w = w.transpose(0, 1, 2, 4, 3, 5)
    return w[:, np.array(ATTN_IN_BLOCK_ORDER)].reshape(d_in, -1)


def _prep_attn_w_out(w):
    w = w.astype(BF16).reshape(ATTN_HEADS // 2, 2, 2, HEAD_DIM // 2, w.shape[1])
    return w.transpose(0, 2, 1, 3, 4).reshape(ATTN_WIDTH, -1)


def _trunk(x, ada, norm_g, attn_w, s5_w, s5_ops, final_norm_g):
    bsz, s, _ = x.shape
    chunks_per_seq = s // CHUNK
    n_chunks = bsz * chunks_per_seq
    tables = _rope_tables(s)
    fg = final_norm_g.reshape(1, D_MODEL)
    for i in range(DEPTH):
        mod = ada[i].reshape(bsz, 1, 3 * D_MODEL)
        g = norm_g[i].reshape(1, D_MODEL)
        j = i // N_MIXERS
        if i % N_MIXERS == 0:
            w_in, w_out = attn_w[j]
            nat, qkv12 = _attn_in(x, mod, g, w_in, tables)
            x = _attn_out(_attention(nat, qkv12), nat, x, mod, w_out)
        else:
            w_in_t, wglu_t, w_out = s5_w[j]
            toep, mb, mc, al = s5_ops[j]
            x3 = x.reshape(n_chunks, CHUNK, D_MODEL)
            u_cl, z_cl = _s5_in(x3, ada[i], g, w_in_t, chunks_per_seq)
            g_cl = _s5_core(u_cl, toep, mb, mc, al, chunks_per_seq)
            x3 = _s5_out(g_cl, z_cl, wglu_t, w_out, x3, ada[i], fg, i == DEPTH - 1, chunks_per_seq)
            x = x3.reshape(bsz, s, D_MODEL)
    return x


def kernel(x_prompt, x_sample, c_prompt, c_sample, norm_g, ada_w, ada_b, attn_w_in, attn_w_out,
           ssm_w_in, ssm_lam_re, ssm_lam_im, ssm_log_dt, ssm_b_re, ssm_b_im, ssm_c_re, ssm_c_im,
           ssm_d, ssm_w_glu, ssm_w_out, final_norm_g):
    assert (DEPTH - 1) % N_MIXERS == 1, "the final norm is fused into the last S5 layer"
    assert all(w // (2 * d) == RADIUS for w, d in DILATED_PAIRS)
    assert [d for _, d in DILATED_PAIRS] == [1, 4, STREAMS]
    n_prompt = c_prompt.shape[0]
    ada = _ada(jnp.concatenate([c_prompt, c_sample], axis=0), ada_w, ada_b)
    attn_w = [(_prep_attn_w_in(attn_w_in[j]), _prep_attn_w_out(attn_w_out[j]))
              for j in range(attn_w_in.shape[0])]
    s5_w, s5_ops = [], []
    for j in range(ssm_w_in.shape[0]):
        s5_w.append((ssm_w_in[j].T.astype(BF16), ssm_w_glu[j].T.astype(BF16), ssm_w_out[j].astype(BF16)))
        s5_ops.append(_s5_prep(ssm_lam_re[j], ssm_lam_im[j], ssm_log_dt[j], ssm_b_re[j],
                               ssm_b_im[j], ssm_c_re[j], ssm_c_im[j], ssm_d[j]))
    y_prompt = _trunk(x_prompt, ada[:, :n_prompt], norm_g, attn_w, s5_w, s5_ops, final_norm_g)
    y_sample = _trunk(x_sample, ada[:, n_prompt:], norm_g, attn_w, s5_w, s5_ops, final_norm_g)
    return (y_prompt, y_sample)
```

```python
import functools
import math

import numpy as np
import jax
import jax.numpy as jnp
from jax import lax
from jax.experimental import pallas as pl
from jax.experimental.pallas import tpu as pltpu

D_MODEL = 1024
DEPTH = 4
N_MIXERS = 2
ATTN_HEADS = 16
HEAD_DIM = 64
ATTN_WIDTH = ATTN_HEADS * HEAD_DIM
DILATED_PAIRS = ((128, 1), (512, 4), (2048, 16))
N_DIL = len(DILATED_PAIRS)
ROPE_THETA = 10000.0
SSM_WIDTH = D_MODEL
GROUP_CH = 16
SSM_GROUPS = SSM_WIDTH // GROUP_CH
STATE = 64
LAMBDA_RE_MAX = -1e-4
NORM_EPS = 1e-6
NEG_INF = -1e30

LANES = 128
SUBLANES = 8
CHUNK = 64
CHUNK_ROWS = CHUNK * GROUP_CH
T_PER_VREG = LANES // GROUP_CH
V_LANES = 2 * CHUNK_ROWS + LANES
STREAMS = 16
RADIUS = 64
ATTN_TQ = 128
ATTN_SUB = 4
STAT_PITCH = 40
ATTN_IN_BLOCK_ORDER = (1, 2, 0, 9, 4, 5, 7, 8, 3, 6)
PERM_ROWS = 512
UNPERM_ROWS = 256
LOG2_E = math.log2(math.e)
S5_GROUPS_PER_STEP = 2
S5_OUT_K = 4
SLAB = 256
SLAB_CHUNKS = SLAB // SUBLANES
VMEM_LIMIT = 48 * 1024 * 1024
VMEM_LIMIT_BIG = 56 * 1024 * 1024

F32 = jnp.float32
BF16 = jnp.bfloat16
HI = lax.Precision.HIGHEST


def _sigmoid(v):
    return 0.5 * jnp.tanh(0.5 * v) + 0.5


def _silu(v):
    return v * _sigmoid(v)


def _rms_modulate(x, g, scale, shift):
    rs = lax.rsqrt(jnp.mean(x * x, axis=-1, keepdims=True) + NORM_EPS)
    return (x * rs * g) * (1.0 + scale) + shift


def _ada_kernel(c_ref, w_ref, b_ref, o_ref):
    o_ref[...] = jnp.dot(_silu(c_ref[...]), w_ref[...], precision=HI,
                         preferred_element_type=F32) + b_ref[...]


def _ada(c, ada_w, ada_b):
    nb = c.shape[0]
    return pl.pallas_call(
        _ada_kernel,
        out_shape=jax.ShapeDtypeStruct((DEPTH, nb, 3 * D_MODEL), F32),
        grid=(DEPTH, 3),
        in_specs=[
            pl.BlockSpec((nb, D_MODEL), lambda i, j: (0, 0)),
            pl.BlockSpec((None, D_MODEL, D_MODEL), lambda i, j: (i, 0, j)),
            pl.BlockSpec((None, 1, D_MODEL), lambda i, j: (i, 0, j)),
        ],
        out_specs=pl.BlockSpec((None, nb, D_MODEL), lambda i, j: (i, 0, j)),
        name="ada",
    )(c, ada_w, ada_b.reshape(DEPTH, 1, 3 * D_MODEL))


def _attn_in_kernel(x_ref, shift_ref, scale_ref, g_ref, w_ref, cos_ref, sin_ref, cosp_ref, sinp_ref,
                    perm_ref, nat_ref, sm_ref, h_nat, h_perm):
    j = pl.program_id(2)
    tm = x_ref.shape[0]
    n_sub = tm // PERM_ROWS
    sub_rows = PERM_ROWS // STREAMS

    @pl.when(j == 0)
    def _():
        h = _rms_modulate(x_ref[...], g_ref[...], scale_ref[...], shift_ref[...])
        h_nat[...] = h.astype(BF16)
        for u in range(n_sub):
            rs = slice(u * PERM_ROWS, (u + 1) * PERM_ROWS)
            h_perm[rs, :] = jnp.dot(perm_ref[...], h_nat[rs, :], preferred_element_type=F32).astype(BF16)

    def rope_blocks(acc, cos, sin):
        for half in range(2):
            is_q = jnp.logical_or(j == 4, jnp.logical_and(j == 1, half == 0))
            plain = (j != 4) if half == 1 else (j < 0)
            qscale = jnp.where(is_q, HEAD_DIM ** -0.5 * LOG2_E, 1.0).astype(F32)
            c = jnp.where(plain, 1.0, cos * qscale)
            sn = jnp.where(plain, 0.0, sin * qscale)
            for b in range(ATTN_WIDTH // LANES):
                sl = slice(half * ATTN_WIDTH + b * LANES, half * ATTN_WIDTH + (b + 1) * LANES)
                t = acc[:, sl]
                yield sl, (t * c + pltpu.roll(t, LANES // 2, 1) * sn).astype(BF16)

    @pl.when(j < 2)
    def _():
        acc = jnp.dot(h_nat[...], w_ref[...], preferred_element_type=F32)
        for sl, blk in rope_blocks(acc, cos_ref[...], sin_ref[...]):
            nat_ref[:, sl] = blk

    @pl.when(j >= 2)
    def _():
        acc = jnp.dot(h_perm[...], w_ref[...], preferred_element_type=F32)

        def table(ref):
            return jnp.concatenate([ref[:, u * sub_rows:(u + 1) * sub_rows, :].reshape(PERM_ROWS, LANES)
                                    for u in range(n_sub)], axis=0)

        for sl, blk in rope_blocks(acc, table(cosp_ref), table(sinp_ref)):
            for u in range(n_sub):
                sm_ref[:, u * sub_rows:(u + 1) * sub_rows, sl] = (
                    blk[u * PERM_ROWS:(u + 1) * PERM_ROWS].reshape(STREAMS, sub_rows, LANES))


def _stream_perm(n):
    dst = np.arange(n)
    rows = n // STREAMS
    perm = np.zeros((n, n), np.float32)
    perm[dst, STREAMS * (dst % rows) + dst // rows] = 1.0
    return perm


def _attn_in(x, mod, g, w, tables, tm=1024):
    bsz, s, _ = x.shape
    cos, sin, cosp, sinp = tables
    rows = tm // STREAMS
    wide = 2 * ATTN_WIDTH
    perm = jnp.asarray(_stream_perm(PERM_ROWS), BF16)
    return pl.pallas_call(
        _attn_in_kernel,
        out_shape=(jax.ShapeDtypeStruct((bsz, s, 2 * wide), BF16),
                   jax.ShapeDtypeStruct((bsz, STREAMS, s // STREAMS, 3 * wide), BF16)),
        grid=(bsz, s // tm, 5),
        in_specs=[
            pl.BlockSpec((None, tm, D_MODEL), lambda b, i, j: (b, i, 0)),
            pl.BlockSpec((None, 1, D_MODEL), lambda b, i, j: (b, 0, 0)),
            pl.BlockSpec((None, 1, D_MODEL), lambda b, i, j: (b, 0, 1)),
            pl.BlockSpec((1, D_MODEL), lambda b, i, j: (0, 0)),
            pl.BlockSpec((D_MODEL, wide), lambda b, i, j: (0, j)),
            pl.BlockSpec((tm, LANES), lambda b, i, j: (i, 0)),
            pl.BlockSpec((tm, LANES), lambda b, i, j: (i, 0)),
            pl.BlockSpec((STREAMS, rows, LANES), lambda b, i, j: (0, i, 0)),
            pl.BlockSpec((STREAMS, rows, LANES), lambda b, i, j: (0, i, 0)),
            pl.BlockSpec((PERM_ROWS, PERM_ROWS), lambda b, i, j: (0, 0)),
        ],
        out_specs=(
            pl.BlockSpec((None, tm, wide), lambda b, i, j: (b, i, jnp.clip(j, 0, 1))),
            pl.BlockSpec((None, STREAMS, rows, wide), lambda b, i, j: (b, 0, i, jnp.clip(j - 2, 0, 2))),
        ),
        scratch_shapes=[pltpu.VMEM((tm, D_MODEL), BF16), pltpu.VMEM((tm, D_MODEL), BF16)],
        compiler_params=pltpu.CompilerParams(
            dimension_semantics=("parallel", "parallel", "arbitrary"),
            vmem_limit_bytes=VMEM_LIMIT_BIG),
        name="attn_in",
    )(x, mod, mod, g, w, cos, sin, cosp, sinp, perm)


def _lane_of_odd_head(lane):
    return (lane // (HEAD_DIM // 2)) % 2 == 1


def _mask_bias(valid):
    return jnp.where(valid, 0.0, NEG_INF).astype(F32)


def _attn_pairs(get_q, get_k, get_v, bias, put_acc):
    tq = bias.shape[0]
    lane = lax.broadcasted_iota(jnp.int32, (tq, LANES), 1)
    q_is_odd = _lane_of_odd_head(lane)
    mx_tile = jnp.zeros((tq, LANES), F32)
    den_tile = jnp.ones((tq, LANES), F32)
    ones = jnp.ones((bias.shape[1], LANES), BF16)
    bias2 = jnp.concatenate([bias, bias], axis=0)
    for b in range(ATTN_WIDTH // LANES):
        sl = slice(b * LANES, (b + 1) * LANES)
        qb = get_q(sl).astype(F32)
        kb = get_k(sl)
        vb = jnp.concatenate([get_v(sl), ones], axis=1)
        qm = jnp.concatenate([jnp.where(q_is_odd, 0.0, qb), jnp.where(q_is_odd, qb, 0.0)], axis=0).astype(BF16)
        sc = lax.dot_general(qm, kb, (((1,), (1,)), ((), ())), preferred_element_type=F32) + bias2
        mx = jnp.max(sc, axis=1, keepdims=True)
        p = jnp.exp2((sc - mx).astype(BF16))
        acc = jnp.dot(p, vb, preferred_element_type=F32)
        for odd in (0, 1):
            rows = slice(odd * tq, (odd + 1) * tq)
            mx_tile = jnp.where(lane == 2 * b + odd, mx[rows], mx_tile)
            den_tile = jnp.where(lane == 2 * b + odd, acc[rows, LANES:], den_tile)
        put_acc(sl, jnp.where(q_is_odd, acc[tq:, :LANES], acc[:tq, :LANES]))
    return mx_tile, den_tile


def _pair_factor(tile, b):
    lane = lax.broadcasted_iota(jnp.int32, tile.shape, 1)
    head = 2 * b + _lane_of_odd_head(lane).astype(jnp.int32)
    return jnp.take_along_axis(tile, head, axis=1, mode="promise_in_bounds")


def _merge_groups(stats, get_accs, put_o):
    top = functools.reduce(jnp.maximum, [mx for mx, _ in stats])
    ws = [jnp.exp2(mx - top) for mx, _ in stats]
    inv = 1.0 / sum(w * den for w, (_, den) in zip(ws, stats))
    fs = [w * inv for w in ws]
    for b in range(ATTN_WIDTH // LANES):
        sl = slice(b * LANES, (b + 1) * LANES)
        put_o(sl, sum(_pair_factor(f, b) * get_acc(sl) for f, get_acc in zip(fs, get_accs)))


def _window_mask(tile, tq, m):
    nk = tq + 2 * RADIUS
    row = lax.broadcasted_iota(jnp.int32, (tq, nk), 0)
    col = lax.broadcasted_iota(jnp.int32, (tq, nk), 1)
    kabs = tile * tq - RADIUS + col
    return _mask_bias((jnp.abs(col - RADIUS - row) <= RADIUS) & (kabs >= 0) & (kabs < m))


def _stack_rows(read, prev_ref, cur_ref, next_ref, lo, cnt):
    halo, cur_n = prev_ref.shape[-2], cur_ref.shape[-2]
    a, b = lo, lo + cnt
    parts = []
    if a < 0:
        parts.append(read(prev_ref, halo + a, halo))
        a = 0
    parts.append(read(cur_ref, a, min(b, cur_n)))
    if b > cur_n:
        parts.append(read(next_ref, 0, b - cur_n))
    return parts


def _kv_getters(stack):
    get_k = lambda sl: stack(sl)
    get_v = lambda sl: stack(slice(ATTN_WIDTH + sl.start, ATTN_WIDTH + sl.stop))
    return get_k, get_v


def _attn16_kernel(q_ref, kvp_ref, kvc_ref, kvn_ref, o_ref, mx_ref, den_ref, *, m):
    tq = ATTN_TQ
    n_sub = q_ref.shape[0] // tq
    for t in range(n_sub):
        rows = slice(t * tq, (t + 1) * tq)
        bias = _window_mask(pl.program_id(2) * n_sub + t, tq, m)

        def stack(cols):
            return jnp.concatenate(_stack_rows(lambda ref, a, b: ref[a:b, cols], kvp_ref, kvc_ref, kvn_ref,
                                               t * tq - RADIUS, tq + 2 * RADIUS), axis=0)

        def put_acc(sl, val):
            o_ref[rows, sl] = val.astype(BF16)

        mx_ref[rows], den_ref[rows] = _attn_pairs(lambda sl: q_ref[rows, sl], *_kv_getters(stack), bias, put_acc)


def _attn4_kernel(q_ref, kvp_ref, kvc_ref, kvn_ref, o_ref, mx_ref, den_ref, *, m):
    ns, halo = q_ref.shape[0], kvp_ref.shape[1]
    rows = ATTN_TQ // ns
    span = rows + 2 * halo
    n_sub = q_ref.shape[1] // rows
    n = lax.broadcasted_iota(jnp.int32, (ns * rows, ns * span), 0)
    c = lax.broadcasted_iota(jnp.int32, (ns * rows, ns * span), 1)
    for t in range(n_sub):
        base = (pl.program_id(2) * n_sub + t) * rows
        rs = slice(t * rows, (t + 1) * rows)
        krow = base - halo + c % span
        rel = ns * (krow - (base + n % rows)) + (c // span - n // rows)
        bias = _mask_bias((jnp.abs(rel) <= RADIUS) & (krow >= 0) & (krow < m))

        def stack(cols):
            parts = []
            for s in range(ns):
                parts += _stack_rows(lambda ref, a, b: ref[s, a:b, cols], kvp_ref, kvc_ref, kvn_ref,
                                     t * rows - halo, span)
            return jnp.concatenate(parts, axis=0)

        flat = lambda ref: (lambda sl: ref[:, rs, sl].reshape(ns * rows, LANES))

        def put_acc(sl, val):
            o_ref[:, rs, sl] = val.astype(BF16).reshape(ns, rows, LANES)

        mx, den = _attn_pairs(flat(q_ref), *_kv_getters(stack), bias, put_acc)
        mx_ref[:, rs, :] = mx.reshape(ns, rows, LANES)
        den_ref[:, rs, :] = den.reshape(ns, rows, LANES)


def _attn1_kernel(q_ref, kvp_ref, kvc_ref, kvn_ref, o_ref, mx_ref, den_ref, *, m):
    tq = ATTN_TQ
    n_sub = q_ref.shape[0] // tq
    for t in range(n_sub):
        rows = slice(t * tq, (t + 1) * tq)
        bias = _window_mask(pl.program_id(1) * n_sub + t, tq, m)

        def stack(cols):
            return jnp.concatenate(_stack_rows(lambda ref, a, b: ref[a:b, cols], kvp_ref, kvc_ref, kvn_ref,
                                               t * tq - RADIUS, tq + 2 * RADIUS), axis=0)

        def put_acc(sl, val):
            o_ref[rows, sl] = val.astype(BF16)

        mx_ref[rows], den_ref[rows] = _attn_pairs(lambda sl: q_ref[rows, sl], *_kv_getters(stack), bias, put_acc)


def _attention(qkv0, qkv12):
    bsz, s, _ = qkv0.shape
    m16 = s // STREAMS
    tq, half = ATTN_TQ, RADIUS
    cur_rows = min(ATTN_SUB * tq, m16)
    params = lambda n: pltpu.CompilerParams(dimension_semantics=("parallel",) * n,
                                            vmem_limit_bytes=VMEM_LIMIT)
    same = lambda i: i

    def halo_idx(cur_n, halo_n, total):
        per = cur_n // halo_n
        return (lambda i: jnp.maximum(i * per - 1, 0)), (lambda i: jnp.minimum((i + 1) * per, total // halo_n - 1))

    def spec16(nrows, row_idx, col, width=ATTN_WIDTH):
        return pl.BlockSpec((None, None, nrows, width), lambda b, r, i: (b, r, row_idx(i), col))
    prv, nxt = halo_idx(cur_rows, half, m16)
    stat16 = jax.ShapeDtypeStruct((bsz, STREAMS, m16, LANES), F32)
    o2, mx2, den2 = pl.pallas_call(
        functools.partial(_attn16_kernel, m=m16),
        out_shape=(jax.ShapeDtypeStruct((bsz, STREAMS, m16, ATTN_WIDTH), BF16), stat16, stat16),
        grid=(bsz, STREAMS, m16 // cur_rows),
        in_specs=[spec16(cur_rows, same, 5), spec16(half, prv, 1, 2 * ATTN_WIDTH),
                  spec16(cur_rows, same, 1, 2 * ATTN_WIDTH), spec16(half, nxt, 1, 2 * ATTN_WIDTH)],
        out_specs=(spec16(cur_rows, same, 0), spec16(cur_rows, same, 0, LANES), spec16(cur_rows, same, 0, LANES)),
        compiler_params=params(3), name="attn_dil16",
    )(*([qkv12] * 4))

    ns = STREAMS // 4
    rows4, halo4 = cur_rows // ns, half // ns

    def view4(a):
        return a.reshape(bsz, ns, 4, m16, a.shape[-1])

    def spec4(nrows, row_idx, col, width=ATTN_WIDTH):
        return pl.BlockSpec((None, ns, None, nrows, width), lambda b, r, i: (b, 0, r, row_idx(i), col))
    prv4, nxt4 = halo_idx(rows4, halo4, m16)
    q4 = view4(qkv12)
    stat4 = jax.ShapeDtypeStruct((bsz, ns, 4, m16, LANES), F32)
    o1, mx1, den1 = pl.pallas_call(
        functools.partial(_attn4_kernel, m=m16),
        out_shape=(jax.ShapeDtypeStruct((bsz, ns, 4, m16, ATTN_WIDTH), BF16), stat4, stat4),
        grid=(bsz, 4, m16 // rows4),
        in_specs=[spec4(rows4, same, 4), spec4(halo4, prv4, 0, 2 * ATTN_WIDTH),
                  spec4(rows4, same, 0, 2 * ATTN_WIDTH), spec4(halo4, nxt4, 0, 2 * ATTN_WIDTH)],
        out_specs=(spec4(rows4, same, 0), spec4(rows4, same, 0, LANES), spec4(rows4, same, 0, LANES)),
        compiler_params=params(3), name="attn_dil4",
    )(q4, q4, q4, q4)
    streams = lambda a: a.reshape(bsz, STREAMS, m16, a.shape[-1])

    cur1 = ATTN_SUB * tq

    def spec1(nrows, row_idx, col, width=ATTN_WIDTH):
        return pl.BlockSpec((None, nrows, width), lambda b, i: (b, row_idx(i), col))
    prv1, nxt1 = halo_idx(cur1, half, s)
    stat1 = jax.ShapeDtypeStruct((bsz, s, LANES), F32)
    group0 = pl.pallas_call(
        functools.partial(_attn1_kernel, m=s),
        out_shape=(jax.ShapeDtypeStruct((bsz, s, ATTN_WIDTH), BF16), stat1, stat1),
        grid=(bsz, s // cur1),
        in_specs=[spec1(cur1, same, 2), spec1(half, prv1, 0, 2 * ATTN_WIDTH),
                  spec1(cur1, same, 0, 2 * ATTN_WIDTH), spec1(half, nxt1, 0, 2 * ATTN_WIDTH)],
        out_specs=(spec1(cur1, same, 0), spec1(cur1, same, 0, LANES), spec1(cur1, same, 0, LANES)),
        compiler_params=params(2), name="attn_dil1",
    )(*([qkv0] * 4))
    return group0, (o2, mx2, den2), (streams(o1), streams(mx1), streams(den1))


def _attn_out_kernel(o0_ref, mx0_ref, den0_ref, o2_ref, mx2_ref, den2_ref, o1_ref, mx1_ref, den1_ref,
                     unperm_ref, z_ref, x_ref, gate_ref, w_ref, out_ref, y_scr, a2_scr, a1_scr, stat_scr):
    tm = x_ref.shape[0]
    tq = ATTN_TQ
    srow, n_srow = tq // STREAMS, mx2_ref.shape[1]
    n_un = unperm_ref.shape[0]
    for src, dst in ((o2_ref, a2_scr), (o1_ref, a1_scr)):
        for u in range(tm // n_un):
            rs = slice(u * n_un // STREAMS, (u + 1) * n_un // STREAMS)
            part = jnp.concatenate([src[r, rs, :] for r in range(STREAMS)], axis=0)
            dst[u * n_un:(u + 1) * n_un, :] = jnp.dot(unperm_ref[...], part, preferred_element_type=F32)
    for k, src in enumerate((mx2_ref, den2_ref, mx1_ref, den1_ref)):
        for r in range(STREAMS):
            stat_scr[k, r * STAT_PITCH:r * STAT_PITCH + n_srow, :] = src[r]
    for t in range(tm // tq):
        rows = slice(t * tq, (t + 1) * tq)

        def natural(k):
            return jnp.concatenate([stat_scr[k, pl.ds(t * srow + ii, STREAMS, stride=STAT_PITCH), :]
                                    for ii in range(srow)], axis=0)

        def put_y(sl, val):
            y_scr[rows, sl] = (val * _silu(z_ref[rows, sl].astype(F32))).astype(BF16)

        stats = [(mx0_ref[rows], den0_ref[rows]), (natural(0), natural(1)), (natural(2), natural(3))]
        _merge_groups(stats, [lambda sl: o0_ref[rows, sl].astype(F32), lambda sl: a2_scr[rows, sl],
                              lambda sl: a1_scr[rows, sl]], put_y)
    out = jnp.dot(y_scr[...], w_ref[...], preferred_element_type=F32)
    out_ref[...] = x_ref[...] + gate_ref[...] * out


def _attn_out(groups, nat, x, mod, w, tm=PERM_ROWS):
    bsz, s, _ = x.shape
    row = lambda width: pl.BlockSpec((None, tm, width), lambda b, i: (b, i, 0))
    sm = lambda width: pl.BlockSpec((None, STREAMS, tm // STREAMS, width), lambda b, i: (b, 0, i, 0))
    z_col = nat.shape[-1] // ATTN_WIDTH - 1
    unperm = jnp.asarray(_stream_perm(UNPERM_ROWS).T, BF16)
    return pl.pallas_call(
        _attn_out_kernel,
        out_shape=jax.ShapeDtypeStruct(x.shape, F32),
        grid=(bsz, s // tm),
        in_specs=[row(ATTN_WIDTH), row(LANES), row(LANES)] + [sm(ATTN_WIDTH), sm(LANES), sm(LANES)] * 2
                 + [pl.BlockSpec((UNPERM_ROWS, UNPERM_ROWS), lambda b, i: (0, 0)),
                    pl.BlockSpec((None, tm, ATTN_WIDTH), lambda b, i: (b, i, z_col)), row(D_MODEL),
                    pl.BlockSpec((None, 1, D_MODEL), lambda b, i: (b, 0, 2)),
                    pl.BlockSpec((ATTN_WIDTH, D_MODEL), lambda b, i: (0, 0))],
        out_specs=row(D_MODEL),
        scratch_shapes=[pltpu.VMEM((tm, ATTN_WIDTH), BF16), pltpu.VMEM((tm, ATTN_WIDTH), F32),
                        pltpu.VMEM((tm, ATTN_WIDTH), F32), pltpu.VMEM((4, STREAMS * STAT_PITCH, LANES), F32)],
        compiler_params=pltpu.CompilerParams(
            dimension_semantics=("parallel", "parallel"),
            vmem_limit_bytes=VMEM_LIMIT),
        name="attn_out",
    )(*groups[0], *groups[1], *groups[2], unperm, nat, x, mod, w)


def _cmul(ar, ai, br, bi):
    return ar * br - ai * bi, ar * bi + ai * br


def _s5_prep_kernel(*refs):
    ins, d_diag, outs, v_scr = refs[:9], refs[9], refs[10:14], refs[14]
    for gi in range(d_diag.shape[0]):
        _s5_prep_group(*[r.at[:, gi] for r in ins], d_diag.at[gi], *[o.at[gi] for o in outs], v_scr.at[gi])


def _s5_prep_group(lam_re_a, lam_im_a, lam_re_b, lam_im_b, logdt, bt_re, bt_im,
                   c_re, c_im, d_diag, toep_ref, mb_ref, mc_ref, al_ref, v_scr):
    lane = lax.broadcasted_iota(jnp.int32, (STATE, LANES), 1)
    a_idx = lane // GROUP_CH
    eye = (lax.broadcasted_iota(jnp.int32, (STATE, STATE), 0)
           == lax.broadcasted_iota(jnp.int32, (STATE, STATE), 1))
    n_tiles = CHUNK // T_PER_VREG

    mb_rows = []
    v_rhs = []
    v_lhs = []
    mc_cols = []
    zbar_f = None
    for dr in range(2):
        dt = jnp.exp(logdt[dr])
        lrb = jnp.minimum(lam_re_b[dr], LAMBDA_RE_MAX)
        mag = jnp.exp(lrb * dt)
        pw_r, pw_i = mag * jnp.cos(lam_im_b[dr] * dt), mag * jnp.sin(lam_im_b[dr] * dt)
        diag = lambda v: jnp.sum(jnp.where(eye, jnp.broadcast_to(v, (STATE, STATE)), 0.0), axis=1, keepdims=True)
        lr = jnp.minimum(lam_re_a[dr], LAMBDA_RE_MAX)
        li = lam_im_a[dr]
        l1r, l1i = diag(pw_r), diag(pw_i)
        small = [(jnp.ones_like(l1r), jnp.zeros_like(l1r)), (l1r, l1i)]
        for _ in range(T_PER_VREG - 1):
            small.append(_cmul(*small[-1], l1r, l1i))

        def lane_powers(exponent_of_step):
            pr, pi = small[exponent_of_step[0]]
            pr, pi = jnp.broadcast_to(pr, (STATE, LANES)), jnp.broadcast_to(pi, (STATE, LANES))
            for a in range(1, T_PER_VREG):
                er, ei = small[exponent_of_step[a]]
                pr, pi = jnp.where(a_idx == a, er, pr), jnp.where(a_idx == a, ei, pi)
            return pr, pi

        den = lr * lr + li * li
        nr, ni = l1r - 1.0, l1i
        cr, ci = (nr * lr + ni * li) / den, (ni * lr - nr * li) / den
        bbr, bbi = _cmul(cr, ci, bt_re[dr], bt_im[dr])
        l8r, l8i = small[T_PER_VREG]
        tiles = [None] * n_tiles
        if dr == 0:
            pr, pi = lane_powers([T_PER_VREG - 1 - a for a in range(T_PER_VREG)])
            cur = _cmul(pr, pi, bbr, bbi)
            for jt in range(n_tiles - 1, -1, -1):
                tiles[jt] = cur
                if jt:
                    cur = _cmul(cur[0], cur[1], l8r, l8i)
            v_rhs.append(_cmul(tiles[n_tiles - 1][0], tiles[n_tiles - 1][1], l1r, l1i))
            zbar_f = (jnp.where(lane < GROUP_CH, bbr, 0.0), jnp.where(lane < GROUP_CH, bbi, 0.0))
        else:
            pr, pi = lane_powers(list(range(T_PER_VREG)))
            cur = _cmul(pr, pi, bbr, bbi)
            for jt in range(n_tiles):
                tiles[jt] = cur
                if jt < n_tiles - 1:
                    cur = _cmul(cur[0], cur[1], l8r, l8i)
            v_rhs.append(tiles[0])
        mb_rows.append(jnp.concatenate([t[0] for t in tiles], axis=1))
        mb_rows.append(jnp.concatenate([t[1] for t in tiles], axis=1))
        a_r, a_i = l8r, l8i
        for _ in range(int(math.log2(CHUNK // T_PER_VREG))):
            a_r, a_i = _cmul(a_r, a_i, a_r, a_i)
        al_ref[2 * dr] = jnp.broadcast_to(a_r, (STATE, LANES))
        al_ref[2 * dr + 1] = jnp.broadcast_to(a_i, (STATE, LANES))

        pows = [(pw_r, pw_i)]
        for _ in range(int(math.log2(CHUNK)) - 1):
            pows.append(_cmul(*pows[-1], *pows[-1]))
        cre, cim = c_re[dr], c_im[dr]
        xr, xi = _cmul(cre, cim, *pows[0])
        for k in range(int(math.log2(CHUNK))):
            yr, yi = _cmul(xr, xi, *pows[k])
            if dr == 0:
                xr, xi = jnp.concatenate([xr, yr], 0), jnp.concatenate([xi, yi], 0)
            else:
                xr, xi = jnp.concatenate([yr, xr], 0), jnp.concatenate([yi, xi], 0)
        mc_cols += [xr, -xi]
        xr, xi = cre, cim
        for k in range(int(math.log2(n_tiles))):
            yr, yi = _cmul(xr, xi, *pows[k + int(math.log2(T_PER_VREG))])
            if dr == 0:
                xr, xi = jnp.concatenate([yr, xr], 0), jnp.concatenate([yi, xi], 0)
            else:
                xr, xi = jnp.concatenate([xr, yr], 0), jnp.concatenate([xi, yi], 0)
        v_lhs.append((xr, xi))

    mb_ref[...] = jnp.concatenate(mb_rows, axis=0).astype(BF16)
    mc_ref[...] = jnp.concatenate(mc_cols, axis=1).astype(BF16)

    def cdot(lhs, rhs):
        return (jnp.dot(lhs[0], rhs[0], precision=HI, preferred_element_type=F32)
                - jnp.dot(lhs[1], rhs[1], precision=HI, preferred_element_type=F32))

    out_f = cdot(v_lhs[0], v_rhs[0])
    zero_rows = jnp.zeros((CHUNK_ROWS // T_PER_VREG - GROUP_CH, STATE), F32)
    lag0_lhs = (jnp.concatenate([c_re[0], zero_rows], 0), jnp.concatenate([c_im[0], zero_rows], 0))
    out_b = cdot(v_lhs[1], v_rhs[1]) + cdot(lag0_lhs, zbar_f)
    out_b = out_b + jnp.concatenate([d_diag[...], jnp.zeros((CHUNK_ROWS // T_PER_VREG - GROUP_CH, LANES), F32)], 0)
    for jt in range(n_tiles):
        rows = slice(jt * GROUP_CH, (jt + 1) * GROUP_CH)
        v_scr[:, jt * LANES:(jt + 1) * LANES] = out_f[rows]
        v_scr[:, (n_tiles + jt) * LANES:(n_tiles + jt + 1) * LANES] = out_b[rows]
    v_scr[:, 2 * CHUNK_ROWS:] = jnp.zeros((GROUP_CH, LANES), F32)

    for k in range(T_PER_VREG):
        vk = v_scr[:, k * GROUP_CH:k * GROUP_CH + 2 * CHUNK_ROWS]
        for t_out in range(CHUNK):
            off = (CHUNK - t_out) * GROUP_CH
            if off % LANES == k * GROUP_CH:
                al_off = off - k * GROUP_CH
                toep_ref[t_out * GROUP_CH:(t_out + 1) * GROUP_CH, :] = (
                    vk[:, al_off:al_off + CHUNK_ROWS].astype(BF16))


def _s5_prep(lam_re, lam_im, log_dt, b_re, b_im, c_re, c_im, d_skip):
    g = SSM_GROUPS
    d_diag = d_skip.reshape(g, GROUP_CH, 1) * jnp.eye(GROUP_CH, LANES, dtype=F32)
    lam_a = lambda a: a.reshape(2, g, STATE, 1)
    lam_b = lambda a: a.reshape(2, g, 1, STATE)
    tile_b = lambda a: jnp.tile(a, (1, 1, 1, T_PER_VREG))
    gs = S5_GROUPS_PER_STEP
    dir_spec = lambda r, c: pl.BlockSpec((2, gs, r, c), lambda i: (0, i, 0, 0))
    return pl.pallas_call(
        _s5_prep_kernel,
        out_shape=(jax.ShapeDtypeStruct((g, CHUNK_ROWS, CHUNK_ROWS), BF16),
                   jax.ShapeDtypeStruct((g, 4 * STATE, CHUNK_ROWS), BF16),
                   jax.ShapeDtypeStruct((g, CHUNK_ROWS, 4 * STATE), BF16),
                   jax.ShapeDtypeStruct((g, 4, STATE, LANES), F32)),
        grid=(g // gs,),
        in_specs=[dir_spec(STATE, 1), dir_spec(STATE, 1), dir_spec(1, STATE), dir_spec(1, STATE),
                  dir_spec(1, 1), dir_spec(STATE, LANES), dir_spec(STATE, LANES),
                  dir_spec(GROUP_CH, STATE), dir_spec(GROUP_CH, STATE),
                  pl.BlockSpec((gs, GROUP_CH, LANES), lambda i: (i, 0, 0))],
        out_specs=(pl.BlockSpec((gs, CHUNK_ROWS, CHUNK_ROWS), lambda i: (i, 0, 0)),
                   pl.BlockSpec((gs, 4 * STATE, CHUNK_ROWS), lambda i: (i, 0, 0)),
                   pl.BlockSpec((gs, CHUNK_ROWS, 4 * STATE), lambda i: (i, 0, 0)),
                   pl.BlockSpec((gs, 4, STATE, LANES), lambda i: (i, 0, 0, 0))),
        scratch_shapes=[pltpu.VMEM((gs, GROUP_CH, V_LANES), F32)],
        compiler_params=pltpu.CompilerParams(dimension_semantics=("parallel",)),
        name="s5_prep",
    )(lam_a(lam_re), lam_a(lam_im), lam_b(lam_re), lam_b(lam_im), log_dt.reshape(2, g, 1, 1),
      tile_b(b_re), tile_b(b_im), c_re, c_im, d_diag)


def _slab_perm():
    dst = np.arange(SLAB)
    perm = np.zeros((SLAB, SLAB), np.float32)
    perm[dst, (dst % SLAB_CHUNKS) * SUBLANES + dst // SLAB_CHUNKS] = 1.0
    return perm


def _s5_in_kernel(x_ref, mod_ref, g_ref, w_ref, p_ref, u_ref, z_ref, h_scr, *, chunks_per_seq):
    n_chunks = x_ref.shape[0]
    for b in range(n_chunks // chunks_per_seq):
        rows = slice(b * chunks_per_seq, (b + 1) * chunks_per_seq)
        h = _rms_modulate(x_ref[rows], g_ref[...], mod_ref[b:b + 1, D_MODEL:2 * D_MODEL],
                          mod_ref[b:b + 1, :D_MODEL])
        h2 = h.reshape(chunks_per_seq * SUBLANES, D_MODEL).astype(BF16)
        for s in range(chunks_per_seq // SLAB_CHUNKS):
            hp = jnp.dot(p_ref[...], h2[s * SLAB:(s + 1) * SLAB], preferred_element_type=F32)
            c0 = b * chunks_per_seq + s * SLAB_CHUNKS
            for k in range(SUBLANES):
                h_scr[k, c0:c0 + SLAB_CHUNKS, :] = hp[k * SLAB_CHUNKS:(k + 1) * SLAB_CHUNKS].astype(BF16)
    for k in range(SUBLANES):
        ht = h_scr[k].astype(F32).T.astype(BF16)
        uz = jnp.dot(w_ref[...], ht, preferred_element_type=F32)
        u_ref[k] = uz[:SSM_WIDTH].astype(BF16)
        z_ref[k] = uz[SSM_WIDTH:].astype(BF16)


def _s5_in(x3, mod, g, w_t, chunks_per_seq):
    n_chunks = x3.shape[0]
    once = lambda shape: pl.BlockSpec(shape, lambda t: (0,) * len(shape), pipeline_mode=pl.Buffered(1))
    out = pl.BlockSpec((SUBLANES, SSM_WIDTH, n_chunks), lambda t: (t, 0, 0))
    return pl.pallas_call(
        functools.partial(_s5_in_kernel, chunks_per_seq=chunks_per_seq),
        out_shape=(jax.ShapeDtypeStruct((CHUNK, SSM_WIDTH, n_chunks), BF16),) * 2,
        grid=(CHUNK // SUBLANES,),
        in_specs=[pl.BlockSpec((n_chunks, SUBLANES, D_MODEL), lambda t: (0, t, 0)),
                  once(mod.shape), once((1, D_MODEL)), once((2 * SSM_WIDTH, D_MODEL)),
                  once((SLAB, SLAB))],
        out_specs=(out, out),
        scratch_shapes=[pltpu.VMEM((SUBLANES, n_chunks, D_MODEL), BF16)],
        compiler_params=pltpu.CompilerParams(
            dimension_semantics=("parallel",), vmem_limit_bytes=VMEM_LIMIT_BIG),
        name="s5_in",
    )(x3, mod, g, w_t, jnp.asarray(_slab_perm(), BF16))


def _gelu_tanh(v):
    return 0.5 * v * (1.0 + jnp.tanh(math.sqrt(2.0 / math.pi) * (v + 0.044715 * (v * v * v))))


def _s5_core_kernel(u_ref, toep_ref, mb_ref, mc_ref, al_ref, o_ref, *, chunks_per_seq):
    for gi in range(u_ref.shape[1]):
        _s5_core_group(u_ref.at[:, gi], toep_ref.at[gi], mb_ref.at[gi], mc_ref.at[gi], al_ref.at[gi],
                       o_ref.at[:, gi], chunks_per_seq)


def _s5_core_group(u_ref, toep_ref, mb_ref, mc_ref, al_ref, o_ref, chunks_per_seq):
    n_chunks = u_ref.shape[-1]
    ub = u_ref[...].reshape(CHUNK_ROWS, n_chunks)
    inc = jnp.dot(mb_ref[...], ub, preferred_element_type=F32)
    pos = lax.broadcasted_iota(jnp.int32, (STATE, n_chunks), 1) % chunks_per_seq
    reps = n_chunks // LANES

    def lane_tile(a):
        return jnp.concatenate([a] * reps, axis=1) if reps > 1 else a

    states = []
    for dr in range(2):
        xr = inc[(2 * dr) * STATE:(2 * dr + 1) * STATE]
        xi = inc[(2 * dr + 1) * STATE:(2 * dr + 2) * STATE]
        ar, ai = al_ref[2 * dr], al_ref[2 * dr + 1]

        def shifted(v, step):
            if dr == 0:
                return jnp.where(pos >= step, pltpu.roll(v, step, 1), 0.0)
            return jnp.where(pos < chunks_per_seq - step, pltpu.roll(v, n_chunks - step, 1), 0.0)

        step = 1
        while step < chunks_per_seq:
            sr, si = shifted(xr, step), shifted(xi, step)
            tr, ti = lane_tile(ar), lane_tile(ai)
            xr, xi = xr + tr * sr - ti * si, xi + tr * si + ti * sr
            ar, ai = _cmul(ar, ai, ar, ai)
            step *= 2
        states += [shifted(xr, 1), shifted(xi, 1)]
    h_in = jnp.concatenate(states, axis=0).astype(BF16)
    y = (jnp.dot(toep_ref[...], ub, preferred_element_type=F32)
         + jnp.dot(mc_ref[...], h_in, preferred_element_type=F32))
    o_ref[...] = _gelu_tanh(y).astype(BF16).reshape(CHUNK, GROUP_CH, n_chunks)


def _s5_core(u_cl, toep, mb, mc, al, chunks_per_seq):
    n_chunks = u_cl.shape[-1]
    u4 = u_cl.reshape(CHUNK, SSM_GROUPS, GROUP_CH, n_chunks)
    gs = S5_GROUPS_PER_STEP
    grp = pl.BlockSpec((CHUNK, gs, GROUP_CH, n_chunks), lambda g: (0, g, 0, 0))
    out = pl.pallas_call(
        functools.partial(_s5_core_kernel, chunks_per_seq=chunks_per_seq),
        out_shape=jax.ShapeDtypeStruct(u4.shape, BF16),
        grid=(SSM_GROUPS // gs,),
        in_specs=[grp,
                  pl.BlockSpec((gs, CHUNK_ROWS, CHUNK_ROWS), lambda g: (g, 0, 0)),
                  pl.BlockSpec((gs, 4 * STATE, CHUNK_ROWS), lambda g: (g, 0, 0)),
                  pl.BlockSpec((gs, CHUNK_ROWS, 4 * STATE), lambda g: (g, 0, 0)),
                  pl.BlockSpec((gs, 4, STATE, LANES), lambda g: (g, 0, 0, 0))],
        out_specs=grp,
        compiler_params=pltpu.CompilerParams(
            dimension_semantics=("parallel",), vmem_limit_bytes=VMEM_LIMIT),
        name="s5_core",
    )(u4, toep, mb, mc, al)
    return out.reshape(CHUNK, SSM_WIDTH, n_chunks)


def _s5_out_kernel(g_ref, z_ref, wglu_ref, wout_ref, pt_ref, x_ref, mod_ref, fg_ref, o_ref, y_scr,
                   *, final, chunks_per_seq):
    q = pl.program_id(1)
    n_half = x_ref.shape[0]
    k_per = g_ref.shape[0]
    n_glu = SUBLANES // k_per

    for qv in range(n_glu):
        @pl.when(q == qv)
        def _():
            for kk in range(k_per):
                gb = g_ref[kk]
                glu = jnp.dot(wglu_ref[...], gb, preferred_element_type=F32)
                y = gb.astype(F32) * _sigmoid(glu) * _silu(z_ref[kk].astype(F32))
                y_scr[qv * k_per + kk] = y.T.astype(BF16)

    for hv in range(y_scr.shape[1] // n_half):
        @pl.when(q == n_glu + hv)
        def _():
            for s in range(n_half // SLAB_CHUNKS):
                c0 = hv * n_half + s * SLAB_CHUNKS
                src = jnp.concatenate([y_scr[k, c0:c0 + SLAB_CHUNKS, :] for k in range(SUBLANES)], axis=0)
                yp = jnp.dot(pt_ref[...], src, preferred_element_type=F32).astype(BF16)
                out = jnp.dot(yp, wout_ref[...], preferred_element_type=F32)
                b = c0 // chunks_per_seq
                upd = mod_ref[b:b + 1, 2 * D_MODEL:] * out
                rows = slice(s * SLAB_CHUNKS, (s + 1) * SLAB_CHUNKS)
                xn = x_ref[rows] + upd.reshape(SLAB_CHUNKS, SUBLANES, D_MODEL)
                if final:
                    xn = xn * lax.rsqrt(jnp.mean(xn * xn, axis=-1, keepdims=True) + NORM_EPS) * fg_ref[...]
                o_ref[rows] = xn


def _s5_out(g_cl, z_cl, wglu_t, w_out, x3, mod, final_g, final, chunks_per_seq, n_split=2):
    n_chunks = x3.shape[0]
    n_half = n_chunks // n_split
    n_glu = SUBLANES // S5_OUT_K
    once = lambda shape: pl.BlockSpec(shape, lambda t, q: (0,) * len(shape), pipeline_mode=pl.Buffered(1))
    act = pl.BlockSpec((S5_OUT_K, SSM_WIDTH, n_chunks), lambda t, q: (t * n_glu + jnp.minimum(q, n_glu - 1), 0, 0))
    row = pl.BlockSpec((n_half, SUBLANES, D_MODEL), lambda t, q: (jnp.maximum(q - n_glu, 0), t, 0))
    return pl.pallas_call(
        functools.partial(_s5_out_kernel, final=final, chunks_per_seq=chunks_per_seq),
        out_shape=jax.ShapeDtypeStruct(x3.shape, F32),
        grid=(CHUNK // SUBLANES, n_glu + n_split),
        in_specs=[act, act, once((SSM_WIDTH, SSM_WIDTH)), once((SSM_WIDTH, D_MODEL)), once((SLAB, SLAB)),
                  row, once(mod.shape), once((1, D_MODEL))],
        out_specs=row,
        scratch_shapes=[pltpu.VMEM((SUBLANES, n_chunks, SSM_WIDTH), BF16)],
        compiler_params=pltpu.CompilerParams(
            dimension_semantics=("parallel", "arbitrary"), vmem_limit_bytes=VMEM_LIMIT_BIG),
        name="s5_out",
    )(g_cl, z_cl, wglu_t, w_out, jnp.asarray(_slab_perm().T, BF16), x3, mod, final_g)


def _rope_tables(s):
    inv_freq = ROPE_THETA ** (-jnp.arange(0, HEAD_DIM, 2, dtype=F32) / HEAD_DIM)
    ang = jnp.arange(s, dtype=F32)[:, None] * inv_freq[None, :]
    reps = LANES // (HEAD_DIM // 2)
    sign = jnp.where(jnp.arange(LANES) < LANES // 2, -1.0, 1.0).astype(F32)
    cos, sin = jnp.tile(jnp.cos(ang), (1, reps)), jnp.tile(jnp.sin(ang), (1, reps)) * sign
    stream_major = lambda a: a.reshape(s // STREAMS, STREAMS, LANES).transpose(1, 0, 2)
    return cos, sin, stream_major(cos), stream_major(sin)


def _prep_attn_w_in(w):
    d_in = w.shape[0]
    w = w.astype(BF16).reshape(d_in, -1, ATTN_HEADS // 2, 2, 2, HEAD_DIM // 2)
    w = w.transpose(0, 1, 2, 4, 3, 5)
    return w[:, np.array(ATTN_IN_BLOCK_ORDER)].reshape(d_in, -1)


def _prep_attn_w_out(w):
    w = w.astype(BF16).reshape(ATTN_HEADS // 2, 2, 2, HEAD_DIM // 2, w.shape[1])
    return w.transpose(0, 2, 1, 3, 4).reshape(ATTN_WIDTH, -1)


def _trunk(x, ada, norm_g, attn_w, s5_w, s5_ops, tables, final_norm_g):
    bsz, s, _ = x.shape
    chunks_per_seq = s // CHUNK
    n_chunks = bsz * chunks_per_seq
    fg = final_norm_g.reshape(1, D_MODEL)
    for i in range(DEPTH):
        mod = ada[i].reshape(bsz, 1, 3 * D_MODEL)
        g = norm_g[i].reshape(1, D_MODEL)
        j = i // N_MIXERS
        if i % N_MIXERS == 0:
            w_in, w_out = attn_w[j]
            nat, qkv12 = _attn_in(x, mod, g, w_in, tables)
            x = _attn_out(_attention(nat, qkv12), nat, x, mod, w_out)
        else:
            w_in_t, wglu_t, w_out = s5_w[j]
            toep, mb, mc, al = s5_ops[j]
            x3 = x.reshape(n_chunks, CHUNK, D_MODEL)
            u_cl, z_cl = _s5_in(x3, ada[i], g, w_in_t, chunks_per_seq)
            g_cl = _s5_core(u_cl, toep, mb, mc, al, chunks_per_seq)
            x3 = _s5_out(g_cl, z_cl, wglu_t, w_out, x3, ada[i], fg, i == DEPTH - 1, chunks_per_seq)
            x = x3.reshape(bsz, s, D_MODEL)
    return x


def kernel(x_prompt, x_sample, c_prompt, c_sample, norm_g, ada_w, ada_b, attn_w_in, attn_w_out,
           ssm_w_in, ssm_lam_re, ssm_lam_im, ssm_log_dt, ssm_b_re, ssm_b_im, ssm_c_re, ssm_c_im,
           ssm_d, ssm_w_glu, ssm_w_out, final_norm_g):
    assert (DEPTH - 1) % N_MIXERS == 1, "the final norm is fused into the last S5 layer"
    assert all(w // (2 * d) == RADIUS for w, d in DILATED_PAIRS)
    assert [d for _, d in DILATED_PAIRS] == [1, 4, STREAMS]
    n_prompt = c_prompt.shape[0]
    ada = _ada(jnp.concatenate([c_prompt, c_sample], axis=0), ada_w, ada_b)
    attn_w = [(_prep_attn_w_in(attn_w_in[j]), _prep_attn_w_out(attn_w_out[j]))
              for j in range(attn_w_in.shape[0])]
    s5_w, s5_ops = [], []
    for j in range(ssm_w_in.shape[0]):
        s5_w.append((ssm_w_in[j].T.astype(BF16), ssm_w_glu[j].T.astype(BF16), ssm_w_out[j].astype(BF16)))
        s5_ops.append(_s5_prep(ssm_lam_re[j], ssm_lam_im[j], ssm_log_dt[j], ssm_b_re[j],
                               ssm_b_im[j], ssm_c_re[j], ssm_c_im[j], ssm_d[j]))
    tables = _rope_tables(max(x_prompt.shape[1], x_sample.shape[1]))
    y_prompt = _trunk(x_prompt, ada[:, :n_prompt], norm_g, attn_w, s5_w, s5_ops, tables, final_norm_g)
    y_sample = _trunk(x_sample, ada[:, n_prompt:], norm_g, attn_w, s5_w, s5_ops, tables, final_norm_g)
    return (y_prompt, y_sample)
```

```python
import functools
import math

import numpy as np
import jax
import jax.numpy as jnp
from jax import lax
from jax.experimental import pallas as pl
from jax.experimental.pallas import tpu as pltpu

D_MODEL = 1024
DEPTH = 4
N_MIXERS = 2
ATTN_HEADS = 16
HEAD_DIM = 64
ATTN_WIDTH = ATTN_HEADS * HEAD_DIM
DILATED_PAIRS = ((128, 1), (512, 4), (2048, 16))
N_DIL = len(DILATED_PAIRS)
ROPE_THETA = 10000.0
SSM_WIDTH = D_MODEL
GROUP_CH = 16
SSM_GROUPS = SSM_WIDTH // GROUP_CH
STATE = 64
LAMBDA_RE_MAX = -1e-4
NORM_EPS = 1e-6
NEG_INF = -1e30

LANES = 128
SUBLANES = 8
CHUNK = 64
CHUNK_ROWS = CHUNK * GROUP_CH
T_PER_VREG = LANES // GROUP_CH
V_LANES = 2 * CHUNK_ROWS + LANES
STREAMS = 16
RADIUS = 64
ATTN_TQ = 128
ATTN_SUB = 4
STAT_PITCH = 40
ATTN_IN_BLOCK_ORDER = (1, 2, 0, 9, 4, 5, 7, 8, 3, 6)
PERM_ROWS = 512
UNPERM_ROWS = 256
LOG2_E = math.log2(math.e)
S5_GROUPS_PER_STEP = 2
S5_OUT_K = 4
SLAB = 256
SLAB_CHUNKS = SLAB // SUBLANES
VMEM_LIMIT = 48 * 1024 * 1024
VMEM_LIMIT_BIG = 56 * 1024 * 1024

F32 = jnp.float32
BF16 = jnp.bfloat16
HI = lax.Precision.HIGHEST


def _sigmoid(v):
    return 0.5 * jnp.tanh(0.5 * v) + 0.5


def _silu(v):
    return v * _sigmoid(v)


def _rms_modulate(x, g, scale, shift):
    rs = lax.rsqrt(jnp.mean(x * x, axis=-1, keepdims=True) + NORM_EPS)
    return (x * rs * g) * (1.0 + scale) + shift


def _ada_kernel(c_ref, w_ref, b_ref, o_ref):
    o_ref[...] = jnp.dot(_silu(c_ref[...]), w_ref[...], precision=HI,
                         preferred_element_type=F32) + b_ref[...]


def _ada(c, ada_w, ada_b):
    nb = c.shape[0]
    return pl.pallas_call(
        _ada_kernel,
        out_shape=jax.ShapeDtypeStruct((DEPTH, nb, 3 * D_MODEL), F32),
        grid=(DEPTH, 3),
        in_specs=[
            pl.BlockSpec((nb, D_MODEL), lambda i, j: (0, 0)),
            pl.BlockSpec((None, D_MODEL, D_MODEL), lambda i, j: (i, 0, j)),
            pl.BlockSpec((None, 1, D_MODEL), lambda i, j: (i, 0, j)),
        ],
        out_specs=pl.BlockSpec((None, nb, D_MODEL), lambda i, j: (i, 0, j)),
        name="ada",
    )(c, ada_w, ada_b.reshape(DEPTH, 1, 3 * D_MODEL))


def _attn_in_kernel(x_ref, shift_ref, scale_ref, g_ref, w_ref, cos_ref, sin_ref, cosp_ref, sinp_ref,
                    perm_ref, nat_ref, sm_ref, h_nat, h_perm):
    j = pl.program_id(2)
    tm = x_ref.shape[0]
    n_sub = tm // PERM_ROWS
    sub_rows = PERM_ROWS // STREAMS

    @pl.when(j == 0)
    def _():
        h = _rms_modulate(x_ref[...], g_ref[...], scale_ref[...], shift_ref[...])
        h_nat[...] = h.astype(BF16)
        for u in range(n_sub):
            rs = slice(u * PERM_ROWS, (u + 1) * PERM_ROWS)
            h_perm[rs, :] = jnp.dot(perm_ref[...], h_nat[rs, :], preferred_element_type=F32).astype(BF16)

    def rope_blocks(acc, cos, sin):
        for half in range(2):
            is_q = jnp.logical_or(j == 4, jnp.logical_and(j == 1, half == 0))
            plain = (j != 4) if half == 1 else (j < 0)
            qscale = jnp.where(is_q, HEAD_DIM ** -0.5 * LOG2_E, 1.0).astype(F32)
            c = jnp.where(plain, 1.0, cos * qscale)
            sn = jnp.where(plain, 0.0, sin * qscale)
            for b in range(ATTN_WIDTH // LANES):
                sl = slice(half * ATTN_WIDTH + b * LANES, half * ATTN_WIDTH + (b + 1) * LANES)
                t = acc[:, sl]
                yield sl, (t * c + pltpu.roll(t, LANES // 2, 1) * sn).astype(BF16)

    @pl.when(j < 2)
    def _():
        acc = jnp.dot(h_nat[...], w_ref[...], preferred_element_type=F32)
        for sl, blk in rope_blocks(acc, cos_ref[...], sin_ref[...]):
            nat_ref[:, sl] = blk

    @pl.when(j >= 2)
    def _():
        acc = jnp.dot(h_perm[...], w_ref[...], preferred_element_type=F32)

        def table(ref):
            return jnp.concatenate([ref[:, u * sub_rows:(u + 1) * sub_rows, :].reshape(PERM_ROWS, LANES)
                                    for u in range(n_sub)], axis=0)

        for sl, blk in rope_blocks(acc, table(cosp_ref), table(sinp_ref)):
            for u in range(n_sub):
                sm_ref[:, u * sub_rows:(u + 1) * sub_rows, sl] = (
                    blk[u * PERM_ROWS:(u + 1) * PERM_ROWS].reshape(STREAMS, sub_rows, LANES))


def _stream_perm(n):
    dst = np.arange(n)
    rows = n // STREAMS
    perm = np.zeros((n, n), np.float32)
    perm[dst, STREAMS * (dst % rows) + dst // rows] = 1.0
    return perm


def _attn_in(x, mod, g, w, tables, tm=1024):
    bsz, s, _ = x.shape
    cos, sin, cosp, sinp = tables
    rows = tm // STREAMS
    wide = 2 * ATTN_WIDTH
    perm = jnp.asarray(_stream_perm(PERM_ROWS), BF16)
    return pl.pallas_call(
        _attn_in_kernel,
        out_shape=(jax.ShapeDtypeStruct((bsz, s, 2 * wide), BF16),
                   jax.ShapeDtypeStruct((bsz, STREAMS, s // STREAMS, 3 * wide), BF16)),
        grid=(bsz, s // tm, 5),
        in_specs=[
            pl.BlockSpec((None, tm, D_MODEL), lambda b, i, j: (b, i, 0)),
            pl.BlockSpec((None, 1, D_MODEL), lambda b, i, j: (b, 0, 0)),
            pl.BlockSpec((None, 1, D_MODEL), lambda b, i, j: (b, 0, 1)),
            pl.BlockSpec((1, D_MODEL), lambda b, i, j: (0, 0)),
            pl.BlockSpec((D_MODEL, wide), lambda b, i, j: (0, j)),
            pl.BlockSpec((tm, LANES), lambda b, i, j: (i, 0)),
            pl.BlockSpec((tm, LANES), lambda b, i, j: (i, 0)),
            pl.BlockSpec((STREAMS, rows, LANES), lambda b, i, j: (0, i, 0)),
            pl.BlockSpec((STREAMS, rows, LANES), lambda b, i, j: (0, i, 0)),
            pl.BlockSpec((PERM_ROWS, PERM_ROWS), lambda b, i, j: (0, 0)),
        ],
        out_specs=(
            pl.BlockSpec((None, tm, wide), lambda b, i, j: (b, i, jnp.clip(j, 0, 1))),
            pl.BlockSpec((None, STREAMS, rows, wide), lambda b, i, j: (b, 0, i, jnp.clip(j - 2, 0, 2))),
        ),
        scratch_shapes=[pltpu.VMEM((tm, D_MODEL), BF16), pltpu.VMEM((tm, D_MODEL), BF16)],
        compiler_params=pltpu.CompilerParams(
            dimension_semantics=("parallel", "parallel", "arbitrary"),
            vmem_limit_bytes=VMEM_LIMIT_BIG),
        name="attn_in",
    )(x, mod, mod, g, w, cos, sin, cosp, sinp, perm)


def _lane_of_odd_head(lane):
    return (lane // (HEAD_DIM // 2)) % 2 == 1


def _mask_bias(valid):
    return jnp.where(valid, 0.0, NEG_INF).astype(F32)


def _attn_pairs(get_q, get_k, get_v, bias, put_acc):
    tq = bias.shape[0]
    lane = lax.broadcasted_iota(jnp.int32, (tq, LANES), 1)
    q_is_odd = _lane_of_odd_head(lane)
    mx_tile = jnp.zeros((tq, LANES), F32)
    den_tile = jnp.ones((tq, LANES), F32)
    ones = jnp.ones((bias.shape[1], LANES), BF16)
    bias2 = jnp.concatenate([bias, bias], axis=0)
    for b in range(ATTN_WIDTH // LANES):
        sl = slice(b * LANES, (b + 1) * LANES)
        qb = get_q(sl).astype(F32)
        kb = get_k(sl)
        vb = jnp.concatenate([get_v(sl), ones], axis=1)
        qm = jnp.concatenate([jnp.where(q_is_odd, 0.0, qb), jnp.where(q_is_odd, qb, 0.0)], axis=0).astype(BF16)
        sc = lax.dot_general(qm, kb, (((1,), (1,)), ((), ())), preferred_element_type=F32) + bias2
        mx = jnp.max(sc, axis=1, keepdims=True)
        p = jnp.exp2((sc - mx).astype(BF16))
        acc = jnp.dot(p, vb, preferred_element_type=F32)
        for odd in (0, 1):
            rows = slice(odd * tq, (odd + 1) * tq)
            mx_tile = jnp.where(lane == 2 * b + odd, mx[rows], mx_tile)
            den_tile = jnp.where(lane == 2 * b + odd, acc[rows, LANES:], den_tile)
        put_acc(sl, jnp.where(q_is_odd, acc[tq:, :LANES], acc[:tq, :LANES]))
    return mx_tile, den_tile


def _pair_factor(tile, b):
    lane = lax.broadcasted_iota(jnp.int32, tile.shape, 1)
    head = 2 * b + _lane_of_odd_head(lane).astype(jnp.int32)
    return jnp.take_along_axis(tile, head, axis=1, mode="promise_in_bounds")


def _merge_groups(stats, get_accs, put_o):
    top = functools.reduce(jnp.maximum, [mx for mx, _ in stats])
    ws = [jnp.exp2(mx - top) for mx, _ in stats]
    inv = 1.0 / sum(w * den for w, (_, den) in zip(ws, stats))
    fs = [w * inv for w in ws]
    for b in range(ATTN_WIDTH // LANES):
        sl = slice(b * LANES, (b + 1) * LANES)
        put_o(sl, sum(_pair_factor(f, b) * get_acc(sl) for f, get_acc in zip(fs, get_accs)))


def _window_mask(tile, tq, m):
    nk = tq + 2 * RADIUS
    row = lax.broadcasted_iota(jnp.int32, (tq, nk), 0)
    col = lax.broadcasted_iota(jnp.int32, (tq, nk), 1)
    kabs = tile * tq - RADIUS + col
    return _mask_bias((jnp.abs(col - RADIUS - row) <= RADIUS) & (kabs >= 0) & (kabs < m))


def _stack_rows(read, prev_ref, cur_ref, next_ref, lo, cnt):
    halo, cur_n = prev_ref.shape[-2], cur_ref.shape[-2]
    a, b = lo, lo + cnt
    parts = []
    if a < 0:
        parts.append(read(prev_ref, halo + a, halo))
        a = 0
    parts.append(read(cur_ref, a, min(b, cur_n)))
    if b > cur_n:
        parts.append(read(next_ref, 0, b - cur_n))
    return parts


def _kv_getters(stack):
    get_k = lambda sl: stack(sl)
    get_v = lambda sl: stack(slice(ATTN_WIDTH + sl.start, ATTN_WIDTH + sl.stop))
    return get_k, get_v


def _attn16_kernel(q_ref, kvp_ref, kvc_ref, kvn_ref, o_ref, mx_ref, den_ref, *, m):
    tq = ATTN_TQ
    n_sub = q_ref.shape[0] // tq
    for t in range(n_sub):
        rows = slice(t * tq, (t + 1) * tq)
        bias = _window_mask(pl.program_id(2) * n_sub + t, tq, m)

        def stack(cols):
            return jnp.concatenate(_stack_rows(lambda ref, a, b: ref[a:b, cols], kvp_ref, kvc_ref, kvn_ref,
                                               t * tq - RADIUS, tq + 2 * RADIUS), axis=0)

        def put_acc(sl, val):
            o_ref[rows, sl] = val.astype(BF16)

        mx_ref[rows], den_ref[rows] = _attn_pairs(lambda sl: q_ref[rows, sl], *_kv_getters(stack), bias, put_acc)


def _attn4_kernel(q_ref, kvp_ref, kvc_ref, kvn_ref, o_ref, mx_ref, den_ref, *, m):
    ns, halo = q_ref.shape[0], kvp_ref.shape[1]
    rows = ATTN_TQ // ns
    span = rows + 2 * halo
    n_sub = q_ref.shape[1] // rows
    n = lax.broadcasted_iota(jnp.int32, (ns * rows, ns * span), 0)
    c = lax.broadcasted_iota(jnp.int32, (ns * rows, ns * span), 1)
    for t in range(n_sub):
        base = (pl.program_id(2) * n_sub + t) * rows
        rs = slice(t * rows, (t + 1) * rows)
        krow = base - halo + c % span
        rel = ns * (krow - (base + n % rows)) + (c // span - n // rows)
        bias = _mask_bias((jnp.abs(rel) <= RADIUS) & (krow >= 0) & (krow < m))

        def stack(cols):
            parts = []
            for s in range(ns):
                parts += _stack_rows(lambda ref, a, b: ref[s, a:b, cols], kvp_ref, kvc_ref, kvn_ref,
                                     t * rows - halo, span)
            return jnp.concatenate(parts, axis=0)

        flat = lambda ref: (lambda sl: ref[:, rs, sl].reshape(ns * rows, LANES))

        def put_acc(sl, val):
            o_ref[:, rs, sl] = val.astype(BF16).reshape(ns, rows, LANES)

        mx, den = _attn_pairs(flat(q_ref), *_kv_getters(stack), bias, put_acc)
        mx_ref[:, rs, :] = mx.reshape(ns, rows, LANES)
        den_ref[:, rs, :] = den.reshape(ns, rows, LANES)


def _attn1_kernel(q_ref, kvp_ref, kvc_ref, kvn_ref, o_ref, mx_ref, den_ref, *, m):
    tq = ATTN_TQ
    n_sub = q_ref.shape[0] // tq
    for t in range(n_sub):
        rows = slice(t * tq, (t + 1) * tq)
        bias = _window_mask(pl.program_id(1) * n_sub + t, tq, m)

        def stack(cols):
            return jnp.concatenate(_stack_rows(lambda ref, a, b: ref[a:b, cols], kvp_ref, kvc_ref, kvn_ref,
                                               t * tq - RADIUS, tq + 2 * RADIUS), axis=0)

        def put_acc(sl, val):
            o_ref[rows, sl] = val.astype(BF16)

        mx_ref[rows], den_ref[rows] = _attn_pairs(lambda sl: q_ref[rows, sl], *_kv_getters(stack), bias, put_acc)


def _attention(qkv0, qkv12):
    bsz, s, _ = qkv0.shape
    m16 = s // STREAMS
    tq, half = ATTN_TQ, RADIUS
    cur_rows = min(ATTN_SUB * tq, m16)
    params = lambda n: pltpu.CompilerParams(dimension_semantics=("parallel",) * n,
                                            vmem_limit_bytes=VMEM_LIMIT)
    same = lambda i: i

    def halo_idx(cur_n, halo_n, total):
        per = cur_n // halo_n
        return (lambda i: jnp.maximum(i * per - 1, 0)), (lambda i: jnp.minimum((i + 1) * per, total // halo_n - 1))

    def spec16(nrows, row_idx, col, width=ATTN_WIDTH):
        return pl.BlockSpec((None, None, nrows, width), lambda b, r, i: (b, r, row_idx(i), col))
    prv, nxt = halo_idx(cur_rows, half, m16)
    stat16 = jax.ShapeDtypeStruct((bsz, STREAMS, m16, LANES), F32)
    o2, mx2, den2 = pl.pallas_call(
        functools.partial(_attn16_kernel, m=m16),
        out_shape=(jax.ShapeDtypeStruct((bsz, STREAMS, m16, ATTN_WIDTH), BF16), stat16, stat16),
        grid=(bsz, STREAMS, m16 // cur_rows),
        in_specs=[spec16(cur_rows, same, 5), spec16(half, prv, 1, 2 * ATTN_WIDTH),
                  spec16(cur_rows, same, 1, 2 * ATTN_WIDTH), spec16(half, nxt, 1, 2 * ATTN_WIDTH)],
        out_specs=(spec16(cur_rows, same, 0), spec16(cur_rows, same, 0, LANES), spec16(cur_rows, same, 0, LANES)),
        compiler_params=params(3), name="attn_dil16",
    )(*([qkv12] * 4))

    ns = STREAMS // 4
    rows4, halo4 = cur_rows // ns, half // ns

    def view4(a):
        return a.reshape(bsz, ns, 4, m16, a.shape[-1])

    def spec4(nrows, row_idx, col, width=ATTN_WIDTH):
        return pl.BlockSpec((None, ns, None, nrows, width), lambda b, r, i: (b, 0, r, row_idx(i), col))
    prv4, nxt4 = halo_idx(rows4, halo4, m16)
    q4 = view4(qkv12)
    stat4 = jax.ShapeDtypeStruct((bsz, ns, 4, m16, LANES), F32)
    o1, mx1, den1 = pl.pallas_call(
        functools.partial(_attn4_kernel, m=m16),
        out_shape=(jax.ShapeDtypeStruct((bsz, ns, 4, m16, ATTN_WIDTH), BF16), stat4, stat4),
        grid=(bsz, 4, m16 // rows4),
        in_specs=[spec4(rows4, same, 4), spec4(halo4, prv4, 0, 2 * ATTN_WIDTH),
                  spec4(rows4, same, 0, 2 * ATTN_WIDTH), spec4(halo4, nxt4, 0, 2 * ATTN_WIDTH)],
        out_specs=(spec4(rows4, same, 0), spec4(rows4, same, 0, LANES), spec4(rows4, same, 0, LANES)),
        compiler_params=params(3), name="attn_dil4",
    )(q4, q4, q4, q4)
    streams = lambda a: a.reshape(bsz, STREAMS, m16, a.shape[-1])

    cur1 = ATTN_SUB * tq

    def spec1(nrows, row_idx, col, width=ATTN_WIDTH):
        return pl.BlockSpec((None, nrows, width), lambda b, i: (b, row_idx(i), col))
    prv1, nxt1 = halo_idx(cur1, half, s)
    stat1 = jax.ShapeDtypeStruct((bsz, s, LANES), F32)
    group0 = pl.pallas_call(
        functools.partial(_attn1_kernel, m=s),
        out_shape=(jax.ShapeDtypeStruct((bsz, s, ATTN_WIDTH), BF16), stat1, stat1),
        grid=(bsz, s // cur1),
        in_specs=[spec1(cur1, same, 2), spec1(half, prv1, 0, 2 * ATTN_WIDTH),
                  spec1(cur1, same, 0, 2 * ATTN_WIDTH), spec1(half, nxt1, 0, 2 * ATTN_WIDTH)],
        out_specs=(spec1(cur1, same, 0), spec1(cur1, same, 0, LANES), spec1(cur1, same, 0, LANES)),
        compiler_params=params(2), name="attn_dil1",
    )(*([qkv0] * 4))
    return group0, (o2, mx2, den2), (streams(o1), streams(mx1), streams(den1))


def _attn_out_kernel(o0_ref, mx0_ref, den0_ref, o2_ref, mx2_ref, den2_ref, o1_ref, mx1_ref, den1_ref,
                     unperm_ref, z_ref, x_ref, gate_ref, w_ref, out_ref, y_scr, a2_scr, a1_scr, stat_scr):
    tm = x_ref.shape[0]
    tq = ATTN_TQ
    srow, n_srow = tq // STREAMS, mx2_ref.shape[1]
    n_un = unperm_ref.shape[0]
    for src, dst in ((o2_ref, a2_scr), (o1_ref, a1_scr)):
        for u in range(tm // n_un):
            rs = slice(u * n_un // STREAMS, (u + 1) * n_un // STREAMS)
            part = jnp.concatenate([src[r, rs, :] for r in range(STREAMS)], axis=0)
            dst[u * n_un:(u + 1) * n_un, :] = jnp.dot(unperm_ref[...], part, preferred_element_type=F32)
    for k, src in enumerate((mx2_ref, den2_ref, mx1_ref, den1_ref)):
        for r in range(STREAMS):
            stat_scr[k, r * STAT_PITCH:r * STAT_PITCH + n_srow, :] = src[r]
    for t in range(tm // tq):
        rows = slice(t * tq, (t + 1) * tq)

        def natural(k):
            return jnp.concatenate([stat_scr[k, pl.ds(t * srow + ii, STREAMS, stride=STAT_PITCH), :]
                                    for ii in range(srow)], axis=0)

        def put_y(sl, val):
            y_scr[rows, sl] = (val * _silu(z_ref[rows, sl].astype(F32))).astype(BF16)

        stats = [(mx0_ref[rows], den0_ref[rows]), (natural(0), natural(1)), (natural(2), natural(3))]
        _merge_groups(stats, [lambda sl: o0_ref[rows, sl].astype(F32), lambda sl: a2_scr[rows, sl],
                              lambda sl: a1_scr[rows, sl]], put_y)
    out = jnp.dot(y_scr[...], w_ref[...], preferred_element_type=F32)
    out_ref[...] = x_ref[...] + gate_ref[...] * out


def _attn_out(groups, nat, x, mod, w, tm=PERM_ROWS):
    bsz, s, _ = x.shape
    row = lambda width: pl.BlockSpec((None, tm, width), lambda b, i: (b, i, 0))
    sm = lambda width: pl.BlockSpec((None, STREAMS, tm // STREAMS, width), lambda b, i: (b, 0, i, 0))
    z_col = nat.shape[-1] // ATTN_WIDTH - 1
    unperm = jnp.asarray(_stream_perm(UNPERM_ROWS).T, BF16)
    return pl.pallas_call(
        _attn_out_kernel,
        out_shape=jax.ShapeDtypeStruct(x.shape, F32),
        grid=(bsz, s // tm),
        in_specs=[row(ATTN_WIDTH), row(LANES), row(LANES)] + [sm(ATTN_WIDTH), sm(LANES), sm(LANES)] * 2
                 + [pl.BlockSpec((UNPERM_ROWS, UNPERM_ROWS), lambda b, i: (0, 0)),
                    pl.BlockSpec((None, tm, ATTN_WIDTH), lambda b, i: (b, i, z_col)), row(D_MODEL),
                    pl.BlockSpec((None, 1, D_MODEL), lambda b, i: (b, 0, 2)),
                    pl.BlockSpec((ATTN_WIDTH, D_MODEL), lambda b, i: (0, 0))],
        out_specs=row(D_MODEL),
        scratch_shapes=[pltpu.VMEM((tm, ATTN_WIDTH), BF16), pltpu.VMEM((tm, ATTN_WIDTH), F32),
                        pltpu.VMEM((tm, ATTN_WIDTH), F32), pltpu.VMEM((4, STREAMS * STAT_PITCH, LANES), F32)],
        compiler_params=pltpu.CompilerParams(
            dimension_semantics=("parallel", "parallel"),
            vmem_limit_bytes=VMEM_LIMIT),
        name="attn_out",
    )(*groups[0], *groups[1], *groups[2], unperm, nat, x, mod, w)


def _cmul(ar, ai, br, bi):
    return ar * br - ai * bi, ar * bi + ai * br


def _s5_prep_kernel(*refs):
    ins, d_diag, outs, v_scr = refs[:9], refs[9], refs[10:14], refs[14]
    for gi in range(d_diag.shape[0]):
        _s5_prep_group(*[r.at[:, gi] for r in ins], d_diag.at[gi], *[o.at[gi] for o in outs], v_scr.at[gi])


def _s5_prep_group(lam_re_a, lam_im_a, lam_re_b, lam_im_b, logdt, bt_re, bt_im,
                   c_re, c_im, d_diag, toep_ref, mb_ref, mc_ref, al_ref, v_scr):
    lane = lax.broadcasted_iota(jnp.int32, (STATE, LANES), 1)
    a_idx = lane // GROUP_CH
    eye = (lax.broadcasted_iota(jnp.int32, (STATE, STATE), 0)
           == lax.broadcasted_iota(jnp.int32, (STATE, STATE), 1))
    n_tiles = CHUNK // T_PER_VREG

    mb_rows = []
    v_rhs = []
    v_lhs = []
    mc_cols = []
    zbar_f = None
    for dr in range(2):
        dt = jnp.exp(logdt[dr])
        lrb = jnp.minimum(lam_re_b[dr], LAMBDA_RE_MAX)
        mag = jnp.exp(lrb * dt)
        pw_r, pw_i = mag * jnp.cos(lam_im_b[dr] * dt), mag * jnp.sin(lam_im_b[dr] * dt)
        diag = lambda v: jnp.sum(jnp.where(eye, jnp.broadcast_to(v, (STATE, STATE)), 0.0), axis=1, keepdims=True)
        lr = jnp.minimum(lam_re_a[dr], LAMBDA_RE_MAX)
        li = lam_im_a[dr]
        l1r, l1i = diag(pw_r), diag(pw_i)
        small = [(jnp.ones_like(l1r), jnp.zeros_like(l1r)), (l1r, l1i)]
        for _ in range(T_PER_VREG - 1):
            small.append(_cmul(*small[-1], l1r, l1i))

        def lane_powers(exponent_of_step):
            pr, pi = small[exponent_of_step[0]]
            pr, pi = jnp.broadcast_to(pr, (STATE, LANES)), jnp.broadcast_to(pi, (STATE, LANES))
            for a in range(1, T_PER_VREG):
                er, ei = small[exponent_of_step[a]]
                pr, pi = jnp.where(a_idx == a, er, pr), jnp.where(a_idx == a, ei, pi)
            return pr, pi

        den = lr * lr + li * li
        nr, ni = l1r - 1.0, l1i
        cr, ci = (nr * lr + ni * li) / den, (ni * lr - nr * li) / den
        bbr, bbi = _cmul(cr, ci, bt_re[dr], bt_im[dr])
        l8r, l8i = small[T_PER_VREG]
        tiles = [None] * n_tiles
        if dr == 0:
            pr, pi = lane_powers([T_PER_VREG - 1 - a for a in range(T_PER_VREG)])
            cur = _cmul(pr, pi, bbr, bbi)
            for jt in range(n_tiles - 1, -1, -1):
                tiles[jt] = cur
                if jt:
                    cur = _cmul(cur[0], cur[1], l8r, l8i)
            v_rhs.append(_cmul(tiles[n_tiles - 1][0], tiles[n_tiles - 1][1], l1r, l1i))
            zbar_f = (jnp.where(lane < GROUP_CH, bbr, 0.0), jnp.where(lane < GROUP_CH, bbi, 0.0))
        else:
            pr, pi = lane_powers(list(range(T_PER_VREG)))
            cur = _cmul(pr, pi, bbr, bbi)
            for jt in range(n_tiles):
                tiles[jt] = cur
                if jt < n_tiles - 1:
                    cur = _cmul(cur[0], cur[1], l8r, l8i)
            v_rhs.append(tiles[0])
        mb_rows.append(jnp.concatenate([t[0] for t in tiles], axis=1))
        mb_rows.append(jnp.concatenate([t[1] for t in tiles], axis=1))
        a_r, a_i = l8r, l8i
        for _ in range(int(math.log2(CHUNK // T_PER_VREG))):
            a_r, a_i = _cmul(a_r, a_i, a_r, a_i)
        al_ref[2 * dr] = jnp.broadcast_to(a_r, (STATE, LANES))
        al_ref[2 * dr + 1] = jnp.broadcast_to(a_i, (STATE, LANES))

        pows = [(pw_r, pw_i)]
        for _ in range(int(math.log2(CHUNK)) - 1):
            pows.append(_cmul(*pows[-1], *pows[-1]))
        cre, cim = c_re[dr], c_im[dr]
        xr, xi = _cmul(cre, cim, *pows[0])
        for k in range(int(math.log2(CHUNK))):
            yr, yi = _cmul(xr, xi, *pows[k])
            if dr == 0:
                xr, xi = jnp.concatenate([xr, yr], 0), jnp.concatenate([xi, yi], 0)
            else:
                xr, xi = jnp.concatenate([yr, xr], 0), jnp.concatenate([yi, xi], 0)
        mc_cols += [xr, -xi]
        xr, xi = cre, cim
        for k in range(int(math.log2(n_tiles))):
            yr, yi = _cmul(xr, xi, *pows[k + int(math.log2(T_PER_VREG))])
            if dr == 0:
                xr, xi = jnp.concatenate([yr, xr], 0), jnp.concatenate([yi, xi], 0)
            else:
                xr, xi = jnp.concatenate([xr, yr], 0), jnp.concatenate([xi, yi], 0)
        v_lhs.append((xr, xi))

    mb_ref[...] = jnp.concatenate(mb_rows, axis=0).astype(BF16)
    mc_ref[...] = jnp.concatenate(mc_cols, axis=1).astype(BF16)

    def cdot(lhs, rhs):
        return (jnp.dot(lhs[0], rhs[0], precision=HI, preferred_element_type=F32)
                - jnp.dot(lhs[1], rhs[1], precision=HI, preferred_element_type=F32))

    out_f = cdot(v_lhs[0], v_rhs[0])
    zero_rows = jnp.zeros((CHUNK_ROWS // T_PER_VREG - GROUP_CH, STATE), F32)
    lag0_lhs = (jnp.concatenate([c_re[0], zero_rows], 0), jnp.concatenate([c_im[0], zero_rows], 0))
    out_b = cdot(v_lhs[1], v_rhs[1]) + cdot(lag0_lhs, zbar_f)
    out_b = out_b + jnp.concatenate([d_diag[...], jnp.zeros((CHUNK_ROWS // T_PER_VREG - GROUP_CH, LANES), F32)], 0)
    for jt in range(n_tiles):
        rows = slice(jt * GROUP_CH, (jt + 1) * GROUP_CH)
        v_scr[:, jt * LANES:(jt + 1) * LANES] = out_f[rows]
        v_scr[:, (n_tiles + jt) * LANES:(n_tiles + jt + 1) * LANES] = out_b[rows]
    v_scr[:, 2 * CHUNK_ROWS:] = jnp.zeros((GROUP_CH, LANES), F32)

    for k in range(T_PER_VREG):
        vk = v_scr[:, k * GROUP_CH:k * GROUP_CH + 2 * CHUNK_ROWS]
        for t_out in range(CHUNK):
            off = (CHUNK - t_out) * GROUP_CH
            if off % LANES == k * GROUP_CH:
                al_off = off - k * GROUP_CH
                toep_ref[t_out * GROUP_CH:(t_out + 1) * GROUP_CH, :] = (
                    vk[:, al_off:al_off + CHUNK_ROWS].astype(BF16))


def _s5_prep(lam_re, lam_im, log_dt, b_re, b_im, c_re, c_im, d_skip):
    g = SSM_GROUPS
    d_diag = d_skip.reshape(g, GROUP_CH, 1) * jnp.eye(GROUP_CH, LANES, dtype=F32)
    lam_a = lambda a: a.reshape(2, g, STATE, 1)
    lam_b = lambda a: a.reshape(2, g, 1, STATE)
    tile_b = lambda a: jnp.tile(a, (1, 1, 1, T_PER_VREG))
    gs = S5_GROUPS_PER_STEP
    dir_spec = lambda r, c: pl.BlockSpec((2, gs, r, c), lambda i: (0, i, 0, 0))
    return pl.pallas_call(
        _s5_prep_kernel,
        out_shape=(jax.ShapeDtypeStruct((g, CHUNK_ROWS, CHUNK_ROWS), BF16),
                   jax.ShapeDtypeStruct((g, 4 * STATE, CHUNK_ROWS), BF16),
                   jax.ShapeDtypeStruct((g, CHUNK_ROWS, 4 * STATE), BF16),
                   jax.ShapeDtypeStruct((g, 4, STATE, LANES), F32)),
        grid=(g // gs,),
        in_specs=[dir_spec(STATE, 1), dir_spec(STATE, 1), dir_spec(1, STATE), dir_spec(1, STATE),
                  dir_spec(1, 1), dir_spec(STATE, LANES), dir_spec(STATE, LANES),
                  dir_spec(GROUP_CH, STATE), dir_spec(GROUP_CH, STATE),
                  pl.BlockSpec((gs, GROUP_CH, LANES), lambda i: (i, 0, 0))],
        out_specs=(pl.BlockSpec((gs, CHUNK_ROWS, CHUNK_ROWS), lambda i: (i, 0, 0)),
                   pl.BlockSpec((gs, 4 * STATE, CHUNK_ROWS), lambda i: (i, 0, 0)),
                   pl.BlockSpec((gs, CHUNK_ROWS, 4 * STATE), lambda i: (i, 0, 0)),
                   pl.BlockSpec((gs, 4, STATE, LANES), lambda i: (i, 0, 0, 0))),
        scratch_shapes=[pltpu.VMEM((gs, GROUP_CH, V_LANES), F32)],
        compiler_params=pltpu.CompilerParams(dimension_semantics=("parallel",)),
        name="s5_prep",
    )(lam_a(lam_re), lam_a(lam_im), lam_b(lam_re), lam_b(lam_im), log_dt.reshape(2, g, 1, 1),
      tile_b(b_re), tile_b(b_im), c_re, c_im, d_diag)


def _slab_perm():
    dst = np.arange(SLAB)
    perm = np.zeros((SLAB, SLAB), np.float32)
    perm[dst, (dst % SLAB_CHUNKS) * SUBLANES + dst // SLAB_CHUNKS] = 1.0
    return perm


def _s5_in_kernel(x_ref, mod_ref, g_ref, w_ref, p_ref, u_ref, z_ref, h_scr, *, chunks_per_seq):
    n_chunks = x_ref.shape[0]
    for b in range(n_chunks // chunks_per_seq):
        rows = slice(b * chunks_per_seq, (b + 1) * chunks_per_seq)
        h = _rms_modulate(x_ref[rows], g_ref[...], mod_ref[b:b + 1, D_MODEL:2 * D_MODEL],
                          mod_ref[b:b + 1, :D_MODEL])
        h2 = h.reshape(chunks_per_seq * SUBLANES, D_MODEL).astype(BF16)
        for s in range(chunks_per_seq // SLAB_CHUNKS):
            hp = jnp.dot(p_ref[...], h2[s * SLAB:(s + 1) * SLAB], preferred_element_type=F32)
            c0 = b * chunks_per_seq + s * SLAB_CHUNKS
            for k in range(SUBLANES):
                h_scr[k, c0:c0 + SLAB_CHUNKS, :] = hp[k * SLAB_CHUNKS:(k + 1) * SLAB_CHUNKS].astype(BF16)
    for k in range(SUBLANES):
        ht = h_scr[k].astype(F32).T.astype(BF16)
        uz = jnp.dot(w_ref[...], ht, preferred_element_type=F32)
        u_ref[k] = uz[:SSM_WIDTH].astype(BF16)
        z_ref[k] = uz[SSM_WIDTH:].astype(BF16)


def _s5_in(x3, mod, g, w_t, chunks_per_seq):
    n_chunks = x3.shape[0]
    once = lambda shape: pl.BlockSpec(shape, lambda t: (0,) * len(shape), pipeline_mode=pl.Buffered(1))
    out = pl.BlockSpec((SUBLANES, SSM_WIDTH, n_chunks), lambda t: (t, 0, 0))
    return pl.pallas_call(
        functools.partial(_s5_in_kernel, chunks_per_seq=chunks_per_seq),
        out_shape=(jax.ShapeDtypeStruct((CHUNK, SSM_WIDTH, n_chunks), BF16),) * 2,
        grid=(CHUNK // SUBLANES,),
        in_specs=[pl.BlockSpec((n_chunks, SUBLANES, D_MODEL), lambda t: (0, t, 0)),
                  once(mod.shape), once((1, D_MODEL)), once((2 * SSM_WIDTH, D_MODEL)),
                  once((SLAB, SLAB))],
        out_specs=(out, out),
        scratch_shapes=[pltpu.VMEM((SUBLANES, n_chunks, D_MODEL), BF16)],
        compiler_params=pltpu.CompilerParams(
            dimension_semantics=("parallel",), vmem_limit_bytes=VMEM_LIMIT_BIG),
        name="s5_in",
    )(x3, mod, g, w_t, jnp.asarray(_slab_perm(), BF16))


def _gelu_tanh(v):
    return 0.5 * v * (1.0 + jnp.tanh(math.sqrt(2.0 / math.pi) * (v + 0.044715 * (v * v * v))))


def _s5_core_kernel(u_ref, toep_ref, mb_ref, mc_ref, al_ref, o_ref, *, chunks_per_seq):
    for gi in range(u_ref.shape[1]):
        _s5_core_group(u_ref.at[:, gi], toep_ref.at[gi], mb_ref.at[gi], mc_ref.at[gi], al_ref.at[gi],
                       o_ref.at[:, gi], chunks_per_seq)


def _s5_core_group(u_ref, toep_ref, mb_ref, mc_ref, al_ref, o_ref, chunks_per_seq):
    n_chunks = u_ref.shape[-1]
    ub = u_ref[...].reshape(CHUNK_ROWS, n_chunks)
    inc = jnp.dot(mb_ref[...], ub, preferred_element_type=F32)
    pos = lax.broadcasted_iota(jnp.int32, (STATE, n_chunks), 1) % chunks_per_seq
    reps = n_chunks // LANES

    def lane_tile(a):
        return jnp.concatenate([a] * reps, axis=1) if reps > 1 else a

    states = []
    for dr in range(2):
        xr = inc[(2 * dr) * STATE:(2 * dr + 1) * STATE]
        xi = inc[(2 * dr + 1) * STATE:(2 * dr + 2) * STATE]
        ar, ai = al_ref[2 * dr], al_ref[2 * dr + 1]

        def shifted(v, step):
            if dr == 0:
                return jnp.where(pos >= step, pltpu.roll(v, step, 1), 0.0)
            return jnp.where(pos < chunks_per_seq - step, pltpu.roll(v, n_chunks - step, 1), 0.0)

        step = 1
        while step < chunks_per_seq:
            sr, si = shifted(xr, step), shifted(xi, step)
            tr, ti = lane_tile(ar), lane_tile(ai)
            xr, xi = xr + tr * sr - ti * si, xi + tr * si + ti * sr
            ar, ai = _cmul(ar, ai, ar, ai)
            step *= 2
        states += [shifted(xr, 1), shifted(xi, 1)]
    h_in = jnp.concatenate(states, axis=0).astype(BF16)
    y = (jnp.dot(toep_ref[...], ub, preferred_element_type=F32)
         + jnp.dot(mc_ref[...], h_in, preferred_element_type=F32))
    o_ref[...] = _gelu_tanh(y).astype(BF16).reshape(CHUNK, GROUP_CH, n_chunks)


def _s5_core(u_cl, toep, mb, mc, al, chunks_per_seq):
    n_chunks = u_cl.shape[-1]
    u4 = u_cl.reshape(CHUNK, SSM_GROUPS, GROUP_CH, n_chunks)
    gs = S5_GROUPS_PER_STEP
    grp = pl.BlockSpec((CHUNK, gs, GROUP_CH, n_chunks), lambda g: (0, g, 0, 0))
    out = pl.pallas_call(
        functools.partial(_s5_core_kernel, chunks_per_seq=chunks_per_seq),
        out_shape=jax.ShapeDtypeStruct(u4.shape, BF16),
        grid=(SSM_GROUPS // gs,),
        in_specs=[grp,
                  pl.BlockSpec((gs, CHUNK_ROWS, CHUNK_ROWS), lambda g: (g, 0, 0)),
                  pl.BlockSpec((gs, 4 * STATE, CHUNK_ROWS), lambda g: (g, 0, 0)),
                  pl.BlockSpec((gs, CHUNK_ROWS, 4 * STATE), lambda g: (g, 0, 0)),
                  pl.BlockSpec((gs, 4, STATE, LANES), lambda g: (g, 0, 0, 0))],
        out_specs=grp,
        compiler_params=pltpu.CompilerParams(
            dimension_semantics=("parallel",), vmem_limit_bytes=VMEM_LIMIT),
        name="s5_core",
    )(u4, toep, mb, mc, al)
    return out.reshape(CHUNK, SSM_WIDTH, n_chunks)


def _s5_out_kernel(g_ref, z_ref, wglu_ref, wout_ref, pt_ref, x_ref, mod_ref, fg_ref, o_ref, y_scr,
                   *, final, chunks_per_seq):
    q = pl.program_id(1)
    n_half = x_ref.shape[0]
    k_per = g_ref.shape[0]
    n_glu = SUBLANES // k_per

    for qv in range(n_glu):
        @pl.when(q == qv)
        def _():
            for kk in range(k_per):
                gb = g_ref[kk]
                glu = jnp.dot(wglu_ref[...], gb, preferred_element_type=F32)
                y = gb.astype(F32) * _sigmoid(glu) * _silu(z_ref[kk].astype(F32))
                y_scr[qv * k_per + kk] = y.T.astype(BF16)

    for hv in range(y_scr.shape[1] // n_half):
        @pl.when(q == n_glu + hv)
        def _():
            for s in range(n_half // SLAB_CHUNKS):
                c0 = hv * n_half + s * SLAB_CHUNKS
                src = jnp.concatenate([y_scr[k, c0:c0 + SLAB_CHUNKS, :] for k in range(SUBLANES)], axis=0)
                yp = jnp.dot(pt_ref[...], src, preferred_element_type=F32).astype(BF16)
                out = jnp.dot(yp, wout_ref[...], preferred_element_type=F32)
                b = c0 // chunks_per_seq
                upd = mod_ref[b:b + 1, 2 * D_MODEL:] * out
                rows = slice(s * SLAB_CHUNKS, (s + 1) * SLAB_CHUNKS)
                xn = x_ref[rows] + upd.reshape(SLAB_CHUNKS, SUBLANES, D_MODEL)
                if final:
                    xn = xn * lax.rsqrt(jnp.mean(xn * xn, axis=-1, keepdims=True) + NORM_EPS) * fg_ref[...]
                o_ref[rows] = xn


def _s5_out(g_cl, z_cl, wglu_t, w_out, x3, mod, final_g, final, chunks_per_seq, n_split=2):
    n_chunks = x3.shape[0]
    n_half = n_chunks // n_split
    n_glu = SUBLANES // S5_OUT_K
    once = lambda shape: pl.BlockSpec(shape, lambda t, q: (0,) * len(shape), pipeline_mode=pl.Buffered(1))
    act = pl.BlockSpec((S5_OUT_K, SSM_WIDTH, n_chunks), lambda t, q: (t * n_glu + jnp.minimum(q, n_glu - 1), 0, 0))
    row = pl.BlockSpec((n_half, SUBLANES, D_MODEL), lambda t, q: (jnp.maximum(q - n_glu, 0), t, 0))
    return pl.pallas_call(
        functools.partial(_s5_out_kernel, final=final, chunks_per_seq=chunks_per_seq),
        out_shape=jax.ShapeDtypeStruct(x3.shape, F32),
        grid=(CHUNK // SUBLANES, n_glu + n_split),
        in_specs=[act, act, once((SSM_WIDTH, SSM_WIDTH)), once((SSM_WIDTH, D_MODEL)), once((SLAB, SLAB)),
                  row, once(mod.shape), once((1, D_MODEL))],
        out_specs=row,
        scratch_shapes=[pltpu.VMEM((SUBLANES, n_chunks, SSM_WIDTH), BF16)],
        compiler_params=pltpu.CompilerParams(
            dimension_semantics=("parallel", "arbitrary"), vmem_limit_bytes=VMEM_LIMIT_BIG),
        name="s5_out",
    )(g_cl, z_cl, wglu_t, w_out, jnp.asarray(_slab_perm().T, BF16), x3, mod, final_g)


def _rope_tables(s):
    inv_freq = ROPE_THETA ** (-jnp.arange(0, HEAD_DIM, 2, dtype=F32) / HEAD_DIM)
    ang = jnp.arange(s, dtype=F32)[:, None] * inv_freq[None, :]
    reps = LANES // (HEAD_DIM // 2)
    sign = jnp.where(jnp.arange(LANES) < LANES // 2, -1.0, 1.0).astype(F32)
    cos, sin = jnp.tile(jnp.cos(ang), (1, reps)), jnp.tile(jnp.sin(ang), (1, reps)) * sign
    stream_major = lambda a: a.reshape(s // STREAMS, STREAMS, LANES).transpose(1, 0, 2)
    return cos, sin, stream_major(cos), stream_major(sin)


def _prep_attn_w_in(w):
    d_in = w.shape[0]
    w = w.reshape(d_in, -1, ATTN_HEADS // 2, 2, 2, HEAD_DIM // 2)
    w = jnp.stack([w[:, i] for i in ATTN_IN_BLOCK_ORDER], axis=1)
    return w.transpose(0, 1, 2, 4, 3, 5).astype(BF16).reshape(d_in, -1)


def _prep_attn_w_out(w):
    w = w.astype(BF16).reshape(ATTN_HEADS // 2, 2, 2, HEAD_DIM // 2, w.shape[1])
    return w.transpose(0, 2, 1, 3, 4).reshape(ATTN_WIDTH, -1)


def _trunk(x, ada, norm_g, attn_w, s5_w, s5_ops, tables, final_norm_g):
    bsz, s, _ = x.shape
    chunks_per_seq = s // CHUNK
    n_chunks = bsz * chunks_per_seq
    fg = final_norm_g.reshape(1, D_MODEL)
    for i in range(DEPTH):
        mod = ada[i].reshape(bsz, 1, 3 * D_MODEL)
        g = norm_g[i].reshape(1, D_MODEL)
        j = i // N_MIXERS
        if i % N_MIXERS == 0:
            w_in, w_out = attn_w[j]
            nat, qkv12 = _attn_in(x, mod, g, w_in, tables)
            x = _attn_out(_attention(nat, qkv12), nat, x, mod, w_out)
        else:
            w_in_t, wglu_t, w_out = s5_w[j]
            toep, mb, mc, al = s5_ops[j]
            x3 = x.reshape(n_chunks, CHUNK, D_MODEL)
            u_cl, z_cl = _s5_in(x3, ada[i], g, w_in_t, chunks_per_seq)
            g_cl = _s5_core(u_cl, toep, mb, mc, al, chunks_per_seq)
            x3 = _s5_out(g_cl, z_cl, wglu_t, w_out, x3, ada[i], fg, i == DEPTH - 1, chunks_per_seq)
            x = x3.reshape(bsz, s, D_MODEL)
    return x


def kernel(x_prompt, x_sample, c_prompt, c_sample, norm_g, ada_w, ada_b, attn_w_in, attn_w_out,
           ssm_w_in, ssm_lam_re, ssm_lam_im, ssm_log_dt, ssm_b_re, ssm_b_im, ssm_c_re, ssm_c_im,
           ssm_d, ssm_w_glu, ssm_w_out, final_norm_g):
    assert (DEPTH - 1) % N_MIXERS == 1, "the final norm is fused into the last S5 layer"
    assert all(w // (2 * d) == RADIUS for w, d in DILATED_PAIRS)
    assert [d for _, d in DILATED_PAIRS] == [1, 4, STREAMS]
    n_prompt = c_prompt.shape[0]
    ada = _ada(jnp.concatenate([c_prompt, c_sample], axis=0), ada_w, ada_b)
    attn_w = [(_prep_attn_w_in(attn_w_in[j]), _prep_attn_w_out(attn_w_out[j]))
              for j in range(attn_w_in.shape[0])]
    s5_w, s5_ops = [], []
    for j in range(ssm_w_in.shape[0]):
        s5_w.append((ssm_w_in[j].T.astype(BF16), ssm_w_glu[j].T.astype(BF16), ssm_w_out[j].astype(BF16)))
        s5_ops.append(_s5_prep(ssm_lam_re[j], ssm_lam_im[j], ssm_log_dt[j], ssm_b_re[j],
                               ssm_b_im[j], ssm_c_re[j], ssm_c_im[j], ssm_d[j]))
    tables = _rope_tables(max(x_prompt.shape[1], x_sample.shape[1]))
    y_prompt = _trunk(x_prompt, ada[:, :n_prompt], norm_g, attn_w, s5_w, s5_ops, tables, final_norm_g)
    y_sample = _trunk(x_sample, ada[:, n_prompt:], norm_g, attn_w, s5_w, s5_ops, tables, final_norm_g)
    return (y_prompt, y_sample)
```

```python
import functools
import math

import numpy as np
import jax
import jax.numpy as jnp
from jax import lax
from jax.experimental import pallas as pl
from jax.experimental.pallas import tpu as pltpu

D_MODEL = 1024
DEPTH = 4
N_MIXERS = 2
ATTN_HEADS = 16
HEAD_DIM = 64
ATTN_WIDTH = ATTN_HEADS * HEAD_DIM
DILATED_PAIRS = ((128, 1), (512, 4), (2048, 16))
N_DIL = len(DILATED_PAIRS)
ROPE_THETA = 10000.0
SSM_WIDTH = D_MODEL
GROUP_CH = 16
SSM_GROUPS = SSM_WIDTH // GROUP_CH
STATE = 64
LAMBDA_RE_MAX = -1e-4
NORM_EPS = 1e-6
NEG_INF = -1e30

LANES = 128
SUBLANES = 8
CHUNK = 64
CHUNK_ROWS = CHUNK * GROUP_CH
T_PER_VREG = LANES // GROUP_CH
V_LANES = 2 * CHUNK_ROWS + LANES
STREAMS = 16
RADIUS = 64
ATTN_TQ = 128
ATTN_SUB = 4
STAT_PITCH = 40
ATTN_IN_BLOCK_ORDER = (1, 2, 0, 9, 4, 5, 7, 8, 3, 6)
PERM_ROWS = 512
UNPERM_ROWS = 256
LOG2_E = math.log2(math.e)
S5_GROUPS_PER_STEP = 2
S5_OUT_K = 4
SLAB = 256
SLAB_CHUNKS = SLAB // SUBLANES
VMEM_LIMIT = 48 * 1024 * 1024
VMEM_LIMIT_BIG = 56 * 1024 * 1024

F32 = jnp.float32
BF16 = jnp.bfloat16
HI = lax.Precision.HIGHEST


def _sigmoid(v):
    return 0.5 * jnp.tanh(0.5 * v) + 0.5


def _silu(v):
    return v * _sigmoid(v)


def _rms_modulate(x, g, scale, shift):
    rs = lax.rsqrt(jnp.mean(x * x, axis=-1, keepdims=True) + NORM_EPS)
    return (x * rs * g) * (1.0 + scale) + shift


def _ada_kernel(c_ref, w_ref, b_ref, o_ref):
    o_ref[...] = jnp.dot(_silu(c_ref[...]), w_ref[...], precision=HI,
                         preferred_element_type=F32) + b_ref[...]


def _ada(c, ada_w, ada_b):
    nb = c.shape[0]
    return pl.pallas_call(
        _ada_kernel,
        out_shape=jax.ShapeDtypeStruct((DEPTH, nb, 3 * D_MODEL), F32),
        grid=(DEPTH, 3),
        in_specs=[
            pl.BlockSpec((nb, D_MODEL), lambda i, j: (0, 0)),
            pl.BlockSpec((None, D_MODEL, D_MODEL), lambda i, j: (i, 0, j)),
            pl.BlockSpec((None, 1, D_MODEL), lambda i, j: (i, 0, j)),
        ],
        out_specs=pl.BlockSpec((None, nb, D_MODEL), lambda i, j: (i, 0, j)),
        name="ada",
    )(c, ada_w, ada_b.reshape(DEPTH, 1, 3 * D_MODEL))


def _attn_in_kernel(x_ref, shift_ref, scale_ref, g_ref, w_ref, cos_ref, sin_ref, cosp_ref, sinp_ref,
                    perm_ref, nat_ref, sm_ref, h_nat, h_perm):
    j = pl.program_id(2)
    tm = x_ref.shape[0]
    n_sub = tm // PERM_ROWS
    sub_rows = PERM_ROWS // STREAMS

    @pl.when(j == 0)
    def _():
        h = _rms_modulate(x_ref[...], g_ref[...], scale_ref[...], shift_ref[...])
        h_nat[...] = h.astype(BF16)
        for u in range(n_sub):
            rs = slice(u * PERM_ROWS, (u + 1) * PERM_ROWS)
            h_perm[rs, :] = jnp.dot(perm_ref[...], h_nat[rs, :], preferred_element_type=F32).astype(BF16)

    def rope_blocks(acc, cos, sin, second_is_plain):
        for half in range(2):
            is_q = jnp.logical_or(j == 4, jnp.logical_and(j == 1, half == 0))
            qscale = jnp.where(is_q, HEAD_DIM ** -0.5 * LOG2_E, 1.0).astype(F32)
            c, sn = cos * qscale, sin * qscale
            for b in range(ATTN_WIDTH // LANES):
                sl = slice(half * ATTN_WIDTH + b * LANES, half * ATTN_WIDTH + (b + 1) * LANES)
                t = acc[:, sl]
                if half == 1 and second_is_plain:
                    yield sl, t.astype(BF16)
                else:
                    yield sl, (t * c + pltpu.roll(t, LANES // 2, 1) * sn).astype(BF16)

    @pl.when(j < 2)
    def _():
        acc = jnp.dot(h_nat[...], w_ref[...], preferred_element_type=F32)
        for sl, blk in rope_blocks(acc, cos_ref[...], sin_ref[...], True):
            nat_ref[:, sl] = blk

    def stream_major_step(second_is_plain):
        acc = jnp.dot(h_perm[...], w_ref[...], preferred_element_type=F32)

        def table(ref):
            return jnp.concatenate([ref[:, u * sub_rows:(u + 1) * sub_rows, :].reshape(PERM_ROWS, LANES)
                                    for u in range(n_sub)], axis=0)

        for sl, blk in rope_blocks(acc, table(cosp_ref), table(sinp_ref), second_is_plain):
            for u in range(n_sub):
                sm_ref[:, u * sub_rows:(u + 1) * sub_rows, sl] = (
                    blk[u * PERM_ROWS:(u + 1) * PERM_ROWS].reshape(STREAMS, sub_rows, LANES))

    pl.when(jnp.logical_and(j >= 2, j < 4))(lambda: stream_major_step(True))
    pl.when(j == 4)(lambda: stream_major_step(False))


def _stream_perm(n):
    dst = np.arange(n)
    rows = n // STREAMS
    perm = np.zeros((n, n), np.float32)
    perm[dst, STREAMS * (dst % rows) + dst // rows] = 1.0
    return perm


def _attn_in(x, mod, g, w, tables, tm=1024):
    bsz, s, _ = x.shape
    cos, sin, cosp, sinp = tables
    rows = tm // STREAMS
    wide = 2 * ATTN_WIDTH
    perm = jnp.asarray(_stream_perm(PERM_ROWS), BF16)
    return pl.pallas_call(
        _attn_in_kernel,
        out_shape=(jax.ShapeDtypeStruct((bsz, s, 2 * wide), BF16),
                   jax.ShapeDtypeStruct((bsz, STREAMS, s // STREAMS, 3 * wide), BF16)),
        grid=(bsz, s // tm, 5),
        in_specs=[
            pl.BlockSpec((None, tm, D_MODEL), lambda b, i, j: (b, i, 0)),
            pl.BlockSpec((None, 1, D_MODEL), lambda b, i, j: (b, 0, 0)),
            pl.BlockSpec((None, 1, D_MODEL), lambda b, i, j: (b, 0, 1)),
            pl.BlockSpec((1, D_MODEL), lambda b, i, j: (0, 0)),
            pl.BlockSpec((D_MODEL, wide), lambda b, i, j: (0, j)),
            pl.BlockSpec((tm, LANES), lambda b, i, j: (i, 0)),
            pl.BlockSpec((tm, LANES), lambda b, i, j: (i, 0)),
            pl.BlockSpec((STREAMS, rows, LANES), lambda b, i, j: (0, i, 0)),
            pl.BlockSpec((STREAMS, rows, LANES), lambda b, i, j: (0, i, 0)),
            pl.BlockSpec((PERM_ROWS, PERM_ROWS), lambda b, i, j: (0, 0)),
        ],
        out_specs=(
            pl.BlockSpec((None, tm, wide), lambda b, i, j: (b, i, jnp.clip(j, 0, 1))),
            pl.BlockSpec((None, STREAMS, rows, wide), lambda b, i, j: (b, 0, i, jnp.clip(j - 2, 0, 2))),
        ),
        scratch_shapes=[pltpu.VMEM((tm, D_MODEL), BF16), pltpu.VMEM((tm, D_MODEL), BF16)],
        compiler_params=pltpu.CompilerParams(
            dimension_semantics=("parallel", "parallel", "arbitrary"),
            vmem_limit_bytes=VMEM_LIMIT_BIG),
        name="attn_in",
    )(x, mod, mod, g, w, cos, sin, cosp, sinp, perm)


def _lane_of_odd_head(lane):
    return (lane // (HEAD_DIM // 2)) % 2 == 1


def _mask_bias(valid):
    return jnp.where(valid, 0.0, NEG_INF).astype(F32)


def _attn_pairs(get_q, get_k, get_v, bias, put_acc):
    tq = bias.shape[0]
    lane = lax.broadcasted_iota(jnp.int32, (tq, LANES), 1)
    q_is_odd = _lane_of_odd_head(lane)
    mx_tile = jnp.zeros((tq, LANES), F32)
    den_tile = jnp.ones((tq, LANES), F32)
    ones = jnp.ones((bias.shape[1], LANES), BF16)
    bias2 = jnp.concatenate([bias, bias], axis=0)
    for b in range(ATTN_WIDTH // LANES):
        sl = slice(b * LANES, (b + 1) * LANES)
        qb = get_q(sl).astype(F32)
        kb = get_k(sl)
        vb = jnp.concatenate([get_v(sl), ones], axis=1)
        qm = jnp.concatenate([jnp.where(q_is_odd, 0.0, qb), jnp.where(q_is_odd, qb, 0.0)], axis=0).astype(BF16)
        sc = lax.dot_general(qm, kb, (((1,), (1,)), ((), ())), preferred_element_type=F32) + bias2
        mx = jnp.max(sc, axis=1, keepdims=True)
        p = jnp.exp2((sc - mx).astype(BF16))
        acc = jnp.dot(p, vb, preferred_element_type=F32)
        for odd in (0, 1):
            rows = slice(odd * tq, (odd + 1) * tq)
            mx_tile = jnp.where(lane == 2 * b + odd, mx[rows], mx_tile)
            den_tile = jnp.where(lane == 2 * b + odd, acc[rows, LANES:], den_tile)
        put_acc(sl, jnp.where(q_is_odd, acc[tq:, :LANES], acc[:tq, :LANES]))
    return mx_tile, den_tile


def _pair_factor(tile, b):
    lane = lax.broadcasted_iota(jnp.int32, tile.shape, 1)
    head = 2 * b + _lane_of_odd_head(lane).astype(jnp.int32)
    return jnp.take_along_axis(tile, head, axis=1, mode="promise_in_bounds")


def _merge_groups(stats, get_accs, put_o):
    top = functools.reduce(jnp.maximum, [mx for mx, _ in stats])
    ws = [jnp.exp2(mx - top) for mx, _ in stats]
    inv = 1.0 / sum(w * den for w, (_, den) in zip(ws, stats))
    fs = [w * inv for w in ws]
    for b in range(ATTN_WIDTH // LANES):
        sl = slice(b * LANES, (b + 1) * LANES)
        put_o(sl, sum(_pair_factor(f, b) * get_acc(sl) for f, get_acc in zip(fs, get_accs)))


def _window_mask(tile, tq, m):
    nk = tq + 2 * RADIUS
    row = lax.broadcasted_iota(jnp.int32, (tq, nk), 0)
    col = lax.broadcasted_iota(jnp.int32, (tq, nk), 1)
    kabs = tile * tq - RADIUS + col
    return _mask_bias((jnp.abs(col - RADIUS - row) <= RADIUS) & (kabs >= 0) & (kabs < m))


def _stack_rows(read, prev_ref, cur_ref, next_ref, lo, cnt):
    halo, cur_n = prev_ref.shape[-2], cur_ref.shape[-2]
    a, b = lo, lo + cnt
    parts = []
    if a < 0:
        parts.append(read(prev_ref, halo + a, halo))
        a = 0
    parts.append(read(cur_ref, a, min(b, cur_n)))
    if b > cur_n:
        parts.append(read(next_ref, 0, b - cur_n))
    return parts


def _kv_getters(stack):
    get_k = lambda sl: stack(sl)
    get_v = lambda sl: stack(slice(ATTN_WIDTH + sl.start, ATTN_WIDTH + sl.stop))
    return get_k, get_v


def _attn16_kernel(q_ref, kvp_ref, kvc_ref, kvn_ref, o_ref, mx_ref, den_ref, *, m):
    tq = ATTN_TQ
    n_sub = q_ref.shape[0] // tq
    for t in range(n_sub):
        rows = slice(t * tq, (t + 1) * tq)
        bias = _window_mask(pl.program_id(2) * n_sub + t, tq, m)

        def stack(cols):
            return jnp.concatenate(_stack_rows(lambda ref, a, b: ref[a:b, cols], kvp_ref, kvc_ref, kvn_ref,
                                               t * tq - RADIUS, tq + 2 * RADIUS), axis=0)

        def put_acc(sl, val):
            o_ref[rows, sl] = val.astype(BF16)

        mx_ref[rows], den_ref[rows] = _attn_pairs(lambda sl: q_ref[rows, sl], *_kv_getters(stack), bias, put_acc)


def _attn4_kernel(q_ref, kvp_ref, kvc_ref, kvn_ref, o_ref, mx_ref, den_ref, *, m):
    ns, halo = q_ref.shape[0], kvp_ref.shape[1]
    rows = ATTN_TQ // ns
    span = rows + 2 * halo
    n_sub = q_ref.shape[1] // rows
    n = lax.broadcasted_iota(jnp.int32, (ns * rows, ns * span), 0)
    c = lax.broadcasted_iota(jnp.int32, (ns * rows, ns * span), 1)
    for t in range(n_sub):
        base = (pl.program_id(2) * n_sub + t) * rows
        rs = slice(t * rows, (t + 1) * rows)
        krow = base - halo + c % span
        rel = ns * (krow - (base + n % rows)) + (c // span - n // rows)
        bias = _mask_bias((jnp.abs(rel) <= RADIUS) & (krow >= 0) & (krow < m))

        def stack(cols):
            parts = []
            for s in range(ns):
                parts += _stack_rows(lambda ref, a, b: ref[s, a:b, cols], kvp_ref, kvc_ref, kvn_ref,
                                     t * rows - halo, span)
            return jnp.concatenate(parts, axis=0)

        flat = lambda ref: (lambda sl: ref[:, rs, sl].reshape(ns * rows, LANES))

        def put_acc(sl, val):
            o_ref[:, rs, sl] = val.astype(BF16).reshape(ns, rows, LANES)

        mx, den = _attn_pairs(flat(q_ref), *_kv_getters(stack), bias, put_acc)
        mx_ref[:, rs, :] = mx.reshape(ns, rows, LANES)
        den_ref[:, rs, :] = den.reshape(ns, rows, LANES)


def _attn1_kernel(q_ref, kvp_ref, kvc_ref, kvn_ref, o_ref, mx_ref, den_ref, *, m):
    tq = ATTN_TQ
    n_sub = q_ref.shape[0] // tq
    for t in range(n_sub):
        rows = slice(t * tq, (t + 1) * tq)
        bias = _window_mask(pl.program_id(1) * n_sub + t, tq, m)

        def stack(cols):
            return jnp.concatenate(_stack_rows(lambda ref, a, b: ref[a:b, cols], kvp_ref, kvc_ref, kvn_ref,
                                               t * tq - RADIUS, tq + 2 * RADIUS), axis=0)

        def put_acc(sl, val):
            o_ref[rows, sl] = val.astype(BF16)

        mx_ref[rows], den_ref[rows] = _attn_pairs(lambda sl: q_ref[rows, sl], *_kv_getters(stack), bias, put_acc)


def _attention(qkv0, qkv12):
    bsz, s, _ = qkv0.shape
    m16 = s // STREAMS
    tq, half = ATTN_TQ, RADIUS
    cur_rows = min(ATTN_SUB * tq, m16)
    params = lambda n: pltpu.CompilerParams(dimension_semantics=("parallel",) * n,
                                            vmem_limit_bytes=VMEM_LIMIT)
    same = lambda i: i

    def halo_idx(cur_n, halo_n, total):
        per = cur_n // halo_n
        return (lambda i: jnp.maximum(i * per - 1, 0)), (lambda i: jnp.minimum((i + 1) * per, total // halo_n - 1))

    def spec16(nrows, row_idx, col, width=ATTN_WIDTH):
        return pl.BlockSpec((None, None, nrows, width), lambda b, r, i: (b, r, row_idx(i), col))
    prv, nxt = halo_idx(cur_rows, half, m16)
    stat16 = jax.ShapeDtypeStruct((bsz, STREAMS, m16, LANES), F32)
    o2, mx2, den2 = pl.pallas_call(
        functools.partial(_attn16_kernel, m=m16),
        out_shape=(jax.ShapeDtypeStruct((bsz, STREAMS, m16, ATTN_WIDTH), BF16), stat16, stat16),
        grid=(bsz, STREAMS, m16 // cur_rows),
        in_specs=[spec16(cur_rows, same, 5), spec16(half, prv, 1, 2 * ATTN_WIDTH),
                  spec16(cur_rows, same, 1, 2 * ATTN_WIDTH), spec16(half, nxt, 1, 2 * ATTN_WIDTH)],
        out_specs=(spec16(cur_rows, same, 0), spec16(cur_rows, same, 0, LANES), spec16(cur_rows, same, 0, LANES)),
        compiler_params=params(3), name="attn_dil16",
    )(*([qkv12] * 4))

    ns = STREAMS // 4
    rows4, halo4 = cur_rows // ns, half // ns

    def view4(a):
        return a.reshape(bsz, ns, 4, m16, a.shape[-1])

    def spec4(nrows, row_idx, col, width=ATTN_WIDTH):
        return pl.BlockSpec((None, ns, None, nrows, width), lambda b, r, i: (b, 0, r, row_idx(i), col))
    prv4, nxt4 = halo_idx(rows4, halo4, m16)
    q4 = view4(qkv12)
    stat4 = jax.ShapeDtypeStruct((bsz, ns, 4, m16, LANES), F32)
    o1, mx1, den1 = pl.pallas_call(
        functools.partial(_attn4_kernel, m=m16),
        out_shape=(jax.ShapeDtypeStruct((bsz, ns, 4, m16, ATTN_WIDTH), BF16), stat4, stat4),
        grid=(bsz, 4, m16 // rows4),
        in_specs=[spec4(rows4, same, 4), spec4(halo4, prv4, 0, 2 * ATTN_WIDTH),
                  spec4(rows4, same, 0, 2 * ATTN_WIDTH), spec4(halo4, nxt4, 0, 2 * ATTN_WIDTH)],
        out_specs=(spec4(rows4, same, 0), spec4(rows4, same, 0, LANES), spec4(rows4, same, 0, LANES)),
        compiler_params=params(3), name="attn_dil4",
    )(q4, q4, q4, q4)
    streams = lambda a: a.reshape(bsz, STREAMS, m16, a.shape[-1])

    cur1 = ATTN_SUB * tq

    def spec1(nrows, row_idx, col, width=ATTN_WIDTH):
        return pl.BlockSpec((None, nrows, width), lambda b, i: (b, row_idx(i), col))
    prv1, nxt1 = halo_idx(cur1, half, s)
    stat1 = jax.ShapeDtypeStruct((bsz, s, LANES), F32)
    group0 = pl.pallas_call(
        functools.partial(_attn1_kernel, m=s),
        out_shape=(jax.ShapeDtypeStruct((bsz, s, ATTN_WIDTH), BF16), stat1, stat1),
        grid=(bsz, s // cur1),
        in_specs=[spec1(cur1, same, 2), spec1(half, prv1, 0, 2 * ATTN_WIDTH),
                  spec1(cur1, same, 0, 2 * ATTN_WIDTH), spec1(half, nxt1, 0, 2 * ATTN_WIDTH)],
        out_specs=(spec1(cur1, same, 0), spec1(cur1, same, 0, LANES), spec1(cur1, same, 0, LANES)),
        compiler_params=params(2), name="attn_dil1",
    )(*([qkv0] * 4))
    return group0, (o2, mx2, den2), (streams(o1), streams(mx1), streams(den1))


def _attn_out_kernel(o0_ref, mx0_ref, den0_ref, o2_ref, mx2_ref, den2_ref, o1_ref, mx1_ref, den1_ref,
                     unperm_ref, z_ref, x_ref, gate_ref, w_ref, out_ref, y_scr, a2_scr, a1_scr, stat_scr):
    tm = x_ref.shape[0]
    tq = ATTN_TQ
    srow, n_srow = tq // STREAMS, mx2_ref.shape[1]
    n_un = unperm_ref.shape[0]
    for src, dst in ((o2_ref, a2_scr), (o1_ref, a1_scr)):
        for u in range(tm // n_un):
            rs = slice(u * n_un // STREAMS, (u + 1) * n_un // STREAMS)
            part = jnp.concatenate([src[r, rs, :] for r in range(STREAMS)], axis=0)
            dst[u * n_un:(u + 1) * n_un, :] = jnp.dot(unperm_ref[...], part, preferred_element_type=F32)
    for k, src in enumerate((mx2_ref, den2_ref, mx1_ref, den1_ref)):
        for r in range(STREAMS):
            stat_scr[k, r * STAT_PITCH:r * STAT_PITCH + n_srow, :] = src[r]
    for t in range(tm // tq):
        rows = slice(t * tq, (t + 1) * tq)

        def natural(k):
            return jnp.concatenate([stat_scr[k, pl.ds(t * srow + ii, STREAMS, stride=STAT_PITCH), :]
                                    for ii in range(srow)], axis=0)

        def put_y(sl, val):
            y_scr[rows, sl] = (val * _silu(z_ref[rows, sl].astype(F32))).astype(BF16)

        stats = [(mx0_ref[rows], den0_ref[rows]), (natural(0), natural(1)), (natural(2), natural(3))]
        _merge_groups(stats, [lambda sl: o0_ref[rows, sl].astype(F32), lambda sl: a2_scr[rows, sl],
                              lambda sl: a1_scr[rows, sl]], put_y)
    out = jnp.dot(y_scr[...], w_ref[...], preferred_element_type=F32)
    out_ref[...] = x_ref[...] + gate_ref[...] * out


def _attn_out(groups, nat, x, mod, w, tm=PERM_ROWS):
    bsz, s, _ = x.shape
    row = lambda width: pl.BlockSpec((None, tm, width), lambda b, i: (b, i, 0))
    sm = lambda width: pl.BlockSpec((None, STREAMS, tm // STREAMS, width), lambda b, i: (b, 0, i, 0))
    z_col = nat.shape[-1] // ATTN_WIDTH - 1
    unperm = jnp.asarray(_stream_perm(UNPERM_ROWS).T, BF16)
    return pl.pallas_call(
        _attn_out_kernel,
        out_shape=jax.ShapeDtypeStruct(x.shape, F32),
        grid=(bsz, s // tm),
        in_specs=[row(ATTN_WIDTH), row(LANES), row(LANES)] + [sm(ATTN_WIDTH), sm(LANES), sm(LANES)] * 2
                 + [pl.BlockSpec((UNPERM_ROWS, UNPERM_ROWS), lambda b, i: (0, 0)),
                    pl.BlockSpec((None, tm, ATTN_WIDTH), lambda b, i: (b, i, z_col)), row(D_MODEL),
                    pl.BlockSpec((None, 1, D_MODEL), lambda b, i: (b, 0, 2)),
                    pl.BlockSpec((ATTN_WIDTH, D_MODEL), lambda b, i: (0, 0))],
        out_specs=row(D_MODEL),
        scratch_shapes=[pltpu.VMEM((tm, ATTN_WIDTH), BF16), pltpu.VMEM((tm, ATTN_WIDTH), F32),
                        pltpu.VMEM((tm, ATTN_WIDTH), F32), pltpu.VMEM((4, STREAMS * STAT_PITCH, LANES), F32)],
        compiler_params=pltpu.CompilerParams(
            dimension_semantics=("parallel", "parallel"),
            vmem_limit_bytes=VMEM_LIMIT),
        name="attn_out",
    )(*groups[0], *groups[1], *groups[2], unperm, nat, x, mod, w)


def _cmul(ar, ai, br, bi):
    return ar * br - ai * bi, ar * bi + ai * br


def _s5_prep_kernel(*refs):
    ins, d_diag, outs, v_scr = refs[:9], refs[9], refs[10:14], refs[14]
    for gi in range(d_diag.shape[0]):
        _s5_prep_group(*[r.at[:, gi] for r in ins], d_diag.at[gi], *[o.at[gi] for o in outs], v_scr.at[gi])


def _s5_prep_group(lam_re_a, lam_im_a, lam_re_b, lam_im_b, logdt, bt_re, bt_im,
                   c_re, c_im, d_diag, toep_ref, mb_ref, mc_ref, al_ref, v_scr):
    lane = lax.broadcasted_iota(jnp.int32, (STATE, LANES), 1)
    a_idx = lane // GROUP_CH
    eye = (lax.broadcasted_iota(jnp.int32, (STATE, STATE), 0)
           == lax.broadcasted_iota(jnp.int32, (STATE, STATE), 1))
    n_tiles = CHUNK // T_PER_VREG

    mb_rows = []
    v_rhs = []
    v_lhs = []
    mc_cols = []
    zbar_f = None
    for dr in range(2):
        dt = jnp.exp(logdt[dr])
        lrb = jnp.minimum(lam_re_b[dr], LAMBDA_RE_MAX)
        mag = jnp.exp(lrb * dt)
        pw_r, pw_i = mag * jnp.cos(lam_im_b[dr] * dt), mag * jnp.sin(lam_im_b[dr] * dt)
        diag = lambda v: jnp.sum(jnp.where(eye, jnp.broadcast_to(v, (STATE, STATE)), 0.0), axis=1, keepdims=True)
        lr = jnp.minimum(lam_re_a[dr], LAMBDA_RE_MAX)
        li = lam_im_a[dr]
        l1r, l1i = diag(pw_r), diag(pw_i)
        small = [(jnp.ones_like(l1r), jnp.zeros_like(l1r)), (l1r, l1i)]
        for _ in range(T_PER_VREG - 1):
            small.append(_cmul(*small[-1], l1r, l1i))

        def lane_powers(exponent_of_step):
            pr, pi = small[exponent_of_step[0]]
            pr, pi = jnp.broadcast_to(pr, (STATE, LANES)), jnp.broadcast_to(pi, (STATE, LANES))
            for a in range(1, T_PER_VREG):
                er, ei = small[exponent_of_step[a]]
                pr, pi = jnp.where(a_idx == a, er, pr), jnp.where(a_idx == a, ei, pi)
            return pr, pi

        den = lr * lr + li * li
        nr, ni = l1r - 1.0, l1i
        cr, ci = (nr * lr + ni * li) / den, (ni * lr - nr * li) / den
        bbr, bbi = _cmul(cr, ci, bt_re[dr], bt_im[dr])
        l8r, l8i = small[T_PER_VREG]
        tiles = [None] * n_tiles
        if dr == 0:
            pr, pi = lane_powers([T_PER_VREG - 1 - a for a in range(T_PER_VREG)])
            cur = _cmul(pr, pi, bbr, bbi)
            for jt in range(n_tiles - 1, -1, -1):
                tiles[jt] = cur
                if jt:
                    cur = _cmul(cur[0], cur[1], l8r, l8i)
            v_rhs.append(_cmul(tiles[n_tiles - 1][0], tiles[n_tiles - 1][1], l1r, l1i))
            zbar_f = (jnp.where(lane < GROUP_CH, bbr, 0.0), jnp.where(lane < GROUP_CH, bbi, 0.0))
        else:
            pr, pi = lane_powers(list(range(T_PER_VREG)))
            cur = _cmul(pr, pi, bbr, bbi)
            for jt in range(n_tiles):
                tiles[jt] = cur
                if jt < n_tiles - 1:
                    cur = _cmul(cur[0], cur[1], l8r, l8i)
            v_rhs.append(tiles[0])
        mb_rows.append(jnp.concatenate([t[0] for t in tiles], axis=1))
        mb_rows.append(jnp.concatenate([t[1] for t in tiles], axis=1))
        a_r, a_i = l8r, l8i
        for _ in range(int(math.log2(CHUNK // T_PER_VREG))):
            a_r, a_i = _cmul(a_r, a_i, a_r, a_i)
        al_ref[2 * dr] = jnp.broadcast_to(a_r, (STATE, LANES))
        al_ref[2 * dr + 1] = jnp.broadcast_to(a_i, (STATE, LANES))

        pows = [(pw_r, pw_i)]
        for _ in range(int(math.log2(CHUNK)) - 1):
            pows.append(_cmul(*pows[-1], *pows[-1]))
        cre, cim = c_re[dr], c_im[dr]
        xr, xi = _cmul(cre, cim, *pows[0])
        for k in range(int(math.log2(CHUNK))):
            yr, yi = _cmul(xr, xi, *pows[k])
            if dr == 0:
                xr, xi = jnp.concatenate([xr, yr], 0), jnp.concatenate([xi, yi], 0)
            else:
                xr, xi = jnp.concatenate([yr, xr], 0), jnp.concatenate([yi, xi], 0)
        mc_cols += [xr, -xi]
        xr, xi = cre, cim
        for k in range(int(math.log2(n_tiles))):
            yr, yi = _cmul(xr, xi, *pows[k + int(math.log2(T_PER_VREG))])
            if dr == 0:
                xr, xi = jnp.concatenate([yr, xr], 0), jnp.concatenate([yi, xi], 0)
            else:
                xr, xi = jnp.concatenate([xr, yr], 0), jnp.concatenate([xi, yi], 0)
        v_lhs.append((xr, xi))

    mb_ref[...] = jnp.concatenate(mb_rows, axis=0).astype(BF16)
    mc_ref[...] = jnp.concatenate(mc_cols, axis=1).astype(BF16)

    def cdot(lhs, rhs):
        return (jnp.dot(lhs[0], rhs[0], precision=HI, preferred_element_type=F32)
                - jnp.dot(lhs[1], rhs[1], precision=HI, preferred_element_type=F32))

    out_f = cdot(v_lhs[0], v_rhs[0])
    zero_rows = jnp.zeros((CHUNK_ROWS // T_PER_VREG - GROUP_CH, STATE), F32)
    lag0_lhs = (jnp.concatenate([c_re[0], zero_rows], 0), jnp.concatenate([c_im[0], zero_rows], 0))
    out_b = cdot(v_lhs[1], v_rhs[1]) + cdot(lag0_lhs, zbar_f)
    out_b = out_b + jnp.concatenate([d_diag[...], jnp.zeros((CHUNK_ROWS // T_PER_VREG - GROUP_CH, LANES), F32)], 0)
    for jt in range(n_tiles):
        rows = slice(jt * GROUP_CH, (jt + 1) * GROUP_CH)
        v_scr[:, jt * LANES:(jt + 1) * LANES] = out_f[rows]
        v_scr[:, (n_tiles + jt) * LANES:(n_tiles + jt + 1) * LANES] = out_b[rows]
    v_scr[:, 2 * CHUNK_ROWS:] = jnp.zeros((GROUP_CH, LANES), F32)

    for k in range(T_PER_VREG):
        vk = v_scr[:, k * GROUP_CH:k * GROUP_CH + 2 * CHUNK_ROWS]
        for t_out in range(CHUNK):
            off = (CHUNK - t_out) * GROUP_CH
            if off % LANES == k * GROUP_CH:
                al_off = off - k * GROUP_CH
                toep_ref[t_out * GROUP_CH:(t_out + 1) * GROUP_CH, :] = (
                    vk[:, al_off:al_off + CHUNK_ROWS].astype(BF16))


def _s5_prep(lam_re, lam_im, log_dt, b_re, b_im, c_re, c_im, d_skip):
    g = SSM_GROUPS
    d_diag = d_skip.reshape(g, GROUP_CH, 1) * jnp.eye(GROUP_CH, LANES, dtype=F32)
    lam_a = lambda a: a.reshape(2, g, STATE, 1)
    lam_b = lambda a: a.reshape(2, g, 1, STATE)
    tile_b = lambda a: jnp.tile(a, (1, 1, 1, T_PER_VREG))
    gs = S5_GROUPS_PER_STEP
    dir_spec = lambda r, c: pl.BlockSpec((2, gs, r, c), lambda i: (0, i, 0, 0))
    return pl.pallas_call(
        _s5_prep_kernel,
        out_shape=(jax.ShapeDtypeStruct((g, CHUNK_ROWS, CHUNK_ROWS), BF16),
                   jax.ShapeDtypeStruct((g, 4 * STATE, CHUNK_ROWS), BF16),
                   jax.ShapeDtypeStruct((g, CHUNK_ROWS, 4 * STATE), BF16),
                   jax.ShapeDtypeStruct((g, 4, STATE, LANES), F32)),
        grid=(g // gs,),
        in_specs=[dir_spec(STATE, 1), dir_spec(STATE, 1), dir_spec(1, STATE), dir_spec(1, STATE),
                  dir_spec(1, 1), dir_spec(STATE, LANES), dir_spec(STATE, LANES),
                  dir_spec(GROUP_CH, STATE), dir_spec(GROUP_CH, STATE),
                  pl.BlockSpec((gs, GROUP_CH, LANES), lambda i: (i, 0, 0))],
        out_specs=(pl.BlockSpec((gs, CHUNK_ROWS, CHUNK_ROWS), lambda i: (i, 0, 0)),
                   pl.BlockSpec((gs, 4 * STATE, CHUNK_ROWS), lambda i: (i, 0, 0)),
                   pl.BlockSpec((gs, CHUNK_ROWS, 4 * STATE), lambda i: (i, 0, 0)),
                   pl.BlockSpec((gs, 4, STATE, LANES), lambda i: (i, 0, 0, 0))),
        scratch_shapes=[pltpu.VMEM((gs, GROUP_CH, V_LANES), F32)],
        compiler_params=pltpu.CompilerParams(dimension_semantics=("parallel",)),
        name="s5_prep",
    )(lam_a(lam_re), lam_a(lam_im), lam_b(lam_re), lam_b(lam_im), log_dt.reshape(2, g, 1, 1),
      tile_b(b_re), tile_b(b_im), c_re, c_im, d_diag)


def _slab_perm():
    dst = np.arange(SLAB)
    perm = np.zeros((SLAB, SLAB), np.float32)
    perm[dst, (dst % SLAB_CHUNKS) * SUBLANES + dst // SLAB_CHUNKS] = 1.0
    return perm


def _s5_in_kernel(x_ref, mod_ref, g_ref, w_ref, p_ref, u_ref, z_ref, h_scr, *, chunks_per_seq):
    n_chunks = x_ref.shape[0]
    for b in range(n_chunks // chunks_per_seq):
        rows = slice(b * chunks_per_seq, (b + 1) * chunks_per_seq)
        h = _rms_modulate(x_ref[rows], g_ref[...], mod_ref[b:b + 1, D_MODEL:2 * D_MODEL],
                          mod_ref[b:b + 1, :D_MODEL])
        h2 = h.reshape(chunks_per_seq * SUBLANES, D_MODEL).astype(BF16)
        for s in range(chunks_per_seq // SLAB_CHUNKS):
            hp = jnp.dot(p_ref[...], h2[s * SLAB:(s + 1) * SLAB], preferred_element_type=F32)
            c0 = b * chunks_per_seq + s * SLAB_CHUNKS
            for k in range(SUBLANES):
                h_scr[k, c0:c0 + SLAB_CHUNKS, :] = hp[k * SLAB_CHUNKS:(k + 1) * SLAB_CHUNKS].astype(BF16)
    for k in range(SUBLANES):
        ht = h_scr[k].astype(F32).T.astype(BF16)
        uz = jnp.dot(w_ref[...], ht, preferred_element_type=F32)
        u_ref[k] = uz[:SSM_WIDTH].astype(BF16)
        z_ref[k] = uz[SSM_WIDTH:].astype(BF16)


def _s5_in(x3, mod, g, w_t, chunks_per_seq):
    n_chunks = x3.shape[0]
    once = lambda shape: pl.BlockSpec(shape, lambda t: (0,) * len(shape), pipeline_mode=pl.Buffered(1))
    out = pl.BlockSpec((SUBLANES, SSM_WIDTH, n_chunks), lambda t: (t, 0, 0))
    return pl.pallas_call(
        functools.partial(_s5_in_kernel, chunks_per_seq=chunks_per_seq),
        out_shape=(jax.ShapeDtypeStruct((CHUNK, SSM_WIDTH, n_chunks), BF16),) * 2,
        grid=(CHUNK // SUBLANES,),
        in_specs=[pl.BlockSpec((n_chunks, SUBLANES, D_MODEL), lambda t: (0, t, 0)),
                  once(mod.shape), once((1, D_MODEL)), once((2 * SSM_WIDTH, D_MODEL)),
                  once((SLAB, SLAB))],
        out_specs=(out, out),
        scratch_shapes=[pltpu.VMEM((SUBLANES, n_chunks, D_MODEL), BF16)],
        compiler_params=pltpu.CompilerParams(
            dimension_semantics=("parallel",), vmem_limit_bytes=VMEM_LIMIT_BIG),
        name="s5_in",
    )(x3, mod, g, w_t, jnp.asarray(_slab_perm(), BF16))


def _gelu_tanh(v):
    return 0.5 * v * (1.0 + jnp.tanh(math.sqrt(2.0 / math.pi) * (v + 0.044715 * (v * v * v))))


def _s5_core_kernel(u_ref, toep_ref, mb_ref, mc_ref, al_ref, o_ref, *, chunks_per_seq):
    for gi in range(u_ref.shape[1]):
        _s5_core_group(u_ref.at[:, gi], toep_ref.at[gi], mb_ref.at[gi], mc_ref.at[gi], al_ref.at[gi],
                       o_ref.at[:, gi], chunks_per_seq)


def _s5_core_group(u_ref, toep_ref, mb_ref, mc_ref, al_ref, o_ref, chunks_per_seq):
    n_chunks = u_ref.shape[-1]
    ub = u_ref[...].reshape(CHUNK_ROWS, n_chunks)
    inc = jnp.dot(mb_ref[...], ub, preferred_element_type=F32)
    pos = lax.broadcasted_iota(jnp.int32, (STATE, n_chunks), 1) % chunks_per_seq
    reps = n_chunks // LANES

    def lane_tile(a):
        return jnp.concatenate([a] * reps, axis=1) if reps > 1 else a

    states = []
    for dr in range(2):
        xr = inc[(2 * dr) * STATE:(2 * dr + 1) * STATE]
        xi = inc[(2 * dr + 1) * STATE:(2 * dr + 2) * STATE]
        ar, ai = al_ref[2 * dr], al_ref[2 * dr + 1]

        def shifted(v, step):
            if dr == 0:
                return jnp.where(pos >= step, pltpu.roll(v, step, 1), 0.0)
            return jnp.where(pos < chunks_per_seq - step, pltpu.roll(v, n_chunks - step, 1), 0.0)

        step = 1
        while step < chunks_per_seq:
            sr, si = shifted(xr, step), shifted(xi, step)
            tr, ti = lane_tile(ar), lane_tile(ai)
            xr, xi = xr + tr * sr - ti * si, xi + tr * si + ti * sr
            ar, ai = _cmul(ar, ai, ar, ai)
            step *= 2
        states += [shifted(xr, 1), shifted(xi, 1)]
    h_in = jnp.concatenate(states, axis=0).astype(BF16)
    y = (jnp.dot(toep_ref[...], ub, preferred_element_type=F32)
         + jnp.dot(mc_ref[...], h_in, preferred_element_type=F32))
    o_ref[...] = _gelu_tanh(y).astype(BF16).reshape(CHUNK, GROUP_CH, n_chunks)


def _s5_core(u_cl, toep, mb, mc, al, chunks_per_seq):
    n_chunks = u_cl.shape[-1]
    u4 = u_cl.reshape(CHUNK, SSM_GROUPS, GROUP_CH, n_chunks)
    gs = S5_GROUPS_PER_STEP
    grp = pl.BlockSpec((CHUNK, gs, GROUP_CH, n_chunks), lambda g: (0, g, 0, 0))
    out = pl.pallas_call(
        functools.partial(_s5_core_kernel, chunks_per_seq=chunks_per_seq),
        out_shape=jax.ShapeDtypeStruct(u4.shape, BF16),
        grid=(SSM_GROUPS // gs,),
        in_specs=[grp,
                  pl.BlockSpec((gs, CHUNK_ROWS, CHUNK_ROWS), lambda g: (g, 0, 0)),
                  pl.BlockSpec((gs, 4 * STATE, CHUNK_ROWS), lambda g: (g, 0, 0)),
                  pl.BlockSpec((gs, CHUNK_ROWS, 4 * STATE), lambda g: (g, 0, 0)),
                  pl.BlockSpec((gs, 4, STATE, LANES), lambda g: (g, 0, 0, 0))],
        out_specs=grp,
        compiler_params=pltpu.CompilerParams(
            dimension_semantics=("parallel",), vmem_limit_bytes=VMEM_LIMIT),
        name="s5_core",
    )(u4, toep, mb, mc, al)
    return out.reshape(CHUNK, SSM_WIDTH, n_chunks)


def _s5_out_kernel(g_ref, z_ref, wglu_ref, wout_ref, pt_ref, x_ref, mod_ref, fg_ref, o_ref, y_scr,
                   *, final, chunks_per_seq):
    q = pl.program_id(1)
    n_half = x_ref.shape[0]
    k_per = g_ref.shape[0]
    n_glu = SUBLANES // k_per

    for qv in range(n_glu):
        @pl.when(q == qv)
        def _():
            for kk in range(k_per):
                gb = g_ref[kk]
                glu = jnp.dot(wglu_ref[...], gb, preferred_element_type=F32)
                y = gb.astype(F32) * _sigmoid(glu) * _silu(z_ref[kk].astype(F32))
                y_scr[qv * k_per + kk] = y.T.astype(BF16)

    for hv in range(y_scr.shape[1] // n_half):
        @pl.when(q == n_glu + hv)
        def _():
            for s in range(n_half // SLAB_CHUNKS):
                c0 = hv * n_half + s * SLAB_CHUNKS
                src = jnp.concatenate([y_scr[k, c0:c0 + SLAB_CHUNKS, :] for k in range(SUBLANES)], axis=0)
                yp = jnp.dot(pt_ref[...], src, preferred_element_type=F32).astype(BF16)
                out = jnp.dot(yp, wout_ref[...], preferred_element_type=F32)
                b = c0 // chunks_per_seq
                upd = mod_ref[b:b + 1, 2 * D_MODEL:] * out
                rows = slice(s * SLAB_CHUNKS, (s + 1) * SLAB_CHUNKS)
                xn = x_ref[rows] + upd.reshape(SLAB_CHUNKS, SUBLANES, D_MODEL)
                if final:
                    xn = xn * lax.rsqrt(jnp.mean(xn * xn, axis=-1, keepdims=True) + NORM_EPS) * fg_ref[...]
                o_ref[rows] = xn


def _s5_out(g_cl, z_cl, wglu_t, w_out, x3, mod, final_g, final, chunks_per_seq, n_split=2):
    n_chunks = x3.shape[0]
    n_half = n_chunks // n_split
    n_glu = SUBLANES // S5_OUT_K
    once = lambda shape: pl.BlockSpec(shape, lambda t, q: (0,) * len(shape), pipeline_mode=pl.Buffered(1))
    act = pl.BlockSpec((S5_OUT_K, SSM_WIDTH, n_chunks), lambda t, q: (t * n_glu + jnp.minimum(q, n_glu - 1), 0, 0))
    row = pl.BlockSpec((n_half, SUBLANES, D_MODEL), lambda t, q: (jnp.maximum(q - n_glu, 0), t, 0))
    return pl.pallas_call(
        functools.partial(_s5_out_kernel, final=final, chunks_per_seq=chunks_per_seq),
        out_shape=jax.ShapeDtypeStruct(x3.shape, F32),
        grid=(CHUNK // SUBLANES, n_glu + n_split),
        in_specs=[act, act, once((SSM_WIDTH, SSM_WIDTH)), once((SSM_WIDTH, D_MODEL)), once((SLAB, SLAB)),
                  row, once(mod.shape), once((1, D_MODEL))],
        out_specs=row,
        scratch_shapes=[pltpu.VMEM((SUBLANES, n_chunks, SSM_WIDTH), BF16)],
        compiler_params=pltpu.CompilerParams(
            dimension_semantics=("parallel", "arbitrary"), vmem_limit_bytes=VMEM_LIMIT_BIG),
        name="s5_out",
    )(g_cl, z_cl, wglu_t, w_out, jnp.asarray(_slab_perm().T, BF16), x3, mod, final_g)


def _rope_tables(s):
    inv_freq = ROPE_THETA ** (-jnp.arange(0, HEAD_DIM, 2, dtype=F32) / HEAD_DIM)
    ang = jnp.arange(s, dtype=F32)[:, None] * inv_freq[None, :]
    reps = LANES // (HEAD_DIM // 2)
    sign = jnp.where(jnp.arange(LANES) < LANES // 2, -1.0, 1.0).astype(F32)
    cos, sin = jnp.tile(jnp.cos(ang), (1, reps)), jnp.tile(jnp.sin(ang), (1, reps)) * sign
    stream_major = lambda a: a.reshape(s // STREAMS, STREAMS, LANES).transpose(1, 0, 2)
    return cos, sin, stream_major(cos), stream_major(sin)


def _prep_attn_w_in(w):
    d_in = w.shape[0]
    w = w.reshape(d_in, -1, ATTN_HEADS // 2, 2, 2, HEAD_DIM // 2)
    w = jnp.stack([w[:, i] for i in ATTN_IN_BLOCK_ORDER], axis=1)
    return w.transpose(0, 1, 2, 4, 3, 5).astype(BF16).reshape(d_in, -1)


def _prep_attn_w_out(w):
    w = w.astype(BF16).reshape(ATTN_HEADS // 2, 2, 2, HEAD_DIM // 2, w.shape[1])
    return w.transpose(0, 2, 1, 3, 4).reshape(ATTN_WIDTH, -1)


def _trunk(x, ada, norm_g, attn_w, s5_w, s5_ops, tables, final_norm_g):
    bsz, s, _ = x.shape
    chunks_per_seq = s // CHUNK
    n_chunks = bsz * chunks_per_seq
    fg = final_norm_g.reshape(1, D_MODEL)
    for i in range(DEPTH):
        mod = ada[i].reshape(bsz, 1, 3 * D_MODEL)
        g = norm_g[i].reshape(1, D_MODEL)
        j = i // N_MIXERS
        if i % N_MIXERS == 0:
            w_in, w_out = attn_w[j]
            nat, qkv12 = _attn_in(x, mod, g, w_in, tables)
            x = _attn_out(_attention(nat, qkv12), nat, x, mod, w_out)
        else:
            w_in_t, wglu_t, w_out = s5_w[j]
            toep, mb, mc, al = s5_ops[j]
            x3 = x.reshape(n_chunks, CHUNK, D_MODEL)
            u_cl, z_cl = _s5_in(x3, ada[i], g, w_in_t, chunks_per_seq)
            g_cl = _s5_core(u_cl, toep, mb, mc, al, chunks_per_seq)
            x3 = _s5_out(g_cl, z_cl, wglu_t, w_out, x3, ada[i], fg, i == DEPTH - 1, chunks_per_seq)
            x = x3.reshape(bsz, s, D_MODEL)
    return x


def kernel(x_prompt, x_sample, c_prompt, c_sample, norm_g, ada_w, ada_b, attn_w_in, attn_w_out,
           ssm_w_in, ssm_lam_re, ssm_lam_im, ssm_log_dt, ssm_b_re, ssm_b_im, ssm_c_re, ssm_c_im,
           ssm_d, ssm_w_glu, ssm_w_out, final_norm_g):
    assert (DEPTH - 1) % N_MIXERS == 1, "the final norm is fused into the last S5 layer"
    assert all(w // (2 * d) == RADIUS for w, d in DILATED_PAIRS)
    assert [d for _, d in DILATED_PAIRS] == [1, 4, STREAMS]
    n_prompt = c_prompt.shape[0]
    ada = _ada(jnp.concatenate([c_prompt, c_sample], axis=0), ada_w, ada_b)
    attn_w = [(_prep_attn_w_in(attn_w_in[j]), _prep_attn_w_out(attn_w_out[j]))
              for j in range(attn_w_in.shape[0])]
    s5_w, s5_ops = [], []
    for j in range(ssm_w_in.shape[0]):
        s5_w.append((ssm_w_in[j].T.astype(BF16), ssm_w_glu[j].T.astype(BF16), ssm_w_out[j].astype(BF16)))
        s5_ops.append(_s5_prep(ssm_lam_re[j], ssm_lam_im[j], ssm_log_dt[j], ssm_b_re[j],
                               ssm_b_im[j], ssm_c_re[j], ssm_c_im[j], ssm_d[j]))
    tables = _rope_tables(max(x_prompt.shape[1], x_sample.shape[1]))
    y_prompt = _trunk(x_prompt, ada[:, :n_prompt], norm_g, attn_w, s5_w, s5_ops, tables, final_norm_g)
    y_sample = _trunk(x_sample, ada[:, n_prompt:], norm_g, attn_w, s5_w, s5_ops, tables, final_norm_g)
    return (y_prompt, y_sample)
```

```python
import functools
import math

import numpy as np
import jax
import jax.numpy as jnp
from jax import lax
from jax.experimental import pallas as pl
from jax.experimental.pallas import tpu as pltpu

D_MODEL = 1024
DEPTH = 4
N_MIXERS = 2
ATTN_HEADS = 16
HEAD_DIM = 64
ATTN_WIDTH = ATTN_HEADS * HEAD_DIM
DILATED_PAIRS = ((128, 1), (512, 4), (2048, 16))
N_DIL = len(DILATED_PAIRS)
ROPE_THETA = 10000.0
SSM_WIDTH = D_MODEL
GROUP_CH = 16
SSM_GROUPS = SSM_WIDTH // GROUP_CH
STATE = 64
LAMBDA_RE_MAX = -1e-4
NORM_EPS = 1e-6
NEG_INF = -1e30

LANES = 128
SUBLANES = 8
CHUNK = 64
CHUNK_ROWS = CHUNK * GROUP_CH
T_PER_VREG = LANES // GROUP_CH
V_LANES = 2 * CHUNK_ROWS + LANES
STREAMS = 16
RADIUS = 64
ATTN_TQ = 128
ATTN_SUB = 4
STAT_PITCH = 40
ATTN_IN_BLOCK_ORDER = (1, 2, 0, 9, 4, 5, 7, 8, 3, 6)
PERM_ROWS = 512
UNPERM_ROWS = 256
LOG2_E = math.log2(math.e)
S5_GROUPS_PER_STEP = 2
S5_OUT_K = 4
SLAB = 256
SLAB_CHUNKS = SLAB // SUBLANES
VMEM_LIMIT = 48 * 1024 * 1024
VMEM_LIMIT_BIG = 56 * 1024 * 1024

F32 = jnp.float32
BF16 = jnp.bfloat16
HI = lax.Precision.HIGHEST


def _sigmoid(v):
    return 0.5 * jnp.tanh(0.5 * v) + 0.5


def _silu(v):
    return v * _sigmoid(v)


def _rms_modulate(x, g, scale, shift):
    rs = lax.rsqrt(jnp.mean(x * x, axis=-1, keepdims=True) + NORM_EPS)
    return (x * rs * g) * (1.0 + scale) + shift


def _ada_kernel(c_ref, w_ref, b_ref, o_ref):
    o_ref[...] = jnp.dot(_silu(c_ref[...]), w_ref[...], precision=HI,
                         preferred_element_type=F32) + b_ref[...]


def _ada(c, ada_w, ada_b):
    nb = c.shape[0]
    return pl.pallas_call(
        _ada_kernel,
        out_shape=jax.ShapeDtypeStruct((DEPTH, nb, 3 * D_MODEL), F32),
        grid=(DEPTH, 3),
        in_specs=[
            pl.BlockSpec((nb, D_MODEL), lambda i, j: (0, 0)),
            pl.BlockSpec((None, D_MODEL, D_MODEL), lambda i, j: (i, 0, j)),
            pl.BlockSpec((None, 1, D_MODEL), lambda i, j: (i, 0, j)),
        ],
        out_specs=pl.BlockSpec((None, nb, D_MODEL), lambda i, j: (i, 0, j)),
        name="ada",
    )(c, ada_w, ada_b.reshape(DEPTH, 1, 3 * D_MODEL))


def _attn_in_kernel(x_ref, shift_ref, scale_ref, g_ref, w_ref, cos_ref, sin_ref, cosp_ref, sinp_ref,
                    perm_ref, nat_ref, sm_ref, h_nat, h_perm):
    j = pl.program_id(2)
    tm = x_ref.shape[0]
    n_sub = tm // PERM_ROWS
    sub_rows = PERM_ROWS // STREAMS

    @pl.when(j == 0)
    def _():
        h = _rms_modulate(x_ref[...], g_ref[...], scale_ref[...], shift_ref[...])
        h_nat[...] = h.astype(BF16)
        for u in range(n_sub):
            rs = slice(u * PERM_ROWS, (u + 1) * PERM_ROWS)
            h_perm[rs, :] = jnp.dot(perm_ref[...], h_nat[rs, :], preferred_element_type=F32).astype(BF16)

    def rope_blocks(acc, cos, sin, second_is_plain):
        for half in range(2):
            is_q = jnp.logical_or(j == 4, jnp.logical_and(j == 1, half == 0))
            qscale = jnp.where(is_q, HEAD_DIM ** -0.5 * LOG2_E, 1.0).astype(F32)
            c, sn = cos * qscale, sin * qscale
            for b in range(ATTN_WIDTH // LANES):
                sl = slice(half * ATTN_WIDTH + b * LANES, half * ATTN_WIDTH + (b + 1) * LANES)
                t = acc[:, sl]
                if half == 1 and second_is_plain:
                    yield sl, t.astype(BF16)
                else:
                    yield sl, (t * c + pltpu.roll(t, LANES // 2, 1) * sn).astype(BF16)

    @pl.when(j < 2)
    def _():
        acc = jnp.dot(h_nat[...], w_ref[...], preferred_element_type=F32)
        for sl, blk in rope_blocks(acc, cos_ref[...], sin_ref[...], True):
            nat_ref[:, sl] = blk

    def stream_major_step(second_is_plain):
        acc = jnp.dot(h_perm[...], w_ref[...], preferred_element_type=F32)

        def table(ref):
            return jnp.concatenate([ref[:, u * sub_rows:(u + 1) * sub_rows, :].reshape(PERM_ROWS, LANES)
                                    for u in range(n_sub)], axis=0)

        for sl, blk in rope_blocks(acc, table(cosp_ref), table(sinp_ref), second_is_plain):
            for u in range(n_sub):
                sm_ref[:, u * sub_rows:(u + 1) * sub_rows, sl] = (
                    blk[u * PERM_ROWS:(u + 1) * PERM_ROWS].reshape(STREAMS, sub_rows, LANES))

    pl.when(jnp.logical_and(j >= 2, j < 4))(lambda: stream_major_step(True))
    pl.when(j == 4)(lambda: stream_major_step(False))


def _stream_perm(n):
    dst = np.arange(n)
    rows = n // STREAMS
    perm = np.zeros((n, n), np.float32)
    perm[dst, STREAMS * (dst % rows) + dst // rows] = 1.0
    return perm


def _attn_in(x, mod, g, w, tables, tm=1024):
    bsz, s, _ = x.shape
    cos, sin, cosp, sinp = tables
    rows = tm // STREAMS
    wide = 2 * ATTN_WIDTH
    perm = jnp.asarray(_stream_perm(PERM_ROWS), BF16)
    return pl.pallas_call(
        _attn_in_kernel,
        out_shape=(jax.ShapeDtypeStruct((bsz, s, 2 * wide), BF16),
                   jax.ShapeDtypeStruct((bsz, STREAMS, s // STREAMS, 3 * wide), BF16)),
        grid=(bsz, s // tm, 5),
        in_specs=[
            pl.BlockSpec((None, tm, D_MODEL), lambda b, i, j: (b, i, 0)),
            pl.BlockSpec((None, 1, D_MODEL), lambda b, i, j: (b, 0, 0)),
            pl.BlockSpec((None, 1, D_MODEL), lambda b, i, j: (b, 0, 1)),
            pl.BlockSpec((1, D_MODEL), lambda b, i, j: (0, 0)),
            pl.BlockSpec((D_MODEL, wide), lambda b, i, j: (0, j)),
            pl.BlockSpec((tm, LANES), lambda b, i, j: (i, 0)),
            pl.BlockSpec((tm, LANES), lambda b, i, j: (i, 0)),
            pl.BlockSpec((STREAMS, rows, LANES), lambda b, i, j: (0, i, 0)),
            pl.BlockSpec((STREAMS, rows, LANES), lambda b, i, j: (0, i, 0)),
            pl.BlockSpec((PERM_ROWS, PERM_ROWS), lambda b, i, j: (0, 0)),
        ],
        out_specs=(
            pl.BlockSpec((None, tm, wide), lambda b, i, j: (b, i, jnp.clip(j, 0, 1))),
            pl.BlockSpec((None, STREAMS, rows, wide), lambda b, i, j: (b, 0, i, jnp.clip(j - 2, 0, 2))),
        ),
        scratch_shapes=[pltpu.VMEM((tm, D_MODEL), BF16), pltpu.VMEM((tm, D_MODEL), BF16)],
        compiler_params=pltpu.CompilerParams(
            dimension_semantics=("parallel", "parallel", "arbitrary"),
            vmem_limit_bytes=VMEM_LIMIT_BIG),
        name="attn_in",
    )(x, mod, mod, g, w, cos, sin, cosp, sinp, perm)


def _lane_of_odd_head(lane):
    return (lane // (HEAD_DIM // 2)) % 2 == 1


def _mask_bias(valid):
    return jnp.where(valid, 0.0, NEG_INF).astype(F32)


def _attn_pairs(get_q, get_k, get_v, bias, put_acc):
    tq = bias.shape[0]
    lane = lax.broadcasted_iota(jnp.int32, (tq, LANES), 1)
    q_is_odd = _lane_of_odd_head(lane)
    mx_tile = jnp.zeros((tq, LANES), F32)
    den_tile = jnp.ones((tq, LANES), F32)
    ones = jnp.ones((bias.shape[1], LANES), BF16)
    bias2 = jnp.concatenate([bias, bias], axis=0)
    for b in range(ATTN_WIDTH // LANES):
        sl = slice(b * LANES, (b + 1) * LANES)
        qb = get_q(sl).astype(F32)
        kb = get_k(sl)
        vb = jnp.concatenate([get_v(sl), ones], axis=1)
        qm = jnp.concatenate([jnp.where(q_is_odd, 0.0, qb), jnp.where(q_is_odd, qb, 0.0)], axis=0).astype(BF16)
        sc = lax.dot_general(qm, kb, (((1,), (1,)), ((), ())), preferred_element_type=F32) + bias2
        mx = jnp.max(sc, axis=1, keepdims=True)
        p = jnp.exp2((sc - mx).astype(BF16))
        acc = jnp.dot(p, vb, preferred_element_type=F32)
        for odd in (0, 1):
            rows = slice(odd * tq, (odd + 1) * tq)
            mx_tile = jnp.where(lane == 2 * b + odd, mx[rows], mx_tile)
            den_tile = jnp.where(lane == 2 * b + odd, acc[rows, LANES:], den_tile)
        put_acc(sl, jnp.where(q_is_odd, acc[tq:, :LANES], acc[:tq, :LANES]))
    return mx_tile, den_tile


def _pair_factor(tile, b):
    lane = lax.broadcasted_iota(jnp.int32, tile.shape, 1)
    head = 2 * b + _lane_of_odd_head(lane).astype(jnp.int32)
    return jnp.take_along_axis(tile, head, axis=1, mode="promise_in_bounds")


def _merge_groups(stats, get_accs, put_o):
    top = functools.reduce(jnp.maximum, [mx for mx, _ in stats])
    ws = [jnp.exp2(mx - top) for mx, _ in stats]
    inv = 1.0 / sum(w * den for w, (_, den) in zip(ws, stats))
    fs = [w * inv for w in ws]
    for b in range(ATTN_WIDTH // LANES):
        sl = slice(b * LANES, (b + 1) * LANES)
        put_o(sl, sum(_pair_factor(f, b) * get_acc(sl) for f, get_acc in zip(fs, get_accs)))


def _window_mask(tile, tq, m):
    nk = tq + 2 * RADIUS
    row = lax.broadcasted_iota(jnp.int32, (tq, nk), 0)
    col = lax.broadcasted_iota(jnp.int32, (tq, nk), 1)
    kabs = tile * tq - RADIUS + col
    return _mask_bias((jnp.abs(col - RADIUS - row) <= RADIUS) & (kabs >= 0) & (kabs < m))


def _stack_rows(read, prev_ref, cur_ref, next_ref, lo, cnt):
    halo, cur_n = prev_ref.shape[-2], cur_ref.shape[-2]
    a, b = lo, lo + cnt
    parts = []
    if a < 0:
        parts.append(read(prev_ref, halo + a, halo))
        a = 0
    parts.append(read(cur_ref, a, min(b, cur_n)))
    if b > cur_n:
        parts.append(read(next_ref, 0, b - cur_n))
    return parts


def _kv_getters(stack):
    get_k = lambda sl: stack(sl)
    get_v = lambda sl: stack(slice(ATTN_WIDTH + sl.start, ATTN_WIDTH + sl.stop))
    return get_k, get_v


def _attn16_kernel(q_ref, kvp_ref, kvc_ref, kvn_ref, o_ref, mx_ref, den_ref, *, m):
    tq = ATTN_TQ
    n_sub = q_ref.shape[0] // tq
    for t in range(n_sub):
        rows = slice(t * tq, (t + 1) * tq)
        bias = _window_mask(pl.program_id(2) * n_sub + t, tq, m)

        def stack(cols):
            return jnp.concatenate(_stack_rows(lambda ref, a, b: ref[a:b, cols], kvp_ref, kvc_ref, kvn_ref,
                                               t * tq - RADIUS, tq + 2 * RADIUS), axis=0)

        def put_acc(sl, val):
            o_ref[rows, sl] = val.astype(BF16)

        mx_ref[rows], den_ref[rows] = _attn_pairs(lambda sl: q_ref[rows, sl], *_kv_getters(stack), bias, put_acc)


def _attn4_kernel(q_ref, kvp_ref, kvc_ref, kvn_ref, o_ref, mx_ref, den_ref, *, m):
    ns, halo = q_ref.shape[0], kvp_ref.shape[1]
    rows = ATTN_TQ // ns
    span = rows + 2 * halo
    n_sub = q_ref.shape[1] // rows
    n = lax.broadcasted_iota(jnp.int32, (ns * rows, ns * span), 0)
    c = lax.broadcasted_iota(jnp.int32, (ns * rows, ns * span), 1)
    for t in range(n_sub):
        base = (pl.program_id(2) * n_sub + t) * rows
        rs = slice(t * rows, (t + 1) * rows)
        krow = base - halo + c % span
        rel = ns * (krow - (base + n % rows)) + (c // span - n // rows)
        bias = _mask_bias((jnp.abs(rel) <= RADIUS) & (krow >= 0) & (krow < m))

        def stack(cols):
            parts = []
            for s in range(ns):
                parts += _stack_rows(lambda ref, a, b: ref[s, a:b, cols], kvp_ref, kvc_ref, kvn_ref,
                                     t * rows - halo, span)
            return jnp.concatenate(parts, axis=0)

        flat = lambda ref: (lambda sl: ref[:, rs, sl].reshape(ns * rows, LANES))

        def put_acc(sl, val):
            o_ref[:, rs, sl] = val.astype(BF16).reshape(ns, rows, LANES)

        mx, den = _attn_pairs(flat(q_ref), *_kv_getters(stack), bias, put_acc)
        mx_ref[:, rs, :] = mx.reshape(ns, rows, LANES)
        den_ref[:, rs, :] = den.reshape(ns, rows, LANES)


def _attn1_kernel(q_ref, kvp_ref, kvc_ref, kvn_ref, o_ref, mx_ref, den_ref, *, m):
    tq = ATTN_TQ
    n_sub = q_ref.shape[0] // tq
    for t in range(n_sub):
        rows = slice(t * tq, (t + 1) * tq)
        bias = _window_mask(pl.program_id(1) * n_sub + t, tq, m)

        def stack(cols):
            return jnp.concatenate(_stack_rows(lambda ref, a, b: ref[a:b, cols], kvp_ref, kvc_ref, kvn_ref,
                                               t * tq - RADIUS, tq + 2 * RADIUS), axis=0)

        def put_acc(sl, val):
            o_ref[rows, sl] = val.astype(BF16)

        mx_ref[rows], den_ref[rows] = _attn_pairs(lambda sl: q_ref[rows, sl], *_kv_getters(stack), bias, put_acc)


def _attention(qkv0, qkv12):
    bsz, s, _ = qkv0.shape
    m16 = s // STREAMS
    tq, half = ATTN_TQ, RADIUS
    cur_rows = min(ATTN_SUB * tq, m16)
    params = lambda n: pltpu.CompilerParams(dimension_semantics=("parallel",) * n,
                                            vmem_limit_bytes=VMEM_LIMIT)
    same = lambda i: i

    def halo_idx(cur_n, halo_n, total):
        per = cur_n // halo_n
        return (lambda i: jnp.maximum(i * per - 1, 0)), (lambda i: jnp.minimum((i + 1) * per, total // halo_n - 1))

    def spec16(nrows, row_idx, col, width=ATTN_WIDTH):
        return pl.BlockSpec((None, None, nrows, width), lambda b, r, i: (b, r, row_idx(i), col))
    prv, nxt = halo_idx(cur_rows, half, m16)
    stat16 = jax.ShapeDtypeStruct((bsz, STREAMS, m16, LANES), F32)
    o2, mx2, den2 = pl.pallas_call(
        functools.partial(_attn16_kernel, m=m16),
        out_shape=(jax.ShapeDtypeStruct((bsz, STREAMS, m16, ATTN_WIDTH), BF16), stat16, stat16),
        grid=(bsz, STREAMS, m16 // cur_rows),
        in_specs=[spec16(cur_rows, same, 5), spec16(half, prv, 1, 2 * ATTN_WIDTH),
                  spec16(cur_rows, same, 1, 2 * ATTN_WIDTH), spec16(half, nxt, 1, 2 * ATTN_WIDTH)],
        out_specs=(spec16(cur_rows, same, 0), spec16(cur_rows, same, 0, LANES), spec16(cur_rows, same, 0, LANES)),
        compiler_params=params(3), name="attn_dil16",
    )(*([qkv12] * 4))

    ns = STREAMS // 4
    rows4, halo4 = cur_rows // ns, half // ns

    def view4(a):
        return a.reshape(bsz, ns, 4, m16, a.shape[-1])

    def spec4(nrows, row_idx, col, width=ATTN_WIDTH):
        return pl.BlockSpec((None, ns, None, nrows, width), lambda b, r, i: (b, 0, r, row_idx(i), col))
    prv4, nxt4 = halo_idx(rows4, halo4, m16)
    q4 = view4(qkv12)
    stat4 = jax.ShapeDtypeStruct((bsz, ns, 4, m16, LANES), F32)
    o1, mx1, den1 = pl.pallas_call(
        functools.partial(_attn4_kernel, m=m16),
        out_shape=(jax.ShapeDtypeStruct((bsz, ns, 4, m16, ATTN_WIDTH), BF16), stat4, stat4),
        grid=(bsz, 4, m16 // rows4),
        in_specs=[spec4(rows4, same, 4), spec4(halo4, prv4, 0, 2 * ATTN_WIDTH),
                  spec4(rows4, same, 0, 2 * ATTN_WIDTH), spec4(halo4, nxt4, 0, 2 * ATTN_WIDTH)],
        out_specs=(spec4(rows4, same, 0), spec4(rows4, same, 0, LANES), spec4(rows4, same, 0, LANES)),
        compiler_params=params(3), name="attn_dil4",
    )(q4, q4, q4, q4)
    streams = lambda a: a.reshape(bsz, STREAMS, m16, a.shape[-1])

    cur1 = 2 * ATTN_SUB * tq

    def spec1(nrows, row_idx, col, width=ATTN_WIDTH):
        return pl.BlockSpec((None, nrows, width), lambda b, i: (b, row_idx(i), col))
    prv1, nxt1 = halo_idx(cur1, half, s)
    stat1 = jax.ShapeDtypeStruct((bsz, s, LANES), F32)
    group0 = pl.pallas_call(
        functools.partial(_attn1_kernel, m=s),
        out_shape=(jax.ShapeDtypeStruct((bsz, s, ATTN_WIDTH), BF16), stat1, stat1),
        grid=(bsz, s // cur1),
        in_specs=[spec1(cur1, same, 2), spec1(half, prv1, 0, 2 * ATTN_WIDTH),
                  spec1(cur1, same, 0, 2 * ATTN_WIDTH), spec1(half, nxt1, 0, 2 * ATTN_WIDTH)],
        out_specs=(spec1(cur1, same, 0), spec1(cur1, same, 0, LANES), spec1(cur1, same, 0, LANES)),
        compiler_params=params(2), name="attn_dil1",
    )(*([qkv0] * 4))
    return group0, (o2, mx2, den2), (streams(o1), streams(mx1), streams(den1))


def _attn_out_kernel(o0_ref, mx0_ref, den0_ref, o2_ref, mx2_ref, den2_ref, o1_ref, mx1_ref, den1_ref,
                     unperm_ref, z_ref, x_ref, gate_ref, w_ref, out_ref, y_scr, a2_scr, a1_scr, stat_scr):
    tm = x_ref.shape[0]
    tq = ATTN_TQ
    srow, n_srow = tq // STREAMS, mx2_ref.shape[1]
    n_un = unperm_ref.shape[0]
    for k, src in enumerate((mx2_ref, den2_ref, mx1_ref, den1_ref)):
        for r in range(STREAMS):
            stat_scr[k, r * STAT_PITCH:r * STAT_PITCH + n_srow, :] = src[r]
    for u in range(tm // n_un):
        piece = slice(u * n_un, (u + 1) * n_un)
        for src, dst in ((o2_ref, a2_scr), (o1_ref, a1_scr)):
            rs = slice(u * n_un // STREAMS, (u + 1) * n_un // STREAMS)
            part = jnp.concatenate([src[r, rs, :] for r in range(STREAMS)], axis=0)
            dst[piece, :] = jnp.dot(unperm_ref[...], part, preferred_element_type=F32)
        for t in range(u * n_un // tq, (u + 1) * n_un // tq):
            rows = slice(t * tq, (t + 1) * tq)

            def natural(k):
                return jnp.concatenate([stat_scr[k, pl.ds(t * srow + ii, STREAMS, stride=STAT_PITCH), :]
                                        for ii in range(srow)], axis=0)

            def put_y(sl, val):
                y_scr[rows, sl] = (val * _silu(z_ref[rows, sl].astype(F32))).astype(BF16)

            stats = [(mx0_ref[rows], den0_ref[rows]), (natural(0), natural(1)), (natural(2), natural(3))]
            _merge_groups(stats, [lambda sl: o0_ref[rows, sl].astype(F32), lambda sl: a2_scr[rows, sl],
                                  lambda sl: a1_scr[rows, sl]], put_y)
        out = jnp.dot(y_scr[piece, :], w_ref[...], preferred_element_type=F32)
        out_ref[piece, :] = x_ref[piece, :] + gate_ref[...] * out


def _attn_out(groups, nat, x, mod, w, tm=PERM_ROWS):
    bsz, s, _ = x.shape
    row = lambda width: pl.BlockSpec((None, tm, width), lambda b, i: (b, i, 0))
    sm = lambda width: pl.BlockSpec((None, STREAMS, tm // STREAMS, width), lambda b, i: (b, 0, i, 0))
    z_col = nat.shape[-1] // ATTN_WIDTH - 1
    unperm = jnp.asarray(_stream_perm(UNPERM_ROWS).T, BF16)
    return pl.pallas_call(
        _attn_out_kernel,
        out_shape=jax.ShapeDtypeStruct(x.shape, F32),
        grid=(bsz, s // tm),
        in_specs=[row(ATTN_WIDTH), row(LANES), row(LANES)] + [sm(ATTN_WIDTH), sm(LANES), sm(LANES)] * 2
                 + [pl.BlockSpec((UNPERM_ROWS, UNPERM_ROWS), lambda b, i: (0, 0)),
                    pl.BlockSpec((None, tm, ATTN_WIDTH), lambda b, i: (b, i, z_col)), row(D_MODEL),
                    pl.BlockSpec((None, 1, D_MODEL), lambda b, i: (b, 0, 2)),
                    pl.BlockSpec((ATTN_WIDTH, D_MODEL), lambda b, i: (0, 0))],
        out_specs=row(D_MODEL),
        scratch_shapes=[pltpu.VMEM((tm, ATTN_WIDTH), BF16), pltpu.VMEM((tm, ATTN_WIDTH), F32),
                        pltpu.VMEM((tm, ATTN_WIDTH), F32), pltpu.VMEM((4, STREAMS * STAT_PITCH, LANES), F32)],
        compiler_params=pltpu.CompilerParams(
            dimension_semantics=("parallel", "parallel"),
            vmem_limit_bytes=VMEM_LIMIT),
        name="attn_out",
    )(*groups[0], *groups[1], *groups[2], unperm, nat, x, mod, w)


def _cmul(ar, ai, br, bi):
    return ar * br - ai * bi, ar * bi + ai * br


def _s5_prep_kernel(*refs):
    ins, d_diag, outs, v_scr = refs[:9], refs[9], refs[10:14], refs[14]
    for gi in range(d_diag.shape[0]):
        _s5_prep_group(*[r.at[:, gi] for r in ins], d_diag.at[gi], *[o.at[gi] for o in outs], v_scr.at[gi])


def _s5_prep_group(lam_re_a, lam_im_a, lam_re_b, lam_im_b, logdt, bt_re, bt_im,
                   c_re, c_im, d_diag, toep_ref, mb_ref, mc_ref, al_ref, v_scr):
    lane = lax.broadcasted_iota(jnp.int32, (STATE, LANES), 1)
    a_idx = lane // GROUP_CH
    eye = (lax.broadcasted_iota(jnp.int32, (STATE, STATE), 0)
           == lax.broadcasted_iota(jnp.int32, (STATE, STATE), 1))
    n_tiles = CHUNK // T_PER_VREG

    mb_rows = []
    v_rhs = []
    v_lhs = []
    mc_cols = []
    zbar_f = None
    for dr in range(2):
        dt = jnp.exp(logdt[dr])
        lrb = jnp.minimum(lam_re_b[dr], LAMBDA_RE_MAX)
        mag = jnp.exp(lrb * dt)
        pw_r, pw_i = mag * jnp.cos(lam_im_b[dr] * dt), mag * jnp.sin(lam_im_b[dr] * dt)
        diag = lambda v: jnp.sum(jnp.where(eye, jnp.broadcast_to(v, (STATE, STATE)), 0.0), axis=1, keepdims=True)
        lr = jnp.minimum(lam_re_a[dr], LAMBDA_RE_MAX)
        li = lam_im_a[dr]
        l1r, l1i = diag(pw_r), diag(pw_i)
        small = [(jnp.ones_like(l1r), jnp.zeros_like(l1r)), (l1r, l1i)]
        for _ in range(T_PER_VREG - 1):
            small.append(_cmul(*small[-1], l1r, l1i))

        def lane_powers(exponent_of_step):
            pr, pi = small[exponent_of_step[0]]
            pr, pi = jnp.broadcast_to(pr, (STATE, LANES)), jnp.broadcast_to(pi, (STATE, LANES))
            for a in range(1, T_PER_VREG):
                er, ei = small[exponent_of_step[a]]
                pr, pi = jnp.where(a_idx == a, er, pr), jnp.where(a_idx == a, ei, pi)
            return pr, pi

        den = lr * lr + li * li
        nr, ni = l1r - 1.0, l1i
        cr, ci = (nr * lr + ni * li) / den, (ni * lr - nr * li) / den
        bbr, bbi = _cmul(cr, ci, bt_re[dr], bt_im[dr])
        l8r, l8i = small[T_PER_VREG]
        tiles = [None] * n_tiles
        if dr == 0:
            pr, pi = lane_powers([T_PER_VREG - 1 - a for a in range(T_PER_VREG)])
            cur = _cmul(pr, pi, bbr, bbi)
            for jt in range(n_tiles - 1, -1, -1):
                tiles[jt] = cur
                if jt:
                    cur = _cmul(cur[0], cur[1], l8r, l8i)
            v_rhs.append(_cmul(tiles[n_tiles - 1][0], tiles[n_tiles - 1][1], l1r, l1i))
            zbar_f = (jnp.where(lane < GROUP_CH, bbr, 0.0), jnp.where(lane < GROUP_CH, bbi, 0.0))
        else:
            pr, pi = lane_powers(list(range(T_PER_VREG)))
            cur = _cmul(pr, pi, bbr, bbi)
            for jt in range(n_tiles):
                tiles[jt] = cur
                if jt < n_tiles - 1:
                    cur = _cmul(cur[0], cur[1], l8r, l8i)
            v_rhs.append(tiles[0])
        mb_rows.append(jnp.concatenate([t[0] for t in tiles], axis=1))
        mb_rows.append(jnp.concatenate([t[1] for t in tiles], axis=1))
        a_r, a_i = l8r, l8i
        for _ in range(int(math.log2(CHUNK // T_PER_VREG))):
            a_r, a_i = _cmul(a_r, a_i, a_r, a_i)
        al_ref[2 * dr] = jnp.broadcast_to(a_r, (STATE, LANES))
        al_ref[2 * dr + 1] = jnp.broadcast_to(a_i, (STATE, LANES))

        pows = [(pw_r, pw_i)]
        for _ in range(int(math.log2(CHUNK)) - 1):
            pows.append(_cmul(*pows[-1], *pows[-1]))
        cre, cim = c_re[dr], c_im[dr]
        xr, xi = _cmul(cre, cim, *pows[0])
        for k in range(int(math.log2(CHUNK))):
            yr, yi = _cmul(xr, xi, *pows[k])
            if dr == 0:
                xr, xi = jnp.concatenate([xr, yr], 0), jnp.concatenate([xi, yi], 0)
            else:
                xr, xi = jnp.concatenate([yr, xr], 0), jnp.concatenate([yi, xi], 0)
        mc_cols += [xr, -xi]
        xr, xi = cre, cim
        for k in range(int(math.log2(n_tiles))):
            yr, yi = _cmul(xr, xi, *pows[k + int(math.log2(T_PER_VREG))])
            if dr == 0:
                xr, xi = jnp.concatenate([yr, xr], 0), jnp.concatenate([yi, xi], 0)
            else:
                xr, xi = jnp.concatenate([xr, yr], 0), jnp.concatenate([xi, yi], 0)
        v_lhs.append((xr, xi))

    mb_ref[...] = jnp.concatenate(mb_rows, axis=0).astype(BF16)
    mc_ref[...] = jnp.concatenate(mc_cols, axis=1).astype(BF16)

    def cdot(lhs, rhs):
        return (jnp.dot(lhs[0], rhs[0], precision=HI, preferred_element_type=F32)
                - jnp.dot(lhs[1], rhs[1], precision=HI, preferred_element_type=F32))

    out_f = cdot(v_lhs[0], v_rhs[0])
    zero_rows = jnp.zeros((CHUNK_ROWS // T_PER_VREG - GROUP_CH, STATE), F32)
    lag0_lhs = (jnp.concatenate([c_re[0], zero_rows], 0), jnp.concatenate([c_im[0], zero_rows], 0))
    out_b = cdot(v_lhs[1], v_rhs[1]) + cdot(lag0_lhs, zbar_f)
    out_b = out_b + jnp.concatenate([d_diag[...], jnp.zeros((CHUNK_ROWS // T_PER_VREG - GROUP_CH, LANES), F32)], 0)
    for jt in range(n_tiles):
        rows = slice(jt * GROUP_CH, (jt + 1) * GROUP_CH)
        v_scr[:, jt * LANES:(jt + 1) * LANES] = out_f[rows]
        v_scr[:, (n_tiles + jt) * LANES:(n_tiles + jt + 1) * LANES] = out_b[rows]
    v_scr[:, 2 * CHUNK_ROWS:] = jnp.zeros((GROUP_CH, LANES), F32)

    for k in range(T_PER_VREG):
        vk = v_scr[:, k * GROUP_CH:k * GROUP_CH + 2 * CHUNK_ROWS]
        for t_out in range(CHUNK):
            off = (CHUNK - t_out) * GROUP_CH
            if off % LANES == k * GROUP_CH:
                al_off = off - k * GROUP_CH
                toep_ref[t_out * GROUP_CH:(t_out + 1) * GROUP_CH, :] = (
                    vk[:, al_off:al_off + CHUNK_ROWS].astype(BF16))


def _s5_prep(lam_re, lam_im, log_dt, b_re, b_im, c_re, c_im, d_skip):
    g = SSM_GROUPS
    d_diag = d_skip.reshape(g, GROUP_CH, 1) * jnp.eye(GROUP_CH, LANES, dtype=F32)
    lam_a = lambda a: a.reshape(2, g, STATE, 1)
    lam_b = lambda a: a.reshape(2, g, 1, STATE)
    tile_b = lambda a: jnp.tile(a, (1, 1, 1, T_PER_VREG))
    gs = S5_GROUPS_PER_STEP
    dir_spec = lambda r, c: pl.BlockSpec((2, gs, r, c), lambda i: (0, i, 0, 0))
    return pl.pallas_call(
        _s5_prep_kernel,
        out_shape=(jax.ShapeDtypeStruct((g, CHUNK_ROWS, CHUNK_ROWS), BF16),
                   jax.ShapeDtypeStruct((g, 4 * STATE, CHUNK_ROWS), BF16),
                   jax.ShapeDtypeStruct((g, CHUNK_ROWS, 4 * STATE), BF16),
                   jax.ShapeDtypeStruct((g, 4, STATE, LANES), F32)),
        grid=(g // gs,),
        in_specs=[dir_spec(STATE, 1), dir_spec(STATE, 1), dir_spec(1, STATE), dir_spec(1, STATE),
                  dir_spec(1, 1), dir_spec(STATE, LANES), dir_spec(STATE, LANES),
                  dir_spec(GROUP_CH, STATE), dir_spec(GROUP_CH, STATE),
                  pl.BlockSpec((gs, GROUP_CH, LANES), lambda i: (i, 0, 0))],
        out_specs=(pl.BlockSpec((gs, CHUNK_ROWS, CHUNK_ROWS), lambda i: (i, 0, 0)),
                   pl.BlockSpec((gs, 4 * STATE, CHUNK_ROWS), lambda i: (i, 0, 0)),
                   pl.BlockSpec((gs, CHUNK_ROWS, 4 * STATE), lambda i: (i, 0, 0)),
                   pl.BlockSpec((gs, 4, STATE, LANES), lambda i: (i, 0, 0, 0))),
        scratch_shapes=[pltpu.VMEM((gs, GROUP_CH, V_LANES), F32)],
        compiler_params=pltpu.CompilerParams(dimension_semantics=("parallel",)),
        name="s5_prep",
    )(lam_a(lam_re), lam_a(lam_im), lam_b(lam_re), lam_b(lam_im), log_dt.reshape(2, g, 1, 1),
      tile_b(b_re), tile_b(b_im), c_re, c_im, d_diag)


def _slab_perm():
    dst = np.arange(SLAB)
    perm = np.zeros((SLAB, SLAB), np.float32)
    perm[dst, (dst % SLAB_CHUNKS) * SUBLANES + dst // SLAB_CHUNKS] = 1.0
    return perm


def _s5_in_kernel(x_ref, mod_ref, g_ref, w_ref, p_ref, u_ref, z_ref, h_scr, *, chunks_per_seq):
    n_chunks = x_ref.shape[0]
    for b in range(n_chunks // chunks_per_seq):
        rows = slice(b * chunks_per_seq, (b + 1) * chunks_per_seq)
        h = _rms_modulate(x_ref[rows], g_ref[...], mod_ref[b:b + 1, D_MODEL:2 * D_MODEL],
                          mod_ref[b:b + 1, :D_MODEL])
        h2 = h.reshape(chunks_per_seq * SUBLANES, D_MODEL).astype(BF16)
        for s in range(chunks_per_seq // SLAB_CHUNKS):
            hp = jnp.dot(p_ref[...], h2[s * SLAB:(s + 1) * SLAB], preferred_element_type=F32)
            c0 = b * chunks_per_seq + s * SLAB_CHUNKS
            for k in range(SUBLANES):
                h_scr[k, c0:c0 + SLAB_CHUNKS, :] = hp[k * SLAB_CHUNKS:(k + 1) * SLAB_CHUNKS].astype(BF16)
    for k in range(SUBLANES):
        ht = h_scr[k].astype(F32).T.astype(BF16)
        uz = jnp.dot(w_ref[...], ht, preferred_element_type=F32)
        u_ref[k] = uz[:SSM_WIDTH].astype(BF16)
        z_ref[k] = uz[SSM_WIDTH:].astype(BF16)


def _s5_in(x3, mod, g, w_t, chunks_per_seq):
    n_chunks = x3.shape[0]
    once = lambda shape: pl.BlockSpec(shape, lambda t: (0,) * len(shape), pipeline_mode=pl.Buffered(1))
    out = pl.BlockSpec((SUBLANES, SSM_WIDTH, n_chunks), lambda t: (t, 0, 0))
    return pl.pallas_call(
        functools.partial(_s5_in_kernel, chunks_per_seq=chunks_per_seq),
        out_shape=(jax.ShapeDtypeStruct((CHUNK, SSM_WIDTH, n_chunks), BF16),) * 2,
        grid=(CHUNK // SUBLANES,),
        in_specs=[pl.BlockSpec((n_chunks, SUBLANES, D_MODEL), lambda t: (0, t, 0)),
                  once(mod.shape), once((1, D_MODEL)), once((2 * SSM_WIDTH, D_MODEL)),
                  once((SLAB, SLAB))],
        out_specs=(out, out),
        scratch_shapes=[pltpu.VMEM((SUBLANES, n_chunks, D_MODEL), BF16)],
        compiler_params=pltpu.CompilerParams(
            dimension_semantics=("parallel",), vmem_limit_bytes=VMEM_LIMIT_BIG),
        name="s5_in",
    )(x3, mod, g, w_t, jnp.asarray(_slab_perm(), BF16))


def _gelu_tanh(v):
    return 0.5 * v * (1.0 + jnp.tanh(math.sqrt(2.0 / math.pi) * (v + 0.044715 * (v * v * v))))


def _s5_core_kernel(u_ref, toep_ref, mb_ref, mc_ref, al_ref, o_ref, *, chunks_per_seq):
    for gi in range(u_ref.shape[1]):
        _s5_core_group(u_ref.at[:, gi], toep_ref.at[gi], mb_ref.at[gi], mc_ref.at[gi], al_ref.at[gi],
                       o_ref.at[:, gi], chunks_per_seq)


def _s5_core_group(u_ref, toep_ref, mb_ref, mc_ref, al_ref, o_ref, chunks_per_seq):
    n_chunks = u_ref.shape[-1]
    ub = u_ref[...].reshape(CHUNK_ROWS, n_chunks)
    inc = jnp.dot(mb_ref[...], ub, preferred_element_type=F32)
    pos = lax.broadcasted_iota(jnp.int32, (STATE, n_chunks), 1) % chunks_per_seq
    reps = n_chunks // LANES

    def lane_tile(a):
        return jnp.concatenate([a] * reps, axis=1) if reps > 1 else a

    states = []
    for dr in range(2):
        xr = inc[(2 * dr) * STATE:(2 * dr + 1) * STATE]
        xi = inc[(2 * dr + 1) * STATE:(2 * dr + 2) * STATE]
        ar, ai = al_ref[2 * dr], al_ref[2 * dr + 1]

        def shifted(v, step):
            if dr == 0:
                return jnp.where(pos >= step, pltpu.roll(v, step, 1), 0.0)
            return jnp.where(pos < chunks_per_seq - step, pltpu.roll(v, n_chunks - step, 1), 0.0)

        step = 1
        while step < chunks_per_seq:
            sr, si = shifted(xr, step), shifted(xi, step)
            tr, ti = lane_tile(ar), lane_tile(ai)
            xr, xi = xr + tr * sr - ti * si, xi + tr * si + ti * sr
            ar, ai = _cmul(ar, ai, ar, ai)
            step *= 2
        states += [shifted(xr, 1), shifted(xi, 1)]
    h_in = jnp.concatenate(states, axis=0).astype(BF16)
    y = (jnp.dot(toep_ref[...], ub, preferred_element_type=F32)
         + jnp.dot(mc_ref[...], h_in, preferred_element_type=F32))
    o_ref[...] = _gelu_tanh(y).astype(BF16).reshape(CHUNK, GROUP_CH, n_chunks)


def _s5_core(u_cl, toep, mb, mc, al, chunks_per_seq):
    n_chunks = u_cl.shape[-1]
    u4 = u_cl.reshape(CHUNK, SSM_GROUPS, GROUP_CH, n_chunks)
    gs = S5_GROUPS_PER_STEP
    grp = pl.BlockSpec((CHUNK, gs, GROUP_CH, n_chunks), lambda g: (0, g, 0, 0))
    out = pl.pallas_call(
        functools.partial(_s5_core_kernel, chunks_per_seq=chunks_per_seq),
        out_shape=jax.ShapeDtypeStruct(u4.shape, BF16),
        grid=(SSM_GROUPS // gs,),
        in_specs=[grp,
                  pl.BlockSpec((gs, CHUNK_ROWS, CHUNK_ROWS), lambda g: (g, 0, 0)),
                  pl.BlockSpec((gs, 4 * STATE, CHUNK_ROWS), lambda g: (g, 0, 0)),
                  pl.BlockSpec((gs, CHUNK_ROWS, 4 * STATE), lambda g: (g, 0, 0)),
                  pl.BlockSpec((gs, 4, STATE, LANES), lambda g: (g, 0, 0, 0))],
        out_specs=grp,
        compiler_params=pltpu.CompilerParams(
            dimension_semantics=("parallel",), vmem_limit_bytes=VMEM_LIMIT),
        name="s5_core",
    )(u4, toep, mb, mc, al)
    return out.reshape(CHUNK, SSM_WIDTH, n_chunks)


def _s5_out_kernel(g_ref, z_ref, wglu_ref, wout_ref, pt_ref, x_ref, mod_ref, fg_ref, o_ref, y_scr,
                   *, final, chunks_per_seq):
    q = pl.program_id(1)
    n_half = x_ref.shape[0]
    k_per = g_ref.shape[0]
    n_glu = SUBLANES // k_per

    for qv in range(n_glu):
        @pl.when(q == qv)
        def _():
            for kk in range(k_per):
                gb = g_ref[kk]
                glu = jnp.dot(wglu_ref[...], gb, preferred_element_type=F32)
                y = gb.astype(F32) * _sigmoid(glu) * _silu(z_ref[kk].astype(F32))
                y_scr[qv * k_per + kk] = y.T.astype(BF16)

    for hv in range(y_scr.shape[1] // n_half):
        @pl.when(q == n_glu + hv)
        def _():
            for s in range(n_half // SLAB_CHUNKS):
                c0 = hv * n_half + s * SLAB_CHUNKS
                src = jnp.concatenate([y_scr[k, c0:c0 + SLAB_CHUNKS, :] for k in range(SUBLANES)], axis=0)
                yp = jnp.dot(pt_ref[...], src, preferred_element_type=F32).astype(BF16)
                out = jnp.dot(yp, wout_ref[...], preferred_element_type=F32)
                b = c0 // chunks_per_seq
                upd = mod_ref[b:b + 1, 2 * D_MODEL:] * out
                rows = slice(s * SLAB_CHUNKS, (s + 1) * SLAB_CHUNKS)
                xn = x_ref[rows] + upd.reshape(SLAB_CHUNKS, SUBLANES, D_MODEL)
                if final:
                    xn = xn * lax.rsqrt(jnp.mean(xn * xn, axis=-1, keepdims=True) + NORM_EPS) * fg_ref[...]
                o_ref[rows] = xn


def _s5_out(g_cl, z_cl, wglu_t, w_out, x3, mod, final_g, final, chunks_per_seq, n_split=2):
    n_chunks = x3.shape[0]
    n_half = n_chunks // n_split
    n_glu = SUBLANES // S5_OUT_K
    once = lambda shape: pl.BlockSpec(shape, lambda t, q: (0,) * len(shape), pipeline_mode=pl.Buffered(1))
    act = pl.BlockSpec((S5_OUT_K, SSM_WIDTH, n_chunks), lambda t, q: (t * n_glu + jnp.minimum(q, n_glu - 1), 0, 0))
    row = pl.BlockSpec((n_half, SUBLANES, D_MODEL), lambda t, q: (jnp.maximum(q - n_glu, 0), t, 0))
    return pl.pallas_call(
        functools.partial(_s5_out_kernel, final=final, chunks_per_seq=chunks_per_seq),
        out_shape=jax.ShapeDtypeStruct(x3.shape, F32),
        grid=(CHUNK // SUBLANES, n_glu + n_split),
        in_specs=[act, act, once((SSM_WIDTH, SSM_WIDTH)), once((SSM_WIDTH, D_MODEL)), once((SLAB, SLAB)),
                  row, once(mod.shape), once((1, D_MODEL))],
        out_specs=row,
        scratch_shapes=[pltpu.VMEM((SUBLANES, n_chunks, SSM_WIDTH), BF16)],
        compiler_params=pltpu.CompilerParams(
            dimension_semantics=("parallel", "arbitrary"), vmem_limit_bytes=VMEM_LIMIT_BIG),
        name="s5_out",
    )(g_cl, z_cl, wglu_t, w_out, jnp.asarray(_slab_perm().T, BF16), x3, mod, final_g)


def _rope_tables(s):
    inv_freq = ROPE_THETA ** (-jnp.arange(0, HEAD_DIM, 2, dtype=F32) / HEAD_DIM)
    ang = jnp.arange(s, dtype=F32)[:, None] * inv_freq[None, :]
    reps = LANES // (HEAD_DIM // 2)
    sign = jnp.where(jnp.arange(LANES) < LANES // 2, -1.0, 1.0).astype(F32)
    cos, sin = jnp.tile(jnp.cos(ang), (1, reps)), jnp.tile(jnp.sin(ang), (1, reps)) * sign
    stream_major = lambda a: a.reshape(s // STREAMS, STREAMS, LANES).transpose(1, 0, 2)
    return cos, sin, stream_major(cos), stream_major(sin)


def _prep_attn_w_in(w):
    d_in = w.shape[0]
    w = w.reshape(d_in, -1, ATTN_HEADS // 2, 2, 2, HEAD_DIM // 2)
    w = jnp.stack([w[:, i] for i in ATTN_IN_BLOCK_ORDER], axis=1)
    return w.transpose(0, 1, 2, 4, 3, 5).astype(BF16).reshape(d_in, -1)


def _prep_attn_w_out(w):
    w = w.astype(BF16).reshape(ATTN_HEADS // 2, 2, 2, HEAD_DIM // 2, w.shape[1])
    return w.transpose(0, 2, 1, 3, 4).reshape(ATTN_WIDTH, -1)


def _trunk(x, ada, norm_g, attn_w, s5_w, s5_ops, tables, final_norm_g):
    bsz, s, _ = x.shape
    chunks_per_seq = s // CHUNK
    n_chunks = bsz * chunks_per_seq
    fg = final_norm_g.reshape(1, D_MODEL)
    for i in range(DEPTH):
        mod = ada[i].reshape(bsz, 1, 3 * D_MODEL)
        g = norm_g[i].reshape(1, D_MODEL)
        j = i // N_MIXERS
        if i % N_MIXERS == 0:
            w_in, w_out = attn_w[j]
            nat, qkv12 = _attn_in(x, mod, g, w_in, tables)
            x = _attn_out(_attention(nat, qkv12), nat, x, mod, w_out)
        else:
            w_in_t, wglu_t, w_out = s5_w[j]
            toep, mb, mc, al = s5_ops[j]
            x3 = x.reshape(n_chunks, CHUNK, D_MODEL)
            u_cl, z_cl = _s5_in(x3, ada[i], g, w_in_t, chunks_per_seq)
            g_cl = _s5_core(u_cl, toep, mb, mc, al, chunks_per_seq)
            x3 = _s5_out(g_cl, z_cl, wglu_t, w_out, x3, ada[i], fg, i == DEPTH - 1, chunks_per_seq)
            x = x3.reshape(bsz, s, D_MODEL)
    return x


def kernel(x_prompt, x_sample, c_prompt, c_sample, norm_g, ada_w, ada_b, attn_w_in, attn_w_out,
           ssm_w_in, ssm_lam_re, ssm_lam_im, ssm_log_dt, ssm_b_re, ssm_b_im, ssm_c_re, ssm_c_im,
           ssm_d, ssm_w_glu, ssm_w_out, final_norm_g):
    assert (DEPTH - 1) % N_MIXERS == 1, "the final norm is fused into the last S5 layer"
    assert all(w // (2 * d) == RADIUS for w, d in DILATED_PAIRS)
    assert [d for _, d in DILATED_PAIRS] == [1, 4, STREAMS]
    n_prompt = c_prompt.shape[0]
    ada = _ada(jnp.concatenate([c_prompt, c_sample], axis=0), ada_w, ada_b)
    attn_w = [(_prep_attn_w_in(attn_w_in[j]), _prep_attn_w_out(attn_w_out[j]))
              for j in range(attn_w_in.shape[0])]
    s5_w, s5_ops = [], []
    for j in range(ssm_w_in.shape[0]):
        s5_w.append((ssm_w_in[j].T.astype(BF16), ssm_w_glu[j].T.astype(BF16), ssm_w_out[j].astype(BF16)))
        s5_ops.append(_s5_prep(ssm_lam_re[j], ssm_lam_im[j], ssm_log_dt[j], ssm_b_re[j],
                               ssm_b_im[j], ssm_c_re[j], ssm_c_im[j], ssm_d[j]))
    tables = _rope_tables(max(x_prompt.shape[1], x_sample.shape[1]))
    y_prompt = _trunk(x_prompt, ada[:, :n_prompt], norm_g, attn_w, s5_w, s5_ops, tables, final_norm_g)
    y_sample = _trunk(x_sample, ada[:, n_prompt:], norm_g, attn_w, s5_w, s5_ops, tables, final_norm_g)
    return (y_prompt, y_sample)
```

```python
import functools
import math

import numpy as np
import jax
import jax.numpy as jnp
from jax import lax
from jax.experimental import pallas as pl
from jax.experimental.pallas import tpu as pltpu

D_MODEL = 1024
DEPTH = 4
N_MIXERS = 2
ATTN_HEADS = 16
HEAD_DIM = 64
ATTN_WIDTH = ATTN_HEADS * HEAD_DIM
DILATED_PAIRS = ((128, 1), (512, 4), (2048, 16))
N_DIL = len(DILATED_PAIRS)
ROPE_THETA = 10000.0
SSM_WIDTH = D_MODEL
GROUP_CH = 16
SSM_GROUPS = SSM_WIDTH // GROUP_CH
STATE = 64
LAMBDA_RE_MAX = -1e-4
NORM_EPS = 1e-6
NEG_INF = -1e30

LANES = 128
SUBLANES = 8
CHUNK = 64
CHUNK_ROWS = CHUNK * GROUP_CH
T_PER_VREG = LANES // GROUP_CH
V_LANES = 2 * CHUNK_ROWS + LANES
STREAMS = 16
RADIUS = 64
ATTN_TQ = 128
ATTN_SUB = 4
STAT_PITCH = 40
ATTN_IN_BLOCK_ORDER = (1, 2, 0, 9, 4, 5, 7, 8, 3, 6)
PERM_ROWS = 512
UNPERM_ROWS = 256
LOG2_E = math.log2(math.e)
S5_GROUPS_PER_STEP = 2
S5_OUT_K = 4
SLAB = 256
SLAB_CHUNKS = SLAB // SUBLANES
VMEM_LIMIT = 48 * 1024 * 1024
VMEM_LIMIT_BIG = 56 * 1024 * 1024

F32 = jnp.float32
BF16 = jnp.bfloat16
HI = lax.Precision.HIGHEST


def _sigmoid(v):
    return 0.5 * jnp.tanh(0.5 * v) + 0.5


def _silu(v):
    return v * _sigmoid(v)


def _rms_modulate(x, g, scale, shift):
    rs = lax.rsqrt(jnp.mean(x * x, axis=-1, keepdims=True) + NORM_EPS)
    return (x * rs * g) * (1.0 + scale) + shift


def _ada_kernel(c_ref, w_ref, b_ref, o_ref):
    o_ref[...] = jnp.dot(_silu(c_ref[...]), w_ref[...], precision=HI,
                         preferred_element_type=F32) + b_ref[...]


def _ada(c, ada_w, ada_b):
    nb = c.shape[0]
    return pl.pallas_call(
        _ada_kernel,
        out_shape=jax.ShapeDtypeStruct((DEPTH, nb, 3 * D_MODEL), F32),
        grid=(DEPTH, 3),
        in_specs=[
            pl.BlockSpec((nb, D_MODEL), lambda i, j: (0, 0)),
            pl.BlockSpec((None, D_MODEL, D_MODEL), lambda i, j: (i, 0, j)),
            pl.BlockSpec((None, 1, D_MODEL), lambda i, j: (i, 0, j)),
        ],
        out_specs=pl.BlockSpec((None, nb, D_MODEL), lambda i, j: (i, 0, j)),
        name="ada",
    )(c, ada_w, ada_b.reshape(DEPTH, 1, 3 * D_MODEL))


def _attn_in_kernel(x_ref, shift_ref, scale_ref, g_ref, w_ref, cos_ref, sin_ref, cosp_ref, sinp_ref,
                    perm_ref, nat_ref, sm_ref, h_nat, h_perm):
    j = pl.program_id(2)
    tm = x_ref.shape[0]
    n_sub = tm // PERM_ROWS
    sub_rows = PERM_ROWS // STREAMS

    @pl.when(j == 0)
    def _():
        h = _rms_modulate(x_ref[...], g_ref[...], scale_ref[...], shift_ref[...])
        h_nat[...] = h.astype(BF16)
        for u in range(n_sub):
            rs = slice(u * PERM_ROWS, (u + 1) * PERM_ROWS)
            h_perm[rs, :] = jnp.dot(perm_ref[...], h_nat[rs, :], preferred_element_type=F32).astype(BF16)

    def rope_blocks(acc, cos, sin, second_is_plain):
        for half in range(2):
            is_q = jnp.logical_or(j == 4, jnp.logical_and(j == 1, half == 0))
            qscale = jnp.where(is_q, HEAD_DIM ** -0.5 * LOG2_E, 1.0).astype(F32)
            c, sn = cos * qscale, sin * qscale
            for b in range(ATTN_WIDTH // LANES):
                sl = slice(half * ATTN_WIDTH + b * LANES, half * ATTN_WIDTH + (b + 1) * LANES)
                t = acc[:, sl]
                if half == 1 and second_is_plain:
                    yield sl, t.astype(BF16)
                else:
                    yield sl, (t * c + pltpu.roll(t, LANES // 2, 1) * sn).astype(BF16)

    @pl.when(j < 2)
    def _():
        acc = jnp.dot(h_nat[...], w_ref[...], preferred_element_type=F32)
        for sl, blk in rope_blocks(acc, cos_ref[...], sin_ref[...], True):
            nat_ref[:, sl] = blk

    def stream_major_step(second_is_plain):
        acc = jnp.dot(h_perm[...], w_ref[...], preferred_element_type=F32)

        def table(ref):
            return jnp.concatenate([ref[:, u * sub_rows:(u + 1) * sub_rows, :].reshape(PERM_ROWS, LANES)
                                    for u in range(n_sub)], axis=0)

        for sl, blk in rope_blocks(acc, table(cosp_ref), table(sinp_ref), second_is_plain):
            for u in range(n_sub):
                sm_ref[:, u * sub_rows:(u + 1) * sub_rows, sl] = (
                    blk[u * PERM_ROWS:(u + 1) * PERM_ROWS].reshape(STREAMS, sub_rows, LANES))

    pl.when(jnp.logical_and(j >= 2, j < 4))(lambda: stream_major_step(True))
    pl.when(j == 4)(lambda: stream_major_step(False))


def _stream_perm(n):
    dst = np.arange(n)
    rows = n // STREAMS
    perm = np.zeros((n, n), np.float32)
    perm[dst, STREAMS * (dst % rows) + dst // rows] = 1.0
    return perm


def _attn_in(x, mod, g, w, tables, tm=1024):
    bsz, s, _ = x.shape
    cos, sin, cosp, sinp = tables
    rows = tm // STREAMS
    wide = 2 * ATTN_WIDTH
    perm = jnp.asarray(_stream_perm(PERM_ROWS), BF16)
    return pl.pallas_call(
        _attn_in_kernel,
        out_shape=(jax.ShapeDtypeStruct((bsz, s, 2 * wide), BF16),
                   jax.ShapeDtypeStruct((bsz, STREAMS, s // STREAMS, 3 * wide), BF16)),
        grid=(bsz, s // tm, 5),
        in_specs=[
            pl.BlockSpec((None, tm, D_MODEL), lambda b, i, j: (b, i, 0)),
            pl.BlockSpec((None, 1, D_MODEL), lambda b, i, j: (b, 0, 0)),
            pl.BlockSpec((None, 1, D_MODEL), lambda b, i, j: (b, 0, 1)),
            pl.BlockSpec((1, D_MODEL), lambda b, i, j: (0, 0)),
            pl.BlockSpec((D_MODEL, wide), lambda b, i, j: (0, j)),
            pl.BlockSpec((tm, LANES), lambda b, i, j: (i, 0)),
            pl.BlockSpec((tm, LANES), lambda b, i, j: (i, 0)),
            pl.BlockSpec((STREAMS, rows, LANES), lambda b, i, j: (0, i, 0)),
            pl.BlockSpec((STREAMS, rows, LANES), lambda b, i, j: (0, i, 0)),
            pl.BlockSpec((PERM_ROWS, PERM_ROWS), lambda b, i, j: (0, 0)),
        ],
        out_specs=(
            pl.BlockSpec((None, tm, wide), lambda b, i, j: (b, i, jnp.clip(j, 0, 1))),
            pl.BlockSpec((None, STREAMS, rows, wide), lambda b, i, j: (b, 0, i, jnp.clip(j - 2, 0, 2))),
        ),
        scratch_shapes=[pltpu.VMEM((tm, D_MODEL), BF16), pltpu.VMEM((tm, D_MODEL), BF16)],
        compiler_params=pltpu.CompilerParams(
            dimension_semantics=("parallel", "parallel", "arbitrary"),
            vmem_limit_bytes=VMEM_LIMIT_BIG),
        name="attn_in",
    )(x, mod, mod, g, w, cos, sin, cosp, sinp, perm)


def _lane_of_odd_head(lane):
    return (lane // (HEAD_DIM // 2)) % 2 == 1


def _mask_bias(valid):
    return jnp.where(valid, 0.0, NEG_INF).astype(F32)


def _attn_pairs(get_q, get_k, get_v, bias, put_acc):
    tq = bias.shape[0]
    lane = lax.broadcasted_iota(jnp.int32, (tq, LANES), 1)
    q_is_odd = _lane_of_odd_head(lane)
    mx_tile = jnp.zeros((tq, LANES), F32)
    den_tile = jnp.ones((tq, LANES), F32)
    ones = jnp.ones((bias.shape[1], LANES), BF16)
    bias2 = jnp.concatenate([bias, bias], axis=0)
    for b in range(ATTN_WIDTH // LANES):
        sl = slice(b * LANES, (b + 1) * LANES)
        qb = get_q(sl).astype(F32)
        kb = get_k(sl)
        vb = jnp.concatenate([get_v(sl), ones], axis=1)
        qm = jnp.concatenate([jnp.where(q_is_odd, 0.0, qb), jnp.where(q_is_odd, qb, 0.0)], axis=0).astype(BF16)
        sc = lax.dot_general(qm, kb, (((1,), (1,)), ((), ())), preferred_element_type=F32) + bias2
        mx = jnp.max(sc, axis=1, keepdims=True)
        p = jnp.exp2((sc - mx).astype(BF16))
        acc = jnp.dot(p, vb, preferred_element_type=F32)
        for odd in (0, 1):
            rows = slice(odd * tq, (odd + 1) * tq)
            mx_tile = jnp.where(lane == 2 * b + odd, mx[rows], mx_tile)
            den_tile = jnp.where(lane == 2 * b + odd, acc[rows, LANES:], den_tile)
        put_acc(sl, jnp.where(q_is_odd, acc[tq:, :LANES], acc[:tq, :LANES]))
    return mx_tile, den_tile


def _pair_factor(tile, b):
    lane = lax.broadcasted_iota(jnp.int32, tile.shape, 1)
    head = 2 * b + _lane_of_odd_head(lane).astype(jnp.int32)
    return jnp.take_along_axis(tile, head, axis=1, mode="promise_in_bounds")


def _merge_groups(stats, get_accs, put_o):
    top = functools.reduce(jnp.maximum, [mx for mx, _ in stats])
    ws = [jnp.exp2(mx - top) for mx, _ in stats]
    inv = 1.0 / sum(w * den for w, (_, den) in zip(ws, stats))
    fs = [w * inv for w in ws]
    for b in range(ATTN_WIDTH // LANES):
        sl = slice(b * LANES, (b + 1) * LANES)
        put_o(sl, sum(_pair_factor(f, b) * get_acc(sl) for f, get_acc in zip(fs, get_accs)))


def _window_mask(tile, tq, m):
    nk = tq + 2 * RADIUS
    row = lax.broadcasted_iota(jnp.int32, (tq, nk), 0)
    col = lax.broadcasted_iota(jnp.int32, (tq, nk), 1)
    kabs = tile * tq - RADIUS + col
    return _mask_bias((jnp.abs(col - RADIUS - row) <= RADIUS) & (kabs >= 0) & (kabs < m))


def _stack_rows(read, prev_ref, cur_ref, next_ref, lo, cnt):
    halo, cur_n = prev_ref.shape[-2], cur_ref.shape[-2]
    a, b = lo, lo + cnt
    parts = []
    if a < 0:
        parts.append(read(prev_ref, halo + a, halo))
        a = 0
    parts.append(read(cur_ref, a, min(b, cur_n)))
    if b > cur_n:
        parts.append(read(next_ref, 0, b - cur_n))
    return parts


def _kv_getters(stack):
    get_k = lambda sl: stack(sl)
    get_v = lambda sl: stack(slice(ATTN_WIDTH + sl.start, ATTN_WIDTH + sl.stop))
    return get_k, get_v


def _attn16_kernel(*refs, m):
    for r in range(refs[0].shape[0]):
        _attn16_stream(*[ref.at[r] for ref in refs], m=m)


def _attn16_stream(q_ref, kvp_ref, kvc_ref, kvn_ref, o_ref, mx_ref, den_ref, *, m):
    tq = ATTN_TQ
    n_sub = q_ref.shape[0] // tq
    for t in range(n_sub):
        rows = slice(t * tq, (t + 1) * tq)
        bias = _window_mask(pl.program_id(2) * n_sub + t, tq, m)

        def stack(cols):
            return jnp.concatenate(_stack_rows(lambda ref, a, b: ref[a:b, cols], kvp_ref, kvc_ref, kvn_ref,
                                               t * tq - RADIUS, tq + 2 * RADIUS), axis=0)

        def put_acc(sl, val):
            o_ref[rows, sl] = val.astype(BF16)

        mx_ref[rows], den_ref[rows] = _attn_pairs(lambda sl: q_ref[rows, sl], *_kv_getters(stack), bias, put_acc)


def _attn4_kernel(*refs, m):
    for r in range(refs[0].shape[1]):
        _attn4_residue(*[ref.at[:, r] for ref in refs], m=m)


def _attn4_residue(q_ref, kvp_ref, kvc_ref, kvn_ref, o_ref, mx_ref, den_ref, *, m):
    ns, halo = q_ref.shape[0], kvp_ref.shape[1]
    rows = ATTN_TQ // ns
    span = rows + 2 * halo
    n_sub = q_ref.shape[1] // rows
    n = lax.broadcasted_iota(jnp.int32, (ns * rows, ns * span), 0)
    c = lax.broadcasted_iota(jnp.int32, (ns * rows, ns * span), 1)
    for t in range(n_sub):
        base = (pl.program_id(2) * n_sub + t) * rows
        rs = slice(t * rows, (t + 1) * rows)
        krow = base - halo + c % span
        rel = ns * (krow - (base + n % rows)) + (c // span - n // rows)
        bias = _mask_bias((jnp.abs(rel) <= RADIUS) & (krow >= 0) & (krow < m))

        def stack(cols):
            parts = []
            for s in range(ns):
                parts += _stack_rows(lambda ref, a, b: ref[s, a:b, cols], kvp_ref, kvc_ref, kvn_ref,
                                     t * rows - halo, span)
            return jnp.concatenate(parts, axis=0)

        flat = lambda ref: (lambda sl: ref[:, rs, sl].reshape(ns * rows, LANES))

        def put_acc(sl, val):
            o_ref[:, rs, sl] = val.astype(BF16).reshape(ns, rows, LANES)

        mx, den = _attn_pairs(flat(q_ref), *_kv_getters(stack), bias, put_acc)
        mx_ref[:, rs, :] = mx.reshape(ns, rows, LANES)
        den_ref[:, rs, :] = den.reshape(ns, rows, LANES)


def _attn1_kernel(q_ref, kvp_ref, kvc_ref, kvn_ref, o_ref, mx_ref, den_ref, *, m):
    tq = ATTN_TQ
    n_sub = q_ref.shape[0] // tq
    for t in range(n_sub):
        rows = slice(t * tq, (t + 1) * tq)
        bias = _window_mask(pl.program_id(1) * n_sub + t, tq, m)

        def stack(cols):
            return jnp.concatenate(_stack_rows(lambda ref, a, b: ref[a:b, cols], kvp_ref, kvc_ref, kvn_ref,
                                               t * tq - RADIUS, tq + 2 * RADIUS), axis=0)

        def put_acc(sl, val):
            o_ref[rows, sl] = val.astype(BF16)

        mx_ref[rows], den_ref[rows] = _attn_pairs(lambda sl: q_ref[rows, sl], *_kv_getters(stack), bias, put_acc)


def _attention(qkv0, qkv12):
    bsz, s, _ = qkv0.shape
    m16 = s // STREAMS
    tq, half = ATTN_TQ, RADIUS
    cur_rows = min(ATTN_SUB * tq, m16)
    params = lambda n: pltpu.CompilerParams(dimension_semantics=("parallel",) * n,
                                            vmem_limit_bytes=VMEM_LIMIT)
    same = lambda i: i

    def halo_idx(cur_n, halo_n, total):
        per = cur_n // halo_n
        return (lambda i: jnp.maximum(i * per - 1, 0)), (lambda i: jnp.minimum((i + 1) * per, total // halo_n - 1))

    n_seq = (ATTN_SUB * tq) // cur_rows

    def spec16(nrows, row_idx, col, width=ATTN_WIDTH):
        return pl.BlockSpec((None, n_seq, nrows, width), lambda b, r, i: (b, r, row_idx(i), col))
    prv, nxt = halo_idx(cur_rows, half, m16)
    stat16 = jax.ShapeDtypeStruct((bsz, STREAMS, m16, LANES), F32)
    o2, mx2, den2 = pl.pallas_call(
        functools.partial(_attn16_kernel, m=m16),
        out_shape=(jax.ShapeDtypeStruct((bsz, STREAMS, m16, ATTN_WIDTH), BF16), stat16, stat16),
        grid=(bsz, STREAMS // n_seq, m16 // cur_rows),
        in_specs=[spec16(cur_rows, same, 5), spec16(half, prv, 1, 2 * ATTN_WIDTH),
                  spec16(cur_rows, same, 1, 2 * ATTN_WIDTH), spec16(half, nxt, 1, 2 * ATTN_WIDTH)],
        out_specs=(spec16(cur_rows, same, 0), spec16(cur_rows, same, 0, LANES), spec16(cur_rows, same, 0, LANES)),
        compiler_params=params(3), name="attn_dil16",
    )(*([qkv12] * 4))

    ns = STREAMS // 4
    rows4, halo4 = cur_rows // ns, half // ns

    def view4(a):
        return a.reshape(bsz, ns, 4, m16, a.shape[-1])

    def spec4(nrows, row_idx, col, width=ATTN_WIDTH):
        return pl.BlockSpec((None, ns, n_seq, nrows, width), lambda b, r, i: (b, 0, r, row_idx(i), col))
    prv4, nxt4 = halo_idx(rows4, halo4, m16)
    q4 = view4(qkv12)
    stat4 = jax.ShapeDtypeStruct((bsz, ns, 4, m16, LANES), F32)
    o1, mx1, den1 = pl.pallas_call(
        functools.partial(_attn4_kernel, m=m16),
        out_shape=(jax.ShapeDtypeStruct((bsz, ns, 4, m16, ATTN_WIDTH), BF16), stat4, stat4),
        grid=(bsz, 4 // n_seq, m16 // rows4),
        in_specs=[spec4(rows4, same, 4), spec4(halo4, prv4, 0, 2 * ATTN_WIDTH),
                  spec4(rows4, same, 0, 2 * ATTN_WIDTH), spec4(halo4, nxt4, 0, 2 * ATTN_WIDTH)],
        out_specs=(spec4(rows4, same, 0), spec4(rows4, same, 0, LANES), spec4(rows4, same, 0, LANES)),
        compiler_params=params(3), name="attn_dil4",
    )(q4, q4, q4, q4)
    streams = lambda a: a.reshape(bsz, STREAMS, m16, a.shape[-1])

    cur1 = 2 * ATTN_SUB * tq

    def spec1(nrows, row_idx, col, width=ATTN_WIDTH):
        return pl.BlockSpec((None, nrows, width), lambda b, i: (b, row_idx(i), col))
    prv1, nxt1 = halo_idx(cur1, half, s)
    stat1 = jax.ShapeDtypeStruct((bsz, s, LANES), F32)
    group0 = pl.pallas_call(
        functools.partial(_attn1_kernel, m=s),
        out_shape=(jax.ShapeDtypeStruct((bsz, s, ATTN_WIDTH), BF16), stat1, stat1),
        grid=(bsz, s // cur1),
        in_specs=[spec1(cur1, same, 2), spec1(half, prv1, 0, 2 * ATTN_WIDTH),
                  spec1(cur1, same, 0, 2 * ATTN_WIDTH), spec1(half, nxt1, 0, 2 * ATTN_WIDTH)],
        out_specs=(spec1(cur1, same, 0), spec1(cur1, same, 0, LANES), spec1(cur1, same, 0, LANES)),
        compiler_params=params(2), name="attn_dil1",
    )(*([qkv0] * 4))
    return group0, (o2, mx2, den2), (streams(o1), streams(mx1), streams(den1))


def _attn_out_kernel(o0_ref, mx0_ref, den0_ref, o2_ref, mx2_ref, den2_ref, o1_ref, mx1_ref, den1_ref,
                     unperm_ref, z_ref, x_ref, gate_ref, w_ref, out_ref, y_scr, a2_scr, a1_scr, stat_scr):
    tm = x_ref.shape[0]
    tq = ATTN_TQ
    srow, n_srow = tq // STREAMS, mx2_ref.shape[1]
    n_un = unperm_ref.shape[0]
    for k, src in enumerate((mx2_ref, den2_ref, mx1_ref, den1_ref)):
        for r in range(STREAMS):
            stat_scr[k, r * STAT_PITCH:r * STAT_PITCH + n_srow, :] = src[r]
    for u in range(tm // n_un):
        piece = slice(u * n_un, (u + 1) * n_un)
        for src, dst in ((o2_ref, a2_scr), (o1_ref, a1_scr)):
            rs = slice(u * n_un // STREAMS, (u + 1) * n_un // STREAMS)
            part = jnp.concatenate([src[r, rs, :] for r in range(STREAMS)], axis=0)
            dst[piece, :] = jnp.dot(unperm_ref[...], part, preferred_element_type=F32)
        for t in range(u * n_un // tq, (u + 1) * n_un // tq):
            rows = slice(t * tq, (t + 1) * tq)

            def natural(k):
                return jnp.concatenate([stat_scr[k, pl.ds(t * srow + ii, STREAMS, stride=STAT_PITCH), :]
                                        for ii in range(srow)], axis=0)

            def put_y(sl, val):
                y_scr[rows, sl] = (val * _silu(z_ref[rows, sl].astype(F32))).astype(BF16)

            stats = [(mx0_ref[rows], den0_ref[rows]), (natural(0), natural(1)), (natural(2), natural(3))]
            _merge_groups(stats, [lambda sl: o0_ref[rows, sl].astype(F32), lambda sl: a2_scr[rows, sl],
                                  lambda sl: a1_scr[rows, sl]], put_y)
        out = jnp.dot(y_scr[piece, :], w_ref[...], preferred_element_type=F32)
        out_ref[piece, :] = x_ref[piece, :] + gate_ref[...] * out


def _attn_out(groups, nat, x, mod, w, tm=PERM_ROWS):
    bsz, s, _ = x.shape
    row = lambda width: pl.BlockSpec((None, tm, width), lambda b, i: (b, i, 0))
    sm = lambda width: pl.BlockSpec((None, STREAMS, tm // STREAMS, width), lambda b, i: (b, 0, i, 0))
    z_col = nat.shape[-1] // ATTN_WIDTH - 1
    unperm = jnp.asarray(_stream_perm(UNPERM_ROWS).T, BF16)
    return pl.pallas_call(
        _attn_out_kernel,
        out_shape=jax.ShapeDtypeStruct(x.shape, F32),
        grid=(bsz, s // tm),
        in_specs=[row(ATTN_WIDTH), row(LANES), row(LANES)] + [sm(ATTN_WIDTH), sm(LANES), sm(LANES)] * 2
                 + [pl.BlockSpec((UNPERM_ROWS, UNPERM_ROWS), lambda b, i: (0, 0)),
                    pl.BlockSpec((None, tm, ATTN_WIDTH), lambda b, i: (b, i, z_col)), row(D_MODEL),
                    pl.BlockSpec((None, 1, D_MODEL), lambda b, i: (b, 0, 2)),
                    pl.BlockSpec((ATTN_WIDTH, D_MODEL), lambda b, i: (0, 0))],
        out_specs=row(D_MODEL),
        scratch_shapes=[pltpu.VMEM((tm, ATTN_WIDTH), BF16), pltpu.VMEM((tm, ATTN_WIDTH), F32),
                        pltpu.VMEM((tm, ATTN_WIDTH), F32), pltpu.VMEM((4, STREAMS * STAT_PITCH, LANES), F32)],
        compiler_params=pltpu.CompilerParams(
            dimension_semantics=("parallel", "parallel"),
            vmem_limit_bytes=VMEM_LIMIT),
        name="attn_out",
    )(*groups[0], *groups[1], *groups[2], unperm, nat, x, mod, w)


def _cmul(ar, ai, br, bi):
    return ar * br - ai * bi, ar * bi + ai * br


def _s5_prep_kernel(*refs):
    ins, d_diag, outs, v_scr = refs[:9], refs[9], refs[10:14], refs[14]
    for gi in range(d_diag.shape[0]):
        _s5_prep_group(*[r.at[:, gi] for r in ins], d_diag.at[gi], *[o.at[gi] for o in outs], v_scr.at[gi])


def _s5_prep_group(lam_re_a, lam_im_a, lam_re_b, lam_im_b, logdt, bt_re, bt_im,
                   c_re, c_im, d_diag, toep_ref, mb_ref, mc_ref, al_ref, v_scr):
    lane = lax.broadcasted_iota(jnp.int32, (STATE, LANES), 1)
    a_idx = lane // GROUP_CH
    eye = (lax.broadcasted_iota(jnp.int32, (STATE, STATE), 0)
           == lax.broadcasted_iota(jnp.int32, (STATE, STATE), 1))
    n_tiles = CHUNK // T_PER_VREG

    mb_rows = []
    v_rhs = []
    v_lhs = []
    mc_cols = []
    zbar_f = None
    for dr in range(2):
        dt = jnp.exp(logdt[dr])
        lrb = jnp.minimum(lam_re_b[dr], LAMBDA_RE_MAX)
        mag = jnp.exp(lrb * dt)
        pw_r, pw_i = mag * jnp.cos(lam_im_b[dr] * dt), mag * jnp.sin(lam_im_b[dr] * dt)
        diag = lambda v: jnp.sum(jnp.where(eye, jnp.broadcast_to(v, (STATE, STATE)), 0.0), axis=1, keepdims=True)
        lr = jnp.minimum(lam_re_a[dr], LAMBDA_RE_MAX)
        li = lam_im_a[dr]
        l1r, l1i = diag(pw_r), diag(pw_i)
        small = [(jnp.ones_like(l1r), jnp.zeros_like(l1r)), (l1r, l1i)]
        for _ in range(T_PER_VREG - 1):
            small.append(_cmul(*small[-1], l1r, l1i))

        def lane_powers(exponent_of_step):
            pr, pi = small[exponent_of_step[0]]
            pr, pi = jnp.broadcast_to(pr, (STATE, LANES)), jnp.broadcast_to(pi, (STATE, LANES))
            for a in range(1, T_PER_VREG):
                er, ei = small[exponent_of_step[a]]
                pr, pi = jnp.where(a_idx == a, er, pr), jnp.where(a_idx == a, ei, pi)
            return pr, pi

        den = lr * lr + li * li
        nr, ni = l1r - 1.0, l1i
        cr, ci = (nr * lr + ni * li) / den, (ni * lr - nr * li) / den
        bbr, bbi = _cmul(cr, ci, bt_re[dr], bt_im[dr])
        l8r, l8i = small[T_PER_VREG]
        tiles = [None] * n_tiles
        if dr == 0:
            pr, pi = lane_powers([T_PER_VREG - 1 - a for a in range(T_PER_VREG)])
            cur = _cmul(pr, pi, bbr, bbi)
            for jt in range(n_tiles - 1, -1, -1):
                tiles[jt] = cur
                if jt:
                    cur = _cmul(cur[0], cur[1], l8r, l8i)
            v_rhs.append(_cmul(tiles[n_tiles - 1][0], tiles[n_tiles - 1][1], l1r, l1i))
            zbar_f = (jnp.where(lane < GROUP_CH, bbr, 0.0), jnp.where(lane < GROUP_CH, bbi, 0.0))
        else:
            pr, pi = lane_powers(list(range(T_PER_VREG)))
            cur = _cmul(pr, pi, bbr, bbi)
            for jt in range(n_tiles):
                tiles[jt] = cur
                if jt < n_tiles - 1:
                    cur = _cmul(cur[0], cur[1], l8r, l8i)
            v_rhs.append(tiles[0])
        mb_rows.append(jnp.concatenate([t[0] for t in tiles], axis=1))
        mb_rows.append(jnp.concatenate([t[1] for t in tiles], axis=1))
        a_r, a_i = l8r, l8i
        for _ in range(int(math.log2(CHUNK // T_PER_VREG))):
            a_r, a_i = _cmul(a_r, a_i, a_r, a_i)
        al_ref[2 * dr] = jnp.broadcast_to(a_r, (STATE, LANES))
        al_ref[2 * dr + 1] = jnp.broadcast_to(a_i, (STATE, LANES))

        pows = [(pw_r, pw_i)]
        for _ in range(int(math.log2(CHUNK)) - 1):
            pows.append(_cmul(*pows[-1], *pows[-1]))
        cre, cim = c_re[dr], c_im[dr]
        xr, xi = _cmul(cre, cim, *pows[0])
        for k in range(int(math.log2(CHUNK))):
            yr, yi = _cmul(xr, xi, *pows[k])
            if dr == 0:
                xr, xi = jnp.concatenate([xr, yr], 0), jnp.concatenate([xi, yi], 0)
            else:
                xr, xi = jnp.concatenate([yr, xr], 0), jnp.concatenate([yi, xi], 0)
        mc_cols += [xr, -xi]
        xr, xi = cre, cim
        for k in range(int(math.log2(n_tiles))):
            yr, yi = _cmul(xr, xi, *pows[k + int(math.log2(T_PER_VREG))])
            if dr == 0:
                xr, xi = jnp.concatenate([yr, xr], 0), jnp.concatenate([yi, xi], 0)
            else:
                xr, xi = jnp.concatenate([xr, yr], 0), jnp.concatenate([xi, yi], 0)
        v_lhs.append((xr, xi))

    mb_ref[...] = jnp.concatenate(mb_rows, axis=0).astype(BF16)
    mc_ref[...] = jnp.concatenate(mc_cols, axis=1).astype(BF16)

    def cdot(lhs, rhs):
        return (jnp.dot(lhs[0], rhs[0], precision=HI, preferred_element_type=F32)
                - jnp.dot(lhs[1], rhs[1], precision=HI, preferred_element_type=F32))

    out_f = cdot(v_lhs[0], v_rhs[0])
    zero_rows = jnp.zeros((CHUNK_ROWS // T_PER_VREG - GROUP_CH, STATE), F32)
    lag0_lhs = (jnp.concatenate([c_re[0], zero_rows], 0), jnp.concatenate([c_im[0], zero_rows], 0))
    out_b = cdot(v_lhs[1], v_rhs[1]) + cdot(lag0_lhs, zbar_f)
    out_b = out_b + jnp.concatenate([d_diag[...], jnp.zeros((CHUNK_ROWS // T_PER_VREG - GROUP_CH, LANES), F32)], 0)
    for jt in range(n_tiles):
        rows = slice(jt * GROUP_CH, (jt + 1) * GROUP_CH)
        v_scr[:, jt * LANES:(jt + 1) * LANES] = out_f[rows]
        v_scr[:, (n_tiles + jt) * LANES:(n_tiles + jt + 1) * LANES] = out_b[rows]
    v_scr[:, 2 * CHUNK_ROWS:] = jnp.zeros((GROUP_CH, LANES), F32)

    for k in range(T_PER_VREG):
        vk = v_scr[:, k * GROUP_CH:k * GROUP_CH + 2 * CHUNK_ROWS]
        for t_out in range(CHUNK):
            off = (CHUNK - t_out) * GROUP_CH
            if off % LANES == k * GROUP_CH:
                al_off = off - k * GROUP_CH
                toep_ref[t_out * GROUP_CH:(t_out + 1) * GROUP_CH, :] = (
                    vk[:, al_off:al_off + CHUNK_ROWS].astype(BF16))


def _s5_prep(lam_re, lam_im, log_dt, b_re, b_im, c_re, c_im, d_skip):
    g = SSM_GROUPS
    d_diag = d_skip.reshape(g, GROUP_CH, 1) * jnp.eye(GROUP_CH, LANES, dtype=F32)
    lam_a = lambda a: a.reshape(2, g, STATE, 1)
    lam_b = lambda a: a.reshape(2, g, 1, STATE)
    tile_b = lambda a: jnp.tile(a, (1, 1, 1, T_PER_VREG))
    gs = S5_GROUPS_PER_STEP
    dir_spec = lambda r, c: pl.BlockSpec((2, gs, r, c), lambda i: (0, i, 0, 0))
    return pl.pallas_call(
        _s5_prep_kernel,
        out_shape=(jax.ShapeDtypeStruct((g, CHUNK_ROWS, CHUNK_ROWS), BF16),
                   jax.ShapeDtypeStruct((g, 4 * STATE, CHUNK_ROWS), BF16),
                   jax.ShapeDtypeStruct((g, CHUNK_ROWS, 4 * STATE), BF16),
                   jax.ShapeDtypeStruct((g, 4, STATE, LANES), F32)),
        grid=(g // gs,),
        in_specs=[dir_spec(STATE, 1), dir_spec(STATE, 1), dir_spec(1, STATE), dir_spec(1, STATE),
                  dir_spec(1, 1), dir_spec(STATE, LANES), dir_spec(STATE, LANES),
                  dir_spec(GROUP_CH, STATE), dir_spec(GROUP_CH, STATE),
                  pl.BlockSpec((gs, GROUP_CH, LANES), lambda i: (i, 0, 0))],
        out_specs=(pl.BlockSpec((gs, CHUNK_ROWS, CHUNK_ROWS), lambda i: (i, 0, 0)),
                   pl.BlockSpec((gs, 4 * STATE, CHUNK_ROWS), lambda i: (i, 0, 0)),
                   pl.BlockSpec((gs, CHUNK_ROWS, 4 * STATE), lambda i: (i, 0, 0)),
                   pl.BlockSpec((gs, 4, STATE, LANES), lambda i: (i, 0, 0, 0))),
        scratch_shapes=[pltpu.VMEM((gs, GROUP_CH, V_LANES), F32)],
        compiler_params=pltpu.CompilerParams(dimension_semantics=("parallel",)),
        name="s5_prep",
    )(lam_a(lam_re), lam_a(lam_im), lam_b(lam_re), lam_b(lam_im), log_dt.reshape(2, g, 1, 1),
      tile_b(b_re), tile_b(b_im), c_re, c_im, d_diag)


def _slab_perm():
    dst = np.arange(SLAB)
    perm = np.zeros((SLAB, SLAB), np.float32)
    perm[dst, (dst % SLAB_CHUNKS) * SUBLANES + dst // SLAB_CHUNKS] = 1.0
    return perm


def _s5_in_kernel(x_ref, mod_ref, g_ref, w_ref, p_ref, u_ref, z_ref, h_scr, *, chunks_per_seq):
    n_chunks = x_ref.shape[0]
    for b in range(n_chunks // chunks_per_seq):
        rows = slice(b * chunks_per_seq, (b + 1) * chunks_per_seq)
        h = _rms_modulate(x_ref[rows], g_ref[...], mod_ref[b:b + 1, D_MODEL:2 * D_MODEL],
                          mod_ref[b:b + 1, :D_MODEL])
        h2 = h.reshape(chunks_per_seq * SUBLANES, D_MODEL).astype(BF16)
        for s in range(chunks_per_seq // SLAB_CHUNKS):
            hp = jnp.dot(p_ref[...], h2[s * SLAB:(s + 1) * SLAB], preferred_element_type=F32)
            c0 = b * chunks_per_seq + s * SLAB_CHUNKS
            for k in range(SUBLANES):
                h_scr[k, c0:c0 + SLAB_CHUNKS, :] = hp[k * SLAB_CHUNKS:(k + 1) * SLAB_CHUNKS].astype(BF16)
    for k in range(SUBLANES):
        ht = h_scr[k].astype(F32).T.astype(BF16)
        uz = jnp.dot(w_ref[...], ht, preferred_element_type=F32)
        u_ref[k] = uz[:SSM_WIDTH].astype(BF16)
        z_ref[k] = uz[SSM_WIDTH:].astype(BF16)


def _s5_in(x3, mod, g, w_t, chunks_per_seq):
    n_chunks = x3.shape[0]
    once = lambda shape: pl.BlockSpec(shape, lambda t: (0,) * len(shape), pipeline_mode=pl.Buffered(1))
    out = pl.BlockSpec((SUBLANES, SSM_WIDTH, n_chunks), lambda t: (t, 0, 0))
    return pl.pallas_call(
        functools.partial(_s5_in_kernel, chunks_per_seq=chunks_per_seq),
        out_shape=(jax.ShapeDtypeStruct((CHUNK, SSM_WIDTH, n_chunks), BF16),) * 2,
        grid=(CHUNK // SUBLANES,),
        in_specs=[pl.BlockSpec((n_chunks, SUBLANES, D_MODEL), lambda t: (0, t, 0)),
                  once(mod.shape), once((1, D_MODEL)), once((2 * SSM_WIDTH, D_MODEL)),
                  once((SLAB, SLAB))],
        out_specs=(out, out),
        scratch_shapes=[pltpu.VMEM((SUBLANES, n_chunks, D_MODEL), BF16)],
        compiler_params=pltpu.CompilerParams(
            dimension_semantics=("parallel",), vmem_limit_bytes=VMEM_LIMIT_BIG),
        name="s5_in",
    )(x3, mod, g, w_t, jnp.asarray(_slab_perm(), BF16))


def _gelu_tanh(v):
    return 0.5 * v * (1.0 + jnp.tanh(math.sqrt(2.0 / math.pi) * (v + 0.044715 * (v * v * v))))


def _s5_core_kernel(u_ref, toep_ref, mb_ref, mc_ref, al_ref, o_ref, *, chunks_per_seq):
    for gi in range(u_ref.shape[1]):
        _s5_core_group(u_ref.at[:, gi], toep_ref.at[gi], mb_ref.at[gi], mc_ref.at[gi], al_ref.at[gi],
                       o_ref.at[:, gi], chunks_per_seq)


def _s5_core_group(u_ref, toep_ref, mb_ref, mc_ref, al_ref, o_ref, chunks_per_seq):
    n_chunks = u_ref.shape[-1]
    ub = u_ref[...].reshape(CHUNK_ROWS, n_chunks)
    inc = jnp.dot(mb_ref[...], ub, preferred_element_type=F32)
    pos = lax.broadcasted_iota(jnp.int32, (STATE, n_chunks), 1) % chunks_per_seq
    reps = n_chunks // LANES

    def lane_tile(a):
        return jnp.concatenate([a] * reps, axis=1) if reps > 1 else a

    states = []
    for dr in range(2):
        xr = inc[(2 * dr) * STATE:(2 * dr + 1) * STATE]
        xi = inc[(2 * dr + 1) * STATE:(2 * dr + 2) * STATE]
        ar, ai = al_ref[2 * dr], al_ref[2 * dr + 1]

        def shifted(v, step):
            if dr == 0:
                return jnp.where(pos >= step, pltpu.roll(v, step, 1), 0.0)
            return jnp.where(pos < chunks_per_seq - step, pltpu.roll(v, n_chunks - step, 1), 0.0)

        step = 1
        while step < chunks_per_seq:
            sr, si = shifted(xr, step), shifted(xi, step)
            tr, ti = lane_tile(ar), lane_tile(ai)
            xr, xi = xr + tr * sr - ti * si, xi + tr * si + ti * sr
            ar, ai = _cmul(ar, ai, ar, ai)
            step *= 2
        states += [shifted(xr, 1), shifted(xi, 1)]
    h_in = jnp.concatenate(states, axis=0).astype(BF16)
    y = (jnp.dot(toep_ref[...], ub, preferred_element_type=F32)
         + jnp.dot(mc_ref[...], h_in, preferred_element_type=F32))
    o_ref[...] = _gelu_tanh(y).astype(BF16).reshape(CHUNK, GROUP_CH, n_chunks)


def _s5_core(u_cl, toep, mb, mc, al, chunks_per_seq):
    n_chunks = u_cl.shape[-1]
    u4 = u_cl.reshape(CHUNK, SSM_GROUPS, GROUP_CH, n_chunks)
    gs = S5_GROUPS_PER_STEP
    grp = pl.BlockSpec((CHUNK, gs, GROUP_CH, n_chunks), lambda g: (0, g, 0, 0))
    out = pl.pallas_call(
        functools.partial(_s5_core_kernel, chunks_per_seq=chunks_per_seq),
        out_shape=jax.ShapeDtypeStruct(u4.shape, BF16),
        grid=(SSM_GROUPS // gs,),
        in_specs=[grp,
                  pl.BlockSpec((gs, CHUNK_ROWS, CHUNK_ROWS), lambda g: (g, 0, 0)),
                  pl.BlockSpec((gs, 4 * STATE, CHUNK_ROWS), lambda g: (g, 0, 0)),
                  pl.BlockSpec((gs, CHUNK_ROWS, 4 * STATE), lambda g: (g, 0, 0)),
                  pl.BlockSpec((gs, 4, STATE, LANES), lambda g: (g, 0, 0, 0))],
        out_specs=grp,
        compiler_params=pltpu.CompilerParams(
            dimension_semantics=("parallel",), vmem_limit_bytes=VMEM_LIMIT),
        name="s5_core",
    )(u4, toep, mb, mc, al)
    return out.reshape(CHUNK, SSM_WIDTH, n_chunks)


def _s5_out_kernel(g_ref, z_ref, wglu_ref, wout_ref, pt_ref, x_ref, mod_ref, fg_ref, o_ref, y_scr,
                   *, final, chunks_per_seq):
    q = pl.program_id(1)
    n_half = x_ref.shape[0]
    k_per = g_ref.shape[0]
    n_glu = SUBLANES // k_per

    for qv in range(n_glu):
        @pl.when(q == qv)
        def _():
            for kk in range(k_per):
                gb = g_ref[kk]
                glu = jnp.dot(wglu_ref[...], gb, preferred_element_type=F32)
                y = gb.astype(F32) * _sigmoid(glu) * _silu(z_ref[kk].astype(F32))
                y_scr[qv * k_per + kk] = y.T.astype(BF16)

    for hv in range(y_scr.shape[1] // n_half):
        @pl.when(q == n_glu + hv)
        def _():
            for s in range(n_half // SLAB_CHUNKS):
                c0 = hv * n_half + s * SLAB_CHUNKS
                src = jnp.concatenate([y_scr[k, c0:c0 + SLAB_CHUNKS, :] for k in range(SUBLANES)], axis=0)
                yp = jnp.dot(pt_ref[...], src, preferred_element_type=F32).astype(BF16)
                out = jnp.dot(yp, wout_ref[...], preferred_element_type=F32)
                b = c0 // chunks_per_seq
                upd = mod_ref[b:b + 1, 2 * D_MODEL:] * out
                rows = slice(s * SLAB_CHUNKS, (s + 1) * SLAB_CHUNKS)
                xn = x_ref[rows] + upd.reshape(SLAB_CHUNKS, SUBLANES, D_MODEL)
                if final:
                    xn = xn * lax.rsqrt(jnp.mean(xn * xn, axis=-1, keepdims=True) + NORM_EPS) * fg_ref[...]
                o_ref[rows] = xn


def _s5_out(g_cl, z_cl, wglu_t, w_out, x3, mod, final_g, final, chunks_per_seq, n_split=2):
    n_chunks = x3.shape[0]
    n_half = n_chunks // n_split
    n_glu = SUBLANES // S5_OUT_K
    once = lambda shape: pl.BlockSpec(shape, lambda t, q: (0,) * len(shape), pipeline_mode=pl.Buffered(1))
    act = pl.BlockSpec((S5_OUT_K, SSM_WIDTH, n_chunks), lambda t, q: (t * n_glu + jnp.minimum(q, n_glu - 1), 0, 0))
    row = pl.BlockSpec((n_half, SUBLANES, D_MODEL), lambda t, q: (jnp.maximum(q - n_glu, 0), t, 0))
    return pl.pallas_call(
        functools.partial(_s5_out_kernel, final=final, chunks_per_seq=chunks_per_seq),
        out_shape=jax.ShapeDtypeStruct(x3.shape, F32),
        grid=(CHUNK // SUBLANES, n_glu + n_split),
        in_specs=[act, act, once((SSM_WIDTH, SSM_WIDTH)), once((SSM_WIDTH, D_MODEL)), once((SLAB, SLAB)),
                  row, once(mod.shape), once((1, D_MODEL))],
        out_specs=row,
        scratch_shapes=[pltpu.VMEM((SUBLANES, n_chunks, SSM_WIDTH), BF16)],
        compiler_params=pltpu.CompilerParams(
            dimension_semantics=("parallel", "arbitrary"), vmem_limit_bytes=VMEM_LIMIT_BIG),
        name="s5_out",
    )(g_cl, z_cl, wglu_t, w_out, jnp.asarray(_slab_perm().T, BF16), x3, mod, final_g)


def _rope_tables(s):
    inv_freq = ROPE_THETA ** (-jnp.arange(0, HEAD_DIM, 2, dtype=F32) / HEAD_DIM)
    ang = jnp.arange(s, dtype=F32)[:, None] * inv_freq[None, :]
    reps = LANES // (HEAD_DIM // 2)
    sign = jnp.where(jnp.arange(LANES) < LANES // 2, -1.0, 1.0).astype(F32)
    cos, sin = jnp.tile(jnp.cos(ang), (1, reps)), jnp.tile(jnp.sin(ang), (1, reps)) * sign
    stream_major = lambda a: a.reshape(s // STREAMS, STREAMS, LANES).transpose(1, 0, 2)
    return cos, sin, stream_major(cos), stream_major(sin)


def _prep_attn_w_in(w):
    d_in = w.shape[0]
    w = w.reshape(d_in, -1, ATTN_HEADS // 2, 2, 2, HEAD_DIM // 2)
    w = jnp.stack([w[:, i] for i in ATTN_IN_BLOCK_ORDER], axis=1)
    return w.transpose(0, 1, 2, 4, 3, 5).astype(BF16).reshape(d_in, -1)


def _prep_attn_w_out(w):
    w = w.astype(BF16).reshape(ATTN_HEADS // 2, 2, 2, HEAD_DIM // 2, w.shape[1])
    return w.transpose(0, 2, 1, 3, 4).reshape(ATTN_WIDTH, -1)


def _trunk(x, ada, norm_g, attn_w, s5_w, s5_ops, tables, final_norm_g):
    bsz, s, _ = x.shape
    chunks_per_seq = s // CHUNK
    n_chunks = bsz * chunks_per_seq
    fg = final_norm_g.reshape(1, D_MODEL)
    for i in range(DEPTH):
        mod = ada[i].reshape(bsz, 1, 3 * D_MODEL)
        g = norm_g[i].reshape(1, D_MODEL)
        j = i // N_MIXERS
        if i % N_MIXERS == 0:
            w_in, w_out = attn_w[j]
            nat, qkv12 = _attn_in(x, mod, g, w_in, tables)
            x = _attn_out(_attention(nat, qkv12), nat, x, mod, w_out)
        else:
            w_in_t, wglu_t, w_out = s5_w[j]
            toep, mb, mc, al = s5_ops[j]
            x3 = x.reshape(n_chunks, CHUNK, D_MODEL)
            u_cl, z_cl = _s5_in(x3, ada[i], g, w_in_t, chunks_per_seq)
            g_cl = _s5_core(u_cl, toep, mb, mc, al, chunks_per_seq)
            x3 = _s5_out(g_cl, z_cl, wglu_t, w_out, x3, ada[i], fg, i == DEPTH - 1, chunks_per_seq)
            x = x3.reshape(bsz, s, D_MODEL)
    return x


def kernel(x_prompt, x_sample, c_prompt, c_sample, norm_g, ada_w, ada_b, attn_w_in, attn_w_out,
           ssm_w_in, ssm_lam_re, ssm_lam_im, ssm_log_dt, ssm_b_re, ssm_b_im, ssm_c_re, ssm_c_im,
           ssm_d, ssm_w_glu, ssm_w_out, final_norm_g):
    assert (DEPTH - 1) % N_MIXERS == 1, "the final norm is fused into the last S5 layer"
    assert all(w // (2 * d) == RADIUS for w, d in DILATED_PAIRS)
    assert [d for _, d in DILATED_PAIRS] == [1, 4, STREAMS]
    n_prompt = c_prompt.shape[0]
    ada = _ada(jnp.concatenate([c_prompt, c_sample], axis=0), ada_w, ada_b)
    attn_w = [(_prep_attn_w_in(attn_w_in[j]), _prep_attn_w_out(attn_w_out[j]))
              for j in range(attn_w_in.shape[0])]
    s5_w, s5_ops = [], []
    for j in range(ssm_w_in.shape[0]):
        s5_w.append((ssm_w_in[j].T.astype(BF16), ssm_w_glu[j].T.astype(BF16), ssm_w_out[j].astype(BF16)))
        s5_ops.append(_s5_prep(ssm_lam_re[j], ssm_lam_im[j], ssm_log_dt[j], ssm_b_re[j],
                               ssm_b_im[j], ssm_c_re[j], ssm_c_im[j], ssm_d[j]))
    tables = _rope_tables(max(x_prompt.shape[1], x_sample.shape[1]))
    y_prompt = _trunk(x_prompt, ada[:, :n_prompt], norm_g, attn_w, s5_w, s5_ops, tables, final_norm_g)
    y_sample = _trunk(x_sample, ada[:, n_prompt:], norm_g, attn_w, s5_w, s5_ops, tables, final_norm_g)
    return (y_prompt, y_sample)
```

```python
import functools
import math

import numpy as np
import jax
import jax.numpy as jnp
from jax import lax
from jax.experimental import pallas as pl
from jax.experimental.pallas import tpu as pltpu

D_MODEL = 1024
DEPTH = 4
N_MIXERS = 2
ATTN_HEADS = 16
HEAD_DIM = 64
ATTN_WIDTH = ATTN_HEADS * HEAD_DIM
DILATED_PAIRS = ((128, 1), (512, 4), (2048, 16))
N_DIL = len(DILATED_PAIRS)
ROPE_THETA = 10000.0
SSM_WIDTH = D_MODEL
GROUP_CH = 16
SSM_GROUPS = SSM_WIDTH // GROUP_CH
STATE = 64
LAMBDA_RE_MAX = -1e-4
NORM_EPS = 1e-6
NEG_INF = -1e30

LANES = 128
SUBLANES = 8
CHUNK = 64
CHUNK_ROWS = CHUNK * GROUP_CH
T_PER_VREG = LANES // GROUP_CH
V_LANES = 2 * CHUNK_ROWS + LANES
STREAMS = 16
RADIUS = 64
ATTN_TQ = 128
ATTN_SUB = 8
STAT_PITCH = 40
ATTN_IN_BLOCK_ORDER = (1, 2, 0, 9, 4, 5, 7, 8, 3, 6)
PERM_ROWS = 512
UNPERM_ROWS = 256
LOG2_E = math.log2(math.e)
S5_GROUPS_PER_STEP = 2
S5_OUT_K = 4
SLAB = 256
SLAB_CHUNKS = SLAB // SUBLANES
VMEM_LIMIT = 48 * 1024 * 1024
VMEM_LIMIT_BIG = 56 * 1024 * 1024

F32 = jnp.float32
BF16 = jnp.bfloat16
HI = lax.Precision.HIGHEST


def _sigmoid(v):
    return 0.5 * jnp.tanh(0.5 * v) + 0.5


def _silu(v):
    return v * _sigmoid(v)


def _rms_modulate(x, g, scale, shift):
    rs = lax.rsqrt(jnp.mean(x * x, axis=-1, keepdims=True) + NORM_EPS)
    return (x * rs * g) * (1.0 + scale) + shift


def _ada_kernel(c_ref, w_ref, b_ref, o_ref):
    o_ref[...] = jnp.dot(_silu(c_ref[...]), w_ref[...], precision=HI,
                         preferred_element_type=F32) + b_ref[...]


def _ada(c, ada_w, ada_b):
    nb = c.shape[0]
    return pl.pallas_call(
        _ada_kernel,
        out_shape=jax.ShapeDtypeStruct((DEPTH, nb, 3 * D_MODEL), F32),
        grid=(DEPTH, 3),
        in_specs=[
            pl.BlockSpec((nb, D_MODEL), lambda i, j: (0, 0)),
            pl.BlockSpec((None, D_MODEL, D_MODEL), lambda i, j: (i, 0, j)),
            pl.BlockSpec((None, 1, D_MODEL), lambda i, j: (i, 0, j)),
        ],
        out_specs=pl.BlockSpec((None, nb, D_MODEL), lambda i, j: (i, 0, j)),
        name="ada",
    )(c, ada_w, ada_b.reshape(DEPTH, 1, 3 * D_MODEL))


def _attn_in_kernel(x_ref, shift_ref, scale_ref, g_ref, w_ref, cos_ref, sin_ref, cosp_ref, sinp_ref,
                    perm_ref, nat_ref, sm_ref, h_nat, h_perm):
    j = pl.program_id(2)
    tm = x_ref.shape[0]
    n_sub = tm // PERM_ROWS
    sub_rows = PERM_ROWS // STREAMS

    @pl.when(j == 0)
    def _():
        h = _rms_modulate(x_ref[...], g_ref[...], scale_ref[...], shift_ref[...])
        h_nat[...] = h.astype(BF16)
        for u in range(n_sub):
            rs = slice(u * PERM_ROWS, (u + 1) * PERM_ROWS)
            h_perm[rs, :] = jnp.dot(perm_ref[...], h_nat[rs, :], preferred_element_type=F32).astype(BF16)

    def rope_blocks(acc, cos, sin, second_is_plain):
        for half in range(2):
            is_q = jnp.logical_or(j == 4, jnp.logical_and(j == 1, half == 0))
            qscale = jnp.where(is_q, HEAD_DIM ** -0.5 * LOG2_E, 1.0).astype(F32)
            c, sn = cos * qscale, sin * qscale
            for b in range(ATTN_WIDTH // LANES):
                sl = slice(half * ATTN_WIDTH + b * LANES, half * ATTN_WIDTH + (b + 1) * LANES)
                t = acc[:, sl]
                if half == 1 and second_is_plain:
                    yield sl, t.astype(BF16)
                else:
                    yield sl, (t * c + pltpu.roll(t, LANES // 2, 1) * sn).astype(BF16)

    @pl.when(j < 2)
    def _():
        acc = jnp.dot(h_nat[...], w_ref[...], preferred_element_type=F32)
        for sl, blk in rope_blocks(acc, cos_ref[...], sin_ref[...], True):
            nat_ref[:, sl] = blk

    def stream_major_step(second_is_plain):
        acc = jnp.dot(h_perm[...], w_ref[...], preferred_element_type=F32)

        def table(ref):
            return jnp.concatenate([ref[:, u * sub_rows:(u + 1) * sub_rows, :].reshape(PERM_ROWS, LANES)
                                    for u in range(n_sub)], axis=0)

        for sl, blk in rope_blocks(acc, table(cosp_ref), table(sinp_ref), second_is_plain):
            for u in range(n_sub):
                sm_ref[:, u * sub_rows:(u + 1) * sub_rows, sl] = (
                    blk[u * PERM_ROWS:(u + 1) * PERM_ROWS].reshape(STREAMS, sub_rows, LANES))

    pl.when(jnp.logical_and(j >= 2, j < 4))(lambda: stream_major_step(True))
    pl.when(j == 4)(lambda: stream_major_step(False))


def _stream_perm(n):
    dst = np.arange(n)
    rows = n // STREAMS
    perm = np.zeros((n, n), np.float32)
    perm[dst, STREAMS * (dst % rows) + dst // rows] = 1.0
    return perm


def _attn_in(x, mod, g, w, tables, tm=1024):
    bsz, s, _ = x.shape
    cos, sin, cosp, sinp = tables
    rows = tm // STREAMS
    wide = 2 * ATTN_WIDTH
    perm = jnp.asarray(_stream_perm(PERM_ROWS), BF16)
    return pl.pallas_call(
        _attn_in_kernel,
        out_shape=(jax.ShapeDtypeStruct((bsz, s, 2 * wide), BF16),
                   jax.ShapeDtypeStruct((bsz, STREAMS, s // STREAMS, 3 * wide), BF16)),
        grid=(bsz, s // tm, 5),
        in_specs=[
            pl.BlockSpec((None, tm, D_MODEL), lambda b, i, j: (b, i, 0)),
            pl.BlockSpec((None, 1, D_MODEL), lambda b, i, j: (b, 0, 0)),
            pl.BlockSpec((None, 1, D_MODEL), lambda b, i, j: (b, 0, 1)),
            pl.BlockSpec((1, D_MODEL), lambda b, i, j: (0, 0)),
            pl.BlockSpec((D_MODEL, wide), lambda b, i, j: (0, j)),
            pl.BlockSpec((tm, LANES), lambda b, i, j: (i, 0)),
            pl.BlockSpec((tm, LANES), lambda b, i, j: (i, 0)),
            pl.BlockSpec((STREAMS, rows, LANES), lambda b, i, j: (0, i, 0)),
            pl.BlockSpec((STREAMS, rows, LANES), lambda b, i, j: (0, i, 0)),
            pl.BlockSpec((PERM_ROWS, PERM_ROWS), lambda b, i, j: (0, 0)),
        ],
        out_specs=(
            pl.BlockSpec((None, tm, wide), lambda b, i, j: (b, i, jnp.clip(j, 0, 1))),
            pl.BlockSpec((None, STREAMS, rows, wide), lambda b, i, j: (b, 0, i, jnp.clip(j - 2, 0, 2))),
        ),
        scratch_shapes=[pltpu.VMEM((tm, D_MODEL), BF16), pltpu.VMEM((tm, D_MODEL), BF16)],
        compiler_params=pltpu.CompilerParams(
            dimension_semantics=("parallel", "parallel", "arbitrary"),
            vmem_limit_bytes=VMEM_LIMIT_BIG),
        name="attn_in",
    )(x, mod, mod, g, w, cos, sin, cosp, sinp, perm)


def _lane_of_odd_head(lane):
    return (lane // (HEAD_DIM // 2)) % 2 == 1


def _mask_bias(valid):
    return jnp.where(valid, 0.0, NEG_INF).astype(F32)


def _attn_pairs(get_q, get_k, get_v, bias, put_acc):
    tq = bias.shape[0]
    lane = lax.broadcasted_iota(jnp.int32, (tq, LANES), 1)
    q_is_odd = _lane_of_odd_head(lane)
    mx_tile = jnp.zeros((tq, LANES), F32)
    den_tile = jnp.ones((tq, LANES), F32)
    ones = jnp.ones((bias.shape[1], LANES), BF16)
    bias2 = jnp.concatenate([bias, bias], axis=0)
    for b in range(ATTN_WIDTH // LANES):
        sl = slice(b * LANES, (b + 1) * LANES)
        qb = get_q(sl).astype(F32)
        kb = get_k(sl)
        vb = jnp.concatenate([get_v(sl), ones], axis=1)
        qm = jnp.concatenate([jnp.where(q_is_odd, 0.0, qb), jnp.where(q_is_odd, qb, 0.0)], axis=0).astype(BF16)
        sc = lax.dot_general(qm, kb, (((1,), (1,)), ((), ())), preferred_element_type=F32) + bias2
        mx = jnp.max(sc, axis=1, keepdims=True)
        p = jnp.exp2((sc - mx).astype(BF16))
        acc = jnp.dot(p, vb, preferred_element_type=F32)
        for odd in (0, 1):
            rows = slice(odd * tq, (odd + 1) * tq)
            mx_tile = jnp.where(lane == 2 * b + odd, mx[rows], mx_tile)
            den_tile = jnp.where(lane == 2 * b + odd, acc[rows, LANES:], den_tile)
        put_acc(sl, jnp.where(q_is_odd, acc[tq:, :LANES], acc[:tq, :LANES]))
    return mx_tile, den_tile


def _pair_factor(tile, b):
    lane = lax.broadcasted_iota(jnp.int32, tile.shape, 1)
    head = 2 * b + _lane_of_odd_head(lane).astype(jnp.int32)
    return jnp.take_along_axis(tile, head, axis=1, mode="promise_in_bounds")


def _merge_groups(stats, get_accs, put_o):
    top = functools.reduce(jnp.maximum, [mx for mx, _ in stats])
    ws = [jnp.exp2(mx - top) for mx, _ in stats]
    inv = 1.0 / sum(w * den for w, (_, den) in zip(ws, stats))
    fs = [w * inv for w in ws]
    for b in range(ATTN_WIDTH // LANES):
        sl = slice(b * LANES, (b + 1) * LANES)
        put_o(sl, sum(_pair_factor(f, b) * get_acc(sl) for f, get_acc in zip(fs, get_accs)))


def _window_mask(tile, tq, m):
    nk = tq + 2 * RADIUS
    row = lax.broadcasted_iota(jnp.int32, (tq, nk), 0)
    col = lax.broadcasted_iota(jnp.int32, (tq, nk), 1)
    kabs = tile * tq - RADIUS + col
    return _mask_bias((jnp.abs(col - RADIUS - row) <= RADIUS) & (kabs >= 0) & (kabs < m))


def _stack_rows(read, prev_ref, cur_ref, next_ref, lo, cnt):
    halo, cur_n = prev_ref.shape[-2], cur_ref.shape[-2]
    a, b = lo, lo + cnt
    parts = []
    if a < 0:
        parts.append(read(prev_ref, halo + a, halo))
        a = 0
    parts.append(read(cur_ref, a, min(b, cur_n)))
    if b > cur_n:
        parts.append(read(next_ref, 0, b - cur_n))
    return parts


def _kv_getters(stack):
    get_k = lambda sl: stack(sl)
    get_v = lambda sl: stack(slice(ATTN_WIDTH + sl.start, ATTN_WIDTH + sl.stop))
    return get_k, get_v


def _attn16_kernel(*refs, m):
    for r in range(refs[0].shape[0]):
        _attn16_stream(*[ref.at[r] for ref in refs], m=m)


def _attn16_stream(q_ref, kvp_ref, kvc_ref, kvn_ref, o_ref, mx_ref, den_ref, *, m):
    tq = ATTN_TQ
    n_sub = q_ref.shape[0] // tq
    for t in range(n_sub):
        rows = slice(t * tq, (t + 1) * tq)
        bias = _window_mask(pl.program_id(2) * n_sub + t, tq, m)

        def stack(cols):
            return jnp.concatenate(_stack_rows(lambda ref, a, b: ref[a:b, cols], kvp_ref, kvc_ref, kvn_ref,
                                               t * tq - RADIUS, tq + 2 * RADIUS), axis=0)

        def put_acc(sl, val):
            o_ref[rows, sl] = val.astype(BF16)

        mx_ref[rows], den_ref[rows] = _attn_pairs(lambda sl: q_ref[rows, sl], *_kv_getters(stack), bias, put_acc)


def _attn4_kernel(*refs, m):
    for r in range(refs[0].shape[1]):
        _attn4_residue(*[ref.at[:, r] for ref in refs], m=m)


def _attn4_residue(q_ref, kvp_ref, kvc_ref, kvn_ref, o_ref, mx_ref, den_ref, *, m):
    ns, halo = q_ref.shape[0], kvp_ref.shape[1]
    rows = ATTN_TQ // ns
    span = rows + 2 * halo
    n_sub = q_ref.shape[1] // rows
    n = lax.broadcasted_iota(jnp.int32, (ns * rows, ns * span), 0)
    c = lax.broadcasted_iota(jnp.int32, (ns * rows, ns * span), 1)
    for t in range(n_sub):
        base = (pl.program_id(2) * n_sub + t) * rows
        rs = slice(t * rows, (t + 1) * rows)
        krow = base - halo + c % span
        rel = ns * (krow - (base + n % rows)) + (c // span - n // rows)
        bias = _mask_bias((jnp.abs(rel) <= RADIUS) & (krow >= 0) & (krow < m))

        def stack(cols):
            parts = []
            for s in range(ns):
                parts += _stack_rows(lambda ref, a, b: ref[s, a:b, cols], kvp_ref, kvc_ref, kvn_ref,
                                     t * rows - halo, span)
            return jnp.concatenate(parts, axis=0)

        flat = lambda ref: (lambda sl: ref[:, rs, sl].reshape(ns * rows, LANES))

        def put_acc(sl, val):
            o_ref[:, rs, sl] = val.astype(BF16).reshape(ns, rows, LANES)

        mx, den = _attn_pairs(flat(q_ref), *_kv_getters(stack), bias, put_acc)
        mx_ref[:, rs, :] = mx.reshape(ns, rows, LANES)
        den_ref[:, rs, :] = den.reshape(ns, rows, LANES)


def _attn1_kernel(q_ref, kvp_ref, kvc_ref, kvn_ref, o_ref, mx_ref, den_ref, *, m):
    tq = ATTN_TQ
    n_sub = q_ref.shape[0] // tq
    for t in range(n_sub):
        rows = slice(t * tq, (t + 1) * tq)
        bias = _window_mask(pl.program_id(1) * n_sub + t, tq, m)

        def stack(cols):
            return jnp.concatenate(_stack_rows(lambda ref, a, b: ref[a:b, cols], kvp_ref, kvc_ref, kvn_ref,
                                               t * tq - RADIUS, tq + 2 * RADIUS), axis=0)

        def put_acc(sl, val):
            o_ref[rows, sl] = val.astype(BF16)

        mx_ref[rows], den_ref[rows] = _attn_pairs(lambda sl: q_ref[rows, sl], *_kv_getters(stack), bias, put_acc)


def _attention(qkv0, qkv12):
    bsz, s, _ = qkv0.shape
    m16 = s // STREAMS
    tq, half = ATTN_TQ, RADIUS
    cur_rows = min(ATTN_SUB * tq, m16)
    params = lambda n: pltpu.CompilerParams(dimension_semantics=("parallel",) * n,
                                            vmem_limit_bytes=VMEM_LIMIT)
    same = lambda i: i

    def halo_idx(cur_n, halo_n, total):
        per = cur_n // halo_n
        return (lambda i: jnp.maximum(i * per - 1, 0)), (lambda i: jnp.minimum((i + 1) * per, total // halo_n - 1))

    n_seq = (ATTN_SUB * tq) // cur_rows

    def spec16(nrows, row_idx, col, width=ATTN_WIDTH):
        return pl.BlockSpec((None, n_seq, nrows, width), lambda b, r, i: (b, r, row_idx(i), col))
    prv, nxt = halo_idx(cur_rows, half, m16)
    stat16 = jax.ShapeDtypeStruct((bsz, STREAMS, m16, LANES), F32)
    o2, mx2, den2 = pl.pallas_call(
        functools.partial(_attn16_kernel, m=m16),
        out_shape=(jax.ShapeDtypeStruct((bsz, STREAMS, m16, ATTN_WIDTH), BF16), stat16, stat16),
        grid=(bsz, STREAMS // n_seq, m16 // cur_rows),
        in_specs=[spec16(cur_rows, same, 5), spec16(half, prv, 1, 2 * ATTN_WIDTH),
                  spec16(cur_rows, same, 1, 2 * ATTN_WIDTH), spec16(half, nxt, 1, 2 * ATTN_WIDTH)],
        out_specs=(spec16(cur_rows, same, 0), spec16(cur_rows, same, 0, LANES), spec16(cur_rows, same, 0, LANES)),
        compiler_params=params(3), name="attn_dil16",
    )(*([qkv12] * 4))

    ns = STREAMS // 4
    rows4, halo4 = cur_rows // ns, half // ns

    def view4(a):
        return a.reshape(bsz, ns, 4, m16, a.shape[-1])

    def spec4(nrows, row_idx, col, width=ATTN_WIDTH):
        return pl.BlockSpec((None, ns, n_seq, nrows, width), lambda b, r, i: (b, 0, r, row_idx(i), col))
    prv4, nxt4 = halo_idx(rows4, halo4, m16)
    q4 = view4(qkv12)
    stat4 = jax.ShapeDtypeStruct((bsz, ns, 4, m16, LANES), F32)
    o1, mx1, den1 = pl.pallas_call(
        functools.partial(_attn4_kernel, m=m16),
        out_shape=(jax.ShapeDtypeStruct((bsz, ns, 4, m16, ATTN_WIDTH), BF16), stat4, stat4),
        grid=(bsz, 4 // n_seq, m16 // rows4),
        in_specs=[spec4(rows4, same, 4), spec4(halo4, prv4, 0, 2 * ATTN_WIDTH),
                  spec4(rows4, same, 0, 2 * ATTN_WIDTH), spec4(halo4, nxt4, 0, 2 * ATTN_WIDTH)],
        out_specs=(spec4(rows4, same, 0), spec4(rows4, same, 0, LANES), spec4(rows4, same, 0, LANES)),
        compiler_params=params(3), name="attn_dil4",
    )(q4, q4, q4, q4)
    streams = lambda a: a.reshape(bsz, STREAMS, m16, a.shape[-1])

    cur1 = ATTN_SUB * tq

    def spec1(nrows, row_idx, col, width=ATTN_WIDTH):
        return pl.BlockSpec((None, nrows, width), lambda b, i: (b, row_idx(i), col))
    prv1, nxt1 = halo_idx(cur1, half, s)
    stat1 = jax.ShapeDtypeStruct((bsz, s, LANES), F32)
    group0 = pl.pallas_call(
        functools.partial(_attn1_kernel, m=s),
        out_shape=(jax.ShapeDtypeStruct((bsz, s, ATTN_WIDTH), BF16), stat1, stat1),
        grid=(bsz, s // cur1),
        in_specs=[spec1(cur1, same, 2), spec1(half, prv1, 0, 2 * ATTN_WIDTH),
                  spec1(cur1, same, 0, 2 * ATTN_WIDTH), spec1(half, nxt1, 0, 2 * ATTN_WIDTH)],
        out_specs=(spec1(cur1, same, 0), spec1(cur1, same, 0, LANES), spec1(cur1, same, 0, LANES)),
        compiler_params=params(2), name="attn_dil1",
    )(*([qkv0] * 4))
    return group0, (o2, mx2, den2), (streams(o1), streams(mx1), streams(den1))


def _attn_out_kernel(o0_ref, mx0_ref, den0_ref, o2_ref, mx2_ref, den2_ref, o1_ref, mx1_ref, den1_ref,
                     unperm_ref, z_ref, x_ref, gate_ref, w_ref, out_ref, y_scr, a2_scr, a1_scr, stat_scr):
    tm = x_ref.shape[0]
    tq = ATTN_TQ
    srow, n_srow = tq // STREAMS, mx2_ref.shape[1]
    n_un = unperm_ref.shape[0]
    for k, src in enumerate((mx2_ref, den2_ref, mx1_ref, den1_ref)):
        for r in range(STREAMS):
            stat_scr[k, r * STAT_PITCH:r * STAT_PITCH + n_srow, :] = src[r]
    for u in range(tm // n_un):
        piece = slice(u * n_un, (u + 1) * n_un)
        for src, dst in ((o2_ref, a2_scr), (o1_ref, a1_scr)):
            rs = slice(u * n_un // STREAMS, (u + 1) * n_un // STREAMS)
            part = jnp.concatenate([src[r, rs, :] for r in range(STREAMS)], axis=0)
            dst[piece, :] = jnp.dot(unperm_ref[...], part, preferred_element_type=F32)
        for t in range(u * n_un // tq, (u + 1) * n_un // tq):
            rows = slice(t * tq, (t + 1) * tq)

            def natural(k):
                return jnp.concatenate([stat_scr[k, pl.ds(t * srow + ii, STREAMS, stride=STAT_PITCH), :]
                                        for ii in range(srow)], axis=0)

            def put_y(sl, val):
                y_scr[rows, sl] = (val * _silu(z_ref[rows, sl].astype(F32))).astype(BF16)

            stats = [(mx0_ref[rows], den0_ref[rows]), (natural(0), natural(1)), (natural(2), natural(3))]
            _merge_groups(stats, [lambda sl: o0_ref[rows, sl].astype(F32), lambda sl: a2_scr[rows, sl],
                                  lambda sl: a1_scr[rows, sl]], put_y)
        out = jnp.dot(y_scr[piece, :], w_ref[...], preferred_element_type=F32)
        out_ref[piece, :] = x_ref[piece, :] + gate_ref[...] * out


def _attn_out(groups, nat, x, mod, w, tm=PERM_ROWS):
    bsz, s, _ = x.shape
    row = lambda width: pl.BlockSpec((None, tm, width), lambda b, i: (b, i, 0))
    sm = lambda width: pl.BlockSpec((None, STREAMS, tm // STREAMS, width), lambda b, i: (b, 0, i, 0))
    z_col = nat.shape[-1] // ATTN_WIDTH - 1
    unperm = jnp.asarray(_stream_perm(UNPERM_ROWS).T, BF16)
    return pl.pallas_call(
        _attn_out_kernel,
        out_shape=jax.ShapeDtypeStruct(x.shape, F32),
        grid=(bsz, s // tm),
        in_specs=[row(ATTN_WIDTH), row(LANES), row(LANES)] + [sm(ATTN_WIDTH), sm(LANES), sm(LANES)] * 2
                 + [pl.BlockSpec((UNPERM_ROWS, UNPERM_ROWS), lambda b, i: (0, 0)),
                    pl.BlockSpec((None, tm, ATTN_WIDTH), lambda b, i: (b, i, z_col)), row(D_MODEL),
                    pl.BlockSpec((None, 1, D_MODEL), lambda b, i: (b, 0, 2)),
                    pl.BlockSpec((ATTN_WIDTH, D_MODEL), lambda b, i: (0, 0))],
        out_specs=row(D_MODEL),
        scratch_shapes=[pltpu.VMEM((tm, ATTN_WIDTH), BF16), pltpu.VMEM((tm, ATTN_WIDTH), F32),
                        pltpu.VMEM((tm, ATTN_WIDTH), F32), pltpu.VMEM((4, STREAMS * STAT_PITCH, LANES), F32)],
        compiler_params=pltpu.CompilerParams(
            dimension_semantics=("parallel", "parallel"),
            vmem_limit_bytes=VMEM_LIMIT),
        name="attn_out",
    )(*groups[0], *groups[1], *groups[2], unperm, nat, x, mod, w)


def _cmul(ar, ai, br, bi):
    return ar * br - ai * bi, ar * bi + ai * br


def _s5_prep_kernel(*refs):
    ins, d_diag, outs, v_scr = refs[:9], refs[9], refs[10:14], refs[14]
    for gi in range(d_diag.shape[0]):
        _s5_prep_group(*[r.at[:, gi] for r in ins], d_diag.at[gi], *[o.at[gi] for o in outs], v_scr.at[gi])


def _s5_prep_group(lam_re_a, lam_im_a, lam_re_b, lam_im_b, logdt, bt_re, bt_im,
                   c_re, c_im, d_diag, toep_ref, mb_ref, mc_ref, al_ref, v_scr):
    lane = lax.broadcasted_iota(jnp.int32, (STATE, LANES), 1)
    a_idx = lane // GROUP_CH
    eye = (lax.broadcasted_iota(jnp.int32, (STATE, STATE), 0)
           == lax.broadcasted_iota(jnp.int32, (STATE, STATE), 1))
    n_tiles = CHUNK // T_PER_VREG

    mb_rows = []
    v_rhs = []
    v_lhs = []
    mc_cols = []
    zbar_f = None
    for dr in range(2):
        dt = jnp.exp(logdt[dr])
        lrb = jnp.minimum(lam_re_b[dr], LAMBDA_RE_MAX)
        mag = jnp.exp(lrb * dt)
        pw_r, pw_i = mag * jnp.cos(lam_im_b[dr] * dt), mag * jnp.sin(lam_im_b[dr] * dt)
        diag = lambda v: jnp.sum(jnp.where(eye, jnp.broadcast_to(v, (STATE, STATE)), 0.0), axis=1, keepdims=True)
        lr = jnp.minimum(lam_re_a[dr], LAMBDA_RE_MAX)
        li = lam_im_a[dr]
        l1r, l1i = diag(pw_r), diag(pw_i)
        small = [(jnp.ones_like(l1r), jnp.zeros_like(l1r)), (l1r, l1i)]
        for _ in range(T_PER_VREG - 1):
            small.append(_cmul(*small[-1], l1r, l1i))

        def lane_powers(exponent_of_step):
            pr, pi = small[exponent_of_step[0]]
            pr, pi = jnp.broadcast_to(pr, (STATE, LANES)), jnp.broadcast_to(pi, (STATE, LANES))
            for a in range(1, T_PER_VREG):
                er, ei = small[exponent_of_step[a]]
                pr, pi = jnp.where(a_idx == a, er, pr), jnp.where(a_idx == a, ei, pi)
            return pr, pi

        den = lr * lr + li * li
        nr, ni = l1r - 1.0, l1i
        cr, ci = (nr * lr + ni * li) / den, (ni * lr - nr * li) / den
        bbr, bbi = _cmul(cr, ci, bt_re[dr], bt_im[dr])
        l8r, l8i = small[T_PER_VREG]
        tiles = [None] * n_tiles
        if dr == 0:
            pr, pi = lane_powers([T_PER_VREG - 1 - a for a in range(T_PER_VREG)])
            cur = _cmul(pr, pi, bbr, bbi)
            for jt in range(n_tiles - 1, -1, -1):
                tiles[jt] = cur
                if jt:
                    cur = _cmul(cur[0], cur[1], l8r, l8i)
            v_rhs.append(_cmul(tiles[n_tiles - 1][0], tiles[n_tiles - 1][1], l1r, l1i))
            zbar_f = (jnp.where(lane < GROUP_CH, bbr, 0.0), jnp.where(lane < GROUP_CH, bbi, 0.0))
        else:
            pr, pi = lane_powers(list(range(T_PER_VREG)))
            cur = _cmul(pr, pi, bbr, bbi)
            for jt in range(n_tiles):
                tiles[jt] = cur
                if jt < n_tiles - 1:
                    cur = _cmul(cur[0], cur[1], l8r, l8i)
            v_rhs.append(tiles[0])
        mb_rows.append(jnp.concatenate([t[0] for t in tiles], axis=1))
        mb_rows.append(jnp.concatenate([t[1] for t in tiles], axis=1))
        a_r, a_i = l8r, l8i
        for _ in range(int(math.log2(CHUNK // T_PER_VREG))):
            a_r, a_i = _cmul(a_r, a_i, a_r, a_i)
        al_ref[2 * dr] = jnp.broadcast_to(a_r, (STATE, LANES))
        al_ref[2 * dr + 1] = jnp.broadcast_to(a_i, (STATE, LANES))

        pows = [(pw_r, pw_i)]
        for _ in range(int(math.log2(CHUNK)) - 1):
            pows.append(_cmul(*pows[-1], *pows[-1]))
        cre, cim = c_re[dr], c_im[dr]
        xr, xi = _cmul(cre, cim, *pows[0])
        for k in range(int(math.log2(CHUNK))):
            yr, yi = _cmul(xr, xi, *pows[k])
            if dr == 0:
                xr, xi = jnp.concatenate([xr, yr], 0), jnp.concatenate([xi, yi], 0)
            else:
                xr, xi = jnp.concatenate([yr, xr], 0), jnp.concatenate([yi, xi], 0)
        mc_cols += [xr, -xi]
        xr, xi = cre, cim
        for k in range(int(math.log2(n_tiles))):
            yr, yi = _cmul(xr, xi, *pows[k + int(math.log2(T_PER_VREG))])
            if dr == 0:
                xr, xi = jnp.concatenate([yr, xr], 0), jnp.concatenate([yi, xi], 0)
            else:
                xr, xi = jnp.concatenate([xr, yr], 0), jnp.concatenate([xi, yi], 0)
        v_lhs.append((xr, xi))

    mb_ref[...] = jnp.concatenate(mb_rows, axis=0).astype(BF16)
    mc_ref[...] = jnp.concatenate(mc_cols, axis=1).astype(BF16)

    def cdot(lhs, rhs):
        return (jnp.dot(lhs[0], rhs[0], precision=HI, preferred_element_type=F32)
                - jnp.dot(lhs[1], rhs[1], precision=HI, preferred_element_type=F32))

    out_f = cdot(v_lhs[0], v_rhs[0])
    zero_rows = jnp.zeros((CHUNK_ROWS // T_PER_VREG - GROUP_CH, STATE), F32)
    lag0_lhs = (jnp.concatenate([c_re[0], zero_rows], 0), jnp.concatenate([c_im[0], zero_rows], 0))
    out_b = cdot(v_lhs[1], v_rhs[1]) + cdot(lag0_lhs, zbar_f)
    out_b = out_b + jnp.concatenate([d_diag[...], jnp.zeros((CHUNK_ROWS // T_PER_VREG - GROUP_CH, LANES), F32)], 0)
    for jt in range(n_tiles):
        rows = slice(jt * GROUP_CH, (jt + 1) * GROUP_CH)
        v_scr[:, jt * LANES:(jt + 1) * LANES] = out_f[rows]
        v_scr[:, (n_tiles + jt) * LANES:(n_tiles + jt + 1) * LANES] = out_b[rows]
    v_scr[:, 2 * CHUNK_ROWS:] = jnp.zeros((GROUP_CH, LANES), F32)

    for k in range(T_PER_VREG):
        vk = v_scr[:, k * GROUP_CH:k * GROUP_CH + 2 * CHUNK_ROWS]
        for t_out in range(CHUNK):
            off = (CHUNK - t_out) * GROUP_CH
            if off % LANES == k * GROUP_CH:
                al_off = off - k * GROUP_CH
                toep_ref[t_out * GROUP_CH:(t_out + 1) * GROUP_CH, :] = (
                    vk[:, al_off:al_off + CHUNK_ROWS].astype(BF16))


def _s5_prep(lam_re, lam_im, log_dt, b_re, b_im, c_re, c_im, d_skip):
    g = SSM_GROUPS
    d_diag = d_skip.reshape(g, GROUP_CH, 1) * jnp.eye(GROUP_CH, LANES, dtype=F32)
    lam_a = lambda a: a.reshape(2, g, STATE, 1)
    lam_b = lambda a: a.reshape(2, g, 1, STATE)
    tile_b = lambda a: jnp.tile(a, (1, 1, 1, T_PER_VREG))
    gs = S5_GROUPS_PER_STEP
    dir_spec = lambda r, c: pl.BlockSpec((2, gs, r, c), lambda i: (0, i, 0, 0))
    return pl.pallas_call(
        _s5_prep_kernel,
        out_shape=(jax.ShapeDtypeStruct((g, CHUNK_ROWS, CHUNK_ROWS), BF16),
                   jax.ShapeDtypeStruct((g, 4 * STATE, CHUNK_ROWS), BF16),
                   jax.ShapeDtypeStruct((g, CHUNK_ROWS, 4 * STATE), BF16),
                   jax.ShapeDtypeStruct((g, 4, STATE, LANES), F32)),
        grid=(g // gs,),
        in_specs=[dir_spec(STATE, 1), dir_spec(STATE, 1), dir_spec(1, STATE), dir_spec(1, STATE),
                  dir_spec(1, 1), dir_spec(STATE, LANES), dir_spec(STATE, LANES),
                  dir_spec(GROUP_CH, STATE), dir_spec(GROUP_CH, STATE),
                  pl.BlockSpec((gs, GROUP_CH, LANES), lambda i: (i, 0, 0))],
        out_specs=(pl.BlockSpec((gs, CHUNK_ROWS, CHUNK_ROWS), lambda i: (i, 0, 0)),
                   pl.BlockSpec((gs, 4 * STATE, CHUNK_ROWS), lambda i: (i, 0, 0)),
                   pl.BlockSpec((gs, CHUNK_ROWS, 4 * STATE), lambda i: (i, 0, 0)),
                   pl.BlockSpec((gs, 4, STATE, LANES), lambda i: (i, 0, 0, 0))),
        scratch_shapes=[pltpu.VMEM((gs, GROUP_CH, V_LANES), F32)],
        compiler_params=pltpu.CompilerParams(dimension_semantics=("parallel",)),
        name="s5_prep",
    )(lam_a(lam_re), lam_a(lam_im), lam_b(lam_re), lam_b(lam_im), log_dt.reshape(2, g, 1, 1),
      tile_b(b_re), tile_b(b_im), c_re, c_im, d_diag)


def _slab_perm():
    dst = np.arange(SLAB)
    perm = np.zeros((SLAB, SLAB), np.float32)
    perm[dst, (dst % SLAB_CHUNKS) * SUBLANES + dst // SLAB_CHUNKS] = 1.0
    return perm


def _s5_in_kernel(x_ref, mod_ref, g_ref, w_ref, p_ref, u_ref, z_ref, h_scr, *, chunks_per_seq):
    n_chunks = x_ref.shape[0]
    for b in range(n_chunks // chunks_per_seq):
        rows = slice(b * chunks_per_seq, (b + 1) * chunks_per_seq)
        h = _rms_modulate(x_ref[rows], g_ref[...], mod_ref[b:b + 1, D_MODEL:2 * D_MODEL],
                          mod_ref[b:b + 1, :D_MODEL])
        h2 = h.reshape(chunks_per_seq * SUBLANES, D_MODEL).astype(BF16)
        for s in range(chunks_per_seq // SLAB_CHUNKS):
            hp = jnp.dot(p_ref[...], h2[s * SLAB:(s + 1) * SLAB], preferred_element_type=F32)
            c0 = b * chunks_per_seq + s * SLAB_CHUNKS
            for k in range(SUBLANES):
                h_scr[k, c0:c0 + SLAB_CHUNKS, :] = hp[k * SLAB_CHUNKS:(k + 1) * SLAB_CHUNKS].astype(BF16)
    for k in range(SUBLANES):
        ht = h_scr[k].astype(F32).T.astype(BF16)
        uz = jnp.dot(w_ref[...], ht, preferred_element_type=F32)
        u_ref[k] = uz[:SSM_WIDTH].astype(BF16)
        z_ref[k] = uz[SSM_WIDTH:].astype(BF16)


def _s5_in(x3, mod, g, w_t, chunks_per_seq):
    n_chunks = x3.shape[0]
    once = lambda shape: pl.BlockSpec(shape, lambda t: (0,) * len(shape), pipeline_mode=pl.Buffered(1))
    out = pl.BlockSpec((SUBLANES, SSM_WIDTH, n_chunks), lambda t: (t, 0, 0))
    return pl.pallas_call(
        functools.partial(_s5_in_kernel, chunks_per_seq=chunks_per_seq),
        out_shape=(jax.ShapeDtypeStruct((CHUNK, SSM_WIDTH, n_chunks), BF16),) * 2,
        grid=(CHUNK // SUBLANES,),
        in_specs=[pl.BlockSpec((n_chunks, SUBLANES, D_MODEL), lambda t: (0, t, 0)),
                  once(mod.shape), once((1, D_MODEL)), once((2 * SSM_WIDTH, D_MODEL)),
                  once((SLAB, SLAB))],
        out_specs=(out, out),
        scratch_shapes=[pltpu.VMEM((SUBLANES, n_chunks, D_MODEL), BF16)],
        compiler_params=pltpu.CompilerParams(
            dimension_semantics=("parallel",), vmem_limit_bytes=VMEM_LIMIT_BIG),
        name="s5_in",
    )(x3, mod, g, w_t, jnp.asarray(_slab_perm(), BF16))


def _gelu_tanh(v):
    return 0.5 * v * (1.0 + jnp.tanh(math.sqrt(2.0 / math.pi) * (v + 0.044715 * (v * v * v))))


def _s5_core_kernel(u_ref, toep_ref, mb_ref, mc_ref, al_ref, o_ref, *, chunks_per_seq):
    for gi in range(u_ref.shape[1]):
        _s5_core_group(u_ref.at[:, gi], toep_ref.at[gi], mb_ref.at[gi], mc_ref.at[gi], al_ref.at[gi],
                       o_ref.at[:, gi], chunks_per_seq)


def _s5_core_group(u_ref, toep_ref, mb_ref, mc_ref, al_ref, o_ref, chunks_per_seq):
    n_chunks = u_ref.shape[-1]
    ub = u_ref[...].reshape(CHUNK_ROWS, n_chunks)
    inc = jnp.dot(mb_ref[...], ub, preferred_element_type=F32)
    pos = lax.broadcasted_iota(jnp.int32, (STATE, n_chunks), 1) % chunks_per_seq
    reps = n_chunks // LANES

    def lane_tile(a):
        return jnp.concatenate([a] * reps, axis=1) if reps > 1 else a

    states = []
    for dr in range(2):
        xr = inc[(2 * dr) * STATE:(2 * dr + 1) * STATE]
        xi = inc[(2 * dr + 1) * STATE:(2 * dr + 2) * STATE]
        ar, ai = al_ref[2 * dr], al_ref[2 * dr + 1]

        def shifted(v, step):
            if dr == 0:
                return jnp.where(pos >= step, pltpu.roll(v, step, 1), 0.0)
            return jnp.where(pos < chunks_per_seq - step, pltpu.roll(v, n_chunks - step, 1), 0.0)

        step = 1
        while step < chunks_per_seq:
            sr, si = shifted(xr, step), shifted(xi, step)
            tr, ti = lane_tile(ar), lane_tile(ai)
            xr, xi = xr + tr * sr - ti * si, xi + tr * si + ti * sr
            ar, ai = _cmul(ar, ai, ar, ai)
            step *= 2
        states += [shifted(xr, 1), shifted(xi, 1)]
    h_in = jnp.concatenate(states, axis=0).astype(BF16)
    y = (jnp.dot(toep_ref[...], ub, preferred_element_type=F32)
         + jnp.dot(mc_ref[...], h_in, preferred_element_type=F32))
    o_ref[...] = _gelu_tanh(y).astype(BF16).reshape(CHUNK, GROUP_CH, n_chunks)


def _s5_core(u_cl, toep, mb, mc, al, chunks_per_seq):
    n_chunks = u_cl.shape[-1]
    u4 = u_cl.reshape(CHUNK, SSM_GROUPS, GROUP_CH, n_chunks)
    gs = S5_GROUPS_PER_STEP
    grp = pl.BlockSpec((CHUNK, gs, GROUP_CH, n_chunks), lambda g: (0, g, 0, 0))
    out = pl.pallas_call(
        functools.partial(_s5_core_kernel, chunks_per_seq=chunks_per_seq),
        out_shape=jax.ShapeDtypeStruct(u4.shape, BF16),
        grid=(SSM_GROUPS // gs,),
        in_specs=[grp,
                  pl.BlockSpec((gs, CHUNK_ROWS, CHUNK_ROWS), lambda g: (g, 0, 0)),
                  pl.BlockSpec((gs, 4 * STATE, CHUNK_ROWS), lambda g: (g, 0, 0)),
                  pl.BlockSpec((gs, CHUNK_ROWS, 4 * STATE), lambda g: (g, 0, 0)),
                  pl.BlockSpec((gs, 4, STATE, LANES), lambda g: (g, 0, 0, 0))],
        out_specs=grp,
        compiler_params=pltpu.CompilerParams(
            dimension_semantics=("parallel",), vmem_limit_bytes=VMEM_LIMIT),
        name="s5_core",
    )(u4, toep, mb, mc, al)
    return out.reshape(CHUNK, SSM_WIDTH, n_chunks)


def _s5_out_kernel(g_ref, z_ref, wglu_ref, wout_ref, pt_ref, x_ref, mod_ref, fg_ref, o_ref, y_scr,
                   *, final, chunks_per_seq):
    q = pl.program_id(1)
    n_half = x_ref.shape[0]
    k_per = g_ref.shape[0]
    n_glu = SUBLANES // k_per

    for qv in range(n_glu):
        @pl.when(q == qv)
        def _():
            for kk in range(k_per):
                gb = g_ref[kk]
                glu = jnp.dot(wglu_ref[...], gb, preferred_element_type=F32)
                y = gb.astype(F32) * _sigmoid(glu) * _silu(z_ref[kk].astype(F32))
                y_scr[qv * k_per + kk] = y.T.astype(BF16)

    for hv in range(y_scr.shape[1] // n_half):
        @pl.when(q == n_glu + hv)
        def _():
            for s in range(n_half // SLAB_CHUNKS):
                c0 = hv * n_half + s * SLAB_CHUNKS
                src = jnp.concatenate([y_scr[k, c0:c0 + SLAB_CHUNKS, :] for k in range(SUBLANES)], axis=0)
                yp = jnp.dot(pt_ref[...], src, preferred_element_type=F32).astype(BF16)
                out = jnp.dot(yp, wout_ref[...], preferred_element_type=F32)
                b = c0 // chunks_per_seq
                upd = mod_ref[b:b + 1, 2 * D_MODEL:] * out
                rows = slice(s * SLAB_CHUNKS, (s + 1) * SLAB_CHUNKS)
                xn = x_ref[rows] + upd.reshape(SLAB_CHUNKS, SUBLANES, D_MODEL)
                if final:
                    xn = xn * lax.rsqrt(jnp.mean(xn * xn, axis=-1, keepdims=True) + NORM_EPS) * fg_ref[...]
                o_ref[rows] = xn


def _s5_out(g_cl, z_cl, wglu_t, w_out, x3, mod, final_g, final, chunks_per_seq, n_split=2):
    n_chunks = x3.shape[0]
    n_half = n_chunks // n_split
    n_glu = SUBLANES // S5_OUT_K
    once = lambda shape: pl.BlockSpec(shape, lambda t, q: (0,) * len(shape), pipeline_mode=pl.Buffered(1))
    act = pl.BlockSpec((S5_OUT_K, SSM_WIDTH, n_chunks), lambda t, q: (t * n_glu + jnp.minimum(q, n_glu - 1), 0, 0))
    row = pl.BlockSpec((n_half, SUBLANES, D_MODEL), lambda t, q: (jnp.maximum(q - n_glu, 0), t, 0))
    return pl.pallas_call(
        functools.partial(_s5_out_kernel, final=final, chunks_per_seq=chunks_per_seq),
        out_shape=jax.ShapeDtypeStruct(x3.shape, F32),
        grid=(CHUNK // SUBLANES, n_glu + n_split),
        in_specs=[act, act, once((SSM_WIDTH, SSM_WIDTH)), once((SSM_WIDTH, D_MODEL)), once((SLAB, SLAB)),
                  row, once(mod.shape), once((1, D_MODEL))],
        out_specs=row,
        scratch_shapes=[pltpu.VMEM((SUBLANES, n_chunks, SSM_WIDTH), BF16)],
        compiler_params=pltpu.CompilerParams(
            dimension_semantics=("parallel", "arbitrary"), vmem_limit_bytes=VMEM_LIMIT_BIG),
        name="s5_out",
    )(g_cl, z_cl, wglu_t, w_out, jnp.asarray(_slab_perm().T, BF16), x3, mod, final_g)


def _rope_tables(s):
    inv_freq = ROPE_THETA ** (-jnp.arange(0, HEAD_DIM, 2, dtype=F32) / HEAD_DIM)
    ang = jnp.arange(s, dtype=F32)[:, None] * inv_freq[None, :]
    reps = LANES // (HEAD_DIM // 2)
    sign = jnp.where(jnp.arange(LANES) < LANES // 2, -1.0, 1.0).astype(F32)
    cos, sin = jnp.tile(jnp.cos(ang), (1, reps)), jnp.tile(jnp.sin(ang), (1, reps)) * sign
    stream_major = lambda a: a.reshape(s // STREAMS, STREAMS, LANES).transpose(1, 0, 2)
    return cos, sin, stream_major(cos), stream_major(sin)


def _prep_attn_w_in(w):
    d_in = w.shape[0]
    w = w.reshape(d_in, -1, ATTN_HEADS // 2, 2, 2, HEAD_DIM // 2)
    w = jnp.stack([w[:, i] for i in ATTN_IN_BLOCK_ORDER], axis=1)
    return w.transpose(0, 1, 2, 4, 3, 5).astype(BF16).reshape(d_in, -1)


def _prep_attn_w_out(w):
    w = w.astype(BF16).reshape(ATTN_HEADS // 2, 2, 2, HEAD_DIM // 2, w.shape[1])
    return w.transpose(0, 2, 1, 3, 4).reshape(ATTN_WIDTH, -1)


def _trunk(x, ada, norm_g, attn_w, s5_w, s5_ops, tables, final_norm_g):
    bsz, s, _ = x.shape
    chunks_per_seq = s // CHUNK
    n_chunks = bsz * chunks_per_seq
    fg = final_norm_g.reshape(1, D_MODEL)
    for i in range(DEPTH):
        mod = ada[i].reshape(bsz, 1, 3 * D_MODEL)
        g = norm_g[i].reshape(1, D_MODEL)
        j = i // N_MIXERS
        if i % N_MIXERS == 0:
            w_in, w_out = attn_w[j]
            nat, qkv12 = _attn_in(x, mod, g, w_in, tables)
            x = _attn_out(_attention(nat, qkv12), nat, x, mod, w_out)
        else:
            w_in_t, wglu_t, w_out = s5_w[j]
            toep, mb, mc, al = s5_ops[j]
            x3 = x.reshape(n_chunks, CHUNK, D_MODEL)
            u_cl, z_cl = _s5_in(x3, ada[i], g, w_in_t, chunks_per_seq)
            g_cl = _s5_core(u_cl, toep, mb, mc, al, chunks_per_seq)
            x3 = _s5_out(g_cl, z_cl, wglu_t, w_out, x3, ada[i], fg, i == DEPTH - 1, chunks_per_seq)
            x = x3.reshape(bsz, s, D_MODEL)
    return x


def kernel(x_prompt, x_sample, c_prompt, c_sample, norm_g, ada_w, ada_b, attn_w_in, attn_w_out,
           ssm_w_in, ssm_lam_re, ssm_lam_im, ssm_log_dt, ssm_b_re, ssm_b_im, ssm_c_re, ssm_c_im,
           ssm_d, ssm_w_glu, ssm_w_out, final_norm_g):
    assert (DEPTH - 1) % N_MIXERS == 1, "the final norm is fused into the last S5 layer"
    assert all(w // (2 * d) == RADIUS for w, d in DILATED_PAIRS)
    assert [d for _, d in DILATED_PAIRS] == [1, 4, STREAMS]
    n_prompt = c_prompt.shape[0]
    ada = _ada(jnp.concatenate([c_prompt, c_sample], axis=0), ada_w, ada_b)
    attn_w = [(_prep_attn_w_in(attn_w_in[j]), _prep_attn_w_out(attn_w_out[j]))
              for j in range(attn_w_in.shape[0])]
    s5_w, s5_ops = [], []
    for j in range(ssm_w_in.shape[0]):
        s5_w.append((ssm_w_in[j].T.astype(BF16), ssm_w_glu[j].T.astype(BF16), ssm_w_out[j].astype(BF16)))
        s5_ops.append(_s5_prep(ssm_lam_re[j], ssm_lam_im[j], ssm_log_dt[j], ssm_b_re[j],
                               ssm_b_im[j], ssm_c_re[j], ssm_c_im[j], ssm_d[j]))
    tables = _rope_tables(max(x_prompt.shape[1], x_sample.shape[1]))
    y_prompt = _trunk(x_prompt, ada[:, :n_prompt], norm_g, attn_w, s5_w, s5_ops, tables, final_norm_g)
    y_sample = _trunk(x_sample, ada[:, n_prompt:], norm_g, attn_w, s5_w, s5_ops, tables, final_norm_g)
    return (y_prompt, y_sample)
```

```python
import functools
import math

import numpy as np
import jax
import jax.numpy as jnp
from jax import lax
from jax.experimental import pallas as pl
from jax.experimental.pallas import tpu as pltpu

D_MODEL = 1024
DEPTH = 4
N_MIXERS = 2
ATTN_HEADS = 16
HEAD_DIM = 64
ATTN_WIDTH = ATTN_HEADS * HEAD_DIM
DILATED_PAIRS = ((128, 1), (512, 4), (2048, 16))
N_DIL = len(DILATED_PAIRS)
ROPE_THETA = 10000.0
SSM_WIDTH = D_MODEL
GROUP_CH = 16
SSM_GROUPS = SSM_WIDTH // GROUP_CH
STATE = 64
LAMBDA_RE_MAX = -1e-4
NORM_EPS = 1e-6
NEG_INF = -1e30

LANES = 128
SUBLANES = 8
CHUNK = 64
CHUNK_ROWS = CHUNK * GROUP_CH
T_PER_VREG = LANES // GROUP_CH
V_LANES = 2 * CHUNK_ROWS + LANES
STREAMS = 16
RADIUS = 64
ATTN_TQ = 128
ATTN_SUB = 8
STAT_PITCH = 40
ATTN_IN_BLOCK_ORDER = (1, 2, 0, 9, 4, 5, 7, 8, 3, 6)
PERM_ROWS = 512
UNPERM_ROWS = 256
LOG2_E = math.log2(math.e)
S5_GROUPS_PER_STEP = 2
S5_OUT_K = 4
SLAB = 256
SLAB_CHUNKS = SLAB // SUBLANES
VMEM_LIMIT = 48 * 1024 * 1024
VMEM_LIMIT_BIG = 56 * 1024 * 1024

F32 = jnp.float32
BF16 = jnp.bfloat16
HI = lax.Precision.HIGHEST


def _sigmoid(v):
    return 0.5 * jnp.tanh(0.5 * v) + 0.5


def _silu(v):
    return v * _sigmoid(v)


def _rms_modulate(x, g, scale, shift):
    rs = lax.rsqrt(jnp.mean(x * x, axis=-1, keepdims=True) + NORM_EPS)
    return (x * rs * g) * (1.0 + scale) + shift


def _ada_kernel(c_ref, w_ref, b_ref, o_ref):
    o_ref[...] = jnp.dot(_silu(c_ref[...]), w_ref[...], precision=HI,
                         preferred_element_type=F32) + b_ref[...]


def _ada(c, ada_w, ada_b):
    nb = c.shape[0]
    return pl.pallas_call(
        _ada_kernel,
        out_shape=jax.ShapeDtypeStruct((DEPTH, nb, 3 * D_MODEL), F32),
        grid=(DEPTH, 3),
        in_specs=[
            pl.BlockSpec((nb, D_MODEL), lambda i, j: (0, 0)),
            pl.BlockSpec((None, D_MODEL, D_MODEL), lambda i, j: (i, 0, j)),
            pl.BlockSpec((None, 1, D_MODEL), lambda i, j: (i, 0, j)),
        ],
        out_specs=pl.BlockSpec((None, nb, D_MODEL), lambda i, j: (i, 0, j)),
        name="ada",
    )(c, ada_w, ada_b.reshape(DEPTH, 1, 3 * D_MODEL))


def _attn_in_kernel(x_ref, shift_ref, scale_ref, g_ref, w_ref, cos_ref, sin_ref, cosp_ref, sinp_ref,
                    perm_ref, nat_ref, sm_ref, h_nat, h_perm):
    j = pl.program_id(2)
    tm = x_ref.shape[0]
    n_sub = tm // PERM_ROWS
    sub_rows = PERM_ROWS // STREAMS

    @pl.when(j == 0)
    def _():
        h = _rms_modulate(x_ref[...], g_ref[...], scale_ref[...], shift_ref[...])
        h_nat[...] = h.astype(BF16)
        for u in range(n_sub):
            rs = slice(u * PERM_ROWS, (u + 1) * PERM_ROWS)
            h_perm[rs, :] = jnp.dot(perm_ref[...], h_nat[rs, :], preferred_element_type=F32).astype(BF16)

    def rope_blocks(acc, cos, sin, second_is_plain):
        for half in range(2):
            is_q = jnp.logical_or(j == 4, jnp.logical_and(j == 1, half == 0))
            qscale = jnp.where(is_q, HEAD_DIM ** -0.5 * LOG2_E, 1.0).astype(F32)
            c, sn = cos * qscale, sin * qscale
            for b in range(ATTN_WIDTH // LANES):
                sl = slice(half * ATTN_WIDTH + b * LANES, half * ATTN_WIDTH + (b + 1) * LANES)
                t = acc[:, sl]
                if half == 1 and second_is_plain:
                    yield sl, t.astype(BF16)
                else:
                    yield sl, (t * c + pltpu.roll(t, LANES // 2, 1) * sn).astype(BF16)

    @pl.when(j < 2)
    def _():
        acc = jnp.dot(h_nat[...], w_ref[...], preferred_element_type=F32)
        for sl, blk in rope_blocks(acc, cos_ref[...], sin_ref[...], True):
            nat_ref[:, sl] = blk

    def stream_major_step(second_is_plain):
        acc = jnp.dot(h_perm[...], w_ref[...], preferred_element_type=F32)

        def table(ref):
            return jnp.concatenate([ref[:, u * sub_rows:(u + 1) * sub_rows, :].reshape(PERM_ROWS, LANES)
                                    for u in range(n_sub)], axis=0)

        for sl, blk in rope_blocks(acc, table(cosp_ref), table(sinp_ref), second_is_plain):
            for u in range(n_sub):
                sm_ref[:, u * sub_rows:(u + 1) * sub_rows, sl] = (
                    blk[u * PERM_ROWS:(u + 1) * PERM_ROWS].reshape(STREAMS, sub_rows, LANES))

    pl.when(jnp.logical_and(j >= 2, j < 4))(lambda: stream_major_step(True))
    pl.when(j == 4)(lambda: stream_major_step(False))


def _stream_perm(n):
    dst = np.arange(n)
    rows = n // STREAMS
    perm = np.zeros((n, n), np.float32)
    perm[dst, STREAMS * (dst % rows) + dst // rows] = 1.0
    return perm


def _attn_in(x, mod, g, w, tables, tm=1024):
    bsz, s, _ = x.shape
    cos, sin, cosp, sinp = tables
    rows = tm // STREAMS
    wide = 2 * ATTN_WIDTH
    perm = jnp.asarray(_stream_perm(PERM_ROWS), BF16)
    return pl.pallas_call(
        _attn_in_kernel,
        out_shape=(jax.ShapeDtypeStruct((bsz, s, 2 * wide), BF16),
                   jax.ShapeDtypeStruct((bsz, STREAMS, s // STREAMS, 3 * wide), BF16)),
        grid=(bsz, s // tm, 5),
        in_specs=[
            pl.BlockSpec((None, tm, D_MODEL), lambda b, i, j: (b, i, 0)),
            pl.BlockSpec((None, 1, D_MODEL), lambda b, i, j: (b, 0, 0)),
            pl.BlockSpec((None, 1, D_MODEL), lambda b, i, j: (b, 0, 1)),
            pl.BlockSpec((1, D_MODEL), lambda b, i, j: (0, 0)),
            pl.BlockSpec((D_MODEL, wide), lambda b, i, j: (0, j)),
            pl.BlockSpec((tm, LANES), lambda b, i, j: (i, 0)),
            pl.BlockSpec((tm, LANES), lambda b, i, j: (i, 0)),
            pl.BlockSpec((STREAMS, rows, LANES), lambda b, i, j: (0, i, 0)),
            pl.BlockSpec((STREAMS, rows, LANES), lambda b, i, j: (0, i, 0)),
            pl.BlockSpec((PERM_ROWS, PERM_ROWS), lambda b, i, j: (0, 0)),
        ],
        out_specs=(
            pl.BlockSpec((None, tm, wide), lambda b, i, j: (b, i, jnp.clip(j, 0, 1))),
            pl.BlockSpec((None, STREAMS, rows, wide), lambda b, i, j: (b, 0, i, jnp.clip(j - 2, 0, 2))),
        ),
        scratch_shapes=[pltpu.VMEM((tm, D_MODEL), BF16), pltpu.VMEM((tm, D_MODEL), BF16)],
        compiler_params=pltpu.CompilerParams(
            dimension_semantics=("parallel", "parallel", "arbitrary"),
            vmem_limit_bytes=VMEM_LIMIT_BIG),
        name="attn_in",
    )(x, mod, mod, g, w, cos, sin, cosp, sinp, perm)


def _lane_of_odd_head(lane):
    return (lane // (HEAD_DIM // 2)) % 2 == 1


def _mask_bias(valid):
    return jnp.where(valid, 0.0, NEG_INF).astype(F32)


def _attn_pairs(get_q, get_k, get_v, bias, put_acc):
    tq = bias.shape[0]
    lane = lax.broadcasted_iota(jnp.int32, (tq, LANES), 1)
    q_is_odd = _lane_of_odd_head(lane)
    mx_tile = jnp.zeros((tq, LANES), F32)
    den_tile = jnp.ones((tq, LANES), F32)
    ones = jnp.ones((bias.shape[1], LANES), BF16)
    bias2 = jnp.concatenate([bias, bias], axis=0)
    for b in range(ATTN_WIDTH // LANES):
        sl = slice(b * LANES, (b + 1) * LANES)
        qb = get_q(sl).astype(F32)
        kb = get_k(sl)
        vb = jnp.concatenate([get_v(sl), ones], axis=1)
        qm = jnp.concatenate([jnp.where(q_is_odd, 0.0, qb), jnp.where(q_is_odd, qb, 0.0)], axis=0).astype(BF16)
        sc = lax.dot_general(qm, kb, (((1,), (1,)), ((), ())), preferred_element_type=F32) + bias2
        mx = jnp.max(sc, axis=1, keepdims=True)
        p = jnp.exp2((sc - mx).astype(BF16))
        acc = jnp.dot(p, vb, preferred_element_type=F32)
        for odd in (0, 1):
            rows = slice(odd * tq, (odd + 1) * tq)
            mx_tile = jnp.where(lane == 2 * b + odd, mx[rows], mx_tile)
            den_tile = jnp.where(lane == 2 * b + odd, acc[rows, LANES:], den_tile)
        put_acc(sl, jnp.where(q_is_odd, acc[tq:, :LANES], acc[:tq, :LANES]))
    return mx_tile, den_tile


def _pair_factor(tile, b):
    lane = lax.broadcasted_iota(jnp.int32, tile.shape, 1)
    head = 2 * b + _lane_of_odd_head(lane).astype(jnp.int32)
    return jnp.take_along_axis(tile, head, axis=1, mode="promise_in_bounds")


def _merge_groups(stats, get_accs, put_o):
    top = functools.reduce(jnp.maximum, [mx for mx, _ in stats])
    ws = [jnp.exp2(mx - top) for mx, _ in stats]
    inv = 1.0 / sum(w * den for w, (_, den) in zip(ws, stats))
    fs = [w * inv for w in ws]
    for b in range(ATTN_WIDTH // LANES):
        sl = slice(b * LANES, (b + 1) * LANES)
        put_o(sl, sum(_pair_factor(f, b) * get_acc(sl) for f, get_acc in zip(fs, get_accs)))


def _window_mask(tile, tq, m):
    nk = tq + 2 * RADIUS
    row = lax.broadcasted_iota(jnp.int32, (tq, nk), 0)
    col = lax.broadcasted_iota(jnp.int32, (tq, nk), 1)
    kabs = tile * tq - RADIUS + col
    return _mask_bias((jnp.abs(col - RADIUS - row) <= RADIUS) & (kabs >= 0) & (kabs < m))


def _stack_rows(read, prev_ref, cur_ref, next_ref, lo, cnt):
    halo, cur_n = prev_ref.shape[-2], cur_ref.shape[-2]
    a, b = lo, lo + cnt
    parts = []
    if a < 0:
        parts.append(read(prev_ref, halo + a, halo))
        a = 0
    parts.append(read(cur_ref, a, min(b, cur_n)))
    if b > cur_n:
        parts.append(read(next_ref, 0, b - cur_n))
    return parts


def _kv_getters(stack):
    get_k = lambda sl: stack(sl)
    get_v = lambda sl: stack(slice(ATTN_WIDTH + sl.start, ATTN_WIDTH + sl.stop))
    return get_k, get_v


def _attn16_kernel(*refs, m):
    for r in range(refs[0].shape[0]):
        _attn16_stream(*[ref.at[r] for ref in refs], m=m)


def _attn16_stream(q_ref, kvp_ref, kvc_ref, kvn_ref, o_ref, mx_ref, den_ref, *, m):
    tq = ATTN_TQ
    n_sub = q_ref.shape[0] // tq
    for t in range(n_sub):
        rows = slice(t * tq, (t + 1) * tq)
        bias = _window_mask(pl.program_id(2) * n_sub + t, tq, m)

        def stack(cols):
            return jnp.concatenate(_stack_rows(lambda ref, a, b: ref[a:b, cols], kvp_ref, kvc_ref, kvn_ref,
                                               t * tq - RADIUS, tq + 2 * RADIUS), axis=0)

        def put_acc(sl, val):
            o_ref[rows, sl] = val.astype(BF16)

        mx_ref[rows], den_ref[rows] = _attn_pairs(lambda sl: q_ref[rows, sl], *_kv_getters(stack), bias, put_acc)


def _attn4_kernel(*refs, m):
    for r in range(refs[0].shape[1]):
        _attn4_residue(*[ref.at[:, r] for ref in refs], m=m)


def _attn4_residue(q_ref, kvp_ref, kvc_ref, kvn_ref, o_ref, mx_ref, den_ref, *, m):
    ns, halo = q_ref.shape[0], kvp_ref.shape[1]
    rows = ATTN_TQ // ns
    span = rows + 2 * halo
    n_sub = q_ref.shape[1] // rows
    n = lax.broadcasted_iota(jnp.int32, (ns * rows, ns * span), 0)
    c = lax.broadcasted_iota(jnp.int32, (ns * rows, ns * span), 1)
    for t in range(n_sub):
        base = (pl.program_id(2) * n_sub + t) * rows
        rs = slice(t * rows, (t + 1) * rows)
        krow = base - halo + c % span
        rel = ns * (krow - (base + n % rows)) + (c // span - n // rows)
        bias = _mask_bias((jnp.abs(rel) <= RADIUS) & (krow >= 0) & (krow < m))

        def stack(cols):
            parts = []
            for s in range(ns):
                parts += _stack_rows(lambda ref, a, b: ref[s, a:b, cols], kvp_ref, kvc_ref, kvn_ref,
                                     t * rows - halo, span)
            return jnp.concatenate(parts, axis=0)

        flat = lambda ref: (lambda sl: ref[:, rs, sl].reshape(ns * rows, LANES))

        def put_acc(sl, val):
            o_ref[:, rs, sl] = val.astype(BF16).reshape(ns, rows, LANES)

        mx, den = _attn_pairs(flat(q_ref), *_kv_getters(stack), bias, put_acc)
        mx_ref[:, rs, :] = mx.reshape(ns, rows, LANES)
        den_ref[:, rs, :] = den.reshape(ns, rows, LANES)


def _attn1_kernel(q_ref, kvp_ref, kvc_ref, kvn_ref, o_ref, mx_ref, den_ref, *, m):
    tq = ATTN_TQ
    n_sub = q_ref.shape[0] // tq
    for t in range(n_sub):
        rows = slice(t * tq, (t + 1) * tq)
        bias = _window_mask(pl.program_id(1) * n_sub + t, tq, m)

        def stack(cols):
            return jnp.concatenate(_stack_rows(lambda ref, a, b: ref[a:b, cols], kvp_ref, kvc_ref, kvn_ref,
                                               t * tq - RADIUS, tq + 2 * RADIUS), axis=0)

        def put_acc(sl, val):
            o_ref[rows, sl] = val.astype(BF16)

        mx_ref[rows], den_ref[rows] = _attn_pairs(lambda sl: q_ref[rows, sl], *_kv_getters(stack), bias, put_acc)


def _attention(qkv0, qkv12):
    bsz, s, _ = qkv0.shape
    m16 = s // STREAMS
    tq, half = ATTN_TQ, RADIUS
    cur_rows = min(ATTN_SUB * tq, m16)
    assert m16 % cur_rows == 0 and cur_rows % tq == 0 and s % (ATTN_SUB * tq) == 0
    params = lambda n: pltpu.CompilerParams(dimension_semantics=("parallel",) * n,
                                            vmem_limit_bytes=VMEM_LIMIT)
    same = lambda i: i

    def halo_idx(cur_n, halo_n, total):
        per = cur_n // halo_n
        return (lambda i: jnp.maximum(i * per - 1, 0)), (lambda i: jnp.minimum((i + 1) * per, total // halo_n - 1))

    n_seq = (ATTN_SUB * tq) // cur_rows

    def spec16(nrows, row_idx, col, width=ATTN_WIDTH):
        return pl.BlockSpec((None, n_seq, nrows, width), lambda b, r, i: (b, r, row_idx(i), col))
    prv, nxt = halo_idx(cur_rows, half, m16)
    stat16 = jax.ShapeDtypeStruct((bsz, STREAMS, m16, LANES), F32)
    o2, mx2, den2 = pl.pallas_call(
        functools.partial(_attn16_kernel, m=m16),
        out_shape=(jax.ShapeDtypeStruct((bsz, STREAMS, m16, ATTN_WIDTH), BF16), stat16, stat16),
        grid=(bsz, STREAMS // n_seq, m16 // cur_rows),
        in_specs=[spec16(cur_rows, same, 5), spec16(half, prv, 1, 2 * ATTN_WIDTH),
                  spec16(cur_rows, same, 1, 2 * ATTN_WIDTH), spec16(half, nxt, 1, 2 * ATTN_WIDTH)],
        out_specs=(spec16(cur_rows, same, 0), spec16(cur_rows, same, 0, LANES), spec16(cur_rows, same, 0, LANES)),
        compiler_params=params(3), name="attn_dil16",
    )(*([qkv12] * 4))

    ns = STREAMS // 4
    rows4, halo4 = cur_rows // ns, half // ns

    def view4(a):
        return a.reshape(bsz, ns, 4, m16, a.shape[-1])

    def spec4(nrows, row_idx, col, width=ATTN_WIDTH):
        return pl.BlockSpec((None, ns, n_seq, nrows, width), lambda b, r, i: (b, 0, r, row_idx(i), col))
    prv4, nxt4 = halo_idx(rows4, halo4, m16)
    q4 = view4(qkv12)
    stat4 = jax.ShapeDtypeStruct((bsz, ns, 4, m16, LANES), F32)
    o1, mx1, den1 = pl.pallas_call(
        functools.partial(_attn4_kernel, m=m16),
        out_shape=(jax.ShapeDtypeStruct((bsz, ns, 4, m16, ATTN_WIDTH), BF16), stat4, stat4),
        grid=(bsz, 4 // n_seq, m16 // rows4),
        in_specs=[spec4(rows4, same, 4), spec4(halo4, prv4, 0, 2 * ATTN_WIDTH),
                  spec4(rows4, same, 0, 2 * ATTN_WIDTH), spec4(halo4, nxt4, 0, 2 * ATTN_WIDTH)],
        out_specs=(spec4(rows4, same, 0), spec4(rows4, same, 0, LANES), spec4(rows4, same, 0, LANES)),
        compiler_params=params(3), name="attn_dil4",
    )(q4, q4, q4, q4)
    streams = lambda a: a.reshape(bsz, STREAMS, m16, a.shape[-1])

    cur1 = ATTN_SUB * tq

    def spec1(nrows, row_idx, col, width=ATTN_WIDTH):
        return pl.BlockSpec((None, nrows, width), lambda b, i: (b, row_idx(i), col))
    prv1, nxt1 = halo_idx(cur1, half, s)
    stat1 = jax.ShapeDtypeStruct((bsz, s, LANES), F32)
    group0 = pl.pallas_call(
        functools.partial(_attn1_kernel, m=s),
        out_shape=(jax.ShapeDtypeStruct((bsz, s, ATTN_WIDTH), BF16), stat1, stat1),
        grid=(bsz, s // cur1),
        in_specs=[spec1(cur1, same, 2), spec1(half, prv1, 0, 2 * ATTN_WIDTH),
                  spec1(cur1, same, 0, 2 * ATTN_WIDTH), spec1(half, nxt1, 0, 2 * ATTN_WIDTH)],
        out_specs=(spec1(cur1, same, 0), spec1(cur1, same, 0, LANES), spec1(cur1, same, 0, LANES)),
        compiler_params=params(2), name="attn_dil1",
    )(*([qkv0] * 4))
    return group0, (o2, mx2, den2), (streams(o1), streams(mx1), streams(den1))


def _attn_out_kernel(o0_ref, mx0_ref, den0_ref, o2_ref, mx2_ref, den2_ref, o1_ref, mx1_ref, den1_ref,
                     unperm_ref, z_ref, x_ref, gate_ref, w_ref, out_ref, y_scr, a2_scr, a1_scr, stat_scr):
    tm = x_ref.shape[0]
    tq = ATTN_TQ
    srow, n_srow = tq // STREAMS, mx2_ref.shape[1]
    n_un = unperm_ref.shape[0]
    for k, src in enumerate((mx2_ref, den2_ref, mx1_ref, den1_ref)):
        for r in range(STREAMS):
            stat_scr[k, r * STAT_PITCH:r * STAT_PITCH + n_srow, :] = src[r]
    for u in range(tm // n_un):
        piece = slice(u * n_un, (u + 1) * n_un)
        for src, dst in ((o2_ref, a2_scr), (o1_ref, a1_scr)):
            rs = slice(u * n_un // STREAMS, (u + 1) * n_un // STREAMS)
            part = jnp.concatenate([src[r, rs, :] for r in range(STREAMS)], axis=0)
            dst[piece, :] = jnp.dot(unperm_ref[...], part, preferred_element_type=F32)
        for t in range(u * n_un // tq, (u + 1) * n_un // tq):
            rows = slice(t * tq, (t + 1) * tq)

            def natural(k):
                return jnp.concatenate([stat_scr[k, pl.ds(t * srow + ii, STREAMS, stride=STAT_PITCH), :]
                                        for ii in range(srow)], axis=0)

            def put_y(sl, val):
                y_scr[rows, sl] = (val * _silu(z_ref[rows, sl].astype(F32))).astype(BF16)

            stats = [(mx0_ref[rows], den0_ref[rows]), (natural(0), natural(1)), (natural(2), natural(3))]
            _merge_groups(stats, [lambda sl: o0_ref[rows, sl].astype(F32), lambda sl: a2_scr[rows, sl],
                                  lambda sl: a1_scr[rows, sl]], put_y)
        out = jnp.dot(y_scr[piece, :], w_ref[...], preferred_element_type=F32)
        out_ref[piece, :] = x_ref[piece, :] + gate_ref[...] * out


def _attn_out(groups, nat, x, mod, w, tm=PERM_ROWS):
    bsz, s, _ = x.shape
    assert s % tm == 0 and tm % UNPERM_ROWS == 0 and tm // STREAMS <= STAT_PITCH
    row = lambda width: pl.BlockSpec((None, tm, width), lambda b, i: (b, i, 0))
    sm = lambda width: pl.BlockSpec((None, STREAMS, tm // STREAMS, width), lambda b, i: (b, 0, i, 0))
    z_col = nat.shape[-1] // ATTN_WIDTH - 1
    unperm = jnp.asarray(_stream_perm(UNPERM_ROWS).T, BF16)
    return pl.pallas_call(
        _attn_out_kernel,
        out_shape=jax.ShapeDtypeStruct(x.shape, F32),
        grid=(bsz, s // tm),
        in_specs=[row(ATTN_WIDTH), row(LANES), row(LANES)] + [sm(ATTN_WIDTH), sm(LANES), sm(LANES)] * 2
                 + [pl.BlockSpec((UNPERM_ROWS, UNPERM_ROWS), lambda b, i: (0, 0)),
                    pl.BlockSpec((None, tm, ATTN_WIDTH), lambda b, i: (b, i, z_col)), row(D_MODEL),
                    pl.BlockSpec((None, 1, D_MODEL), lambda b, i: (b, 0, 2)),
                    pl.BlockSpec((ATTN_WIDTH, D_MODEL), lambda b, i: (0, 0))],
        out_specs=row(D_MODEL),
        scratch_shapes=[pltpu.VMEM((tm, ATTN_WIDTH), BF16), pltpu.VMEM((tm, ATTN_WIDTH), F32),
                        pltpu.VMEM((tm, ATTN_WIDTH), F32), pltpu.VMEM((4, STREAMS * STAT_PITCH, LANES), F32)],
        compiler_params=pltpu.CompilerParams(
            dimension_semantics=("parallel", "parallel"),
            vmem_limit_bytes=VMEM_LIMIT),
        name="attn_out",
    )(*groups[0], *groups[1], *groups[2], unperm, nat, x, mod, w)


def _cmul(ar, ai, br, bi):
    return ar * br - ai * bi, ar * bi + ai * br


def _s5_prep_kernel(*refs):
    ins, d_diag, outs, v_scr = refs[:9], refs[9], refs[10:14], refs[14]
    for gi in range(d_diag.shape[0]):
        _s5_prep_group(*[r.at[:, gi] for r in ins], d_diag.at[gi], *[o.at[gi] for o in outs], v_scr.at[gi])


def _s5_prep_group(lam_re_a, lam_im_a, lam_re_b, lam_im_b, logdt, bt_re, bt_im,
                   c_re, c_im, d_diag, toep_ref, mb_ref, mc_ref, al_ref, v_scr):
    lane = lax.broadcasted_iota(jnp.int32, (STATE, LANES), 1)
    a_idx = lane // GROUP_CH
    eye = (lax.broadcasted_iota(jnp.int32, (STATE, STATE), 0)
           == lax.broadcasted_iota(jnp.int32, (STATE, STATE), 1))
    n_tiles = CHUNK // T_PER_VREG

    mb_rows = []
    v_rhs = []
    v_lhs = []
    mc_cols = []
    zbar_f = None
    for dr in range(2):
        dt = jnp.exp(logdt[dr])
        lrb = jnp.minimum(lam_re_b[dr], LAMBDA_RE_MAX)
        mag = jnp.exp(lrb * dt)
        pw_r, pw_i = mag * jnp.cos(lam_im_b[dr] * dt), mag * jnp.sin(lam_im_b[dr] * dt)
        diag = lambda v: jnp.sum(jnp.where(eye, jnp.broadcast_to(v, (STATE, STATE)), 0.0), axis=1, keepdims=True)
        lr = jnp.minimum(lam_re_a[dr], LAMBDA_RE_MAX)
        li = lam_im_a[dr]
        l1r, l1i = diag(pw_r), diag(pw_i)
        small = [(jnp.ones_like(l1r), jnp.zeros_like(l1r)), (l1r, l1i)]
        for _ in range(T_PER_VREG - 1):
            small.append(_cmul(*small[-1], l1r, l1i))

        def lane_powers(exponent_of_step):
            pr, pi = small[exponent_of_step[0]]
            pr, pi = jnp.broadcast_to(pr, (STATE, LANES)), jnp.broadcast_to(pi, (STATE, LANES))
            for a in range(1, T_PER_VREG):
                er, ei = small[exponent_of_step[a]]
                pr, pi = jnp.where(a_idx == a, er, pr), jnp.where(a_idx == a, ei, pi)
            return pr, pi

        den = lr * lr + li * li
        nr, ni = l1r - 1.0, l1i
        cr, ci = (nr * lr + ni * li) / den, (ni * lr - nr * li) / den
        bbr, bbi = _cmul(cr, ci, bt_re[dr], bt_im[dr])
        l8r, l8i = small[T_PER_VREG]
        tiles = [None] * n_tiles
        if dr == 0:
            pr, pi = lane_powers([T_PER_VREG - 1 - a for a in range(T_PER_VREG)])
            cur = _cmul(pr, pi, bbr, bbi)
            for jt in range(n_tiles - 1, -1, -1):
                tiles[jt] = cur
                if jt:
                    cur = _cmul(cur[0], cur[1], l8r, l8i)
            v_rhs.append(_cmul(tiles[n_tiles - 1][0], tiles[n_tiles - 1][1], l1r, l1i))
            zbar_f = (jnp.where(lane < GROUP_CH, bbr, 0.0), jnp.where(lane < GROUP_CH, bbi, 0.0))
        else:
            pr, pi = lane_powers(list(range(T_PER_VREG)))
            cur = _cmul(pr, pi, bbr, bbi)
            for jt in range(n_tiles):
                tiles[jt] = cur
                if jt < n_tiles - 1:
                    cur = _cmul(cur[0], cur[1], l8r, l8i)
            v_rhs.append(tiles[0])
        mb_rows.append(jnp.concatenate([t[0] for t in tiles], axis=1))
        mb_rows.append(jnp.concatenate([t[1] for t in tiles], axis=1))
        a_r, a_i = l8r, l8i
        for _ in range(int(math.log2(CHUNK // T_PER_VREG))):
            a_r, a_i = _cmul(a_r, a_i, a_r, a_i)
        al_ref[2 * dr] = jnp.broadcast_to(a_r, (STATE, LANES))
        al_ref[2 * dr + 1] = jnp.broadcast_to(a_i, (STATE, LANES))

        pows = [(pw_r, pw_i)]
        for _ in range(int(math.log2(CHUNK)) - 1):
            pows.append(_cmul(*pows[-1], *pows[-1]))
        cre, cim = c_re[dr], c_im[dr]
        xr, xi = _cmul(cre, cim, *pows[0])
        for k in range(int(math.log2(CHUNK))):
            yr, yi = _cmul(xr, xi, *pows[k])
            if dr == 0:
                xr, xi = jnp.concatenate([xr, yr], 0), jnp.concatenate([xi, yi], 0)
            else:
                xr, xi = jnp.concatenate([yr, xr], 0), jnp.concatenate([yi, xi], 0)
        mc_cols += [xr, -xi]
        xr, xi = cre, cim
        for k in range(int(math.log2(n_tiles))):
            yr, yi = _cmul(xr, xi, *pows[k + int(math.log2(T_PER_VREG))])
            if dr == 0:
                xr, xi = jnp.concatenate([yr, xr], 0), jnp.concatenate([yi, xi], 0)
            else:
                xr, xi = jnp.concatenate([xr, yr], 0), jnp.concatenate([xi, yi], 0)
        v_lhs.append((xr, xi))

    mb_ref[...] = jnp.concatenate(mb_rows, axis=0).astype(BF16)
    mc_ref[...] = jnp.concatenate(mc_cols, axis=1).astype(BF16)

    def cdot(lhs, rhs):
        return (jnp.dot(lhs[0], rhs[0], precision=HI, preferred_element_type=F32)
                - jnp.dot(lhs[1], rhs[1], precision=HI, preferred_element_type=F32))

    out_f = cdot(v_lhs[0], v_rhs[0])
    zero_rows = jnp.zeros((CHUNK_ROWS // T_PER_VREG - GROUP_CH, STATE), F32)
    lag0_lhs = (jnp.concatenate([c_re[0], zero_rows], 0), jnp.concatenate([c_im[0], zero_rows], 0))
    out_b = cdot(v_lhs[1], v_rhs[1]) + cdot(lag0_lhs, zbar_f)
    out_b = out_b + jnp.concatenate([d_diag[...], jnp.zeros((CHUNK_ROWS // T_PER_VREG - GROUP_CH, LANES), F32)], 0)
    for jt in range(n_tiles):
        rows = slice(jt * GROUP_CH, (jt + 1) * GROUP_CH)
        v_scr[:, jt * LANES:(jt + 1) * LANES] = out_f[rows]
        v_scr[:, (n_tiles + jt) * LANES:(n_tiles + jt + 1) * LANES] = out_b[rows]
    v_scr[:, 2 * CHUNK_ROWS:] = jnp.zeros((GROUP_CH, LANES), F32)

    for k in range(T_PER_VREG):
        vk = v_scr[:, k * GROUP_CH:k * GROUP_CH + 2 * CHUNK_ROWS]
        for t_out in range(CHUNK):
            off = (CHUNK - t_out) * GROUP_CH
            if off % LANES == k * GROUP_CH:
                al_off = off - k * GROUP_CH
                toep_ref[t_out * GROUP_CH:(t_out + 1) * GROUP_CH, :] = (
                    vk[:, al_off:al_off + CHUNK_ROWS].astype(BF16))


def _s5_prep(lam_re, lam_im, log_dt, b_re, b_im, c_re, c_im, d_skip):
    g = SSM_GROUPS
    d_diag = d_skip.reshape(g, GROUP_CH, 1) * jnp.eye(GROUP_CH, LANES, dtype=F32)
    lam_a = lambda a: a.reshape(2, g, STATE, 1)
    lam_b = lambda a: a.reshape(2, g, 1, STATE)
    tile_b = lambda a: jnp.tile(a, (1, 1, 1, T_PER_VREG))
    gs = S5_GROUPS_PER_STEP
    dir_spec = lambda r, c: pl.BlockSpec((2, gs, r, c), lambda i: (0, i, 0, 0))
    return pl.pallas_call(
        _s5_prep_kernel,
        out_shape=(jax.ShapeDtypeStruct((g, CHUNK_ROWS, CHUNK_ROWS), BF16),
                   jax.ShapeDtypeStruct((g, 4 * STATE, CHUNK_ROWS), BF16),
                   jax.ShapeDtypeStruct((g, CHUNK_ROWS, 4 * STATE), BF16),
                   jax.ShapeDtypeStruct((g, 4, STATE, LANES), F32)),
        grid=(g // gs,),
        in_specs=[dir_spec(STATE, 1), dir_spec(STATE, 1), dir_spec(1, STATE), dir_spec(1, STATE),
                  dir_spec(1, 1), dir_spec(STATE, LANES), dir_spec(STATE, LANES),
                  dir_spec(GROUP_CH, STATE), dir_spec(GROUP_CH, STATE),
                  pl.BlockSpec((gs, GROUP_CH, LANES), lambda i: (i, 0, 0))],
        out_specs=(pl.BlockSpec((gs, CHUNK_ROWS, CHUNK_ROWS), lambda i: (i, 0, 0)),
                   pl.BlockSpec((gs, 4 * STATE, CHUNK_ROWS), lambda i: (i, 0, 0)),
                   pl.BlockSpec((gs, CHUNK_ROWS, 4 * STATE), lambda i: (i, 0, 0)),
                   pl.BlockSpec((gs, 4, STATE, LANES), lambda i: (i, 0, 0, 0))),
        scratch_shapes=[pltpu.VMEM((gs, GROUP_CH, V_LANES), F32)],
        compiler_params=pltpu.CompilerParams(dimension_semantics=("parallel",)),
        name="s5_prep",
    )(lam_a(lam_re), lam_a(lam_im), lam_b(lam_re), lam_b(lam_im), log_dt.reshape(2, g, 1, 1),
      tile_b(b_re), tile_b(b_im), c_re, c_im, d_diag)


def _slab_perm():
    dst = np.arange(SLAB)
    perm = np.zeros((SLAB, SLAB), np.float32)
    perm[dst, (dst % SLAB_CHUNKS) * SUBLANES + dst // SLAB_CHUNKS] = 1.0
    return perm


def _s5_in_kernel(x_ref, mod_ref, g_ref, w_ref, p_ref, u_ref, z_ref, h_scr, *, chunks_per_seq):
    n_chunks = x_ref.shape[0]
    for b in range(n_chunks // chunks_per_seq):
        rows = slice(b * chunks_per_seq, (b + 1) * chunks_per_seq)
        h = _rms_modulate(x_ref[rows], g_ref[...], mod_ref[b:b + 1, D_MODEL:2 * D_MODEL],
                          mod_ref[b:b + 1, :D_MODEL])
        h2 = h.reshape(chunks_per_seq * SUBLANES, D_MODEL).astype(BF16)
        for s in range(chunks_per_seq // SLAB_CHUNKS):
            hp = jnp.dot(p_ref[...], h2[s * SLAB:(s + 1) * SLAB], preferred_element_type=F32)
            c0 = b * chunks_per_seq + s * SLAB_CHUNKS
            for k in range(SUBLANES):
                h_scr[k, c0:c0 + SLAB_CHUNKS, :] = hp[k * SLAB_CHUNKS:(k + 1) * SLAB_CHUNKS].astype(BF16)
    for k in range(SUBLANES):
        ht = h_scr[k].astype(F32).T.astype(BF16)
        uz = jnp.dot(w_ref[...], ht, preferred_element_type=F32)
        u_ref[k] = uz[:SSM_WIDTH].astype(BF16)
        z_ref[k] = uz[SSM_WIDTH:].astype(BF16)


def _s5_in(x3, mod, g, w_t, chunks_per_seq):
    n_chunks = x3.shape[0]
    once = lambda shape: pl.BlockSpec(shape, lambda t: (0,) * len(shape), pipeline_mode=pl.Buffered(1))
    out = pl.BlockSpec((SUBLANES, SSM_WIDTH, n_chunks), lambda t: (t, 0, 0))
    return pl.pallas_call(
        functools.partial(_s5_in_kernel, chunks_per_seq=chunks_per_seq),
        out_shape=(jax.ShapeDtypeStruct((CHUNK, SSM_WIDTH, n_chunks), BF16),) * 2,
        grid=(CHUNK // SUBLANES,),
        in_specs=[pl.BlockSpec((n_chunks, SUBLANES, D_MODEL), lambda t: (0, t, 0)),
                  once(mod.shape), once((1, D_MODEL)), once((2 * SSM_WIDTH, D_MODEL)),
                  once((SLAB, SLAB))],
        out_specs=(out, out),
        scratch_shapes=[pltpu.VMEM((SUBLANES, n_chunks, D_MODEL), BF16)],
        compiler_params=pltpu.CompilerParams(
            dimension_semantics=("parallel",), vmem_limit_bytes=VMEM_LIMIT_BIG),
        name="s5_in",
    )(x3, mod, g, w_t, jnp.asarray(_slab_perm(), BF16))


def _gelu_tanh(v):
    return 0.5 * v * (1.0 + jnp.tanh(math.sqrt(2.0 / math.pi) * (v + 0.044715 * (v * v * v))))


def _s5_core_kernel(u_ref, toep_ref, mb_ref, mc_ref, al_ref, o_ref, *, chunks_per_seq):
    for gi in range(u_ref.shape[1]):
        _s5_core_group(u_ref.at[:, gi], toep_ref.at[gi], mb_ref.at[gi], mc_ref.at[gi], al_ref.at[gi],
                       o_ref.at[:, gi], chunks_per_seq)


def _s5_core_group(u_ref, toep_ref, mb_ref, mc_ref, al_ref, o_ref, chunks_per_seq):
    n_chunks = u_ref.shape[-1]
    ub = u_ref[...].reshape(CHUNK_ROWS, n_chunks)
    inc = jnp.dot(mb_ref[...], ub, preferred_element_type=F32)
    pos = lax.broadcasted_iota(jnp.int32, (STATE, n_chunks), 1) % chunks_per_seq
    reps = n_chunks // LANES

    def lane_tile(a):
        return jnp.concatenate([a] * reps, axis=1) if reps > 1 else a

    states = []
    for dr in range(2):
        xr = inc[(2 * dr) * STATE:(2 * dr + 1) * STATE]
        xi = inc[(2 * dr + 1) * STATE:(2 * dr + 2) * STATE]
        ar, ai = al_ref[2 * dr], al_ref[2 * dr + 1]

        def shifted(v, step):
            if dr == 0:
                return jnp.where(pos >= step, pltpu.roll(v, step, 1), 0.0)
            return jnp.where(pos < chunks_per_seq - step, pltpu.roll(v, n_chunks - step, 1), 0.0)

        step = 1
        while step < chunks_per_seq:
            sr, si = shifted(xr, step), shifted(xi, step)
            tr, ti = lane_tile(ar), lane_tile(ai)
            xr, xi = xr + tr * sr - ti * si, xi + tr * si + ti * sr
            ar, ai = _cmul(ar, ai, ar, ai)
            step *= 2
        states += [shifted(xr, 1), shifted(xi, 1)]
    h_in = jnp.concatenate(states, axis=0).astype(BF16)
    y = (jnp.dot(toep_ref[...], ub, preferred_element_type=F32)
         + jnp.dot(mc_ref[...], h_in, preferred_element_type=F32))
    o_ref[...] = _gelu_tanh(y).astype(BF16).reshape(CHUNK, GROUP_CH, n_chunks)


def _s5_core(u_cl, toep, mb, mc, al, chunks_per_seq):
    n_chunks = u_cl.shape[-1]
    u4 = u_cl.reshape(CHUNK, SSM_GROUPS, GROUP_CH, n_chunks)
    gs = S5_GROUPS_PER_STEP
    grp = pl.BlockSpec((CHUNK, gs, GROUP_CH, n_chunks), lambda g: (0, g, 0, 0))
    out = pl.pallas_call(
        functools.partial(_s5_core_kernel, chunks_per_seq=chunks_per_seq),
        out_shape=jax.ShapeDtypeStruct(u4.shape, BF16),
        grid=(SSM_GROUPS // gs,),
        in_specs=[grp,
                  pl.BlockSpec((gs, CHUNK_ROWS, CHUNK_ROWS), lambda g: (g, 0, 0)),
                  pl.BlockSpec((gs, 4 * STATE, CHUNK_ROWS), lambda g: (g, 0, 0)),
                  pl.BlockSpec((gs, CHUNK_ROWS, 4 * STATE), lambda g: (g, 0, 0)),
                  pl.BlockSpec((gs, 4, STATE, LANES), lambda g: (g, 0, 0, 0))],
        out_specs=grp,
        compiler_params=pltpu.CompilerParams(
            dimension_semantics=("parallel",), vmem_limit_bytes=VMEM_LIMIT),
        name="s5_core",
    )(u4, toep, mb, mc, al)
    return out.reshape(CHUNK, SSM_WIDTH, n_chunks)


def _s5_out_kernel(g_ref, z_ref, wglu_ref, wout_ref, pt_ref, x_ref, mod_ref, fg_ref, o_ref, y_scr,
                   *, final, chunks_per_seq):
    q = pl.program_id(1)
    n_half = x_ref.shape[0]
    k_per = g_ref.shape[0]
    n_glu = SUBLANES // k_per

    for qv in range(n_glu):
        @pl.when(q == qv)
        def _():
            for kk in range(k_per):
                gb = g_ref[kk]
                glu = jnp.dot(wglu_ref[...], gb, preferred_element_type=F32)
                y = gb.astype(F32) * _sigmoid(glu) * _silu(z_ref[kk].astype(F32))
                y_scr[qv * k_per + kk] = y.T.astype(BF16)

    for hv in range(y_scr.shape[1] // n_half):
        @pl.when(q == n_glu + hv)
        def _():
            for s in range(n_half // SLAB_CHUNKS):
                c0 = hv * n_half + s * SLAB_CHUNKS
                src = jnp.concatenate([y_scr[k, c0:c0 + SLAB_CHUNKS, :] for k in range(SUBLANES)], axis=0)
                yp = jnp.dot(pt_ref[...], src, preferred_element_type=F32).astype(BF16)
                out = jnp.dot(yp, wout_ref[...], preferred_element_type=F32)
                b = c0 // chunks_per_seq
                upd = mod_ref[b:b + 1, 2 * D_MODEL:] * out
                rows = slice(s * SLAB_CHUNKS, (s + 1) * SLAB_CHUNKS)
                xn = x_ref[rows] + upd.reshape(SLAB_CHUNKS, SUBLANES, D_MODEL)
                if final:
                    xn = xn * lax.rsqrt(jnp.mean(xn * xn, axis=-1, keepdims=True) + NORM_EPS) * fg_ref[...]
                o_ref[rows] = xn


def _s5_out(g_cl, z_cl, wglu_t, w_out, x3, mod, final_g, final, chunks_per_seq, n_split=2):
    n_chunks = x3.shape[0]
    n_half = n_chunks // n_split
    n_glu = SUBLANES // S5_OUT_K
    once = lambda shape: pl.BlockSpec(shape, lambda t, q: (0,) * len(shape), pipeline_mode=pl.Buffered(1))
    act = pl.BlockSpec((S5_OUT_K, SSM_WIDTH, n_chunks), lambda t, q: (t * n_glu + jnp.minimum(q, n_glu - 1), 0, 0))
    row = pl.BlockSpec((n_half, SUBLANES, D_MODEL), lambda t, q: (jnp.maximum(q - n_glu, 0), t, 0))
    return pl.pallas_call(
        functools.partial(_s5_out_kernel, final=final, chunks_per_seq=chunks_per_seq),
        out_shape=jax.ShapeDtypeStruct(x3.shape, F32),
        grid=(CHUNK // SUBLANES, n_glu + n_split),
        in_specs=[act, act, once((SSM_WIDTH, SSM_WIDTH)), once((SSM_WIDTH, D_MODEL)), once((SLAB, SLAB)),
                  row, once(mod.shape), once((1, D_MODEL))],
        out_specs=row,
        scratch_shapes=[pltpu.VMEM((SUBLANES, n_chunks, SSM_WIDTH), BF16)],
        compiler_params=pltpu.CompilerParams(
            dimension_semantics=("parallel", "arbitrary"), vmem_limit_bytes=VMEM_LIMIT_BIG),
        name="s5_out",
    )(g_cl, z_cl, wglu_t, w_out, jnp.asarray(_slab_perm().T, BF16), x3, mod, final_g)


def _rope_tables(s):
    inv_freq = ROPE_THETA ** (-jnp.arange(0, HEAD_DIM, 2, dtype=F32) / HEAD_DIM)
    ang = jnp.arange(s, dtype=F32)[:, None] * inv_freq[None, :]
    reps = LANES // (HEAD_DIM // 2)
    sign = jnp.where(jnp.arange(LANES) < LANES // 2, -1.0, 1.0).astype(F32)
    cos, sin = jnp.tile(jnp.cos(ang), (1, reps)), jnp.tile(jnp.sin(ang), (1, reps)) * sign
    stream_major = lambda a: a.reshape(s // STREAMS, STREAMS, LANES).transpose(1, 0, 2)
    return cos, sin, stream_major(cos), stream_major(sin)


def _prep_attn_w_in(w):
    d_in = w.shape[0]
    w = w.reshape(d_in, -1, ATTN_HEADS // 2, 2, 2, HEAD_DIM // 2)
    w = jnp.stack([w[:, i] for i in ATTN_IN_BLOCK_ORDER], axis=1)
    return w.transpose(0, 1, 2, 4, 3, 5).astype(BF16).reshape(d_in, -1)


def _prep_attn_w_out(w):
    w = w.astype(BF16).reshape(ATTN_HEADS // 2, 2, 2, HEAD_DIM // 2, w.shape[1])
    return w.transpose(0, 2, 1, 3, 4).reshape(ATTN_WIDTH, -1)


def _trunk(x, ada, norm_g, attn_w, s5_w, s5_ops, tables, final_norm_g):
    bsz, s, _ = x.shape
    chunks_per_seq = s // CHUNK
    n_chunks = bsz * chunks_per_seq
    fg = final_norm_g.reshape(1, D_MODEL)
    for i in range(DEPTH):
        mod = ada[i].reshape(bsz, 1, 3 * D_MODEL)
        g = norm_g[i].reshape(1, D_MODEL)
        j = i // N_MIXERS
        if i % N_MIXERS == 0:
            w_in, w_out = attn_w[j]
            nat, qkv12 = _attn_in(x, mod, g, w_in, tables)
            x = _attn_out(_attention(nat, qkv12), nat, x, mod, w_out)
        else:
            w_in_t, wglu_t, w_out = s5_w[j]
            toep, mb, mc, al = s5_ops[j]
            x3 = x.reshape(n_chunks, CHUNK, D_MODEL)
            u_cl, z_cl = _s5_in(x3, ada[i], g, w_in_t, chunks_per_seq)
            g_cl = _s5_core(u_cl, toep, mb, mc, al, chunks_per_seq)
            x3 = _s5_out(g_cl, z_cl, wglu_t, w_out, x3, ada[i], fg, i == DEPTH - 1, chunks_per_seq)
            x = x3.reshape(bsz, s, D_MODEL)
    return x


def kernel(x_prompt, x_sample, c_prompt, c_sample, norm_g, ada_w, ada_b, attn_w_in, attn_w_out,
           ssm_w_in, ssm_lam_re, ssm_lam_im, ssm_log_dt, ssm_b_re, ssm_b_im, ssm_c_re, ssm_c_im,
           ssm_d, ssm_w_glu, ssm_w_out, final_norm_g):
    assert (DEPTH - 1) % N_MIXERS == 1, "the final norm is fused into the last S5 layer"
    assert all(w // (2 * d) == RADIUS for w, d in DILATED_PAIRS)
    assert [d for _, d in DILATED_PAIRS] == [1, 4, STREAMS]
    n_prompt = c_prompt.shape[0]
    ada = _ada(jnp.concatenate([c_prompt, c_sample], axis=0), ada_w, ada_b)
    attn_w = [(_prep_attn_w_in(attn_w_in[j]), _prep_attn_w_out(attn_w_out[j]))
              for j in range(attn_w_in.shape[0])]
    s5_w, s5_ops = [], []
    for j in range(ssm_w_in.shape[0]):
        s5_w.append((ssm_w_in[j].T.astype(BF16), ssm_w_glu[j].T.astype(BF16), ssm_w_out[j].astype(BF16)))
        s5_ops.append(_s5_prep(ssm_lam_re[j], ssm_lam_im[j], ssm_log_dt[j], ssm_b_re[j],
                               ssm_b_im[j], ssm_c_re[j], ssm_c_im[j], ssm_d[j]))
    tables = _rope_tables(max(x_prompt.shape[1], x_sample.shape[1]))
    y_prompt = _trunk(x_prompt, ada[:, :n_prompt], norm_g, attn_w, s5_w, s5_ops, tables, final_norm_g)
    y_sample = _trunk(x_sample, ada[:, n_prompt:], norm_g, attn_w, s5_w, s5_ops, tables, final_norm_g)
    return (y_prompt, y_sample)
```

```python
import functools
import math

import numpy as np
import jax
import jax.numpy as jnp
from jax import lax
from jax.experimental import pallas as pl
from jax.experimental.pallas import tpu as pltpu

D_MODEL = 1024
DEPTH = 4
N_MIXERS = 2
ATTN_HEADS = 16
HEAD_DIM = 64
ATTN_WIDTH = ATTN_HEADS * HEAD_DIM
DILATED_PAIRS = ((128, 1), (512, 4), (2048, 16))
N_DIL = len(DILATED_PAIRS)
ROPE_THETA = 10000.0
SSM_WIDTH = D_MODEL
GROUP_CH = 16
SSM_GROUPS = SSM_WIDTH // GROUP_CH
STATE = 64
LAMBDA_RE_MAX = -1e-4
NORM_EPS = 1e-6
NEG_INF = -1e30

LANES = 128
SUBLANES = 8
CHUNK = 64
CHUNK_ROWS = CHUNK * GROUP_CH
T_PER_VREG = LANES // GROUP_CH
V_LANES = 2 * CHUNK_ROWS + LANES
STREAMS = 16
RADIUS = 64
ATTN_TQ = 128
ATTN_SUB = 8
STAT_PITCH = 40
ATTN_IN_BLOCK_ORDER = (1, 2, 0, 9, 4, 5, 7, 8, 3, 6)
PERM_ROWS = 512
UNPERM_ROWS = 256
LOG2_E = math.log2(math.e)
S5_GROUPS_PER_STEP = 2
S5_OUT_K = 4
SLAB = 256
SLAB_CHUNKS = SLAB // SUBLANES
VMEM_LIMIT = 48 * 1024 * 1024
VMEM_LIMIT_BIG = 56 * 1024 * 1024

F32 = jnp.float32
BF16 = jnp.bfloat16
HI = lax.Precision.HIGHEST


def _sigmoid(v):
    return 0.5 * jnp.tanh(0.5 * v) + 0.5


def _silu(v):
    return v * _sigmoid(v)


def _rms_modulate(x, g, scale, shift):
    rs = lax.rsqrt(jnp.mean(x * x, axis=-1, keepdims=True) + NORM_EPS)
    return (x * rs * g) * (1.0 + scale) + shift


def _ada_kernel(c_ref, w_ref, b_ref, o_ref):
    o_ref[...] = jnp.dot(_silu(c_ref[...]), w_ref[...], precision=HI,
                         preferred_element_type=F32) + b_ref[...]


def _ada(c, ada_w, ada_b):
    nb = c.shape[0]
    return pl.pallas_call(
        _ada_kernel,
        out_shape=jax.ShapeDtypeStruct((DEPTH, nb, 3 * D_MODEL), F32),
        grid=(DEPTH, 3),
        in_specs=[
            pl.BlockSpec((nb, D_MODEL), lambda i, j: (0, 0)),
            pl.BlockSpec((None, D_MODEL, D_MODEL), lambda i, j: (i, 0, j)),
            pl.BlockSpec((None, 1, D_MODEL), lambda i, j: (i, 0, j)),
        ],
        out_specs=pl.BlockSpec((None, nb, D_MODEL), lambda i, j: (i, 0, j)),
        name="ada",
    )(c, ada_w, ada_b.reshape(DEPTH, 1, 3 * D_MODEL))


def _attn_in_kernel(x_ref, shift_ref, scale_ref, g_ref, w_ref, cos_ref, sin_ref, cosp_ref, sinp_ref,
                    perm_ref, nat_ref, sm_ref, h_nat, h_perm):
    j = pl.program_id(2)
    tm = x_ref.shape[0]
    n_sub = tm // PERM_ROWS
    sub_rows = PERM_ROWS // STREAMS

    @pl.when(j == 0)
    def _():
        h = _rms_modulate(x_ref[...], g_ref[...], scale_ref[...], shift_ref[...])
        h_nat[...] = h.astype(BF16)
        for u in range(n_sub):
            rs = slice(u * PERM_ROWS, (u + 1) * PERM_ROWS)
            h_perm[rs, :] = jnp.dot(perm_ref[...], h_nat[rs, :], preferred_element_type=F32).astype(BF16)

    def rope_blocks(acc, cos, sin, second_is_plain):
        for half in range(2):
            is_q = jnp.logical_or(j == 4, jnp.logical_and(j == 1, half == 0))
            qscale = jnp.where(is_q, HEAD_DIM ** -0.5 * LOG2_E, 1.0).astype(F32)
            c, sn = cos * qscale, sin * qscale
            for b in range(ATTN_WIDTH // LANES):
                sl = slice(half * ATTN_WIDTH + b * LANES, half * ATTN_WIDTH + (b + 1) * LANES)
                t = acc[:, sl]
                if half == 1 and second_is_plain:
                    yield sl, t.astype(BF16)
                else:
                    yield sl, (t * c + pltpu.roll(t, LANES // 2, 1) * sn).astype(BF16)

    @pl.when(j < 2)
    def _():
        acc = jnp.dot(h_nat[...], w_ref[...], preferred_element_type=F32)
        for sl, blk in rope_blocks(acc, cos_ref[...], sin_ref[...], True):
            nat_ref[:, sl] = blk

    def stream_major_step(second_is_plain):
        acc = jnp.dot(h_perm[...], w_ref[...], preferred_element_type=F32)

        def table(ref):
            return jnp.concatenate([ref[:, u * sub_rows:(u + 1) * sub_rows, :].reshape(PERM_ROWS, LANES)
                                    for u in range(n_sub)], axis=0)

        for sl, blk in rope_blocks(acc, table(cosp_ref), table(sinp_ref), second_is_plain):
            for u in range(n_sub):
                sm_ref[:, u * sub_rows:(u + 1) * sub_rows, sl] = (
                    blk[u * PERM_ROWS:(u + 1) * PERM_ROWS].reshape(STREAMS, sub_rows, LANES))

    pl.when(jnp.logical_and(j >= 2, j < 4))(lambda: stream_major_step(True))
    pl.when(j == 4)(lambda: stream_major_step(False))


def _stream_perm(n):
    dst = np.arange(n)
    rows = n // STREAMS
    perm = np.zeros((n, n), np.float32)
    perm[dst, STREAMS * (dst % rows) + dst // rows] = 1.0
    return perm


def _attn_in(x, mod, g, w, tables, tm=1024):
    bsz, s, _ = x.shape
    cos, sin, cosp, sinp = tables
    rows = tm // STREAMS
    wide = 2 * ATTN_WIDTH
    perm = jnp.asarray(_stream_perm(PERM_ROWS), BF16)
    return pl.pallas_call(
        _attn_in_kernel,
        out_shape=(jax.ShapeDtypeStruct((bsz, s, 2 * wide), BF16),
                   jax.ShapeDtypeStruct((bsz, STREAMS, s // STREAMS, 3 * wide), BF16)),
        grid=(bsz, s // tm, 5),
        in_specs=[
            pl.BlockSpec((None, tm, D_MODEL), lambda b, i, j: (b, i, 0)),
            pl.BlockSpec((None, 1, D_MODEL), lambda b, i, j: (b, 0, 0)),
            pl.BlockSpec((None, 1, D_MODEL), lambda b, i, j: (b, 0, 1)),
            pl.BlockSpec((1, D_MODEL), lambda b, i, j: (0, 0)),
            pl.BlockSpec((D_MODEL, wide), lambda b, i, j: (0, j)),
            pl.BlockSpec((tm, LANES), lambda b, i, j: (i, 0)),
            pl.BlockSpec((tm, LANES), lambda b, i, j: (i, 0)),
            pl.BlockSpec((STREAMS, rows, LANES), lambda b, i, j: (0, i, 0)),
            pl.BlockSpec((STREAMS, rows, LANES), lambda b, i, j: (0, i, 0)),
            pl.BlockSpec((PERM_ROWS, PERM_ROWS), lambda b, i, j: (0, 0)),
        ],
        out_specs=(
            pl.BlockSpec((None, tm, wide), lambda b, i, j: (b, i, jnp.clip(j, 0, 1))),
            pl.BlockSpec((None, STREAMS, rows, wide), lambda b, i, j: (b, 0, i, jnp.clip(j - 2, 0, 2))),
        ),
        scratch_shapes=[pltpu.VMEM((tm, D_MODEL), BF16), pltpu.VMEM((tm, D_MODEL), BF16)],
        compiler_params=pltpu.CompilerParams(
            dimension_semantics=("parallel", "parallel", "arbitrary"),
            vmem_limit_bytes=VMEM_LIMIT_BIG),
        name="attn_in",
    )(x, mod, mod, g, w, cos, sin, cosp, sinp, perm)


def _lane_of_odd_head(lane):
    return (lane // (HEAD_DIM // 2)) % 2 == 1


def _mask_bias(valid):
    return jnp.where(valid, 0.0, NEG_INF).astype(F32)


def _attn_pairs(get_q, get_k, get_v, bias, put_acc):
    tq = bias.shape[0]
    lane = lax.broadcasted_iota(jnp.int32, (tq, LANES), 1)
    q_is_odd = _lane_of_odd_head(lane)
    mx_tile = jnp.zeros((tq, LANES), F32)
    den_tile = jnp.ones((tq, LANES), F32)
    ones = jnp.ones((bias.shape[1], LANES), BF16)
    bias2 = jnp.concatenate([bias, bias], axis=0)
    for b in range(ATTN_WIDTH // LANES):
        sl = slice(b * LANES, (b + 1) * LANES)
        qb = get_q(sl).astype(F32)
        kb = get_k(sl)
        vb = jnp.concatenate([get_v(sl), ones], axis=1)
        qm = jnp.concatenate([jnp.where(q_is_odd, 0.0, qb), jnp.where(q_is_odd, qb, 0.0)], axis=0).astype(BF16)
        sc = lax.dot_general(qm, kb, (((1,), (1,)), ((), ())), preferred_element_type=F32) + bias2
        mx = jnp.max(sc, axis=1, keepdims=True)
        p = jnp.exp2((sc - mx).astype(BF16))
        acc = jnp.dot(p, vb, preferred_element_type=F32)
        for odd in (0, 1):
            rows = slice(odd * tq, (odd + 1) * tq)
            mx_tile = jnp.where(lane == 2 * b + odd, mx[rows], mx_tile)
            den_tile = jnp.where(lane == 2 * b + odd, acc[rows, LANES:], den_tile)
        put_acc(sl, jnp.where(q_is_odd, acc[tq:, :LANES], acc[:tq, :LANES]))
    return mx_tile, den_tile


def _pair_factor(tile, b):
    lane = lax.broadcasted_iota(jnp.int32, tile.shape, 1)
    head = 2 * b + _lane_of_odd_head(lane).astype(jnp.int32)
    return jnp.take_along_axis(tile, head, axis=1, mode="promise_in_bounds")


def _merge_groups(stats, get_accs, put_o):
    top = functools.reduce(jnp.maximum, [mx for mx, _ in stats])
    ws = [jnp.exp2(mx - top) for mx, _ in stats]
    inv = 1.0 / sum(w * den for w, (_, den) in zip(ws, stats))
    fs = [w * inv for w in ws]
    for b in range(ATTN_WIDTH // LANES):
        sl = slice(b * LANES, (b + 1) * LANES)
        put_o(sl, sum(_pair_factor(f, b) * get_acc(sl) for f, get_acc in zip(fs, get_accs)))


def _window_mask(tile, tq, m):
    nk = tq + 2 * RADIUS
    row = lax.broadcasted_iota(jnp.int32, (tq, nk), 0)
    col = lax.broadcasted_iota(jnp.int32, (tq, nk), 1)
    band = _mask_bias(jnp.abs(col - RADIUS - row) <= RADIUS)
    key = lax.broadcasted_iota(jnp.int32, (1, nk), 1) + (tile * tq - RADIUS)
    return band + _mask_bias((key >= 0) & (key < m))


def _stack_rows(read, prev_ref, cur_ref, next_ref, lo, cnt):
    halo, cur_n = prev_ref.shape[-2], cur_ref.shape[-2]
    a, b = lo, lo + cnt
    parts = []
    if a < 0:
        parts.append(read(prev_ref, halo + a, halo))
        a = 0
    parts.append(read(cur_ref, a, min(b, cur_n)))
    if b > cur_n:
        parts.append(read(next_ref, 0, b - cur_n))
    return parts


def _kv_getters(stack):
    get_k = lambda sl: stack(sl)
    get_v = lambda sl: stack(slice(ATTN_WIDTH + sl.start, ATTN_WIDTH + sl.stop))
    return get_k, get_v


def _attn16_kernel(*refs, m):
    for r in range(refs[0].shape[0]):
        _attn16_stream(*[ref.at[r] for ref in refs], m=m)


def _attn16_stream(q_ref, kvp_ref, kvc_ref, kvn_ref, o_ref, mx_ref, den_ref, *, m):
    tq = ATTN_TQ
    n_sub = q_ref.shape[0] // tq
    for t in range(n_sub):
        rows = slice(t * tq, (t + 1) * tq)
        bias = _window_mask(pl.program_id(2) * n_sub + t, tq, m)

        def stack(cols):
            return jnp.concatenate(_stack_rows(lambda ref, a, b: ref[a:b, cols], kvp_ref, kvc_ref, kvn_ref,
                                               t * tq - RADIUS, tq + 2 * RADIUS), axis=0)

        def put_acc(sl, val):
            o_ref[rows, sl] = val.astype(BF16)

        mx_ref[rows], den_ref[rows] = _attn_pairs(lambda sl: q_ref[rows, sl], *_kv_getters(stack), bias, put_acc)


def _attn4_kernel(*refs, m):
    for r in range(refs[0].shape[1]):
        _attn4_residue(*[ref.at[:, r] for ref in refs], m=m)


def _attn4_residue(q_ref, kvp_ref, kvc_ref, kvn_ref, o_ref, mx_ref, den_ref, *, m):
    ns, halo = q_ref.shape[0], kvp_ref.shape[1]
    rows = ATTN_TQ // ns
    span = rows + 2 * halo
    n_sub = q_ref.shape[1] // rows
    n = lax.broadcasted_iota(jnp.int32, (ns * rows, ns * span), 0)
    c = lax.broadcasted_iota(jnp.int32, (ns * rows, ns * span), 1)
    for t in range(n_sub):
        base = (pl.program_id(2) * n_sub + t) * rows
        rs = slice(t * rows, (t + 1) * rows)
        krow = base - halo + c % span
        rel = ns * (krow - (base + n % rows)) + (c // span - n // rows)
        bias = _mask_bias((jnp.abs(rel) <= RADIUS) & (krow >= 0) & (krow < m))

        def stack(cols):
            parts = []
            for s in range(ns):
                parts += _stack_rows(lambda ref, a, b: ref[s, a:b, cols], kvp_ref, kvc_ref, kvn_ref,
                                     t * rows - halo, span)
            return jnp.concatenate(parts, axis=0)

        flat = lambda ref: (lambda sl: ref[:, rs, sl].reshape(ns * rows, LANES))

        def put_acc(sl, val):
            o_ref[:, rs, sl] = val.astype(BF16).reshape(ns, rows, LANES)

        mx, den = _attn_pairs(flat(q_ref), *_kv_getters(stack), bias, put_acc)
        mx_ref[:, rs, :] = mx.reshape(ns, rows, LANES)
        den_ref[:, rs, :] = den.reshape(ns, rows, LANES)


def _attn1_kernel(q_ref, kvp_ref, kvc_ref, kvn_ref, o_ref, mx_ref, den_ref, *, m):
    tq = ATTN_TQ
    n_sub = q_ref.shape[0] // tq
    for t in range(n_sub):
        rows = slice(t * tq, (t + 1) * tq)
        bias = _window_mask(pl.program_id(1) * n_sub + t, tq, m)

        def stack(cols):
            return jnp.concatenate(_stack_rows(lambda ref, a, b: ref[a:b, cols], kvp_ref, kvc_ref, kvn_ref,
                                               t * tq - RADIUS, tq + 2 * RADIUS), axis=0)

        def put_acc(sl, val):
            o_ref[rows, sl] = val.astype(BF16)

        mx_ref[rows], den_ref[rows] = _attn_pairs(lambda sl: q_ref[rows, sl], *_kv_getters(stack), bias, put_acc)


def _attention(qkv0, qkv12):
    bsz, s, _ = qkv0.shape
    m16 = s // STREAMS
    tq, half = ATTN_TQ, RADIUS
    cur_rows = min(ATTN_SUB * tq, m16)
    assert m16 % cur_rows == 0 and cur_rows % tq == 0 and s % (ATTN_SUB * tq) == 0
    params = lambda n: pltpu.CompilerParams(dimension_semantics=("parallel",) * n,
                                            vmem_limit_bytes=VMEM_LIMIT)
    same = lambda i: i

    def halo_idx(cur_n, halo_n, total):
        per = cur_n // halo_n
        return (lambda i: jnp.maximum(i * per - 1, 0)), (lambda i: jnp.minimum((i + 1) * per, total // halo_n - 1))

    n_seq = (ATTN_SUB * tq) // cur_rows

    def spec16(nrows, row_idx, col, width=ATTN_WIDTH):
        return pl.BlockSpec((None, n_seq, nrows, width), lambda b, r, i: (b, r, row_idx(i), col))
    prv, nxt = halo_idx(cur_rows, half, m16)
    stat16 = jax.ShapeDtypeStruct((bsz, STREAMS, m16, LANES), F32)
    o2, mx2, den2 = pl.pallas_call(
        functools.partial(_attn16_kernel, m=m16),
        out_shape=(jax.ShapeDtypeStruct((bsz, STREAMS, m16, ATTN_WIDTH), BF16), stat16, stat16),
        grid=(bsz, STREAMS // n_seq, m16 // cur_rows),
        in_specs=[spec16(cur_rows, same, 5), spec16(half, prv, 1, 2 * ATTN_WIDTH),
                  spec16(cur_rows, same, 1, 2 * ATTN_WIDTH), spec16(half, nxt, 1, 2 * ATTN_WIDTH)],
        out_specs=(spec16(cur_rows, same, 0), spec16(cur_rows, same, 0, LANES), spec16(cur_rows, same, 0, LANES)),
        compiler_params=params(3), name="attn_dil16",
    )(*([qkv12] * 4))

    ns = STREAMS // 4
    rows4, halo4 = cur_rows // ns, half // ns

    def view4(a):
        return a.reshape(bsz, ns, 4, m16, a.shape[-1])

    def spec4(nrows, row_idx, col, width=ATTN_WIDTH):
        return pl.BlockSpec((None, ns, n_seq, nrows, width), lambda b, r, i: (b, 0, r, row_idx(i), col))
    prv4, nxt4 = halo_idx(rows4, halo4, m16)
    q4 = view4(qkv12)
    stat4 = jax.ShapeDtypeStruct((bsz, ns, 4, m16, LANES), F32)
    o1, mx1, den1 = pl.pallas_call(
        functools.partial(_attn4_kernel, m=m16),
        out_shape=(jax.ShapeDtypeStruct((bsz, ns, 4, m16, ATTN_WIDTH), BF16), stat4, stat4),
        grid=(bsz, 4 // n_seq, m16 // rows4),
        in_specs=[spec4(rows4, same, 4), spec4(halo4, prv4, 0, 2 * ATTN_WIDTH),
                  spec4(rows4, same, 0, 2 * ATTN_WIDTH), spec4(halo4, nxt4, 0, 2 * ATTN_WIDTH)],
        out_specs=(spec4(rows4, same, 0), spec4(rows4, same, 0, LANES), spec4(rows4, same, 0, LANES)),
        compiler_params=params(3), name="attn_dil4",
    )(q4, q4, q4, q4)
    streams = lambda a: a.reshape(bsz, STREAMS, m16, a.shape[-1])

    cur1 = ATTN_SUB * tq

    def spec1(nrows, row_idx, col, width=ATTN_WIDTH):
        return pl.BlockSpec((None, nrows, width), lambda b, i: (b, row_idx(i), col))
    prv1, nxt1 = halo_idx(cur1, half, s)
    stat1 = jax.ShapeDtypeStruct((bsz, s, LANES), F32)
    group0 = pl.pallas_call(
        functools.partial(_attn1_kernel, m=s),
        out_shape=(jax.ShapeDtypeStruct((bsz, s, ATTN_WIDTH), BF16), stat1, stat1),
        grid=(bsz, s // cur1),
        in_specs=[spec1(cur1, same, 2), spec1(half, prv1, 0, 2 * ATTN_WIDTH),
                  spec1(cur1, same, 0, 2 * ATTN_WIDTH), spec1(half, nxt1, 0, 2 * ATTN_WIDTH)],
        out_specs=(spec1(cur1, same, 0), spec1(cur1, same, 0, LANES), spec1(cur1, same, 0, LANES)),
        compiler_params=params(2), name="attn_dil1",
    )(*([qkv0] * 4))
    return group0, (o2, mx2, den2), (streams(o1), streams(mx1), streams(den1))


def _attn_out_kernel(o0_ref, mx0_ref, den0_ref, o2_ref, mx2_ref, den2_ref, o1_ref, mx1_ref, den1_ref,
                     unperm_ref, z_ref, x_ref, gate_ref, w_ref, out_ref, y_scr, a2_scr, a1_scr, stat_scr):
    tm = x_ref.shape[0]
    tq = ATTN_TQ
    srow, n_srow = tq // STREAMS, mx2_ref.shape[1]
    n_un = unperm_ref.shape[0]
    for k, src in enumerate((mx2_ref, den2_ref, mx1_ref, den1_ref)):
        for r in range(STREAMS):
            stat_scr[k, r * STAT_PITCH:r * STAT_PITCH + n_srow, :] = src[r]
    for u in range(tm // n_un):
        piece = slice(u * n_un, (u + 1) * n_un)
        for src, dst in ((o2_ref, a2_scr), (o1_ref, a1_scr)):
            rs = slice(u * n_un // STREAMS, (u + 1) * n_un // STREAMS)
            part = jnp.concatenate([src[r, rs, :] for r in range(STREAMS)], axis=0)
            dst[piece, :] = jnp.dot(unperm_ref[...], part, preferred_element_type=F32)
        for t in range(u * n_un // tq, (u + 1) * n_un // tq):
            rows = slice(t * tq, (t + 1) * tq)

            def natural(k):
                return jnp.concatenate([stat_scr[k, pl.ds(t * srow + ii, STREAMS, stride=STAT_PITCH), :]
                                        for ii in range(srow)], axis=0)

            def put_y(sl, val):
                y_scr[rows, sl] = (val * _silu(z_ref[rows, sl].astype(F32))).astype(BF16)

            stats = [(mx0_ref[rows], den0_ref[rows]), (natural(0), natural(1)), (natural(2), natural(3))]
            _merge_groups(stats, [lambda sl: o0_ref[rows, sl].astype(F32), lambda sl: a2_scr[rows, sl],
                                  lambda sl: a1_scr[rows, sl]], put_y)
        out = jnp.dot(y_scr[piece, :], w_ref[...], preferred_element_type=F32)
        out_ref[piece, :] = x_ref[piece, :] + gate_ref[...] * out


def _attn_out(groups, nat, x, mod, w, tm=PERM_ROWS):
    bsz, s, _ = x.shape
    assert s % tm == 0 and tm % UNPERM_ROWS == 0 and tm // STREAMS <= STAT_PITCH
    row = lambda width: pl.BlockSpec((None, tm, width), lambda b, i: (b, i, 0))
    sm = lambda width: pl.BlockSpec((None, STREAMS, tm // STREAMS, width), lambda b, i: (b, 0, i, 0))
    z_col = nat.shape[-1] // ATTN_WIDTH - 1
    unperm = jnp.asarray(_stream_perm(UNPERM_ROWS).T, BF16)
    return pl.pallas_call(
        _attn_out_kernel,
        out_shape=jax.ShapeDtypeStruct(x.shape, F32),
        grid=(bsz, s // tm),
        in_specs=[row(ATTN_WIDTH), row(LANES), row(LANES)] + [sm(ATTN_WIDTH), sm(LANES), sm(LANES)] * 2
                 + [pl.BlockSpec((UNPERM_ROWS, UNPERM_ROWS), lambda b, i: (0, 0)),
                    pl.BlockSpec((None, tm, ATTN_WIDTH), lambda b, i: (b, i, z_col)), row(D_MODEL),
                    pl.BlockSpec((None, 1, D_MODEL), lambda b, i: (b, 0, 2)),
                    pl.BlockSpec((ATTN_WIDTH, D_MODEL), lambda b, i: (0, 0))],
        out_specs=row(D_MODEL),
        scratch_shapes=[pltpu.VMEM((tm, ATTN_WIDTH), BF16), pltpu.VMEM((tm, ATTN_WIDTH), F32),
                        pltpu.VMEM((tm, ATTN_WIDTH), F32), pltpu.VMEM((4, STREAMS * STAT_PITCH, LANES), F32)],
        compiler_params=pltpu.CompilerParams(
            dimension_semantics=("parallel", "parallel"),
            vmem_limit_bytes=VMEM_LIMIT),
        name="attn_out",
    )(*groups[0], *groups[1], *groups[2], unperm, nat, x, mod, w)


def _cmul(ar, ai, br, bi):
    return ar * br - ai * bi, ar * bi + ai * br


def _s5_prep_kernel(*refs):
    ins, d_diag, outs, v_scr = refs[:9], refs[9], refs[10:14], refs[14]
    for gi in range(d_diag.shape[0]):
        _s5_prep_group(*[r.at[:, gi] for r in ins], d_diag.at[gi], *[o.at[gi] for o in outs], v_scr.at[gi])


def _s5_prep_group(lam_re_a, lam_im_a, lam_re_b, lam_im_b, logdt, bt_re, bt_im,
                   c_re, c_im, d_diag, toep_ref, mb_ref, mc_ref, al_ref, v_scr):
    lane = lax.broadcasted_iota(jnp.int32, (STATE, LANES), 1)
    a_idx = lane // GROUP_CH
    eye = (lax.broadcasted_iota(jnp.int32, (STATE, STATE), 0)
           == lax.broadcasted_iota(jnp.int32, (STATE, STATE), 1))
    n_tiles = CHUNK // T_PER_VREG

    mb_rows = []
    v_rhs = []
    v_lhs = []
    mc_cols = []
    zbar_f = None
    for dr in range(2):
        dt = jnp.exp(logdt[dr])
        lrb = jnp.minimum(lam_re_b[dr], LAMBDA_RE_MAX)
        mag = jnp.exp(lrb * dt)
        pw_r, pw_i = mag * jnp.cos(lam_im_b[dr] * dt), mag * jnp.sin(lam_im_b[dr] * dt)
        diag = lambda v: jnp.sum(jnp.where(eye, jnp.broadcast_to(v, (STATE, STATE)), 0.0), axis=1, keepdims=True)
        lr = jnp.minimum(lam_re_a[dr], LAMBDA_RE_MAX)
        li = lam_im_a[dr]
        l1r, l1i = diag(pw_r), diag(pw_i)
        small = [(jnp.ones_like(l1r), jnp.zeros_like(l1r)), (l1r, l1i)]
        for _ in range(T_PER_VREG - 1):
            small.append(_cmul(*small[-1], l1r, l1i))

        def lane_powers(exponent_of_step):
            pr, pi = small[exponent_of_step[0]]
            pr, pi = jnp.broadcast_to(pr, (STATE, LANES)), jnp.broadcast_to(pi, (STATE, LANES))
            for a in range(1, T_PER_VREG):
                er, ei = small[exponent_of_step[a]]
                pr, pi = jnp.where(a_idx == a, er, pr), jnp.where(a_idx == a, ei, pi)
            return pr, pi

        den = lr * lr + li * li
        nr, ni = l1r - 1.0, l1i
        cr, ci = (nr * lr + ni * li) / den, (ni * lr - nr * li) / den
        bbr, bbi = _cmul(cr, ci, bt_re[dr], bt_im[dr])
        l8r, l8i = small[T_PER_VREG]
        tiles = [None] * n_tiles
        if dr == 0:
            pr, pi = lane_powers([T_PER_VREG - 1 - a for a in range(T_PER_VREG)])
            cur = _cmul(pr, pi, bbr, bbi)
            for jt in range(n_tiles - 1, -1, -1):
                tiles[jt] = cur
                if jt:
                    cur = _cmul(cur[0], cur[1], l8r, l8i)
            v_rhs.append(_cmul(tiles[n_tiles - 1][0], tiles[n_tiles - 1][1], l1r, l1i))
            zbar_f = (jnp.where(lane < GROUP_CH, bbr, 0.0), jnp.where(lane < GROUP_CH, bbi, 0.0))
        else:
            pr, pi = lane_powers(list(range(T_PER_VREG)))
            cur = _cmul(pr, pi, bbr, bbi)
            for jt in range(n_tiles):
                tiles[jt] = cur
                if jt < n_tiles - 1:
                    cur = _cmul(cur[0], cur[1], l8r, l8i)
            v_rhs.append(tiles[0])
        mb_rows.append(jnp.concatenate([t[0] for t in tiles], axis=1))
        mb_rows.append(jnp.concatenate([t[1] for t in tiles], axis=1))
        a_r, a_i = l8r, l8i
        for _ in range(int(math.log2(CHUNK // T_PER_VREG))):
            a_r, a_i = _cmul(a_r, a_i, a_r, a_i)
        al_ref[2 * dr] = jnp.broadcast_to(a_r, (STATE, LANES))
        al_ref[2 * dr + 1] = jnp.broadcast_to(a_i, (STATE, LANES))

        pows = [(pw_r, pw_i)]
        for _ in range(int(math.log2(CHUNK)) - 1):
            pows.append(_cmul(*pows[-1], *pows[-1]))
        cre, cim = c_re[dr], c_im[dr]
        xr, xi = _cmul(cre, cim, *pows[0])
        for k in range(int(math.log2(CHUNK))):
            yr, yi = _cmul(xr, xi, *pows[k])
            if dr == 0:
                xr, xi = jnp.concatenate([xr, yr], 0), jnp.concatenate([xi, yi], 0)
            else:
                xr, xi = jnp.concatenate([yr, xr], 0), jnp.concatenate([yi, xi], 0)
        mc_cols += [xr, -xi]
        xr, xi = cre, cim
        for k in range(int(math.log2(n_tiles))):
            yr, yi = _cmul(xr, xi, *pows[k + int(math.log2(T_PER_VREG))])
            if dr == 0:
                xr, xi = jnp.concatenate([yr, xr], 0), jnp.concatenate([yi, xi], 0)
            else:
                xr, xi = jnp.concatenate([xr, yr], 0), jnp.concatenate([xi, yi], 0)
        v_lhs.append((xr, xi))

    mb_ref[...] = jnp.concatenate(mb_rows, axis=0).astype(BF16)
    mc_ref[...] = jnp.concatenate(mc_cols, axis=1).astype(BF16)

    def cdot(lhs, rhs):
        return (jnp.dot(lhs[0], rhs[0], precision=HI, preferred_element_type=F32)
                - jnp.dot(lhs[1], rhs[1], precision=HI, preferred_element_type=F32))

    out_f = cdot(v_lhs[0], v_rhs[0])
    zero_rows = jnp.zeros((CHUNK_ROWS // T_PER_VREG - GROUP_CH, STATE), F32)
    lag0_lhs = (jnp.concatenate([c_re[0], zero_rows], 0), jnp.concatenate([c_im[0], zero_rows], 0))
    out_b = cdot(v_lhs[1], v_rhs[1]) + cdot(lag0_lhs, zbar_f)
    out_b = out_b + jnp.concatenate([d_diag[...], jnp.zeros((CHUNK_ROWS // T_PER_VREG - GROUP_CH, LANES), F32)], 0)
    for jt in range(n_tiles):
        rows = slice(jt * GROUP_CH, (jt + 1) * GROUP_CH)
        v_scr[:, jt * LANES:(jt + 1) * LANES] = out_f[rows]
        v_scr[:, (n_tiles + jt) * LANES:(n_tiles + jt + 1) * LANES] = out_b[rows]
    v_scr[:, 2 * CHUNK_ROWS:] = jnp.zeros((GROUP_CH, LANES), F32)

    for k in range(T_PER_VREG):
        vk = v_scr[:, k * GROUP_CH:k * GROUP_CH + 2 * CHUNK_ROWS]
        for t_out in range(CHUNK):
            off = (CHUNK - t_out) * GROUP_CH
            if off % LANES == k * GROUP_CH:
                al_off = off - k * GROUP_CH
                toep_ref[t_out * GROUP_CH:(t_out + 1) * GROUP_CH, :] = (
                    vk[:, al_off:al_off + CHUNK_ROWS].astype(BF16))


def _s5_prep(lam_re, lam_im, log_dt, b_re, b_im, c_re, c_im, d_skip):
    g = SSM_GROUPS
    d_diag = d_skip.reshape(g, GROUP_CH, 1) * jnp.eye(GROUP_CH, LANES, dtype=F32)
    lam_a = lambda a: a.reshape(2, g, STATE, 1)
    lam_b = lambda a: a.reshape(2, g, 1, STATE)
    tile_b = lambda a: jnp.tile(a, (1, 1, 1, T_PER_VREG))
    gs = S5_GROUPS_PER_STEP
    dir_spec = lambda r, c: pl.BlockSpec((2, gs, r, c), lambda i: (0, i, 0, 0))
    return pl.pallas_call(
        _s5_prep_kernel,
        out_shape=(jax.ShapeDtypeStruct((g, CHUNK_ROWS, CHUNK_ROWS), BF16),
                   jax.ShapeDtypeStruct((g, 4 * STATE, CHUNK_ROWS), BF16),
                   jax.ShapeDtypeStruct((g, CHUNK_ROWS, 4 * STATE), BF16),
                   jax.ShapeDtypeStruct((g, 4, STATE, LANES), F32)),
        grid=(g // gs,),
        in_specs=[dir_spec(STATE, 1), dir_spec(STATE, 1), dir_spec(1, STATE), dir_spec(1, STATE),
                  dir_spec(1, 1), dir_spec(STATE, LANES), dir_spec(STATE, LANES),
                  dir_spec(GROUP_CH, STATE), dir_spec(GROUP_CH, STATE),
                  pl.BlockSpec((gs, GROUP_CH, LANES), lambda i: (i, 0, 0))],
        out_specs=(pl.BlockSpec((gs, CHUNK_ROWS, CHUNK_ROWS), lambda i: (i, 0, 0)),
                   pl.BlockSpec((gs, 4 * STATE, CHUNK_ROWS), lambda i: (i, 0, 0)),
                   pl.BlockSpec((gs, CHUNK_ROWS, 4 * STATE), lambda i: (i, 0, 0)),
                   pl.BlockSpec((gs, 4, STATE, LANES), lambda i: (i, 0, 0, 0))),
        scratch_shapes=[pltpu.VMEM((gs, GROUP_CH, V_LANES), F32)],
        compiler_params=pltpu.CompilerParams(dimension_semantics=("parallel",)),
        name="s5_prep",
    )(lam_a(lam_re), lam_a(lam_im), lam_b(lam_re), lam_b(lam_im), log_dt.reshape(2, g, 1, 1),
      tile_b(b_re), tile_b(b_im), c_re, c_im, d_diag)


def _slab_perm():
    dst = np.arange(SLAB)
    perm = np.zeros((SLAB, SLAB), np.float32)
    perm[dst, (dst % SLAB_CHUNKS) * SUBLANES + dst // SLAB_CHUNKS] = 1.0
    return perm


def _s5_in_kernel(x_ref, mod_ref, g_ref, w_ref, p_ref, u_ref, z_ref, h_scr, *, chunks_per_seq):
    n_chunks = x_ref.shape[0]
    for b in range(n_chunks // chunks_per_seq):
        rows = slice(b * chunks_per_seq, (b + 1) * chunks_per_seq)
        h = _rms_modulate(x_ref[rows], g_ref[...], mod_ref[b:b + 1, D_MODEL:2 * D_MODEL],
                          mod_ref[b:b + 1, :D_MODEL])
        h2 = h.reshape(chunks_per_seq * SUBLANES, D_MODEL).astype(BF16)
        for s in range(chunks_per_seq // SLAB_CHUNKS):
            hp = jnp.dot(p_ref[...], h2[s * SLAB:(s + 1) * SLAB], preferred_element_type=F32)
            c0 = b * chunks_per_seq + s * SLAB_CHUNKS
            for k in range(SUBLANES):
                h_scr[k, c0:c0 + SLAB_CHUNKS, :] = hp[k * SLAB_CHUNKS:(k + 1) * SLAB_CHUNKS].astype(BF16)
    for k in range(SUBLANES):
        ht = h_scr[k].astype(F32).T.astype(BF16)
        uz = jnp.dot(w_ref[...], ht, preferred_element_type=F32)
        u_ref[k] = uz[:SSM_WIDTH].astype(BF16)
        z_ref[k] = uz[SSM_WIDTH:].astype(BF16)


def _s5_in(x3, mod, g, w_t, chunks_per_seq):
    n_chunks = x3.shape[0]
    once = lambda shape: pl.BlockSpec(shape, lambda t: (0,) * len(shape), pipeline_mode=pl.Buffered(1))
    out = pl.BlockSpec((SUBLANES, SSM_WIDTH, n_chunks), lambda t: (t, 0, 0))
    return pl.pallas_call(
        functools.partial(_s5_in_kernel, chunks_per_seq=chunks_per_seq),
        out_shape=(jax.ShapeDtypeStruct((CHUNK, SSM_WIDTH, n_chunks), BF16),) * 2,
        grid=(CHUNK // SUBLANES,),
        in_specs=[pl.BlockSpec((n_chunks, SUBLANES, D_MODEL), lambda t: (0, t, 0)),
                  once(mod.shape), once((1, D_MODEL)), once((2 * SSM_WIDTH, D_MODEL)),
                  once((SLAB, SLAB))],
        out_specs=(out, out),
        scratch_shapes=[pltpu.VMEM((SUBLANES, n_chunks, D_MODEL), BF16)],
        compiler_params=pltpu.CompilerParams(
            dimension_semantics=("parallel",), vmem_limit_bytes=VMEM_LIMIT_BIG),
        name="s5_in",
    )(x3, mod, g, w_t, jnp.asarray(_slab_perm(), BF16))


def _gelu_tanh(v):
    return 0.5 * v * (1.0 + jnp.tanh(math.sqrt(2.0 / math.pi) * (v + 0.044715 * (v * v * v))))


def _s5_core_kernel(u_ref, toep_ref, mb_ref, mc_ref, al_ref, o_ref, *, chunks_per_seq):
    for gi in range(u_ref.shape[1]):
        _s5_core_group(u_ref.at[:, gi], toep_ref.at[gi], mb_ref.at[gi], mc_ref.at[gi], al_ref.at[gi],
                       o_ref.at[:, gi], chunks_per_seq)


def _s5_core_group(u_ref, toep_ref, mb_ref, mc_ref, al_ref, o_ref, chunks_per_seq):
    n_chunks = u_ref.shape[-1]
    ub = u_ref[...].reshape(CHUNK_ROWS, n_chunks)
    inc = jnp.dot(mb_ref[...], ub, preferred_element_type=F32)
    pos = lax.broadcasted_iota(jnp.int32, (STATE, n_chunks), 1) % chunks_per_seq
    reps = n_chunks // LANES

    def lane_tile(a):
        return jnp.concatenate([a] * reps, axis=1) if reps > 1 else a

    states = []
    for dr in range(2):
        xr = inc[(2 * dr) * STATE:(2 * dr + 1) * STATE]
        xi = inc[(2 * dr + 1) * STATE:(2 * dr + 2) * STATE]
        ar, ai = al_ref[2 * dr], al_ref[2 * dr + 1]

        def shifted(v, step):
            if dr == 0:
                return jnp.where(pos >= step, pltpu.roll(v, step, 1), 0.0)
            return jnp.where(pos < chunks_per_seq - step, pltpu.roll(v, n_chunks - step, 1), 0.0)

        step = 1
        while step < chunks_per_seq:
            sr, si = shifted(xr, step), shifted(xi, step)
            tr, ti = lane_tile(ar), lane_tile(ai)
            xr, xi = xr + tr * sr - ti * si, xi + tr * si + ti * sr
            ar, ai = _cmul(ar, ai, ar, ai)
            step *= 2
        states += [shifted(xr, 1), shifted(xi, 1)]
    h_in = jnp.concatenate(states, axis=0).astype(BF16)
    y = (jnp.dot(toep_ref[...], ub, preferred_element_type=F32)
         + jnp.dot(mc_ref[...], h_in, preferred_element_type=F32))
    o_ref[...] = _gelu_tanh(y).astype(BF16).reshape(CHUNK, GROUP_CH, n_chunks)


def _s5_core(u_cl, toep, mb, mc, al, chunks_per_seq):
    n_chunks = u_cl.shape[-1]
    u4 = u_cl.reshape(CHUNK, SSM_GROUPS, GROUP_CH, n_chunks)
    gs = S5_GROUPS_PER_STEP
    grp = pl.BlockSpec((CHUNK, gs, GROUP_CH, n_chunks), lambda g: (0, g, 0, 0))
    out = pl.pallas_call(
        functools.partial(_s5_core_kernel, chunks_per_seq=chunks_per_seq),
        out_shape=jax.ShapeDtypeStruct(u4.shape, BF16),
        grid=(SSM_GROUPS // gs,),
        in_specs=[grp,
                  pl.BlockSpec((gs, CHUNK_ROWS, CHUNK_ROWS), lambda g: (g, 0, 0)),
                  pl.BlockSpec((gs, 4 * STATE, CHUNK_ROWS), lambda g: (g, 0, 0)),
                  pl.BlockSpec((gs, CHUNK_ROWS, 4 * STATE), lambda g: (g, 0, 0)),
                  pl.BlockSpec((gs, 4, STATE, LANES), lambda g: (g, 0, 0, 0))],
        out_specs=grp,
        compiler_params=pltpu.CompilerParams(
            dimension_semantics=("parallel",), vmem_limit_bytes=VMEM_LIMIT),
        name="s5_core",
    )(u4, toep, mb, mc, al)
    return out.reshape(CHUNK, SSM_WIDTH, n_chunks)


def _s5_out_kernel(g_ref, z_ref, wglu_ref, wout_ref, pt_ref, x_ref, mod_ref, fg_ref, o_ref, y_scr,
                   *, final, chunks_per_seq):
    q = pl.program_id(1)
    n_half = x_ref.shape[0]
    k_per = g_ref.shape[0]
    n_glu = SUBLANES // k_per

    for qv in range(n_glu):
        @pl.when(q == qv)
        def _():
            for kk in range(k_per):
                gb = g_ref[kk]
                glu = jnp.dot(wglu_ref[...], gb, preferred_element_type=F32)
                y = gb.astype(F32) * _sigmoid(glu) * _silu(z_ref[kk].astype(F32))
                y_scr[qv * k_per + kk] = y.T.astype(BF16)

    for hv in range(y_scr.shape[1] // n_half):
        @pl.when(q == n_glu + hv)
        def _():
            for s in range(n_half // SLAB_CHUNKS):
                c0 = hv * n_half + s * SLAB_CHUNKS
                src = jnp.concatenate([y_scr[k, c0:c0 + SLAB_CHUNKS, :] for k in range(SUBLANES)], axis=0)
                yp = jnp.dot(pt_ref[...], src, preferred_element_type=F32).astype(BF16)
                out = jnp.dot(yp, wout_ref[...], preferred_element_type=F32)
                b = c0 // chunks_per_seq
                upd = mod_ref[b:b + 1, 2 * D_MODEL:] * out
                rows = slice(s * SLAB_CHUNKS, (s + 1) * SLAB_CHUNKS)
                xn = x_ref[rows] + upd.reshape(SLAB_CHUNKS, SUBLANES, D_MODEL)
                if final:
                    xn = xn * lax.rsqrt(jnp.mean(xn * xn, axis=-1, keepdims=True) + NORM_EPS) * fg_ref[...]
                o_ref[rows] = xn


def _s5_out(g_cl, z_cl, wglu_t, w_out, x3, mod, final_g, final, chunks_per_seq, n_split=2):
    n_chunks = x3.shape[0]
    n_half = n_chunks // n_split
    n_glu = SUBLANES // S5_OUT_K
    once = lambda shape: pl.BlockSpec(shape, lambda t, q: (0,) * len(shape), pipeline_mode=pl.Buffered(1))
    act = pl.BlockSpec((S5_OUT_K, SSM_WIDTH, n_chunks), lambda t, q: (t * n_glu + jnp.minimum(q, n_glu - 1), 0, 0))
    row = pl.BlockSpec((n_half, SUBLANES, D_MODEL), lambda t, q: (jnp.maximum(q - n_glu, 0), t, 0))
    return pl.pallas_call(
        functools.partial(_s5_out_kernel, final=final, chunks_per_seq=chunks_per_seq),
        out_shape=jax.ShapeDtypeStruct(x3.shape, F32),
        grid=(CHUNK // SUBLANES, n_glu + n_split),
        in_specs=[act, act, once((SSM_WIDTH, SSM_WIDTH)), once((SSM_WIDTH, D_MODEL)), once((SLAB, SLAB)),
                  row, once(mod.shape), once((1, D_MODEL))],
        out_specs=row,
        scratch_shapes=[pltpu.VMEM((SUBLANES, n_chunks, SSM_WIDTH), BF16)],
        compiler_params=pltpu.CompilerParams(
            dimension_semantics=("parallel", "arbitrary"), vmem_limit_bytes=VMEM_LIMIT_BIG),
        name="s5_out",
    )(g_cl, z_cl, wglu_t, w_out, jnp.asarray(_slab_perm().T, BF16), x3, mod, final_g)


def _rope_tables(s):
    inv_freq = ROPE_THETA ** (-jnp.arange(0, HEAD_DIM, 2, dtype=F32) / HEAD_DIM)
    ang = jnp.arange(s, dtype=F32)[:, None] * inv_freq[None, :]
    reps = LANES // (HEAD_DIM // 2)
    sign = jnp.where(jnp.arange(LANES) < LANES // 2, -1.0, 1.0).astype(F32)
    cos, sin = jnp.tile(jnp.cos(ang), (1, reps)), jnp.tile(jnp.sin(ang), (1, reps)) * sign
    stream_major = lambda a: a.reshape(s // STREAMS, STREAMS, LANES).transpose(1, 0, 2)
    return cos, sin, stream_major(cos), stream_major(sin)


def _prep_attn_w_in(w):
    d_in = w.shape[0]
    w = w.reshape(d_in, -1, ATTN_HEADS // 2, 2, 2, HEAD_DIM // 2)
    w = jnp.stack([w[:, i] for i in ATTN_IN_BLOCK_ORDER], axis=1)
    return w.transpose(0, 1, 2, 4, 3, 5).astype(BF16).reshape(d_in, -1)


def _prep_attn_w_out(w):
    w = w.astype(BF16).reshape(ATTN_HEADS // 2, 2, 2, HEAD_DIM // 2, w.shape[1])
    return w.transpose(0, 2, 1, 3, 4).reshape(ATTN_WIDTH, -1)


def _trunk(x, ada, norm_g, attn_w, s5_w, s5_ops, tables, final_norm_g):
    bsz, s, _ = x.shape
    chunks_per_seq = s // CHUNK
    n_chunks = bsz * chunks_per_seq
    fg = final_norm_g.reshape(1, D_MODEL)
    for i in range(DEPTH):
        mod = ada[i].reshape(bsz, 1, 3 * D_MODEL)
        g = norm_g[i].reshape(1, D_MODEL)
        j = i // N_MIXERS
        if i % N_MIXERS == 0:
            w_in, w_out = attn_w[j]
            nat, qkv12 = _attn_in(x, mod, g, w_in, tables)
            x = _attn_out(_attention(nat, qkv12), nat, x, mod, w_out)
        else:
            w_in_t, wglu_t, w_out = s5_w[j]
            toep, mb, mc, al = s5_ops[j]
            x3 = x.reshape(n_chunks, CHUNK, D_MODEL)
            u_cl, z_cl = _s5_in(x3, ada[i], g, w_in_t, chunks_per_seq)
            g_cl = _s5_core(u_cl, toep, mb, mc, al, chunks_per_seq)
            x3 = _s5_out(g_cl, z_cl, wglu_t, w_out, x3, ada[i], fg, i == DEPTH - 1, chunks_per_seq)
            x = x3.reshape(bsz, s, D_MODEL)
    return x


def kernel(x_prompt, x_sample, c_prompt, c_sample, norm_g, ada_w, ada_b, attn_w_in, attn_w_out,
           ssm_w_in, ssm_lam_re, ssm_lam_im, ssm_log_dt, ssm_b_re, ssm_b_im, ssm_c_re, ssm_c_im,
           ssm_d, ssm_w_glu, ssm_w_out, final_norm_g):
    assert (DEPTH - 1) % N_MIXERS == 1, "the final norm is fused into the last S5 layer"
    assert all(w // (2 * d) == RADIUS for w, d in DILATED_PAIRS)
    assert [d for _, d in DILATED_PAIRS] == [1, 4, STREAMS]
    n_prompt = c_prompt.shape[0]
    ada = _ada(jnp.concatenate([c_prompt, c_sample], axis=0), ada_w, ada_b)
    attn_w = [(_prep_attn_w_in(attn_w_in[j]), _prep_attn_w_out(attn_w_out[j]))
              for j in range(attn_w_in.shape[0])]
    s5_w, s5_ops = [], []
    for j in range(ssm_w_in.shape[0]):
        s5_w.append((ssm_w_in[j].T.astype(BF16), ssm_w_glu[j].T.astype(BF16), ssm_w_out[j].astype(BF16)))
        s5_ops.append(_s5_prep(ssm_lam_re[j], ssm_lam_im[j], ssm_log_dt[j], ssm_b_re[j],
                               ssm_b_im[j], ssm_c_re[j], ssm_c_im[j], ssm_d[j]))
    tables = _rope_tables(max(x_prompt.shape[1], x_sample.shape[1]))
    y_prompt = _trunk(x_prompt, ada[:, :n_prompt], norm_g, attn_w, s5_w, s5_ops, tables, final_norm_g)
    y_sample = _trunk(x_sample, ada[:, n_prompt:], norm_g, attn_w, s5_w, s5_ops, tables, final_norm_g)
    return (y_prompt, y_sample)
```
